```python
import math
import jax, jax.numpy as jnp
from jax import lax
import numpy as np

D_MODEL = 2048
BATCH = 8
SEQ = 8192
DEPTH = 1

D_SSM = 1024
SSM_GROUP = 16
N_SSM_GROUPS = D_SSM // SSM_GROUP
SSM_STATE = 64
DT_MIN = 0.001
DT_MAX = 0.1
N_Q_HEADS = 16
N_KV_HEADS = 4
HEAD_DIM = 64
Q_PER_KV = N_Q_HEADS // N_KV_HEADS
D_ATTN = N_Q_HEADS * HEAD_DIM
D_KV = N_KV_HEADS * HEAD_DIM
WINDOW = 128
BLOCK = 128
N_BUCKETS = 32
MAX_DISTANCE = 128
N_BRANCHES = 2
D_IN = D_SSM + D_SSM + D_ATTN + D_KV + D_KV + D_ATTN + N_BRANCHES * D_MODEL
DEEPNORM_ALPHA = (2.0 * DEPTH) ** 0.25
DEEPNORM_BETA = (8.0 * DEPTH) ** -0.25
LN_EPS = 1e-5
NEG_INF = -1e30

kernel_name = "hybrid_s5_swa_sink_gated_deepnorm"


def _split_columns(proj):
    sizes = (D_SSM, D_SSM, D_ATTN, D_KV, D_KV, D_ATTN, N_BRANCHES * D_MODEL)
    points = []
    acc = 0
    for s in sizes[:-1]:
        acc += s
        points.append(acc)
    return jnp.split(proj, points, axis=-1)


def _layer_norm(x, gain, bias):
    xf = x.astype(jnp.float32)
    mu = jnp.mean(xf, axis=-1, keepdims=True)
    var = jnp.mean(jnp.square(xf - mu), axis=-1, keepdims=True)
    y = (xf - mu) * lax.rsqrt(var + LN_EPS) * gain.astype(jnp.float32) + bias.astype(jnp.float32)
    return y.astype(x.dtype)


def _t5_causal_bucket(dist):
    max_exact = N_BUCKETS // 2
    is_small = dist < max_exact
    d = jnp.maximum(dist, 1).astype(jnp.float32)
    large = max_exact + (jnp.log(d / max_exact) / math.log(MAX_DISTANCE / max_exact)
                         * (N_BUCKETS - max_exact)).astype(jnp.int32)
    large = jnp.minimum(large, N_BUCKETS - 1)
    return jnp.where(is_small, dist, large)


def _band_bias_and_mask(rel_bias_table, n_blocks):
    i = jnp.arange(BLOCK)[:, None]
    j = jnp.arange(2 * BLOCK)[None, :]
    dist = BLOCK + i - j
    band_ok = (dist >= 0) & (dist < WINDOW)
    bucket = _t5_causal_bucket(jnp.clip(dist, 0, None))
    bias = rel_bias_table.astype(jnp.float32)[bucket]
    bias = jnp.transpose(bias, (2, 0, 1)).reshape(N_KV_HEADS, Q_PER_KV, BLOCK, 2 * BLOCK)
    n = jnp.arange(n_blocks)[:, None, None]
    key_abs = n * BLOCK - BLOCK + j[None]
    mask = band_ok[None] & (key_abs >= 0)
    return bias, mask[None, :, None, None]


def _sliding_window_gqa(q, k, v, sinks, rel_bias_table):
    b, s, _ = q.shape
    nb = s // BLOCK
    q = q.reshape(b, nb, BLOCK, N_KV_HEADS, Q_PER_KV, HEAD_DIM)

    def band(t):
        t = t.reshape(b, s, N_KV_HEADS, HEAD_DIM)
        t = jnp.pad(t, ((0, 0), (BLOCK, 0), (0, 0), (0, 0))).reshape(b, nb + 1, BLOCK, N_KV_HEADS, HEAD_DIM)
        return jnp.concatenate([t[:, :-1], t[:, 1:]], axis=2)

    kb, vb = band(k), band(v)
    bias, mask = _band_bias_and_mask(rel_bias_table, nb)
    logits = jnp.einsum("bnqkgd,bnskd->bnkgqs", q, kb).astype(jnp.float32) * (HEAD_DIM ** -0.5)
    logits = jnp.where(mask, logits + bias, NEG_INF)
    sink = sinks.astype(jnp.float32).reshape(N_KV_HEADS, Q_PER_KV)[None, None, :, :, None, None]
    m = jnp.maximum(jnp.max(logits, axis=-1, keepdims=True), sink)
    p = jnp.exp(logits - m)
    p = p / (jnp.sum(p, axis=-1, keepdims=True) + jnp.exp(sink - m))
    out = jnp.einsum("bnkgqs,bnskd->bnqkgd", p.astype(vb.dtype), vb)
    return out.reshape(b, s, D_ATTN)


def _s5_scan_op(e1, e2):
    a1, b1 = e1
    a2, b2 = e2
    return a1 * a2, a2 * b1 + b2


def _s5_ssm(u, lam_re, lam_im, b_re, b_im, c_re, c_im, d_skip, log_step):
    b, s, _ = u.shape
    f32 = jnp.float32
    step = jnp.exp(log_step.astype(f32))[:, None]
    lam = lax.complex(lam_re.astype(f32), lam_im.astype(f32))
    lam_bar = jnp.exp(lam * step)
    b_cplx = lax.complex(b_re.astype(f32), b_im.astype(f32))
    b_bar = ((lam_bar - 1.0) / lam)[..., None] * b_cplx
    ug = u.astype(f32).reshape(b, s, N_SSM_GROUPS, SSM_GROUP)
    bu = lax.complex(jnp.einsum("bsgh,gph->sbgp", ug, jnp.real(b_bar)),
                     jnp.einsum("bsgh,gph->sbgp", ug, jnp.imag(b_bar)))
    a = jnp.broadcast_to(lam_bar[None, None], (s, 1, N_SSM_GROUPS, SSM_STATE))
    _, states = lax.associative_scan(_s5_scan_op, (a, bu), axis=0)
    y = (jnp.einsum("sbgp,ghp->bsgh", jnp.real(states), c_re.astype(f32))
         - jnp.einsum("sbgp,ghp->bsgh", jnp.imag(states), c_im.astype(f32)))
    y = y + d_skip.astype(f32).reshape(N_SSM_GROUPS, SSM_GROUP) * ug
    return y.reshape(b, s, D_SSM)


def _fwd_setup_inputs(seed: int = 0) -> dict:
    key = jax.random.key(seed)
    ks = jax.random.split(key, 20)
    f32 = jnp.float32
    x = jax.random.normal(ks[0], (BATCH, SEQ, D_MODEL), f32)
    w_in = jax.random.normal(ks[1], (DEPTH, D_MODEL, D_IN), f32) * D_MODEL ** -0.5
    n_idx = jnp.arange(SSM_STATE, dtype=f32)
    ssm_lambda_re = -0.5 + 0.01 * jax.random.normal(ks[2], (DEPTH, N_SSM_GROUPS, SSM_STATE), f32)
    ssm_lambda_im = math.pi * n_idx + 0.01 * jax.random.normal(ks[3], (DEPTH, N_SSM_GROUPS, SSM_STATE), f32)
    ssm_b_re = jax.random.normal(ks[4], (DEPTH, N_SSM_GROUPS, SSM_STATE, SSM_GROUP), f32) * (2.0 * SSM_GROUP) ** -0.5
    ssm_b_im = jax.random.normal(ks[5], (DEPTH, N_SSM_GROUPS, SSM_STATE, SSM_GROUP), f32) * (2.0 * SSM_GROUP) ** -0.5
    ssm_c_re = jax.random.normal(ks[6], (DEPTH, N_SSM_GROUPS, SSM_GROUP, SSM_STATE), f32) * SSM_STATE ** -0.5
    ssm_c_im = jax.random.normal(ks[7], (DEPTH, N_SSM_GROUPS, SSM_GROUP, SSM_STATE), f32) * SSM_STATE ** -0.5
    ssm_d = jax.random.normal(ks[8], (DEPTH, D_SSM), f32)
    ssm_log_step = jax.random.uniform(ks[9], (DEPTH, N_SSM_GROUPS), f32,
                                      minval=math.log(DT_MIN), maxval=math.log(DT_MAX))
    w_glu = jax.random.normal(ks[10], (DEPTH, D_SSM, 2 * D_SSM), f32) * D_SSM ** -0.5
    attn_sinks = jax.random.normal(ks[11], (DEPTH, N_Q_HEADS), f32)
    rel_bias_table = 0.5 * jax.random.normal(ks[12], (N_BUCKETS, N_Q_HEADS), f32)
    w_branch_ssm = jax.random.normal(ks[13], (DEPTH, D_SSM, D_MODEL), f32) * D_SSM ** -0.5 * DEEPNORM_BETA
    w_branch_attn = jax.random.normal(ks[14], (DEPTH, D_ATTN, D_MODEL), f32) * D_ATTN ** -0.5 * DEEPNORM_BETA
    w_out = jax.random.normal(ks[15], (DEPTH, D_MODEL, D_MODEL), f32) * D_MODEL ** -0.5 * DEEPNORM_BETA
    ln_gain = 1.0 + 0.02 * jax.random.normal(ks[16], (DEPTH, D_MODEL), f32)
    ln_bias = 0.02 * jax.random.normal(ks[17], (DEPTH, D_MODEL), f32)
    return {"x": x, "w_in": w_in, "ssm_lambda_re": ssm_lambda_re, "ssm_lambda_im": ssm_lambda_im,
            "ssm_b_re": ssm_b_re, "ssm_b_im": ssm_b_im, "ssm_c_re": ssm_c_re, "ssm_c_im": ssm_c_im,
            "ssm_d": ssm_d, "ssm_log_step": ssm_log_step, "w_glu": w_glu, "attn_sinks": attn_sinks,
            "rel_bias_table": rel_bias_table, "w_branch_ssm": w_branch_ssm, "w_branch_attn": w_branch_attn,
            "w_out": w_out, "ln_gain": ln_gain, "ln_bias": ln_bias}


def _fwd_reference(x, w_in, ssm_lambda_re, ssm_lambda_im, ssm_b_re, ssm_b_im, ssm_c_re, ssm_c_im,
              ssm_d, ssm_log_step, w_glu, attn_sinks, rel_bias_table, w_branch_ssm, w_branch_attn,
              w_out, ln_gain, ln_bias):
    for layer in range(DEPTH):
        proj = jnp.einsum("bsd,de->bse", x, w_in[layer])
        u_ssm, z_ssm, q, k, v, z_attn, gate_logits = _split_columns(proj)

        y_ssm = _s5_ssm(u_ssm, ssm_lambda_re[layer], ssm_lambda_im[layer], ssm_b_re[layer], ssm_b_im[layer],
                        ssm_c_re[layer], ssm_c_im[layer], ssm_d[layer], ssm_log_step[layer])
        glu_in = jax.nn.gelu(y_ssm, approximate=False)
        glu_a, glu_b = jnp.split(jnp.einsum("bsc,ce->bse", glu_in, w_glu[layer].astype(jnp.float32)), 2, axis=-1)
        h_ssm = (glu_a * jax.nn.sigmoid(glu_b)).astype(x.dtype) * jax.nn.silu(z_ssm)

        h_attn = _sliding_window_gqa(q, k, v, attn_sinks[layer], rel_bias_table) * jax.nn.silu(z_attn)

        gates = jax.nn.sigmoid(gate_logits.astype(jnp.float32)).astype(x.dtype)
        gate_ssm, gate_attn = jnp.split(gates, 2, axis=-1)
        merged = (gate_ssm * jnp.einsum("bsc,cd->bsd", h_ssm, w_branch_ssm[layer])
                  + gate_attn * jnp.einsum("bsc,cd->bsd", h_attn, w_branch_attn[layer]))
        out = jnp.einsum("bsd,de->bse", merged, w_out[layer])

        x = _layer_norm(DEEPNORM_ALPHA * x + out.astype(x.dtype), ln_gain[layer], ln_bias[layer])
    return x


import jax as _jax
import jax.numpy as _jnp

TWIN_FORMAT = 'train_step'
FWD_PARAMS = ['x', 'w_in', 'ssm_lambda_re', 'ssm_lambda_im', 'ssm_b_re', 'ssm_b_im', 'ssm_c_re', 'ssm_c_im', 'ssm_d', 'ssm_log_step', 'w_glu', 'attn_sinks', 'rel_bias_table', 'w_branch_ssm', 'w_branch_attn', 'w_out', 'ln_gain', 'ln_bias']
TWIN_WEIGHTS = ['w_in', 'ssm_lambda_re', 'ssm_lambda_im', 'ssm_b_re', 'ssm_b_im', 'ssm_c_re', 'ssm_c_im', 'ssm_d', 'ssm_log_step', 'w_glu', 'attn_sinks', 'rel_bias_table', 'w_branch_ssm', 'w_branch_attn', 'w_out', 'ln_gain', 'ln_bias']
TWIN_DIFF_INPUT = 'x'
TWIN_INPUTS = ['x', 'w_in', 'ssm_lambda_re', 'ssm_lambda_im', 'ssm_b_re', 'ssm_b_im', 'ssm_c_re', 'ssm_c_im', 'ssm_d', 'ssm_log_step', 'w_glu', 'attn_sinks', 'rel_bias_table', 'w_branch_ssm', 'w_branch_attn', 'w_out', 'ln_gain', 'ln_bias', 'loss_target', 'm_w_in', 'm_ssm_lambda_re', 'm_ssm_lambda_im', 'm_ssm_b_re', 'm_ssm_b_im', 'm_ssm_c_re', 'm_ssm_c_im', 'm_ssm_d', 'm_ssm_log_step', 'm_w_glu', 'm_attn_sinks', 'm_rel_bias_table', 'm_w_branch_ssm', 'm_w_branch_attn', 'm_w_out', 'm_ln_gain', 'm_ln_bias', 'v_w_in', 'v_ssm_lambda_re', 'v_ssm_lambda_im', 'v_ssm_b_re', 'v_ssm_b_im', 'v_ssm_c_re', 'v_ssm_c_im', 'v_ssm_d', 'v_ssm_log_step', 'v_w_glu', 'v_attn_sinks', 'v_rel_bias_table', 'v_w_branch_ssm', 'v_w_branch_attn', 'v_w_out', 'v_ln_gain', 'v_ln_bias']
TWIN_OUTPUTS = ['loss', 'grad_x', 'grad_w_in', 'grad_ssm_lambda_re', 'grad_ssm_lambda_im', 'grad_ssm_b_re', 'grad_ssm_b_im', 'grad_ssm_c_re', 'grad_ssm_c_im', 'grad_ssm_d', 'grad_ssm_log_step', 'grad_w_glu', 'grad_attn_sinks', 'grad_rel_bias_table', 'grad_w_branch_ssm', 'grad_w_branch_attn', 'grad_w_out', 'grad_ln_gain', 'grad_ln_bias', 'delta_w_in', 'delta_ssm_lambda_re', 'delta_ssm_lambda_im', 'delta_ssm_b_re', 'delta_ssm_b_im', 'delta_ssm_c_re', 'delta_ssm_c_im', 'delta_ssm_d', 'delta_ssm_log_step', 'delta_w_glu', 'delta_attn_sinks', 'delta_rel_bias_table', 'delta_w_branch_ssm', 'delta_w_branch_attn', 'delta_w_out', 'delta_ln_gain', 'delta_ln_bias', 'new_m_w_in', 'new_m_ssm_lambda_re', 'new_m_ssm_lambda_im', 'new_m_ssm_b_re', 'new_m_ssm_b_im', 'new_m_ssm_c_re', 'new_m_ssm_c_im', 'new_m_ssm_d', 'new_m_ssm_log_step', 'new_m_w_glu', 'new_m_attn_sinks', 'new_m_rel_bias_table', 'new_m_w_branch_ssm', 'new_m_w_branch_attn', 'new_m_w_out', 'new_m_ln_gain', 'new_m_ln_bias', 'new_v_w_in', 'new_v_ssm_lambda_re', 'new_v_ssm_lambda_im', 'new_v_ssm_b_re', 'new_v_ssm_b_im', 'new_v_ssm_c_re', 'new_v_ssm_c_im', 'new_v_ssm_d', 'new_v_ssm_log_step', 'new_v_w_glu', 'new_v_attn_sinks', 'new_v_rel_bias_table', 'new_v_w_branch_ssm', 'new_v_w_branch_attn', 'new_v_w_out', 'new_v_ln_gain', 'new_v_ln_bias']
TWIN_LEAF_KINDS = {'loss': 'loss', 'grad_x': 'grad_x', 'grad_w_in': 'grad_w', 'grad_ssm_lambda_re': 'grad_w', 'grad_ssm_lambda_im': 'grad_w', 'grad_ssm_b_re': 'grad_w', 'grad_ssm_b_im': 'grad_w', 'grad_ssm_c_re': 'grad_w', 'grad_ssm_c_im': 'grad_w', 'grad_ssm_d': 'grad_w', 'grad_ssm_log_step': 'grad_w', 'grad_w_glu': 'grad_w', 'grad_attn_sinks': 'grad_w', 'grad_rel_bias_table': 'grad_w', 'grad_w_branch_ssm': 'grad_w', 'grad_w_branch_attn': 'grad_w', 'grad_w_out': 'grad_w', 'grad_ln_gain': 'grad_w', 'grad_ln_bias': 'grad_w', 'delta_w_in': 'delta_w', 'delta_ssm_lambda_re': 'delta_w', 'delta_ssm_lambda_im': 'delta_w', 'delta_ssm_b_re': 'delta_w', 'delta_ssm_b_im': 'delta_w', 'delta_ssm_c_re': 'delta_w', 'delta_ssm_c_im': 'delta_w', 'delta_ssm_d': 'delta_w', 'delta_ssm_log_step': 'delta_w', 'delta_w_glu': 'delta_w', 'delta_attn_sinks': 'delta_w', 'delta_rel_bias_table': 'delta_w', 'delta_w_branch_ssm': 'delta_w', 'delta_w_branch_attn': 'delta_w', 'delta_w_out': 'delta_w', 'delta_ln_gain': 'delta_w', 'delta_ln_bias': 'delta_w', 'new_m_w_in': 'new_m', 'new_m_ssm_lambda_re': 'new_m', 'new_m_ssm_lambda_im': 'new_m', 'new_m_ssm_b_re': 'new_m', 'new_m_ssm_b_im': 'new_m', 'new_m_ssm_c_re': 'new_m', 'new_m_ssm_c_im': 'new_m', 'new_m_ssm_d': 'new_m', 'new_m_ssm_log_step': 'new_m', 'new_m_w_glu': 'new_m', 'new_m_attn_sinks': 'new_m', 'new_m_rel_bias_table': 'new_m', 'new_m_w_branch_ssm': 'new_m', 'new_m_w_branch_attn': 'new_m', 'new_m_w_out': 'new_m', 'new_m_ln_gain': 'new_m', 'new_m_ln_bias': 'new_m', 'new_v_w_in': 'new_v', 'new_v_ssm_lambda_re': 'new_v', 'new_v_ssm_lambda_im': 'new_v', 'new_v_ssm_b_re': 'new_v', 'new_v_ssm_b_im': 'new_v', 'new_v_ssm_c_re': 'new_v', 'new_v_ssm_c_im': 'new_v', 'new_v_ssm_d': 'new_v', 'new_v_ssm_log_step': 'new_v', 'new_v_w_glu': 'new_v', 'new_v_attn_sinks': 'new_v', 'new_v_rel_bias_table': 'new_v', 'new_v_w_branch_ssm': 'new_v', 'new_v_w_branch_attn': 'new_v', 'new_v_w_out': 'new_v', 'new_v_ln_gain': 'new_v', 'new_v_ln_bias': 'new_v'}


def _forward(args):
    return _fwd_reference(*[args[k] for k in FWD_PARAMS])


def _output_shape():
    def fwd():
        inp = _fwd_setup_inputs(0)
        return _fwd_reference(*[inp[k] for k in FWD_PARAMS])
    out = _jax.eval_shape(fwd)
    return out.shape, out.dtype

N_MICROBATCH = 1
ADAM_LR = 0.001
ADAM_B1 = 0.9
ADAM_B2 = 0.999
ADAM_EPS = 1e-08
ADAM_WD = 0.01
ADAM_STEP = 10
PER_EXAMPLE_BATCH_AXIS = {'x': 0, 'loss_target': 0}
SHARED_INPUTS = []
_WEIGHT_DTYPES = {'w_in': _jnp.float32, 'ssm_lambda_re': _jnp.float32, 'ssm_lambda_im': _jnp.float32, 'ssm_b_re': _jnp.float32, 'ssm_b_im': _jnp.float32, 'ssm_c_re': _jnp.float32, 'ssm_c_im': _jnp.float32, 'ssm_d': _jnp.float32, 'ssm_log_step': _jnp.float32, 'w_glu': _jnp.float32, 'attn_sinks': _jnp.float32, 'rel_bias_table': _jnp.float32, 'w_branch_ssm': _jnp.float32, 'w_branch_attn': _jnp.float32, 'w_out': _jnp.float32, 'ln_gain': _jnp.float32, 'ln_bias': _jnp.float32}
MOMENT_SCALE = {'w_in': 3.922544e-03, 'ssm_lambda_re': 5.451980e-04, 'ssm_lambda_im': 4.369740e-04, 'ssm_b_re': 2.941071e-04, 'ssm_b_im': 2.991662e-04, 'ssm_c_re': 4.115270e-04, 'ssm_c_im': 4.198642e-04, 'ssm_d': 6.741252e-03, 'ssm_log_step': 5.835063e-01, 'w_glu': 4.586775e-03, 'attn_sinks': 3.625891e-03, 'rel_bias_table': 4.438700e-03, 'w_branch_ssm': 7.381688e-03, 'w_branch_attn': 4.410537e-03, 'w_out': 8.235580e-03, 'ln_gain': 3.196861e+01, 'ln_bias': 4.665965e-01}


def _to_microbatches(a, axis):
    t = _jnp.moveaxis(a, axis, 0)
    t = t.reshape((N_MICROBATCH, t.shape[0] // N_MICROBATCH) + t.shape[1:])
    return _jnp.moveaxis(t, 1, axis + 1)


def setup_inputs(seed: int = 0) -> dict:
    inp = _fwd_setup_inputs(seed)
    key = _jax.random.fold_in(_jax.random.key(seed), 7919)
    shape, _ = _output_shape()
    out = dict(inp)
    out["loss_target"] = _jax.random.normal(_jax.random.fold_in(key, 0), shape, _jnp.float32)
    for i, name in enumerate(TWIN_WEIGHTS):
        w = inp[name].astype(_jnp.float32)
        if MOMENT_SCALE is None:
            s = _jnp.sqrt(_jnp.mean(_jnp.square(w)) + 1e-30)
        else:
            s = MOMENT_SCALE[name]
        km, kv = _jax.random.split(_jax.random.fold_in(key, i + 1))
        out[name] = w
        out["m_" + name] = s * _jax.random.normal(km, w.shape, _jnp.float32)
        out["v_" + name] = (s * s) * _jax.random.uniform(kv, w.shape, _jnp.float32, 0.5, 1.5)
    if N_MICROBATCH > 1:
        for name, axis in PER_EXAMPLE_BATCH_AXIS.items():
            out[name] = _to_microbatches(out[name], axis)
    return {'x': out['x'], 'w_in': out['w_in'], 'ssm_lambda_re': out['ssm_lambda_re'], 'ssm_lambda_im': out['ssm_lambda_im'], 'ssm_b_re': out['ssm_b_re'], 'ssm_b_im': out['ssm_b_im'], 'ssm_c_re': out['ssm_c_re'], 'ssm_c_im': out['ssm_c_im'], 'ssm_d': out['ssm_d'], 'ssm_log_step': out['ssm_log_step'], 'w_glu': out['w_glu'], 'attn_sinks': out['attn_sinks'], 'rel_bias_table': out['rel_bias_table'], 'w_branch_ssm': out['w_branch_ssm'], 'w_branch_attn': out['w_branch_attn'], 'w_out': out['w_out'], 'ln_gain': out['ln_gain'], 'ln_bias': out['ln_bias'], 'loss_target': out['loss_target'], 'm_w_in': out['m_w_in'], 'm_ssm_lambda_re': out['m_ssm_lambda_re'], 'm_ssm_lambda_im': out['m_ssm_lambda_im'], 'm_ssm_b_re': out['m_ssm_b_re'], 'm_ssm_b_im': out['m_ssm_b_im'], 'm_ssm_c_re': out['m_ssm_c_re'], 'm_ssm_c_im': out['m_ssm_c_im'], 'm_ssm_d': out['m_ssm_d'], 'm_ssm_log_step': out['m_ssm_log_step'], 'm_w_glu': out['m_w_glu'], 'm_attn_sinks': out['m_attn_sinks'], 'm_rel_bias_table': out['m_rel_bias_table'], 'm_w_branch_ssm': out['m_w_branch_ssm'], 'm_w_branch_attn': out['m_w_branch_attn'], 'm_w_out': out['m_w_out'], 'm_ln_gain': out['m_ln_gain'], 'm_ln_bias': out['m_ln_bias'], 'v_w_in': out['v_w_in'], 'v_ssm_lambda_re': out['v_ssm_lambda_re'], 'v_ssm_lambda_im': out['v_ssm_lambda_im'], 'v_ssm_b_re': out['v_ssm_b_re'], 'v_ssm_b_im': out['v_ssm_b_im'], 'v_ssm_c_re': out['v_ssm_c_re'], 'v_ssm_c_im': out['v_ssm_c_im'], 'v_ssm_d': out['v_ssm_d'], 'v_ssm_log_step': out['v_ssm_log_step'], 'v_w_glu': out['v_w_glu'], 'v_attn_sinks': out['v_attn_sinks'], 'v_rel_bias_table': out['v_rel_bias_table'], 'v_w_branch_ssm': out['v_w_branch_ssm'], 'v_w_branch_attn': out['v_w_branch_attn'], 'v_w_out': out['v_w_out'], 'v_ln_gain': out['v_ln_gain'], 'v_ln_bias': out['v_ln_bias']}


def _loss(weights, diff, rest, loss_target):
    with _jax.named_scope("forward"):
        args = {**rest, TWIN_DIFF_INPUT: diff, **{k: w.astype(_WEIGHT_DTYPES[k]) for k, w in weights.items()}}
        y = _forward(args)
    with _jax.named_scope("loss_head"):
        err = _jnp.square(y.astype(_jnp.float32) - loss_target)
        return 0.5 * _jnp.sum(_jnp.mean(err, axis=-1)) if err.ndim else 0.5 * err


def _adamw(w, g, m, v):
    m = ADAM_B1 * m + (1.0 - ADAM_B1) * g
    v = ADAM_B2 * v + (1.0 - ADAM_B2) * _jnp.square(g)
    m_hat = m / (1.0 - ADAM_B1 ** ADAM_STEP)
    v_hat = v / (1.0 - ADAM_B2 ** ADAM_STEP)
    delta = -ADAM_LR * (m_hat / (_jnp.sqrt(v_hat) + ADAM_EPS) + ADAM_WD * w)
    return delta, m, v


def reference(x, w_in, ssm_lambda_re, ssm_lambda_im, ssm_b_re, ssm_b_im, ssm_c_re, ssm_c_im, ssm_d, ssm_log_step, w_glu, attn_sinks, rel_bias_table, w_branch_ssm, w_branch_attn, w_out, ln_gain, ln_bias, loss_target, m_w_in, m_ssm_lambda_re, m_ssm_lambda_im, m_ssm_b_re, m_ssm_b_im, m_ssm_c_re, m_ssm_c_im, m_ssm_d, m_ssm_log_step, m_w_glu, m_attn_sinks, m_rel_bias_table, m_w_branch_ssm, m_w_branch_attn, m_w_out, m_ln_gain, m_ln_bias, v_w_in, v_ssm_lambda_re, v_ssm_lambda_im, v_ssm_b_re, v_ssm_b_im, v_ssm_c_re, v_ssm_c_im, v_ssm_d, v_ssm_log_step, v_w_glu, v_attn_sinks, v_rel_bias_table, v_w_branch_ssm, v_w_branch_attn, v_w_out, v_ln_gain, v_ln_bias):
    given = dict(x=x, w_in=w_in, ssm_lambda_re=ssm_lambda_re, ssm_lambda_im=ssm_lambda_im, ssm_b_re=ssm_b_re, ssm_b_im=ssm_b_im, ssm_c_re=ssm_c_re, ssm_c_im=ssm_c_im, ssm_d=ssm_d, ssm_log_step=ssm_log_step, w_glu=w_glu, attn_sinks=attn_sinks, rel_bias_table=rel_bias_table, w_branch_ssm=w_branch_ssm, w_branch_attn=w_branch_attn, w_out=w_out, ln_gain=ln_gain, ln_bias=ln_bias, loss_target=loss_target, m_w_in=m_w_in, m_ssm_lambda_re=m_ssm_lambda_re, m_ssm_lambda_im=m_ssm_lambda_im, m_ssm_b_re=m_ssm_b_re, m_ssm_b_im=m_ssm_b_im, m_ssm_c_re=m_ssm_c_re, m_ssm_c_im=m_ssm_c_im, m_ssm_d=m_ssm_d, m_ssm_log_step=m_ssm_log_step, m_w_glu=m_w_glu, m_attn_sinks=m_attn_sinks, m_rel_bias_table=m_rel_bias_table, m_w_branch_ssm=m_w_branch_ssm, m_w_branch_attn=m_w_branch_attn, m_w_out=m_w_out, m_ln_gain=m_ln_gain, m_ln_bias=m_ln_bias, v_w_in=v_w_in, v_ssm_lambda_re=v_ssm_lambda_re, v_ssm_lambda_im=v_ssm_lambda_im, v_ssm_b_re=v_ssm_b_re, v_ssm_b_im=v_ssm_b_im, v_ssm_c_re=v_ssm_c_re, v_ssm_c_im=v_ssm_c_im, v_ssm_d=v_ssm_d, v_ssm_log_step=v_ssm_log_step, v_w_glu=v_w_glu, v_attn_sinks=v_attn_sinks, v_rel_bias_table=v_rel_bias_table, v_w_branch_ssm=v_w_branch_ssm, v_w_branch_attn=v_w_branch_attn, v_w_out=v_w_out, v_ln_gain=v_ln_gain, v_ln_bias=v_ln_bias)
    weights = {n: given[n] for n in TWIN_WEIGHTS}
    shared = {n: given[n] for n in SHARED_INPUTS}
    per_example = {n: given[n] for n in ['x']}
    grad_fn = _jax.value_and_grad(_loss, argnums=(0, 1))

    def one_microbatch(ex, loss_target):
        ex = dict(ex)
        diff = ex.pop(TWIN_DIFF_INPUT)
        return grad_fn(weights, diff, {**shared, **ex}, loss_target)

    if N_MICROBATCH == 1:
        loss, (grad_w, grad_x) = one_microbatch(per_example, given["loss_target"])
    else:
        def body(carry, xs):
            loss_sum, grad_sum = carry
            l_k, (gw_k, gx_k) = one_microbatch(xs[0], xs[1])
            with _jax.named_scope("update"):
                return (loss_sum + l_k, _jax.tree.map(_jnp.add, grad_sum, gw_k)), gx_k

        init = (_jnp.zeros((), _jnp.float32), _jax.tree.map(_jnp.zeros_like, weights))
        (loss, grad_w), grad_x = _jax.lax.scan(body, init, (per_example, given["loss_target"]))
    with _jax.named_scope("update"):
        delta_w, new_m, new_v = {}, {}, {}
        for n in TWIN_WEIGHTS:
            delta_w[n], new_m[n], new_v[n] = _adamw(weights[n], grad_w[n], given["m_" + n], given["v_" + n])
    return (loss, grad_x, *[grad_w[n] for n in TWIN_WEIGHTS], *[delta_w[n] for n in TWIN_WEIGHTS],
            *[new_m[n] for n in TWIN_WEIGHTS], *[new_v[n] for n in TWIN_WEIGHTS])
```

```python
import functools
import math

import jax
import jax.numpy as jnp
from jax import lax
from jax.experimental import pallas as pl
from jax.experimental.pallas import tpu as pltpu

F32 = jnp.float32
MXU_DTYPE = jnp.bfloat16

D_MODEL = 2048
D_SSM = 1024
SSM_GROUP = 16
N_GROUPS = 64
SSM_STATE = 64
N_Q_HEADS = 16
N_KV_HEADS = 4
Q_PER_KV = 4
HEAD_DIM = 64
D_ATTN = 1024
D_KV = 256
WINDOW = 128
BLOCK = 128
N_BUCKETS = 32
MAX_DISTANCE = 128
D_IN = 8704
DEEPNORM_ALPHA = 2.0 ** 0.25
LN_EPS = 1e-5
NEG_INF = -1e30
ATTN_SCALE = HEAD_DIM ** -0.5

ADAM_LR = 0.001
ADAM_B1 = 0.9
ADAM_B2 = 0.999
ADAM_EPS = 1e-08
ADAM_WD = 0.01
ADAM_STEP = 10

N_DEV = 8
SSM_CHUNK = 16
CHUNK_W = SSM_CHUNK * SSM_GROUP
COL = 256
OFF_U, OFF_ZS, OFF_Q, OFF_K, OFF_V, OFF_ZA, OFF_GS, OFF_GA = 0, 4, 8, 12, 13, 14, 18, 26

VMEM_CAP = 56 * 1024 * 1024
MESH_AXES = ("x", "y", "c")


def _vmem_limit(block_bytes):
    return int(min(max(3 * block_bytes, 16 * 1024 * 1024), VMEM_CAP))


def _nbytes(shape, dtype):
    return math.prod(shape) * jnp.dtype(dtype).itemsize


def _tile(n, pref):
    if n <= pref:
        return n
    t = (pref // 128) * 128
    while t >= 128:
        if n % t == 0:
            return t
        t -= 128
    return n


def _mm(a, b, *, name, ta=False, tb=False, out_dtype=F32, tm=1024, tn=512, tk=512, add=None, add_scale=1.0):
    squeeze = a.ndim == 2
    if squeeze:
        a, b = a[None], b[None]
        if add is not None:
            add = add[None]
    nb = a.shape[0]
    m, k = (a.shape[2], a.shape[1]) if ta else (a.shape[1], a.shape[2])
    n = b.shape[1] if tb else b.shape[2]
    tm, tn, tk = _tile(m, tm), _tile(n, tn), _tile(k, tk)
    nk = k // tk
    dn = (((0 if ta else 1,), (1 if tb else 0,)), ((), ()))

    a_spec = (pl.BlockSpec((None, tk, tm), lambda g, i, j, kk: (g, kk, i)) if ta
              else pl.BlockSpec((None, tm, tk), lambda g, i, j, kk: (g, i, kk)))
    b_spec = (pl.BlockSpec((None, tn, tk), lambda g, i, j, kk: (g, j, kk)) if tb
              else pl.BlockSpec((None, tk, tn), lambda g, i, j, kk: (g, kk, j)))
    o_spec = pl.BlockSpec((None, tm, tn), lambda g, i, j, kk: (g, i, j))
    in_specs = [a_spec, b_spec]
    operands = [a, b]
    if add is not None:
        in_specs.append(o_spec)
        operands.append(add)

    def body(*refs):
        a_ref, b_ref = refs[0], refs[1]
        add_ref = refs[2] if add is not None else None
        o_ref = refs[3] if add is not None else refs[2]
        acc_ref = refs[-1]
        kk = pl.program_id(3)
        part = lax.dot_general(a_ref[...].astype(MXU_DTYPE), b_ref[...].astype(MXU_DTYPE), dn,
                               preferred_element_type=F32)

        @pl.when(kk == 0)
        def _():
            acc_ref[...] = part

        @pl.when(kk > 0)
        def _():
            acc_ref[...] += part

        @pl.when(kk == nk - 1)
        def _():
            r = acc_ref[...]
            if add_ref is not None:
                r = r + add_scale * add_ref[...]
            o_ref[...] = r.astype(out_dtype)

    blocks = (_nbytes((tm, tk), a.dtype) + _nbytes((tk, tn), b.dtype) + _nbytes((tm, tn), out_dtype)
              + (_nbytes((tm, tn), F32) if add is not None else 0))
    out = pl.pallas_call(
        body,
        name=name,
        grid=(nb, m // tm, n // tn, nk),
        in_specs=in_specs,
        out_specs=o_spec,
        out_shape=jax.ShapeDtypeStruct((nb, m, n), out_dtype),
        scratch_shapes=[pltpu.VMEM((tm, tn), F32)],
        compiler_params=pltpu.CompilerParams(
            dimension_semantics=("parallel", "parallel", "parallel", "arbitrary"),
            vmem_limit_bytes=_vmem_limit(2 * blocks + 2 * _nbytes((tm, tn), F32))),
    )(*operands)
    return out[0] if squeeze else out


def _ew(fn, ins, outs, *, rows, cw, ncb, tr, name):
    tr = min(tr, rows)
    n_in = len(ins)

    def row_map(off):
        return lambda j, i: (i, off + j)

    def vec_map(off):
        return lambda j, i: (0, off + j)

    in_specs = []
    for arr, kind, off in ins:
        if kind == "row":
            in_specs.append(pl.BlockSpec((tr, cw), row_map(off)))
        else:
            in_specs.append(pl.BlockSpec((1, cw), vec_map(off)))
    out_specs, out_shapes = [], []
    for bw, dt, kind in outs:
        if kind == "row":
            out_specs.append(pl.BlockSpec((tr, bw), row_map(0)))
            out_shapes.append(jax.ShapeDtypeStruct((rows, ncb * bw), dt))
        else:
            out_specs.append(pl.BlockSpec((1, bw), vec_map(0)))
            out_shapes.append(jax.ShapeDtypeStruct((1, ncb * bw), F32))

    def body(*refs):
        i = pl.program_id(1)
        vals = fn(*[r[...] for r in refs[:n_in]])
        for r, (bw, dt, kind), v in zip(refs[n_in:], outs, vals):
            if kind == "row":
                r[...] = v.astype(dt)
            else:
                @pl.when(i == 0)
                def _(r=r):
                    r[...] = jnp.zeros_like(r)

                r[...] += v

    blocks = sum(_nbytes((tr, cw), a.dtype) for a, kind, _ in ins if kind == "row")
    blocks += sum(_nbytes((tr, bw), dt) for bw, dt, kind in outs if kind == "row")
    res = pl.pallas_call(
        body,
        name=name,
        grid=(ncb, rows // tr),
        in_specs=in_specs,
        out_specs=out_specs,
        out_shape=out_shapes,
        compiler_params=pltpu.CompilerParams(
            dimension_semantics=("parallel", "arbitrary"),
            vmem_limit_bytes=_vmem_limit(4 * blocks)),
    )(*[a for a, _, _ in ins])
    return res


def _sigmoid(x):
    return 1.0 / (1.0 + jnp.exp(-x))


def _erf(x):
    ax = jnp.abs(x)
    t = 1.0 / (1.0 + 0.3275911 * ax)
    poly = t * (0.254829592 + t * (-0.284496736 + t * (1.421413741 + t * (-1.453152027 + t * 1.061405429))))
    r = 1.0 - poly * jnp.exp(-ax * ax)
    return jnp.where(x < 0, -r, r)


INV_SQRT2 = 0.7071067811865476
INV_SQRT_2PI = 0.3989422804014327


def _gelu(x):
    return 0.5 * x * (1.0 + lax.erf(x * INV_SQRT2))


def _gelu_grad(x):
    return 0.5 * (1.0 + lax.erf(x * INV_SQRT2)) + x * INV_SQRT_2PI * jnp.exp(-0.5 * x * x)


def _silu(x):
    return x * _sigmoid(x)


def _silu_grad(x):
    s = _sigmoid(x)
    return s * (1.0 + x * (1.0 - s))


def _ssm_matrices(lam_re, lam_im, b_re, b_im, c_re, c_im, log_step):
    hi = lax.Precision.HIGHEST
    L = SSM_CHUNK
    step = jnp.exp(log_step)[:, None]
    ea, eb = lam_re * step, lam_im * step
    mag = jnp.exp(ea)
    lbr, lbi = mag * jnp.cos(eb), mag * jnp.sin(eb)
    den = lam_re * lam_re + lam_im * lam_im
    nr, ni = lbr - 1.0, lbi
    cr = (nr * lam_re + ni * lam_im) / den
    ci = (ni * lam_re - nr * lam_im) / den
    bbr = cr[..., None] * b_re - ci[..., None] * b_im
    bbi = cr[..., None] * b_im + ci[..., None] * b_re
    taus = jnp.arange(L + 1, dtype=F32)[:, None, None]
    pmag = jnp.exp(taus * ea[None])
    pwr, pwi = pmag * jnp.cos(taus * eb[None]), pmag * jnp.sin(taus * eb[None])
    mr = c_re[None] * pwr[:L, :, None, :] - c_im[None] * pwi[:L, :, None, :]
    mi = c_re[None] * pwi[:L, :, None, :] + c_im[None] * pwr[:L, :, None, :]
    kk = (jnp.einsum("tghp,gpk->tghk", mr, bbr, precision=hi)
          - jnp.einsum("tghp,gpk->tghk", mi, bbi, precision=hi))
    jj = jnp.arange(L)[:, None]
    tt = jnp.arange(L)[None, :]
    d = tt - jj
    kt = kk[jnp.clip(d, 0, L - 1)]
    kt = jnp.where((d >= 0)[:, :, None, None, None], kt, 0.0)
    toep = jnp.transpose(kt, (2, 0, 4, 1, 3)).reshape(N_GROUPS, CHUNK_W, CHUNK_W)
    rev_r, rev_i = pwr[L - 1 - jnp.arange(L)], pwi[L - 1 - jnp.arange(L)]
    wer = rev_r[..., None] * bbr[None] - rev_i[..., None] * bbi[None]
    wei = rev_r[..., None] * bbi[None] + rev_i[..., None] * bbr[None]
    wer = jnp.transpose(wer, (1, 0, 3, 2)).reshape(N_GROUPS, CHUNK_W, SSM_STATE)
    wei = jnp.transpose(wei, (1, 0, 3, 2)).reshape(N_GROUPS, CHUNK_W, SSM_STATE)
    m1r = c_re[None] * pwr[1:, :, None, :] - c_im[None] * pwi[1:, :, None, :]
    m1i = c_re[None] * pwi[1:, :, None, :] + c_im[None] * pwr[1:, :, None, :]
    wor = jnp.transpose(m1r, (1, 3, 0, 2)).reshape(N_GROUPS, SSM_STATE, CHUNK_W)
    woi = -jnp.transpose(m1i, (1, 3, 0, 2)).reshape(N_GROUPS, SSM_STATE, CHUNK_W)
    ar, ai = pwr[L][:, None, :], pwi[L][:, None, :]
    return toep, wer, wei, wor, woi, ar, ai


def _per_group(fn, ins, outs, *, name):
    g = ins[0].shape[0]
    in_specs = [pl.BlockSpec((None,) + a.shape[1:], lambda i: (i, 0, 0)) for a in ins]
    out_specs = [pl.BlockSpec((None,) + s[1:], lambda i: (i, 0, 0)) for s, _ in outs]
    n_in = len(ins)

    def body(*refs):
        vals = fn(*[r[...] for r in refs[:n_in]])
        for r, v in zip(refs[n_in:], vals):
            r[...] = v.astype(r.dtype)

    return pl.pallas_call(
        body, name=name, grid=(g,), in_specs=in_specs, out_specs=out_specs,
        out_shape=[jax.ShapeDtypeStruct(s, dt) for s, dt in outs],
        compiler_params=pltpu.CompilerParams(dimension_semantics=("parallel",)),
    )(*ins)


def _dot(a, b):
    return jnp.dot(a.astype(MXU_DTYPE), b.astype(MXU_DTYPE), preferred_element_type=F32)


def _dot_nt(a, b):
    return lax.dot_general(a.astype(MXU_DTYPE), b.astype(MXU_DTYPE), (((1,), (1,)), ((), ())),
                           preferred_element_type=F32)


def _dot_tn(a, b):
    return lax.dot_general(a.astype(MXU_DTYPE), b.astype(MXU_DTYPE), (((0,), (0,)), ((), ())),
                           preferred_element_type=F32)


SCAN_GROUPS = 8


def _chunk_scan(er, ei, ar, ai):
    g, nc, p = er.shape
    spec = pl.BlockSpec((SCAN_GROUPS, nc, p), lambda i: (i, 0, 0))
    aspec = pl.BlockSpec((SCAN_GROUPS, 1, p), lambda i: (i, 0, 0))

    def body(er_ref, ei_ref, ar_ref, ai_ref, sr_ref, si_ref):
        a_r, a_i = ar_ref[...], ai_ref[...]

        def step(c, carry):
            s_r, s_i = carry
            sr_ref[:, pl.ds(c, 1), :] = s_r
            si_ref[:, pl.ds(c, 1), :] = s_i
            e_r = er_ref[:, pl.ds(c, 1), :]
            e_i = ei_ref[:, pl.ds(c, 1), :]
            return (a_r * s_r - a_i * s_i + e_r, a_r * s_i + a_i * s_r + e_i)

        zero = jnp.zeros((SCAN_GROUPS, 1, p), F32)
        lax.fori_loop(0, nc, step, (zero, zero))

    return pl.pallas_call(
        body, name="ssm_chunk_scan", grid=(g // SCAN_GROUPS,),
        in_specs=[spec, spec, aspec, aspec], out_specs=[spec, spec],
        out_shape=[jax.ShapeDtypeStruct(er.shape, F32)] * 2,
        compiler_params=pltpu.CompilerParams(dimension_semantics=("parallel",)),
    )(er, ei, ar, ai)


def _chunk_scan_bwd(dsr, dsi, sr, si, ar, ai):
    g, nc, p = dsr.shape
    spec = pl.BlockSpec((SCAN_GROUPS, nc, p), lambda i: (i, 0, 0))
    aspec = pl.BlockSpec((SCAN_GROUPS, 1, p), lambda i: (i, 0, 0))

    def body(dsr_ref, dsi_ref, sr_ref, si_ref, ar_ref, ai_ref, ger_ref, gei_ref, dar_ref, dai_ref):
        a_r, a_i = ar_ref[...], ai_ref[...]

        def step(t, carry):
            g_r, g_i, d_r, d_i = carry
            c = nc - 1 - t
            ger_ref[:, pl.ds(c, 1), :] = g_r
            gei_ref[:, pl.ds(c, 1), :] = g_i
            s_r = sr_ref[:, pl.ds(c, 1), :]
            s_i = si_ref[:, pl.ds(c, 1), :]
            d_r = d_r + g_r * s_r + g_i * s_i
            d_i = d_i + g_i * s_r - g_r * s_i
            n_r = dsr_ref[:, pl.ds(c, 1), :] + a_r * g_r + a_i * g_i
            n_i = dsi_ref[:, pl.ds(c, 1), :] + a_r * g_i - a_i * g_r
            return (n_r, n_i, d_r, d_i)

        zero = jnp.zeros((SCAN_GROUPS, 1, p), F32)
        _, _, d_r, d_i = lax.fori_loop(0, nc, step, (zero, zero, zero, zero))
        dar_ref[...] = d_r
        dai_ref[...] = d_i

    return pl.pallas_call(
        body, name="ssm_chunk_scan_bwd", grid=(g // SCAN_GROUPS,),
        in_specs=[spec, spec, spec, spec, aspec, aspec], out_specs=[spec, spec, aspec, aspec],
        out_shape=[jax.ShapeDtypeStruct(dsr.shape, F32)] * 2 + [jax.ShapeDtypeStruct(ar.shape, F32)] * 2,
        compiler_params=pltpu.CompilerParams(dimension_semantics=("parallel",)),
    )(dsr, dsi, sr, si, ar, ai)


def _to_chunks(a):
    t = a.shape[0]
    return (a.reshape(t // SSM_CHUNK, SSM_CHUNK, N_GROUPS, SSM_GROUP).transpose(2, 0, 1, 3)
            .reshape(N_GROUPS, t // SSM_CHUNK, CHUNK_W))


def _from_chunks(a):
    nc = a.shape[1]
    return (a.reshape(N_GROUPS, nc, SSM_CHUNK, SSM_GROUP).transpose(1, 2, 0, 3)
            .reshape(nc * SSM_CHUNK, D_SSM))


def _ssm_forward(uc, mats):
    toep, wer, wei, wor, woi, ar, ai = mats
    g, nc, _ = uc.shape
    er, ei = _per_group(lambda u, a, b: (_dot(u, a), _dot(u, b)), [uc, wer, wei],
                        [((g, nc, SSM_STATE), F32)] * 2, name="ssm_chunk_end")
    sr, si = _chunk_scan(er, ei, ar, ai)
    (yc,) = _per_group(lambda u, t, s_r, s_i, w_r, w_i: (_dot(u, t) + _dot(s_r, w_r) + _dot(s_i, w_i),),
                       [uc, toep, sr, si, wor, woi], [((g, nc, CHUNK_W), F32)], name="ssm_chunk_out")
    return yc, sr, si


def _ssm_backward(dyc, uc, sr, si, mats):
    toep, wer, wei, wor, woi, ar, ai = mats
    g, nc, _ = uc.shape

    def first(dy, u, t, s_r, s_i, w_r, w_i):
        return (_dot_nt(dy, t), _dot_tn(u, dy), _dot_nt(dy, w_r), _dot_nt(dy, w_i), _dot_tn(s_r, dy), _dot_tn(s_i, dy))

    du1, dtoep, dsr, dsi, dwor, dwoi = _per_group(
        first, [dyc, uc, toep, sr, si, wor, woi],
        [((g, nc, CHUNK_W), F32), ((g, CHUNK_W, CHUNK_W), F32), ((g, nc, SSM_STATE), F32), ((g, nc, SSM_STATE), F32),
         ((g, SSM_STATE, CHUNK_W), F32), ((g, SSM_STATE, CHUNK_W), F32)], name="ssm_bwd_out")
    ger, gei, dar, dai = _chunk_scan_bwd(dsr, dsi, sr, si, ar, ai)

    def second(d1, u, g_r, g_i, w_r, w_i):
        return (d1 + _dot_nt(g_r, w_r) + _dot_nt(g_i, w_i), _dot_tn(u, g_r), _dot_tn(u, g_i))

    duc, dwer, dwei = _per_group(
        second, [du1, uc, ger, gei, wer, wei],
        [((g, nc, CHUNK_W), MXU_DTYPE), ((g, CHUNK_W, SSM_STATE), F32), ((g, CHUNK_W, SSM_STATE), F32)],
        name="ssm_bwd_end")
    return duc, (dtoep, dwer, dwei, dwor, dwoi, dar, dai)


def _t5_bucket(dist):
    max_exact = N_BUCKETS // 2
    is_small = dist < max_exact
    d = jnp.maximum(dist, 1).astype(F32)
    large = max_exact + (jnp.log(d / max_exact) / math.log(MAX_DISTANCE / max_exact)
                         * (N_BUCKETS - max_exact)).astype(jnp.int32)
    large = jnp.minimum(large, N_BUCKETS - 1)
    return jnp.where(is_small, dist, large)


def _band_bucket_and_mask():
    i = jnp.arange(BLOCK)[:, None]
    j = jnp.arange(2 * BLOCK)[None, :]
    dist = BLOCK + i - j
    ok = (dist >= 0) & (dist < WINDOW)
    return _t5_bucket(jnp.clip(dist, 0, None)), ok


def _band_bias(rel_bias_table):
    bucket, ok = _band_bucket_and_mask()
    bias = rel_bias_table[bucket]
    return jnp.transpose(jnp.where(ok[:, :, None], bias, NEG_INF), (2, 0, 1))


def _band_bias_grad(dbias):
    bucket, ok = _band_bucket_and_mask()
    onehot = ((bucket[:, :, None] == jnp.arange(N_BUCKETS)[None, None, :]) & ok[:, :, None]).astype(F32)
    return jnp.einsum("hqs,qsb->bh", dbias, onehot, precision=lax.Precision.HIGHEST)


def _attn_specs():
    head = lambda j, n: (j, n, 0)
    prev = lambda j, n: (j, jnp.maximum(n - 1, 0), 0)
    q_spec = pl.BlockSpec((Q_PER_KV, BLOCK, HEAD_DIM), head)
    kv_prev = pl.BlockSpec((None, BLOCK, HEAD_DIM), prev)
    kv_cur = pl.BlockSpec((None, BLOCK, HEAD_DIM), head)
    bias_spec = pl.BlockSpec((Q_PER_KV, BLOCK, 2 * BLOCK), lambda j, n: (j, 0, 0))
    sink_spec = pl.BlockSpec(memory_space=pltpu.SMEM)
    return q_spec, kv_prev, kv_cur, bias_spec, sink_spec


def _attn_probs(q, kb, bias, edge, sink):
    s = _dot_nt(q, kb) * ATTN_SCALE + bias + edge
    m = jnp.maximum(jnp.max(s, axis=1, keepdims=True), sink)
    e = jnp.exp(s - m)
    es = jnp.exp(sink - m)
    inv = 1.0 / (jnp.sum(e, axis=1, keepdims=True) + es)
    return e * inv, es * inv


def _edge_mask(n):
    col = lax.broadcasted_iota(jnp.int32, (BLOCK, 2 * BLOCK), 1)
    return jnp.where(jnp.logical_or(n > 0, col >= BLOCK), 0.0, NEG_INF)


def _attn_forward(qh, kh, vh, bias, sinks):
    t = qh.shape[1]
    q_spec, kv_prev, kv_cur, bias_spec, sink_spec = _attn_specs()

    def body(sink_ref, q_ref, kp_ref, kc_ref, vp_ref, vc_ref, bias_ref, o_ref):
        j, n = pl.program_id(0), pl.program_id(1)
        kb = jnp.concatenate([kp_ref[...], kc_ref[...]], axis=0)
        vb = jnp.concatenate([vp_ref[...], vc_ref[...]], axis=0)
        edge = _edge_mask(n)
        for g in range(Q_PER_KV):
            p, _ = _attn_probs(q_ref[g], kb, bias_ref[g], edge, sink_ref[j * Q_PER_KV + g])
            o_ref[g] = _dot(p, vb)

    return pl.pallas_call(
        body, name="attn_fwd", grid=(N_KV_HEADS, t // BLOCK),
        in_specs=[sink_spec, q_spec, kv_prev, kv_cur, kv_prev, kv_cur, bias_spec],
        out_specs=q_spec,
        out_shape=jax.ShapeDtypeStruct(qh.shape, F32),
        compiler_params=pltpu.CompilerParams(dimension_semantics=("parallel", "arbitrary")),
    )(sinks, qh, kh, kh, vh, vh, bias)


def _attn_backward(qh, kh, vh, bias, sinks, doh):
    t = qh.shape[1]
    q_spec, kv_prev, kv_cur, bias_spec, sink_spec = _attn_specs()
    sink_out = pl.BlockSpec((Q_PER_KV, 8, 128), lambda j, n: (j, 0, 0))

    def body(sink_ref, q_ref, kp_ref, kc_ref, vp_ref, vc_ref, bias_ref, do_ref,
             dq_ref, dka_ref, dkb_ref, dva_ref, dvb_ref, dbias_ref, dsink_ref):
        j, n = pl.program_id(0), pl.program_id(1)

        @pl.when(n == 0)
        def _():
            dbias_ref[...] = jnp.zeros_like(dbias_ref)
            dsink_ref[...] = jnp.zeros_like(dsink_ref)

        kb = jnp.concatenate([kp_ref[...], kc_ref[...]], axis=0)
        vb = jnp.concatenate([vp_ref[...], vc_ref[...]], axis=0)
        edge = _edge_mask(n)
        dk = jnp.zeros((2 * BLOCK, HEAD_DIM), F32)
        dv = jnp.zeros((2 * BLOCK, HEAD_DIM), F32)
        for g in range(Q_PER_KV):
            q = q_ref[g]
            do = do_ref[g]
            p, ps = _attn_probs(q, kb, bias_ref[g], edge, sink_ref[j * Q_PER_KV + g])
            dp = _dot_nt(do, vb)
            delta = jnp.sum(p * dp, axis=1, keepdims=True)
            ds = p * (dp - delta)
            dbias_ref[g] += ds
            dsink_ref[g] += jnp.broadcast_to(jnp.sum(-ps * delta, axis=0, keepdims=True), (8, 128))
            dq_ref[g] = (_dot(ds, kb) * ATTN_SCALE).astype(dq_ref.dtype)
            dk = dk + _dot_tn(ds, q) * ATTN_SCALE
            dv = dv + _dot_tn(p, do)
        dkb_ref[...] = dk[:BLOCK]
        dka_ref[...] = dk[BLOCK:]
        dvb_ref[...] = dv[:BLOCK]
        dva_ref[...] = dv[BLOCK:]

    kv_shape = jax.ShapeDtypeStruct(kh.shape, F32)
    return pl.pallas_call(
        body, name="attn_bwd", grid=(N_KV_HEADS, t // BLOCK),
        in_specs=[sink_spec, q_spec, kv_prev, kv_cur, kv_prev, kv_cur, bias_spec, q_spec],
        out_specs=[q_spec, kv_cur, kv_cur, kv_cur, kv_cur, bias_spec, sink_out],
        out_shape=[jax.ShapeDtypeStruct(qh.shape, MXU_DTYPE), kv_shape, kv_shape, kv_shape, kv_shape,
                   jax.ShapeDtypeStruct(bias.shape, F32), jax.ShapeDtypeStruct((N_Q_HEADS, 8, 128), F32)],
        compiler_params=pltpu.CompilerParams(dimension_semantics=("parallel", "arbitrary")),
    )(sinks, qh, kh, kh, vh, vh, bias, doh)


def _to_heads(a, heads):
    return a.reshape(a.shape[0], heads, HEAD_DIM).transpose(1, 0, 2)


def _from_heads(a):
    return a.transpose(1, 0, 2).reshape(a.shape[1], a.shape[0] * HEAD_DIM)


def _shift_blocks(cur, prev):
    return cur + jnp.concatenate([prev[:, BLOCK:], jnp.zeros_like(prev[:, :BLOCK])], axis=1)


def _mesh_pos():
    return lax.axis_index("x"), lax.axis_index("y"), lax.axis_index("c")


def _all_gather(x, *, name):
    def body(x_ref, out_ref, send_sems, recv_sems, local_sem):
        x, y, c = _mesh_pos()
        me, sibling = (x, y, c), (x, y, 1 - c)
        chips = [(1 - x, y), (x, 1 - y), (1 - x, 1 - y)]

        def slot(px, py, pc):
            return out_ref.at[4 * px + 2 * py + pc]

        def copy(k, block, to, src=None):
            return pltpu.make_async_remote_copy(
                src_ref=slot(*block) if src is None else src, dst_ref=slot(*block),
                send_sem=send_sems.at[k], recv_sem=recv_sems.at[k],
                device_id=to, device_id_type=pl.DeviceIdType.MESH)

        mine = pltpu.make_async_copy(x_ref, slot(*me), local_sem)
        mine.start()
        first = [copy(0, me, sibling, src=x_ref)]
        first += [copy(1 + j, me, (*chip, c), src=x_ref) for j, chip in enumerate(chips)]
        for cp in first:
            cp.start()
        passed = [copy(4 + j, (*chip, c), sibling) for j, chip in enumerate(chips)]
        for j, chip in enumerate(chips):
            copy(1 + j, (*chip, c), me).wait_recv()
            passed[j].start()
        copy(0, sibling, me).wait_recv()
        for j, chip in enumerate(chips):
            copy(4 + j, (*chip, 1 - c), me).wait_recv()
        for cp in first + passed:
            cp.wait_send()
        mine.wait()

    return pl.pallas_call(
        body, name=name,
        in_specs=[pl.BlockSpec(memory_space=pl.ANY)],
        out_specs=pl.BlockSpec(memory_space=pl.ANY),
        out_shape=jax.ShapeDtypeStruct((N_DEV,) + x.shape, x.dtype),
        scratch_shapes=[pltpu.SemaphoreType.DMA((7,)), pltpu.SemaphoreType.DMA((7,)), pltpu.SemaphoreType.DMA],
    )(x)


def _all_to_all(x, *, name):
    def body(x_ref, out_ref, send_sems, recv_sems, local_sem):
        x, y, c = _mesh_pos()
        me = 4 * x + 2 * y + c
        mine = pltpu.make_async_copy(x_ref.at[me], out_ref.at[me], local_sem)
        mine.start()
        copies = []
        for k in range(1, N_DEV):
            px, py, pc = x ^ (k >> 2), y ^ ((k >> 1) & 1), c ^ (k & 1)
            peer = 4 * px + 2 * py + pc
            copies.append(pltpu.make_async_remote_copy(
                src_ref=x_ref.at[peer], dst_ref=out_ref.at[me],
                send_sem=send_sems.at[k - 1], recv_sem=recv_sems.at[k - 1],
                device_id=(px, py, pc), device_id_type=pl.DeviceIdType.MESH))
        for cp in copies:
            cp.start()
        for k in range(1, N_DEV):
            px, py, pc = x ^ (k >> 2), y ^ ((k >> 1) & 1), c ^ (k & 1)
            peer = 4 * px + 2 * py + pc
            pltpu.make_async_remote_copy(
                src_ref=x_ref.at[peer], dst_ref=out_ref.at[peer],
                send_sem=send_sems.at[k - 1], recv_sem=recv_sems.at[k - 1],
                device_id=(px, py, pc), device_id_type=pl.DeviceIdType.MESH).wait_recv()
        for cp in copies:
            cp.wait_send()
        mine.wait()

    return pl.pallas_call(
        body, name=name,
        in_specs=[pl.BlockSpec(memory_space=pl.ANY)],
        out_specs=pl.BlockSpec(memory_space=pl.ANY),
        out_shape=jax.ShapeDtypeStruct(x.shape, x.dtype),
        scratch_shapes=[pltpu.SemaphoreType.DMA((7,)), pltpu.SemaphoreType.DMA((7,)), pltpu.SemaphoreType.DMA],
    )(x)


def _adamw_math(w, g, m, v):
    m = ADAM_B1 * m + (1.0 - ADAM_B1) * g
    v = ADAM_B2 * v + (1.0 - ADAM_B2) * (g * g)
    m_hat = m / (1.0 - ADAM_B1 ** ADAM_STEP)
    v_hat = v / (1.0 - ADAM_B2 ** ADAM_STEP)
    delta = -ADAM_LR * (m_hat / (jnp.sqrt(v_hat) + ADAM_EPS) + ADAM_WD * w)
    return delta, m, v


def _adamw_reduce(parts, w, m, v, *, name, tr):
    r, c = w.shape
    tr = min(tr, r)
    spec = pl.BlockSpec((tr, c), lambda i: (i, 0))

    def body(p_ref, w_ref, m_ref, v_ref, g_ref, d_ref, nm_ref, nv_ref):
        g = p_ref[0]
        for s in range(1, N_DEV):
            g = g + p_ref[s]
        delta, nm, nv = _adamw_math(w_ref[...], g, m_ref[...], v_ref[...])
        g_ref[...] = g
        d_ref[...] = delta
        nm_ref[...] = nm
        nv_ref[...] = nv

    return pl.pallas_call(
        body, name=name, grid=(r // tr,),
        in_specs=[pl.BlockSpec((N_DEV, tr, c), lambda i: (0, i, 0)), spec, spec, spec],
        out_specs=[spec] * 4,
        out_shape=[jax.ShapeDtypeStruct((r, c), F32)] * 4,
        compiler_params=pltpu.CompilerParams(
            dimension_semantics=("parallel",),
            vmem_limit_bytes=_vmem_limit(2 * 15 * _nbytes((tr, c), F32))),
    )(parts, w, m, v)


SMALL = ["ssm_lambda_re", "ssm_lambda_im", "ssm_b_re", "ssm_b_im", "ssm_c_re", "ssm_c_im", "ssm_d",
         "ssm_log_step", "attn_sinks", "rel_bias_table", "ln_gain", "ln_bias"]


def _pack(arrs):
    flat = jnp.concatenate([a.reshape(-1) for a in arrs])
    pad = (-flat.shape[0]) % 1024
    return jnp.pad(flat, (0, pad)).reshape(-1, 128)


def _unpack(packed, like):
    flat = packed.reshape(-1)
    out, pos = [], 0
    for a in like:
        out.append(flat[pos:pos + a.size].reshape(a.shape))
        pos += a.size
    return out


def kernel(x, w_in, ssm_lambda_re, ssm_lambda_im, ssm_b_re, ssm_b_im, ssm_c_re, ssm_c_im, ssm_d, ssm_log_step, w_glu, attn_sinks, rel_bias_table, w_branch_ssm, w_branch_attn, w_out, ln_gain, ln_bias, loss_target, m_w_in, m_ssm_lambda_re, m_ssm_lambda_im, m_ssm_b_re, m_ssm_b_im, m_ssm_c_re, m_ssm_c_im, m_ssm_d, m_ssm_log_step, m_w_glu, m_attn_sinks, m_rel_bias_table, m_w_branch_ssm, m_w_branch_attn, m_w_out, m_ln_gain, m_ln_bias, v_w_in, v_ssm_lambda_re, v_ssm_lambda_im, v_ssm_b_re, v_ssm_b_im, v_ssm_c_re, v_ssm_c_im, v_ssm_d, v_ssm_log_step, v_w_glu, v_attn_sinks, v_rel_bias_table, v_w_branch_ssm, v_w_branch_attn, v_w_out, v_ln_gain, v_ln_bias):
    t = x.shape[1]
    xs = x[0]
    target = loss_target[0]
    col_in = w_in.shape[2]
    col_br = w_glu.shape[2]
    row_out = w_out.shape[1]

    g_in = _all_gather(w_in[0].astype(MXU_DTYPE), name="gather_w_in")
    three = jnp.concatenate([w_glu[0], w_branch_ssm[0], w_branch_attn[0]], axis=0).astype(MXU_DTYPE)
    g_three = _all_gather(three, name="gather_w_1024")
    g_out = _all_gather(w_out[0].astype(MXU_DTYPE), name="gather_w_out")
    win = g_in.transpose(1, 0, 2).reshape(D_MODEL, D_IN)
    three_full = g_three.transpose(1, 0, 2).reshape(3 * D_SSM, N_DEV * col_br)
    wglu, wbs, wba = three_full[:D_SSM], three_full[D_SSM:2 * D_SSM], three_full[2 * D_SSM:]
    wout = g_out.reshape(D_MODEL, D_MODEL)

    ssm_params = (ssm_lambda_re[0], ssm_lambda_im[0], ssm_b_re[0], ssm_b_im[0], ssm_c_re[0], ssm_c_im[0],
                  ssm_log_step[0])
    mats, mats_vjp = jax.vjp(_ssm_matrices, *ssm_params)
    bias, bias_vjp = jax.vjp(_band_bias, rel_bias_table)
    sinks = attn_sinks[0]
    d_skip = ssm_d

    x_mx = xs.astype(MXU_DTYPE)
    proj = _mm(x_mx, win, name="in_proj", tm=2048, tn=512, tk=2048)
    uc = _to_chunks(proj[:, :D_SSM].astype(MXU_DTYPE))
    yc, s_re, s_im = _ssm_forward(uc, mats)
    y_conv = _from_chunks(yc)

    def f_gelu(yv, u, d):
        ys = yv + d * u
        return ys, _gelu(ys)

    y_ssm, glu_in = _ew(f_gelu, [(y_conv, "row", 0), (proj, "row", OFF_U), (d_skip, "vec", 0)],
                        [(COL, F32, "row"), (COL, MXU_DTYPE, "row")], rows=t, cw=COL, ncb=4, tr=1024, name="ssm_gelu")
    glu = _mm(glu_in, wglu, name="glu_proj", tm=2048, tn=512, tk=1024)

    def f_hssm(ga, gb, z):
        return (ga * _sigmoid(gb) * _silu(z),)

    (h_ssm,) = _ew(f_hssm, [(glu, "row", 0), (glu, "row", 4), (proj, "row", OFF_ZS)],
                   [(COL, MXU_DTYPE, "row")], rows=t, cw=COL, ncb=4, tr=1024, name="ssm_gate")

    qh = _to_heads(proj[:, OFF_Q * COL:OFF_Q * COL + D_ATTN].astype(MXU_DTYPE), N_Q_HEADS)
    kh = _to_heads(proj[:, OFF_K * COL:OFF_K * COL + D_KV].astype(MXU_DTYPE), N_KV_HEADS)
    vh = _to_heads(proj[:, OFF_V * COL:OFF_V * COL + D_KV].astype(MXU_DTYPE), N_KV_HEADS)
    attn = _from_heads(_attn_forward(qh, kh, vh, bias, sinks))

    def f_hattn(a, z):
        return (a * _silu(z),)

    (h_attn,) = _ew(f_hattn, [(attn, "row", 0), (proj, "row", OFF_ZA)], [(COL, MXU_DTYPE, "row")],
                    rows=t, cw=COL, ncb=4, tr=1024, name="attn_gate")
    p_ssm = _mm(h_ssm, wbs, name="branch_ssm", tm=2048, tn=512, tk=1024)
    p_attn = _mm(h_attn, wba, name="branch_attn", tm=2048, tn=512, tk=1024)

    def f_merge(ps, pa, ls, la):
        return (_sigmoid(ls) * ps + _sigmoid(la) * pa,)

    (merged,) = _ew(f_merge, [(p_ssm, "row", 0), (p_attn, "row", 0), (proj, "row", OFF_GS), (proj, "row", OFF_GA)],
                    [(COL, MXU_DTYPE, "row")], rows=t, cw=COL, ncb=8, tr=1024, name="merge")
    out = _mm(merged, wout, name="out_proj", tm=2048, tn=512, tk=2048)

    def f_norm(xv, ov, tg, gain, lbias):
        r = DEEPNORM_ALPHA * xv + ov
        mu = jnp.mean(r, axis=1, keepdims=True)
        cen = r - mu
        var = jnp.mean(cen * cen, axis=1, keepdims=True)
        rstd = lax.rsqrt(var + LN_EPS)
        xhat = cen * rstd
        yv = xhat * gain + lbias
        diff = yv - tg
        row_loss = 0.5 * jnp.mean(diff * diff, axis=1, keepdims=True)
        loss = jnp.broadcast_to(jnp.sum(row_loss, axis=0, keepdims=True), (1, 128))
        dy = diff * (1.0 / D_MODEL)
        dgain = jnp.sum(dy * xhat, axis=0, keepdims=True)
        dbias = jnp.sum(dy, axis=0, keepdims=True)
        dxh = dy * gain
        dr = rstd * (dxh - jnp.mean(dxh, axis=1, keepdims=True) - xhat * jnp.mean(dxh * xhat, axis=1, keepdims=True))
        return dr, loss, dgain, dbias

    dr, loss_part, g_ln_gain, g_ln_bias = _ew(
        f_norm, [(xs, "row", 0), (out, "row", 0), (target, "row", 0), (ln_gain, "vec", 0), (ln_bias, "vec", 0)],
        [(D_MODEL, F32, "row"), (128, F32, "acc"), (D_MODEL, F32, "acc"), (D_MODEL, F32, "acc")],
        rows=t, cw=D_MODEL, ncb=1, tr=256, name="norm_loss")

    gw_out = _mm(merged, dr, ta=True, name="grad_w_out", tm=2048, tn=512, tk=1024)
    d_merged = _mm(dr, wout, tb=True, name="d_merged", tm=2048, tn=512, tk=2048)

    def b_merge(dm, ps, pa, ls, la):
        gs, ga = _sigmoid(ls), _sigmoid(la)
        return dm * gs, dm * ga, dm * ps * gs * (1.0 - gs), dm * pa * ga * (1.0 - ga)

    dp_ssm, dp_attn, dgl_s, dgl_a = _ew(
        b_merge, [(d_merged, "row", 0), (p_ssm, "row", 0), (p_attn, "row", 0), (proj, "row", OFF_GS), (proj, "row", OFF_GA)],
        [(COL, MXU_DTYPE, "row")] * 4, rows=t, cw=COL, ncb=8, tr=1024, name="merge_bwd")
    gw_bs = _mm(h_ssm, dp_ssm, ta=True, name="grad_w_branch_ssm", tm=1024, tn=512, tk=1024)
    gw_ba = _mm(h_attn, dp_attn, ta=True, name="grad_w_branch_attn", tm=1024, tn=512, tk=1024)
    dh_ssm = _mm(dp_ssm, wbs, tb=True, name="d_h_ssm", tm=2048, tn=512, tk=2048)
    dh_attn = _mm(dp_attn, wba, tb=True, name="d_h_attn", tm=2048, tn=512, tk=2048)

    def b_hssm(dh, ga, gb, z):
        sg = _sigmoid(gb)
        dgate = dh * _silu(z)
        return dgate * sg, dgate * ga * sg * (1.0 - sg), dh * ga * sg * _silu_grad(z)

    dglu_a, dglu_b, dz_ssm = _ew(b_hssm, [(dh_ssm, "row", 0), (glu, "row", 0), (glu, "row", 4), (proj, "row", OFF_ZS)],
                                 [(COL, MXU_DTYPE, "row")] * 3, rows=t, cw=COL, ncb=4, tr=1024, name="ssm_gate_bwd")
    dglu = jnp.concatenate([dglu_a, dglu_b], axis=1)
    gw_glu = _mm(glu_in, dglu, ta=True, name="grad_w_glu", tm=1024, tn=512, tk=1024)
    dglu_in = _mm(dglu, wglu, tb=True, name="d_glu_in", tm=2048, tn=512, tk=2048)

    def b_gelu(dgi, ys, u, d):
        dys = dgi * _gelu_grad(ys)
        return dys, d * dys, jnp.sum(dys * u, axis=0, keepdims=True)

    dy_ssm, du_skip, g_ssm_d = _ew(b_gelu, [(dglu_in, "row", 0), (y_ssm, "row", 0), (proj, "row", OFF_U), (d_skip, "vec", 0)],
                                   [(COL, MXU_DTYPE, "row"), (COL, F32, "row"), (COL, F32, "acc")],
                                   rows=t, cw=COL, ncb=4, tr=1024, name="ssm_gelu_bwd")
    duc, dmats = _ssm_backward(_to_chunks(dy_ssm), uc, s_re, s_im, mats)
    du = (_from_chunks(duc).astype(F32) + du_skip).astype(MXU_DTYPE)
    g_lre, g_lim, g_bre, g_bim, g_cre, g_cim, g_lstep = mats_vjp(dmats)

    def b_hattn(dh, a, z):
        return dh * _silu(z), dh * a * _silu_grad(z)

    d_attn, dz_attn = _ew(b_hattn, [(dh_attn, "row", 0), (attn, "row", 0), (proj, "row", OFF_ZA)],
                          [(COL, F32, "row"), (COL, MXU_DTYPE, "row")], rows=t, cw=COL, ncb=4, tr=1024, name="attn_gate_bwd")
    dqh, dka, dkb, dva, dvb, dbias, dsink = _attn_backward(qh, kh, vh, bias, sinks, _to_heads(d_attn, N_Q_HEADS))
    dq = _from_heads(dqh)
    dk = _from_heads(_shift_blocks(dka, dkb)).astype(MXU_DTYPE)
    dv = _from_heads(_shift_blocks(dva, dvb)).astype(MXU_DTYPE)
    (g_table,) = bias_vjp(dbias)
    g_sinks = dsink[:, 0, 0]

    dproj = jnp.concatenate([du, dz_ssm, dq, dk, dv, dz_attn, dgl_s, dgl_a], axis=1)
    gw_in = _mm(x_mx, dproj, ta=True, name="grad_w_in", tm=2048, tn=512, tk=1024)
    grad_x = _mm(dproj, win, tb=True, add=dr, add_scale=DEEPNORM_ALPHA, name="grad_x", tm=1024, tn=1024, tk=1088)

    def scatter_cols(g, cols):
        return g.reshape(g.shape[0], N_DEV, cols).transpose(1, 0, 2)

    parts_in = _all_to_all(scatter_cols(gw_in, col_in), name="scatter_g_in")
    three_g = jnp.concatenate([gw_glu, gw_bs, gw_ba], axis=0)
    parts_three = _all_to_all(scatter_cols(three_g, col_br), name="scatter_g_1024")
    parts_out = _all_to_all(gw_out.reshape(N_DEV, row_out, D_MODEL), name="scatter_g_out")

    o_in = _adamw_reduce(parts_in, w_in[0], m_w_in[0], v_w_in[0], name="adamw_w_in", tr=128)
    three_w = jnp.concatenate([w_glu[0], w_branch_ssm[0], w_branch_attn[0]], axis=0)
    three_m = jnp.concatenate([m_w_glu[0], m_w_branch_ssm[0], m_w_branch_attn[0]], axis=0)
    three_v = jnp.concatenate([v_w_glu[0], v_w_branch_ssm[0], v_w_branch_attn[0]], axis=0)
    o_three = _adamw_reduce(parts_three, three_w, three_m, three_v, name="adamw_w_1024", tr=512)
    o_out = _adamw_reduce(parts_out, w_out[0], m_w_out[0], v_w_out[0], name="adamw_w_out", tr=128)

    small_w = [ssm_lambda_re, ssm_lambda_im, ssm_b_re, ssm_b_im, ssm_c_re, ssm_c_im, ssm_d, ssm_log_step,
               attn_sinks, rel_bias_table, ln_gain, ln_bias]
    small_m = [m_ssm_lambda_re, m_ssm_lambda_im, m_ssm_b_re, m_ssm_b_im, m_ssm_c_re, m_ssm_c_im, m_ssm_d,
               m_ssm_log_step, m_attn_sinks, m_rel_bias_table, m_ln_gain, m_ln_bias]
    small_v = [v_ssm_lambda_re, v_ssm_lambda_im, v_ssm_b_re, v_ssm_b_im, v_ssm_c_re, v_ssm_c_im, v_ssm_d,
               v_ssm_log_step, v_attn_sinks, v_rel_bias_table, v_ln_gain, v_ln_bias]
    small_g = [g_lre, g_lim, g_bre, g_bim, g_cre, g_cim, g_ssm_d, g_lstep, g_sinks, g_table, g_ln_gain, g_ln_bias]
    parts_small = _all_gather(_pack(small_g), name="gather_g_small")
    o_small = _adamw_reduce(parts_small, _pack(small_w), _pack(small_m), _pack(small_v), name="adamw_small", tr=2160)
    sg, sd, sm, sv = [_unpack(o, small_w) for o in o_small]

    loss = lax.psum(loss_part[0, 0], MESH_AXES)

    def big(o, idx):
        g_in_, g_three_, g_out_ = o_in[idx], o_three[idx], o_out[idx]
        return {"w_in": g_in_[None], "w_glu": g_three_[None, :D_SSM], "w_branch_ssm": g_three_[None, D_SSM:2 * D_SSM],
                "w_branch_attn": g_three_[None, 2 * D_SSM:], "w_out": g_out_[None]}

    order = ["w_in", "ssm_lambda_re", "ssm_lambda_im", "ssm_b_re", "ssm_b_im", "ssm_c_re", "ssm_c_im", "ssm_d",
             "ssm_log_step", "w_glu", "attn_sinks", "rel_bias_table", "w_branch_ssm", "w_branch_attn", "w_out",
             "ln_gain", "ln_bias"]
    outs = [loss, grad_x[None]]
    for idx, small in enumerate([sg, sd, sm, sv]):
        table = big(None, idx)
        table.update(dict(zip(SMALL, small)))
        outs += [table[n] for n in order]
    return tuple(outs)
```

```python
import functools
import math

import jax
import jax.numpy as jnp
from jax import lax
from jax.experimental import pallas as pl
from jax.experimental.pallas import tpu as pltpu

F32 = jnp.float32
MXU_DTYPE = jnp.bfloat16
WIRE_DTYPE = jnp.bfloat16

D_MODEL = 2048
D_SSM = 1024
SSM_GROUP = 16
N_GROUPS = 64
SSM_STATE = 64
N_Q_HEADS = 16
N_KV_HEADS = 4
Q_PER_KV = 4
HEAD_DIM = 64
D_ATTN = 1024
D_KV = 256
WINDOW = 128
BLOCK = 128
N_BUCKETS = 32
MAX_DISTANCE = 128
D_IN = 8704
DEEPNORM_ALPHA = 2.0 ** 0.25
LN_EPS = 1e-5
NEG_INF = -1e30
ATTN_SCALE = HEAD_DIM ** -0.5

ADAM_LR = 0.001
ADAM_B1 = 0.9
ADAM_B2 = 0.999
ADAM_EPS = 1e-08
ADAM_WD = 0.01
ADAM_STEP = 10

N_DEV = 8
SSM_CHUNK = 16
CHUNK_W = SSM_CHUNK * SSM_GROUP
COL = 256
OFF_U, OFF_ZS, OFF_Q, OFF_K, OFF_V, OFF_ZA, OFF_GS, OFF_GA = 0, 4, 8, 12, 13, 14, 18, 26

VMEM_CAP = 56 * 1024 * 1024
MESH_AXES = ("x", "y", "c")


def _vmem_limit(block_bytes):
    return int(min(max(3 * block_bytes, 16 * 1024 * 1024), VMEM_CAP))


def _nbytes(shape, dtype):
    return math.prod(shape) * jnp.dtype(dtype).itemsize


def _tile(n, pref):
    if n <= pref:
        return n
    t = (pref // 128) * 128
    while t >= 128:
        if n % t == 0:
            return t
        t -= 128
    return n


def _mm(a, b, *, name, ta=False, tb=False, out_dtype=F32, tm=1024, tn=512, tk=512, add=None, add_scale=1.0):
    squeeze = a.ndim == 2
    if squeeze:
        a, b = a[None], b[None]
        if add is not None:
            add = add[None]
    nb = a.shape[0]
    m, k = (a.shape[2], a.shape[1]) if ta else (a.shape[1], a.shape[2])
    n = b.shape[1] if tb else b.shape[2]
    tm, tn, tk = _tile(m, tm), _tile(n, tn), _tile(k, tk)
    nk = k // tk
    dn = (((0 if ta else 1,), (1 if tb else 0,)), ((), ()))

    a_spec = (pl.BlockSpec((None, tk, tm), lambda g, i, j, kk: (g, kk, i)) if ta
              else pl.BlockSpec((None, tm, tk), lambda g, i, j, kk: (g, i, kk)))
    b_spec = (pl.BlockSpec((None, tn, tk), lambda g, i, j, kk: (g, j, kk)) if tb
              else pl.BlockSpec((None, tk, tn), lambda g, i, j, kk: (g, kk, j)))
    o_spec = pl.BlockSpec((None, tm, tn), lambda g, i, j, kk: (g, i, j))
    in_specs = [a_spec, b_spec]
    operands = [a, b]
    if add is not None:
        in_specs.append(o_spec)
        operands.append(add)

    def body(*refs):
        a_ref, b_ref = refs[0], refs[1]
        add_ref = refs[2] if add is not None else None
        o_ref = refs[3] if add is not None else refs[2]
        acc_ref = refs[-1]
        kk = pl.program_id(3)
        part = lax.dot_general(a_ref[...].astype(MXU_DTYPE), b_ref[...].astype(MXU_DTYPE), dn,
                               preferred_element_type=F32)

        @pl.when(kk == 0)
        def _():
            acc_ref[...] = part

        @pl.when(kk > 0)
        def _():
            acc_ref[...] += part

        @pl.when(kk == nk - 1)
        def _():
            r = acc_ref[...]
            if add_ref is not None:
                r = r + add_scale * add_ref[...]
            o_ref[...] = r.astype(out_dtype)

    blocks = (_nbytes((tm, tk), a.dtype) + _nbytes((tk, tn), b.dtype) + _nbytes((tm, tn), out_dtype)
              + (_nbytes((tm, tn), F32) if add is not None else 0))
    out = pl.pallas_call(
        body,
        name=name,
        grid=(nb, m // tm, n // tn, nk),
        in_specs=in_specs,
        out_specs=o_spec,
        out_shape=jax.ShapeDtypeStruct((nb, m, n), out_dtype),
        scratch_shapes=[pltpu.VMEM((tm, tn), F32)],
        compiler_params=pltpu.CompilerParams(
            dimension_semantics=("parallel", "parallel", "parallel", "arbitrary"),
            vmem_limit_bytes=_vmem_limit(2 * blocks + 2 * _nbytes((tm, tn), F32))),
    )(*operands)
    return out[0] if squeeze else out


def _ew(fn, ins, outs, *, rows, cw, ncb, tr, name):
    tr = min(tr, rows)
    n_in = len(ins)

    def row_map(off):
        return lambda j, i: (i, off + j)

    def vec_map(off):
        return lambda j, i: (0, off + j)

    in_specs = []
    for arr, kind, off in ins:
        if kind == "row":
            in_specs.append(pl.BlockSpec((tr, cw), row_map(off)))
        else:
            in_specs.append(pl.BlockSpec((1, cw), vec_map(off)))
    out_specs, out_shapes = [], []
    for bw, dt, kind in outs:
        if kind == "row":
            out_specs.append(pl.BlockSpec((tr, bw), row_map(0)))
            out_shapes.append(jax.ShapeDtypeStruct((rows, ncb * bw), dt))
        else:
            out_specs.append(pl.BlockSpec((1, bw), vec_map(0)))
            out_shapes.append(jax.ShapeDtypeStruct((1, ncb * bw), F32))

    def body(*refs):
        i = pl.program_id(1)
        vals = fn(*[r[...] for r in refs[:n_in]])
        for r, (bw, dt, kind), v in zip(refs[n_in:], outs, vals):
            if kind == "row":
                r[...] = v.astype(dt)
            else:
                @pl.when(i == 0)
                def _(r=r):
                    r[...] = jnp.zeros_like(r)

                r[...] += v

    blocks = sum(_nbytes((tr, cw), a.dtype) for a, kind, _ in ins if kind == "row")
    blocks += sum(_nbytes((tr, bw), dt) for bw, dt, kind in outs if kind == "row")
    res = pl.pallas_call(
        body,
        name=name,
        grid=(ncb, rows // tr),
        in_specs=in_specs,
        out_specs=out_specs,
        out_shape=out_shapes,
        compiler_params=pltpu.CompilerParams(
            dimension_semantics=("parallel", "arbitrary"),
            vmem_limit_bytes=_vmem_limit(4 * blocks)),
    )(*[a for a, _, _ in ins])
    return res


def _sigmoid(x):
    return 1.0 / (1.0 + jnp.exp(-x))


INV_SQRT2 = 0.7071067811865476
INV_SQRT_2PI = 0.3989422804014327


def _gelu(x):
    return 0.5 * x * (1.0 + lax.erf(x * INV_SQRT2))


def _gelu_grad(x):
    return 0.5 * (1.0 + lax.erf(x * INV_SQRT2)) + x * INV_SQRT_2PI * jnp.exp(-0.5 * x * x)


def _silu(x):
    return x * _sigmoid(x)


def _silu_grad(x):
    s = _sigmoid(x)
    return s * (1.0 + x * (1.0 - s))


def _ssm_matrices(lam_re, lam_im, b_re, b_im, c_re, c_im, log_step):
    hi = lax.Precision.HIGHEST
    L = SSM_CHUNK
    step = jnp.exp(log_step)[:, None]
    ea, eb = lam_re * step, lam_im * step
    mag = jnp.exp(ea)
    lbr, lbi = mag * jnp.cos(eb), mag * jnp.sin(eb)
    den = lam_re * lam_re + lam_im * lam_im
    nr, ni = lbr - 1.0, lbi
    cr = (nr * lam_re + ni * lam_im) / den
    ci = (ni * lam_re - nr * lam_im) / den
    bbr = cr[..., None] * b_re - ci[..., None] * b_im
    bbi = cr[..., None] * b_im + ci[..., None] * b_re
    taus = jnp.arange(L + 1, dtype=F32)[:, None, None]
    pmag = jnp.exp(taus * ea[None])
    pwr, pwi = pmag * jnp.cos(taus * eb[None]), pmag * jnp.sin(taus * eb[None])
    mr = c_re[None] * pwr[:L, :, None, :] - c_im[None] * pwi[:L, :, None, :]
    mi = c_re[None] * pwi[:L, :, None, :] + c_im[None] * pwr[:L, :, None, :]
    kk = (jnp.einsum("tghp,gpk->tghk", mr, bbr, precision=hi)
          - jnp.einsum("tghp,gpk->tghk", mi, bbi, precision=hi))
    rows = [jnp.concatenate([jnp.zeros((j,) + kk.shape[1:], F32), kk[:L - j]], axis=0) for j in range(L)]
    kt = jnp.stack(rows, axis=0)
    toep = jnp.transpose(kt, (2, 0, 4, 1, 3)).reshape(N_GROUPS, CHUNK_W, CHUNK_W)
    rev_r, rev_i = pwr[L - 1 - jnp.arange(L)], pwi[L - 1 - jnp.arange(L)]
    wer = rev_r[..., None] * bbr[None] - rev_i[..., None] * bbi[None]
    wei = rev_r[..., None] * bbi[None] + rev_i[..., None] * bbr[None]
    wer = jnp.transpose(wer, (1, 0, 3, 2)).reshape(N_GROUPS, CHUNK_W, SSM_STATE)
    wei = jnp.transpose(wei, (1, 0, 3, 2)).reshape(N_GROUPS, CHUNK_W, SSM_STATE)
    m1r = c_re[None] * pwr[1:, :, None, :] - c_im[None] * pwi[1:, :, None, :]
    m1i = c_re[None] * pwi[1:, :, None, :] + c_im[None] * pwr[1:, :, None, :]
    wor = jnp.transpose(m1r, (1, 3, 0, 2)).reshape(N_GROUPS, SSM_STATE, CHUNK_W)
    woi = -jnp.transpose(m1i, (1, 3, 0, 2)).reshape(N_GROUPS, SSM_STATE, CHUNK_W)
    ar, ai = pwr[L][:, None, :], pwi[L][:, None, :]
    return toep, wer, wei, wor, woi, ar, ai


def _per_group(fn, ins, outs, *, name):
    g = ins[0].shape[0]
    in_specs = [pl.BlockSpec((None,) + a.shape[1:], lambda i: (i, 0, 0)) for a in ins]
    out_specs = [pl.BlockSpec((None,) + s[1:], lambda i: (i, 0, 0)) for s, _ in outs]
    n_in = len(ins)

    def body(*refs):
        vals = fn(*[r[...] for r in refs[:n_in]])
        for r, v in zip(refs[n_in:], vals):
            r[...] = v.astype(r.dtype)

    return pl.pallas_call(
        body, name=name, grid=(g,), in_specs=in_specs, out_specs=out_specs,
        out_shape=[jax.ShapeDtypeStruct(s, dt) for s, dt in outs],
        compiler_params=pltpu.CompilerParams(dimension_semantics=("parallel",)),
    )(*ins)


def _dot(a, b):
    return jnp.dot(a.astype(MXU_DTYPE), b.astype(MXU_DTYPE), preferred_element_type=F32)


def _dot_nt(a, b):
    return lax.dot_general(a.astype(MXU_DTYPE), b.astype(MXU_DTYPE), (((1,), (1,)), ((), ())),
                           preferred_element_type=F32)


def _dot_tn(a, b):
    return lax.dot_general(a.astype(MXU_DTYPE), b.astype(MXU_DTYPE), (((0,), (0,)), ((), ())),
                           preferred_element_type=F32)


SCAN_GROUPS = 8


def _chunk_scan(er, ei, ar, ai):
    g, nc, p = er.shape
    spec = pl.BlockSpec((SCAN_GROUPS, nc, p), lambda i: (i, 0, 0))
    aspec = pl.BlockSpec((SCAN_GROUPS, 1, p), lambda i: (i, 0, 0))

    def body(er_ref, ei_ref, ar_ref, ai_ref, sr_ref, si_ref):
        a_r, a_i = ar_ref[...], ai_ref[...]

        def step(c, carry):
            s_r, s_i = carry
            sr_ref[:, pl.ds(c, 1), :] = s_r
            si_ref[:, pl.ds(c, 1), :] = s_i
            e_r = er_ref[:, pl.ds(c, 1), :]
            e_i = ei_ref[:, pl.ds(c, 1), :]
            return (a_r * s_r - a_i * s_i + e_r, a_r * s_i + a_i * s_r + e_i)

        zero = jnp.zeros((SCAN_GROUPS, 1, p), F32)
        lax.fori_loop(0, nc, step, (zero, zero))

    return pl.pallas_call(
        body, name="ssm_chunk_scan", grid=(g // SCAN_GROUPS,),
        in_specs=[spec, spec, aspec, aspec], out_specs=[spec, spec],
        out_shape=[jax.ShapeDtypeStruct(er.shape, F32)] * 2,
        compiler_params=pltpu.CompilerParams(dimension_semantics=("parallel",)),
    )(er, ei, ar, ai)


def _chunk_scan_bwd(dsr, dsi, sr, si, ar, ai):
    g, nc, p = dsr.shape
    spec = pl.BlockSpec((SCAN_GROUPS, nc, p), lambda i: (i, 0, 0))
    aspec = pl.BlockSpec((SCAN_GROUPS, 1, p), lambda i: (i, 0, 0))

    def body(dsr_ref, dsi_ref, sr_ref, si_ref, ar_ref, ai_ref, ger_ref, gei_ref, dar_ref, dai_ref):
        a_r, a_i = ar_ref[...], ai_ref[...]

        def step(t, carry):
            g_r, g_i, d_r, d_i = carry
            c = nc - 1 - t
            ger_ref[:, pl.ds(c, 1), :] = g_r
            gei_ref[:, pl.ds(c, 1), :] = g_i
            s_r = sr_ref[:, pl.ds(c, 1), :]
            s_i = si_ref[:, pl.ds(c, 1), :]
            d_r = d_r + g_r * s_r + g_i * s_i
            d_i = d_i + g_i * s_r - g_r * s_i
            n_r = dsr_ref[:, pl.ds(c, 1), :] + a_r * g_r + a_i * g_i
            n_i = dsi_ref[:, pl.ds(c, 1), :] + a_r * g_i - a_i * g_r
            return (n_r, n_i, d_r, d_i)

        zero = jnp.zeros((SCAN_GROUPS, 1, p), F32)
        _, _, d_r, d_i = lax.fori_loop(0, nc, step, (zero, zero, zero, zero))
        dar_ref[...] = d_r
        dai_ref[...] = d_i

    return pl.pallas_call(
        body, name="ssm_chunk_scan_bwd", grid=(g // SCAN_GROUPS,),
        in_specs=[spec, spec, spec, spec, aspec, aspec], out_specs=[spec, spec, aspec, aspec],
        out_shape=[jax.ShapeDtypeStruct(dsr.shape, F32)] * 2 + [jax.ShapeDtypeStruct(ar.shape, F32)] * 2,
        compiler_params=pltpu.CompilerParams(dimension_semantics=("parallel",)),
    )(dsr, dsi, sr, si, ar, ai)


def _to_chunks(a):
    t = a.shape[0]
    return (a.reshape(t // SSM_CHUNK, SSM_CHUNK, N_GROUPS, SSM_GROUP).transpose(2, 0, 1, 3)
            .reshape(N_GROUPS, t // SSM_CHUNK, CHUNK_W))


def _from_chunks(a):
    nc = a.shape[1]
    return (a.reshape(N_GROUPS, nc, SSM_CHUNK, SSM_GROUP).transpose(1, 2, 0, 3)
            .reshape(nc * SSM_CHUNK, D_SSM))


def _ssm_forward(uc, mats):
    toep, wer, wei, wor, woi, ar, ai = mats
    g, nc, _ = uc.shape
    er, ei = _per_group(lambda u, a, b: (_dot(u, a), _dot(u, b)), [uc, wer, wei],
                        [((g, nc, SSM_STATE), F32)] * 2, name="ssm_chunk_end")
    sr, si = _chunk_scan(er, ei, ar, ai)
    (yc,) = _per_group(lambda u, t, s_r, s_i, w_r, w_i: (_dot(u, t) + _dot(s_r, w_r) + _dot(s_i, w_i),),
                       [uc, toep, sr, si, wor, woi], [((g, nc, CHUNK_W), F32)], name="ssm_chunk_out")
    return yc, sr, si


def _ssm_backward(dyc, uc, sr, si, mats):
    toep, wer, wei, wor, woi, ar, ai = mats
    g, nc, _ = uc.shape

    def first(dy, u, t, s_r, s_i, w_r, w_i):
        return (_dot_nt(dy, t), _dot_tn(u, dy), _dot_nt(dy, w_r), _dot_nt(dy, w_i), _dot_tn(s_r, dy), _dot_tn(s_i, dy))

    du1, dtoep, dsr, dsi, dwor, dwoi = _per_group(
        first, [dyc, uc, toep, sr, si, wor, woi],
        [((g, nc, CHUNK_W), F32), ((g, CHUNK_W, CHUNK_W), F32), ((g, nc, SSM_STATE), F32), ((g, nc, SSM_STATE), F32),
         ((g, SSM_STATE, CHUNK_W), F32), ((g, SSM_STATE, CHUNK_W), F32)], name="ssm_bwd_out")
    ger, gei, dar, dai = _chunk_scan_bwd(dsr, dsi, sr, si, ar, ai)

    def second(d1, u, g_r, g_i, w_r, w_i):
        return (d1 + _dot_nt(g_r, w_r) + _dot_nt(g_i, w_i), _dot_tn(u, g_r), _dot_tn(u, g_i))

    duc, dwer, dwei = _per_group(
        second, [du1, uc, ger, gei, wer, wei],
        [((g, nc, CHUNK_W), MXU_DTYPE), ((g, CHUNK_W, SSM_STATE), F32), ((g, CHUNK_W, SSM_STATE), F32)],
        name="ssm_bwd_end")
    return duc, (dtoep, dwer, dwei, dwor, dwoi, dar, dai)


def _t5_bucket(dist):
    max_exact = N_BUCKETS // 2
    is_small = dist < max_exact
    d = jnp.maximum(dist, 1).astype(F32)
    large = max_exact + (jnp.log(d / max_exact) / math.log(MAX_DISTANCE / max_exact)
                         * (N_BUCKETS - max_exact)).astype(jnp.int32)
    large = jnp.minimum(large, N_BUCKETS - 1)
    return jnp.where(is_small, dist, large)


def _band_bucket_and_mask():
    i = jnp.arange(BLOCK)[:, None]
    j = jnp.arange(2 * BLOCK)[None, :]
    dist = BLOCK + i - j
    ok = (dist >= 0) & (dist < WINDOW)
    return _t5_bucket(jnp.clip(dist, 0, None)), ok


def _band_bias(rel_bias_table):
    bucket, ok = _band_bucket_and_mask()
    onehot = ((bucket[:, :, None] == jnp.arange(N_BUCKETS)[None, None, :]) & ok[:, :, None]).astype(F32)
    bias = jnp.einsum("qsb,bh->hqs", onehot, rel_bias_table, precision=lax.Precision.HIGHEST)
    return bias + jnp.where(ok, 0.0, NEG_INF)[None]


def _attn_specs():
    head = lambda j, n: (j, n, 0)
    prev = lambda j, n: (j, jnp.maximum(n - 1, 0), 0)
    q_spec = pl.BlockSpec((Q_PER_KV, BLOCK, HEAD_DIM), head)
    kv_prev = pl.BlockSpec((None, BLOCK, HEAD_DIM), prev)
    kv_cur = pl.BlockSpec((None, BLOCK, HEAD_DIM), head)
    bias_spec = pl.BlockSpec((Q_PER_KV, BLOCK, 2 * BLOCK), lambda j, n: (j, 0, 0))
    sink_spec = pl.BlockSpec(memory_space=pltpu.SMEM)
    return q_spec, kv_prev, kv_cur, bias_spec, sink_spec


def _attn_probs(q, kb, bias, edge, sink):
    s = _dot_nt(q, kb) * ATTN_SCALE + bias + edge
    m = jnp.maximum(jnp.max(s, axis=1, keepdims=True), sink)
    e = jnp.exp(s - m)
    es = jnp.exp(sink - m)
    inv = 1.0 / (jnp.sum(e, axis=1, keepdims=True) + es)
    return e * inv, es * inv


def _edge_mask(n):
    col = lax.broadcasted_iota(jnp.int32, (BLOCK, 2 * BLOCK), 1)
    return jnp.where(jnp.logical_or(n > 0, col >= BLOCK), 0.0, NEG_INF)


def _attn_forward(qh, kh, vh, bias, sinks):
    t = qh.shape[1]
    q_spec, kv_prev, kv_cur, bias_spec, sink_spec = _attn_specs()

    def body(sink_ref, q_ref, kp_ref, kc_ref, vp_ref, vc_ref, bias_ref, o_ref):
        j, n = pl.program_id(0), pl.program_id(1)
        kb = jnp.concatenate([kp_ref[...], kc_ref[...]], axis=0)
        vb = jnp.concatenate([vp_ref[...], vc_ref[...]], axis=0)
        edge = _edge_mask(n)
        for g in range(Q_PER_KV):
            p, _ = _attn_probs(q_ref[g], kb, bias_ref[g], edge, sink_ref[j * Q_PER_KV + g])
            o_ref[g] = _dot(p, vb)

    return pl.pallas_call(
        body, name="attn_fwd", grid=(N_KV_HEADS, t // BLOCK),
        in_specs=[sink_spec, q_spec, kv_prev, kv_cur, kv_prev, kv_cur, bias_spec],
        out_specs=q_spec,
        out_shape=jax.ShapeDtypeStruct(qh.shape, F32),
        compiler_params=pltpu.CompilerParams(dimension_semantics=("parallel", "arbitrary")),
    )(sinks, qh, kh, kh, vh, vh, bias)


def _attn_backward(qh, kh, vh, bias, sinks, doh):
    t = qh.shape[1]
    q_spec, kv_prev, kv_cur, bias_spec, sink_spec = _attn_specs()
    sink_out = pl.BlockSpec((Q_PER_KV, 8, 128), lambda j, n: (j, 0, 0))

    def body(sink_ref, q_ref, kp_ref, kc_ref, vp_ref, vc_ref, bias_ref, do_ref,
             dq_ref, dka_ref, dkb_ref, dva_ref, dvb_ref, dbias_ref, dsink_ref):
        j, n = pl.program_id(0), pl.program_id(1)

        @pl.when(n == 0)
        def _():
            dbias_ref[...] = jnp.zeros_like(dbias_ref)
            dsink_ref[...] = jnp.zeros_like(dsink_ref)

        kb = jnp.concatenate([kp_ref[...], kc_ref[...]], axis=0)
        vb = jnp.concatenate([vp_ref[...], vc_ref[...]], axis=0)
        edge = _edge_mask(n)
        dk = jnp.zeros((2 * BLOCK, HEAD_DIM), F32)
        dv = jnp.zeros((2 * BLOCK, HEAD_DIM), F32)
        for g in range(Q_PER_KV):
            q = q_ref[g]
            do = do_ref[g]
            p, ps = _attn_probs(q, kb, bias_ref[g], edge, sink_ref[j * Q_PER_KV + g])
            dp = _dot_nt(do, vb)
            delta = jnp.sum(p * dp, axis=1, keepdims=True)
            ds = p * (dp - delta)
            dbias_ref[g] += ds
            dsink_ref[g] += jnp.broadcast_to(jnp.sum(-ps * delta, axis=0, keepdims=True), (8, 128))
            dq_ref[g] = (_dot(ds, kb) * ATTN_SCALE).astype(dq_ref.dtype)
            dk = dk + _dot_tn(ds, q) * ATTN_SCALE
            dv = dv + _dot_tn(p, do)
        dkb_ref[...] = dk[:BLOCK]
        dka_ref[...] = dk[BLOCK:]
        dvb_ref[...] = dv[:BLOCK]
        dva_ref[...] = dv[BLOCK:]

    kv_shape = jax.ShapeDtypeStruct(kh.shape, F32)
    return pl.pallas_call(
        body, name="attn_bwd", grid=(N_KV_HEADS, t // BLOCK),
        in_specs=[sink_spec, q_spec, kv_prev, kv_cur, kv_prev, kv_cur, bias_spec, q_spec],
        out_specs=[q_spec, kv_cur, kv_cur, kv_cur, kv_cur, bias_spec, sink_out],
        out_shape=[jax.ShapeDtypeStruct(qh.shape, MXU_DTYPE), kv_shape, kv_shape, kv_shape, kv_shape,
                   jax.ShapeDtypeStruct(bias.shape, F32), jax.ShapeDtypeStruct((N_Q_HEADS, 8, 128), F32)],
        compiler_params=pltpu.CompilerParams(dimension_semantics=("parallel", "arbitrary")),
    )(sinks, qh, kh, kh, vh, vh, bias, doh)


def _to_heads(a, heads):
    return a.reshape(a.shape[0], heads, HEAD_DIM).transpose(1, 0, 2)


def _from_heads(a):
    return a.transpose(1, 0, 2).reshape(a.shape[1], a.shape[0] * HEAD_DIM)


def _shift_blocks(cur, prev):
    return cur + jnp.concatenate([prev[:, BLOCK:], jnp.zeros_like(prev[:, :BLOCK])], axis=1)


def _mesh_pos():
    return lax.axis_index("x"), lax.axis_index("y"), lax.axis_index("c")


def _all_gather(x, *, name):
    def body(x_ref, out_ref, send_sems, recv_sems, local_sem):
        x, y, c = _mesh_pos()
        me, sibling = (x, y, c), (x, y, 1 - c)
        chips = [(1 - x, y), (x, 1 - y), (1 - x, 1 - y)]

        def slot(px, py, pc):
            return out_ref.at[4 * px + 2 * py + pc]

        def copy(k, block, to, src=None):
            return pltpu.make_async_remote_copy(
                src_ref=slot(*block) if src is None else src, dst_ref=slot(*block),
                send_sem=send_sems.at[k], recv_sem=recv_sems.at[k],
                device_id=to, device_id_type=pl.DeviceIdType.MESH)

        mine = pltpu.make_async_copy(x_ref, slot(*me), local_sem)
        mine.start()
        first = [copy(0, me, sibling, src=x_ref)]
        first += [copy(1 + j, me, (*chip, c), src=x_ref) for j, chip in enumerate(chips)]
        for cp in first:
            cp.start()
        passed = [copy(4 + j, (*chip, c), sibling) for j, chip in enumerate(chips)]
        for j, chip in enumerate(chips):
            copy(1 + j, (*chip, c), me).wait_recv()
            passed[j].start()
        copy(0, sibling, me).wait_recv()
        for j, chip in enumerate(chips):
            copy(4 + j, (*chip, 1 - c), me).wait_recv()
        for cp in first + passed:
            cp.wait_send()
        mine.wait()

    return pl.pallas_call(
        body, name=name,
        in_specs=[pl.BlockSpec(memory_space=pl.ANY)],
        out_specs=pl.BlockSpec(memory_space=pl.ANY),
        out_shape=jax.ShapeDtypeStruct((N_DEV,) + x.shape, x.dtype),
        scratch_shapes=[pltpu.SemaphoreType.DMA((7,)), pltpu.SemaphoreType.DMA((7,)), pltpu.SemaphoreType.DMA],
    )(x)


def _all_to_all(x, *, name):
    def body(x_ref, out_ref, send_sems, recv_sems, local_sem):
        x, y, c = _mesh_pos()
        me = 4 * x + 2 * y + c
        mine = pltpu.make_async_copy(x_ref.at[me], out_ref.at[me], local_sem)
        mine.start()
        copies = []
        for k in range(1, N_DEV):
            px, py, pc = x ^ (k >> 2), y ^ ((k >> 1) & 1), c ^ (k & 1)
            peer = 4 * px + 2 * py + pc
            copies.append(pltpu.make_async_remote_copy(
                src_ref=x_ref.at[peer], dst_ref=out_ref.at[me],
                send_sem=send_sems.at[k - 1], recv_sem=recv_sems.at[k - 1],
                device_id=(px, py, pc), device_id_type=pl.DeviceIdType.MESH))
        for cp in copies:
            cp.start()
        for k in range(1, N_DEV):
            px, py, pc = x ^ (k >> 2), y ^ ((k >> 1) & 1), c ^ (k & 1)
            peer = 4 * px + 2 * py + pc
            pltpu.make_async_remote_copy(
                src_ref=x_ref.at[peer], dst_ref=out_ref.at[peer],
                send_sem=send_sems.at[k - 1], recv_sem=recv_sems.at[k - 1],
                device_id=(px, py, pc), device_id_type=pl.DeviceIdType.MESH).wait_recv()
        for cp in copies:
            cp.wait_send()
        mine.wait()

    return pl.pallas_call(
        body, name=name,
        in_specs=[pl.BlockSpec(memory_space=pl.ANY)],
        out_specs=pl.BlockSpec(memory_space=pl.ANY),
        out_shape=jax.ShapeDtypeStruct(x.shape, x.dtype),
        scratch_shapes=[pltpu.SemaphoreType.DMA((7,)), pltpu.SemaphoreType.DMA((7,)), pltpu.SemaphoreType.DMA],
    )(x)


def _adamw_math(w, g, m, v):
    m = ADAM_B1 * m + (1.0 - ADAM_B1) * g
    v = ADAM_B2 * v + (1.0 - ADAM_B2) * (g * g)
    m_hat = m / (1.0 - ADAM_B1 ** ADAM_STEP)
    v_hat = v / (1.0 - ADAM_B2 ** ADAM_STEP)
    delta = -ADAM_LR * (m_hat / (jnp.sqrt(v_hat) + ADAM_EPS) + ADAM_WD * w)
    return delta, m, v


def _adamw_reduce(parts, w, m, v, *, name, tr):
    r, c = w.shape
    tr = min(tr, r)
    spec = pl.BlockSpec((tr, c), lambda i: (i, 0))

    def body(p_ref, w_ref, m_ref, v_ref, g_ref, d_ref, nm_ref, nv_ref):
        g = p_ref[0].astype(F32)
        for s in range(1, N_DEV):
            g = g + p_ref[s].astype(F32)
        delta, nm, nv = _adamw_math(w_ref[...], g, m_ref[...], v_ref[...])
        g_ref[...] = g
        d_ref[...] = delta
        nm_ref[...] = nm
        nv_ref[...] = nv

    return pl.pallas_call(
        body, name=name, grid=(r // tr,),
        in_specs=[pl.BlockSpec((N_DEV, tr, c), lambda i: (0, i, 0)), spec, spec, spec],
        out_specs=[spec] * 4,
        out_shape=[jax.ShapeDtypeStruct((r, c), F32)] * 4,
        compiler_params=pltpu.CompilerParams(
            dimension_semantics=("parallel",),
            vmem_limit_bytes=_vmem_limit(2 * 15 * _nbytes((tr, c), F32))),
    )(parts, w, m, v)


SMALL = ["ssm_lambda_re", "ssm_lambda_im", "ssm_b_re", "ssm_b_im", "ssm_c_re", "ssm_c_im", "ssm_d",
         "ssm_log_step", "attn_sinks", "rel_bias_table", "ln_gain", "ln_bias"]


def _pack(arrs):
    flat = jnp.concatenate([a.reshape(-1) for a in arrs])
    pad = (-flat.shape[0]) % 1024
    return jnp.pad(flat, (0, pad)).reshape(-1, 128)


def _unpack(packed, like):
    flat = packed.reshape(-1)
    out, pos = [], 0
    for a in like:
        out.append(flat[pos:pos + a.size].reshape(a.shape))
        pos += a.size
    return out


def kernel(x, w_in, ssm_lambda_re, ssm_lambda_im, ssm_b_re, ssm_b_im, ssm_c_re, ssm_c_im, ssm_d, ssm_log_step, w_glu, attn_sinks, rel_bias_table, w_branch_ssm, w_branch_attn, w_out, ln_gain, ln_bias, loss_target, m_w_in, m_ssm_lambda_re, m_ssm_lambda_im, m_ssm_b_re, m_ssm_b_im, m_ssm_c_re, m_ssm_c_im, m_ssm_d, m_ssm_log_step, m_w_glu, m_attn_sinks, m_rel_bias_table, m_w_branch_ssm, m_w_branch_attn, m_w_out, m_ln_gain, m_ln_bias, v_w_in, v_ssm_lambda_re, v_ssm_lambda_im, v_ssm_b_re, v_ssm_b_im, v_ssm_c_re, v_ssm_c_im, v_ssm_d, v_ssm_log_step, v_w_glu, v_attn_sinks, v_rel_bias_table, v_w_branch_ssm, v_w_branch_attn, v_w_out, v_ln_gain, v_ln_bias):
    t = x.shape[1]
    xs = x[0]
    target = loss_target[0]
    col_in = w_in.shape[2]
    col_br = w_glu.shape[2]
    row_out = w_out.shape[1]

    g_in = _all_gather(w_in[0].astype(MXU_DTYPE), name="gather_w_in")
    three = jnp.concatenate([w_glu[0], w_branch_ssm[0], w_branch_attn[0]], axis=0).astype(MXU_DTYPE)
    g_three = _all_gather(three, name="gather_w_1024")
    g_out = _all_gather(w_out[0].astype(MXU_DTYPE), name="gather_w_out")
    win = g_in.transpose(1, 0, 2).reshape(D_MODEL, D_IN)
    three_full = g_three.transpose(1, 0, 2).reshape(3 * D_SSM, N_DEV * col_br)
    wglu, wbs, wba = three_full[:D_SSM], three_full[D_SSM:2 * D_SSM], three_full[2 * D_SSM:]
    wout = g_out.reshape(D_MODEL, D_MODEL)

    ssm_params = (ssm_lambda_re[0], ssm_lambda_im[0], ssm_b_re[0], ssm_b_im[0], ssm_c_re[0], ssm_c_im[0],
                  ssm_log_step[0])
    mats, mats_vjp = jax.vjp(_ssm_matrices, *ssm_params)
    bias, bias_vjp = jax.vjp(_band_bias, rel_bias_table)
    sinks = attn_sinks[0]
    d_skip = ssm_d

    x_mx = xs.astype(MXU_DTYPE)
    proj = _mm(x_mx, win, name="in_proj", tm=2048, tn=512, tk=2048)
    uc = _to_chunks(proj[:, :D_SSM].astype(MXU_DTYPE))
    yc, s_re, s_im = _ssm_forward(uc, mats)
    y_conv = _from_chunks(yc)

    def f_gelu(yv, u, d):
        ys = yv + d * u
        return ys, _gelu(ys)

    y_ssm, glu_in = _ew(f_gelu, [(y_conv, "row", 0), (proj, "row", OFF_U), (d_skip, "vec", 0)],
                        [(COL, F32, "row"), (COL, MXU_DTYPE, "row")], rows=t, cw=COL, ncb=4, tr=1024, name="ssm_gelu")
    glu = _mm(glu_in, wglu, name="glu_proj", tm=2048, tn=512, tk=1024)

    def f_hssm(ga, gb, z):
        return (ga * _sigmoid(gb) * _silu(z),)

    (h_ssm,) = _ew(f_hssm, [(glu, "row", 0), (glu, "row", 4), (proj, "row", OFF_ZS)],
                   [(COL, MXU_DTYPE, "row")], rows=t, cw=COL, ncb=4, tr=1024, name="ssm_gate")

    qh = _to_heads(proj[:, OFF_Q * COL:OFF_Q * COL + D_ATTN].astype(MXU_DTYPE), N_Q_HEADS)
    kh = _to_heads(proj[:, OFF_K * COL:OFF_K * COL + D_KV].astype(MXU_DTYPE), N_KV_HEADS)
    vh = _to_heads(proj[:, OFF_V * COL:OFF_V * COL + D_KV].astype(MXU_DTYPE), N_KV_HEADS)
    attn = _from_heads(_attn_forward(qh, kh, vh, bias, sinks))

    def f_hattn(a, z):
        return (a * _silu(z),)

    (h_attn,) = _ew(f_hattn, [(attn, "row", 0), (proj, "row", OFF_ZA)], [(COL, MXU_DTYPE, "row")],
                    rows=t, cw=COL, ncb=4, tr=1024, name="attn_gate")
    p_ssm = _mm(h_ssm, wbs, name="branch_ssm", tm=2048, tn=512, tk=1024)
    p_attn = _mm(h_attn, wba, name="branch_attn", tm=2048, tn=512, tk=1024)

    def f_merge(ps, pa, ls, la):
        return (_sigmoid(ls) * ps + _sigmoid(la) * pa,)

    (merged,) = _ew(f_merge, [(p_ssm, "row", 0), (p_attn, "row", 0), (proj, "row", OFF_GS), (proj, "row", OFF_GA)],
                    [(COL, MXU_DTYPE, "row")], rows=t, cw=COL, ncb=8, tr=1024, name="merge")
    out = _mm(merged, wout, name="out_proj", tm=2048, tn=512, tk=2048)

    def f_norm(xv, ov, tg, gain, lbias):
        r = DEEPNORM_ALPHA * xv + ov
        mu = jnp.mean(r, axis=1, keepdims=True)
        cen = r - mu
        var = jnp.mean(cen * cen, axis=1, keepdims=True)
        rstd = lax.rsqrt(var + LN_EPS)
        xhat = cen * rstd
        yv = xhat * gain + lbias
        diff = yv - tg
        row_loss = 0.5 * jnp.mean(diff * diff, axis=1, keepdims=True)
        loss = jnp.broadcast_to(jnp.sum(row_loss, axis=0, keepdims=True), (1, 128))
        dy = diff * (1.0 / D_MODEL)
        dgain = jnp.sum(dy * xhat, axis=0, keepdims=True)
        dbias = jnp.sum(dy, axis=0, keepdims=True)
        dxh = dy * gain
        dr = rstd * (dxh - jnp.mean(dxh, axis=1, keepdims=True) - xhat * jnp.mean(dxh * xhat, axis=1, keepdims=True))
        return dr, loss, dgain, dbias

    dr, loss_part, g_ln_gain, g_ln_bias = _ew(
        f_norm, [(xs, "row", 0), (out, "row", 0), (target, "row", 0), (ln_gain, "vec", 0), (ln_bias, "vec", 0)],
        [(D_MODEL, F32, "row"), (128, F32, "acc"), (D_MODEL, F32, "acc"), (D_MODEL, F32, "acc")],
        rows=t, cw=D_MODEL, ncb=1, tr=256, name="norm_loss")

    gw_out = _mm(merged, dr, ta=True, out_dtype=WIRE_DTYPE, name="grad_w_out", tm=2048, tn=512, tk=1024)
    d_merged = _mm(dr, wout, tb=True, name="d_merged", tm=2048, tn=512, tk=2048)

    def b_merge(dm, ps, pa, ls, la):
        gs, ga = _sigmoid(ls), _sigmoid(la)
        return dm * gs, dm * ga, dm * ps * gs * (1.0 - gs), dm * pa * ga * (1.0 - ga)

    dp_ssm, dp_attn, dgl_s, dgl_a = _ew(
        b_merge, [(d_merged, "row", 0), (p_ssm, "row", 0), (p_attn, "row", 0), (proj, "row", OFF_GS), (proj, "row", OFF_GA)],
        [(COL, MXU_DTYPE, "row")] * 4, rows=t, cw=COL, ncb=8, tr=1024, name="merge_bwd")
    gw_bs = _mm(h_ssm, dp_ssm, ta=True, out_dtype=WIRE_DTYPE, name="grad_w_branch_ssm", tm=1024, tn=512, tk=1024)
    gw_ba = _mm(h_attn, dp_attn, ta=True, out_dtype=WIRE_DTYPE, name="grad_w_branch_attn", tm=1024, tn=512, tk=1024)
    dh_ssm = _mm(dp_ssm, wbs, tb=True, name="d_h_ssm", tm=2048, tn=512, tk=2048)
    dh_attn = _mm(dp_attn, wba, tb=True, name="d_h_attn", tm=2048, tn=512, tk=2048)

    def b_hssm(dh, ga, gb, z):
        sg = _sigmoid(gb)
        dgate = dh * _silu(z)
        return dgate * sg, dgate * ga * sg * (1.0 - sg), dh * ga * sg * _silu_grad(z)

    dglu_a, dglu_b, dz_ssm = _ew(b_hssm, [(dh_ssm, "row", 0), (glu, "row", 0), (glu, "row", 4), (proj, "row", OFF_ZS)],
                                 [(COL, MXU_DTYPE, "row")] * 3, rows=t, cw=COL, ncb=4, tr=1024, name="ssm_gate_bwd")
    dglu = jnp.concatenate([dglu_a, dglu_b], axis=1)
    gw_glu = _mm(glu_in, dglu, ta=True, out_dtype=WIRE_DTYPE, name="grad_w_glu", tm=1024, tn=512, tk=1024)
    dglu_in = _mm(dglu, wglu, tb=True, name="d_glu_in", tm=2048, tn=512, tk=2048)

    def b_gelu(dgi, ys, u, d):
        dys = dgi * _gelu_grad(ys)
        return dys, d * dys, jnp.sum(dys * u, axis=0, keepdims=True)

    dy_ssm, du_skip, g_ssm_d = _ew(b_gelu, [(dglu_in, "row", 0), (y_ssm, "row", 0), (proj, "row", OFF_U), (d_skip, "vec", 0)],
                                   [(COL, MXU_DTYPE, "row"), (COL, F32, "row"), (COL, F32, "acc")],
                                   rows=t, cw=COL, ncb=4, tr=1024, name="ssm_gelu_bwd")
    duc, dmats = _ssm_backward(_to_chunks(dy_ssm), uc, s_re, s_im, mats)
    du = (_from_chunks(duc).astype(F32) + du_skip).astype(MXU_DTYPE)
    g_lre, g_lim, g_bre, g_bim, g_cre, g_cim, g_lstep = mats_vjp(dmats)

    def b_hattn(dh, a, z):
        return dh * _silu(z), dh * a * _silu_grad(z)

    d_attn, dz_attn = _ew(b_hattn, [(dh_attn, "row", 0), (attn, "row", 0), (proj, "row", OFF_ZA)],
                          [(COL, F32, "row"), (COL, MXU_DTYPE, "row")], rows=t, cw=COL, ncb=4, tr=1024, name="attn_gate_bwd")
    dqh, dka, dkb, dva, dvb, dbias, dsink = _attn_backward(qh, kh, vh, bias, sinks, _to_heads(d_attn, N_Q_HEADS))
    dq = _from_heads(dqh)
    dk = _from_heads(_shift_blocks(dka, dkb)).astype(MXU_DTYPE)
    dv = _from_heads(_shift_blocks(dva, dvb)).astype(MXU_DTYPE)
    (g_table,) = bias_vjp(dbias)
    g_sinks = dsink[:, 0, 0]

    dproj = jnp.concatenate([du, dz_ssm, dq, dk, dv, dz_attn, dgl_s, dgl_a], axis=1)
    gw_in = _mm(x_mx, dproj, ta=True, out_dtype=WIRE_DTYPE, name="grad_w_in", tm=2048, tn=512, tk=1024)
    grad_x = _mm(dproj, win, tb=True, add=dr, add_scale=DEEPNORM_ALPHA, name="grad_x", tm=1024, tn=1024, tk=2176)

    def scatter_cols(g, cols):
        return g.reshape(g.shape[0], N_DEV, cols).transpose(1, 0, 2)

    parts_in = _all_to_all(scatter_cols(gw_in, col_in), name="scatter_g_in")
    three_g = jnp.concatenate([gw_glu, gw_bs, gw_ba], axis=0)
    parts_three = _all_to_all(scatter_cols(three_g, col_br), name="scatter_g_1024")
    parts_out = _all_to_all(gw_out.reshape(N_DEV, row_out, D_MODEL), name="scatter_g_out")

    o_in = _adamw_reduce(parts_in, w_in[0], m_w_in[0], v_w_in[0], name="adamw_w_in", tr=128)
    three_w = jnp.concatenate([w_glu[0], w_branch_ssm[0], w_branch_attn[0]], axis=0)
    three_m = jnp.concatenate([m_w_glu[0], m_w_branch_ssm[0], m_w_branch_attn[0]], axis=0)
    three_v = jnp.concatenate([v_w_glu[0], v_w_branch_ssm[0], v_w_branch_attn[0]], axis=0)
    o_three = _adamw_reduce(parts_three, three_w, three_m, three_v, name="adamw_w_1024", tr=512)
    o_out = _adamw_reduce(parts_out, w_out[0], m_w_out[0], v_w_out[0], name="adamw_w_out", tr=128)

    small_w = [ssm_lambda_re, ssm_lambda_im, ssm_b_re, ssm_b_im, ssm_c_re, ssm_c_im, ssm_d, ssm_log_step,
               attn_sinks, rel_bias_table, ln_gain, ln_bias]
    small_m = [m_ssm_lambda_re, m_ssm_lambda_im, m_ssm_b_re, m_ssm_b_im, m_ssm_c_re, m_ssm_c_im, m_ssm_d,
               m_ssm_log_step, m_attn_sinks, m_rel_bias_table, m_ln_gain, m_ln_bias]
    small_v = [v_ssm_lambda_re, v_ssm_lambda_im, v_ssm_b_re, v_ssm_b_im, v_ssm_c_re, v_ssm_c_im, v_ssm_d,
               v_ssm_log_step, v_attn_sinks, v_rel_bias_table, v_ln_gain, v_ln_bias]
    small_g = [g_lre, g_lim, g_bre, g_bim, g_cre, g_cim, g_ssm_d, g_lstep, g_sinks, g_table, g_ln_gain, g_ln_bias]
    parts_small = _all_gather(_pack(small_g), name="gather_g_small")
    o_small = _adamw_reduce(parts_small, _pack(small_w), _pack(small_m), _pack(small_v), name="adamw_small", tr=2160)
    sg, sd, sm, sv = [_unpack(o, small_w) for o in o_small]

    loss = lax.psum(loss_part[0, 0], MESH_AXES)

    def big(o, idx):
        g_in_, g_three_, g_out_ = o_in[idx], o_three[idx], o_out[idx]
        return {"w_in": g_in_[None], "w_glu": g_three_[None, :D_SSM], "w_branch_ssm": g_three_[None, D_SSM:2 * D_SSM],
                "w_branch_attn": g_three_[None, 2 * D_SSM:], "w_out": g_out_[None]}

    order = ["w_in", "ssm_lambda_re", "ssm_lambda_im", "ssm_b_re", "ssm_b_im", "ssm_c_re", "ssm_c_im", "ssm_d",
             "ssm_log_step", "w_glu", "attn_sinks", "rel_bias_table", "w_branch_ssm", "w_branch_attn", "w_out",
             "ln_gain", "ln_bias"]
    outs = [loss, grad_x[None]]
    for idx, small in enumerate([sg, sd, sm, sv]):
        table = big(None, idx)
        table.update(dict(zip(SMALL, small)))
        outs += [table[n] for n in order]
    return tuple(outs)
```

```python
import functools
import math

import jax
import jax.numpy as jnp
from jax import lax
from jax.experimental import pallas as pl
from jax.experimental.pallas import tpu as pltpu

F32 = jnp.float32
MXU_DTYPE = jnp.bfloat16
WIRE_DTYPE = jnp.bfloat16

D_MODEL = 2048
D_SSM = 1024
SSM_GROUP = 16
N_GROUPS = 64
SSM_STATE = 64
N_Q_HEADS = 16
N_KV_HEADS = 4
Q_PER_KV = 4
HEAD_DIM = 64
D_ATTN = 1024
D_KV = 256
WINDOW = 128
BLOCK = 128
N_BUCKETS = 32
MAX_DISTANCE = 128
D_IN = 8704
DEEPNORM_ALPHA = 2.0 ** 0.25
LN_EPS = 1e-5
NEG_INF = -1e30
ATTN_SCALE = HEAD_DIM ** -0.5

ADAM_LR = 0.001
ADAM_B1 = 0.9
ADAM_B2 = 0.999
ADAM_EPS = 1e-08
ADAM_WD = 0.01
ADAM_STEP = 10

N_DEV = 8
SSM_CHUNK = 16
GROUP_BLOCK = 8
N_GB = N_GROUPS // GROUP_BLOCK
GB_LANES = GROUP_BLOCK * SSM_GROUP
GB_STATE = GROUP_BLOCK * SSM_STATE
COL = 256
OFF_U, OFF_ZS, OFF_Q, OFF_K, OFF_V, OFF_ZA, OFF_GS, OFF_GA = 0, 4, 8, 12, 13, 14, 18, 26

VMEM_CAP = 56 * 1024 * 1024
MESH_AXES = ("x", "y", "c")


def _vmem_limit(block_bytes):
    return int(min(max(3 * block_bytes, 16 * 1024 * 1024), VMEM_CAP))


def _nbytes(shape, dtype):
    return math.prod(shape) * jnp.dtype(dtype).itemsize


def _tile(n, pref):
    if n <= pref:
        return n
    t = (pref // 128) * 128
    while t >= 128:
        if n % t == 0:
            return t
        t -= 128
    return n


def _mm(a, b, *, name, ta=False, tb=False, out_dtype=F32, tm=1024, tn=512, tk=512, add=None, add_scale=1.0):
    squeeze = a.ndim == 2
    if squeeze:
        a, b = a[None], b[None]
        if add is not None:
            add = add[None]
    nb = a.shape[0]
    m, k = (a.shape[2], a.shape[1]) if ta else (a.shape[1], a.shape[2])
    n = b.shape[1] if tb else b.shape[2]
    tm, tn, tk = _tile(m, tm), _tile(n, tn), _tile(k, tk)
    nk = k // tk
    dn = (((0 if ta else 1,), (1 if tb else 0,)), ((), ()))

    a_spec = (pl.BlockSpec((None, tk, tm), lambda g, i, j, kk: (g, kk, i)) if ta
              else pl.BlockSpec((None, tm, tk), lambda g, i, j, kk: (g, i, kk)))
    b_spec = (pl.BlockSpec((None, tn, tk), lambda g, i, j, kk: (g, j, kk)) if tb
              else pl.BlockSpec((None, tk, tn), lambda g, i, j, kk: (g, kk, j)))
    o_spec = pl.BlockSpec((None, tm, tn), lambda g, i, j, kk: (g, i, j))
    in_specs = [a_spec, b_spec]
    operands = [a, b]
    if add is not None:
        in_specs.append(o_spec)
        operands.append(add)

    def body(*refs):
        a_ref, b_ref = refs[0], refs[1]
        add_ref = refs[2] if add is not None else None
        o_ref = refs[3] if add is not None else refs[2]
        acc_ref = refs[-1]
        kk = pl.program_id(3)
        part = lax.dot_general(a_ref[...].astype(MXU_DTYPE), b_ref[...].astype(MXU_DTYPE), dn,
                               preferred_element_type=F32)

        @pl.when(kk == 0)
        def _():
            acc_ref[...] = part

        @pl.when(kk > 0)
        def _():
            acc_ref[...] += part

        @pl.when(kk == nk - 1)
        def _():
            r = acc_ref[...]
            if add_ref is not None:
                r = r + add_scale * add_ref[...]
            o_ref[...] = r.astype(out_dtype)

    blocks = (_nbytes((tm, tk), a.dtype) + _nbytes((tk, tn), b.dtype) + _nbytes((tm, tn), out_dtype)
              + (_nbytes((tm, tn), F32) if add is not None else 0))
    out = pl.pallas_call(
        body,
        name=name,
        grid=(nb, m // tm, n // tn, nk),
        in_specs=in_specs,
        out_specs=o_spec,
        out_shape=jax.ShapeDtypeStruct((nb, m, n), out_dtype),
        scratch_shapes=[pltpu.VMEM((tm, tn), F32)],
        compiler_params=pltpu.CompilerParams(
            dimension_semantics=("parallel", "parallel", "parallel", "arbitrary"),
            vmem_limit_bytes=_vmem_limit(2 * blocks + 2 * _nbytes((tm, tn), F32))),
    )(*operands)
    return out[0] if squeeze else out


def _ew(fn, ins, outs, *, rows, cw, ncb, tr, name):
    tr = min(tr, rows)
    n_in = len(ins)

    def row_map(off):
        return lambda j, i: (i, off + j)

    def vec_map(off):
        return lambda j, i: (0, off + j)

    in_specs = []
    for arr, kind, off in ins:
        if kind == "row":
            in_specs.append(pl.BlockSpec((tr, cw), row_map(off)))
        else:
            in_specs.append(pl.BlockSpec((1, cw), vec_map(off)))
    out_specs, out_shapes = [], []
    for bw, dt, kind in outs:
        if kind == "row":
            out_specs.append(pl.BlockSpec((tr, bw), row_map(0)))
            out_shapes.append(jax.ShapeDtypeStruct((rows, ncb * bw), dt))
        else:
            out_specs.append(pl.BlockSpec((1, bw), vec_map(0)))
            out_shapes.append(jax.ShapeDtypeStruct((1, ncb * bw), F32))

    def body(*refs):
        i = pl.program_id(1)
        vals = fn(*[r[...] for r in refs[:n_in]])
        for r, (bw, dt, kind), v in zip(refs[n_in:], outs, vals):
            if kind == "row":
                r[...] = v.astype(dt)
            else:
                @pl.when(i == 0)
                def _(r=r):
                    r[...] = jnp.zeros_like(r)

                r[...] += v

    blocks = sum(_nbytes((tr, cw), a.dtype) for a, kind, _ in ins if kind == "row")
    blocks += sum(_nbytes((tr, bw), dt) for bw, dt, kind in outs if kind == "row")
    res = pl.pallas_call(
        body,
        name=name,
        grid=(ncb, rows // tr),
        in_specs=in_specs,
        out_specs=out_specs,
        out_shape=out_shapes,
        compiler_params=pltpu.CompilerParams(
            dimension_semantics=("parallel", "arbitrary"),
            vmem_limit_bytes=_vmem_limit(4 * blocks)),
    )(*[a for a, _, _ in ins])
    return res


def _sigmoid(x):
    return 1.0 / (1.0 + jnp.exp(-x))


INV_SQRT2 = 0.7071067811865476
INV_SQRT_2PI = 0.3989422804014327


def _gelu(x):
    return 0.5 * x * (1.0 + lax.erf(x * INV_SQRT2))


def _gelu_grad(x):
    return 0.5 * (1.0 + lax.erf(x * INV_SQRT2)) + x * INV_SQRT_2PI * jnp.exp(-0.5 * x * x)


def _silu(x):
    return x * _sigmoid(x)


def _silu_grad(x):
    s = _sigmoid(x)
    return s * (1.0 + x * (1.0 - s))


def _ssm_matrices(lam_re, lam_im, b_re, b_im, c_re, c_im, log_step):
    hi = lax.Precision.HIGHEST
    L = SSM_CHUNK
    step = jnp.exp(log_step)[:, None]
    ea, eb = lam_re * step, lam_im * step
    mag = jnp.exp(ea)
    lbr, lbi = mag * jnp.cos(eb), mag * jnp.sin(eb)
    den = lam_re * lam_re + lam_im * lam_im
    nr, ni = lbr - 1.0, lbi
    cr = (nr * lam_re + ni * lam_im) / den
    ci = (ni * lam_re - nr * lam_im) / den
    bbr = cr[..., None] * b_re - ci[..., None] * b_im
    bbi = cr[..., None] * b_im + ci[..., None] * b_re
    taus = jnp.arange(L + 1, dtype=F32)[:, None, None]
    pmag = jnp.exp(taus * ea[None])
    pwr, pwi = pmag * jnp.cos(taus * eb[None]), pmag * jnp.sin(taus * eb[None])
    mr = c_re[None] * pwr[:L, :, None, :] - c_im[None] * pwi[:L, :, None, :]
    mi = c_re[None] * pwi[:L, :, None, :] + c_im[None] * pwr[:L, :, None, :]
    kk = (jnp.einsum("tghp,gpk->tghk", mr, bbr, precision=hi)
          - jnp.einsum("tghp,gpk->tghk", mi, bbi, precision=hi))
    eye = jnp.eye(GROUP_BLOCK, dtype=F32)
    k5 = jnp.transpose(kk.reshape(L, N_GB, GROUP_BLOCK, SSM_GROUP, SSM_GROUP), (1, 0, 2, 4, 3))
    dblk = (k5[:, :, :, :, None, :] * eye[None, None, :, None, :, None]).reshape(N_GB, L, GB_LANES, GB_LANES)
    rev_r, rev_i = pwr[L - 1 - jnp.arange(L)], pwi[L - 1 - jnp.arange(L)]
    wer = rev_r[..., None] * bbr[None] - rev_i[..., None] * bbi[None]
    wei = rev_r[..., None] * bbi[None] + rev_i[..., None] * bbr[None]

    def expand_in(w):
        w = jnp.transpose(w.reshape(L, N_GB, GROUP_BLOCK, SSM_STATE, SSM_GROUP), (1, 0, 2, 4, 3))
        return (w[:, :, :, :, None, :] * eye[None, None, :, None, :, None]).reshape(N_GB, L * GB_LANES, GB_STATE)

    wend = jnp.concatenate([expand_in(wer), expand_in(wei)], axis=2)
    m1r = c_re[None] * pwr[1:, :, None, :] - c_im[None] * pwi[1:, :, None, :]
    m1i = c_re[None] * pwi[1:, :, None, :] + c_im[None] * pwr[1:, :, None, :]

    def expand_out(m):
        m = jnp.transpose(m.reshape(L, N_GB, GROUP_BLOCK, SSM_GROUP, SSM_STATE), (1, 2, 4, 0, 3))
        return (m[:, :, :, :, None, :] * eye[None, :, None, None, :, None]).reshape(N_GB, GB_STATE, L * GB_LANES)

    wout = jnp.concatenate([expand_out(m1r), expand_out(-m1i)], axis=1)
    acat = jnp.concatenate([pwr[L].reshape(N_GB, 1, GB_STATE), pwi[L].reshape(N_GB, 1, GB_STATE)], axis=2)
    return dblk, wend, wout, acat


def _dot(a, b):
    return jnp.dot(a.astype(MXU_DTYPE), b.astype(MXU_DTYPE), preferred_element_type=F32)


def _dot_nt(a, b):
    return lax.dot_general(a.astype(MXU_DTYPE), b.astype(MXU_DTYPE), (((1,), (1,)), ((), ())),
                           preferred_element_type=F32)


def _dot_tn(a, b):
    return lax.dot_general(a.astype(MXU_DTYPE), b.astype(MXU_DTYPE), (((0,), (0,)), ((), ())),
                           preferred_element_type=F32)


STATE_W = 2 * GB_STATE


def _chunk_scan(e, acat):
    nc = e.shape[0]
    spec = pl.BlockSpec((nc, STATE_W), lambda b: (0, b))
    aspec = pl.BlockSpec((None, 1, STATE_W), lambda b: (b, 0, 0))

    def body(e_ref, a_ref, s_ref):
        a_r, a_i = a_ref[:, :GB_STATE], a_ref[:, GB_STATE:]

        def step(c, carry):
            s_r, s_i = carry
            s_ref[pl.ds(c, 1), :GB_STATE] = s_r
            s_ref[pl.ds(c, 1), GB_STATE:] = s_i
            e_r = e_ref[pl.ds(c, 1), :GB_STATE]
            e_i = e_ref[pl.ds(c, 1), GB_STATE:]
            return (a_r * s_r - a_i * s_i + e_r, a_r * s_i + a_i * s_r + e_i)

        zero = jnp.zeros((1, GB_STATE), F32)
        lax.fori_loop(0, nc, step, (zero, zero))

    return pl.pallas_call(
        body, name="ssm_chunk_scan", grid=(N_GB,),
        in_specs=[spec, aspec], out_specs=spec,
        out_shape=jax.ShapeDtypeStruct(e.shape, F32),
        compiler_params=pltpu.CompilerParams(dimension_semantics=("parallel",)),
    )(e, acat)


def _chunk_scan_bwd(ds, s, acat):
    nc = ds.shape[0]
    spec = pl.BlockSpec((nc, STATE_W), lambda b: (0, b))
    aspec = pl.BlockSpec((None, 1, STATE_W), lambda b: (b, 0, 0))

    def body(ds_ref, s_ref, a_ref, ge_ref, da_ref):
        a_r, a_i = a_ref[:, :GB_STATE], a_ref[:, GB_STATE:]

        def step(t, carry):
            g_r, g_i, d_r, d_i = carry
            c = nc - 1 - t
            ge_ref[pl.ds(c, 1), :GB_STATE] = g_r
            ge_ref[pl.ds(c, 1), GB_STATE:] = g_i
            s_r = s_ref[pl.ds(c, 1), :GB_STATE]
            s_i = s_ref[pl.ds(c, 1), GB_STATE:]
            d_r = d_r + g_r * s_r + g_i * s_i
            d_i = d_i + g_i * s_r - g_r * s_i
            n_r = ds_ref[pl.ds(c, 1), :GB_STATE] + a_r * g_r + a_i * g_i
            n_i = ds_ref[pl.ds(c, 1), GB_STATE:] + a_r * g_i - a_i * g_r
            return (n_r, n_i, d_r, d_i)

        zero = jnp.zeros((1, GB_STATE), F32)
        _, _, d_r, d_i = lax.fori_loop(0, nc, step, (zero, zero, zero, zero))
        da_ref[:, :GB_STATE] = d_r
        da_ref[:, GB_STATE:] = d_i

    return pl.pallas_call(
        body, name="ssm_chunk_scan_bwd", grid=(N_GB,),
        in_specs=[spec, spec, aspec], out_specs=[spec, aspec],
        out_shape=[jax.ShapeDtypeStruct(ds.shape, F32), jax.ShapeDtypeStruct(acat.shape, F32)],
        compiler_params=pltpu.CompilerParams(dimension_semantics=("parallel",)),
    )(ds, s, acat)


def _step_views(arr, nc, grid_rank):
    blocks_per_step = arr.shape[1] // SSM_CHUNK // GB_LANES

    def make(j):
        if grid_rank == 1:
            return pl.BlockSpec((nc, GB_LANES), lambda b: (0, j * blocks_per_step + b))
        return pl.BlockSpec((nc, GB_LANES), lambda b, i: (0, j * blocks_per_step + b))

    return [make(j) for j in range(SSM_CHUNK)]


def _cat_steps(refs):
    return jnp.concatenate([r[...].astype(MXU_DTYPE) for r in refs], axis=1)


def _ssm_forward(uview, mats):
    dblk, wend, wout, acat = mats
    nc = uview.shape[0]
    L = SSM_CHUNK

    def body_end(*refs):
        w_ref, e_ref = refs[L], refs[L + 1]
        e_ref[...] = jnp.dot(_cat_steps(refs[:L]), w_ref[...], preferred_element_type=F32)

    e = pl.pallas_call(
        body_end, name="ssm_chunk_end", grid=(N_GB,),
        in_specs=_step_views(uview, nc, 1) + [pl.BlockSpec((None, L * GB_LANES, STATE_W), lambda b: (b, 0, 0))],
        out_specs=pl.BlockSpec((nc, STATE_W), lambda b: (0, b)),
        out_shape=jax.ShapeDtypeStruct((nc, N_GB * STATE_W), F32),
        compiler_params=pltpu.CompilerParams(dimension_semantics=("parallel",), vmem_limit_bytes=VMEM_CAP),
    )(*([uview] * L), wend)
    s = _chunk_scan(e, acat)

    def body_out(*refs):
        d_ref, s_ref, w_ref, y_ref, xcat = refs[L:]
        t = pl.program_id(1)

        @pl.when(t == 0)
        def _():
            xcat[...] = _cat_steps(refs[:L])

        taps = [jnp.where(j <= t, d_ref[jnp.maximum(t - j, 0)], 0.0) for j in range(L)]
        tcol = jnp.concatenate(taps, axis=0).astype(MXU_DTYPE)
        y_ref[...] = (jnp.dot(xcat[...], tcol, preferred_element_type=F32)
                      + jnp.dot(s_ref[...].astype(MXU_DTYPE), w_ref[...], preferred_element_type=F32))

    yview = pl.pallas_call(
        body_out, name="ssm_chunk_out", grid=(N_GB, L),
        in_specs=_step_views(uview, nc, 2) + [
            pl.BlockSpec((None, L, GB_LANES, GB_LANES), lambda b, t: (b, 0, 0, 0)),
            pl.BlockSpec((nc, STATE_W), lambda b, t: (0, b)),
            pl.BlockSpec((None, STATE_W, GB_LANES), lambda b, t: (b, 0, t))],
        out_specs=pl.BlockSpec((nc, GB_LANES), lambda b, t: (0, t * N_GB + b)),
        out_shape=jax.ShapeDtypeStruct((nc, L * D_SSM), F32),
        scratch_shapes=[pltpu.VMEM((nc, L * GB_LANES), MXU_DTYPE)],
        compiler_params=pltpu.CompilerParams(dimension_semantics=("parallel", "arbitrary"),
                                             vmem_limit_bytes=VMEM_CAP),
    )(*([uview] * L), dblk, s, wout)
    return yview, s


def _ssm_backward(dyview, skipview, uview, s, mats):
    dblk, wend, wout, acat = mats
    nc = uview.shape[0]
    L = SSM_CHUNK

    def body_state(*refs):
        w_ref, ds_ref = refs[L], refs[L + 1]
        ds_ref[...] = _dot_nt(_cat_steps(refs[:L]), w_ref[...])

    ds = pl.pallas_call(
        body_state, name="ssm_bwd_state", grid=(N_GB,),
        in_specs=_step_views(dyview, nc, 1) + [pl.BlockSpec((None, STATE_W, L * GB_LANES), lambda b: (b, 0, 0))],
        out_specs=pl.BlockSpec((nc, STATE_W), lambda b: (0, b)),
        out_shape=jax.ShapeDtypeStruct((nc, N_GB * STATE_W), F32),
        compiler_params=pltpu.CompilerParams(dimension_semantics=("parallel",), vmem_limit_bytes=VMEM_CAP),
    )(*([dyview] * L), wout)
    ge, dacat = _chunk_scan_bwd(ds, s, acat)

    def body_in(*refs):
        x_refs, dy_refs = refs[:L], refs[L:2 * L]
        d_ref, ge_ref, s_ref, w_ref, skip_ref, du_ref, dd_ref, dwe_ref, dwo_ref, x3, dy3 = refs[2 * L:]
        i = pl.program_id(1)

        @pl.when(i == 0)
        def _():
            for j in range(L):
                x3[j] = x_refs[j][...].astype(MXU_DTYPE)
                dy3[j] = dy_refs[j][...].astype(MXU_DTYPE)
            dd_ref[...] = jnp.zeros_like(dd_ref)

        taps = [jnp.where(t >= i, d_ref[jnp.maximum(t - i, 0)], 0.0) for t in range(L)]
        trow = jnp.concatenate(taps, axis=1).astype(MXU_DTYPE)
        dycat = jnp.concatenate([dy3[t] for t in range(L)], axis=1)
        ge = ge_ref[...].astype(MXU_DTYPE)
        du = _dot_nt(dycat, trow) + _dot_nt(ge, w_ref[...]) + skip_ref[...]
        du_ref[...] = du.astype(du_ref.dtype)
        dy_i, x_i = dy3[i], x3[i]
        xcat = jnp.concatenate([x3[j] for j in range(L)], axis=1)
        m = _dot_tn(xcat, dy_i)
        for j in range(L):
            dd_ref[jnp.maximum(i - j, 0)] += jnp.where(j <= i, m[j * GB_LANES:(j + 1) * GB_LANES], 0.0)
        dwe_ref[...] = _dot_tn(x_i, ge)
        dwo_ref[...] = _dot_tn(s_ref[...].astype(MXU_DTYPE), dy_i)

    state_spec = pl.BlockSpec((nc, STATE_W), lambda b, i: (0, b))
    out_view = pl.BlockSpec((nc, GB_LANES), lambda b, i: (0, i * N_GB + b))
    duview, ddblk, dwend, dwout = pl.pallas_call(
        body_in, name="ssm_bwd_in", grid=(N_GB, L),
        in_specs=_step_views(uview, nc, 2) + _step_views(dyview, nc, 2) + [
            pl.BlockSpec((None, L, GB_LANES, GB_LANES), lambda b, i: (b, 0, 0, 0)),
            state_spec, state_spec,
            pl.BlockSpec((None, GB_LANES, STATE_W), lambda b, i: (b, i, 0)),
            out_view],
        out_specs=[out_view,
                   pl.BlockSpec((None, L, GB_LANES, GB_LANES), lambda b, i: (b, 0, 0, 0)),
                   pl.BlockSpec((None, GB_LANES, STATE_W), lambda b, i: (b, i, 0)),
                   pl.BlockSpec((None, STATE_W, GB_LANES), lambda b, i: (b, 0, i))],
        out_shape=[jax.ShapeDtypeStruct(dyview.shape, MXU_DTYPE), jax.ShapeDtypeStruct(dblk.shape, F32),
                   jax.ShapeDtypeStruct(wend.shape, F32), jax.ShapeDtypeStruct(wout.shape, F32)],
        scratch_shapes=[pltpu.VMEM((L, nc, GB_LANES), MXU_DTYPE), pltpu.VMEM((L, nc, GB_LANES), MXU_DTYPE)],
        compiler_params=pltpu.CompilerParams(dimension_semantics=("parallel", "arbitrary"),
                                             vmem_limit_bytes=VMEM_CAP),
    )(*([uview] * L), *([dyview] * L), dblk, ge, s, wend, skipview)
    return duview, (ddblk, dwend, dwout, dacat)


def _t5_bucket(dist):
    max_exact = N_BUCKETS // 2
    is_small = dist < max_exact
    d = jnp.maximum(dist, 1).astype(F32)
    large = max_exact + (jnp.log(d / max_exact) / math.log(MAX_DISTANCE / max_exact)
                         * (N_BUCKETS - max_exact)).astype(jnp.int32)
    large = jnp.minimum(large, N_BUCKETS - 1)
    return jnp.where(is_small, dist, large)


def _band_bucket_and_mask():
    i = jnp.arange(BLOCK)[:, None]
    j = jnp.arange(2 * BLOCK)[None, :]
    dist = BLOCK + i - j
    ok = (dist >= 0) & (dist < WINDOW)
    return _t5_bucket(jnp.clip(dist, 0, None)), ok


def _band_bias(rel_bias_table):
    bucket, ok = _band_bucket_and_mask()
    onehot = ((bucket[:, :, None] == jnp.arange(N_BUCKETS)[None, None, :]) & ok[:, :, None]).astype(F32)
    bias = jnp.einsum("qsb,bh->hqs", onehot, rel_bias_table, precision=lax.Precision.HIGHEST)
    return bias + jnp.where(ok, 0.0, NEG_INF)[None]


KV_PAIR = 2
HEADS_PER_STEP = KV_PAIR * Q_PER_KV
Q_LANES = HEADS_PER_STEP * HEAD_DIM
SLAB = 2 * HEAD_DIM
Q_COL0 = OFF_Q * COL // Q_LANES
K_COL0 = OFF_K * COL // SLAB
V_COL0 = OFF_V * COL // SLAB


def _attn_specs():
    q_spec = pl.BlockSpec((BLOCK, Q_LANES), lambda m, n: (n, Q_COL0 + m))
    k_prev = pl.BlockSpec((BLOCK, SLAB), lambda m, n: (jnp.maximum(n - 1, 0), K_COL0 + m))
    k_cur = pl.BlockSpec((BLOCK, SLAB), lambda m, n: (n, K_COL0 + m))
    v_prev = pl.BlockSpec((BLOCK, SLAB), lambda m, n: (jnp.maximum(n - 1, 0), V_COL0 + m))
    v_cur = pl.BlockSpec((BLOCK, SLAB), lambda m, n: (n, V_COL0 + m))
    bias_spec = pl.BlockSpec((HEADS_PER_STEP, BLOCK, 2 * BLOCK), lambda m, n: (m, 0, 0))
    sink_spec = pl.BlockSpec(memory_space=pltpu.SMEM)
    wide = pl.BlockSpec((BLOCK, Q_LANES), lambda m, n: (n, m))
    pair = pl.BlockSpec((BLOCK, SLAB), lambda m, n: (n, m))
    return [sink_spec, q_spec, k_prev, k_cur, v_prev, v_cur, bias_spec], wide, pair


def _low_lanes(shape):
    return lax.broadcasted_iota(jnp.int32, shape, 1) < HEAD_DIM


def _band_halves(prev_ref, cur_ref):
    kb = jnp.concatenate([prev_ref[...], cur_ref[...]], axis=0)
    sw = pltpu.roll(kb, HEAD_DIM, 1)
    lo = _low_lanes(kb.shape)
    zero = jnp.zeros_like(kb)
    first = (jnp.where(lo, kb, zero).astype(MXU_DTYPE), jnp.where(lo, zero, sw).astype(MXU_DTYPE))
    second = (jnp.where(lo, sw, zero).astype(MXU_DTYPE), jnp.where(lo, zero, kb).astype(MXU_DTYPE))
    return first, second


def _fold_pair(acc):
    f = [x + pltpu.roll(x, HEAD_DIM, 1) for x in acc]
    return jnp.where(_low_lanes(f[0].shape), f[0], f[1])


def _attn_probs(q, kb, bias, edge, sink):
    s = _dot_nt(q, kb) * ATTN_SCALE + bias + edge
    m = jnp.maximum(jnp.max(s, axis=1, keepdims=True), sink)
    e = jnp.exp(s - m)
    es = jnp.exp(sink - m)
    inv = 1.0 / (jnp.sum(e, axis=1, keepdims=True) + es)
    return e * inv, es * inv


def _edge_mask(n):
    col = lax.broadcasted_iota(jnp.int32, (BLOCK, 2 * BLOCK), 1)
    return jnp.where(jnp.logical_or(n > 0, col >= BLOCK), 0.0, NEG_INF)


def _attn_forward(proj, bias, sinks):
    t = proj.shape[0]
    in_specs, wide, _ = _attn_specs()

    def body(sink_ref, q_ref, kp_ref, kc_ref, vp_ref, vc_ref, bias_ref, o_ref):
        m, n = pl.program_id(0), pl.program_id(1)
        keys = _band_halves(kp_ref, kc_ref)
        vals = _band_halves(vp_ref, vc_ref)
        edge = _edge_mask(n)
        for s in range(HEADS_PER_STEP // 2):
            a = s // (Q_PER_KV // 2)
            q = q_ref[:, s * SLAB:(s + 1) * SLAB].astype(MXU_DTYPE)
            out = None
            for e in range(2):
                h = 2 * s + e
                p, _ = _attn_probs(q, keys[a][e], bias_ref[h], edge, sink_ref[m * HEADS_PER_STEP + h])
                o = _dot(p, vals[a][e])
                out = o if out is None else out + o
            o_ref[:, s * SLAB:(s + 1) * SLAB] = out

    return pl.pallas_call(
        body, name="attn_fwd", grid=(N_KV_HEADS // KV_PAIR, t // BLOCK),
        in_specs=in_specs, out_specs=wide,
        out_shape=jax.ShapeDtypeStruct((t, D_ATTN), F32),
        compiler_params=pltpu.CompilerParams(dimension_semantics=("parallel", "arbitrary")),
    )(sinks, proj, proj, proj, proj, proj, bias)


def _attn_backward(proj, bias, sinks, d_attn):
    t = proj.shape[0]
    in_specs, wide, pair = _attn_specs()
    bias_spec = in_specs[-1]
    sink_out = pl.BlockSpec((HEADS_PER_STEP, 8, 128), lambda m, n: (m, 0, 0))

    def body(sink_ref, q_ref, kp_ref, kc_ref, vp_ref, vc_ref, bias_ref, do_ref,
             dq_ref, dka_ref, dkb_ref, dva_ref, dvb_ref, dbias_ref, dsink_ref):
        m, n = pl.program_id(0), pl.program_id(1)

        @pl.when(n == 0)
        def _():
            dbias_ref[...] = jnp.zeros_like(dbias_ref)
            dsink_ref[...] = jnp.zeros_like(dsink_ref)

        keys = _band_halves(kp_ref, kc_ref)
        vals = _band_halves(vp_ref, vc_ref)
        edge = _edge_mask(n)
        lo = _low_lanes((2 * BLOCK, SLAB))
        dk = [jnp.zeros((2 * BLOCK, SLAB), F32) for _ in range(KV_PAIR)]
        dv = [jnp.zeros((2 * BLOCK, SLAB), F32) for _ in range(KV_PAIR)]
        for s in range(HEADS_PER_STEP // 2):
            a = s // (Q_PER_KV // 2)
            q = q_ref[:, s * SLAB:(s + 1) * SLAB].astype(MXU_DTYPE)
            do = do_ref[:, s * SLAB:(s + 1) * SLAB].astype(MXU_DTYPE)
            dq, rk, rv = None, [], []
            for e in range(2):
                h = 2 * s + e
                p, ps = _attn_probs(q, keys[a][e], bias_ref[h], edge, sink_ref[m * HEADS_PER_STEP + h])
                dp = _dot_nt(do, vals[a][e])
                delta = jnp.sum(p * dp, axis=1, keepdims=True)
                ds = p * (dp - delta)
                dbias_ref[h] += ds
                dsink_ref[h] += jnp.broadcast_to(jnp.sum(-ps * delta, axis=0, keepdims=True), (8, 128))
                dqe = _dot(ds, keys[a][e])
                dq = dqe if dq is None else dq + dqe
                rk.append(_dot_tn(ds, q))
                rv.append(_dot_tn(p, do))
            dq_ref[:, s * SLAB:(s + 1) * SLAB] = (dq * ATTN_SCALE).astype(dq_ref.dtype)
            dk[a] = dk[a] + jnp.where(lo, rk[0], rk[1])
            dv[a] = dv[a] + jnp.where(lo, rv[0], rv[1])
        dkp = _fold_pair(dk) * ATTN_SCALE
        dvp = _fold_pair(dv)
        dkb_ref[...] = dkp[:BLOCK]
        dka_ref[...] = dkp[BLOCK:]
        dvb_ref[...] = dvp[:BLOCK]
        dva_ref[...] = dvp[BLOCK:]

    kv_shape = jax.ShapeDtypeStruct((t, D_KV), F32)
    return pl.pallas_call(
        body, name="attn_bwd", grid=(N_KV_HEADS // KV_PAIR, t // BLOCK),
        in_specs=in_specs + [wide],
        out_specs=[wide, pair, pair, pair, pair, bias_spec, sink_out],
        out_shape=[jax.ShapeDtypeStruct((t, D_ATTN), MXU_DTYPE), kv_shape, kv_shape, kv_shape, kv_shape,
                   jax.ShapeDtypeStruct(bias.shape, F32), jax.ShapeDtypeStruct((N_Q_HEADS, 8, 128), F32)],
        compiler_params=pltpu.CompilerParams(dimension_semantics=("parallel", "arbitrary")),
    )(sinks, proj, proj, proj, proj, proj, bias, d_attn)


def _shift_blocks(cur, prev):
    return cur + jnp.concatenate([prev[BLOCK:], jnp.zeros_like(prev[:BLOCK])], axis=0)


def _mesh_pos():
    return lax.axis_index("x"), lax.axis_index("y"), lax.axis_index("c")


def _all_gather(x, *, name):
    def body(x_ref, out_ref, send_sems, recv_sems, local_sem):
        x, y, c = _mesh_pos()
        me, sibling = (x, y, c), (x, y, 1 - c)
        chips = [(1 - x, y), (x, 1 - y), (1 - x, 1 - y)]

        def slot(px, py, pc):
            return out_ref.at[4 * px + 2 * py + pc]

        def copy(k, block, to, src=None):
            return pltpu.make_async_remote_copy(
                src_ref=slot(*block) if src is None else src, dst_ref=slot(*block),
                send_sem=send_sems.at[k], recv_sem=recv_sems.at[k],
                device_id=to, device_id_type=pl.DeviceIdType.MESH)

        mine = pltpu.make_async_copy(x_ref, slot(*me), local_sem)
        mine.start()
        first = [copy(0, me, sibling, src=x_ref)]
        first += [copy(1 + j, me, (*chip, c), src=x_ref) for j, chip in enumerate(chips)]
        for cp in first:
            cp.start()
        passed = [copy(4 + j, (*chip, c), sibling) for j, chip in enumerate(chips)]
        for j, chip in enumerate(chips):
            copy(1 + j, (*chip, c), me).wait_recv()
            passed[j].start()
        copy(0, sibling, me).wait_recv()
        for j, chip in enumerate(chips):
            copy(4 + j, (*chip, 1 - c), me).wait_recv()
        for cp in first + passed:
            cp.wait_send()
        mine.wait()

    return pl.pallas_call(
        body, name=name,
        in_specs=[pl.BlockSpec(memory_space=pl.ANY)],
        out_specs=pl.BlockSpec(memory_space=pl.ANY),
        out_shape=jax.ShapeDtypeStruct((N_DEV,) + x.shape, x.dtype),
        scratch_shapes=[pltpu.SemaphoreType.DMA((7,)), pltpu.SemaphoreType.DMA((7,)), pltpu.SemaphoreType.DMA],
    )(x)


def _all_to_all(x, *, name):
    def body(x_ref, out_ref, send_sems, recv_sems, local_sem):
        x, y, c = _mesh_pos()
        me = 4 * x + 2 * y + c
        mine = pltpu.make_async_copy(x_ref.at[me], out_ref.at[me], local_sem)
        mine.start()
        copies = []
        for k in range(1, N_DEV):
            px, py, pc = x ^ (k >> 2), y ^ ((k >> 1) & 1), c ^ (k & 1)
            peer = 4 * px + 2 * py + pc
            copies.append(pltpu.make_async_remote_copy(
                src_ref=x_ref.at[peer], dst_ref=out_ref.at[me],
                send_sem=send_sems.at[k - 1], recv_sem=recv_sems.at[k - 1],
                device_id=(px, py, pc), device_id_type=pl.DeviceIdType.MESH))
        for cp in copies:
            cp.start()
        for k in range(1, N_DEV):
            px, py, pc = x ^ (k >> 2), y ^ ((k >> 1) & 1), c ^ (k & 1)
            peer = 4 * px + 2 * py + pc
            pltpu.make_async_remote_copy(
                src_ref=x_ref.at[peer], dst_ref=out_ref.at[peer],
                send_sem=send_sems.at[k - 1], recv_sem=recv_sems.at[k - 1],
                device_id=(px, py, pc), device_id_type=pl.DeviceIdType.MESH).wait_recv()
        for cp in copies:
            cp.wait_send()
        mine.wait()

    return pl.pallas_call(
        body, name=name,
        in_specs=[pl.BlockSpec(memory_space=pl.ANY)],
        out_specs=pl.BlockSpec(memory_space=pl.ANY),
        out_shape=jax.ShapeDtypeStruct(x.shape, x.dtype),
        scratch_shapes=[pltpu.SemaphoreType.DMA((7,)), pltpu.SemaphoreType.DMA((7,)), pltpu.SemaphoreType.DMA],
    )(x)


def _adamw_math(w, g, m, v):
    m = ADAM_B1 * m + (1.0 - ADAM_B1) * g
    v = ADAM_B2 * v + (1.0 - ADAM_B2) * (g * g)
    m_hat = m / (1.0 - ADAM_B1 ** ADAM_STEP)
    v_hat = v / (1.0 - ADAM_B2 ** ADAM_STEP)
    delta = -ADAM_LR * (m_hat / (jnp.sqrt(v_hat) + ADAM_EPS) + ADAM_WD * w)
    return delta, m, v


def _adamw_reduce(parts, w, m, v, *, name, tr):
    r, c = w.shape
    tr = min(tr, r)
    spec = pl.BlockSpec((tr, c), lambda i: (i, 0))

    def body(p_ref, w_ref, m_ref, v_ref, g_ref, d_ref, nm_ref, nv_ref):
        g = p_ref[0].astype(F32)
        for s in range(1, N_DEV):
            g = g + p_ref[s].astype(F32)
        delta, nm, nv = _adamw_math(w_ref[...], g, m_ref[...], v_ref[...])
        g_ref[...] = g
        d_ref[...] = delta
        nm_ref[...] = nm
        nv_ref[...] = nv

    return pl.pallas_call(
        body, name=name, grid=(r // tr,),
        in_specs=[pl.BlockSpec((N_DEV, tr, c), lambda i: (0, i, 0)), spec, spec, spec],
        out_specs=[spec] * 4,
        out_shape=[jax.ShapeDtypeStruct((r, c), F32)] * 4,
        compiler_params=pltpu.CompilerParams(
            dimension_semantics=("parallel",),
            vmem_limit_bytes=_vmem_limit(2 * 15 * _nbytes((tr, c), F32))),
    )(parts, w, m, v)


SMALL = ["ssm_lambda_re", "ssm_lambda_im", "ssm_b_re", "ssm_b_im", "ssm_c_re", "ssm_c_im", "ssm_d",
         "ssm_log_step", "attn_sinks", "rel_bias_table", "ln_gain", "ln_bias"]


def _pack(arrs):
    flat = jnp.concatenate([a.reshape(-1) for a in arrs])
    pad = (-flat.shape[0]) % 1024
    return jnp.pad(flat, (0, pad)).reshape(-1, 128)


def _unpack(packed, like):
    flat = packed.reshape(-1)
    out, pos = [], 0
    for a in like:
        out.append(flat[pos:pos + a.size].reshape(a.shape))
        pos += a.size
    return out


def kernel(x, w_in, ssm_lambda_re, ssm_lambda_im, ssm_b_re, ssm_b_im, ssm_c_re, ssm_c_im, ssm_d, ssm_log_step, w_glu, attn_sinks, rel_bias_table, w_branch_ssm, w_branch_attn, w_out, ln_gain, ln_bias, loss_target, m_w_in, m_ssm_lambda_re, m_ssm_lambda_im, m_ssm_b_re, m_ssm_b_im, m_ssm_c_re, m_ssm_c_im, m_ssm_d, m_ssm_log_step, m_w_glu, m_attn_sinks, m_rel_bias_table, m_w_branch_ssm, m_w_branch_attn, m_w_out, m_ln_gain, m_ln_bias, v_w_in, v_ssm_lambda_re, v_ssm_lambda_im, v_ssm_b_re, v_ssm_b_im, v_ssm_c_re, v_ssm_c_im, v_ssm_d, v_ssm_log_step, v_w_glu, v_attn_sinks, v_rel_bias_table, v_w_branch_ssm, v_w_branch_attn, v_w_out, v_ln_gain, v_ln_bias):
    t = x.shape[1]
    xs = x[0]
    target = loss_target[0]
    col_in = w_in.shape[2]
    col_br = w_glu.shape[2]
    row_out = w_out.shape[1]

    g_in = _all_gather(w_in[0].astype(MXU_DTYPE), name="gather_w_in")
    three = jnp.concatenate([w_glu[0], w_branch_ssm[0], w_branch_attn[0]], axis=0).astype(MXU_DTYPE)
    g_three = _all_gather(three, name="gather_w_1024")
    g_out = _all_gather(w_out[0].astype(MXU_DTYPE), name="gather_w_out")
    win = g_in.transpose(1, 0, 2).reshape(D_MODEL, D_IN)
    three_full = g_three.transpose(1, 0, 2).reshape(3 * D_SSM, N_DEV * col_br)
    wglu, wbs, wba = three_full[:D_SSM], three_full[D_SSM:2 * D_SSM], three_full[2 * D_SSM:]
    wout = g_out.reshape(D_MODEL, D_MODEL)

    ssm_params = (ssm_lambda_re[0], ssm_lambda_im[0], ssm_b_re[0], ssm_b_im[0], ssm_c_re[0], ssm_c_im[0],
                  ssm_log_step[0])
    mats, mats_vjp = jax.vjp(_ssm_matrices, *ssm_params)
    bias, bias_vjp = jax.vjp(_band_bias, rel_bias_table)
    sinks = attn_sinks[0]
    d_skip = ssm_d

    x_mx = xs.astype(MXU_DTYPE)
    proj = _mm(x_mx, win, name="in_proj", tm=2048, tn=512, tk=2048)
    nc = t // SSM_CHUNK
    uview = proj.reshape(nc, SSM_CHUNK * D_IN)
    dblk, wend, wout_s, acat = mats
    mats_mx = (dblk, wend.astype(MXU_DTYPE), wout_s.astype(MXU_DTYPE), acat)
    yview, states = _ssm_forward(uview, mats_mx)
    y_conv = yview.reshape(t, D_SSM)

    def f_gelu(yv, u, d):
        ys = yv + d * u
        return ys, _gelu(ys)

    y_ssm, glu_in = _ew(f_gelu, [(y_conv, "row", 0), (proj, "row", OFF_U), (d_skip, "vec", 0)],
                        [(COL, F32, "row"), (COL, MXU_DTYPE, "row")], rows=t, cw=COL, ncb=4, tr=1024, name="ssm_gelu")
    glu = _mm(glu_in, wglu, name="glu_proj", tm=2048, tn=512, tk=1024)

    def f_hssm(ga, gb, z):
        return (ga * _sigmoid(gb) * _silu(z),)

    (h_ssm,) = _ew(f_hssm, [(glu, "row", 0), (glu, "row", 4), (proj, "row", OFF_ZS)],
                   [(COL, MXU_DTYPE, "row")], rows=t, cw=COL, ncb=4, tr=1024, name="ssm_gate")

    attn = _attn_forward(proj, bias, sinks)

    def f_hattn(a, z):
        return (a * _silu(z),)

    (h_attn,) = _ew(f_hattn, [(attn, "row", 0), (proj, "row", OFF_ZA)], [(COL, MXU_DTYPE, "row")],
                    rows=t, cw=COL, ncb=4, tr=1024, name="attn_gate")
    p_ssm = _mm(h_ssm, wbs, name="branch_ssm", tm=2048, tn=512, tk=1024)
    p_attn = _mm(h_attn, wba, name="branch_attn", tm=2048, tn=512, tk=1024)

    def f_merge(ps, pa, ls, la):
        return (_sigmoid(ls) * ps + _sigmoid(la) * pa,)

    (merged,) = _ew(f_merge, [(p_ssm, "row", 0), (p_attn, "row", 0), (proj, "row", OFF_GS), (proj, "row", OFF_GA)],
                    [(COL, MXU_DTYPE, "row")], rows=t, cw=COL, ncb=8, tr=1024, name="merge")
    out = _mm(merged, wout, name="out_proj", tm=2048, tn=512, tk=2048)

    def f_norm(xv, ov, tg, gain, lbias):
        r = DEEPNORM_ALPHA * xv + ov
        mu = jnp.mean(r, axis=1, keepdims=True)
        cen = r - mu
        var = jnp.mean(cen * cen, axis=1, keepdims=True)
        rstd = lax.rsqrt(var + LN_EPS)
        xhat = cen * rstd
        yv = xhat * gain + lbias
        diff = yv - tg
        row_loss = 0.5 * jnp.mean(diff * diff, axis=1, keepdims=True)
        loss = jnp.broadcast_to(jnp.sum(row_loss, axis=0, keepdims=True), (1, 128))
        dy = diff * (1.0 / D_MODEL)
        dgain = jnp.sum(dy * xhat, axis=0, keepdims=True)
        dbias = jnp.sum(dy, axis=0, keepdims=True)
        dxh = dy * gain
        dr = rstd * (dxh - jnp.mean(dxh, axis=1, keepdims=True) - xhat * jnp.mean(dxh * xhat, axis=1, keepdims=True))
        return dr, loss, dgain, dbias

    dr, loss_part, g_ln_gain, g_ln_bias = _ew(
        f_norm, [(xs, "row", 0), (out, "row", 0), (target, "row", 0), (ln_gain, "vec", 0), (ln_bias, "vec", 0)],
        [(D_MODEL, F32, "row"), (128, F32, "acc"), (D_MODEL, F32, "acc"), (D_MODEL, F32, "acc")],
        rows=t, cw=D_MODEL, ncb=1, tr=256, name="norm_loss")

    gw_out = _mm(merged, dr, ta=True, out_dtype=WIRE_DTYPE, name="grad_w_out", tm=2048, tn=512, tk=1024)
    d_merged = _mm(dr, wout, tb=True, name="d_merged", tm=2048, tn=512, tk=2048)

    def b_merge(dm, ps, pa, ls, la):
        gs, ga = _sigmoid(ls), _sigmoid(la)
        return dm * gs, dm * ga, dm * ps * gs * (1.0 - gs), dm * pa * ga * (1.0 - ga)

    dp_ssm, dp_attn, dgl_s, dgl_a = _ew(
        b_merge, [(d_merged, "row", 0), (p_ssm, "row", 0), (p_attn, "row", 0), (proj, "row", OFF_GS), (proj, "row", OFF_GA)],
        [(COL, MXU_DTYPE, "row")] * 4, rows=t, cw=COL, ncb=8, tr=1024, name="merge_bwd")
    gw_bs = _mm(h_ssm, dp_ssm, ta=True, out_dtype=WIRE_DTYPE, name="grad_w_branch_ssm", tm=1024, tn=512, tk=1024)
    gw_ba = _mm(h_attn, dp_attn, ta=True, out_dtype=WIRE_DTYPE, name="grad_w_branch_attn", tm=1024, tn=512, tk=1024)
    dh_ssm = _mm(dp_ssm, wbs, tb=True, name="d_h_ssm", tm=2048, tn=512, tk=2048)
    dh_attn = _mm(dp_attn, wba, tb=True, name="d_h_attn", tm=2048, tn=512, tk=2048)

    def b_hssm(dh, ga, gb, z):
        sg = _sigmoid(gb)
        dgate = dh * _silu(z)
        return dgate * sg, dgate * ga * sg * (1.0 - sg), dh * ga * sg * _silu_grad(z)

    dglu_a, dglu_b, dz_ssm = _ew(b_hssm, [(dh_ssm, "row", 0), (glu, "row", 0), (glu, "row", 4), (proj, "row", OFF_ZS)],
                                 [(COL, MXU_DTYPE, "row")] * 3, rows=t, cw=COL, ncb=4, tr=1024, name="ssm_gate_bwd")
    dglu = jnp.concatenate([dglu_a, dglu_b], axis=1)
    gw_glu = _mm(glu_in, dglu, ta=True, out_dtype=WIRE_DTYPE, name="grad_w_glu", tm=1024, tn=512, tk=1024)
    dglu_in = _mm(dglu, wglu, tb=True, name="d_glu_in", tm=2048, tn=512, tk=2048)

    def b_gelu(dgi, ys, u, d):
        dys = dgi * _gelu_grad(ys)
        return dys, d * dys, jnp.sum(dys * u, axis=0, keepdims=True)

    dy_ssm, du_skip, g_ssm_d = _ew(b_gelu, [(dglu_in, "row", 0), (y_ssm, "row", 0), (proj, "row", OFF_U), (d_skip, "vec", 0)],
                                   [(COL, MXU_DTYPE, "row"), (COL, F32, "row"), (COL, F32, "acc")],
                                   rows=t, cw=COL, ncb=4, tr=1024, name="ssm_gelu_bwd")
    duview, dmats = _ssm_backward(dy_ssm.reshape(nc, SSM_CHUNK * D_SSM), du_skip.reshape(nc, SSM_CHUNK * D_SSM),
                                  uview, states, mats_mx)
    du = duview.reshape(t, D_SSM)
    g_lre, g_lim, g_bre, g_bim, g_cre, g_cim, g_lstep = mats_vjp(dmats)

    def b_hattn(dh, a, z):
        return dh * _silu(z), dh * a * _silu_grad(z)

    d_attn, dz_attn = _ew(b_hattn, [(dh_attn, "row", 0), (attn, "row", 0), (proj, "row", OFF_ZA)],
                          [(COL, F32, "row"), (COL, MXU_DTYPE, "row")], rows=t, cw=COL, ncb=4, tr=1024, name="attn_gate_bwd")
    dq, dka, dkb, dva, dvb, dbias, dsink = _attn_backward(proj, bias, sinks, d_attn)
    dk = _shift_blocks(dka, dkb).astype(MXU_DTYPE)
    dv = _shift_blocks(dva, dvb).astype(MXU_DTYPE)
    (g_table,) = bias_vjp(dbias)
    g_sinks = dsink[:, 0, 0]

    dproj = jnp.concatenate([du, dz_ssm, dq, dk, dv, dz_attn, dgl_s, dgl_a], axis=1)
    gw_in = _mm(x_mx, dproj, ta=True, out_dtype=WIRE_DTYPE, name="grad_w_in", tm=2048, tn=512, tk=1024)
    grad_x = _mm(dproj, win, tb=True, add=dr, add_scale=DEEPNORM_ALPHA, name="grad_x", tm=1024, tn=1024, tk=2176)

    def scatter_cols(g, cols):
        return g.reshape(g.shape[0], N_DEV, cols).transpose(1, 0, 2)

    parts_in = _all_to_all(scatter_cols(gw_in, col_in), name="scatter_g_in")
    three_g = jnp.concatenate([gw_glu, gw_bs, gw_ba], axis=0)
    parts_three = _all_to_all(scatter_cols(three_g, col_br), name="scatter_g_1024")
    parts_out = _all_to_all(gw_out.reshape(N_DEV, row_out, D_MODEL), name="scatter_g_out")

    o_in = _adamw_reduce(parts_in, w_in[0], m_w_in[0], v_w_in[0], name="adamw_w_in", tr=128)
    three_w = jnp.concatenate([w_glu[0], w_branch_ssm[0], w_branch_attn[0]], axis=0)
    three_m = jnp.concatenate([m_w_glu[0], m_w_branch_ssm[0], m_w_branch_attn[0]], axis=0)
    three_v = jnp.concatenate([v_w_glu[0], v_w_branch_ssm[0], v_w_branch_attn[0]], axis=0)
    o_three = _adamw_reduce(parts_three, three_w, three_m, three_v, name="adamw_w_1024", tr=512)
    o_out = _adamw_reduce(parts_out, w_out[0], m_w_out[0], v_w_out[0], name="adamw_w_out", tr=128)

    small_w = [ssm_lambda_re, ssm_lambda_im, ssm_b_re, ssm_b_im, ssm_c_re, ssm_c_im, ssm_d, ssm_log_step,
               attn_sinks, rel_bias_table, ln_gain, ln_bias]
    small_m = [m_ssm_lambda_re, m_ssm_lambda_im, m_ssm_b_re, m_ssm_b_im, m_ssm_c_re, m_ssm_c_im, m_ssm_d,
               m_ssm_log_step, m_attn_sinks, m_rel_bias_table, m_ln_gain, m_ln_bias]
    small_v = [v_ssm_lambda_re, v_ssm_lambda_im, v_ssm_b_re, v_ssm_b_im, v_ssm_c_re, v_ssm_c_im, v_ssm_d,
               v_ssm_log_step, v_attn_sinks, v_rel_bias_table, v_ln_gain, v_ln_bias]
    small_g = [g_lre, g_lim, g_bre, g_bim, g_cre, g_cim, g_ssm_d, g_lstep, g_sinks, g_table, g_ln_gain, g_ln_bias]
    parts_small = _all_gather(_pack(small_g), name="gather_g_small")
    o_small = _adamw_reduce(parts_small, _pack(small_w), _pack(small_m), _pack(small_v), name="adamw_small", tr=2160)
    sg, sd, sm, sv = [_unpack(o, small_w) for o in o_small]

    loss = lax.psum(loss_part[0, 0], MESH_AXES)

    def big(o, idx):
        g_in_, g_three_, g_out_ = o_in[idx], o_three[idx], o_out[idx]
        return {"w_in": g_in_[None], "w_glu": g_three_[None, :D_SSM], "w_branch_ssm": g_three_[None, D_SSM:2 * D_SSM],
                "w_branch_attn": g_three_[None, 2 * D_SSM:], "w_out": g_out_[None]}

    order = ["w_in", "ssm_lambda_re", "ssm_lambda_im", "ssm_b_re", "ssm_b_im", "ssm_c_re", "ssm_c_im", "ssm_d",
             "ssm_log_step", "w_glu", "attn_sinks", "rel_bias_table", "w_branch_ssm", "w_branch_attn", "w_out",
             "ln_gain", "ln_bias"]
    outs = [loss, grad_x[None]]
    for idx, small in enumerate([sg, sd, sm, sv]):
        table = big(None, idx)
        table.update(dict(zip(SMALL, small)))
        outs += [table[n] for n in order]
    return tuple(outs)
```

```python
import functools
import math

import jax
import jax.numpy as jnp
from jax import lax
from jax.experimental import pallas as pl
from jax.experimental.pallas import tpu as pltpu

F32 = jnp.float32
MXU_DTYPE = jnp.bfloat16
WIRE_DTYPE = jnp.bfloat16

D_MODEL = 2048
D_SSM = 1024
SSM_GROUP = 16
N_GROUPS = 64
SSM_STATE = 64
N_Q_HEADS = 16
N_KV_HEADS = 4
Q_PER_KV = 4
HEAD_DIM = 64
D_ATTN = 1024
D_KV = 256
WINDOW = 128
BLOCK = 128
N_BUCKETS = 32
MAX_DISTANCE = 128
D_IN = 8704
DEEPNORM_ALPHA = 2.0 ** 0.25
LN_EPS = 1e-5
NEG_INF = -1e30
ATTN_SCALE = HEAD_DIM ** -0.5

ADAM_LR = 0.001
ADAM_B1 = 0.9
ADAM_B2 = 0.999
ADAM_EPS = 1e-08
ADAM_WD = 0.01
ADAM_STEP = 10

N_DEV = 8
SSM_CHUNK = 16
GROUP_BLOCK = 8
N_GB = N_GROUPS // GROUP_BLOCK
GB_LANES = GROUP_BLOCK * SSM_GROUP
GB_STATE = GROUP_BLOCK * SSM_STATE
COL = 256
OFF_U, OFF_ZS, OFF_Q, OFF_K, OFF_V, OFF_ZA, OFF_GS, OFF_GA = 0, 4, 8, 12, 13, 14, 18, 26

VMEM_CAP = 56 * 1024 * 1024
MESH_AXES = ("x", "y", "c")


def _vmem_limit(block_bytes):
    return int(min(max(3 * block_bytes, 16 * 1024 * 1024), VMEM_CAP))


def _nbytes(shape, dtype):
    return math.prod(shape) * jnp.dtype(dtype).itemsize


def _tile(n, pref):
    if n <= pref:
        return n
    t = (pref // 128) * 128
    while t >= 128:
        if n % t == 0:
            return t
        t -= 128
    return n


def _mm(a, b, *, name, ta=False, tb=False, out_dtype=F32, tm=1024, tn=512, tk=512, add=None, add_scale=1.0):
    squeeze = a.ndim == 2
    if squeeze:
        a, b = a[None], b[None]
        if add is not None:
            add = add[None]
    nb = a.shape[0]
    m, k = (a.shape[2], a.shape[1]) if ta else (a.shape[1], a.shape[2])
    n = b.shape[1] if tb else b.shape[2]
    tm, tn, tk = _tile(m, tm), _tile(n, tn), _tile(k, tk)
    nk = k // tk
    dn = (((0 if ta else 1,), (1 if tb else 0,)), ((), ()))

    a_spec = (pl.BlockSpec((None, tk, tm), lambda g, i, j, kk: (g, kk, i)) if ta
              else pl.BlockSpec((None, tm, tk), lambda g, i, j, kk: (g, i, kk)))
    b_spec = (pl.BlockSpec((None, tn, tk), lambda g, i, j, kk: (g, j, kk)) if tb
              else pl.BlockSpec((None, tk, tn), lambda g, i, j, kk: (g, kk, j)))
    o_spec = pl.BlockSpec((None, tm, tn), lambda g, i, j, kk: (g, i, j))
    in_specs = [a_spec, b_spec]
    operands = [a, b]
    if add is not None:
        in_specs.append(o_spec)
        operands.append(add)

    def body(*refs):
        a_ref, b_ref = refs[0], refs[1]
        add_ref = refs[2] if add is not None else None
        o_ref = refs[3] if add is not None else refs[2]
        acc_ref = refs[-1]
        kk = pl.program_id(3)
        part = lax.dot_general(a_ref[...].astype(MXU_DTYPE), b_ref[...].astype(MXU_DTYPE), dn,
                               preferred_element_type=F32)

        @pl.when(kk == 0)
        def _():
            acc_ref[...] = part

        @pl.when(kk > 0)
        def _():
            acc_ref[...] += part

        @pl.when(kk == nk - 1)
        def _():
            r = acc_ref[...]
            if add_ref is not None:
                r = r + add_scale * add_ref[...]
            o_ref[...] = r.astype(out_dtype)

    blocks = (_nbytes((tm, tk), a.dtype) + _nbytes((tk, tn), b.dtype) + _nbytes((tm, tn), out_dtype)
              + (_nbytes((tm, tn), F32) if add is not None else 0))
    out = pl.pallas_call(
        body,
        name=name,
        grid=(nb, m // tm, n // tn, nk),
        in_specs=in_specs,
        out_specs=o_spec,
        out_shape=jax.ShapeDtypeStruct((nb, m, n), out_dtype),
        scratch_shapes=[pltpu.VMEM((tm, tn), F32)],
        compiler_params=pltpu.CompilerParams(
            dimension_semantics=("parallel", "parallel", "parallel", "arbitrary"),
            vmem_limit_bytes=_vmem_limit(2 * blocks + 2 * _nbytes((tm, tn), F32))),
    )(*operands)
    return out[0] if squeeze else out


def _ew(fn, ins, outs, *, rows, cw, ncb, tr, name):
    tr = min(tr, rows)
    n_in = len(ins)

    def row_map(off):
        return lambda j, i: (i, off + j)

    def vec_map(off):
        return lambda j, i: (0, off + j)

    in_specs = []
    for arr, kind, off in ins:
        if kind == "row":
            in_specs.append(pl.BlockSpec((tr, cw), row_map(off)))
        else:
            in_specs.append(pl.BlockSpec((1, cw), vec_map(off)))
    out_specs, out_shapes = [], []
    for bw, dt, kind in outs:
        if kind == "row":
            out_specs.append(pl.BlockSpec((tr, bw), row_map(0)))
            out_shapes.append(jax.ShapeDtypeStruct((rows, ncb * bw), dt))
        else:
            out_specs.append(pl.BlockSpec((1, bw), vec_map(0)))
            out_shapes.append(jax.ShapeDtypeStruct((1, ncb * bw), F32))

    def body(*refs):
        i = pl.program_id(1)
        vals = fn(*[r[...] for r in refs[:n_in]])
        for r, (bw, dt, kind), v in zip(refs[n_in:], outs, vals):
            if kind == "row":
                r[...] = v.astype(dt)
            else:
                @pl.when(i == 0)
                def _(r=r):
                    r[...] = jnp.zeros_like(r)

                r[...] += v

    blocks = sum(_nbytes((tr, cw), a.dtype) for a, kind, _ in ins if kind == "row")
    blocks += sum(_nbytes((tr, bw), dt) for bw, dt, kind in outs if kind == "row")
    res = pl.pallas_call(
        body,
        name=name,
        grid=(ncb, rows // tr),
        in_specs=in_specs,
        out_specs=out_specs,
        out_shape=out_shapes,
        compiler_params=pltpu.CompilerParams(
            dimension_semantics=("parallel", "arbitrary"),
            vmem_limit_bytes=_vmem_limit(4 * blocks)),
    )(*[a for a, _, _ in ins])
    return res


def _sigmoid(x):
    return 1.0 / (1.0 + jnp.exp(-x))


INV_SQRT2 = 0.7071067811865476
INV_SQRT_2PI = 0.3989422804014327


def _gelu(x):
    return 0.5 * x * (1.0 + lax.erf(x * INV_SQRT2))


def _gelu_grad(x):
    return 0.5 * (1.0 + lax.erf(x * INV_SQRT2)) + x * INV_SQRT_2PI * jnp.exp(-0.5 * x * x)


def _silu(x):
    return x * _sigmoid(x)


def _silu_grad(x):
    s = _sigmoid(x)
    return s * (1.0 + x * (1.0 - s))


def _ssm_matrices(lam_re, lam_im, b_re, b_im, c_re, c_im, log_step):
    hi = lax.Precision.HIGHEST
    L = SSM_CHUNK
    step = jnp.exp(log_step)[:, None]
    ea, eb = lam_re * step, lam_im * step
    mag = jnp.exp(ea)
    lbr, lbi = mag * jnp.cos(eb), mag * jnp.sin(eb)
    den = lam_re * lam_re + lam_im * lam_im
    nr, ni = lbr - 1.0, lbi
    cr = (nr * lam_re + ni * lam_im) / den
    ci = (ni * lam_re - nr * lam_im) / den
    bbr = cr[..., None] * b_re - ci[..., None] * b_im
    bbi = cr[..., None] * b_im + ci[..., None] * b_re
    taus = jnp.arange(L + 1, dtype=F32)[:, None, None]
    pmag = jnp.exp(taus * ea[None])
    pwr, pwi = pmag * jnp.cos(taus * eb[None]), pmag * jnp.sin(taus * eb[None])
    mr = c_re[None] * pwr[:L, :, None, :] - c_im[None] * pwi[:L, :, None, :]
    mi = c_re[None] * pwi[:L, :, None, :] + c_im[None] * pwr[:L, :, None, :]
    kk = (jnp.einsum("tghp,gpk->tghk", mr, bbr, precision=hi)
          - jnp.einsum("tghp,gpk->tghk", mi, bbi, precision=hi))
    taps = jnp.transpose(kk.reshape(L, N_GB, GROUP_BLOCK, SSM_GROUP, SSM_GROUP), (1, 0, 2, 4, 3))
    taps = taps.reshape(N_GB, L, GB_LANES, SSM_GROUP)
    rev_r, rev_i = pwr[L - 1 - jnp.arange(L)], pwi[L - 1 - jnp.arange(L)]
    wer = rev_r[..., None] * bbr[None] - rev_i[..., None] * bbi[None]
    wei = rev_r[..., None] * bbi[None] + rev_i[..., None] * bbr[None]

    def rows_in(w):
        w = jnp.transpose(w.reshape(L, N_GB, GROUP_BLOCK, SSM_STATE, SSM_GROUP), (1, 0, 2, 4, 3))
        return w.reshape(N_GB, L * GB_LANES, SSM_STATE)

    wend = jnp.concatenate([rows_in(wer), rows_in(wei)], axis=2)
    m1r = c_re[None] * pwr[1:, :, None, :] - c_im[None] * pwi[1:, :, None, :]
    m1i = c_re[None] * pwi[1:, :, None, :] + c_im[None] * pwr[1:, :, None, :]

    def rows_out(m):
        m = jnp.transpose(m.reshape(L, N_GB, GROUP_BLOCK, SSM_GROUP, SSM_STATE), (1, 2, 4, 0, 3))
        return m.reshape(N_GB, GB_STATE, L * SSM_GROUP)

    wout = jnp.concatenate([rows_out(m1r), rows_out(-m1i)], axis=1)
    acat = jnp.concatenate([pwr[L].reshape(N_GB, 1, GB_STATE), pwi[L].reshape(N_GB, 1, GB_STATE)], axis=2)
    return taps, wend, wout, acat


def _ssm_expand(taps, wend, wout):
    L = SSM_CHUNK
    eye = jnp.eye(GROUP_BLOCK, dtype=F32)
    t5 = taps.reshape(N_GB, L, GROUP_BLOCK, SSM_GROUP, 1, SSM_GROUP)
    dblk = (t5 * eye[None, None, :, None, :, None]).reshape(N_GB, L, GB_LANES, GB_LANES)

    def cols_in(w):
        w = w.reshape(N_GB, L, GROUP_BLOCK, SSM_GROUP, 1, SSM_STATE)
        return (w * eye[None, None, :, None, :, None]).reshape(N_GB, L * GB_LANES, GB_STATE)

    wend_b = jnp.concatenate([cols_in(wend[:, :, :SSM_STATE]), cols_in(wend[:, :, SSM_STATE:])], axis=2)

    def cols_out(m):
        m = m.reshape(N_GB, GROUP_BLOCK, SSM_STATE, L, 1, SSM_GROUP)
        return (m * eye[None, :, None, None, :, None]).reshape(N_GB, GB_STATE, L * GB_LANES)

    wout_b = jnp.concatenate([cols_out(wout[:, :GB_STATE]), cols_out(wout[:, GB_STATE:])], axis=1)
    return dblk, wend_b.astype(MXU_DTYPE), wout_b.astype(MXU_DTYPE)


def _dot(a, b):
    return jnp.dot(a.astype(MXU_DTYPE), b.astype(MXU_DTYPE), preferred_element_type=F32)


def _dot_nt(a, b):
    return lax.dot_general(a.astype(MXU_DTYPE), b.astype(MXU_DTYPE), (((1,), (1,)), ((), ())),
                           preferred_element_type=F32)


def _dot_tn(a, b):
    return lax.dot_general(a.astype(MXU_DTYPE), b.astype(MXU_DTYPE), (((0,), (0,)), ((), ())),
                           preferred_element_type=F32)


STATE_W = 2 * GB_STATE


def _chunk_scan(e, acat):
    nc = e.shape[0]
    spec = pl.BlockSpec((nc, STATE_W), lambda b: (0, b))
    aspec = pl.BlockSpec((None, 1, STATE_W), lambda b: (b, 0, 0))

    def body(e_ref, a_ref, s_ref):
        a_r, a_i = a_ref[:, :GB_STATE], a_ref[:, GB_STATE:]

        def step(c, carry):
            s_r, s_i = carry
            s_ref[pl.ds(c, 1), :GB_STATE] = s_r
            s_ref[pl.ds(c, 1), GB_STATE:] = s_i
            e_r = e_ref[pl.ds(c, 1), :GB_STATE]
            e_i = e_ref[pl.ds(c, 1), GB_STATE:]
            return (a_r * s_r - a_i * s_i + e_r, a_r * s_i + a_i * s_r + e_i)

        zero = jnp.zeros((1, GB_STATE), F32)
        lax.fori_loop(0, nc, step, (zero, zero))

    return pl.pallas_call(
        body, name="ssm_chunk_scan", grid=(N_GB,),
        in_specs=[spec, aspec], out_specs=spec,
        out_shape=jax.ShapeDtypeStruct(e.shape, F32),
        compiler_params=pltpu.CompilerParams(dimension_semantics=("parallel",)),
    )(e, acat)


def _chunk_scan_bwd(ds, s, acat):
    nc = ds.shape[0]
    spec = pl.BlockSpec((nc, STATE_W), lambda b: (0, b))
    aspec = pl.BlockSpec((None, 1, STATE_W), lambda b: (b, 0, 0))

    def body(ds_ref, s_ref, a_ref, ge_ref, da_ref):
        a_r, a_i = a_ref[:, :GB_STATE], a_ref[:, GB_STATE:]

        def step(t, carry):
            g_r, g_i, d_r, d_i = carry
            c = nc - 1 - t
            ge_ref[pl.ds(c, 1), :GB_STATE] = g_r
            ge_ref[pl.ds(c, 1), GB_STATE:] = g_i
            s_r = s_ref[pl.ds(c, 1), :GB_STATE]
            s_i = s_ref[pl.ds(c, 1), GB_STATE:]
            d_r = d_r + g_r * s_r + g_i * s_i
            d_i = d_i + g_i * s_r - g_r * s_i
            n_r = ds_ref[pl.ds(c, 1), :GB_STATE] + a_r * g_r + a_i * g_i
            n_i = ds_ref[pl.ds(c, 1), GB_STATE:] + a_r * g_i - a_i * g_r
            return (n_r, n_i, d_r, d_i)

        zero = jnp.zeros((1, GB_STATE), F32)
        _, _, d_r, d_i = lax.fori_loop(0, nc, step, (zero, zero, zero, zero))
        da_ref[:, :GB_STATE] = d_r
        da_ref[:, GB_STATE:] = d_i

    return pl.pallas_call(
        body, name="ssm_chunk_scan_bwd", grid=(N_GB,),
        in_specs=[spec, spec, aspec], out_specs=[spec, aspec],
        out_shape=[jax.ShapeDtypeStruct(ds.shape, F32), jax.ShapeDtypeStruct(acat.shape, F32)],
        compiler_params=pltpu.CompilerParams(dimension_semantics=("parallel",)),
    )(ds, s, acat)


def _step_rows(ref, j, nc):
    return ref[pl.ds(j, nc, stride=SSM_CHUNK), :]


def _lane_group(shape, axis, shift):
    return (lax.broadcasted_iota(jnp.int32, shape, axis) >> shift) & (GROUP_BLOCK - 1)


def _fold_lanes(z, widths):
    for w in widths:
        z = z + pltpu.roll(z, w, 1)
    return z


def _ssm_forward(proj, mats):
    dblk, wend, wout, acat = mats
    t = proj.shape[0]
    nc = t // SSM_CHUNK
    L = SSM_CHUNK
    lanes = pl.BlockSpec((t, GB_LANES), lambda b: (0, b))
    state = pl.BlockSpec((nc, STATE_W), lambda b: (0, b))

    def body_end(u_ref, w_ref, e_ref):
        x = jnp.concatenate([_step_rows(u_ref, j, nc).astype(MXU_DTYPE) for j in range(L)], axis=1)
        e_ref[...] = jnp.dot(x, w_ref[...], preferred_element_type=F32)

    e = pl.pallas_call(
        body_end, name="ssm_chunk_end", grid=(N_GB,),
        in_specs=[lanes, pl.BlockSpec((None, L * GB_LANES, STATE_W), lambda b: (b, 0, 0))],
        out_specs=state, out_shape=jax.ShapeDtypeStruct((nc, N_GB * STATE_W), F32),
        compiler_params=pltpu.CompilerParams(dimension_semantics=("parallel",), vmem_limit_bytes=VMEM_CAP),
    )(proj, wend)
    s = _chunk_scan(e, acat)

    def body_out(u_ref, d_ref, s_ref, w_ref, y_ref):
        xs = [_step_rows(u_ref, j, nc).astype(MXU_DTYPE) for j in range(L)]
        sb = s_ref[...].astype(MXU_DTYPE)
        for tt in range(L):
            xcat = jnp.concatenate(xs[:tt + 1], axis=1)
            taps = jnp.concatenate([d_ref[tt - j] for j in range(tt + 1)], axis=0).astype(MXU_DTYPE)
            y = (jnp.dot(xcat, taps, preferred_element_type=F32)
                 + jnp.dot(sb, w_ref[:, tt * GB_LANES:(tt + 1) * GB_LANES], preferred_element_type=F32))
            y_ref[pl.ds(tt, nc, stride=L), :] = y

    y = pl.pallas_call(
        body_out, name="ssm_chunk_out", grid=(N_GB,),
        in_specs=[lanes, pl.BlockSpec((None, L, GB_LANES, GB_LANES), lambda b: (b, 0, 0, 0)), state,
                  pl.BlockSpec((None, STATE_W, L * GB_LANES), lambda b: (b, 0, 0))],
        out_specs=lanes, out_shape=jax.ShapeDtypeStruct((t, D_SSM), F32),
        compiler_params=pltpu.CompilerParams(dimension_semantics=("parallel",), vmem_limit_bytes=VMEM_CAP),
    )(proj, dblk, s, wout)
    return y, s


def _ssm_backward(dy, proj, s, mats):
    dblk, wend, wout, acat = mats
    t = proj.shape[0]
    nc = t // SSM_CHUNK
    L = SSM_CHUNK
    lanes = pl.BlockSpec((t, GB_LANES), lambda b: (0, b))
    state = pl.BlockSpec((nc, STATE_W), lambda b: (0, b))
    taps_spec = pl.BlockSpec((None, L, GB_LANES, GB_LANES), lambda b: (b, 0, 0, 0))

    def body_state(dy_ref, w_ref, ds_ref):
        dyc = jnp.concatenate([_step_rows(dy_ref, tt, nc).astype(MXU_DTYPE) for tt in range(L)], axis=1)
        ds_ref[...] = _dot_nt(dyc, w_ref[...])

    ds = pl.pallas_call(
        body_state, name="ssm_bwd_state", grid=(N_GB,),
        in_specs=[lanes, pl.BlockSpec((None, STATE_W, L * GB_LANES), lambda b: (b, 0, 0))],
        out_specs=state, out_shape=jax.ShapeDtypeStruct((nc, N_GB * STATE_W), F32),
        compiler_params=pltpu.CompilerParams(dimension_semantics=("parallel",), vmem_limit_bytes=VMEM_CAP),
    )(dy, wout)
    ge, dacat = _chunk_scan_bwd(ds, s, acat)

    def body_in(u_ref, dy_ref, d_ref, ge_ref, w_ref, du_ref, dd_ref):
        xs = [_step_rows(u_ref, j, nc).astype(MXU_DTYPE) for j in range(L)]
        dys = [_step_rows(dy_ref, tt, nc).astype(MXU_DTYPE) for tt in range(L)]
        ge = ge_ref[...].astype(MXU_DTYPE)
        for i in range(L):
            dyc = jnp.concatenate(dys[i:], axis=1)
            taps = jnp.concatenate([d_ref[tt - i] for tt in range(i, L)], axis=1).astype(MXU_DTYPE)
            du_ref[pl.ds(i, nc, stride=L), :] = (
                _dot_nt(dyc, taps) + _dot_nt(ge, w_ref[i * GB_LANES:(i + 1) * GB_LANES, :]))
        for j in range(L):
            m = _dot_tn(xs[j], jnp.concatenate(dys[j:], axis=1))
            for tau in range(L - j):
                part = m[:, tau * GB_LANES:(tau + 1) * GB_LANES]
                if j == 0:
                    dd_ref[tau] = part
                else:
                    dd_ref[tau] += part
        same = _lane_group((GB_LANES, GB_LANES), 0, 4) == _lane_group((GB_LANES, GB_LANES), 1, 4)
        for tau in range(L):
            dd_ref[tau] = _fold_lanes(jnp.where(same, dd_ref[tau], 0.0), (64, 32, 16))

    du, ddblk = pl.pallas_call(
        body_in, name="ssm_bwd_in", grid=(N_GB,),
        in_specs=[lanes, lanes, taps_spec, state,
                  pl.BlockSpec((None, L * GB_LANES, STATE_W), lambda b: (b, 0, 0))],
        out_specs=[lanes, taps_spec],
        out_shape=[jax.ShapeDtypeStruct((t, D_SSM), F32), jax.ShapeDtypeStruct(dblk.shape, F32)],
        compiler_params=pltpu.CompilerParams(dimension_semantics=("parallel",), vmem_limit_bytes=VMEM_CAP),
    )(proj, dy, dblk, ge, wend)

    def body_w(u_ref, dy_ref, ge_ref, s_ref, dwe_ref, dwo_ref):
        ge = ge_ref[...].astype(MXU_DTYPE)
        sb = s_ref[...].astype(MXU_DTYPE)
        keep = _lane_group((GB_LANES, STATE_W), 0, 4) == _lane_group((GB_LANES, STATE_W), 1, 6)
        low = lax.broadcasted_iota(jnp.int32, (GB_LANES, 2 * SSM_STATE), 1) < SSM_STATE

        def fold_state(z):
            z = z[:, :GB_STATE // 2] + z[:, GB_STATE // 2:]
            z = z[:, :GB_STATE // 4] + z[:, GB_STATE // 4:]
            return _fold_lanes(z, (SSM_STATE,))

        for j in range(L):
            z = jnp.where(keep, _dot_tn(_step_rows(u_ref, j, nc).astype(MXU_DTYPE), ge), 0.0)
            dwe_ref[j * GB_LANES:(j + 1) * GB_LANES, :] = jnp.where(
                low, fold_state(z[:, :GB_STATE]), fold_state(z[:, GB_STATE:]))
        own = _lane_group((STATE_W, GB_LANES), 0, 6) == _lane_group((STATE_W, GB_LANES), 1, 4)
        chunk = lax.broadcasted_iota(jnp.int32, (STATE_W, GB_LANES), 1) >> 4
        for half in range(L // GROUP_BLOCK):
            acc = jnp.zeros((STATE_W, GB_LANES), F32)
            for k in range(GROUP_BLOCK):
                tt = half * GROUP_BLOCK + k
                z = jnp.where(own, _dot_tn(sb, _step_rows(dy_ref, tt, nc).astype(MXU_DTYPE)), 0.0)
                acc = acc + jnp.where(chunk == k, _fold_lanes(z, (64, 32, 16)), 0.0)
            dwo_ref[:, half * GB_LANES:(half + 1) * GB_LANES] = acc

    dwend, dwout = pl.pallas_call(
        body_w, name="ssm_bwd_w", grid=(N_GB,),
        in_specs=[lanes, lanes, state, state],
        out_specs=[pl.BlockSpec((None, L * GB_LANES, 2 * SSM_STATE), lambda b: (b, 0, 0)),
                   pl.BlockSpec((None, STATE_W, L * SSM_GROUP), lambda b: (b, 0, 0))],
        out_shape=[jax.ShapeDtypeStruct((N_GB, L * GB_LANES, 2 * SSM_STATE), F32),
                   jax.ShapeDtypeStruct((N_GB, STATE_W, L * SSM_GROUP), F32)],
        compiler_params=pltpu.CompilerParams(dimension_semantics=("parallel",), vmem_limit_bytes=VMEM_CAP),
    )(proj, dy, ge, s)
    return du, (ddblk[:, :, :, :SSM_GROUP], dwend, dwout, dacat)


def _t5_bucket(dist):
    max_exact = N_BUCKETS // 2
    is_small = dist < max_exact
    d = jnp.maximum(dist, 1).astype(F32)
    large = max_exact + (jnp.log(d / max_exact) / math.log(MAX_DISTANCE / max_exact)
                         * (N_BUCKETS - max_exact)).astype(jnp.int32)
    large = jnp.minimum(large, N_BUCKETS - 1)
    return jnp.where(is_small, dist, large)


def _band_bucket_and_mask():
    i = jnp.arange(BLOCK)[:, None]
    j = jnp.arange(2 * BLOCK)[None, :]
    dist = BLOCK + i - j
    ok = (dist >= 0) & (dist < WINDOW)
    return _t5_bucket(jnp.clip(dist, 0, None)), ok


def _band_bias(rel_bias_table):
    bucket, ok = _band_bucket_and_mask()
    onehot = ((bucket[:, :, None] == jnp.arange(N_BUCKETS)[None, None, :]) & ok[:, :, None]).astype(F32)
    bias = jnp.einsum("qsb,bh->hqs", onehot, rel_bias_table, precision=lax.Precision.HIGHEST)
    return bias + jnp.where(ok, 0.0, NEG_INF)[None]


KV_PAIR = 2
HEADS_PER_STEP = KV_PAIR * Q_PER_KV
Q_LANES = HEADS_PER_STEP * HEAD_DIM
SLAB = 2 * HEAD_DIM
Q_COL0 = OFF_Q * COL // Q_LANES
K_COL0 = OFF_K * COL // SLAB
V_COL0 = OFF_V * COL // SLAB


def _attn_specs():
    q_spec = pl.BlockSpec((BLOCK, Q_LANES), lambda m, n: (n, Q_COL0 + m))
    k_prev = pl.BlockSpec((BLOCK, SLAB), lambda m, n: (jnp.maximum(n - 1, 0), K_COL0 + m))
    k_cur = pl.BlockSpec((BLOCK, SLAB), lambda m, n: (n, K_COL0 + m))
    v_prev = pl.BlockSpec((BLOCK, SLAB), lambda m, n: (jnp.maximum(n - 1, 0), V_COL0 + m))
    v_cur = pl.BlockSpec((BLOCK, SLAB), lambda m, n: (n, V_COL0 + m))
    bias_spec = pl.BlockSpec((HEADS_PER_STEP, BLOCK, 2 * BLOCK), lambda m, n: (m, 0, 0))
    sink_spec = pl.BlockSpec(memory_space=pltpu.SMEM)
    wide = pl.BlockSpec((BLOCK, Q_LANES), lambda m, n: (n, m))
    pair = pl.BlockSpec((BLOCK, SLAB), lambda m, n: (n, m))
    return [sink_spec, q_spec, k_prev, k_cur, v_prev, v_cur, bias_spec], wide, pair


def _low_lanes(shape):
    return lax.broadcasted_iota(jnp.int32, shape, 1) < HEAD_DIM


def _band_halves(prev_ref, cur_ref):
    kb = jnp.concatenate([prev_ref[...], cur_ref[...]], axis=0)
    sw = pltpu.roll(kb, HEAD_DIM, 1)
    lo = _low_lanes(kb.shape)
    zero = jnp.zeros_like(kb)
    first = (jnp.where(lo, kb, zero).astype(MXU_DTYPE), jnp.where(lo, zero, sw).astype(MXU_DTYPE))
    second = (jnp.where(lo, sw, zero).astype(MXU_DTYPE), jnp.where(lo, zero, kb).astype(MXU_DTYPE))
    return first, second


def _fold_pair(acc):
    f = [x + pltpu.roll(x, HEAD_DIM, 1) for x in acc]
    return jnp.where(_low_lanes(f[0].shape), f[0], f[1])


def _attn_probs(q, kb, bias, edge, sink):
    s = _dot_nt(q, kb) * ATTN_SCALE + bias + edge
    m = jnp.maximum(jnp.max(s, axis=1, keepdims=True), sink)
    e = jnp.exp(s - m)
    es = jnp.exp(sink - m)
    inv = 1.0 / (jnp.sum(e, axis=1, keepdims=True) + es)
    return e * inv, es * inv


def _edge_mask(n):
    col = lax.broadcasted_iota(jnp.int32, (BLOCK, 2 * BLOCK), 1)
    return jnp.where(jnp.logical_or(n > 0, col >= BLOCK), 0.0, NEG_INF)


def _attn_forward(proj, bias, sinks):
    t = proj.shape[0]
    in_specs, wide, _ = _attn_specs()

    def body(sink_ref, q_ref, kp_ref, kc_ref, vp_ref, vc_ref, bias_ref, o_ref):
        m, n = pl.program_id(0), pl.program_id(1)
        keys = _band_halves(kp_ref, kc_ref)
        vals = _band_halves(vp_ref, vc_ref)
        edge = _edge_mask(n)
        for s in range(HEADS_PER_STEP // 2):
            a = s // (Q_PER_KV // 2)
            q = q_ref[:, s * SLAB:(s + 1) * SLAB].astype(MXU_DTYPE)
            out = None
            for e in range(2):
                h = 2 * s + e
                p, _ = _attn_probs(q, keys[a][e], bias_ref[h], edge, sink_ref[m * HEADS_PER_STEP + h])
                o = _dot(p, vals[a][e])
                out = o if out is None else out + o
            o_ref[:, s * SLAB:(s + 1) * SLAB] = out

    return pl.pallas_call(
        body, name="attn_fwd", grid=(N_KV_HEADS // KV_PAIR, t // BLOCK),
        in_specs=in_specs, out_specs=wide,
        out_shape=jax.ShapeDtypeStruct((t, D_ATTN), F32),
        compiler_params=pltpu.CompilerParams(dimension_semantics=("parallel", "arbitrary")),
    )(sinks, proj, proj, proj, proj, proj, bias)


def _attn_backward(proj, bias, sinks, d_attn):
    t = proj.shape[0]
    in_specs, wide, pair = _attn_specs()
    bias_spec = in_specs[-1]
    sink_out = pl.BlockSpec((HEADS_PER_STEP, 8, 128), lambda m, n: (m, 0, 0))

    def body(sink_ref, q_ref, kp_ref, kc_ref, vp_ref, vc_ref, bias_ref, do_ref,
             dq_ref, dka_ref, dkb_ref, dva_ref, dvb_ref, dbias_ref, dsink_ref):
        m, n = pl.program_id(0), pl.program_id(1)

        @pl.when(n == 0)
        def _():
            dbias_ref[...] = jnp.zeros_like(dbias_ref)
            dsink_ref[...] = jnp.zeros_like(dsink_ref)

        keys = _band_halves(kp_ref, kc_ref)
        vals = _band_halves(vp_ref, vc_ref)
        edge = _edge_mask(n)
        lo = _low_lanes((2 * BLOCK, SLAB))
        dk = [jnp.zeros((2 * BLOCK, SLAB), F32) for _ in range(KV_PAIR)]
        dv = [jnp.zeros((2 * BLOCK, SLAB), F32) for _ in range(KV_PAIR)]
        for s in range(HEADS_PER_STEP // 2):
            a = s // (Q_PER_KV // 2)
            q = q_ref[:, s * SLAB:(s + 1) * SLAB].astype(MXU_DTYPE)
            do = do_ref[:, s * SLAB:(s + 1) * SLAB].astype(MXU_DTYPE)
            dq, rk, rv = None, [], []
            for e in range(2):
                h = 2 * s + e
                p, ps = _attn_probs(q, keys[a][e], bias_ref[h], edge, sink_ref[m * HEADS_PER_STEP + h])
                dp = _dot_nt(do, vals[a][e])
                delta = jnp.sum(p * dp, axis=1, keepdims=True)
                ds = p * (dp - delta)
                dbias_ref[h] += ds
                dsink_ref[h] += jnp.broadcast_to(jnp.sum(-ps * delta, axis=0, keepdims=True), (8, 128))
                dqe = _dot(ds, keys[a][e])
                dq = dqe if dq is None else dq + dqe
                rk.append(_dot_tn(ds, q))
                rv.append(_dot_tn(p, do))
            dq_ref[:, s * SLAB:(s + 1) * SLAB] = (dq * ATTN_SCALE).astype(dq_ref.dtype)
            dk[a] = dk[a] + jnp.where(lo, rk[0], rk[1])
            dv[a] = dv[a] + jnp.where(lo, rv[0], rv[1])
        dkp = _fold_pair(dk) * ATTN_SCALE
        dvp = _fold_pair(dv)
        dkb_ref[...] = dkp[:BLOCK]
        dka_ref[...] = dkp[BLOCK:]
        dvb_ref[...] = dvp[:BLOCK]
        dva_ref[...] = dvp[BLOCK:]

    kv_shape = jax.ShapeDtypeStruct((t, D_KV), F32)
    return pl.pallas_call(
        body, name="attn_bwd", grid=(N_KV_HEADS // KV_PAIR, t // BLOCK),
        in_specs=in_specs + [wide],
        out_specs=[wide, pair, pair, pair, pair, bias_spec, sink_out],
        out_shape=[jax.ShapeDtypeStruct((t, D_ATTN), MXU_DTYPE), kv_shape, kv_shape, kv_shape, kv_shape,
                   jax.ShapeDtypeStruct(bias.shape, F32), jax.ShapeDtypeStruct((N_Q_HEADS, 8, 128), F32)],
        compiler_params=pltpu.CompilerParams(dimension_semantics=("parallel", "arbitrary")),
    )(sinks, proj, proj, proj, proj, proj, bias, d_attn)


def _shift_blocks(cur, prev):
    return cur + jnp.concatenate([prev[BLOCK:], jnp.zeros_like(prev[:BLOCK])], axis=0)


def _mesh_pos():
    return lax.axis_index("x"), lax.axis_index("y"), lax.axis_index("c")


def _all_gather(x, *, name):
    def body(x_ref, out_ref, send_sems, recv_sems, local_sem):
        x, y, c = _mesh_pos()
        me, sibling = (x, y, c), (x, y, 1 - c)
        chips = [(1 - x, y), (x, 1 - y), (1 - x, 1 - y)]

        def slot(px, py, pc):
            return out_ref.at[4 * px + 2 * py + pc]

        def copy(k, block, to, src=None):
            return pltpu.make_async_remote_copy(
                src_ref=slot(*block) if src is None else src, dst_ref=slot(*block),
                send_sem=send_sems.at[k], recv_sem=recv_sems.at[k],
                device_id=to, device_id_type=pl.DeviceIdType.MESH)

        mine = pltpu.make_async_copy(x_ref, slot(*me), local_sem)
        mine.start()
        first = [copy(0, me, sibling, src=x_ref)]
        first += [copy(1 + j, me, (*chip, c), src=x_ref) for j, chip in enumerate(chips)]
        for cp in first:
            cp.start()
        passed = [copy(4 + j, (*chip, c), sibling) for j, chip in enumerate(chips)]
        for j, chip in enumerate(chips):
            copy(1 + j, (*chip, c), me).wait_recv()
            passed[j].start()
        copy(0, sibling, me).wait_recv()
        for j, chip in enumerate(chips):
            copy(4 + j, (*chip, 1 - c), me).wait_recv()
        for cp in first + passed:
            cp.wait_send()
        mine.wait()

    return pl.pallas_call(
        body, name=name,
        in_specs=[pl.BlockSpec(memory_space=pl.ANY)],
        out_specs=pl.BlockSpec(memory_space=pl.ANY),
        out_shape=jax.ShapeDtypeStruct((N_DEV,) + x.shape, x.dtype),
        scratch_shapes=[pltpu.SemaphoreType.DMA((7,)), pltpu.SemaphoreType.DMA((7,)), pltpu.SemaphoreType.DMA],
    )(x)


def _all_to_all(x, *, name):
    def body(x_ref, out_ref, send_sems, recv_sems, local_sem):
        x, y, c = _mesh_pos()
        me = 4 * x + 2 * y + c
        mine = pltpu.make_async_copy(x_ref.at[me], out_ref.at[me], local_sem)
        mine.start()
        copies = []
        for k in range(1, N_DEV):
            px, py, pc = x ^ (k >> 2), y ^ ((k >> 1) & 1), c ^ (k & 1)
            peer = 4 * px + 2 * py + pc
            copies.append(pltpu.make_async_remote_copy(
                src_ref=x_ref.at[peer], dst_ref=out_ref.at[me],
                send_sem=send_sems.at[k - 1], recv_sem=recv_sems.at[k - 1],
                device_id=(px, py, pc), device_id_type=pl.DeviceIdType.MESH))
        for cp in copies:
            cp.start()
        for k in range(1, N_DEV):
            px, py, pc = x ^ (k >> 2), y ^ ((k >> 1) & 1), c ^ (k & 1)
            peer = 4 * px + 2 * py + pc
            pltpu.make_async_remote_copy(
                src_ref=x_ref.at[peer], dst_ref=out_ref.at[peer],
                send_sem=send_sems.at[k - 1], recv_sem=recv_sems.at[k - 1],
                device_id=(px, py, pc), device_id_type=pl.DeviceIdType.MESH).wait_recv()
        for cp in copies:
            cp.wait_send()
        mine.wait()

    return pl.pallas_call(
        body, name=name,
        in_specs=[pl.BlockSpec(memory_space=pl.ANY)],
        out_specs=pl.BlockSpec(memory_space=pl.ANY),
        out_shape=jax.ShapeDtypeStruct(x.shape, x.dtype),
        scratch_shapes=[pltpu.SemaphoreType.DMA((7,)), pltpu.SemaphoreType.DMA((7,)), pltpu.SemaphoreType.DMA],
    )(x)


def _adamw_math(w, g, m, v):
    m = ADAM_B1 * m + (1.0 - ADAM_B1) * g
    v = ADAM_B2 * v + (1.0 - ADAM_B2) * (g * g)
    m_hat = m / (1.0 - ADAM_B1 ** ADAM_STEP)
    v_hat = v / (1.0 - ADAM_B2 ** ADAM_STEP)
    delta = -ADAM_LR * (m_hat / (jnp.sqrt(v_hat) + ADAM_EPS) + ADAM_WD * w)
    return delta, m, v


def _adamw_reduce(parts, w, m, v, *, name, tr):
    r, c = w.shape
    tr = min(tr, r)
    spec = pl.BlockSpec((tr, c), lambda i: (i, 0))

    def body(p_ref, w_ref, m_ref, v_ref, g_ref, d_ref, nm_ref, nv_ref):
        g = p_ref[0].astype(F32)
        for s in range(1, N_DEV):
            g = g + p_ref[s].astype(F32)
        delta, nm, nv = _adamw_math(w_ref[...], g, m_ref[...], v_ref[...])
        g_ref[...] = g
        d_ref[...] = delta
        nm_ref[...] = nm
        nv_ref[...] = nv

    return pl.pallas_call(
        body, name=name, grid=(r // tr,),
        in_specs=[pl.BlockSpec((N_DEV, tr, c), lambda i: (0, i, 0)), spec, spec, spec],
        out_specs=[spec] * 4,
        out_shape=[jax.ShapeDtypeStruct((r, c), F32)] * 4,
        compiler_params=pltpu.CompilerParams(
            dimension_semantics=("parallel",),
            vmem_limit_bytes=_vmem_limit(2 * 15 * _nbytes((tr, c), F32))),
    )(parts, w, m, v)


SMALL = ["ssm_lambda_re", "ssm_lambda_im", "ssm_b_re", "ssm_b_im", "ssm_c_re", "ssm_c_im", "ssm_d",
         "ssm_log_step", "attn_sinks", "rel_bias_table", "ln_gain", "ln_bias"]


def _pack(arrs):
    flat = jnp.concatenate([a.reshape(-1) for a in arrs])
    pad = (-flat.shape[0]) % 1024
    return jnp.pad(flat, (0, pad)).reshape(-1, 128)


def _unpack(packed, like):
    flat = packed.reshape(-1)
    out, pos = [], 0
    for a in like:
        out.append(flat[pos:pos + a.size].reshape(a.shape))
        pos += a.size
    return out


def kernel(x, w_in, ssm_lambda_re, ssm_lambda_im, ssm_b_re, ssm_b_im, ssm_c_re, ssm_c_im, ssm_d, ssm_log_step, w_glu, attn_sinks, rel_bias_table, w_branch_ssm, w_branch_attn, w_out, ln_gain, ln_bias, loss_target, m_w_in, m_ssm_lambda_re, m_ssm_lambda_im, m_ssm_b_re, m_ssm_b_im, m_ssm_c_re, m_ssm_c_im, m_ssm_d, m_ssm_log_step, m_w_glu, m_attn_sinks, m_rel_bias_table, m_w_branch_ssm, m_w_branch_attn, m_w_out, m_ln_gain, m_ln_bias, v_w_in, v_ssm_lambda_re, v_ssm_lambda_im, v_ssm_b_re, v_ssm_b_im, v_ssm_c_re, v_ssm_c_im, v_ssm_d, v_ssm_log_step, v_w_glu, v_attn_sinks, v_rel_bias_table, v_w_branch_ssm, v_w_branch_attn, v_w_out, v_ln_gain, v_ln_bias):
    t = x.shape[1]
    xs = x[0]
    target = loss_target[0]
    col_in = w_in.shape[2]
    col_br = w_glu.shape[2]
    row_out = w_out.shape[1]

    g_in = _all_gather(w_in[0].astype(MXU_DTYPE), name="gather_w_in")
    three = jnp.concatenate([w_glu[0], w_branch_ssm[0], w_branch_attn[0]], axis=0).astype(MXU_DTYPE)
    g_three = _all_gather(three, name="gather_w_1024")
    g_out = _all_gather(w_out[0].astype(MXU_DTYPE), name="gather_w_out")
    win = g_in.transpose(1, 0, 2).reshape(D_MODEL, D_IN)
    three_full = g_three.transpose(1, 0, 2).reshape(3 * D_SSM, N_DEV * col_br)
    wglu, wbs, wba = three_full[:D_SSM], three_full[D_SSM:2 * D_SSM], three_full[2 * D_SSM:]
    wout = g_out.reshape(D_MODEL, D_MODEL)

    ssm_params = (ssm_lambda_re[0], ssm_lambda_im[0], ssm_b_re[0], ssm_b_im[0], ssm_c_re[0], ssm_c_im[0],
                  ssm_log_step[0])
    mats, mats_vjp = jax.vjp(_ssm_matrices, *ssm_params)
    bias, bias_vjp = jax.vjp(_band_bias, rel_bias_table)
    sinks = attn_sinks[0]
    d_skip = ssm_d

    x_mx = xs.astype(MXU_DTYPE)
    proj = _mm(x_mx, win, name="in_proj", tm=2048, tn=512, tk=2048)
    mats_mx = _ssm_expand(*mats[:3]) + (mats[3],)
    y_conv, states = _ssm_forward(proj, mats_mx)

    def f_gelu(yv, u, d):
        ys = yv + d * u
        return ys, _gelu(ys)

    y_ssm, glu_in = _ew(f_gelu, [(y_conv, "row", 0), (proj, "row", OFF_U), (d_skip, "vec", 0)],
                        [(COL, F32, "row"), (COL, MXU_DTYPE, "row")], rows=t, cw=COL, ncb=4, tr=1024, name="ssm_gelu")
    glu = _mm(glu_in, wglu, name="glu_proj", tm=2048, tn=512, tk=1024)

    def f_hssm(ga, gb, z):
        return (ga * _sigmoid(gb) * _silu(z),)

    (h_ssm,) = _ew(f_hssm, [(glu, "row", 0), (glu, "row", 4), (proj, "row", OFF_ZS)],
                   [(COL, MXU_DTYPE, "row")], rows=t, cw=COL, ncb=4, tr=1024, name="ssm_gate")

    attn = _attn_forward(proj, bias, sinks)

    def f_hattn(a, z):
        return (a * _silu(z),)

    (h_attn,) = _ew(f_hattn, [(attn, "row", 0), (proj, "row", OFF_ZA)], [(COL, MXU_DTYPE, "row")],
                    rows=t, cw=COL, ncb=4, tr=1024, name="attn_gate")
    p_ssm = _mm(h_ssm, wbs, name="branch_ssm", tm=2048, tn=512, tk=1024)
    p_attn = _mm(h_attn, wba, name="branch_attn", tm=2048, tn=512, tk=1024)

    def f_merge(ps, pa, ls, la):
        return (_sigmoid(ls) * ps + _sigmoid(la) * pa,)

    (merged,) = _ew(f_merge, [(p_ssm, "row", 0), (p_attn, "row", 0), (proj, "row", OFF_GS), (proj, "row", OFF_GA)],
                    [(COL, MXU_DTYPE, "row")], rows=t, cw=COL, ncb=8, tr=1024, name="merge")
    out = _mm(merged, wout, name="out_proj", tm=2048, tn=512, tk=2048)

    def f_norm(xv, ov, tg, gain, lbias):
        r = DEEPNORM_ALPHA * xv + ov
        mu = jnp.mean(r, axis=1, keepdims=True)
        cen = r - mu
        var = jnp.mean(cen * cen, axis=1, keepdims=True)
        rstd = lax.rsqrt(var + LN_EPS)
        xhat = cen * rstd
        yv = xhat * gain + lbias
        diff = yv - tg
        row_loss = 0.5 * jnp.mean(diff * diff, axis=1, keepdims=True)
        loss = jnp.broadcast_to(jnp.sum(row_loss, axis=0, keepdims=True), (1, 128))
        dy = diff * (1.0 / D_MODEL)
        dgain = jnp.sum(dy * xhat, axis=0, keepdims=True)
        dbias = jnp.sum(dy, axis=0, keepdims=True)
        dxh = dy * gain
        dr = rstd * (dxh - jnp.mean(dxh, axis=1, keepdims=True) - xhat * jnp.mean(dxh * xhat, axis=1, keepdims=True))
        return dr, loss, dgain, dbias

    dr, loss_part, g_ln_gain, g_ln_bias = _ew(
        f_norm, [(xs, "row", 0), (out, "row", 0), (target, "row", 0), (ln_gain, "vec", 0), (ln_bias, "vec", 0)],
        [(D_MODEL, F32, "row"), (128, F32, "acc"), (D_MODEL, F32, "acc"), (D_MODEL, F32, "acc")],
        rows=t, cw=D_MODEL, ncb=1, tr=256, name="norm_loss")

    gw_out = _mm(merged, dr, ta=True, out_dtype=WIRE_DTYPE, name="grad_w_out", tm=2048, tn=512, tk=1024)
    d_merged = _mm(dr, wout, tb=True, name="d_merged", tm=2048, tn=512, tk=2048)

    def b_merge(dm, ps, pa, ls, la):
        gs, ga = _sigmoid(ls), _sigmoid(la)
        return dm * gs, dm * ga, dm * ps * gs * (1.0 - gs), dm * pa * ga * (1.0 - ga)

    dp_ssm, dp_attn, dgl_s, dgl_a = _ew(
        b_merge, [(d_merged, "row", 0), (p_ssm, "row", 0), (p_attn, "row", 0), (proj, "row", OFF_GS), (proj, "row", OFF_GA)],
        [(COL, MXU_DTYPE, "row")] * 4, rows=t, cw=COL, ncb=8, tr=1024, name="merge_bwd")
    gw_bs = _mm(h_ssm, dp_ssm, ta=True, out_dtype=WIRE_DTYPE, name="grad_w_branch_ssm", tm=1024, tn=512, tk=1024)
    gw_ba = _mm(h_attn, dp_attn, ta=True, out_dtype=WIRE_DTYPE, name="grad_w_branch_attn", tm=1024, tn=512, tk=1024)
    dh_ssm = _mm(dp_ssm, wbs, tb=True, name="d_h_ssm", tm=2048, tn=512, tk=2048)
    dh_attn = _mm(dp_attn, wba, tb=True, name="d_h_attn", tm=2048, tn=512, tk=2048)

    def b_hssm(dh, ga, gb, z):
        sg = _sigmoid(gb)
        dgate = dh * _silu(z)
        return dgate * sg, dgate * ga * sg * (1.0 - sg), dh * ga * sg * _silu_grad(z)

    dglu_a, dglu_b, dz_ssm = _ew(b_hssm, [(dh_ssm, "row", 0), (glu, "row", 0), (glu, "row", 4), (proj, "row", OFF_ZS)],
                                 [(COL, MXU_DTYPE, "row")] * 3, rows=t, cw=COL, ncb=4, tr=1024, name="ssm_gate_bwd")
    dglu = jnp.concatenate([dglu_a, dglu_b], axis=1)
    gw_glu = _mm(glu_in, dglu, ta=True, out_dtype=WIRE_DTYPE, name="grad_w_glu", tm=1024, tn=512, tk=1024)
    dglu_in = _mm(dglu, wglu, tb=True, name="d_glu_in", tm=2048, tn=512, tk=2048)

    def b_gelu(dgi, ys, u):
        dys = dgi * _gelu_grad(ys)
        return dys, jnp.sum(dys * u, axis=0, keepdims=True)

    dy_ssm, g_ssm_d = _ew(b_gelu, [(dglu_in, "row", 0), (y_ssm, "row", 0), (proj, "row", OFF_U)],
                          [(COL, F32, "row"), (COL, F32, "acc")],
                          rows=t, cw=COL, ncb=4, tr=1024, name="ssm_gelu_bwd")
    du_ssm, dmats = _ssm_backward(dy_ssm, proj, states, mats_mx)
    du = (du_ssm + d_skip * dy_ssm).astype(MXU_DTYPE)
    g_lre, g_lim, g_bre, g_bim, g_cre, g_cim, g_lstep = mats_vjp(dmats)

    def b_hattn(dh, a, z):
        return dh * _silu(z), dh * a * _silu_grad(z)

    d_attn, dz_attn = _ew(b_hattn, [(dh_attn, "row", 0), (attn, "row", 0), (proj, "row", OFF_ZA)],
                          [(COL, F32, "row"), (COL, MXU_DTYPE, "row")], rows=t, cw=COL, ncb=4, tr=1024, name="attn_gate_bwd")
    dq, dka, dkb, dva, dvb, dbias, dsink = _attn_backward(proj, bias, sinks, d_attn)
    dk = _shift_blocks(dka, dkb).astype(MXU_DTYPE)
    dv = _shift_blocks(dva, dvb).astype(MXU_DTYPE)
    (g_table,) = bias_vjp(dbias)
    g_sinks = dsink[:, 0, 0]

    dproj = jnp.concatenate([du, dz_ssm, dq, dk, dv, dz_attn, dgl_s, dgl_a], axis=1)
    gw_in = _mm(x_mx, dproj, ta=True, out_dtype=WIRE_DTYPE, name="grad_w_in", tm=2048, tn=512, tk=1024)
    grad_x = _mm(dproj, win, tb=True, add=dr, add_scale=DEEPNORM_ALPHA, name="grad_x", tm=1024, tn=1024, tk=2176)

    def scatter_cols(g, cols):
        return g.reshape(g.shape[0], N_DEV, cols).transpose(1, 0, 2)

    parts_in = _all_to_all(scatter_cols(gw_in, col_in), name="scatter_g_in")
    three_g = jnp.concatenate([gw_glu, gw_bs, gw_ba], axis=0)
    parts_three = _all_to_all(scatter_cols(three_g, col_br), name="scatter_g_1024")
    parts_out = _all_to_all(gw_out.reshape(N_DEV, row_out, D_MODEL), name="scatter_g_out")

    o_in = _adamw_reduce(parts_in, w_in[0], m_w_in[0], v_w_in[0], name="adamw_w_in", tr=128)
    three_w = jnp.concatenate([w_glu[0], w_branch_ssm[0], w_branch_attn[0]], axis=0)
    three_m = jnp.concatenate([m_w_glu[0], m_w_branch_ssm[0], m_w_branch_attn[0]], axis=0)
    three_v = jnp.concatenate([v_w_glu[0], v_w_branch_ssm[0], v_w_branch_attn[0]], axis=0)
    o_three = _adamw_reduce(parts_three, three_w, three_m, three_v, name="adamw_w_1024", tr=512)
    o_out = _adamw_reduce(parts_out, w_out[0], m_w_out[0], v_w_out[0], name="adamw_w_out", tr=128)

    small_w = [ssm_lambda_re, ssm_lambda_im, ssm_b_re, ssm_b_im, ssm_c_re, ssm_c_im, ssm_d, ssm_log_step,
               attn_sinks, rel_bias_table, ln_gain, ln_bias]
    small_m = [m_ssm_lambda_re, m_ssm_lambda_im, m_ssm_b_re, m_ssm_b_im, m_ssm_c_re, m_ssm_c_im, m_ssm_d,
               m_ssm_log_step, m_attn_sinks, m_rel_bias_table, m_ln_gain, m_ln_bias]
    small_v = [v_ssm_lambda_re, v_ssm_lambda_im, v_ssm_b_re, v_ssm_b_im, v_ssm_c_re, v_ssm_c_im, v_ssm_d,
               v_ssm_log_step, v_attn_sinks, v_rel_bias_table, v_ln_gain, v_ln_bias]
    small_g = [g_lre, g_lim, g_bre, g_bim, g_cre, g_cim, g_ssm_d, g_lstep, g_sinks, g_table, g_ln_gain, g_ln_bias]
    parts_small = _all_gather(_pack(small_g), name="gather_g_small")
    o_small = _adamw_reduce(parts_small, _pack(small_w), _pack(small_m), _pack(small_v), name="adamw_small", tr=2160)
    sg, sd, sm, sv = [_unpack(o, small_w) for o in o_small]

    loss = lax.psum(loss_part[0, 0], MESH_AXES)

    def big(o, idx):
        g_in_, g_three_, g_out_ = o_in[idx], o_three[idx], o_out[idx]
        return {"w_in": g_in_[None], "w_glu": g_three_[None, :D_SSM], "w_branch_ssm": g_three_[None, D_SSM:2 * D_SSM],
                "w_branch_attn": g_three_[None, 2 * D_SSM:], "w_out": g_out_[None]}

    order = ["w_in", "ssm_lambda_re", "ssm_lambda_im", "ssm_b_re", "ssm_b_im", "ssm_c_re", "ssm_c_im", "ssm_d",
             "ssm_log_step", "w_glu", "attn_sinks", "rel_bias_table", "w_branch_ssm", "w_branch_attn", "w_out",
             "ln_gain", "ln_bias"]
    outs = [loss, grad_x[None]]
    for idx, small in enumerate([sg, sd, sm, sv]):
        table = big(None, idx)
        table.update(dict(zip(SMALL, small)))
        outs += [table[n] for n in order]
    return tuple(outs)
```

```python
import functools
import math

import jax
import jax.numpy as jnp
from jax import lax
from jax.experimental import pallas as pl
from jax.experimental.pallas import tpu as pltpu

F32 = jnp.float32
MXU_DTYPE = jnp.bfloat16
WIRE_DTYPE = jnp.bfloat16

D_MODEL = 2048
D_SSM = 1024
SSM_GROUP = 16
N_GROUPS = 64
SSM_STATE = 64
N_Q_HEADS = 16
N_KV_HEADS = 4
Q_PER_KV = 4
HEAD_DIM = 64
D_ATTN = 1024
D_KV = 256
WINDOW = 128
BLOCK = 128
N_BUCKETS = 32
MAX_DISTANCE = 128
D_IN = 8704
DEEPNORM_ALPHA = 2.0 ** 0.25
LN_EPS = 1e-5
NEG_INF = -1e30
ATTN_SCALE = HEAD_DIM ** -0.5

ADAM_LR = 0.001
ADAM_B1 = 0.9
ADAM_B2 = 0.999
ADAM_EPS = 1e-08
ADAM_WD = 0.01
ADAM_STEP = 10

N_DEV = 8
SSM_CHUNK = 16
GROUP_BLOCK = 8
N_GB = N_GROUPS // GROUP_BLOCK
GB_LANES = GROUP_BLOCK * SSM_GROUP
GB_STATE = GROUP_BLOCK * SSM_STATE
COL = 256
OFF_U, OFF_ZS, OFF_Q, OFF_K, OFF_V, OFF_ZA, OFF_GS, OFF_GA = 0, 4, 8, 12, 13, 14, 18, 26

VMEM_CAP = 56 * 1024 * 1024
MESH_AXES = ("x", "y", "c")


def _vmem_limit(block_bytes):
    return int(min(max(3 * block_bytes, 16 * 1024 * 1024), VMEM_CAP))


def _nbytes(shape, dtype):
    return math.prod(shape) * jnp.dtype(dtype).itemsize


def _tile(n, pref):
    if n <= pref:
        return n
    t = (pref // 128) * 128
    while t >= 128:
        if n % t == 0:
            return t
        t -= 128
    return n


def _mm(a, b, *, name, ta=False, tb=False, out_dtype=F32, tm=1024, tn=512, tk=512, add=None, add_scale=1.0,
        after=()):
    squeeze = a.ndim == 2
    if squeeze:
        a, b = a[None], b[None]
        if add is not None:
            add = add[None]
    nb = a.shape[0]
    m, k = (a.shape[2], a.shape[1]) if ta else (a.shape[1], a.shape[2])
    n = b.shape[1] if tb else b.shape[2]
    tm, tn, tk = _tile(m, tm), _tile(n, tn), _tile(k, tk)
    nk = k // tk
    dn = (((0 if ta else 1,), (1 if tb else 0,)), ((), ()))

    a_spec = (pl.BlockSpec((None, tk, tm), lambda g, i, j, kk: (g, kk, i)) if ta
              else pl.BlockSpec((None, tm, tk), lambda g, i, j, kk: (g, i, kk)))
    b_spec = (pl.BlockSpec((None, tn, tk), lambda g, i, j, kk: (g, j, kk)) if tb
              else pl.BlockSpec((None, tk, tn), lambda g, i, j, kk: (g, kk, j)))
    o_spec = pl.BlockSpec((None, tm, tn), lambda g, i, j, kk: (g, i, j))
    in_specs = [a_spec, b_spec]
    operands = [a, b]
    if add is not None:
        in_specs.append(o_spec)
        operands.append(add)
    for tok in after:
        in_specs.append(pl.BlockSpec(memory_space=pl.ANY))
        operands.append(tok)
    n_in = len(operands)

    def body(*refs):
        a_ref, b_ref = refs[0], refs[1]
        add_ref = refs[2] if add is not None else None
        o_ref = refs[n_in]
        acc_ref = refs[-1]
        kk = pl.program_id(3)
        part = lax.dot_general(a_ref[...].astype(MXU_DTYPE), b_ref[...].astype(MXU_DTYPE), dn,
                               preferred_element_type=F32)

        @pl.when(kk == 0)
        def _():
            acc_ref[...] = part

        @pl.when(kk > 0)
        def _():
            acc_ref[...] += part

        @pl.when(kk == nk - 1)
        def _():
            r = acc_ref[...]
            if add_ref is not None:
                r = r + add_scale * add_ref[...]
            o_ref[...] = r.astype(out_dtype)

    blocks = (_nbytes((tm, tk), a.dtype) + _nbytes((tk, tn), b.dtype) + _nbytes((tm, tn), out_dtype)
              + (_nbytes((tm, tn), F32) if add is not None else 0))
    out = pl.pallas_call(
        body,
        name=name,
        grid=(nb, m // tm, n // tn, nk),
        in_specs=in_specs,
        out_specs=o_spec,
        out_shape=jax.ShapeDtypeStruct((nb, m, n), out_dtype),
        scratch_shapes=[pltpu.VMEM((tm, tn), F32)],
        compiler_params=pltpu.CompilerParams(
            dimension_semantics=("parallel", "parallel", "parallel", "arbitrary"),
            vmem_limit_bytes=_vmem_limit(2 * blocks + 2 * _nbytes((tm, tn), F32))),
    )(*operands)
    return out[0] if squeeze else out


def _ew(fn, ins, outs, *, rows, cw, ncb, tr, name):
    tr = min(tr, rows)
    n_in = len(ins)

    def row_map(off):
        return lambda j, i: (i, off + j)

    def vec_map(off):
        return lambda j, i: (0, off + j)

    in_specs = []
    for arr, kind, off in ins:
        if kind == "row":
            in_specs.append(pl.BlockSpec((tr, cw), row_map(off)))
        else:
            in_specs.append(pl.BlockSpec((1, cw), vec_map(off)))
    out_specs, out_shapes = [], []
    for bw, dt, kind in outs:
        if kind == "row":
            out_specs.append(pl.BlockSpec((tr, bw), row_map(0)))
            out_shapes.append(jax.ShapeDtypeStruct((rows, ncb * bw), dt))
        else:
            out_specs.append(pl.BlockSpec((1, bw), vec_map(0)))
            out_shapes.append(jax.ShapeDtypeStruct((1, ncb * bw), F32))

    def body(*refs):
        i = pl.program_id(1)
        vals = fn(*[r[...] for r in refs[:n_in]])
        for r, (bw, dt, kind), v in zip(refs[n_in:], outs, vals):
            if kind == "row":
                r[...] = v.astype(dt)
            else:
                @pl.when(i == 0)
                def _(r=r):
                    r[...] = jnp.zeros_like(r)

                r[...] += v

    blocks = sum(_nbytes((tr, cw), a.dtype) for a, kind, _ in ins if kind == "row")
    blocks += sum(_nbytes((tr, bw), dt) for bw, dt, kind in outs if kind == "row")
    res = pl.pallas_call(
        body,
        name=name,
        grid=(ncb, rows // tr),
        in_specs=in_specs,
        out_specs=out_specs,
        out_shape=out_shapes,
        compiler_params=pltpu.CompilerParams(
            dimension_semantics=("parallel", "arbitrary"),
            vmem_limit_bytes=_vmem_limit(4 * blocks)),
    )(*[a for a, _, _ in ins])
    return res


def _sigmoid(x):
    return 1.0 / (1.0 + jnp.exp(-x))


INV_SQRT2 = 0.7071067811865476
INV_SQRT_2PI = 0.3989422804014327


def _gelu(x):
    return 0.5 * x * (1.0 + lax.erf(x * INV_SQRT2))


def _gelu_grad(x):
    return 0.5 * (1.0 + lax.erf(x * INV_SQRT2)) + x * INV_SQRT_2PI * jnp.exp(-0.5 * x * x)


def _silu(x):
    return x * _sigmoid(x)


def _silu_grad(x):
    s = _sigmoid(x)
    return s * (1.0 + x * (1.0 - s))


def _ssm_matrices(lam_re, lam_im, b_re, b_im, c_re, c_im, log_step):
    hi = lax.Precision.HIGHEST
    L = SSM_CHUNK
    step = jnp.exp(log_step)[:, None]
    ea, eb = lam_re * step, lam_im * step
    mag = jnp.exp(ea)
    lbr, lbi = mag * jnp.cos(eb), mag * jnp.sin(eb)
    den = lam_re * lam_re + lam_im * lam_im
    nr, ni = lbr - 1.0, lbi
    cr = (nr * lam_re + ni * lam_im) / den
    ci = (ni * lam_re - nr * lam_im) / den
    bbr = cr[..., None] * b_re - ci[..., None] * b_im
    bbi = cr[..., None] * b_im + ci[..., None] * b_re
    taus = jnp.arange(L + 1, dtype=F32)[:, None, None]
    pmag = jnp.exp(taus * ea[None])
    pwr, pwi = pmag * jnp.cos(taus * eb[None]), pmag * jnp.sin(taus * eb[None])
    mr = c_re[None] * pwr[:L, :, None, :] - c_im[None] * pwi[:L, :, None, :]
    mi = c_re[None] * pwi[:L, :, None, :] + c_im[None] * pwr[:L, :, None, :]
    kk = (jnp.einsum("tghp,gpk->tghk", mr, bbr, precision=hi)
          - jnp.einsum("tghp,gpk->tghk", mi, bbi, precision=hi))
    taps = jnp.transpose(kk.reshape(L, N_GB, GROUP_BLOCK, SSM_GROUP, SSM_GROUP), (1, 0, 2, 4, 3))
    taps = taps.reshape(N_GB, L, GB_LANES, SSM_GROUP)
    rev_r, rev_i = pwr[L - 1 - jnp.arange(L)], pwi[L - 1 - jnp.arange(L)]
    wer = rev_r[..., None] * bbr[None] - rev_i[..., None] * bbi[None]
    wei = rev_r[..., None] * bbi[None] + rev_i[..., None] * bbr[None]

    def rows_in(w):
        w = jnp.transpose(w.reshape(L, N_GB, GROUP_BLOCK, SSM_STATE, SSM_GROUP), (1, 0, 2, 4, 3))
        return w.reshape(N_GB, L * GB_LANES, SSM_STATE)

    wend = jnp.concatenate([rows_in(wer), rows_in(wei)], axis=2)
    m1r = c_re[None] * pwr[1:, :, None, :] - c_im[None] * pwi[1:, :, None, :]
    m1i = c_re[None] * pwi[1:, :, None, :] + c_im[None] * pwr[1:, :, None, :]

    def rows_out(m):
        m = jnp.transpose(m.reshape(L, N_GB, GROUP_BLOCK, SSM_GROUP, SSM_STATE), (1, 2, 4, 0, 3))
        return m.reshape(N_GB, GB_STATE, L * SSM_GROUP)

    wout = jnp.concatenate([rows_out(m1r), rows_out(-m1i)], axis=1)
    acat = jnp.concatenate([pwr[L].reshape(N_GB, 1, GB_STATE), pwi[L].reshape(N_GB, 1, GB_STATE)], axis=2)
    return taps, wend, wout, acat


def _ssm_expand(taps, wend, wout):
    L = SSM_CHUNK
    eye = jnp.eye(GROUP_BLOCK, dtype=F32)
    t5 = taps.reshape(N_GB, L, GROUP_BLOCK, SSM_GROUP, 1, SSM_GROUP)
    dblk = (t5 * eye[None, None, :, None, :, None]).reshape(N_GB, L, GB_LANES, GB_LANES)

    def cols_in(w):
        w = w.reshape(N_GB, L, GROUP_BLOCK, SSM_GROUP, 1, SSM_STATE)
        return (w * eye[None, None, :, None, :, None]).reshape(N_GB, L * GB_LANES, GB_STATE)

    wend_b = jnp.concatenate([cols_in(wend[:, :, :SSM_STATE]), cols_in(wend[:, :, SSM_STATE:])], axis=2)

    def cols_out(m):
        m = m.reshape(N_GB, GROUP_BLOCK, SSM_STATE, L, 1, SSM_GROUP)
        return (m * eye[None, :, None, None, :, None]).reshape(N_GB, GB_STATE, L * GB_LANES)

    wout_b = jnp.concatenate([cols_out(wout[:, :GB_STATE]), cols_out(wout[:, GB_STATE:])], axis=1)
    return dblk, wend_b.astype(MXU_DTYPE), wout_b.astype(MXU_DTYPE)


def _dot(a, b):
    return jnp.dot(a.astype(MXU_DTYPE), b.astype(MXU_DTYPE), preferred_element_type=F32)


def _dot_nt(a, b):
    return lax.dot_general(a.astype(MXU_DTYPE), b.astype(MXU_DTYPE), (((1,), (1,)), ((), ())),
                           preferred_element_type=F32)


def _dot_tn(a, b):
    return lax.dot_general(a.astype(MXU_DTYPE), b.astype(MXU_DTYPE), (((0,), (0,)), ((), ())),
                           preferred_element_type=F32)


STATE_W = 2 * GB_STATE


def _chunk_scan(e, acat):
    nc = e.shape[0]
    spec = pl.BlockSpec((nc, STATE_W), lambda b: (0, b))
    aspec = pl.BlockSpec((None, 1, STATE_W), lambda b: (b, 0, 0))

    def body(e_ref, a_ref, s_ref):
        a_r, a_i = a_ref[:, :GB_STATE], a_ref[:, GB_STATE:]

        def step(c, carry):
            s_r, s_i = carry
            s_ref[pl.ds(c, 1), :GB_STATE] = s_r
            s_ref[pl.ds(c, 1), GB_STATE:] = s_i
            e_r = e_ref[pl.ds(c, 1), :GB_STATE]
            e_i = e_ref[pl.ds(c, 1), GB_STATE:]
            return (a_r * s_r - a_i * s_i + e_r, a_r * s_i + a_i * s_r + e_i)

        zero = jnp.zeros((1, GB_STATE), F32)
        lax.fori_loop(0, nc, step, (zero, zero))

    return pl.pallas_call(
        body, name="ssm_chunk_scan", grid=(N_GB,),
        in_specs=[spec, aspec], out_specs=spec,
        out_shape=jax.ShapeDtypeStruct(e.shape, F32),
        compiler_params=pltpu.CompilerParams(dimension_semantics=("parallel",)),
    )(e, acat)


def _chunk_scan_bwd(ds, s, acat):
    nc = ds.shape[0]
    spec = pl.BlockSpec((nc, STATE_W), lambda b: (0, b))
    aspec = pl.BlockSpec((None, 1, STATE_W), lambda b: (b, 0, 0))

    def body(ds_ref, s_ref, a_ref, ge_ref, da_ref):
        a_r, a_i = a_ref[:, :GB_STATE], a_ref[:, GB_STATE:]

        def step(t, carry):
            g_r, g_i, d_r, d_i = carry
            c = nc - 1 - t
            ge_ref[pl.ds(c, 1), :GB_STATE] = g_r
            ge_ref[pl.ds(c, 1), GB_STATE:] = g_i
            s_r = s_ref[pl.ds(c, 1), :GB_STATE]
            s_i = s_ref[pl.ds(c, 1), GB_STATE:]
            d_r = d_r + g_r * s_r + g_i * s_i
            d_i = d_i + g_i * s_r - g_r * s_i
            n_r = ds_ref[pl.ds(c, 1), :GB_STATE] + a_r * g_r + a_i * g_i
            n_i = ds_ref[pl.ds(c, 1), GB_STATE:] + a_r * g_i - a_i * g_r
            return (n_r, n_i, d_r, d_i)

        zero = jnp.zeros((1, GB_STATE), F32)
        _, _, d_r, d_i = lax.fori_loop(0, nc, step, (zero, zero, zero, zero))
        da_ref[:, :GB_STATE] = d_r
        da_ref[:, GB_STATE:] = d_i

    return pl.pallas_call(
        body, name="ssm_chunk_scan_bwd", grid=(N_GB,),
        in_specs=[spec, spec, aspec], out_specs=[spec, aspec],
        out_shape=[jax.ShapeDtypeStruct(ds.shape, F32), jax.ShapeDtypeStruct(acat.shape, F32)],
        compiler_params=pltpu.CompilerParams(dimension_semantics=("parallel",)),
    )(ds, s, acat)


def _step_rows(ref, j, nc):
    return ref[pl.ds(j, nc, stride=SSM_CHUNK), :]


def _lane_group(shape, axis, shift):
    return (lax.broadcasted_iota(jnp.int32, shape, axis) >> shift) & (GROUP_BLOCK - 1)


def _fold_lanes(z, widths):
    for w in widths:
        z = z + pltpu.roll(z, w, 1)
    return z


def _ssm_forward(proj, mats):
    dblk, wend, wout, acat = mats
    t = proj.shape[0]
    nc = t // SSM_CHUNK
    L = SSM_CHUNK
    lanes = pl.BlockSpec((t, GB_LANES), lambda b: (0, b))
    state = pl.BlockSpec((nc, STATE_W), lambda b: (0, b))

    def body_end(u_ref, w_ref, e_ref):
        x = jnp.concatenate([_step_rows(u_ref, j, nc).astype(MXU_DTYPE) for j in range(L)], axis=1)
        e_ref[...] = jnp.dot(x, w_ref[...], preferred_element_type=F32)

    e = pl.pallas_call(
        body_end, name="ssm_chunk_end", grid=(N_GB,),
        in_specs=[lanes, pl.BlockSpec((None, L * GB_LANES, STATE_W), lambda b: (b, 0, 0))],
        out_specs=state, out_shape=jax.ShapeDtypeStruct((nc, N_GB * STATE_W), F32),
        compiler_params=pltpu.CompilerParams(dimension_semantics=("parallel",), vmem_limit_bytes=VMEM_CAP),
    )(proj, wend)
    s = _chunk_scan(e, acat)

    def body_out(u_ref, d_ref, s_ref, w_ref, y_ref):
        xs = [_step_rows(u_ref, j, nc).astype(MXU_DTYPE) for j in range(L)]
        sb = s_ref[...].astype(MXU_DTYPE)
        for tt in range(L):
            xcat = jnp.concatenate(xs[:tt + 1], axis=1)
            taps = jnp.concatenate([d_ref[tt - j] for j in range(tt + 1)], axis=0).astype(MXU_DTYPE)
            y = (jnp.dot(xcat, taps, preferred_element_type=F32)
                 + jnp.dot(sb, w_ref[:, tt * GB_LANES:(tt + 1) * GB_LANES], preferred_element_type=F32))
            y_ref[pl.ds(tt, nc, stride=L), :] = y

    y = pl.pallas_call(
        body_out, name="ssm_chunk_out", grid=(N_GB,),
        in_specs=[lanes, pl.BlockSpec((None, L, GB_LANES, GB_LANES), lambda b: (b, 0, 0, 0)), state,
                  pl.BlockSpec((None, STATE_W, L * GB_LANES), lambda b: (b, 0, 0))],
        out_specs=lanes, out_shape=jax.ShapeDtypeStruct((t, D_SSM), F32),
        compiler_params=pltpu.CompilerParams(dimension_semantics=("parallel",), vmem_limit_bytes=VMEM_CAP),
    )(proj, dblk, s, wout)
    return y, s


def _ssm_backward(dy, proj, s, mats):
    dblk, wend, wout, acat = mats
    t = proj.shape[0]
    nc = t // SSM_CHUNK
    L = SSM_CHUNK
    lanes = pl.BlockSpec((t, GB_LANES), lambda b: (0, b))
    state = pl.BlockSpec((nc, STATE_W), lambda b: (0, b))
    taps_spec = pl.BlockSpec((None, L, GB_LANES, GB_LANES), lambda b: (b, 0, 0, 0))

    def body_state(dy_ref, w_ref, ds_ref):
        dyc = jnp.concatenate([_step_rows(dy_ref, tt, nc).astype(MXU_DTYPE) for tt in range(L)], axis=1)
        ds_ref[...] = _dot_nt(dyc, w_ref[...])

    ds = pl.pallas_call(
        body_state, name="ssm_bwd_state", grid=(N_GB,),
        in_specs=[lanes, pl.BlockSpec((None, STATE_W, L * GB_LANES), lambda b: (b, 0, 0))],
        out_specs=state, out_shape=jax.ShapeDtypeStruct((nc, N_GB * STATE_W), F32),
        compiler_params=pltpu.CompilerParams(dimension_semantics=("parallel",), vmem_limit_bytes=VMEM_CAP),
    )(dy, wout)
    ge, dacat = _chunk_scan_bwd(ds, s, acat)

    def body_in(u_ref, dy_ref, d_ref, ge_ref, w_ref, du_ref, dd_ref):
        xs = [_step_rows(u_ref, j, nc).astype(MXU_DTYPE) for j in range(L)]
        dys = [_step_rows(dy_ref, tt, nc).astype(MXU_DTYPE) for tt in range(L)]
        ge = ge_ref[...].astype(MXU_DTYPE)
        for i in range(L):
            dyc = jnp.concatenate(dys[i:], axis=1)
            taps = jnp.concatenate([d_ref[tt - i] for tt in range(i, L)], axis=1).astype(MXU_DTYPE)
            du_ref[pl.ds(i, nc, stride=L), :] = (
                _dot_nt(dyc, taps) + _dot_nt(ge, w_ref[i * GB_LANES:(i + 1) * GB_LANES, :]))
        for j in range(L):
            m = _dot_tn(xs[j], jnp.concatenate(dys[j:], axis=1))
            for tau in range(L - j):
                part = m[:, tau * GB_LANES:(tau + 1) * GB_LANES]
                if j == 0:
                    dd_ref[tau] = part
                else:
                    dd_ref[tau] += part
        same = _lane_group((GB_LANES, GB_LANES), 0, 4) == _lane_group((GB_LANES, GB_LANES), 1, 4)
        for tau in range(L):
            dd_ref[tau] = _fold_lanes(jnp.where(same, dd_ref[tau], 0.0), (64, 32, 16))

    du, ddblk = pl.pallas_call(
        body_in, name="ssm_bwd_in", grid=(N_GB,),
        in_specs=[lanes, lanes, taps_spec, state,
                  pl.BlockSpec((None, L * GB_LANES, STATE_W), lambda b: (b, 0, 0))],
        out_specs=[lanes, taps_spec],
        out_shape=[jax.ShapeDtypeStruct((t, D_SSM), F32), jax.ShapeDtypeStruct(dblk.shape, F32)],
        compiler_params=pltpu.CompilerParams(dimension_semantics=("parallel",), vmem_limit_bytes=VMEM_CAP),
    )(proj, dy, dblk, ge, wend)

    def body_w(u_ref, dy_ref, ge_ref, s_ref, dwe_ref, dwo_ref):
        ge = ge_ref[...].astype(MXU_DTYPE)
        sb = s_ref[...].astype(MXU_DTYPE)
        keep = _lane_group((GB_LANES, STATE_W), 0, 4) == _lane_group((GB_LANES, STATE_W), 1, 6)
        low = lax.broadcasted_iota(jnp.int32, (GB_LANES, 2 * SSM_STATE), 1) < SSM_STATE

        def fold_state(z):
            z = z[:, :GB_STATE // 2] + z[:, GB_STATE // 2:]
            z = z[:, :GB_STATE // 4] + z[:, GB_STATE // 4:]
            return _fold_lanes(z, (SSM_STATE,))

        for j in range(L):
            z = jnp.where(keep, _dot_tn(_step_rows(u_ref, j, nc).astype(MXU_DTYPE), ge), 0.0)
            dwe_ref[j * GB_LANES:(j + 1) * GB_LANES, :] = jnp.where(
                low, fold_state(z[:, :GB_STATE]), fold_state(z[:, GB_STATE:]))
        own = _lane_group((STATE_W, GB_LANES), 0, 6) == _lane_group((STATE_W, GB_LANES), 1, 4)
        chunk = lax.broadcasted_iota(jnp.int32, (STATE_W, GB_LANES), 1) >> 4
        for half in range(L // GROUP_BLOCK):
            acc = jnp.zeros((STATE_W, GB_LANES), F32)
            for k in range(GROUP_BLOCK):
                tt = half * GROUP_BLOCK + k
                z = jnp.where(own, _dot_tn(sb, _step_rows(dy_ref, tt, nc).astype(MXU_DTYPE)), 0.0)
                acc = acc + jnp.where(chunk == k, _fold_lanes(z, (64, 32, 16)), 0.0)
            dwo_ref[:, half * GB_LANES:(half + 1) * GB_LANES] = acc

    dwend, dwout = pl.pallas_call(
        body_w, name="ssm_bwd_w", grid=(N_GB,),
        in_specs=[lanes, lanes, state, state],
        out_specs=[pl.BlockSpec((None, L * GB_LANES, 2 * SSM_STATE), lambda b: (b, 0, 0)),
                   pl.BlockSpec((None, STATE_W, L * SSM_GROUP), lambda b: (b, 0, 0))],
        out_shape=[jax.ShapeDtypeStruct((N_GB, L * GB_LANES, 2 * SSM_STATE), F32),
                   jax.ShapeDtypeStruct((N_GB, STATE_W, L * SSM_GROUP), F32)],
        compiler_params=pltpu.CompilerParams(dimension_semantics=("parallel",), vmem_limit_bytes=VMEM_CAP),
    )(proj, dy, ge, s)
    return du, (ddblk[:, :, :, :SSM_GROUP], dwend, dwout, dacat)


def _t5_bucket(dist):
    max_exact = N_BUCKETS // 2
    is_small = dist < max_exact
    d = jnp.maximum(dist, 1).astype(F32)
    large = max_exact + (jnp.log(d / max_exact) / math.log(MAX_DISTANCE / max_exact)
                         * (N_BUCKETS - max_exact)).astype(jnp.int32)
    large = jnp.minimum(large, N_BUCKETS - 1)
    return jnp.where(is_small, dist, large)


def _band_bucket_and_mask():
    i = jnp.arange(BLOCK)[:, None]
    j = jnp.arange(2 * BLOCK)[None, :]
    dist = BLOCK + i - j
    ok = (dist >= 0) & (dist < WINDOW)
    return _t5_bucket(jnp.clip(dist, 0, None)), ok


def _band_bias(rel_bias_table):
    bucket, ok = _band_bucket_and_mask()
    onehot = ((bucket[:, :, None] == jnp.arange(N_BUCKETS)[None, None, :]) & ok[:, :, None]).astype(F32)
    bias = jnp.einsum("qsb,bh->hqs", onehot, rel_bias_table, precision=lax.Precision.HIGHEST)
    return bias + jnp.where(ok, 0.0, NEG_INF)[None]


KV_PAIR = 2
HEADS_PER_STEP = KV_PAIR * Q_PER_KV
Q_LANES = HEADS_PER_STEP * HEAD_DIM
SLAB = 2 * HEAD_DIM
Q_COL0 = OFF_Q * COL // Q_LANES
K_COL0 = OFF_K * COL // SLAB
V_COL0 = OFF_V * COL // SLAB


def _attn_specs():
    q_spec = pl.BlockSpec((BLOCK, Q_LANES), lambda m, n: (n, Q_COL0 + m))
    k_prev = pl.BlockSpec((BLOCK, SLAB), lambda m, n: (jnp.maximum(n - 1, 0), K_COL0 + m))
    k_cur = pl.BlockSpec((BLOCK, SLAB), lambda m, n: (n, K_COL0 + m))
    v_prev = pl.BlockSpec((BLOCK, SLAB), lambda m, n: (jnp.maximum(n - 1, 0), V_COL0 + m))
    v_cur = pl.BlockSpec((BLOCK, SLAB), lambda m, n: (n, V_COL0 + m))
    bias_spec = pl.BlockSpec((HEADS_PER_STEP, BLOCK, 2 * BLOCK), lambda m, n: (m, 0, 0))
    sink_spec = pl.BlockSpec(memory_space=pltpu.SMEM)
    wide = pl.BlockSpec((BLOCK, Q_LANES), lambda m, n: (n, m))
    pair = pl.BlockSpec((BLOCK, SLAB), lambda m, n: (n, m))
    return [sink_spec, q_spec, k_prev, k_cur, v_prev, v_cur, bias_spec], wide, pair


def _low_lanes(shape):
    return lax.broadcasted_iota(jnp.int32, shape, 1) < HEAD_DIM


def _band_halves(prev_ref, cur_ref):
    kb = jnp.concatenate([prev_ref[...], cur_ref[...]], axis=0)
    sw = pltpu.roll(kb, HEAD_DIM, 1)
    lo = _low_lanes(kb.shape)
    zero = jnp.zeros_like(kb)
    first = (jnp.where(lo, kb, zero).astype(MXU_DTYPE), jnp.where(lo, zero, sw).astype(MXU_DTYPE))
    second = (jnp.where(lo, sw, zero).astype(MXU_DTYPE), jnp.where(lo, zero, kb).astype(MXU_DTYPE))
    return first, second


def _fold_pair(acc):
    f = [x + pltpu.roll(x, HEAD_DIM, 1) for x in acc]
    return jnp.where(_low_lanes(f[0].shape), f[0], f[1])


def _attn_probs(q, kb, bias, edge, sink):
    s = _dot_nt(q, kb) * ATTN_SCALE + bias + edge
    m = jnp.maximum(jnp.max(s, axis=1, keepdims=True), sink)
    e = jnp.exp(s - m)
    es = jnp.exp(sink - m)
    inv = 1.0 / (jnp.sum(e, axis=1, keepdims=True) + es)
    return e * inv, es * inv


def _edge_mask(n):
    col = lax.broadcasted_iota(jnp.int32, (BLOCK, 2 * BLOCK), 1)
    return jnp.where(jnp.logical_or(n > 0, col >= BLOCK), 0.0, NEG_INF)


def _attn_forward(proj, bias, sinks):
    t = proj.shape[0]
    in_specs, wide, _ = _attn_specs()

    def body(sink_ref, q_ref, kp_ref, kc_ref, vp_ref, vc_ref, bias_ref, o_ref):
        m, n = pl.program_id(0), pl.program_id(1)
        keys = _band_halves(kp_ref, kc_ref)
        vals = _band_halves(vp_ref, vc_ref)
        edge = _edge_mask(n)
        for s in range(HEADS_PER_STEP // 2):
            a = s // (Q_PER_KV // 2)
            q = q_ref[:, s * SLAB:(s + 1) * SLAB].astype(MXU_DTYPE)
            out = None
            for e in range(2):
                h = 2 * s + e
                p, _ = _attn_probs(q, keys[a][e], bias_ref[h], edge, sink_ref[m * HEADS_PER_STEP + h])
                o = _dot(p, vals[a][e])
                out = o if out is None else out + o
            o_ref[:, s * SLAB:(s + 1) * SLAB] = out

    return pl.pallas_call(
        body, name="attn_fwd", grid=(N_KV_HEADS // KV_PAIR, t // BLOCK),
        in_specs=in_specs, out_specs=wide,
        out_shape=jax.ShapeDtypeStruct((t, D_ATTN), F32),
        compiler_params=pltpu.CompilerParams(dimension_semantics=("parallel", "arbitrary")),
    )(sinks, proj, proj, proj, proj, proj, bias)


def _attn_backward(proj, bias, sinks, d_attn):
    t = proj.shape[0]
    in_specs, wide, pair = _attn_specs()
    bias_spec = in_specs[-1]
    sink_out = pl.BlockSpec((HEADS_PER_STEP, 8, 128), lambda m, n: (m, 0, 0))

    def body(sink_ref, q_ref, kp_ref, kc_ref, vp_ref, vc_ref, bias_ref, do_ref,
             dq_ref, dka_ref, dkb_ref, dva_ref, dvb_ref, dbias_ref, dsink_ref):
        m, n = pl.program_id(0), pl.program_id(1)

        @pl.when(n == 0)
        def _():
            dbias_ref[...] = jnp.zeros_like(dbias_ref)
            dsink_ref[...] = jnp.zeros_like(dsink_ref)

        keys = _band_halves(kp_ref, kc_ref)
        vals = _band_halves(vp_ref, vc_ref)
        edge = _edge_mask(n)
        lo = _low_lanes((2 * BLOCK, SLAB))
        dk = [jnp.zeros((2 * BLOCK, SLAB), F32) for _ in range(KV_PAIR)]
        dv = [jnp.zeros((2 * BLOCK, SLAB), F32) for _ in range(KV_PAIR)]
        for s in range(HEADS_PER_STEP // 2):
            a = s // (Q_PER_KV // 2)
            q = q_ref[:, s * SLAB:(s + 1) * SLAB].astype(MXU_DTYPE)
            do = do_ref[:, s * SLAB:(s + 1) * SLAB].astype(MXU_DTYPE)
            dq, rk, rv = None, [], []
            for e in range(2):
                h = 2 * s + e
                p, ps = _attn_probs(q, keys[a][e], bias_ref[h], edge, sink_ref[m * HEADS_PER_STEP + h])
                dp = _dot_nt(do, vals[a][e])
                delta = jnp.sum(p * dp, axis=1, keepdims=True)
                ds = p * (dp - delta)
                dbias_ref[h] += ds
                dsink_ref[h] += jnp.broadcast_to(jnp.sum(-ps * delta, axis=0, keepdims=True), (8, 128))
                dqe = _dot(ds, keys[a][e])
                dq = dqe if dq is None else dq + dqe
                rk.append(_dot_tn(ds, q))
                rv.append(_dot_tn(p, do))
            dq_ref[:, s * SLAB:(s + 1) * SLAB] = (dq * ATTN_SCALE).astype(dq_ref.dtype)
            dk[a] = dk[a] + jnp.where(lo, rk[0], rk[1])
            dv[a] = dv[a] + jnp.where(lo, rv[0], rv[1])
        dkp = _fold_pair(dk) * ATTN_SCALE
        dvp = _fold_pair(dv)
        dkb_ref[...] = dkp[:BLOCK]
        dka_ref[...] = dkp[BLOCK:]
        dvb_ref[...] = dvp[:BLOCK]
        dva_ref[...] = dvp[BLOCK:]

    kv_shape = jax.ShapeDtypeStruct((t, D_KV), F32)
    return pl.pallas_call(
        body, name="attn_bwd", grid=(N_KV_HEADS // KV_PAIR, t // BLOCK),
        in_specs=in_specs + [wide],
        out_specs=[wide, pair, pair, pair, pair, bias_spec, sink_out],
        out_shape=[jax.ShapeDtypeStruct((t, D_ATTN), MXU_DTYPE), kv_shape, kv_shape, kv_shape, kv_shape,
                   jax.ShapeDtypeStruct(bias.shape, F32), jax.ShapeDtypeStruct((N_Q_HEADS, 8, 128), F32)],
        compiler_params=pltpu.CompilerParams(dimension_semantics=("parallel", "arbitrary")),
    )(sinks, proj, proj, proj, proj, proj, bias, d_attn)


def _shift_blocks(cur, prev):
    return cur + jnp.concatenate([prev[BLOCK:], jnp.zeros_like(prev[:BLOCK])], axis=0)


def _mesh_pos():
    return lax.axis_index("x"), lax.axis_index("y"), lax.axis_index("c")


def _all_gather(x, *, name):
    def body(x_ref, out_ref, send_sems, recv_sems, local_sem):
        x, y, c = _mesh_pos()
        me, sibling = (x, y, c), (x, y, 1 - c)
        chips = [(1 - x, y), (x, 1 - y), (1 - x, 1 - y)]

        def slot(px, py, pc):
            return out_ref.at[4 * px + 2 * py + pc]

        def copy(k, block, to, src=None):
            return pltpu.make_async_remote_copy(
                src_ref=slot(*block) if src is None else src, dst_ref=slot(*block),
                send_sem=send_sems.at[k], recv_sem=recv_sems.at[k],
                device_id=to, device_id_type=pl.DeviceIdType.MESH)

        mine = pltpu.make_async_copy(x_ref, slot(*me), local_sem)
        mine.start()
        first = [copy(0, me, sibling, src=x_ref)]
        first += [copy(1 + j, me, (*chip, c), src=x_ref) for j, chip in enumerate(chips)]
        for cp in first:
            cp.start()
        passed = [copy(4 + j, (*chip, c), sibling) for j, chip in enumerate(chips)]
        for j, chip in enumerate(chips):
            copy(1 + j, (*chip, c), me).wait_recv()
            passed[j].start()
        copy(0, sibling, me).wait_recv()
        for j, chip in enumerate(chips):
            copy(4 + j, (*chip, 1 - c), me).wait_recv()
        for cp in first + passed:
            cp.wait_send()
        mine.wait()

    return pl.pallas_call(
        body, name=name,
        in_specs=[pl.BlockSpec(memory_space=pl.ANY)],
        out_specs=pl.BlockSpec(memory_space=pl.ANY),
        out_shape=jax.ShapeDtypeStruct((N_DEV,) + x.shape, x.dtype),
        scratch_shapes=[pltpu.SemaphoreType.DMA((7,)), pltpu.SemaphoreType.DMA((7,)), pltpu.SemaphoreType.DMA],
    )(x)


_HBM = pl.BlockSpec(memory_space=pltpu.HBM)
_SEM = pl.BlockSpec(memory_space=pltpu.SEMAPHORE)
_DATAFLOW = pltpu.SideEffectType.DATAFLOW_SIDE_EFFECTING


def _peers():
    x, y, c = _mesh_pos()
    others = []
    for k in range(1, N_DEV):
        px, py, pc = x ^ (k >> 2), y ^ ((k >> 1) & 1), c ^ (k & 1)
        others.append(((px, py, pc), 4 * px + 2 * py + pc))
    return 4 * x + 2 * y + c, others


def _scatter_start(x, *, name):
    me, _ = _peers()
    own = lax.dynamic_index_in_dim(x, me, 0, keepdims=False)
    land = lax.dynamic_update_index_in_dim(lax.empty(x.shape, x.dtype), own, me, 0)

    def body(x_ref, land_ref, send_sems, recv_sems, x_thru, land_thru, token):
        mine, others = _peers()
        for k, (pos, idx) in enumerate(others):
            pltpu.make_async_remote_copy(
                src_ref=x_ref.at[idx], dst_ref=land_ref.at[mine],
                send_sem=send_sems.at[k], recv_sem=recv_sems.at[k],
                device_id=pos, device_id_type=pl.DeviceIdType.MESH).start()
        token[...] = jnp.zeros_like(token)

    return pl.pallas_call(
        body, name=name,
        out_shape=(pltpu.SemaphoreType.DMA((N_DEV - 1,)), pltpu.SemaphoreType.DMA((N_DEV - 1,)),
                   pltpu.HBM(x.shape, x.dtype), pltpu.HBM(x.shape, x.dtype), jax.ShapeDtypeStruct((8, 128), F32)),
        in_specs=(_HBM, _HBM), out_specs=(_SEM, _SEM, _HBM, _HBM, pl.BlockSpec(memory_space=pltpu.VMEM)),
        input_output_aliases={0: 2, 1: 3},
        compiler_params=pltpu.CompilerParams(has_side_effects=_DATAFLOW),
    )(pltpu.with_memory_space_constraint(x, pltpu.HBM), pltpu.with_memory_space_constraint(land, pltpu.HBM))


def _scatter_wait(started, after, *, name):
    send_sems, recv_sems, x_thru, land_thru, _ = started

    def body(x_ref, land_ref, send_sems, recv_sems, after_ref, x_dead, got_ref):
        _, others = _peers()
        for k, (pos, idx) in enumerate(others):
            copy = pltpu.make_async_remote_copy(
                src_ref=x_ref.at[idx], dst_ref=land_ref.at[idx],
                send_sem=send_sems.at[k], recv_sem=recv_sems.at[k],
                device_id=pos, device_id_type=pl.DeviceIdType.MESH)
            copy.wait_send()
            copy.wait_recv()

    return pl.pallas_call(
        body, name=name,
        out_shape=(pltpu.HBM(x_thru.shape, x_thru.dtype), pltpu.HBM(land_thru.shape, land_thru.dtype)),
        in_specs=(_HBM, _HBM, _SEM, _SEM, pl.BlockSpec(memory_space=pl.ANY)), out_specs=(_HBM, _HBM),
        input_output_aliases={0: 0, 1: 1},
        compiler_params=pltpu.CompilerParams(has_side_effects=_DATAFLOW),
    )(x_thru, land_thru, send_sems, recv_sems, after)[1]


def _adamw_math(w, g, m, v):
    m = ADAM_B1 * m + (1.0 - ADAM_B1) * g
    v = ADAM_B2 * v + (1.0 - ADAM_B2) * (g * g)
    m_hat = m / (1.0 - ADAM_B1 ** ADAM_STEP)
    v_hat = v / (1.0 - ADAM_B2 ** ADAM_STEP)
    delta = -ADAM_LR * (m_hat / (jnp.sqrt(v_hat) + ADAM_EPS) + ADAM_WD * w)
    return delta, m, v


def _adamw_reduce(parts, w, m, v, *, name, tr):
    r, c = w.shape
    tr = min(tr, r)
    spec = pl.BlockSpec((tr, c), lambda i: (i, 0))

    def body(p_ref, w_ref, m_ref, v_ref, g_ref, d_ref, nm_ref, nv_ref):
        g = p_ref[0].astype(F32)
        for s in range(1, N_DEV):
            g = g + p_ref[s].astype(F32)
        delta, nm, nv = _adamw_math(w_ref[...], g, m_ref[...], v_ref[...])
        g_ref[...] = g
        d_ref[...] = delta
        nm_ref[...] = nm
        nv_ref[...] = nv

    return pl.pallas_call(
        body, name=name, grid=(r // tr,),
        in_specs=[pl.BlockSpec((N_DEV, tr, c), lambda i: (0, i, 0)), spec, spec, spec],
        out_specs=[spec] * 4,
        out_shape=[jax.ShapeDtypeStruct((r, c), F32)] * 4,
        compiler_params=pltpu.CompilerParams(
            dimension_semantics=("parallel",),
            vmem_limit_bytes=_vmem_limit(2 * 15 * _nbytes((tr, c), F32))),
    )(parts, w, m, v)


SMALL = ["ssm_lambda_re", "ssm_lambda_im", "ssm_b_re", "ssm_b_im", "ssm_c_re", "ssm_c_im", "ssm_d",
         "ssm_log_step", "attn_sinks", "rel_bias_table", "ln_gain", "ln_bias"]


def _pack(arrs):
    flat = jnp.concatenate([a.reshape(-1) for a in arrs])
    pad = (-flat.shape[0]) % 1024
    return jnp.pad(flat, (0, pad)).reshape(-1, 128)


def _unpack(packed, like):
    flat = packed.reshape(-1)
    out, pos = [], 0
    for a in like:
        out.append(flat[pos:pos + a.size].reshape(a.shape))
        pos += a.size
    return out


def kernel(x, w_in, ssm_lambda_re, ssm_lambda_im, ssm_b_re, ssm_b_im, ssm_c_re, ssm_c_im, ssm_d, ssm_log_step, w_glu, attn_sinks, rel_bias_table, w_branch_ssm, w_branch_attn, w_out, ln_gain, ln_bias, loss_target, m_w_in, m_ssm_lambda_re, m_ssm_lambda_im, m_ssm_b_re, m_ssm_b_im, m_ssm_c_re, m_ssm_c_im, m_ssm_d, m_ssm_log_step, m_w_glu, m_attn_sinks, m_rel_bias_table, m_w_branch_ssm, m_w_branch_attn, m_w_out, m_ln_gain, m_ln_bias, v_w_in, v_ssm_lambda_re, v_ssm_lambda_im, v_ssm_b_re, v_ssm_b_im, v_ssm_c_re, v_ssm_c_im, v_ssm_d, v_ssm_log_step, v_w_glu, v_attn_sinks, v_rel_bias_table, v_w_branch_ssm, v_w_branch_attn, v_w_out, v_ln_gain, v_ln_bias):
    t = x.shape[1]
    xs = x[0]
    target = loss_target[0]
    col_in = w_in.shape[2]
    col_br = w_glu.shape[2]
    row_out = w_out.shape[1]

    g_in = _all_gather(w_in[0].astype(MXU_DTYPE), name="gather_w_in")
    three = jnp.concatenate([w_glu[0], w_branch_ssm[0], w_branch_attn[0]], axis=0).astype(MXU_DTYPE)
    g_three = _all_gather(three, name="gather_w_1024")
    g_out = _all_gather(w_out[0].astype(MXU_DTYPE), name="gather_w_out")
    win = g_in.transpose(1, 0, 2).reshape(D_MODEL, D_IN)
    three_full = g_three.transpose(1, 0, 2).reshape(3 * D_SSM, N_DEV * col_br)
    wglu, wbs, wba = three_full[:D_SSM], three_full[D_SSM:2 * D_SSM], three_full[2 * D_SSM:]
    wout = g_out.reshape(D_MODEL, D_MODEL)

    ssm_params = (ssm_lambda_re[0], ssm_lambda_im[0], ssm_b_re[0], ssm_b_im[0], ssm_c_re[0], ssm_c_im[0],
                  ssm_log_step[0])
    mats, mats_vjp = jax.vjp(_ssm_matrices, *ssm_params)
    bias, bias_vjp = jax.vjp(_band_bias, rel_bias_table)
    sinks = attn_sinks[0]
    d_skip = ssm_d

    x_mx = xs.astype(MXU_DTYPE)
    proj = _mm(x_mx, win, name="in_proj", tm=2048, tn=512, tk=2048)
    mats_mx = _ssm_expand(*mats[:3]) + (mats[3],)
    y_conv, states = _ssm_forward(proj, mats_mx)

    def f_gelu(yv, u, d):
        ys = yv + d * u
        return ys, _gelu(ys)

    y_ssm, glu_in = _ew(f_gelu, [(y_conv, "row", 0), (proj, "row", OFF_U), (d_skip, "vec", 0)],
                        [(COL, F32, "row"), (COL, MXU_DTYPE, "row")], rows=t, cw=COL, ncb=4, tr=1024, name="ssm_gelu")
    glu = _mm(glu_in, wglu, name="glu_proj", tm=2048, tn=512, tk=1024)

    def f_hssm(ga, gb, z):
        return (ga * _sigmoid(gb) * _silu(z),)

    (h_ssm,) = _ew(f_hssm, [(glu, "row", 0), (glu, "row", 4), (proj, "row", OFF_ZS)],
                   [(COL, MXU_DTYPE, "row")], rows=t, cw=COL, ncb=4, tr=1024, name="ssm_gate")

    attn = _attn_forward(proj, bias, sinks)

    def f_hattn(a, z):
        return (a * _silu(z),)

    (h_attn,) = _ew(f_hattn, [(attn, "row", 0), (proj, "row", OFF_ZA)], [(COL, MXU_DTYPE, "row")],
                    rows=t, cw=COL, ncb=4, tr=1024, name="attn_gate")
    p_ssm = _mm(h_ssm, wbs, name="branch_ssm", tm=2048, tn=512, tk=1024)
    p_attn = _mm(h_attn, wba, name="branch_attn", tm=2048, tn=512, tk=1024)

    def f_merge(ps, pa, ls, la):
        return (_sigmoid(ls) * ps + _sigmoid(la) * pa,)

    (merged,) = _ew(f_merge, [(p_ssm, "row", 0), (p_attn, "row", 0), (proj, "row", OFF_GS), (proj, "row", OFF_GA)],
                    [(COL, MXU_DTYPE, "row")], rows=t, cw=COL, ncb=8, tr=1024, name="merge")
    out = _mm(merged, wout, name="out_proj", tm=2048, tn=512, tk=2048)

    def f_norm(xv, ov, tg, gain, lbias):
        r = DEEPNORM_ALPHA * xv + ov
        mu = jnp.mean(r, axis=1, keepdims=True)
        cen = r - mu
        var = jnp.mean(cen * cen, axis=1, keepdims=True)
        rstd = lax.rsqrt(var + LN_EPS)
        xhat = cen * rstd
        yv = xhat * gain + lbias
        diff = yv - tg
        row_loss = 0.5 * jnp.mean(diff * diff, axis=1, keepdims=True)
        loss = jnp.broadcast_to(jnp.sum(row_loss, axis=0, keepdims=True), (1, 128))
        dy = diff * (1.0 / D_MODEL)
        dgain = jnp.sum(dy * xhat, axis=0, keepdims=True)
        dbias = jnp.sum(dy, axis=0, keepdims=True)
        dxh = dy * gain
        dr = rstd * (dxh - jnp.mean(dxh, axis=1, keepdims=True) - xhat * jnp.mean(dxh * xhat, axis=1, keepdims=True))
        return dr, loss, dgain, dbias

    dr, loss_part, g_ln_gain, g_ln_bias = _ew(
        f_norm, [(xs, "row", 0), (out, "row", 0), (target, "row", 0), (ln_gain, "vec", 0), (ln_bias, "vec", 0)],
        [(D_MODEL, F32, "row"), (128, F32, "acc"), (D_MODEL, F32, "acc"), (D_MODEL, F32, "acc")],
        rows=t, cw=D_MODEL, ncb=1, tr=256, name="norm_loss")

    def scatter_cols(g, cols):
        return g.reshape(g.shape[0], N_DEV, cols).transpose(1, 0, 2)

    gw_out = _mm(merged, dr, ta=True, out_dtype=WIRE_DTYPE, name="grad_w_out", tm=2048, tn=512, tk=1024)
    sent_out = _scatter_start(gw_out.reshape(N_DEV, row_out, D_MODEL), name="scatter_g_out_start")
    d_merged = _mm(dr, wout, tb=True, name="d_merged", tm=2048, tn=512, tk=2048, after=(sent_out[4],))

    def b_merge(dm, ps, pa, ls, la):
        gs, ga = _sigmoid(ls), _sigmoid(la)
        return dm * gs, dm * ga, dm * ps * gs * (1.0 - gs), dm * pa * ga * (1.0 - ga)

    dp_ssm, dp_attn, dgl_s, dgl_a = _ew(
        b_merge, [(d_merged, "row", 0), (p_ssm, "row", 0), (p_attn, "row", 0), (proj, "row", OFF_GS), (proj, "row", OFF_GA)],
        [(COL, MXU_DTYPE, "row")] * 4, rows=t, cw=COL, ncb=8, tr=1024, name="merge_bwd")
    gw_bs = _mm(h_ssm, dp_ssm, ta=True, out_dtype=WIRE_DTYPE, name="grad_w_branch_ssm", tm=1024, tn=512, tk=1024)
    gw_ba = _mm(h_attn, dp_attn, ta=True, out_dtype=WIRE_DTYPE, name="grad_w_branch_attn", tm=1024, tn=512, tk=1024)
    dh_ssm = _mm(dp_ssm, wbs, tb=True, name="d_h_ssm", tm=2048, tn=512, tk=2048)
    dh_attn = _mm(dp_attn, wba, tb=True, name="d_h_attn", tm=2048, tn=512, tk=2048)

    def b_hssm(dh, ga, gb, z):
        sg = _sigmoid(gb)
        dgate = dh * _silu(z)
        return dgate * sg, dgate * ga * sg * (1.0 - sg), dh * ga * sg * _silu_grad(z)

    dglu_a, dglu_b, dz_ssm = _ew(b_hssm, [(dh_ssm, "row", 0), (glu, "row", 0), (glu, "row", 4), (proj, "row", OFF_ZS)],
                                 [(COL, MXU_DTYPE, "row")] * 3, rows=t, cw=COL, ncb=4, tr=1024, name="ssm_gate_bwd")
    dglu = jnp.concatenate([dglu_a, dglu_b], axis=1)
    gw_glu = _mm(glu_in, dglu, ta=True, out_dtype=WIRE_DTYPE, name="grad_w_glu", tm=1024, tn=512, tk=1024)
    sent_three = _scatter_start(scatter_cols(jnp.concatenate([gw_glu, gw_bs, gw_ba], axis=0), col_br),
                                name="scatter_g_1024_start")
    dglu_in = _mm(dglu, wglu, tb=True, name="d_glu_in", tm=2048, tn=512, tk=2048, after=(sent_three[4],))

    def b_gelu(dgi, ys, u):
        dys = dgi * _gelu_grad(ys)
        return dys, jnp.sum(dys * u, axis=0, keepdims=True)

    dy_ssm, g_ssm_d = _ew(b_gelu, [(dglu_in, "row", 0), (y_ssm, "row", 0), (proj, "row", OFF_U)],
                          [(COL, F32, "row"), (COL, F32, "acc")],
                          rows=t, cw=COL, ncb=4, tr=1024, name="ssm_gelu_bwd")
    du_ssm, dmats = _ssm_backward(dy_ssm, proj, states, mats_mx)
    du = (du_ssm + d_skip * dy_ssm).astype(MXU_DTYPE)
    g_lre, g_lim, g_bre, g_bim, g_cre, g_cim, g_lstep = mats_vjp(dmats)

    def b_hattn(dh, a, z):
        return dh * _silu(z), dh * a * _silu_grad(z)

    d_attn, dz_attn = _ew(b_hattn, [(dh_attn, "row", 0), (attn, "row", 0), (proj, "row", OFF_ZA)],
                          [(COL, F32, "row"), (COL, MXU_DTYPE, "row")], rows=t, cw=COL, ncb=4, tr=1024, name="attn_gate_bwd")
    dq, dka, dkb, dva, dvb, dbias, dsink = _attn_backward(proj, bias, sinks, d_attn)
    dk = _shift_blocks(dka, dkb).astype(MXU_DTYPE)
    dv = _shift_blocks(dva, dvb).astype(MXU_DTYPE)
    (g_table,) = bias_vjp(dbias)
    g_sinks = dsink[:, 0, 0]

    dproj = jnp.concatenate([du, dz_ssm, dq, dk, dv, dz_attn, dgl_s, dgl_a], axis=1)
    gw_in = _mm(x_mx, dproj, ta=True, out_dtype=WIRE_DTYPE, name="grad_w_in", tm=2048, tn=512, tk=1024)
    sent_in = _scatter_start(scatter_cols(gw_in, col_in), name="scatter_g_in_start")
    grad_x = _mm(dproj, win, tb=True, add=dr, add_scale=DEEPNORM_ALPHA, name="grad_x", tm=1024, tn=1024, tk=2176,
                 after=(sent_in[4],))

    parts_out = _scatter_wait(sent_out, grad_x, name="scatter_g_out_wait")
    parts_three = _scatter_wait(sent_three, parts_out, name="scatter_g_1024_wait")
    parts_in = _scatter_wait(sent_in, parts_three, name="scatter_g_in_wait")

    o_in = _adamw_reduce(parts_in, w_in[0], m_w_in[0], v_w_in[0], name="adamw_w_in", tr=128)
    three_w = jnp.concatenate([w_glu[0], w_branch_ssm[0], w_branch_attn[0]], axis=0)
    three_m = jnp.concatenate([m_w_glu[0], m_w_branch_ssm[0], m_w_branch_attn[0]], axis=0)
    three_v = jnp.concatenate([v_w_glu[0], v_w_branch_ssm[0], v_w_branch_attn[0]], axis=0)
    o_three = _adamw_reduce(parts_three, three_w, three_m, three_v, name="adamw_w_1024", tr=512)
    o_out = _adamw_reduce(parts_out, w_out[0], m_w_out[0], v_w_out[0], name="adamw_w_out", tr=128)

    small_w = [ssm_lambda_re, ssm_lambda_im, ssm_b_re, ssm_b_im, ssm_c_re, ssm_c_im, ssm_d, ssm_log_step,
               attn_sinks, rel_bias_table, ln_gain, ln_bias]
    small_m = [m_ssm_lambda_re, m_ssm_lambda_im, m_ssm_b_re, m_ssm_b_im, m_ssm_c_re, m_ssm_c_im, m_ssm_d,
               m_ssm_log_step, m_attn_sinks, m_rel_bias_table, m_ln_gain, m_ln_bias]
    small_v = [v_ssm_lambda_re, v_ssm_lambda_im, v_ssm_b_re, v_ssm_b_im, v_ssm_c_re, v_ssm_c_im, v_ssm_d,
               v_ssm_log_step, v_attn_sinks, v_rel_bias_table, v_ln_gain, v_ln_bias]
    small_g = [g_lre, g_lim, g_bre, g_bim, g_cre, g_cim, g_ssm_d, g_lstep, g_sinks, g_table, g_ln_gain, g_ln_bias]
    parts_small = _all_gather(_pack(small_g), name="gather_g_small")
    o_small = _adamw_reduce(parts_small, _pack(small_w), _pack(small_m), _pack(small_v), name="adamw_small", tr=2160)
    sg, sd, sm, sv = [_unpack(o, small_w) for o in o_small]

    loss = lax.psum(loss_part[0, 0], MESH_AXES)

    def big(o, idx):
        g_in_, g_three_, g_out_ = o_in[idx], o_three[idx], o_out[idx]
        return {"w_in": g_in_[None], "w_glu": g_three_[None, :D_SSM], "w_branch_ssm": g_three_[None, D_SSM:2 * D_SSM],
                "w_branch_attn": g_three_[None, 2 * D_SSM:], "w_out": g_out_[None]}

    order = ["w_in", "ssm_lambda_re", "ssm_lambda_im", "ssm_b_re", "ssm_b_im", "ssm_c_re", "ssm_c_im", "ssm_d",
             "ssm_log_step", "w_glu", "attn_sinks", "rel_bias_table", "w_branch_ssm", "w_branch_attn", "w_out",
             "ln_gain", "ln_bias"]
    outs = [loss, grad_x[None]]
    for idx, small in enumerate([sg, sd, sm, sv]):
        table = big(None, idx)
        table.update(dict(zip(SMALL, small)))
        outs += [table[n] for n in order]
    return tuple(outs)
```

```python
import functools
import math

import jax
import jax.numpy as jnp
from jax import lax
from jax.experimental import pallas as pl
from jax.experimental.pallas import tpu as pltpu

F32 = jnp.float32
MXU_DTYPE = jnp.bfloat16
WIRE_DTYPE = jnp.bfloat16

D_MODEL = 2048
D_SSM = 1024
SSM_GROUP = 16
N_GROUPS = 64
SSM_STATE = 64
N_Q_HEADS = 16
N_KV_HEADS = 4
Q_PER_KV = 4
HEAD_DIM = 64
D_ATTN = 1024
D_KV = 256
WINDOW = 128
BLOCK = 128
N_BUCKETS = 32
MAX_DISTANCE = 128
D_IN = 8704
DEEPNORM_ALPHA = 2.0 ** 0.25
LN_EPS = 1e-5
NEG_INF = -1e30
ATTN_SCALE = HEAD_DIM ** -0.5

ADAM_LR = 0.001
ADAM_B1 = 0.9
ADAM_B2 = 0.999
ADAM_EPS = 1e-08
ADAM_WD = 0.01
ADAM_STEP = 10

N_DEV = 8
SSM_CHUNK = 16
GROUP_BLOCK = 8
N_GB = N_GROUPS // GROUP_BLOCK
GB_LANES = GROUP_BLOCK * SSM_GROUP
GB_STATE = GROUP_BLOCK * SSM_STATE
COL = 256
OFF_U, OFF_ZS, OFF_Q, OFF_K, OFF_V, OFF_ZA, OFF_GS, OFF_GA = 0, 4, 8, 12, 13, 14, 18, 26

VMEM_CAP = 56 * 1024 * 1024
MESH_AXES = ("x", "y", "c")


def _vmem_limit(block_bytes):
    return int(min(max(3 * block_bytes, 16 * 1024 * 1024), VMEM_CAP))


def _nbytes(shape, dtype):
    return math.prod(shape) * jnp.dtype(dtype).itemsize


def _tile(n, pref):
    if n <= pref:
        return n
    t = (pref // 128) * 128
    while t >= 128:
        if n % t == 0:
            return t
        t -= 128
    return n


def _mm(a, b, *, name, ta=False, tb=False, out_dtype=F32, tm=1024, tn=512, tk=512, add=None, add_scale=1.0,
        after=()):
    squeeze = a.ndim == 2
    if squeeze:
        a, b = a[None], b[None]
        if add is not None:
            add = add[None]
    nb = a.shape[0]
    m, k = (a.shape[2], a.shape[1]) if ta else (a.shape[1], a.shape[2])
    n = b.shape[1] if tb else b.shape[2]
    tm, tn, tk = _tile(m, tm), _tile(n, tn), _tile(k, tk)
    nk = k // tk
    dn = (((0 if ta else 1,), (1 if tb else 0,)), ((), ()))

    a_spec = (pl.BlockSpec((None, tk, tm), lambda g, i, j, kk: (g, kk, i)) if ta
              else pl.BlockSpec((None, tm, tk), lambda g, i, j, kk: (g, i, kk)))
    b_spec = (pl.BlockSpec((None, tn, tk), lambda g, i, j, kk: (g, j, kk)) if tb
              else pl.BlockSpec((None, tk, tn), lambda g, i, j, kk: (g, kk, j)))
    o_spec = pl.BlockSpec((None, tm, tn), lambda g, i, j, kk: (g, i, j))
    in_specs = [a_spec, b_spec]
    operands = [a, b]
    if add is not None:
        in_specs.append(o_spec)
        operands.append(add)
    for tok in after:
        in_specs.append(pl.BlockSpec(memory_space=pl.ANY))
        operands.append(tok)
    n_in = len(operands)

    def body(*refs):
        a_ref, b_ref = refs[0], refs[1]
        add_ref = refs[2] if add is not None else None
        o_ref = refs[n_in]
        acc_ref = refs[-1]
        kk = pl.program_id(3)
        part = lax.dot_general(a_ref[...].astype(MXU_DTYPE), b_ref[...].astype(MXU_DTYPE), dn,
                               preferred_element_type=F32)

        @pl.when(kk == 0)
        def _():
            acc_ref[...] = part

        @pl.when(kk > 0)
        def _():
            acc_ref[...] += part

        @pl.when(kk == nk - 1)
        def _():
            r = acc_ref[...]
            if add_ref is not None:
                r = r + add_scale * add_ref[...]
            o_ref[...] = r.astype(out_dtype)

    blocks = (_nbytes((tm, tk), a.dtype) + _nbytes((tk, tn), b.dtype) + _nbytes((tm, tn), out_dtype)
              + (_nbytes((tm, tn), F32) if add is not None else 0))
    out = pl.pallas_call(
        body,
        name=name,
        grid=(nb, m // tm, n // tn, nk),
        in_specs=in_specs,
        out_specs=o_spec,
        out_shape=jax.ShapeDtypeStruct((nb, m, n), out_dtype),
        scratch_shapes=[pltpu.VMEM((tm, tn), F32)],
        compiler_params=pltpu.CompilerParams(
            dimension_semantics=("parallel", "parallel", "parallel", "arbitrary"),
            vmem_limit_bytes=_vmem_limit(2 * blocks + 2 * _nbytes((tm, tn), F32))),
    )(*operands)
    return out[0] if squeeze else out


def _ew(fn, ins, outs, *, rows, cw, ncb, tr, name):
    tr = min(tr, rows)
    n_in = len(ins)

    def row_map(off):
        return lambda j, i: (i, off + j)

    def vec_map(off):
        return lambda j, i: (0, off + j)

    in_specs = []
    for arr, kind, off in ins:
        if kind == "row":
            in_specs.append(pl.BlockSpec((tr, cw), row_map(off)))
        else:
            in_specs.append(pl.BlockSpec((1, cw), vec_map(off)))
    out_specs, out_shapes = [], []
    for bw, dt, kind in outs:
        if kind == "row":
            out_specs.append(pl.BlockSpec((tr, bw), row_map(0)))
            out_shapes.append(jax.ShapeDtypeStruct((rows, ncb * bw), dt))
        else:
            out_specs.append(pl.BlockSpec((1, bw), vec_map(0)))
            out_shapes.append(jax.ShapeDtypeStruct((1, ncb * bw), F32))

    def body(*refs):
        i = pl.program_id(1)
        vals = fn(*[r[...] for r in refs[:n_in]])
        for r, (bw, dt, kind), v in zip(refs[n_in:], outs, vals):
            if kind == "row":
                r[...] = v.astype(dt)
            else:
                @pl.when(i == 0)
                def _(r=r):
                    r[...] = jnp.zeros_like(r)

                r[...] += v

    blocks = sum(_nbytes((tr, cw), a.dtype) for a, kind, _ in ins if kind == "row")
    blocks += sum(_nbytes((tr, bw), dt) for bw, dt, kind in outs if kind == "row")
    res = pl.pallas_call(
        body,
        name=name,
        grid=(ncb, rows // tr),
        in_specs=in_specs,
        out_specs=out_specs,
        out_shape=out_shapes,
        compiler_params=pltpu.CompilerParams(
            dimension_semantics=("parallel", "arbitrary"),
            vmem_limit_bytes=_vmem_limit(4 * blocks)),
    )(*[a for a, _, _ in ins])
    return res


def _sigmoid(x):
    return 1.0 / (1.0 + jnp.exp(-x))


INV_SQRT2 = 0.7071067811865476
INV_SQRT_2PI = 0.3989422804014327


def _gelu(x):
    return 0.5 * x * (1.0 + lax.erf(x * INV_SQRT2))


def _gelu_grad(x):
    return 0.5 * (1.0 + lax.erf(x * INV_SQRT2)) + x * INV_SQRT_2PI * jnp.exp(-0.5 * x * x)


def _silu(x):
    return x * _sigmoid(x)


def _silu_grad(x):
    s = _sigmoid(x)
    return s * (1.0 + x * (1.0 - s))


def _ssm_matrices(lam_re, lam_im, b_re, b_im, c_re, c_im, log_step):
    hi = lax.Precision.HIGHEST
    L = SSM_CHUNK
    step = jnp.exp(log_step)[:, None]
    ea, eb = lam_re * step, lam_im * step
    mag = jnp.exp(ea)
    lbr, lbi = mag * jnp.cos(eb), mag * jnp.sin(eb)
    den = lam_re * lam_re + lam_im * lam_im
    nr, ni = lbr - 1.0, lbi
    cr = (nr * lam_re + ni * lam_im) / den
    ci = (ni * lam_re - nr * lam_im) / den
    bbr = cr[..., None] * b_re - ci[..., None] * b_im
    bbi = cr[..., None] * b_im + ci[..., None] * b_re
    taus = jnp.arange(L + 1, dtype=F32)[:, None, None]
    pmag = jnp.exp(taus * ea[None])
    pwr, pwi = pmag * jnp.cos(taus * eb[None]), pmag * jnp.sin(taus * eb[None])
    mr = c_re[None] * pwr[:L, :, None, :] - c_im[None] * pwi[:L, :, None, :]
    mi = c_re[None] * pwi[:L, :, None, :] + c_im[None] * pwr[:L, :, None, :]
    kk = (jnp.einsum("tghp,gpk->tghk", mr, bbr, precision=hi)
          - jnp.einsum("tghp,gpk->tghk", mi, bbi, precision=hi))
    taps = jnp.transpose(kk.reshape(L, N_GB, GROUP_BLOCK, SSM_GROUP, SSM_GROUP), (1, 0, 2, 4, 3))
    taps = taps.reshape(N_GB, L, GB_LANES, SSM_GROUP)
    rev_r, rev_i = pwr[L - 1 - jnp.arange(L)], pwi[L - 1 - jnp.arange(L)]
    wer = rev_r[..., None] * bbr[None] - rev_i[..., None] * bbi[None]
    wei = rev_r[..., None] * bbi[None] + rev_i[..., None] * bbr[None]

    def rows_in(w):
        w = jnp.transpose(w.reshape(L, N_GB, GROUP_BLOCK, SSM_STATE, SSM_GROUP), (1, 0, 2, 4, 3))
        return w.reshape(N_GB, L * GB_LANES, SSM_STATE)

    wend = jnp.concatenate([rows_in(wer), rows_in(wei)], axis=2)
    m1r = c_re[None] * pwr[1:, :, None, :] - c_im[None] * pwi[1:, :, None, :]
    m1i = c_re[None] * pwi[1:, :, None, :] + c_im[None] * pwr[1:, :, None, :]

    def rows_out(m):
        m = jnp.transpose(m.reshape(L, N_GB, GROUP_BLOCK, SSM_GROUP, SSM_STATE), (1, 2, 4, 0, 3))
        return m.reshape(N_GB, GB_STATE, L * SSM_GROUP)

    wout = jnp.concatenate([rows_out(m1r), rows_out(-m1i)], axis=1)
    acat = jnp.concatenate([pwr[L].reshape(N_GB, 1, GB_STATE), pwi[L].reshape(N_GB, 1, GB_STATE)], axis=2)
    return taps, wend, wout, acat


def _lane_group(shape, axis, shift):
    return (lax.broadcasted_iota(jnp.int32, shape, axis) >> shift) & (GROUP_BLOCK - 1)


def _ssm_expand(taps, wend, wout):
    L = SSM_CHUNK
    taps = jnp.pad(taps, ((0, 0), (0, 0), (0, 0), (0, GB_LANES - SSM_GROUP)))

    def body(t_ref, we_ref, wo_ref, d_ref, web_ref, wob_ref):
        def rc(shape):
            return lax.broadcasted_iota(jnp.int32, shape, 0), lax.broadcasted_iota(jnp.int32, shape, 1)

        r, c = rc((GB_LANES, GB_LANES))
        spread = ((r < SSM_GROUP) & (r == (c & (SSM_GROUP - 1)))).astype(MXU_DTYPE)
        same = _lane_group((GB_LANES, GB_LANES), 0, 4) == _lane_group((GB_LANES, GB_LANES), 1, 4)
        for tau in range(L):
            full = jnp.dot(t_ref[tau].astype(MXU_DTYPE), spread, preferred_element_type=F32)
            d_ref[tau] = jnp.where(same, full, 0.0).astype(d_ref.dtype)
        r, c = rc((2 * SSM_STATE, STATE_W))
        part = (r >> 6) == (c >> 9)
        spread = (part & ((r & (SSM_STATE - 1)) == (c & (SSM_STATE - 1)))).astype(MXU_DTYPE)
        keep = _lane_group((GB_LANES, STATE_W), 0, 4) == _lane_group((GB_LANES, STATE_W), 1, 6)
        for j in range(L):
            rows = slice(j * GB_LANES, (j + 1) * GB_LANES)
            full = jnp.dot(we_ref[rows, :].astype(MXU_DTYPE), spread, preferred_element_type=F32)
            web_ref[rows, :] = jnp.where(keep, full, 0.0).astype(web_ref.dtype)
        r, c = rc((GB_LANES, GB_LANES))
        own = _lane_group((STATE_W, GB_LANES), 0, 6) == _lane_group((STATE_W, GB_LANES), 1, 4)
        for tt in range(L):
            half, k = tt // GROUP_BLOCK, tt % GROUP_BLOCK
            spread = (((r >> 4) == k) & ((r & (SSM_GROUP - 1)) == (c & (SSM_GROUP - 1)))).astype(MXU_DTYPE)
            src = wo_ref[:, half * GB_LANES:(half + 1) * GB_LANES].astype(MXU_DTYPE)
            full = jnp.dot(src, spread, preferred_element_type=F32)
            wob_ref[:, tt * GB_LANES:(tt + 1) * GB_LANES] = jnp.where(own, full, 0.0).astype(wob_ref.dtype)

    def spec(shape):
        return pl.BlockSpec((None,) + shape[1:], lambda b: (b,) + (0,) * (len(shape) - 1))

    out_shapes = [(N_GB, L, GB_LANES, GB_LANES), (N_GB, L * GB_LANES, STATE_W), (N_GB, STATE_W, L * GB_LANES)]
    return tuple(pl.pallas_call(
        body, name="ssm_expand", grid=(N_GB,),
        in_specs=[spec(taps.shape), spec(wend.shape), spec(wout.shape)],
        out_specs=[spec(s) for s in out_shapes],
        out_shape=[jax.ShapeDtypeStruct(s, MXU_DTYPE) for s in out_shapes],
        compiler_params=pltpu.CompilerParams(dimension_semantics=("parallel",), vmem_limit_bytes=VMEM_CAP),
    )(taps, wend, wout))


def _dot(a, b):
    return jnp.dot(a.astype(MXU_DTYPE), b.astype(MXU_DTYPE), preferred_element_type=F32)


def _dot_nt(a, b):
    return lax.dot_general(a.astype(MXU_DTYPE), b.astype(MXU_DTYPE), (((1,), (1,)), ((), ())),
                           preferred_element_type=F32)


def _dot_tn(a, b):
    return lax.dot_general(a.astype(MXU_DTYPE), b.astype(MXU_DTYPE), (((0,), (0,)), ((), ())),
                           preferred_element_type=F32)


STATE_W = 2 * GB_STATE


def _chunk_scan(e, acat):
    nc = e.shape[0]
    spec = pl.BlockSpec((nc, STATE_W), lambda b: (0, b))
    aspec = pl.BlockSpec((None, 1, STATE_W), lambda b: (b, 0, 0))

    def body(e_ref, a_ref, s_ref):
        a_r, a_i = a_ref[:, :GB_STATE], a_ref[:, GB_STATE:]

        def step(c, carry):
            s_r, s_i = carry
            s_ref[pl.ds(c, 1), :GB_STATE] = s_r
            s_ref[pl.ds(c, 1), GB_STATE:] = s_i
            e_r = e_ref[pl.ds(c, 1), :GB_STATE]
            e_i = e_ref[pl.ds(c, 1), GB_STATE:]
            return (a_r * s_r - a_i * s_i + e_r, a_r * s_i + a_i * s_r + e_i)

        zero = jnp.zeros((1, GB_STATE), F32)
        lax.fori_loop(0, nc, step, (zero, zero))

    return pl.pallas_call(
        body, name="ssm_chunk_scan", grid=(N_GB,),
        in_specs=[spec, aspec], out_specs=spec,
        out_shape=jax.ShapeDtypeStruct(e.shape, F32),
        compiler_params=pltpu.CompilerParams(dimension_semantics=("parallel",)),
    )(e, acat)


def _chunk_scan_bwd(ds, s, acat):
    nc = ds.shape[0]
    spec = pl.BlockSpec((nc, STATE_W), lambda b: (0, b))
    aspec = pl.BlockSpec((None, 1, STATE_W), lambda b: (b, 0, 0))

    def body(ds_ref, s_ref, a_ref, ge_ref, da_ref):
        a_r, a_i = a_ref[:, :GB_STATE], a_ref[:, GB_STATE:]

        def step(t, carry):
            g_r, g_i, d_r, d_i = carry
            c = nc - 1 - t
            ge_ref[pl.ds(c, 1), :GB_STATE] = g_r
            ge_ref[pl.ds(c, 1), GB_STATE:] = g_i
            s_r = s_ref[pl.ds(c, 1), :GB_STATE]
            s_i = s_ref[pl.ds(c, 1), GB_STATE:]
            d_r = d_r + g_r * s_r + g_i * s_i
            d_i = d_i + g_i * s_r - g_r * s_i
            n_r = ds_ref[pl.ds(c, 1), :GB_STATE] + a_r * g_r + a_i * g_i
            n_i = ds_ref[pl.ds(c, 1), GB_STATE:] + a_r * g_i - a_i * g_r
            return (n_r, n_i, d_r, d_i)

        zero = jnp.zeros((1, GB_STATE), F32)
        _, _, d_r, d_i = lax.fori_loop(0, nc, step, (zero, zero, zero, zero))
        da_ref[:, :GB_STATE] = d_r
        da_ref[:, GB_STATE:] = d_i

    return pl.pallas_call(
        body, name="ssm_chunk_scan_bwd", grid=(N_GB,),
        in_specs=[spec, spec, aspec], out_specs=[spec, aspec],
        out_shape=[jax.ShapeDtypeStruct(ds.shape, F32), jax.ShapeDtypeStruct(acat.shape, F32)],
        compiler_params=pltpu.CompilerParams(dimension_semantics=("parallel",)),
    )(ds, s, acat)


def _step_rows(ref, j, nc):
    return ref[pl.ds(j, nc, stride=SSM_CHUNK), :]


def _fold_lanes(z, widths):
    for w in widths:
        z = z + pltpu.roll(z, w, 1)
    return z


def _ssm_forward(proj, mats):
    dblk, wend, wout, acat = mats
    t = proj.shape[0]
    nc = t // SSM_CHUNK
    L = SSM_CHUNK
    lanes = pl.BlockSpec((t, GB_LANES), lambda b: (0, b))
    state = pl.BlockSpec((nc, STATE_W), lambda b: (0, b))

    def body_end(u_ref, w_ref, e_ref):
        x = jnp.concatenate([_step_rows(u_ref, j, nc).astype(MXU_DTYPE) for j in range(L)], axis=1)
        e_ref[...] = jnp.dot(x, w_ref[...], preferred_element_type=F32)

    e = pl.pallas_call(
        body_end, name="ssm_chunk_end", grid=(N_GB,),
        in_specs=[lanes, pl.BlockSpec((None, L * GB_LANES, STATE_W), lambda b: (b, 0, 0))],
        out_specs=state, out_shape=jax.ShapeDtypeStruct((nc, N_GB * STATE_W), F32),
        compiler_params=pltpu.CompilerParams(dimension_semantics=("parallel",), vmem_limit_bytes=VMEM_CAP),
    )(proj, wend)
    s = _chunk_scan(e, acat)

    def body_out(u_ref, d_ref, s_ref, w_ref, y_ref):
        xs = [_step_rows(u_ref, j, nc).astype(MXU_DTYPE) for j in range(L)]
        sb = s_ref[...].astype(MXU_DTYPE)
        for tt in range(L):
            xcat = jnp.concatenate(xs[:tt + 1], axis=1)
            taps = jnp.concatenate([d_ref[tt - j] for j in range(tt + 1)], axis=0).astype(MXU_DTYPE)
            y = (jnp.dot(xcat, taps, preferred_element_type=F32)
                 + jnp.dot(sb, w_ref[:, tt * GB_LANES:(tt + 1) * GB_LANES], preferred_element_type=F32))
            y_ref[pl.ds(tt, nc, stride=L), :] = y

    y = pl.pallas_call(
        body_out, name="ssm_chunk_out", grid=(N_GB,),
        in_specs=[lanes, pl.BlockSpec((None, L, GB_LANES, GB_LANES), lambda b: (b, 0, 0, 0)), state,
                  pl.BlockSpec((None, STATE_W, L * GB_LANES), lambda b: (b, 0, 0))],
        out_specs=lanes, out_shape=jax.ShapeDtypeStruct((t, D_SSM), F32),
        compiler_params=pltpu.CompilerParams(dimension_semantics=("parallel",), vmem_limit_bytes=VMEM_CAP),
    )(proj, dblk, s, wout)
    return y, s


def _ssm_backward(dy, proj, s, mats):
    dblk, wend, wout, acat = mats
    t = proj.shape[0]
    nc = t // SSM_CHUNK
    L = SSM_CHUNK
    lanes = pl.BlockSpec((t, GB_LANES), lambda b: (0, b))
    state = pl.BlockSpec((nc, STATE_W), lambda b: (0, b))
    taps_spec = pl.BlockSpec((None, L, GB_LANES, GB_LANES), lambda b: (b, 0, 0, 0))

    def body_state(dy_ref, w_ref, ds_ref):
        dyc = jnp.concatenate([_step_rows(dy_ref, tt, nc).astype(MXU_DTYPE) for tt in range(L)], axis=1)
        ds_ref[...] = _dot_nt(dyc, w_ref[...])

    ds = pl.pallas_call(
        body_state, name="ssm_bwd_state", grid=(N_GB,),
        in_specs=[lanes, pl.BlockSpec((None, STATE_W, L * GB_LANES), lambda b: (b, 0, 0))],
        out_specs=state, out_shape=jax.ShapeDtypeStruct((nc, N_GB * STATE_W), F32),
        compiler_params=pltpu.CompilerParams(dimension_semantics=("parallel",), vmem_limit_bytes=VMEM_CAP),
    )(dy, wout)
    ge, dacat = _chunk_scan_bwd(ds, s, acat)

    def body_in(u_ref, dy_ref, d_ref, ge_ref, w_ref, du_ref, dd_ref):
        xs = [_step_rows(u_ref, j, nc).astype(MXU_DTYPE) for j in range(L)]
        dys = [_step_rows(dy_ref, tt, nc).astype(MXU_DTYPE) for tt in range(L)]
        ge = ge_ref[...].astype(MXU_DTYPE)
        for i in range(L):
            dyc = jnp.concatenate(dys[i:], axis=1)
            taps = jnp.concatenate([d_ref[tt - i] for tt in range(i, L)], axis=1).astype(MXU_DTYPE)
            du_ref[pl.ds(i, nc, stride=L), :] = (
                _dot_nt(dyc, taps) + _dot_nt(ge, w_ref[i * GB_LANES:(i + 1) * GB_LANES, :]))
        for j in range(L):
            m = _dot_tn(xs[j], jnp.concatenate(dys[j:], axis=1))
            for tau in range(L - j):
                part = m[:, tau * GB_LANES:(tau + 1) * GB_LANES]
                if j == 0:
                    dd_ref[tau] = part
                else:
                    dd_ref[tau] += part
        same = _lane_group((GB_LANES, GB_LANES), 0, 4) == _lane_group((GB_LANES, GB_LANES), 1, 4)
        for tau in range(L):
            dd_ref[tau] = _fold_lanes(jnp.where(same, dd_ref[tau], 0.0), (64, 32, 16))

    du, ddblk = pl.pallas_call(
        body_in, name="ssm_bwd_in", grid=(N_GB,),
        in_specs=[lanes, lanes, taps_spec, state,
                  pl.BlockSpec((None, L * GB_LANES, STATE_W), lambda b: (b, 0, 0))],
        out_specs=[lanes, taps_spec],
        out_shape=[jax.ShapeDtypeStruct((t, D_SSM), F32), jax.ShapeDtypeStruct(dblk.shape, F32)],
        compiler_params=pltpu.CompilerParams(dimension_semantics=("parallel",), vmem_limit_bytes=VMEM_CAP),
    )(proj, dy, dblk, ge, wend)

    def body_w(u_ref, dy_ref, ge_ref, s_ref, dwe_ref, dwo_ref):
        ge = ge_ref[...].astype(MXU_DTYPE)
        sb = s_ref[...].astype(MXU_DTYPE)
        keep = _lane_group((GB_LANES, STATE_W), 0, 4) == _lane_group((GB_LANES, STATE_W), 1, 6)
        low = lax.broadcasted_iota(jnp.int32, (GB_LANES, 2 * SSM_STATE), 1) < SSM_STATE

        def fold_state(z):
            z = z[:, :GB_STATE // 2] + z[:, GB_STATE // 2:]
            z = z[:, :GB_STATE // 4] + z[:, GB_STATE // 4:]
            return _fold_lanes(z, (SSM_STATE,))

        for j in range(L):
            z = jnp.where(keep, _dot_tn(_step_rows(u_ref, j, nc).astype(MXU_DTYPE), ge), 0.0)
            dwe_ref[j * GB_LANES:(j + 1) * GB_LANES, :] = jnp.where(
                low, fold_state(z[:, :GB_STATE]), fold_state(z[:, GB_STATE:]))
        own = _lane_group((STATE_W, GB_LANES), 0, 6) == _lane_group((STATE_W, GB_LANES), 1, 4)
        chunk = lax.broadcasted_iota(jnp.int32, (STATE_W, GB_LANES), 1) >> 4
        for half in range(L // GROUP_BLOCK):
            acc = jnp.zeros((STATE_W, GB_LANES), F32)
            for k in range(GROUP_BLOCK):
                tt = half * GROUP_BLOCK + k
                z = jnp.where(own, _dot_tn(sb, _step_rows(dy_ref, tt, nc).astype(MXU_DTYPE)), 0.0)
                acc = acc + jnp.where(chunk == k, _fold_lanes(z, (64, 32, 16)), 0.0)
            dwo_ref[:, half * GB_LANES:(half + 1) * GB_LANES] = acc

    dwend, dwout = pl.pallas_call(
        body_w, name="ssm_bwd_w", grid=(N_GB,),
        in_specs=[lanes, lanes, state, state],
        out_specs=[pl.BlockSpec((None, L * GB_LANES, 2 * SSM_STATE), lambda b: (b, 0, 0)),
                   pl.BlockSpec((None, STATE_W, L * SSM_GROUP), lambda b: (b, 0, 0))],
        out_shape=[jax.ShapeDtypeStruct((N_GB, L * GB_LANES, 2 * SSM_STATE), F32),
                   jax.ShapeDtypeStruct((N_GB, STATE_W, L * SSM_GROUP), F32)],
        compiler_params=pltpu.CompilerParams(dimension_semantics=("parallel",), vmem_limit_bytes=VMEM_CAP),
    )(proj, dy, ge, s)
    return du, (ddblk[:, :, :, :SSM_GROUP], dwend, dwout, dacat)


def _t5_bucket(dist):
    max_exact = N_BUCKETS // 2
    is_small = dist < max_exact
    d = jnp.maximum(dist, 1).astype(F32)
    large = max_exact + (jnp.log(d / max_exact) / math.log(MAX_DISTANCE / max_exact)
                         * (N_BUCKETS - max_exact)).astype(jnp.int32)
    large = jnp.minimum(large, N_BUCKETS - 1)
    return jnp.where(is_small, dist, large)


def _band_bias(rel_bias_table):
    i = jnp.arange(BLOCK)[:, None]
    j = jnp.arange(BLOCK)[None, :]
    bucket = _t5_bucket(jnp.where(j > i, BLOCK + i - j, i - j))
    onehot = (bucket[:, :, None] == jnp.arange(N_BUCKETS)[None, None, :]).astype(F32)
    return jnp.einsum("qsb,bh->hqs", onehot, rel_bias_table, precision=lax.Precision.HIGHEST)


KV_PAIR = 2
HEADS_PER_STEP = KV_PAIR * Q_PER_KV
Q_LANES = HEADS_PER_STEP * HEAD_DIM
SLAB = 2 * HEAD_DIM
Q_COL0 = OFF_Q * COL // Q_LANES
K_COL0 = OFF_K * COL // SLAB
V_COL0 = OFF_V * COL // SLAB


def _attn_specs():
    q_spec = pl.BlockSpec((BLOCK, Q_LANES), lambda m, n: (n, Q_COL0 + m))
    k_prev = pl.BlockSpec((BLOCK, SLAB), lambda m, n: (jnp.maximum(n - 1, 0), K_COL0 + m))
    k_cur = pl.BlockSpec((BLOCK, SLAB), lambda m, n: (n, K_COL0 + m))
    v_prev = pl.BlockSpec((BLOCK, SLAB), lambda m, n: (jnp.maximum(n - 1, 0), V_COL0 + m))
    v_cur = pl.BlockSpec((BLOCK, SLAB), lambda m, n: (n, V_COL0 + m))
    bias_spec = pl.BlockSpec((HEADS_PER_STEP, BLOCK, BLOCK), lambda m, n: (m, 0, 0))
    sink_spec = pl.BlockSpec(memory_space=pltpu.SMEM)
    wide = pl.BlockSpec((BLOCK, Q_LANES), lambda m, n: (n, m))
    pair = pl.BlockSpec((BLOCK, SLAB), lambda m, n: (n, m))
    return [sink_spec, q_spec, k_prev, k_cur, v_prev, v_cur, bias_spec], wide, pair


def _low_lanes(shape):
    return lax.broadcasted_iota(jnp.int32, shape, 1) < HEAD_DIM


def _pair_halves(ref):
    kb = ref[...]
    sw = pltpu.roll(kb, HEAD_DIM, 1)
    lo = _low_lanes(kb.shape)
    zero = jnp.zeros_like(kb)
    first = (jnp.where(lo, kb, zero).astype(MXU_DTYPE), jnp.where(lo, zero, sw).astype(MXU_DTYPE))
    second = (jnp.where(lo, sw, zero).astype(MXU_DTYPE), jnp.where(lo, zero, kb).astype(MXU_DTYPE))
    return first, second


def _fold_pair(acc):
    f = [x + pltpu.roll(x, HEAD_DIM, 1) for x in acc]
    return jnp.where(_low_lanes(f[0].shape), f[0], f[1])


def _from_prev(n):
    row = lax.broadcasted_iota(jnp.int32, (BLOCK, BLOCK), 0)
    col = lax.broadcasted_iota(jnp.int32, (BLOCK, BLOCK), 1)
    prev = col > row
    return prev, jnp.where(jnp.logical_and(n == 0, prev), NEG_INF, 0.0)


def _attn_probs(q, k_prev, k_cur, prev, bias, edge, sink):
    s = jnp.where(prev, _dot_nt(q, k_prev), _dot_nt(q, k_cur)) * ATTN_SCALE + bias + edge
    m = jnp.maximum(jnp.max(s, axis=1, keepdims=True), sink)
    e = jnp.exp(s - m)
    es = jnp.exp(sink - m)
    inv = 1.0 / (jnp.sum(e, axis=1, keepdims=True) + es)
    return e * inv, es * inv


def _attn_forward(proj, bias, sinks):
    t = proj.shape[0]
    in_specs, wide, _ = _attn_specs()

    def body(sink_ref, q_ref, kp_ref, kc_ref, vp_ref, vc_ref, bias_ref, o_ref):
        m, n = pl.program_id(0), pl.program_id(1)
        kp, kc, vp, vc = (_pair_halves(r) for r in (kp_ref, kc_ref, vp_ref, vc_ref))
        prev, edge = _from_prev(n)
        for s in range(HEADS_PER_STEP // 2):
            a = s // (Q_PER_KV // 2)
            q = q_ref[:, s * SLAB:(s + 1) * SLAB].astype(MXU_DTYPE)
            out = None
            for e in range(2):
                h = 2 * s + e
                p, _ = _attn_probs(q, kp[a][e], kc[a][e], prev, bias_ref[h], edge, sink_ref[m * HEADS_PER_STEP + h])
                o = _dot(jnp.where(prev, p, 0.0), vp[a][e]) + _dot(jnp.where(prev, 0.0, p), vc[a][e])
                out = o if out is None else out + o
            o_ref[:, s * SLAB:(s + 1) * SLAB] = out

    return pl.pallas_call(
        body, name="attn_fwd", grid=(N_KV_HEADS // KV_PAIR, t // BLOCK),
        in_specs=in_specs, out_specs=wide,
        out_shape=jax.ShapeDtypeStruct((t, D_ATTN), F32),
        compiler_params=pltpu.CompilerParams(dimension_semantics=("parallel", "arbitrary")),
    )(sinks, proj, proj, proj, proj, proj, bias)


def _attn_backward(proj, bias, sinks, d_attn):
    t = proj.shape[0]
    in_specs, wide, pair = _attn_specs()
    bias_spec = in_specs[-1]
    sink_out = pl.BlockSpec((HEADS_PER_STEP, 8, 128), lambda m, n: (m, 0, 0))

    def body(sink_ref, q_ref, kp_ref, kc_ref, vp_ref, vc_ref, bias_ref, do_ref,
             dq_ref, dka_ref, dkb_ref, dva_ref, dvb_ref, dbias_ref, dsink_ref):
        m, n = pl.program_id(0), pl.program_id(1)

        @pl.when(n == 0)
        def _():
            dbias_ref[...] = jnp.zeros_like(dbias_ref)
            dsink_ref[...] = jnp.zeros_like(dsink_ref)

        kp, kc, vp, vc = (_pair_halves(r) for r in (kp_ref, kc_ref, vp_ref, vc_ref))
        prev, edge = _from_prev(n)
        lo = _low_lanes((BLOCK, SLAB))
        dk = [[jnp.zeros((BLOCK, SLAB), F32) for _ in range(KV_PAIR)] for _ in range(2)]
        dv = [[jnp.zeros((BLOCK, SLAB), F32) for _ in range(KV_PAIR)] for _ in range(2)]
        for s in range(HEADS_PER_STEP // 2):
            a = s // (Q_PER_KV // 2)
            q = q_ref[:, s * SLAB:(s + 1) * SLAB].astype(MXU_DTYPE)
            do = do_ref[:, s * SLAB:(s + 1) * SLAB].astype(MXU_DTYPE)
            dq = None
            rk, rv = [[], []], [[], []]
            for e in range(2):
                h = 2 * s + e
                p, ps = _attn_probs(q, kp[a][e], kc[a][e], prev, bias_ref[h], edge, sink_ref[m * HEADS_PER_STEP + h])
                dp = jnp.where(prev, _dot_nt(do, vp[a][e]), _dot_nt(do, vc[a][e]))
                delta = jnp.sum(p * dp, axis=1, keepdims=True)
                ds = p * (dp - delta)
                dbias_ref[h] += ds
                dsink_ref[h] += jnp.broadcast_to(jnp.sum(-ps * delta, axis=0, keepdims=True), (8, 128))
                parts = ((jnp.where(prev, ds, 0.0), jnp.where(prev, p, 0.0), kp[a][e]),
                         (jnp.where(prev, 0.0, ds), jnp.where(prev, 0.0, p), kc[a][e]))
                for which, (ds_part, p_part, keys) in enumerate(parts):
                    dqe = _dot(ds_part, keys)
                    dq = dqe if dq is None else dq + dqe
                    rk[which].append(_dot_tn(ds_part, q))
                    rv[which].append(_dot_tn(p_part, do))
            dq_ref[:, s * SLAB:(s + 1) * SLAB] = (dq * ATTN_SCALE).astype(dq_ref.dtype)
            for which in range(2):
                dk[which][a] = dk[which][a] + jnp.where(lo, rk[which][0], rk[which][1])
                dv[which][a] = dv[which][a] + jnp.where(lo, rv[which][0], rv[which][1])
        dkb_ref[...] = _fold_pair(dk[0]) * ATTN_SCALE
        dka_ref[...] = _fold_pair(dk[1]) * ATTN_SCALE
        dvb_ref[...] = _fold_pair(dv[0])
        dva_ref[...] = _fold_pair(dv[1])

    kv_shape = jax.ShapeDtypeStruct((t, D_KV), F32)
    return pl.pallas_call(
        body, name="attn_bwd", grid=(N_KV_HEADS // KV_PAIR, t // BLOCK),
        in_specs=in_specs + [wide],
        out_specs=[wide, pair, pair, pair, pair, bias_spec, sink_out],
        out_shape=[jax.ShapeDtypeStruct((t, D_ATTN), MXU_DTYPE), kv_shape, kv_shape, kv_shape, kv_shape,
                   jax.ShapeDtypeStruct(bias.shape, F32), jax.ShapeDtypeStruct((N_Q_HEADS, 8, 128), F32)],
        compiler_params=pltpu.CompilerParams(dimension_semantics=("parallel", "arbitrary")),
    )(sinks, proj, proj, proj, proj, proj, bias, d_attn)


def _shift_blocks(cur, prev):
    return cur + jnp.concatenate([prev[BLOCK:], jnp.zeros_like(prev[:BLOCK])], axis=0)


def _mesh_pos():
    return lax.axis_index("x"), lax.axis_index("y"), lax.axis_index("c")


def _all_gather(x, *, name):
    def body(x_ref, out_ref, send_sems, recv_sems, local_sem):
        x, y, c = _mesh_pos()
        me, sibling = (x, y, c), (x, y, 1 - c)
        chips = [(1 - x, y), (x, 1 - y), (1 - x, 1 - y)]

        def slot(px, py, pc):
            return out_ref.at[4 * px + 2 * py + pc]

        def copy(k, block, to, src=None):
            return pltpu.make_async_remote_copy(
                src_ref=slot(*block) if src is None else src, dst_ref=slot(*block),
                send_sem=send_sems.at[k], recv_sem=recv_sems.at[k],
                device_id=to, device_id_type=pl.DeviceIdType.MESH)

        mine = pltpu.make_async_copy(x_ref, slot(*me), local_sem)
        mine.start()
        first = [copy(0, me, sibling, src=x_ref)]
        first += [copy(1 + j, me, (*chip, c), src=x_ref) for j, chip in enumerate(chips)]
        for cp in first:
            cp.start()
        passed = [copy(4 + j, (*chip, c), sibling) for j, chip in enumerate(chips)]
        for j, chip in enumerate(chips):
            copy(1 + j, (*chip, c), me).wait_recv()
            passed[j].start()
        copy(0, sibling, me).wait_recv()
        for j, chip in enumerate(chips):
            copy(4 + j, (*chip, 1 - c), me).wait_recv()
        for cp in first + passed:
            cp.wait_send()
        mine.wait()

    return pl.pallas_call(
        body, name=name,
        in_specs=[pl.BlockSpec(memory_space=pl.ANY)],
        out_specs=pl.BlockSpec(memory_space=pl.ANY),
        out_shape=jax.ShapeDtypeStruct((N_DEV,) + x.shape, x.dtype),
        scratch_shapes=[pltpu.SemaphoreType.DMA((7,)), pltpu.SemaphoreType.DMA((7,)), pltpu.SemaphoreType.DMA],
    )(x)


_HBM = pl.BlockSpec(memory_space=pltpu.HBM)
_SEM = pl.BlockSpec(memory_space=pltpu.SEMAPHORE)
_DATAFLOW = pltpu.SideEffectType.DATAFLOW_SIDE_EFFECTING


def _peers():
    x, y, c = _mesh_pos()
    others = []
    for k in range(1, N_DEV):
        px, py, pc = x ^ (k >> 2), y ^ ((k >> 1) & 1), c ^ (k & 1)
        others.append(((px, py, pc), 4 * px + 2 * py + pc))
    return 4 * x + 2 * y + c, others


def _scatter_start(x, *, name):
    me, _ = _peers()
    own = lax.dynamic_index_in_dim(x, me, 0, keepdims=False)
    land = lax.dynamic_update_index_in_dim(lax.empty(x.shape, x.dtype), own, me, 0)

    def body(x_ref, land_ref, send_sems, recv_sems, x_thru, land_thru, token):
        mine, others = _peers()
        for k, (pos, idx) in enumerate(others):
            pltpu.make_async_remote_copy(
                src_ref=x_ref.at[idx], dst_ref=land_ref.at[mine],
                send_sem=send_sems.at[k], recv_sem=recv_sems.at[k],
                device_id=pos, device_id_type=pl.DeviceIdType.MESH).start()
        token[...] = jnp.zeros_like(token)

    return pl.pallas_call(
        body, name=name,
        out_shape=(pltpu.SemaphoreType.DMA((N_DEV - 1,)), pltpu.SemaphoreType.DMA((N_DEV - 1,)),
                   pltpu.HBM(x.shape, x.dtype), pltpu.HBM(x.shape, x.dtype), jax.ShapeDtypeStruct((8, 128), F32)),
        in_specs=(_HBM, _HBM), out_specs=(_SEM, _SEM, _HBM, _HBM, pl.BlockSpec(memory_space=pltpu.VMEM)),
        input_output_aliases={0: 2, 1: 3},
        compiler_params=pltpu.CompilerParams(has_side_effects=_DATAFLOW),
    )(pltpu.with_memory_space_constraint(x, pltpu.HBM), pltpu.with_memory_space_constraint(land, pltpu.HBM))


def _scatter_wait(started, after, *, name):
    send_sems, recv_sems, x_thru, land_thru, _ = started

    def body(x_ref, land_ref, send_sems, recv_sems, after_ref, x_dead, got_ref):
        _, others = _peers()
        for k, (pos, idx) in enumerate(others):
            copy = pltpu.make_async_remote_copy(
                src_ref=x_ref.at[idx], dst_ref=land_ref.at[idx],
                send_sem=send_sems.at[k], recv_sem=recv_sems.at[k],
                device_id=pos, device_id_type=pl.DeviceIdType.MESH)
            copy.wait_send()
            copy.wait_recv()

    return pl.pallas_call(
        body, name=name,
        out_shape=(pltpu.HBM(x_thru.shape, x_thru.dtype), pltpu.HBM(land_thru.shape, land_thru.dtype)),
        in_specs=(_HBM, _HBM, _SEM, _SEM, pl.BlockSpec(memory_space=pl.ANY)), out_specs=(_HBM, _HBM),
        input_output_aliases={0: 0, 1: 1},
        compiler_params=pltpu.CompilerParams(has_side_effects=_DATAFLOW),
    )(x_thru, land_thru, send_sems, recv_sems, after)[1]


def _adamw_math(w, g, m, v):
    m = ADAM_B1 * m + (1.0 - ADAM_B1) * g
    v = ADAM_B2 * v + (1.0 - ADAM_B2) * (g * g)
    m_hat = m / (1.0 - ADAM_B1 ** ADAM_STEP)
    v_hat = v / (1.0 - ADAM_B2 ** ADAM_STEP)
    delta = -ADAM_LR * (m_hat / (jnp.sqrt(v_hat) + ADAM_EPS) + ADAM_WD * w)
    return delta, m, v


def _adamw_reduce(parts, w, m, v, *, name, tr):
    r, c = w.shape
    tr = min(tr, r)
    spec = pl.BlockSpec((tr, c), lambda i: (i, 0))

    def body(p_ref, w_ref, m_ref, v_ref, g_ref, d_ref, nm_ref, nv_ref):
        g = p_ref[0].astype(F32)
        for s in range(1, N_DEV):
            g = g + p_ref[s].astype(F32)
        delta, nm, nv = _adamw_math(w_ref[...], g, m_ref[...], v_ref[...])
        g_ref[...] = g
        d_ref[...] = delta
        nm_ref[...] = nm
        nv_ref[...] = nv

    return pl.pallas_call(
        body, name=name, grid=(r // tr,),
        in_specs=[pl.BlockSpec((N_DEV, tr, c), lambda i: (0, i, 0)), spec, spec, spec],
        out_specs=[spec] * 4,
        out_shape=[jax.ShapeDtypeStruct((r, c), F32)] * 4,
        compiler_params=pltpu.CompilerParams(
            dimension_semantics=("parallel",),
            vmem_limit_bytes=_vmem_limit(2 * 15 * _nbytes((tr, c), F32))),
    )(parts, w, m, v)


SMALL = ["ssm_lambda_re", "ssm_lambda_im", "ssm_b_re", "ssm_b_im", "ssm_c_re", "ssm_c_im", "ssm_d",
         "ssm_log_step", "attn_sinks", "rel_bias_table", "ln_gain", "ln_bias"]


def _pack(arrs):
    flat = jnp.concatenate([a.reshape(-1) for a in arrs])
    pad = (-flat.shape[0]) % 1024
    return jnp.pad(flat, (0, pad)).reshape(-1, 128)


def _unpack(packed, like):
    flat = packed.reshape(-1)
    out, pos = [], 0
    for a in like:
        out.append(flat[pos:pos + a.size].reshape(a.shape))
        pos += a.size
    return out


def kernel(x, w_in, ssm_lambda_re, ssm_lambda_im, ssm_b_re, ssm_b_im, ssm_c_re, ssm_c_im, ssm_d, ssm_log_step, w_glu, attn_sinks, rel_bias_table, w_branch_ssm, w_branch_attn, w_out, ln_gain, ln_bias, loss_target, m_w_in, m_ssm_lambda_re, m_ssm_lambda_im, m_ssm_b_re, m_ssm_b_im, m_ssm_c_re, m_ssm_c_im, m_ssm_d, m_ssm_log_step, m_w_glu, m_attn_sinks, m_rel_bias_table, m_w_branch_ssm, m_w_branch_attn, m_w_out, m_ln_gain, m_ln_bias, v_w_in, v_ssm_lambda_re, v_ssm_lambda_im, v_ssm_b_re, v_ssm_b_im, v_ssm_c_re, v_ssm_c_im, v_ssm_d, v_ssm_log_step, v_w_glu, v_attn_sinks, v_rel_bias_table, v_w_branch_ssm, v_w_branch_attn, v_w_out, v_ln_gain, v_ln_bias):
    t = x.shape[1]
    xs = x[0]
    target = loss_target[0]
    col_in = w_in.shape[2]
    col_br = w_glu.shape[2]
    row_out = w_out.shape[1]

    g_in = _all_gather(w_in[0].astype(MXU_DTYPE), name="gather_w_in")
    three = jnp.concatenate([w_glu[0], w_branch_ssm[0], w_branch_attn[0]], axis=0).astype(MXU_DTYPE)
    g_three = _all_gather(three, name="gather_w_1024")
    g_out = _all_gather(w_out[0].astype(MXU_DTYPE), name="gather_w_out")
    win = g_in.transpose(1, 0, 2).reshape(D_MODEL, D_IN)
    three_full = g_three.transpose(1, 0, 2).reshape(3 * D_SSM, N_DEV * col_br)
    wglu, wbs, wba = three_full[:D_SSM], three_full[D_SSM:2 * D_SSM], three_full[2 * D_SSM:]
    wout = g_out.reshape(D_MODEL, D_MODEL)

    ssm_params = (ssm_lambda_re[0], ssm_lambda_im[0], ssm_b_re[0], ssm_b_im[0], ssm_c_re[0], ssm_c_im[0],
                  ssm_log_step[0])
    mats, mats_vjp = jax.vjp(_ssm_matrices, *ssm_params)
    bias, bias_vjp = jax.vjp(_band_bias, rel_bias_table)
    sinks = attn_sinks[0]
    d_skip = ssm_d

    x_mx = xs.astype(MXU_DTYPE)
    proj = _mm(x_mx, win, name="in_proj", tm=2048, tn=512, tk=2048)
    mats_mx = _ssm_expand(*mats[:3]) + (mats[3],)
    y_conv, states = _ssm_forward(proj, mats_mx)

    def f_gelu(yv, u, d):
        ys = yv + d * u
        return ys, _gelu(ys)

    y_ssm, glu_in = _ew(f_gelu, [(y_conv, "row", 0), (proj, "row", OFF_U), (d_skip, "vec", 0)],
                        [(COL, F32, "row"), (COL, MXU_DTYPE, "row")], rows=t, cw=COL, ncb=4, tr=1024, name="ssm_gelu")
    glu = _mm(glu_in, wglu, name="glu_proj", tm=2048, tn=512, tk=1024)

    def f_hssm(ga, gb, z):
        return (ga * _sigmoid(gb) * _silu(z),)

    (h_ssm,) = _ew(f_hssm, [(glu, "row", 0), (glu, "row", 4), (proj, "row", OFF_ZS)],
                   [(COL, MXU_DTYPE, "row")], rows=t, cw=COL, ncb=4, tr=1024, name="ssm_gate")

    attn = _attn_forward(proj, bias, sinks)

    def f_hattn(a, z):
        return (a * _silu(z),)

    (h_attn,) = _ew(f_hattn, [(attn, "row", 0), (proj, "row", OFF_ZA)], [(COL, MXU_DTYPE, "row")],
                    rows=t, cw=COL, ncb=4, tr=1024, name="attn_gate")
    p_ssm = _mm(h_ssm, wbs, name="branch_ssm", tm=2048, tn=512, tk=1024)
    p_attn = _mm(h_attn, wba, name="branch_attn", tm=2048, tn=512, tk=1024)

    def f_merge(ps, pa, ls, la):
        return (_sigmoid(ls) * ps + _sigmoid(la) * pa,)

    (merged,) = _ew(f_merge, [(p_ssm, "row", 0), (p_attn, "row", 0), (proj, "row", OFF_GS), (proj, "row", OFF_GA)],
                    [(COL, MXU_DTYPE, "row")], rows=t, cw=COL, ncb=8, tr=1024, name="merge")
    out = _mm(merged, wout, name="out_proj", tm=2048, tn=512, tk=2048)

    def f_norm(xv, ov, tg, gain, lbias):
        r = DEEPNORM_ALPHA * xv + ov
        mu = jnp.mean(r, axis=1, keepdims=True)
        cen = r - mu
        var = jnp.mean(cen * cen, axis=1, keepdims=True)
        rstd = lax.rsqrt(var + LN_EPS)
        xhat = cen * rstd
        yv = xhat * gain + lbias
        diff = yv - tg
        row_loss = 0.5 * jnp.mean(diff * diff, axis=1, keepdims=True)
        loss = jnp.broadcast_to(jnp.sum(row_loss, axis=0, keepdims=True), (1, 128))
        dy = diff * (1.0 / D_MODEL)
        dgain = jnp.sum(dy * xhat, axis=0, keepdims=True)
        dbias = jnp.sum(dy, axis=0, keepdims=True)
        dxh = dy * gain
        dr = rstd * (dxh - jnp.mean(dxh, axis=1, keepdims=True) - xhat * jnp.mean(dxh * xhat, axis=1, keepdims=True))
        return dr, loss, dgain, dbias

    dr, loss_part, g_ln_gain, g_ln_bias = _ew(
        f_norm, [(xs, "row", 0), (out, "row", 0), (target, "row", 0), (ln_gain, "vec", 0), (ln_bias, "vec", 0)],
        [(D_MODEL, F32, "row"), (128, F32, "acc"), (D_MODEL, F32, "acc"), (D_MODEL, F32, "acc")],
        rows=t, cw=D_MODEL, ncb=1, tr=256, name="norm_loss")

    def scatter_cols(g, cols):
        return g.reshape(g.shape[0], N_DEV, cols).transpose(1, 0, 2)

    gw_out = _mm(merged, dr, ta=True, out_dtype=WIRE_DTYPE, name="grad_w_out", tm=2048, tn=512, tk=1024)
    sent_out = _scatter_start(gw_out.reshape(N_DEV, row_out, D_MODEL), name="scatter_g_out_start")
    d_merged = _mm(dr, wout, tb=True, name="d_merged", tm=2048, tn=512, tk=2048, after=(sent_out[4],))

    def b_merge(dm, ps, pa, ls, la):
        gs, ga = _sigmoid(ls), _sigmoid(la)
        return dm * gs, dm * ga, dm * ps * gs * (1.0 - gs), dm * pa * ga * (1.0 - ga)

    dp_ssm, dp_attn, dgl_s, dgl_a = _ew(
        b_merge, [(d_merged, "row", 0), (p_ssm, "row", 0), (p_attn, "row", 0), (proj, "row", OFF_GS), (proj, "row", OFF_GA)],
        [(COL, MXU_DTYPE, "row")] * 4, rows=t, cw=COL, ncb=8, tr=1024, name="merge_bwd")
    gw_bs = _mm(h_ssm, dp_ssm, ta=True, out_dtype=WIRE_DTYPE, name="grad_w_branch_ssm", tm=1024, tn=512, tk=1024)
    gw_ba = _mm(h_attn, dp_attn, ta=True, out_dtype=WIRE_DTYPE, name="grad_w_branch_attn", tm=1024, tn=512, tk=1024)
    dh_ssm = _mm(dp_ssm, wbs, tb=True, name="d_h_ssm", tm=2048, tn=512, tk=2048)
    dh_attn = _mm(dp_attn, wba, tb=True, name="d_h_attn", tm=2048, tn=512, tk=2048)

    def b_hssm(dh, ga, gb, z):
        sg = _sigmoid(gb)
        dgate = dh * _silu(z)
        return dgate * sg, dgate * ga * sg * (1.0 - sg), dh * ga * sg * _silu_grad(z)

    dglu_a, dglu_b, dz_ssm = _ew(b_hssm, [(dh_ssm, "row", 0), (glu, "row", 0), (glu, "row", 4), (proj, "row", OFF_ZS)],
                                 [(COL, MXU_DTYPE, "row")] * 3, rows=t, cw=COL, ncb=4, tr=1024, name="ssm_gate_bwd")
    dglu = jnp.concatenate([dglu_a, dglu_b], axis=1)
    gw_glu = _mm(glu_in, dglu, ta=True, out_dtype=WIRE_DTYPE, name="grad_w_glu", tm=1024, tn=512, tk=1024)
    sent_three = _scatter_start(scatter_cols(jnp.concatenate([gw_glu, gw_bs, gw_ba], axis=0), col_br),
                                name="scatter_g_1024_start")
    dglu_in = _mm(dglu, wglu, tb=True, name="d_glu_in", tm=2048, tn=512, tk=2048, after=(sent_three[4],))

    def b_gelu(dgi, ys, u):
        dys = dgi * _gelu_grad(ys)
        return dys, jnp.sum(dys * u, axis=0, keepdims=True)

    dy_ssm, g_ssm_d = _ew(b_gelu, [(dglu_in, "row", 0), (y_ssm, "row", 0), (proj, "row", OFF_U)],
                          [(COL, F32, "row"), (COL, F32, "acc")],
                          rows=t, cw=COL, ncb=4, tr=1024, name="ssm_gelu_bwd")
    du_ssm, dmats = _ssm_backward(dy_ssm, proj, states, mats_mx)
    du = (du_ssm + d_skip * dy_ssm).astype(MXU_DTYPE)
    g_lre, g_lim, g_bre, g_bim, g_cre, g_cim, g_lstep = mats_vjp(dmats)

    def b_hattn(dh, a, z):
        return dh * _silu(z), dh * a * _silu_grad(z)

    d_attn, dz_attn = _ew(b_hattn, [(dh_attn, "row", 0), (attn, "row", 0), (proj, "row", OFF_ZA)],
                          [(COL, F32, "row"), (COL, MXU_DTYPE, "row")], rows=t, cw=COL, ncb=4, tr=1024, name="attn_gate_bwd")
    dq, dka, dkb, dva, dvb, dbias, dsink = _attn_backward(proj, bias, sinks, d_attn)
    dk = _shift_blocks(dka, dkb).astype(MXU_DTYPE)
    dv = _shift_blocks(dva, dvb).astype(MXU_DTYPE)
    (g_table,) = bias_vjp(dbias)
    g_sinks = dsink[:, 0, 0]

    dproj = jnp.concatenate([du, dz_ssm, dq, dk, dv, dz_attn, dgl_s, dgl_a], axis=1)
    gw_in = _mm(x_mx, dproj, ta=True, out_dtype=WIRE_DTYPE, name="grad_w_in", tm=2048, tn=512, tk=1024)
    sent_in = _scatter_start(scatter_cols(gw_in, col_in), name="scatter_g_in_start")
    grad_x = _mm(dproj, win, tb=True, add=dr, add_scale=DEEPNORM_ALPHA, name="grad_x", tm=1024, tn=1024, tk=2176,
                 after=(sent_in[4],))

    parts_out = _scatter_wait(sent_out, grad_x, name="scatter_g_out_wait")
    parts_three = _scatter_wait(sent_three, parts_out, name="scatter_g_1024_wait")
    parts_in = _scatter_wait(sent_in, parts_three, name="scatter_g_in_wait")

    o_in = _adamw_reduce(parts_in, w_in[0], m_w_in[0], v_w_in[0], name="adamw_w_in", tr=128)
    three_w = jnp.concatenate([w_glu[0], w_branch_ssm[0], w_branch_attn[0]], axis=0)
    three_m = jnp.concatenate([m_w_glu[0], m_w_branch_ssm[0], m_w_branch_attn[0]], axis=0)
    three_v = jnp.concatenate([v_w_glu[0], v_w_branch_ssm[0], v_w_branch_attn[0]], axis=0)
    o_three = _adamw_reduce(parts_three, three_w, three_m, three_v, name="adamw_w_1024", tr=512)
    o_out = _adamw_reduce(parts_out, w_out[0], m_w_out[0], v_w_out[0], name="adamw_w_out", tr=128)

    small_w = [ssm_lambda_re, ssm_lambda_im, ssm_b_re, ssm_b_im, ssm_c_re, ssm_c_im, ssm_d, ssm_log_step,
               attn_sinks, rel_bias_table, ln_gain, ln_bias]
    small_m = [m_ssm_lambda_re, m_ssm_lambda_im, m_ssm_b_re, m_ssm_b_im, m_ssm_c_re, m_ssm_c_im, m_ssm_d,
               m_ssm_log_step, m_attn_sinks, m_rel_bias_table, m_ln_gain, m_ln_bias]
    small_v = [v_ssm_lambda_re, v_ssm_lambda_im, v_ssm_b_re, v_ssm_b_im, v_ssm_c_re, v_ssm_c_im, v_ssm_d,
               v_ssm_log_step, v_attn_sinks, v_rel_bias_table, v_ln_gain, v_ln_bias]
    small_g = [g_lre, g_lim, g_bre, g_bim, g_cre, g_cim, g_ssm_d, g_lstep, g_sinks, g_table, g_ln_gain, g_ln_bias]
    parts_small = _all_gather(_pack(small_g), name="gather_g_small")
    o_small = _adamw_reduce(parts_small, _pack(small_w), _pack(small_m), _pack(small_v), name="adamw_small", tr=2160)
    sg, sd, sm, sv = [_unpack(o, small_w) for o in o_small]

    loss = lax.psum(loss_part[0, 0], MESH_AXES)

    def big(o, idx):
        g_in_, g_three_, g_out_ = o_in[idx], o_three[idx], o_out[idx]
        return {"w_in": g_in_[None], "w_glu": g_three_[None, :D_SSM], "w_branch_ssm": g_three_[None, D_SSM:2 * D_SSM],
                "w_branch_attn": g_three_[None, 2 * D_SSM:], "w_out": g_out_[None]}

    order = ["w_in", "ssm_lambda_re", "ssm_lambda_im", "ssm_b_re", "ssm_b_im", "ssm_c_re", "ssm_c_im", "ssm_d",
             "ssm_log_step", "w_glu", "attn_sinks", "rel_bias_table", "w_branch_ssm", "w_branch_attn", "w_out",
             "ln_gain", "ln_bias"]
    outs = [loss, grad_x[None]]
    for idx, small in enumerate([sg, sd, sm, sv]):
        table = big(None, idx)
        table.update(dict(zip(SMALL, small)))
        outs += [table[n] for n in order]
    return tuple(outs)
```

```python
import functools
import math

import jax
import jax.numpy as jnp
from jax import lax
from jax.experimental import pallas as pl
from jax.experimental.pallas import tpu as pltpu

F32 = jnp.float32
MXU_DTYPE = jnp.bfloat16
WIRE_DTYPE = jnp.bfloat16

D_MODEL = 2048
D_SSM = 1024
SSM_GROUP = 16
N_GROUPS = 64
SSM_STATE = 64
N_Q_HEADS = 16
N_KV_HEADS = 4
Q_PER_KV = 4
HEAD_DIM = 64
D_ATTN = 1024
D_KV = 256
WINDOW = 128
BLOCK = 128
N_BUCKETS = 32
MAX_DISTANCE = 128
D_IN = 8704
DEEPNORM_ALPHA = 2.0 ** 0.25
LN_EPS = 1e-5
NEG_INF = -1e30
ATTN_SCALE = HEAD_DIM ** -0.5

ADAM_LR = 0.001
ADAM_B1 = 0.9
ADAM_B2 = 0.999
ADAM_EPS = 1e-08
ADAM_WD = 0.01
ADAM_STEP = 10

N_DEV = 8
SSM_CHUNK = 16
GROUP_BLOCK = 8
N_GB = N_GROUPS // GROUP_BLOCK
GB_LANES = GROUP_BLOCK * SSM_GROUP
GB_STATE = GROUP_BLOCK * SSM_STATE
COL = 256
OFF_U, OFF_ZS, OFF_Q, OFF_K, OFF_V, OFF_ZA, OFF_GS, OFF_GA = 0, 4, 8, 12, 13, 14, 18, 26

VMEM_CAP = 56 * 1024 * 1024
MESH_AXES = ("x", "y", "c")


def _vmem_limit(block_bytes):
    return int(min(max(3 * block_bytes, 16 * 1024 * 1024), VMEM_CAP))


def _nbytes(shape, dtype):
    return math.prod(shape) * jnp.dtype(dtype).itemsize


def _tile(n, pref):
    if n <= pref:
        return n
    t = (pref // 128) * 128
    while t >= 128:
        if n % t == 0:
            return t
        t -= 128
    return n


def _mm(a, b, *, name, ta=False, tb=False, out_dtype=F32, tm=1024, tn=512, tk=512, add=None, add_scale=1.0,
        after=()):
    squeeze = a.ndim == 2
    if squeeze:
        a, b = a[None], b[None]
        if add is not None:
            add = add[None]
    nb = a.shape[0]
    m, k = (a.shape[2], a.shape[1]) if ta else (a.shape[1], a.shape[2])
    n = b.shape[1] if tb else b.shape[2]
    tm, tn, tk = _tile(m, tm), _tile(n, tn), _tile(k, tk)
    nk = k // tk
    dn = (((0 if ta else 1,), (1 if tb else 0,)), ((), ()))

    a_spec = (pl.BlockSpec((None, tk, tm), lambda g, i, j, kk: (g, kk, i)) if ta
              else pl.BlockSpec((None, tm, tk), lambda g, i, j, kk: (g, i, kk)))
    b_spec = (pl.BlockSpec((None, tn, tk), lambda g, i, j, kk: (g, j, kk)) if tb
              else pl.BlockSpec((None, tk, tn), lambda g, i, j, kk: (g, kk, j)))
    o_spec = pl.BlockSpec((None, tm, tn), lambda g, i, j, kk: (g, i, j))
    in_specs = [a_spec, b_spec]
    operands = [a, b]
    if add is not None:
        in_specs.append(o_spec)
        operands.append(add)
    for tok in after:
        in_specs.append(pl.BlockSpec(memory_space=pl.ANY))
        operands.append(tok)
    n_in = len(operands)

    def body(*refs):
        a_ref, b_ref = refs[0], refs[1]
        add_ref = refs[2] if add is not None else None
        o_ref = refs[n_in]
        acc_ref = refs[-1]
        kk = pl.program_id(3)
        part = lax.dot_general(a_ref[...].astype(MXU_DTYPE), b_ref[...].astype(MXU_DTYPE), dn,
                               preferred_element_type=F32)

        @pl.when(kk == 0)
        def _():
            acc_ref[...] = part

        @pl.when(kk > 0)
        def _():
            acc_ref[...] += part

        @pl.when(kk == nk - 1)
        def _():
            r = acc_ref[...]
            if add_ref is not None:
                r = r + add_scale * add_ref[...]
            o_ref[...] = r.astype(out_dtype)

    blocks = (_nbytes((tm, tk), a.dtype) + _nbytes((tk, tn), b.dtype) + _nbytes((tm, tn), out_dtype)
              + (_nbytes((tm, tn), F32) if add is not None else 0))
    out = pl.pallas_call(
        body,
        name=name,
        grid=(nb, m // tm, n // tn, nk),
        in_specs=in_specs,
        out_specs=o_spec,
        out_shape=jax.ShapeDtypeStruct((nb, m, n), out_dtype),
        scratch_shapes=[pltpu.VMEM((tm, tn), F32)],
        compiler_params=pltpu.CompilerParams(
            dimension_semantics=("parallel", "parallel", "parallel", "arbitrary"),
            vmem_limit_bytes=_vmem_limit(2 * blocks + 2 * _nbytes((tm, tn), F32))),
    )(*operands)
    return out[0] if squeeze else out


def _ew(fn, ins, outs, *, rows, cw, ncb, tr, name):
    tr = min(tr, rows)
    n_in = len(ins)

    def row_map(off):
        return lambda j, i: (i, off + j)

    def vec_map(off):
        return lambda j, i: (0, off + j)

    in_specs = []
    for arr, kind, off in ins:
        if kind == "row":
            in_specs.append(pl.BlockSpec((tr, cw), row_map(off)))
        else:
            in_specs.append(pl.BlockSpec((1, cw), vec_map(off)))
    out_specs, out_shapes = [], []
    for bw, dt, kind in outs:
        if kind == "row":
            out_specs.append(pl.BlockSpec((tr, bw), row_map(0)))
            out_shapes.append(jax.ShapeDtypeStruct((rows, ncb * bw), dt))
        else:
            out_specs.append(pl.BlockSpec((1, bw), vec_map(0)))
            out_shapes.append(jax.ShapeDtypeStruct((1, ncb * bw), F32))

    def body(*refs):
        i = pl.program_id(1)
        vals = fn(*[r[...] for r in refs[:n_in]])
        for r, (bw, dt, kind), v in zip(refs[n_in:], outs, vals):
            if kind == "row":
                r[...] = v.astype(dt)
            else:
                @pl.when(i == 0)
                def _(r=r):
                    r[...] = jnp.zeros_like(r)

                r[...] += v

    blocks = sum(_nbytes((tr, cw), a.dtype) for a, kind, _ in ins if kind == "row")
    blocks += sum(_nbytes((tr, bw), dt) for bw, dt, kind in outs if kind == "row")
    res = pl.pallas_call(
        body,
        name=name,
        grid=(ncb, rows // tr),
        in_specs=in_specs,
        out_specs=out_specs,
        out_shape=out_shapes,
        compiler_params=pltpu.CompilerParams(
            dimension_semantics=("parallel", "arbitrary"),
            vmem_limit_bytes=_vmem_limit(4 * blocks)),
    )(*[a for a, _, _ in ins])
    return res


def _sigmoid(x):
    return 1.0 / (1.0 + jnp.exp(-x))


INV_SQRT2 = 0.7071067811865476
INV_SQRT_2PI = 0.3989422804014327


def _gelu(x):
    return 0.5 * x * (1.0 + lax.erf(x * INV_SQRT2))


def _gelu_grad(x):
    return 0.5 * (1.0 + lax.erf(x * INV_SQRT2)) + x * INV_SQRT_2PI * jnp.exp(-0.5 * x * x)


def _silu(x):
    return x * _sigmoid(x)


def _silu_grad(x):
    s = _sigmoid(x)
    return s * (1.0 + x * (1.0 - s))


def _ssm_matrices(lam_re, lam_im, b_re, b_im, c_re, c_im, log_step):
    L = SSM_CHUNK
    step = jnp.exp(log_step)[:, None]
    ea, eb = lam_re * step, lam_im * step
    mag = jnp.exp(ea)
    lbr, lbi = mag * jnp.cos(eb), mag * jnp.sin(eb)
    den = lam_re * lam_re + lam_im * lam_im
    nr, ni = lbr - 1.0, lbi
    cr = (nr * lam_re + ni * lam_im) / den
    ci = (ni * lam_re - nr * lam_im) / den
    bbr = cr[..., None] * b_re - ci[..., None] * b_im
    bbi = cr[..., None] * b_im + ci[..., None] * b_re
    taus = jnp.arange(L + 1, dtype=F32)[:, None, None]
    pmag = jnp.exp(taus * ea[None])
    pwr, pwi = pmag * jnp.cos(taus * eb[None]), pmag * jnp.sin(taus * eb[None])
    mr = c_re[None] * pwr[:L, :, None, :] - c_im[None] * pwi[:L, :, None, :]
    mi = c_re[None] * pwi[:L, :, None, :] + c_im[None] * pwr[:L, :, None, :]
    kk = jnp.sum(mr[..., None] * bbr[None, :, None] - mi[..., None] * bbi[None, :, None], axis=3)
    taps = jnp.transpose(kk.reshape(L, N_GB, GROUP_BLOCK, SSM_GROUP, SSM_GROUP), (1, 0, 2, 4, 3))
    taps = taps.reshape(N_GB, L, GB_LANES, SSM_GROUP)
    rev_r, rev_i = pwr[L - 1 - jnp.arange(L)], pwi[L - 1 - jnp.arange(L)]
    wer = rev_r[..., None] * bbr[None] - rev_i[..., None] * bbi[None]
    wei = rev_r[..., None] * bbi[None] + rev_i[..., None] * bbr[None]

    def rows_in(w):
        w = jnp.transpose(w.reshape(L, N_GB, GROUP_BLOCK, SSM_STATE, SSM_GROUP), (1, 0, 2, 4, 3))
        return w.reshape(N_GB, L * GB_LANES, SSM_STATE)

    wend = jnp.concatenate([rows_in(wer), rows_in(wei)], axis=2)
    m1r = c_re[None] * pwr[1:, :, None, :] - c_im[None] * pwi[1:, :, None, :]
    m1i = c_re[None] * pwi[1:, :, None, :] + c_im[None] * pwr[1:, :, None, :]

    def rows_out(m):
        m = jnp.transpose(m.reshape(L, N_GB, GROUP_BLOCK, SSM_GROUP, SSM_STATE), (1, 2, 4, 0, 3))
        return m.reshape(N_GB, GB_STATE, L * SSM_GROUP)

    wout = jnp.concatenate([rows_out(m1r), rows_out(-m1i)], axis=1)
    acat = jnp.concatenate([pwr[L].reshape(N_GB, 1, GB_STATE), pwi[L].reshape(N_GB, 1, GB_STATE)], axis=2)
    return taps, wend, wout, acat


def _lane_group(shape, axis, shift):
    return (lax.broadcasted_iota(jnp.int32, shape, axis) >> shift) & (GROUP_BLOCK - 1)


def _ssm_expand(taps, wend, wout):
    L = SSM_CHUNK
    taps = jnp.pad(taps, ((0, 0), (0, 0), (0, 0), (0, GB_LANES - SSM_GROUP)))

    def body(t_ref, we_ref, wo_ref, d_ref, web_ref, wob_ref):
        def rc(shape):
            return lax.broadcasted_iota(jnp.int32, shape, 0), lax.broadcasted_iota(jnp.int32, shape, 1)

        r, c = rc((GB_LANES, GB_LANES))
        spread = ((r < SSM_GROUP) & (r == (c & (SSM_GROUP - 1)))).astype(MXU_DTYPE)
        same = _lane_group((GB_LANES, GB_LANES), 0, 4) == _lane_group((GB_LANES, GB_LANES), 1, 4)
        for tau in range(L):
            full = jnp.dot(t_ref[tau].astype(MXU_DTYPE), spread, preferred_element_type=F32)
            d_ref[tau] = jnp.where(same, full, 0.0).astype(d_ref.dtype)
        r, c = rc((2 * SSM_STATE, STATE_W))
        part = (r >> 6) == (c >> 9)
        spread = (part & ((r & (SSM_STATE - 1)) == (c & (SSM_STATE - 1)))).astype(MXU_DTYPE)
        keep = _lane_group((GB_LANES, STATE_W), 0, 4) == _lane_group((GB_LANES, STATE_W), 1, 6)
        for j in range(L):
            rows = slice(j * GB_LANES, (j + 1) * GB_LANES)
            full = jnp.dot(we_ref[rows, :].astype(MXU_DTYPE), spread, preferred_element_type=F32)
            web_ref[rows, :] = jnp.where(keep, full, 0.0).astype(web_ref.dtype)
        r, c = rc((GB_LANES, GB_LANES))
        own = _lane_group((STATE_W, GB_LANES), 0, 6) == _lane_group((STATE_W, GB_LANES), 1, 4)
        for tt in range(L):
            half, k = tt // GROUP_BLOCK, tt % GROUP_BLOCK
            spread = (((r >> 4) == k) & ((r & (SSM_GROUP - 1)) == (c & (SSM_GROUP - 1)))).astype(MXU_DTYPE)
            src = wo_ref[:, half * GB_LANES:(half + 1) * GB_LANES].astype(MXU_DTYPE)
            full = jnp.dot(src, spread, preferred_element_type=F32)
            wob_ref[:, tt * GB_LANES:(tt + 1) * GB_LANES] = jnp.where(own, full, 0.0).astype(wob_ref.dtype)

    def spec(shape):
        return pl.BlockSpec((None,) + shape[1:], lambda b: (b,) + (0,) * (len(shape) - 1))

    out_shapes = [(N_GB, L, GB_LANES, GB_LANES), (N_GB, L * GB_LANES, STATE_W), (N_GB, STATE_W, L * GB_LANES)]
    return tuple(pl.pallas_call(
        body, name="ssm_expand", grid=(N_GB,),
        in_specs=[spec(taps.shape), spec(wend.shape), spec(wout.shape)],
        out_specs=[spec(s) for s in out_shapes],
        out_shape=[jax.ShapeDtypeStruct(s, MXU_DTYPE) for s in out_shapes],
        compiler_params=pltpu.CompilerParams(dimension_semantics=("parallel",), vmem_limit_bytes=VMEM_CAP),
    )(taps, wend, wout))


def _dot(a, b):
    return jnp.dot(a.astype(MXU_DTYPE), b.astype(MXU_DTYPE), preferred_element_type=F32)


def _dot_nt(a, b):
    return lax.dot_general(a.astype(MXU_DTYPE), b.astype(MXU_DTYPE), (((1,), (1,)), ((), ())),
                           preferred_element_type=F32)


def _dot_tn(a, b):
    return lax.dot_general(a.astype(MXU_DTYPE), b.astype(MXU_DTYPE), (((0,), (0,)), ((), ())),
                           preferred_element_type=F32)


STATE_W = 2 * GB_STATE


def _chunk_scan(e, acat):
    nc = e.shape[0]
    spec = pl.BlockSpec((nc, STATE_W), lambda b: (0, b))
    aspec = pl.BlockSpec((None, 1, STATE_W), lambda b: (b, 0, 0))

    def body(e_ref, a_ref, s_ref):
        a_r, a_i = a_ref[:, :GB_STATE], a_ref[:, GB_STATE:]

        def step(c, carry):
            s_r, s_i = carry
            s_ref[pl.ds(c, 1), :GB_STATE] = s_r
            s_ref[pl.ds(c, 1), GB_STATE:] = s_i
            e_r = e_ref[pl.ds(c, 1), :GB_STATE]
            e_i = e_ref[pl.ds(c, 1), GB_STATE:]
            return (a_r * s_r - a_i * s_i + e_r, a_r * s_i + a_i * s_r + e_i)

        zero = jnp.zeros((1, GB_STATE), F32)
        lax.fori_loop(0, nc, step, (zero, zero))

    return pl.pallas_call(
        body, name="ssm_chunk_scan", grid=(N_GB,),
        in_specs=[spec, aspec], out_specs=spec,
        out_shape=jax.ShapeDtypeStruct(e.shape, F32),
        compiler_params=pltpu.CompilerParams(dimension_semantics=("parallel",)),
    )(e, acat)


def _chunk_scan_bwd(ds, s, acat):
    nc = ds.shape[0]
    spec = pl.BlockSpec((nc, STATE_W), lambda b: (0, b))
    aspec = pl.BlockSpec((None, 1, STATE_W), lambda b: (b, 0, 0))

    def body(ds_ref, s_ref, a_ref, ge_ref, da_ref):
        a_r, a_i = a_ref[:, :GB_STATE], a_ref[:, GB_STATE:]

        def step(t, carry):
            g_r, g_i, d_r, d_i = carry
            c = nc - 1 - t
            ge_ref[pl.ds(c, 1), :GB_STATE] = g_r
            ge_ref[pl.ds(c, 1), GB_STATE:] = g_i
            s_r = s_ref[pl.ds(c, 1), :GB_STATE]
            s_i = s_ref[pl.ds(c, 1), GB_STATE:]
            d_r = d_r + g_r * s_r + g_i * s_i
            d_i = d_i + g_i * s_r - g_r * s_i
            n_r = ds_ref[pl.ds(c, 1), :GB_STATE] + a_r * g_r + a_i * g_i
            n_i = ds_ref[pl.ds(c, 1), GB_STATE:] + a_r * g_i - a_i * g_r
            return (n_r, n_i, d_r, d_i)

        zero = jnp.zeros((1, GB_STATE), F32)
        _, _, d_r, d_i = lax.fori_loop(0, nc, step, (zero, zero, zero, zero))
        da_ref[:, :GB_STATE] = d_r
        da_ref[:, GB_STATE:] = d_i

    return pl.pallas_call(
        body, name="ssm_chunk_scan_bwd", grid=(N_GB,),
        in_specs=[spec, spec, aspec], out_specs=[spec, aspec],
        out_shape=[jax.ShapeDtypeStruct(ds.shape, F32), jax.ShapeDtypeStruct(acat.shape, F32)],
        compiler_params=pltpu.CompilerParams(dimension_semantics=("parallel",)),
    )(ds, s, acat)


def _step_rows(ref, j, nc):
    return ref[pl.ds(j, nc, stride=SSM_CHUNK), :]


def _fold_lanes(z, widths):
    for w in widths:
        z = z + pltpu.roll(z, w, 1)
    return z


def _ssm_forward(proj, mats):
    dblk, wend, wout, acat = mats
    t = proj.shape[0]
    nc = t // SSM_CHUNK
    L = SSM_CHUNK
    lanes = pl.BlockSpec((t, GB_LANES), lambda b: (0, b))
    state = pl.BlockSpec((nc, STATE_W), lambda b: (0, b))

    def body_end(u_ref, w_ref, e_ref):
        x = jnp.concatenate([_step_rows(u_ref, j, nc).astype(MXU_DTYPE) for j in range(L)], axis=1)
        e_ref[...] = jnp.dot(x, w_ref[...], preferred_element_type=F32)

    e = pl.pallas_call(
        body_end, name="ssm_chunk_end", grid=(N_GB,),
        in_specs=[lanes, pl.BlockSpec((None, L * GB_LANES, STATE_W), lambda b: (b, 0, 0))],
        out_specs=state, out_shape=jax.ShapeDtypeStruct((nc, N_GB * STATE_W), F32),
        compiler_params=pltpu.CompilerParams(dimension_semantics=("parallel",), vmem_limit_bytes=VMEM_CAP),
    )(proj, wend)
    s = _chunk_scan(e, acat)

    def body_out(u_ref, d_ref, s_ref, w_ref, y_ref):
        xs = [_step_rows(u_ref, j, nc).astype(MXU_DTYPE) for j in range(L)]
        sb = s_ref[...].astype(MXU_DTYPE)
        for tt in range(L):
            xcat = jnp.concatenate(xs[:tt + 1], axis=1)
            taps = jnp.concatenate([d_ref[tt - j] for j in range(tt + 1)], axis=0).astype(MXU_DTYPE)
            y = (jnp.dot(xcat, taps, preferred_element_type=F32)
                 + jnp.dot(sb, w_ref[:, tt * GB_LANES:(tt + 1) * GB_LANES], preferred_element_type=F32))
            y_ref[pl.ds(tt, nc, stride=L), :] = y

    y = pl.pallas_call(
        body_out, name="ssm_chunk_out", grid=(N_GB,),
        in_specs=[lanes, pl.BlockSpec((None, L, GB_LANES, GB_LANES), lambda b: (b, 0, 0, 0)), state,
                  pl.BlockSpec((None, STATE_W, L * GB_LANES), lambda b: (b, 0, 0))],
        out_specs=lanes, out_shape=jax.ShapeDtypeStruct((t, D_SSM), F32),
        compiler_params=pltpu.CompilerParams(dimension_semantics=("parallel",), vmem_limit_bytes=VMEM_CAP),
    )(proj, dblk, s, wout)
    return y, s


def _ssm_backward(dy, proj, s, mats):
    dblk, wend, wout, acat = mats
    t = proj.shape[0]
    nc = t // SSM_CHUNK
    L = SSM_CHUNK
    lanes = pl.BlockSpec((t, GB_LANES), lambda b: (0, b))
    state = pl.BlockSpec((nc, STATE_W), lambda b: (0, b))
    taps_spec = pl.BlockSpec((None, L, GB_LANES, GB_LANES), lambda b: (b, 0, 0, 0))

    def body_state(dy_ref, w_ref, ds_ref):
        dyc = jnp.concatenate([_step_rows(dy_ref, tt, nc).astype(MXU_DTYPE) for tt in range(L)], axis=1)
        ds_ref[...] = _dot_nt(dyc, w_ref[...])

    ds = pl.pallas_call(
        body_state, name="ssm_bwd_state", grid=(N_GB,),
        in_specs=[lanes, pl.BlockSpec((None, STATE_W, L * GB_LANES), lambda b: (b, 0, 0))],
        out_specs=state, out_shape=jax.ShapeDtypeStruct((nc, N_GB * STATE_W), F32),
        compiler_params=pltpu.CompilerParams(dimension_semantics=("parallel",), vmem_limit_bytes=VMEM_CAP),
    )(dy, wout)
    ge, dacat = _chunk_scan_bwd(ds, s, acat)

    def body_in(u_ref, dy_ref, d_ref, ge_ref, w_ref, du_ref, dd_ref):
        xs = [_step_rows(u_ref, j, nc).astype(MXU_DTYPE) for j in range(L)]
        dys = [_step_rows(dy_ref, tt, nc).astype(MXU_DTYPE) for tt in range(L)]
        ge = ge_ref[...].astype(MXU_DTYPE)
        for i in range(L):
            dyc = jnp.concatenate(dys[i:], axis=1)
            taps = jnp.concatenate([d_ref[tt - i] for tt in range(i, L)], axis=1).astype(MXU_DTYPE)
            du_ref[pl.ds(i, nc, stride=L), :] = (
                _dot_nt(dyc, taps) + _dot_nt(ge, w_ref[i * GB_LANES:(i + 1) * GB_LANES, :]))
        for j in range(L):
            m = _dot_tn(xs[j], jnp.concatenate(dys[j:], axis=1))
            for tau in range(L - j):
                part = m[:, tau * GB_LANES:(tau + 1) * GB_LANES]
                if j == 0:
                    dd_ref[tau] = part
                else:
                    dd_ref[tau] += part
        same = _lane_group((GB_LANES, GB_LANES), 0, 4) == _lane_group((GB_LANES, GB_LANES), 1, 4)
        for tau in range(L):
            dd_ref[tau] = _fold_lanes(jnp.where(same, dd_ref[tau], 0.0), (64, 32, 16))

    du, ddblk = pl.pallas_call(
        body_in, name="ssm_bwd_in", grid=(N_GB,),
        in_specs=[lanes, lanes, taps_spec, state,
                  pl.BlockSpec((None, L * GB_LANES, STATE_W), lambda b: (b, 0, 0))],
        out_specs=[lanes, taps_spec],
        out_shape=[jax.ShapeDtypeStruct((t, D_SSM), F32), jax.ShapeDtypeStruct(dblk.shape, F32)],
        compiler_params=pltpu.CompilerParams(dimension_semantics=("parallel",), vmem_limit_bytes=VMEM_CAP),
    )(proj, dy, dblk, ge, wend)

    def body_w(u_ref, dy_ref, ge_ref, s_ref, dwe_ref, dwo_ref):
        ge = ge_ref[...].astype(MXU_DTYPE)
        sb = s_ref[...].astype(MXU_DTYPE)
        keep = _lane_group((GB_LANES, STATE_W), 0, 4) == _lane_group((GB_LANES, STATE_W), 1, 6)
        low = lax.broadcasted_iota(jnp.int32, (GB_LANES, 2 * SSM_STATE), 1) < SSM_STATE

        def fold_state(z):
            z = z[:, :GB_STATE // 2] + z[:, GB_STATE // 2:]
            z = z[:, :GB_STATE // 4] + z[:, GB_STATE // 4:]
            return _fold_lanes(z, (SSM_STATE,))

        for j in range(L):
            z = jnp.where(keep, _dot_tn(_step_rows(u_ref, j, nc).astype(MXU_DTYPE), ge), 0.0)
            dwe_ref[j * GB_LANES:(j + 1) * GB_LANES, :] = jnp.where(
                low, fold_state(z[:, :GB_STATE]), fold_state(z[:, GB_STATE:]))
        own = _lane_group((STATE_W, GB_LANES), 0, 6) == _lane_group((STATE_W, GB_LANES), 1, 4)
        chunk = lax.broadcasted_iota(jnp.int32, (STATE_W, GB_LANES), 1) >> 4
        for half in range(L // GROUP_BLOCK):
            acc = jnp.zeros((STATE_W, GB_LANES), F32)
            for k in range(GROUP_BLOCK):
                tt = half * GROUP_BLOCK + k
                z = jnp.where(own, _dot_tn(sb, _step_rows(dy_ref, tt, nc).astype(MXU_DTYPE)), 0.0)
                acc = acc + jnp.where(chunk == k, _fold_lanes(z, (64, 32, 16)), 0.0)
            dwo_ref[:, half * GB_LANES:(half + 1) * GB_LANES] = acc

    dwend, dwout = pl.pallas_call(
        body_w, name="ssm_bwd_w", grid=(N_GB,),
        in_specs=[lanes, lanes, state, state],
        out_specs=[pl.BlockSpec((None, L * GB_LANES, 2 * SSM_STATE), lambda b: (b, 0, 0)),
                   pl.BlockSpec((None, STATE_W, L * SSM_GROUP), lambda b: (b, 0, 0))],
        out_shape=[jax.ShapeDtypeStruct((N_GB, L * GB_LANES, 2 * SSM_STATE), F32),
                   jax.ShapeDtypeStruct((N_GB, STATE_W, L * SSM_GROUP), F32)],
        compiler_params=pltpu.CompilerParams(dimension_semantics=("parallel",), vmem_limit_bytes=VMEM_CAP),
    )(proj, dy, ge, s)
    return du, (ddblk[:, :, :, :SSM_GROUP], dwend, dwout, dacat)


def _t5_bucket(dist):
    max_exact = N_BUCKETS // 2
    is_small = dist < max_exact
    d = jnp.maximum(dist, 1).astype(F32)
    large = max_exact + (jnp.log(d / max_exact) / math.log(MAX_DISTANCE / max_exact)
                         * (N_BUCKETS - max_exact)).astype(jnp.int32)
    large = jnp.minimum(large, N_BUCKETS - 1)
    return jnp.where(is_small, dist, large)


def _band_bias(rel_bias_table):
    i = jnp.arange(BLOCK)[:, None]
    j = jnp.arange(BLOCK)[None, :]
    bucket = _t5_bucket(jnp.where(j > i, BLOCK + i - j, i - j))
    onehot = (bucket[:, :, None] == jnp.arange(N_BUCKETS)[None, None, :]).astype(F32)
    return jnp.einsum("qsb,bh->hqs", onehot, rel_bias_table, precision=lax.Precision.HIGHEST)


KV_PAIR = 2
HEADS_PER_STEP = KV_PAIR * Q_PER_KV
Q_LANES = HEADS_PER_STEP * HEAD_DIM
SLAB = 2 * HEAD_DIM
Q_COL0 = OFF_Q * COL // Q_LANES
K_COL0 = OFF_K * COL // SLAB
V_COL0 = OFF_V * COL // SLAB


def _attn_specs():
    q_spec = pl.BlockSpec((BLOCK, Q_LANES), lambda m, n: (n, Q_COL0 + m))
    k_prev = pl.BlockSpec((BLOCK, SLAB), lambda m, n: (jnp.maximum(n - 1, 0), K_COL0 + m))
    k_cur = pl.BlockSpec((BLOCK, SLAB), lambda m, n: (n, K_COL0 + m))
    v_prev = pl.BlockSpec((BLOCK, SLAB), lambda m, n: (jnp.maximum(n - 1, 0), V_COL0 + m))
    v_cur = pl.BlockSpec((BLOCK, SLAB), lambda m, n: (n, V_COL0 + m))
    bias_spec = pl.BlockSpec((HEADS_PER_STEP, BLOCK, BLOCK), lambda m, n: (m, 0, 0))
    sink_spec = pl.BlockSpec(memory_space=pltpu.SMEM)
    wide = pl.BlockSpec((BLOCK, Q_LANES), lambda m, n: (n, m))
    pair = pl.BlockSpec((BLOCK, SLAB), lambda m, n: (n, m))
    return [sink_spec, q_spec, k_prev, k_cur, v_prev, v_cur, bias_spec], wide, pair


def _low_lanes(shape):
    return lax.broadcasted_iota(jnp.int32, shape, 1) < HEAD_DIM


def _pair_halves(ref):
    kb = ref[...]
    sw = pltpu.roll(kb, HEAD_DIM, 1)
    lo = _low_lanes(kb.shape)
    zero = jnp.zeros_like(kb)
    first = (jnp.where(lo, kb, zero).astype(MXU_DTYPE), jnp.where(lo, zero, sw).astype(MXU_DTYPE))
    second = (jnp.where(lo, sw, zero).astype(MXU_DTYPE), jnp.where(lo, zero, kb).astype(MXU_DTYPE))
    return first, second


def _fold_pair(acc):
    f = [x + pltpu.roll(x, HEAD_DIM, 1) for x in acc]
    return jnp.where(_low_lanes(f[0].shape), f[0], f[1])


def _from_prev(n):
    row = lax.broadcasted_iota(jnp.int32, (BLOCK, BLOCK), 0)
    col = lax.broadcasted_iota(jnp.int32, (BLOCK, BLOCK), 1)
    prev = col > row
    return prev, jnp.where(jnp.logical_and(n == 0, prev), NEG_INF, 0.0)


def _softmax_sink(s, sink):
    m = jnp.maximum(jnp.max(s, axis=1, keepdims=True), sink)
    e = jnp.exp(s - m)
    es = jnp.exp(sink - m)
    inv = 1.0 / (jnp.sum(e, axis=1, keepdims=True) + es)
    return e * inv, es * inv


def _stack_pair(prev_halves, own_halves, a):
    return jnp.concatenate([prev_halves[a][0], prev_halves[a][1], own_halves[a][0], own_halves[a][1]], axis=0)


def _split_heads(x4, prev):
    return [jnp.where(prev, x4[:, e * BLOCK:(e + 1) * BLOCK], x4[:, (2 + e) * BLOCK:(3 + e) * BLOCK]) for e in range(2)]


def _spread_heads(x, prev):
    return jnp.concatenate([jnp.where(prev, x[0], 0.0), jnp.where(prev, x[1], 0.0),
                            jnp.where(prev, 0.0, x[0]), jnp.where(prev, 0.0, x[1])], axis=1)


def _attn_forward(proj, bias, sinks):
    t = proj.shape[0]
    in_specs, wide, _ = _attn_specs()

    def body(sink_ref, q_ref, kp_ref, kc_ref, vp_ref, vc_ref, bias_ref, o_ref):
        m, n = pl.program_id(0), pl.program_id(1)
        kp, kc, vp, vc = (_pair_halves(r) for r in (kp_ref, kc_ref, vp_ref, vc_ref))
        keys = [_stack_pair(kp, kc, a) for a in range(KV_PAIR)]
        vals = [_stack_pair(vp, vc, a) for a in range(KV_PAIR)]
        prev, edge = _from_prev(n)
        for s in range(HEADS_PER_STEP // 2):
            a = s // (Q_PER_KV // 2)
            q = q_ref[:, s * SLAB:(s + 1) * SLAB].astype(MXU_DTYPE)
            logits = _split_heads(_dot_nt(q, keys[a]), prev)
            probs = [_softmax_sink(logits[e] * ATTN_SCALE + bias_ref[2 * s + e] + edge,
                                   sink_ref[m * HEADS_PER_STEP + 2 * s + e])[0] for e in range(2)]
            o_ref[:, s * SLAB:(s + 1) * SLAB] = _dot(_spread_heads(probs, prev), vals[a])

    return pl.pallas_call(
        body, name="attn_fwd", grid=(N_KV_HEADS // KV_PAIR, t // BLOCK),
        in_specs=in_specs, out_specs=wide,
        out_shape=jax.ShapeDtypeStruct((t, D_ATTN), F32),
        compiler_params=pltpu.CompilerParams(dimension_semantics=("parallel", "arbitrary")),
    )(sinks, proj, proj, proj, proj, proj, bias)


def _attn_backward(proj, bias, sinks, d_attn):
    t = proj.shape[0]
    in_specs, wide, pair = _attn_specs()
    bias_spec = in_specs[-1]
    sink_out = pl.BlockSpec((HEADS_PER_STEP, 8, 128), lambda m, n: (m, 0, 0))

    def body(sink_ref, q_ref, kp_ref, kc_ref, vp_ref, vc_ref, bias_ref, do_ref,
             dq_ref, dka_ref, dkb_ref, dva_ref, dvb_ref, dbias_ref, dsink_ref):
        m, n = pl.program_id(0), pl.program_id(1)

        @pl.when(n == 0)
        def _():
            dbias_ref[...] = jnp.zeros_like(dbias_ref)
            dsink_ref[...] = jnp.zeros_like(dsink_ref)

        kp, kc, vp, vc = (_pair_halves(r) for r in (kp_ref, kc_ref, vp_ref, vc_ref))
        keys = [_stack_pair(kp, kc, a) for a in range(KV_PAIR)]
        vals = [_stack_pair(vp, vc, a) for a in range(KV_PAIR)]
        prev, edge = _from_prev(n)
        lo = _low_lanes((BLOCK, SLAB))
        dk = [[jnp.zeros((BLOCK, SLAB), F32) for _ in range(KV_PAIR)] for _ in range(2)]
        dv = [[jnp.zeros((BLOCK, SLAB), F32) for _ in range(KV_PAIR)] for _ in range(2)]
        for s in range(HEADS_PER_STEP // 2):
            a = s // (Q_PER_KV // 2)
            q = q_ref[:, s * SLAB:(s + 1) * SLAB].astype(MXU_DTYPE)
            do = do_ref[:, s * SLAB:(s + 1) * SLAB].astype(MXU_DTYPE)
            logits = _split_heads(_dot_nt(q, keys[a]), prev)
            dprobs = _split_heads(_dot_nt(do, vals[a]), prev)
            probs, dlogits = [], []
            for e in range(2):
                h = 2 * s + e
                p, ps = _softmax_sink(logits[e] * ATTN_SCALE + bias_ref[h] + edge, sink_ref[m * HEADS_PER_STEP + h])
                delta = jnp.sum(p * dprobs[e], axis=1, keepdims=True)
                ds = p * (dprobs[e] - delta)
                dbias_ref[h] += ds
                dsink_ref[h] += jnp.broadcast_to(jnp.sum(-ps * delta, axis=0, keepdims=True), (8, 128))
                probs.append(p)
                dlogits.append(ds)
            ds4 = _spread_heads(dlogits, prev).astype(MXU_DTYPE)
            p4 = _spread_heads(probs, prev).astype(MXU_DTYPE)
            dq_ref[:, s * SLAB:(s + 1) * SLAB] = (_dot(ds4, keys[a]) * ATTN_SCALE).astype(dq_ref.dtype)
            rk = _dot_tn(ds4, q)
            rv = _dot_tn(p4, do)
            for which in range(2):
                top = 2 * which * BLOCK
                dk[which][a] = dk[which][a] + jnp.where(lo, rk[top:top + BLOCK], rk[top + BLOCK:top + 2 * BLOCK])
                dv[which][a] = dv[which][a] + jnp.where(lo, rv[top:top + BLOCK], rv[top + BLOCK:top + 2 * BLOCK])
        dkb_ref[...] = _fold_pair(dk[0]) * ATTN_SCALE
        dka_ref[...] = _fold_pair(dk[1]) * ATTN_SCALE
        dvb_ref[...] = _fold_pair(dv[0])
        dva_ref[...] = _fold_pair(dv[1])

    kv_shape = jax.ShapeDtypeStruct((t, D_KV), F32)
    return pl.pallas_call(
        body, name="attn_bwd", grid=(N_KV_HEADS // KV_PAIR, t // BLOCK),
        in_specs=in_specs + [wide],
        out_specs=[wide, pair, pair, pair, pair, bias_spec, sink_out],
        out_shape=[jax.ShapeDtypeStruct((t, D_ATTN), MXU_DTYPE), kv_shape, kv_shape, kv_shape, kv_shape,
                   jax.ShapeDtypeStruct(bias.shape, F32), jax.ShapeDtypeStruct((N_Q_HEADS, 8, 128), F32)],
        compiler_params=pltpu.CompilerParams(dimension_semantics=("parallel", "arbitrary")),
    )(sinks, proj, proj, proj, proj, proj, bias, d_attn)


def _shift_blocks(cur, prev):
    return cur + jnp.concatenate([prev[BLOCK:], jnp.zeros_like(prev[:BLOCK])], axis=0)


def _mesh_pos():
    return lax.axis_index("x"), lax.axis_index("y"), lax.axis_index("c")


def _all_gather(x, *, name):
    def body(x_ref, out_ref, send_sems, recv_sems, local_sem):
        x, y, c = _mesh_pos()
        me, sibling = (x, y, c), (x, y, 1 - c)
        chips = [(1 - x, y), (x, 1 - y), (1 - x, 1 - y)]

        def slot(px, py, pc):
            return out_ref.at[4 * px + 2 * py + pc]

        def copy(k, block, to, src=None):
            return pltpu.make_async_remote_copy(
                src_ref=slot(*block) if src is None else src, dst_ref=slot(*block),
                send_sem=send_sems.at[k], recv_sem=recv_sems.at[k],
                device_id=to, device_id_type=pl.DeviceIdType.MESH)

        mine = pltpu.make_async_copy(x_ref, slot(*me), local_sem)
        mine.start()
        first = [copy(0, me, sibling, src=x_ref)]
        first += [copy(1 + j, me, (*chip, c), src=x_ref) for j, chip in enumerate(chips)]
        for cp in first:
            cp.start()
        passed = [copy(4 + j, (*chip, c), sibling) for j, chip in enumerate(chips)]
        for j, chip in enumerate(chips):
            copy(1 + j, (*chip, c), me).wait_recv()
            passed[j].start()
        copy(0, sibling, me).wait_recv()
        for j, chip in enumerate(chips):
            copy(4 + j, (*chip, 1 - c), me).wait_recv()
        for cp in first + passed:
            cp.wait_send()
        mine.wait()

    return pl.pallas_call(
        body, name=name,
        in_specs=[pl.BlockSpec(memory_space=pl.ANY)],
        out_specs=pl.BlockSpec(memory_space=pl.ANY),
        out_shape=jax.ShapeDtypeStruct((N_DEV,) + x.shape, x.dtype),
        scratch_shapes=[pltpu.SemaphoreType.DMA((7,)), pltpu.SemaphoreType.DMA((7,)), pltpu.SemaphoreType.DMA],
    )(x)


_HBM = pl.BlockSpec(memory_space=pltpu.HBM)
_SEM = pl.BlockSpec(memory_space=pltpu.SEMAPHORE)
_DATAFLOW = pltpu.SideEffectType.DATAFLOW_SIDE_EFFECTING


def _peers():
    x, y, c = _mesh_pos()
    others = []
    for k in range(1, N_DEV):
        px, py, pc = x ^ (k >> 2), y ^ ((k >> 1) & 1), c ^ (k & 1)
        others.append(((px, py, pc), 4 * px + 2 * py + pc))
    return 4 * x + 2 * y + c, others


def _scatter_start(x, *, name):
    me, _ = _peers()
    gather = x.ndim == 2
    own = x if gather else lax.dynamic_index_in_dim(x, me, 0, keepdims=False)
    land = lax.dynamic_update_index_in_dim(lax.empty((N_DEV,) + own.shape, x.dtype), own, me, 0)

    def body(x_ref, land_ref, send_sems, recv_sems, x_thru, land_thru, token):
        mine, others = _peers()
        for k, (pos, idx) in enumerate(others):
            pltpu.make_async_remote_copy(
                src_ref=x_ref if gather else x_ref.at[idx], dst_ref=land_ref.at[mine],
                send_sem=send_sems.at[k], recv_sem=recv_sems.at[k],
                device_id=pos, device_id_type=pl.DeviceIdType.MESH).start()
        token[...] = jnp.zeros_like(token)

    return pl.pallas_call(
        body, name=name,
        out_shape=(pltpu.SemaphoreType.DMA((N_DEV - 1,)), pltpu.SemaphoreType.DMA((N_DEV - 1,)),
                   pltpu.HBM(x.shape, x.dtype), pltpu.HBM(land.shape, x.dtype), jax.ShapeDtypeStruct((8, 128), F32)),
        in_specs=(_HBM, _HBM), out_specs=(_SEM, _SEM, _HBM, _HBM, pl.BlockSpec(memory_space=pltpu.VMEM)),
        input_output_aliases={0: 2, 1: 3},
        compiler_params=pltpu.CompilerParams(has_side_effects=_DATAFLOW),
    )(pltpu.with_memory_space_constraint(x, pltpu.HBM), pltpu.with_memory_space_constraint(land, pltpu.HBM))


def _scatter_wait(started, after, *, name):
    send_sems, recv_sems, x_thru, land_thru, _ = started
    gather = x_thru.ndim == 2

    def body(x_ref, land_ref, send_sems, recv_sems, after_ref, x_dead, got_ref):
        _, others = _peers()
        for k, (pos, idx) in enumerate(others):
            copy = pltpu.make_async_remote_copy(
                src_ref=x_ref if gather else x_ref.at[idx], dst_ref=land_ref.at[idx],
                send_sem=send_sems.at[k], recv_sem=recv_sems.at[k],
                device_id=pos, device_id_type=pl.DeviceIdType.MESH)
            copy.wait_send()
            copy.wait_recv()

    return pl.pallas_call(
        body, name=name,
        out_shape=(pltpu.HBM(x_thru.shape, x_thru.dtype), pltpu.HBM(land_thru.shape, land_thru.dtype)),
        in_specs=(_HBM, _HBM, _SEM, _SEM, pl.BlockSpec(memory_space=pl.ANY)), out_specs=(_HBM, _HBM),
        input_output_aliases={0: 0, 1: 1},
        compiler_params=pltpu.CompilerParams(has_side_effects=_DATAFLOW),
    )(x_thru, land_thru, send_sems, recv_sems, after)[1]


def _adamw_math(w, g, m, v):
    m = ADAM_B1 * m + (1.0 - ADAM_B1) * g
    v = ADAM_B2 * v + (1.0 - ADAM_B2) * (g * g)
    m_hat = m / (1.0 - ADAM_B1 ** ADAM_STEP)
    v_hat = v / (1.0 - ADAM_B2 ** ADAM_STEP)
    delta = -ADAM_LR * (m_hat / (jnp.sqrt(v_hat) + ADAM_EPS) + ADAM_WD * w)
    return delta, m, v


def _adamw_reduce(parts, w, m, v, *, name, tr):
    r, c = w.shape
    tr = min(tr, r)
    spec = pl.BlockSpec((tr, c), lambda i: (i, 0))

    def body(p_ref, w_ref, m_ref, v_ref, g_ref, d_ref, nm_ref, nv_ref):
        g = p_ref[0].astype(F32)
        for s in range(1, N_DEV):
            g = g + p_ref[s].astype(F32)
        delta, nm, nv = _adamw_math(w_ref[...], g, m_ref[...], v_ref[...])
        g_ref[...] = g
        d_ref[...] = delta
        nm_ref[...] = nm
        nv_ref[...] = nv

    return pl.pallas_call(
        body, name=name, grid=(r // tr,),
        in_specs=[pl.BlockSpec((N_DEV, tr, c), lambda i: (0, i, 0)), spec, spec, spec],
        out_specs=[spec] * 4,
        out_shape=[jax.ShapeDtypeStruct((r, c), F32)] * 4,
        compiler_params=pltpu.CompilerParams(
            dimension_semantics=("parallel",),
            vmem_limit_bytes=_vmem_limit(2 * 15 * _nbytes((tr, c), F32))),
    )(parts, w, m, v)


SMALL = ["ssm_lambda_re", "ssm_lambda_im", "ssm_b_re", "ssm_b_im", "ssm_c_re", "ssm_c_im", "ssm_d",
         "ssm_log_step", "attn_sinks", "rel_bias_table", "ln_gain", "ln_bias"]


def _pack(arrs):
    flat = jnp.concatenate([a.reshape(-1) for a in arrs])
    pad = (-flat.shape[0]) % 1024
    return jnp.pad(flat, (0, pad)).reshape(-1, 128)


def _unpack(packed, like):
    flat = packed.reshape(-1)
    out, pos = [], 0
    for a in like:
        out.append(flat[pos:pos + a.size].reshape(a.shape))
        pos += a.size
    return out


def kernel(x, w_in, ssm_lambda_re, ssm_lambda_im, ssm_b_re, ssm_b_im, ssm_c_re, ssm_c_im, ssm_d, ssm_log_step, w_glu, attn_sinks, rel_bias_table, w_branch_ssm, w_branch_attn, w_out, ln_gain, ln_bias, loss_target, m_w_in, m_ssm_lambda_re, m_ssm_lambda_im, m_ssm_b_re, m_ssm_b_im, m_ssm_c_re, m_ssm_c_im, m_ssm_d, m_ssm_log_step, m_w_glu, m_attn_sinks, m_rel_bias_table, m_w_branch_ssm, m_w_branch_attn, m_w_out, m_ln_gain, m_ln_bias, v_w_in, v_ssm_lambda_re, v_ssm_lambda_im, v_ssm_b_re, v_ssm_b_im, v_ssm_c_re, v_ssm_c_im, v_ssm_d, v_ssm_log_step, v_w_glu, v_attn_sinks, v_rel_bias_table, v_w_branch_ssm, v_w_branch_attn, v_w_out, v_ln_gain, v_ln_bias):
    t = x.shape[1]
    xs = x[0]
    target = loss_target[0]
    col_in = w_in.shape[2]
    col_br = w_glu.shape[2]
    row_out = w_out.shape[1]

    g_in = _all_gather(w_in[0].astype(MXU_DTYPE), name="gather_w_in")
    three = jnp.concatenate([w_glu[0], w_branch_ssm[0], w_branch_attn[0]], axis=0).astype(MXU_DTYPE)
    win = g_in.transpose(1, 0, 2).reshape(D_MODEL, D_IN)
    hold = g_in[0, 0, 0] * 0
    sent_three = _scatter_start(three + hold, name="gather_w_1024_start")
    sent_wout = _scatter_start(w_out[0].astype(MXU_DTYPE) + hold, name="gather_w_out_start")

    ssm_params = (ssm_lambda_re[0], ssm_lambda_im[0], ssm_b_re[0], ssm_b_im[0], ssm_c_re[0], ssm_c_im[0],
                  ssm_log_step[0])
    mats, mats_vjp = jax.vjp(_ssm_matrices, *ssm_params)
    bias, bias_vjp = jax.vjp(_band_bias, rel_bias_table)
    sinks = attn_sinks[0]
    d_skip = ssm_d

    x_mx = xs.astype(MXU_DTYPE)
    proj = _mm(x_mx, win, name="in_proj", tm=2048, tn=512, tk=2048, after=(sent_three[4], sent_wout[4]))
    mats_mx = _ssm_expand(*mats[:3]) + (mats[3],)
    y_conv, states = _ssm_forward(proj, mats_mx)

    def f_gelu(yv, u, d):
        ys = yv + d * u
        return ys, _gelu(ys)

    y_ssm, glu_in = _ew(f_gelu, [(y_conv, "row", 0), (proj, "row", OFF_U), (d_skip, "vec", 0)],
                        [(COL, F32, "row"), (COL, MXU_DTYPE, "row")], rows=t, cw=COL, ncb=4, tr=1024, name="ssm_gelu")
    g_three = _scatter_wait(sent_three, glu_in, name="gather_w_1024_wait")
    three_full = g_three.transpose(1, 0, 2).reshape(3 * D_SSM, N_DEV * col_br)
    wglu, wbs, wba = three_full[:D_SSM], three_full[D_SSM:2 * D_SSM], three_full[2 * D_SSM:]
    glu = _mm(glu_in, wglu, name="glu_proj", tm=2048, tn=512, tk=1024)

    def f_hssm(ga, gb, z):
        return (ga * _sigmoid(gb) * _silu(z),)

    (h_ssm,) = _ew(f_hssm, [(glu, "row", 0), (glu, "row", 4), (proj, "row", OFF_ZS)],
                   [(COL, MXU_DTYPE, "row")], rows=t, cw=COL, ncb=4, tr=1024, name="ssm_gate")

    attn = _attn_forward(proj, bias, sinks)

    def f_hattn(a, z):
        return (a * _silu(z),)

    (h_attn,) = _ew(f_hattn, [(attn, "row", 0), (proj, "row", OFF_ZA)], [(COL, MXU_DTYPE, "row")],
                    rows=t, cw=COL, ncb=4, tr=1024, name="attn_gate")
    p_ssm = _mm(h_ssm, wbs, name="branch_ssm", tm=2048, tn=512, tk=1024)
    p_attn = _mm(h_attn, wba, name="branch_attn", tm=2048, tn=512, tk=1024)

    def f_merge(ps, pa, ls, la):
        return (_sigmoid(ls) * ps + _sigmoid(la) * pa,)

    (merged,) = _ew(f_merge, [(p_ssm, "row", 0), (p_attn, "row", 0), (proj, "row", OFF_GS), (proj, "row", OFF_GA)],
                    [(COL, MXU_DTYPE, "row")], rows=t, cw=COL, ncb=8, tr=1024, name="merge")
    wout = _scatter_wait(sent_wout, merged, name="gather_w_out_wait").reshape(D_MODEL, D_MODEL)
    out = _mm(merged, wout, name="out_proj", tm=2048, tn=512, tk=2048)

    def f_norm(xv, ov, tg, gain, lbias):
        r = DEEPNORM_ALPHA * xv + ov
        mu = jnp.mean(r, axis=1, keepdims=True)
        cen = r - mu
        var = jnp.mean(cen * cen, axis=1, keepdims=True)
        rstd = lax.rsqrt(var + LN_EPS)
        xhat = cen * rstd
        yv = xhat * gain + lbias
        diff = yv - tg
        row_loss = 0.5 * jnp.mean(diff * diff, axis=1, keepdims=True)
        loss = jnp.broadcast_to(jnp.sum(row_loss, axis=0, keepdims=True), (1, 128))
        dy = diff * (1.0 / D_MODEL)
        dgain = jnp.sum(dy * xhat, axis=0, keepdims=True)
        dbias = jnp.sum(dy, axis=0, keepdims=True)
        dxh = dy * gain
        dr = rstd * (dxh - jnp.mean(dxh, axis=1, keepdims=True) - xhat * jnp.mean(dxh * xhat, axis=1, keepdims=True))
        return dr, loss, dgain, dbias

    dr, loss_part, g_ln_gain, g_ln_bias = _ew(
        f_norm, [(xs, "row", 0), (out, "row", 0), (target, "row", 0), (ln_gain, "vec", 0), (ln_bias, "vec", 0)],
        [(D_MODEL, F32, "row"), (128, F32, "acc"), (D_MODEL, F32, "acc"), (D_MODEL, F32, "acc")],
        rows=t, cw=D_MODEL, ncb=1, tr=256, name="norm_loss")

    def scatter_cols(g, cols):
        return g.reshape(g.shape[0], N_DEV, cols).transpose(1, 0, 2)

    gw_out = _mm(merged, dr, ta=True, out_dtype=WIRE_DTYPE, name="grad_w_out", tm=2048, tn=512, tk=1024)
    sent_out = _scatter_start(gw_out.reshape(N_DEV, row_out, D_MODEL), name="scatter_g_out_start")
    d_merged = _mm(dr, wout, tb=True, name="d_merged", tm=2048, tn=512, tk=2048, after=(sent_out[4],))

    def b_merge(dm, ps, pa, ls, la):
        gs, ga = _sigmoid(ls), _sigmoid(la)
        return dm * gs, dm * ga, dm * ps * gs * (1.0 - gs), dm * pa * ga * (1.0 - ga)

    dp_ssm, dp_attn, dgl_s, dgl_a = _ew(
        b_merge, [(d_merged, "row", 0), (p_ssm, "row", 0), (p_attn, "row", 0), (proj, "row", OFF_GS), (proj, "row", OFF_GA)],
        [(COL, MXU_DTYPE, "row")] * 4, rows=t, cw=COL, ncb=8, tr=1024, name="merge_bwd")
    gw_bs = _mm(h_ssm, dp_ssm, ta=True, out_dtype=WIRE_DTYPE, name="grad_w_branch_ssm", tm=1024, tn=512, tk=1024)
    gw_ba = _mm(h_attn, dp_attn, ta=True, out_dtype=WIRE_DTYPE, name="grad_w_branch_attn", tm=1024, tn=512, tk=1024)
    dh_ssm = _mm(dp_ssm, wbs, tb=True, name="d_h_ssm", tm=2048, tn=512, tk=2048)
    dh_attn = _mm(dp_attn, wba, tb=True, name="d_h_attn", tm=2048, tn=512, tk=2048)

    def b_hssm(dh, ga, gb, z):
        sg = _sigmoid(gb)
        dgate = dh * _silu(z)
        return dgate * sg, dgate * ga * sg * (1.0 - sg), dh * ga * sg * _silu_grad(z)

    dglu_a, dglu_b, dz_ssm = _ew(b_hssm, [(dh_ssm, "row", 0), (glu, "row", 0), (glu, "row", 4), (proj, "row", OFF_ZS)],
                                 [(COL, MXU_DTYPE, "row")] * 3, rows=t, cw=COL, ncb=4, tr=1024, name="ssm_gate_bwd")
    dglu = jnp.concatenate([dglu_a, dglu_b], axis=1)
    gw_glu = _mm(glu_in, dglu, ta=True, out_dtype=WIRE_DTYPE, name="grad_w_glu", tm=1024, tn=512, tk=1024)
    sent_three = _scatter_start(scatter_cols(jnp.concatenate([gw_glu, gw_bs, gw_ba], axis=0), col_br),
                                name="scatter_g_1024_start")
    dglu_in = _mm(dglu, wglu, tb=True, name="d_glu_in", tm=2048, tn=512, tk=2048, after=(sent_three[4],))

    def b_gelu(dgi, ys, u):
        dys = dgi * _gelu_grad(ys)
        return dys, jnp.sum(dys * u, axis=0, keepdims=True)

    dy_ssm, g_ssm_d = _ew(b_gelu, [(dglu_in, "row", 0), (y_ssm, "row", 0), (proj, "row", OFF_U)],
                          [(COL, F32, "row"), (COL, F32, "acc")],
                          rows=t, cw=COL, ncb=4, tr=1024, name="ssm_gelu_bwd")
    du_ssm, dmats = _ssm_backward(dy_ssm, proj, states, mats_mx)
    du = (du_ssm + d_skip * dy_ssm).astype(MXU_DTYPE)
    g_lre, g_lim, g_bre, g_bim, g_cre, g_cim, g_lstep = mats_vjp(dmats)

    def b_hattn(dh, a, z):
        return dh * _silu(z), dh * a * _silu_grad(z)

    d_attn, dz_attn = _ew(b_hattn, [(dh_attn, "row", 0), (attn, "row", 0), (proj, "row", OFF_ZA)],
                          [(COL, F32, "row"), (COL, MXU_DTYPE, "row")], rows=t, cw=COL, ncb=4, tr=1024, name="attn_gate_bwd")
    dq, dka, dkb, dva, dvb, dbias, dsink = _attn_backward(proj, bias, sinks, d_attn)
    dk = _shift_blocks(dka, dkb).astype(MXU_DTYPE)
    dv = _shift_blocks(dva, dvb).astype(MXU_DTYPE)
    (g_table,) = bias_vjp(dbias)
    g_sinks = dsink[:, 0, 0]

    dproj = jnp.concatenate([du, dz_ssm, dq, dk, dv, dz_attn, dgl_s, dgl_a], axis=1)
    gw_in = _mm(x_mx, dproj, ta=True, out_dtype=WIRE_DTYPE, name="grad_w_in", tm=2048, tn=512, tk=1024)
    sent_in = _scatter_start(scatter_cols(gw_in, col_in), name="scatter_g_in_start")
    grad_x = _mm(dproj, win, tb=True, add=dr, add_scale=DEEPNORM_ALPHA, name="grad_x", tm=1024, tn=1024, tk=2176,
                 after=(sent_in[4],))

    parts_out = _scatter_wait(sent_out, grad_x, name="scatter_g_out_wait")
    parts_three = _scatter_wait(sent_three, parts_out, name="scatter_g_1024_wait")
    parts_in = _scatter_wait(sent_in, parts_three, name="scatter_g_in_wait")

    o_in = _adamw_reduce(parts_in, w_in[0], m_w_in[0], v_w_in[0], name="adamw_w_in", tr=128)
    three_w = jnp.concatenate([w_glu[0], w_branch_ssm[0], w_branch_attn[0]], axis=0)
    three_m = jnp.concatenate([m_w_glu[0], m_w_branch_ssm[0], m_w_branch_attn[0]], axis=0)
    three_v = jnp.concatenate([v_w_glu[0], v_w_branch_ssm[0], v_w_branch_attn[0]], axis=0)
    o_three = _adamw_reduce(parts_three, three_w, three_m, three_v, name="adamw_w_1024", tr=512)
    o_out = _adamw_reduce(parts_out, w_out[0], m_w_out[0], v_w_out[0], name="adamw_w_out", tr=128)

    small_w = [ssm_lambda_re, ssm_lambda_im, ssm_b_re, ssm_b_im, ssm_c_re, ssm_c_im, ssm_d, ssm_log_step,
               attn_sinks, rel_bias_table, ln_gain, ln_bias]
    small_m = [m_ssm_lambda_re, m_ssm_lambda_im, m_ssm_b_re, m_ssm_b_im, m_ssm_c_re, m_ssm_c_im, m_ssm_d,
               m_ssm_log_step, m_attn_sinks, m_rel_bias_table, m_ln_gain, m_ln_bias]
    small_v = [v_ssm_lambda_re, v_ssm_lambda_im, v_ssm_b_re, v_ssm_b_im, v_ssm_c_re, v_ssm_c_im, v_ssm_d,
               v_ssm_log_step, v_attn_sinks, v_rel_bias_table, v_ln_gain, v_ln_bias]
    small_g = [g_lre, g_lim, g_bre, g_bim, g_cre, g_cim, g_ssm_d, g_lstep, g_sinks, g_table, g_ln_gain, g_ln_bias]
    parts_small = _all_gather(_pack(small_g), name="gather_g_small")
    o_small = _adamw_reduce(parts_small, _pack(small_w), _pack(small_m), _pack(small_v), name="adamw_small", tr=2160)
    sg, sd, sm, sv = [_unpack(o, small_w) for o in o_small]

    loss = lax.psum(loss_part[0, 0], MESH_AXES)

    def big(o, idx):
        g_in_, g_three_, g_out_ = o_in[idx], o_three[idx], o_out[idx]
        return {"w_in": g_in_[None], "w_glu": g_three_[None, :D_SSM], "w_branch_ssm": g_three_[None, D_SSM:2 * D_SSM],
                "w_branch_attn": g_three_[None, 2 * D_SSM:], "w_out": g_out_[None]}

    order = ["w_in", "ssm_lambda_re", "ssm_lambda_im", "ssm_b_re", "ssm_b_im", "ssm_c_re", "ssm_c_im", "ssm_d",
             "ssm_log_step", "w_glu", "attn_sinks", "rel_bias_table", "w_branch_ssm", "w_branch_attn", "w_out",
             "ln_gain", "ln_bias"]
    outs = [loss, grad_x[None]]
    for idx, small in enumerate([sg, sd, sm, sv]):
        table = big(None, idx)
        table.update(dict(zip(SMALL, small)))
        outs += [table[n] for n in order]
    return tuple(outs)
```

```python
import functools
import math

import jax
import jax.numpy as jnp
from jax import lax
from jax.experimental import pallas as pl
from jax.experimental.pallas import tpu as pltpu

F32 = jnp.float32
MXU_DTYPE = jnp.bfloat16
WIRE_DTYPE = jnp.bfloat16

D_MODEL = 2048
D_SSM = 1024
SSM_GROUP = 16
N_GROUPS = 64
SSM_STATE = 64
N_Q_HEADS = 16
N_KV_HEADS = 4
Q_PER_KV = 4
HEAD_DIM = 64
D_ATTN = 1024
D_KV = 256
WINDOW = 128
BLOCK = 128
N_BUCKETS = 32
MAX_DISTANCE = 128
D_IN = 8704
DEEPNORM_ALPHA = 2.0 ** 0.25
LN_EPS = 1e-5
NEG_INF = -1e30
ATTN_SCALE = HEAD_DIM ** -0.5

ADAM_LR = 0.001
ADAM_B1 = 0.9
ADAM_B2 = 0.999
ADAM_EPS = 1e-08
ADAM_WD = 0.01
ADAM_STEP = 10

N_DEV = 8
SSM_CHUNK = 16
GROUP_BLOCK = 8
N_GB = N_GROUPS // GROUP_BLOCK
GB_LANES = GROUP_BLOCK * SSM_GROUP
GB_STATE = GROUP_BLOCK * SSM_STATE
COL = 256
OFF_U, OFF_ZS, OFF_Q, OFF_K, OFF_V, OFF_ZA, OFF_GS, OFF_GA = 0, 4, 8, 12, 13, 14, 18, 26

VMEM_CAP = 56 * 1024 * 1024
MESH_AXES = ("x", "y", "c")


def _vmem_limit(block_bytes):
    return int(min(max(3 * block_bytes, 16 * 1024 * 1024), VMEM_CAP))


def _nbytes(shape, dtype):
    return math.prod(shape) * jnp.dtype(dtype).itemsize


def _tile(n, pref):
    if n <= pref:
        return n
    t = (pref // 128) * 128
    while t >= 128:
        if n % t == 0:
            return t
        t -= 128
    return n


def _mm(a, b, *, name, ta=False, tb=False, out_dtype=F32, tm=1024, tn=512, tk=512, add=None, add_scale=1.0,
        after=()):
    squeeze = a.ndim == 2
    if squeeze:
        a, b = a[None], b[None]
        if add is not None:
            add = add[None]
    nb = a.shape[0]
    m, k = (a.shape[2], a.shape[1]) if ta else (a.shape[1], a.shape[2])
    n = b.shape[1] if tb else b.shape[2]
    tm, tn, tk = _tile(m, tm), _tile(n, tn), _tile(k, tk)
    nk = k // tk
    dn = (((0 if ta else 1,), (1 if tb else 0,)), ((), ()))

    a_spec = (pl.BlockSpec((None, tk, tm), lambda g, i, j, kk: (g, kk, i)) if ta
              else pl.BlockSpec((None, tm, tk), lambda g, i, j, kk: (g, i, kk)))
    b_spec = (pl.BlockSpec((None, tn, tk), lambda g, i, j, kk: (g, j, kk)) if tb
              else pl.BlockSpec((None, tk, tn), lambda g, i, j, kk: (g, kk, j)))
    o_spec = pl.BlockSpec((None, tm, tn), lambda g, i, j, kk: (g, i, j))
    in_specs = [a_spec, b_spec]
    operands = [a, b]
    if add is not None:
        in_specs.append(o_spec)
        operands.append(add)
    for tok in after:
        in_specs.append(pl.BlockSpec(memory_space=pl.ANY))
        operands.append(tok)
    n_in = len(operands)

    def body(*refs):
        a_ref, b_ref = refs[0], refs[1]
        add_ref = refs[2] if add is not None else None
        o_ref = refs[n_in]
        acc_ref = refs[-1]
        kk = pl.program_id(3)
        part = lax.dot_general(a_ref[...].astype(MXU_DTYPE), b_ref[...].astype(MXU_DTYPE), dn,
                               preferred_element_type=F32)

        @pl.when(kk == 0)
        def _():
            acc_ref[...] = part

        @pl.when(kk > 0)
        def _():
            acc_ref[...] += part

        @pl.when(kk == nk - 1)
        def _():
            r = acc_ref[...]
            if add_ref is not None:
                r = r + add_scale * add_ref[...]
            o_ref[...] = r.astype(out_dtype)

    blocks = (_nbytes((tm, tk), a.dtype) + _nbytes((tk, tn), b.dtype) + _nbytes((tm, tn), out_dtype)
              + (_nbytes((tm, tn), F32) if add is not None else 0))
    out = pl.pallas_call(
        body,
        name=name,
        grid=(nb, m // tm, n // tn, nk),
        in_specs=in_specs,
        out_specs=o_spec,
        out_shape=jax.ShapeDtypeStruct((nb, m, n), out_dtype),
        scratch_shapes=[pltpu.VMEM((tm, tn), F32)],
        compiler_params=pltpu.CompilerParams(
            dimension_semantics=("parallel", "parallel", "parallel", "arbitrary"),
            vmem_limit_bytes=_vmem_limit(2 * blocks + 2 * _nbytes((tm, tn), F32))),
    )(*operands)
    return out[0] if squeeze else out


def _ew(fn, ins, outs, *, rows, cw, ncb, tr, name):
    tr = min(tr, rows)
    n_in = len(ins)

    def row_map(off):
        return lambda j, i: (i, off + j)

    def vec_map(off):
        return lambda j, i: (0, off + j)

    in_specs = []
    for arr, kind, off in ins:
        if kind == "row":
            in_specs.append(pl.BlockSpec((tr, cw), row_map(off)))
        else:
            in_specs.append(pl.BlockSpec((1, cw), vec_map(off)))
    out_specs, out_shapes = [], []
    for bw, dt, kind in outs:
        if kind == "row":
            out_specs.append(pl.BlockSpec((tr, bw), row_map(0)))
            out_shapes.append(jax.ShapeDtypeStruct((rows, ncb * bw), dt))
        else:
            out_specs.append(pl.BlockSpec((1, bw), vec_map(0)))
            out_shapes.append(jax.ShapeDtypeStruct((1, ncb * bw), F32))

    def body(*refs):
        i = pl.program_id(1)
        vals = fn(*[r[...] for r in refs[:n_in]])
        for r, (bw, dt, kind), v in zip(refs[n_in:], outs, vals):
            if kind == "row":
                r[...] = v.astype(dt)
            else:
                @pl.when(i == 0)
                def _(r=r):
                    r[...] = jnp.zeros_like(r)

                r[...] += v

    blocks = sum(_nbytes((tr, cw), a.dtype) for a, kind, _ in ins if kind == "row")
    blocks += sum(_nbytes((tr, bw), dt) for bw, dt, kind in outs if kind == "row")
    res = pl.pallas_call(
        body,
        name=name,
        grid=(ncb, rows // tr),
        in_specs=in_specs,
        out_specs=out_specs,
        out_shape=out_shapes,
        compiler_params=pltpu.CompilerParams(
            dimension_semantics=("parallel", "arbitrary"),
            vmem_limit_bytes=_vmem_limit(4 * blocks)),
    )(*[a for a, _, _ in ins])
    return res


def _sigmoid(x):
    return 1.0 / (1.0 + jnp.exp(-x))


INV_SQRT2 = 0.7071067811865476
INV_SQRT_2PI = 0.3989422804014327


def _gelu(x):
    return 0.5 * x * (1.0 + lax.erf(x * INV_SQRT2))


def _gelu_grad(x):
    return 0.5 * (1.0 + lax.erf(x * INV_SQRT2)) + x * INV_SQRT_2PI * jnp.exp(-0.5 * x * x)


def _silu(x):
    return x * _sigmoid(x)


def _silu_grad(x):
    s = _sigmoid(x)
    return s * (1.0 + x * (1.0 - s))


def _ssm_matrices(lam_re, lam_im, b_re, b_im, c_re, c_im, log_step):
    L = SSM_CHUNK
    step = jnp.exp(log_step)[:, None]
    ea, eb = lam_re * step, lam_im * step
    mag = jnp.exp(ea)
    lbr, lbi = mag * jnp.cos(eb), mag * jnp.sin(eb)
    den = lam_re * lam_re + lam_im * lam_im
    nr, ni = lbr - 1.0, lbi
    cr = (nr * lam_re + ni * lam_im) / den
    ci = (ni * lam_re - nr * lam_im) / den
    bbr = cr[..., None] * b_re - ci[..., None] * b_im
    bbi = cr[..., None] * b_im + ci[..., None] * b_re
    taus = jnp.arange(L + 1, dtype=F32)[:, None, None]
    pmag = jnp.exp(taus * ea[None])
    pwr, pwi = pmag * jnp.cos(taus * eb[None]), pmag * jnp.sin(taus * eb[None])
    mr = c_re[None] * pwr[:L, :, None, :] - c_im[None] * pwi[:L, :, None, :]
    mi = c_re[None] * pwi[:L, :, None, :] + c_im[None] * pwr[:L, :, None, :]
    kk = jnp.sum(mr[..., None] * bbr[None, :, None] - mi[..., None] * bbi[None, :, None], axis=3)
    taps = jnp.transpose(kk.reshape(L, N_GB, GROUP_BLOCK, SSM_GROUP, SSM_GROUP), (1, 0, 2, 4, 3))
    taps = taps.reshape(N_GB, L, GB_LANES, SSM_GROUP)
    rev_r, rev_i = pwr[L - 1 - jnp.arange(L)], pwi[L - 1 - jnp.arange(L)]
    wer = rev_r[..., None] * bbr[None] - rev_i[..., None] * bbi[None]
    wei = rev_r[..., None] * bbi[None] + rev_i[..., None] * bbr[None]

    def rows_in(w):
        w = jnp.transpose(w.reshape(L, N_GB, GROUP_BLOCK, SSM_STATE, SSM_GROUP), (1, 0, 2, 4, 3))
        return w.reshape(N_GB, L * GB_LANES, SSM_STATE)

    wend = jnp.concatenate([rows_in(wer), rows_in(wei)], axis=2)
    m1r = c_re[None] * pwr[1:, :, None, :] - c_im[None] * pwi[1:, :, None, :]
    m1i = c_re[None] * pwi[1:, :, None, :] + c_im[None] * pwr[1:, :, None, :]

    def rows_out(m):
        m = jnp.transpose(m.reshape(L, N_GB, GROUP_BLOCK, SSM_GROUP, SSM_STATE), (1, 2, 4, 0, 3))
        return m.reshape(N_GB, GB_STATE, L * SSM_GROUP)

    wout = jnp.concatenate([rows_out(m1r), rows_out(-m1i)], axis=1)
    acat = jnp.concatenate([pwr[L].reshape(N_GB, 1, GB_STATE), pwi[L].reshape(N_GB, 1, GB_STATE)], axis=2)
    return taps, wend, wout, acat


def _lane_group(shape, axis, shift):
    return (lax.broadcasted_iota(jnp.int32, shape, axis) >> shift) & (GROUP_BLOCK - 1)


def _ssm_expand(taps, wend, wout):
    L = SSM_CHUNK
    taps = jnp.pad(taps, ((0, 0), (0, 0), (0, 0), (0, GB_LANES - SSM_GROUP)))

    def body(t_ref, we_ref, wo_ref, d_ref, web_ref, wob_ref):
        def rc(shape):
            return lax.broadcasted_iota(jnp.int32, shape, 0), lax.broadcasted_iota(jnp.int32, shape, 1)

        r, c = rc((GB_LANES, GB_LANES))
        spread = ((r < SSM_GROUP) & (r == (c & (SSM_GROUP - 1)))).astype(MXU_DTYPE)
        same = _lane_group((GB_LANES, GB_LANES), 0, 4) == _lane_group((GB_LANES, GB_LANES), 1, 4)
        for tau in range(L):
            full = jnp.dot(t_ref[tau].astype(MXU_DTYPE), spread, preferred_element_type=F32)
            d_ref[tau] = jnp.where(same, full, 0.0).astype(d_ref.dtype)
        r, c = rc((2 * SSM_STATE, STATE_W))
        part = (r >> 6) == (c >> 9)
        spread = (part & ((r & (SSM_STATE - 1)) == (c & (SSM_STATE - 1)))).astype(MXU_DTYPE)
        keep = _lane_group((GB_LANES, STATE_W), 0, 4) == _lane_group((GB_LANES, STATE_W), 1, 6)
        for j in range(L):
            rows = slice(j * GB_LANES, (j + 1) * GB_LANES)
            full = jnp.dot(we_ref[rows, :].astype(MXU_DTYPE), spread, preferred_element_type=F32)
            web_ref[rows, :] = jnp.where(keep, full, 0.0).astype(web_ref.dtype)
        r, c = rc((GB_LANES, GB_LANES))
        own = _lane_group((STATE_W, GB_LANES), 0, 6) == _lane_group((STATE_W, GB_LANES), 1, 4)
        for tt in range(L):
            half, k = tt // GROUP_BLOCK, tt % GROUP_BLOCK
            spread = (((r >> 4) == k) & ((r & (SSM_GROUP - 1)) == (c & (SSM_GROUP - 1)))).astype(MXU_DTYPE)
            src = wo_ref[:, half * GB_LANES:(half + 1) * GB_LANES].astype(MXU_DTYPE)
            full = jnp.dot(src, spread, preferred_element_type=F32)
            wob_ref[:, tt * GB_LANES:(tt + 1) * GB_LANES] = jnp.where(own, full, 0.0).astype(wob_ref.dtype)

    def spec(shape):
        return pl.BlockSpec((None,) + shape[1:], lambda b: (b,) + (0,) * (len(shape) - 1))

    out_shapes = [(N_GB, L, GB_LANES, GB_LANES), (N_GB, L * GB_LANES, STATE_W), (N_GB, STATE_W, L * GB_LANES)]
    return tuple(pl.pallas_call(
        body, name="ssm_expand", grid=(N_GB,),
        in_specs=[spec(taps.shape), spec(wend.shape), spec(wout.shape)],
        out_specs=[spec(s) for s in out_shapes],
        out_shape=[jax.ShapeDtypeStruct(s, MXU_DTYPE) for s in out_shapes],
        compiler_params=pltpu.CompilerParams(dimension_semantics=("parallel",), vmem_limit_bytes=VMEM_CAP),
    )(taps, wend, wout))


def _dot(a, b):
    return jnp.dot(a.astype(MXU_DTYPE), b.astype(MXU_DTYPE), preferred_element_type=F32)


def _dot_nt(a, b):
    return lax.dot_general(a.astype(MXU_DTYPE), b.astype(MXU_DTYPE), (((1,), (1,)), ((), ())),
                           preferred_element_type=F32)


def _dot_tn(a, b):
    return lax.dot_general(a.astype(MXU_DTYPE), b.astype(MXU_DTYPE), (((0,), (0,)), ((), ())),
                           preferred_element_type=F32)


STATE_W = 2 * GB_STATE


def _chunk_scan(e, acat):
    nc = e.shape[0]
    spec = pl.BlockSpec((nc, STATE_W), lambda b: (0, b))
    aspec = pl.BlockSpec((None, 1, STATE_W), lambda b: (b, 0, 0))

    def body(e_ref, a_ref, s_ref):
        a_r, a_i = a_ref[:, :GB_STATE], a_ref[:, GB_STATE:]

        def step(c, carry):
            s_r, s_i = carry
            s_ref[pl.ds(c, 1), :GB_STATE] = s_r
            s_ref[pl.ds(c, 1), GB_STATE:] = s_i
            e_r = e_ref[pl.ds(c, 1), :GB_STATE]
            e_i = e_ref[pl.ds(c, 1), GB_STATE:]
            return (a_r * s_r - a_i * s_i + e_r, a_r * s_i + a_i * s_r + e_i)

        zero = jnp.zeros((1, GB_STATE), F32)
        lax.fori_loop(0, nc, step, (zero, zero))

    return pl.pallas_call(
        body, name="ssm_chunk_scan", grid=(N_GB,),
        in_specs=[spec, aspec], out_specs=spec,
        out_shape=jax.ShapeDtypeStruct(e.shape, F32),
        compiler_params=pltpu.CompilerParams(dimension_semantics=("parallel",)),
    )(e, acat)


def _chunk_scan_bwd(ds, s, acat):
    nc = ds.shape[0]
    spec = pl.BlockSpec((nc, STATE_W), lambda b: (0, b))
    aspec = pl.BlockSpec((None, 1, STATE_W), lambda b: (b, 0, 0))

    def body(ds_ref, s_ref, a_ref, ge_ref, da_ref):
        a_r, a_i = a_ref[:, :GB_STATE], a_ref[:, GB_STATE:]

        def step(t, carry):
            g_r, g_i, d_r, d_i = carry
            c = nc - 1 - t
            ge_ref[pl.ds(c, 1), :GB_STATE] = g_r
            ge_ref[pl.ds(c, 1), GB_STATE:] = g_i
            s_r = s_ref[pl.ds(c, 1), :GB_STATE]
            s_i = s_ref[pl.ds(c, 1), GB_STATE:]
            d_r = d_r + g_r * s_r + g_i * s_i
            d_i = d_i + g_i * s_r - g_r * s_i
            n_r = ds_ref[pl.ds(c, 1), :GB_STATE] + a_r * g_r + a_i * g_i
            n_i = ds_ref[pl.ds(c, 1), GB_STATE:] + a_r * g_i - a_i * g_r
            return (n_r, n_i, d_r, d_i)

        zero = jnp.zeros((1, GB_STATE), F32)
        _, _, d_r, d_i = lax.fori_loop(0, nc, step, (zero, zero, zero, zero))
        da_ref[:, :GB_STATE] = d_r
        da_ref[:, GB_STATE:] = d_i

    return pl.pallas_call(
        body, name="ssm_chunk_scan_bwd", grid=(N_GB,),
        in_specs=[spec, spec, aspec], out_specs=[spec, aspec],
        out_shape=[jax.ShapeDtypeStruct(ds.shape, F32), jax.ShapeDtypeStruct(acat.shape, F32)],
        compiler_params=pltpu.CompilerParams(dimension_semantics=("parallel",)),
    )(ds, s, acat)


def _step_rows(ref, j, nc):
    return ref[pl.ds(j, nc, stride=SSM_CHUNK), :]


def _fold_lanes(z, widths):
    for w in widths:
        z = z + pltpu.roll(z, w, 1)
    return z


def _ssm_forward(proj, mats):
    dblk, wend, wout, acat = mats
    t = proj.shape[0]
    nc = t // SSM_CHUNK
    L = SSM_CHUNK
    lanes = pl.BlockSpec((t, GB_LANES), lambda b: (0, b))
    state = pl.BlockSpec((nc, STATE_W), lambda b: (0, b))

    def body_end(u_ref, w_ref, e_ref):
        x = jnp.concatenate([_step_rows(u_ref, j, nc).astype(MXU_DTYPE) for j in range(L)], axis=1)
        e_ref[...] = jnp.dot(x, w_ref[...], preferred_element_type=F32)

    e = pl.pallas_call(
        body_end, name="ssm_chunk_end", grid=(N_GB,),
        in_specs=[lanes, pl.BlockSpec((None, L * GB_LANES, STATE_W), lambda b: (b, 0, 0))],
        out_specs=state, out_shape=jax.ShapeDtypeStruct((nc, N_GB * STATE_W), F32),
        compiler_params=pltpu.CompilerParams(dimension_semantics=("parallel",), vmem_limit_bytes=VMEM_CAP),
    )(proj, wend)
    s = _chunk_scan(e, acat)

    def body_out(u_ref, d_ref, s_ref, w_ref, y_ref):
        xs = [_step_rows(u_ref, j, nc).astype(MXU_DTYPE) for j in range(L)]
        sb = s_ref[...].astype(MXU_DTYPE)
        for tt in range(L):
            xcat = jnp.concatenate(xs[:tt + 1], axis=1)
            taps = jnp.concatenate([d_ref[tt - j] for j in range(tt + 1)], axis=0).astype(MXU_DTYPE)
            y = (jnp.dot(xcat, taps, preferred_element_type=F32)
                 + jnp.dot(sb, w_ref[:, tt * GB_LANES:(tt + 1) * GB_LANES], preferred_element_type=F32))
            y_ref[pl.ds(tt, nc, stride=L), :] = y

    y = pl.pallas_call(
        body_out, name="ssm_chunk_out", grid=(N_GB,),
        in_specs=[lanes, pl.BlockSpec((None, L, GB_LANES, GB_LANES), lambda b: (b, 0, 0, 0)), state,
                  pl.BlockSpec((None, STATE_W, L * GB_LANES), lambda b: (b, 0, 0))],
        out_specs=lanes, out_shape=jax.ShapeDtypeStruct((t, D_SSM), F32),
        compiler_params=pltpu.CompilerParams(dimension_semantics=("parallel",), vmem_limit_bytes=VMEM_CAP),
    )(proj, dblk, s, wout)
    return y, s


def _ssm_backward(dy, proj, s, mats):
    dblk, wend, wout, acat = mats
    t = proj.shape[0]
    nc = t // SSM_CHUNK
    L = SSM_CHUNK
    lanes = pl.BlockSpec((t, GB_LANES), lambda b: (0, b))
    state = pl.BlockSpec((nc, STATE_W), lambda b: (0, b))
    taps_spec = pl.BlockSpec((None, L, GB_LANES, GB_LANES), lambda b: (b, 0, 0, 0))

    def body_state(dy_ref, w_ref, ds_ref):
        dyc = jnp.concatenate([_step_rows(dy_ref, tt, nc).astype(MXU_DTYPE) for tt in range(L)], axis=1)
        ds_ref[...] = _dot_nt(dyc, w_ref[...])

    ds = pl.pallas_call(
        body_state, name="ssm_bwd_state", grid=(N_GB,),
        in_specs=[lanes, pl.BlockSpec((None, STATE_W, L * GB_LANES), lambda b: (b, 0, 0))],
        out_specs=state, out_shape=jax.ShapeDtypeStruct((nc, N_GB * STATE_W), F32),
        compiler_params=pltpu.CompilerParams(dimension_semantics=("parallel",), vmem_limit_bytes=VMEM_CAP),
    )(dy, wout)
    ge, dacat = _chunk_scan_bwd(ds, s, acat)

    def body_in(u_ref, dy_ref, d_ref, ge_ref, w_ref, du_ref, dd_ref):
        xs = [_step_rows(u_ref, j, nc).astype(MXU_DTYPE) for j in range(L)]
        dys = [_step_rows(dy_ref, tt, nc).astype(MXU_DTYPE) for tt in range(L)]
        ge = ge_ref[...].astype(MXU_DTYPE)
        for i in range(L):
            dyc = jnp.concatenate(dys[i:], axis=1)
            taps = jnp.concatenate([d_ref[tt - i] for tt in range(i, L)], axis=1).astype(MXU_DTYPE)
            du_ref[pl.ds(i, nc, stride=L), :] = (
                _dot_nt(dyc, taps) + _dot_nt(ge, w_ref[i * GB_LANES:(i + 1) * GB_LANES, :]))
        for j in range(L):
            m = _dot_tn(xs[j], jnp.concatenate(dys[j:], axis=1))
            for tau in range(L - j):
                part = m[:, tau * GB_LANES:(tau + 1) * GB_LANES]
                if j == 0:
                    dd_ref[tau] = part
                else:
                    dd_ref[tau] += part
        same = _lane_group((GB_LANES, GB_LANES), 0, 4) == _lane_group((GB_LANES, GB_LANES), 1, 4)
        for tau in range(L):
            dd_ref[tau] = _fold_lanes(jnp.where(same, dd_ref[tau], 0.0), (64, 32, 16))

    du, ddblk = pl.pallas_call(
        body_in, name="ssm_bwd_in", grid=(N_GB,),
        in_specs=[lanes, lanes, taps_spec, state,
                  pl.BlockSpec((None, L * GB_LANES, STATE_W), lambda b: (b, 0, 0))],
        out_specs=[lanes, taps_spec],
        out_shape=[jax.ShapeDtypeStruct((t, D_SSM), F32), jax.ShapeDtypeStruct(dblk.shape, F32)],
        compiler_params=pltpu.CompilerParams(dimension_semantics=("parallel",), vmem_limit_bytes=VMEM_CAP),
    )(proj, dy, dblk, ge, wend)

    def body_w(u_ref, dy_ref, ge_ref, s_ref, dwe_ref, dwo_ref):
        ge = ge_ref[...].astype(MXU_DTYPE)
        sb = s_ref[...].astype(MXU_DTYPE)
        keep = _lane_group((GB_LANES, STATE_W), 0, 4) == _lane_group((GB_LANES, STATE_W), 1, 6)
        low = lax.broadcasted_iota(jnp.int32, (GB_LANES, 2 * SSM_STATE), 1) < SSM_STATE

        def fold_state(z):
            z = z[:, :GB_STATE // 2] + z[:, GB_STATE // 2:]
            z = z[:, :GB_STATE // 4] + z[:, GB_STATE // 4:]
            return _fold_lanes(z, (SSM_STATE,))

        for j in range(L):
            z = jnp.where(keep, _dot_tn(_step_rows(u_ref, j, nc).astype(MXU_DTYPE), ge), 0.0)
            dwe_ref[j * GB_LANES:(j + 1) * GB_LANES, :] = jnp.where(
                low, fold_state(z[:, :GB_STATE]), fold_state(z[:, GB_STATE:]))
        own = _lane_group((STATE_W, GB_LANES), 0, 6) == _lane_group((STATE_W, GB_LANES), 1, 4)
        chunk = lax.broadcasted_iota(jnp.int32, (STATE_W, GB_LANES), 1) >> 4
        for half in range(L // GROUP_BLOCK):
            acc = jnp.zeros((STATE_W, GB_LANES), F32)
            for k in range(GROUP_BLOCK):
                tt = half * GROUP_BLOCK + k
                z = jnp.where(own, _dot_tn(sb, _step_rows(dy_ref, tt, nc).astype(MXU_DTYPE)), 0.0)
                acc = acc + jnp.where(chunk == k, _fold_lanes(z, (64, 32, 16)), 0.0)
            dwo_ref[:, half * GB_LANES:(half + 1) * GB_LANES] = acc

    dwend, dwout = pl.pallas_call(
        body_w, name="ssm_bwd_w", grid=(N_GB,),
        in_specs=[lanes, lanes, state, state],
        out_specs=[pl.BlockSpec((None, L * GB_LANES, 2 * SSM_STATE), lambda b: (b, 0, 0)),
                   pl.BlockSpec((None, STATE_W, L * SSM_GROUP), lambda b: (b, 0, 0))],
        out_shape=[jax.ShapeDtypeStruct((N_GB, L * GB_LANES, 2 * SSM_STATE), F32),
                   jax.ShapeDtypeStruct((N_GB, STATE_W, L * SSM_GROUP), F32)],
        compiler_params=pltpu.CompilerParams(dimension_semantics=("parallel",), vmem_limit_bytes=VMEM_CAP),
    )(proj, dy, ge, s)
    return du, (ddblk[:, :, :, :SSM_GROUP], dwend, dwout, dacat)


def _t5_bucket(dist):
    max_exact = N_BUCKETS // 2
    is_small = dist < max_exact
    d = jnp.maximum(dist, 1).astype(F32)
    large = max_exact + (jnp.log(d / max_exact) / math.log(MAX_DISTANCE / max_exact)
                         * (N_BUCKETS - max_exact)).astype(jnp.int32)
    large = jnp.minimum(large, N_BUCKETS - 1)
    return jnp.where(is_small, dist, large)


def _band_bias(rel_bias_table):
    i = jnp.arange(BLOCK)[:, None]
    j = jnp.arange(BLOCK)[None, :]
    bucket = _t5_bucket(jnp.where(j > i, BLOCK + i - j, i - j))
    onehot = (bucket[:, :, None] == jnp.arange(N_BUCKETS)[None, None, :]).astype(F32)
    return jnp.einsum("qsb,bh->hqs", onehot, rel_bias_table, precision=lax.Precision.HIGHEST)


KV_PAIR = 2
HEADS_PER_STEP = KV_PAIR * Q_PER_KV
Q_LANES = HEADS_PER_STEP * HEAD_DIM
SLAB = 2 * HEAD_DIM
Q_COL0 = OFF_Q * COL // Q_LANES
K_COL0 = OFF_K * COL // SLAB
V_COL0 = OFF_V * COL // SLAB


def _attn_specs():
    q_spec = pl.BlockSpec((BLOCK, Q_LANES), lambda m, n: (n, Q_COL0 + m))
    k_prev = pl.BlockSpec((BLOCK, SLAB), lambda m, n: (jnp.maximum(n - 1, 0), K_COL0 + m))
    k_cur = pl.BlockSpec((BLOCK, SLAB), lambda m, n: (n, K_COL0 + m))
    v_prev = pl.BlockSpec((BLOCK, SLAB), lambda m, n: (jnp.maximum(n - 1, 0), V_COL0 + m))
    v_cur = pl.BlockSpec((BLOCK, SLAB), lambda m, n: (n, V_COL0 + m))
    bias_spec = pl.BlockSpec((HEADS_PER_STEP, BLOCK, BLOCK), lambda m, n: (m, 0, 0))
    sink_spec = pl.BlockSpec(memory_space=pltpu.SMEM)
    wide = pl.BlockSpec((BLOCK, Q_LANES), lambda m, n: (n, m))
    pair = pl.BlockSpec((BLOCK, SLAB), lambda m, n: (n, m))
    return [sink_spec, q_spec, k_prev, k_cur, v_prev, v_cur, bias_spec], wide, pair


def _low_lanes(shape):
    return lax.broadcasted_iota(jnp.int32, shape, 1) < HEAD_DIM


def _pair_halves(ref):
    kb = ref[...]
    sw = pltpu.roll(kb, HEAD_DIM, 1)
    lo = _low_lanes(kb.shape)
    zero = jnp.zeros_like(kb)
    first = (jnp.where(lo, kb, zero).astype(MXU_DTYPE), jnp.where(lo, zero, sw).astype(MXU_DTYPE))
    second = (jnp.where(lo, sw, zero).astype(MXU_DTYPE), jnp.where(lo, zero, kb).astype(MXU_DTYPE))
    return first, second


def _fold_pair(acc):
    f = [x + pltpu.roll(x, HEAD_DIM, 1) for x in acc]
    return jnp.where(_low_lanes(f[0].shape), f[0], f[1])


def _from_prev(n):
    row = lax.broadcasted_iota(jnp.int32, (BLOCK, BLOCK), 0)
    col = lax.broadcasted_iota(jnp.int32, (BLOCK, BLOCK), 1)
    prev = col > row
    return prev, jnp.where(jnp.logical_and(n == 0, prev), NEG_INF, 0.0)


def _softmax_sink(s, sink):
    m = jnp.maximum(jnp.max(s, axis=1, keepdims=True), sink)
    e = jnp.exp(s - m)
    es = jnp.exp(sink - m)
    inv = 1.0 / (jnp.sum(e, axis=1, keepdims=True) + es)
    return e * inv, es * inv


def _stack_pair(prev_halves, own_halves, a):
    return jnp.concatenate([prev_halves[a][0], prev_halves[a][1], own_halves[a][0], own_halves[a][1]], axis=0)


def _split_heads(x4, prev):
    return [jnp.where(prev, x4[:, e * BLOCK:(e + 1) * BLOCK], x4[:, (2 + e) * BLOCK:(3 + e) * BLOCK]) for e in range(2)]


def _spread_heads(x, prev):
    return jnp.concatenate([jnp.where(prev, x[0], 0.0), jnp.where(prev, x[1], 0.0),
                            jnp.where(prev, 0.0, x[0]), jnp.where(prev, 0.0, x[1])], axis=1)


def _attn_forward(proj, bias, sinks):
    t = proj.shape[0]
    in_specs, wide, _ = _attn_specs()

    def body(sink_ref, q_ref, kp_ref, kc_ref, vp_ref, vc_ref, bias_ref, o_ref):
        m, n = pl.program_id(0), pl.program_id(1)
        kp, kc, vp, vc = (_pair_halves(r) for r in (kp_ref, kc_ref, vp_ref, vc_ref))
        keys = [_stack_pair(kp, kc, a) for a in range(KV_PAIR)]
        vals = [_stack_pair(vp, vc, a) for a in range(KV_PAIR)]
        prev, edge = _from_prev(n)
        for s in range(HEADS_PER_STEP // 2):
            a = s // (Q_PER_KV // 2)
            q = q_ref[:, s * SLAB:(s + 1) * SLAB].astype(MXU_DTYPE)
            logits = _split_heads(_dot_nt(q, keys[a]), prev)
            probs = [_softmax_sink(logits[e] * ATTN_SCALE + bias_ref[2 * s + e] + edge,
                                   sink_ref[m * HEADS_PER_STEP + 2 * s + e])[0] for e in range(2)]
            o_ref[:, s * SLAB:(s + 1) * SLAB] = _dot(_spread_heads(probs, prev), vals[a])

    return pl.pallas_call(
        body, name="attn_fwd", grid=(N_KV_HEADS // KV_PAIR, t // BLOCK),
        in_specs=in_specs, out_specs=wide,
        out_shape=jax.ShapeDtypeStruct((t, D_ATTN), F32),
        compiler_params=pltpu.CompilerParams(dimension_semantics=("parallel", "arbitrary")),
    )(sinks, proj, proj, proj, proj, proj, bias)


def _attn_backward(proj, bias, sinks, d_attn):
    t = proj.shape[0]
    in_specs, wide, pair = _attn_specs()
    bias_spec = in_specs[-1]
    sink_out = pl.BlockSpec((HEADS_PER_STEP, 8, 128), lambda m, n: (m, 0, 0))

    def body(sink_ref, q_ref, kp_ref, kc_ref, vp_ref, vc_ref, bias_ref, do_ref,
             dq_ref, dka_ref, dkb_ref, dva_ref, dvb_ref, dbias_ref, dsink_ref):
        m, n = pl.program_id(0), pl.program_id(1)

        @pl.when(n == 0)
        def _():
            dbias_ref[...] = jnp.zeros_like(dbias_ref)
            dsink_ref[...] = jnp.zeros_like(dsink_ref)

        kp, kc, vp, vc = (_pair_halves(r) for r in (kp_ref, kc_ref, vp_ref, vc_ref))
        keys = [_stack_pair(kp, kc, a) for a in range(KV_PAIR)]
        vals = [_stack_pair(vp, vc, a) for a in range(KV_PAIR)]
        prev, edge = _from_prev(n)
        lo = _low_lanes((BLOCK, SLAB))
        dk = [[jnp.zeros((BLOCK, SLAB), F32) for _ in range(KV_PAIR)] for _ in range(2)]
        dv = [[jnp.zeros((BLOCK, SLAB), F32) for _ in range(KV_PAIR)] for _ in range(2)]
        for s in range(HEADS_PER_STEP // 2):
            a = s // (Q_PER_KV // 2)
            q = q_ref[:, s * SLAB:(s + 1) * SLAB].astype(MXU_DTYPE)
            do = do_ref[:, s * SLAB:(s + 1) * SLAB].astype(MXU_DTYPE)
            logits = _split_heads(_dot_nt(q, keys[a]), prev)
            dprobs = _split_heads(_dot_nt(do, vals[a]), prev)
            probs, dlogits = [], []
            for e in range(2):
                h = 2 * s + e
                p, ps = _softmax_sink(logits[e] * ATTN_SCALE + bias_ref[h] + edge, sink_ref[m * HEADS_PER_STEP + h])
                delta = jnp.sum(p * dprobs[e], axis=1, keepdims=True)
                ds = p * (dprobs[e] - delta)
                dbias_ref[h] += ds
                dsink_ref[h] += jnp.broadcast_to(jnp.sum(-ps * delta, axis=0, keepdims=True), (8, 128))
                probs.append(p)
                dlogits.append(ds)
            ds4 = _spread_heads(dlogits, prev).astype(MXU_DTYPE)
            p4 = _spread_heads(probs, prev).astype(MXU_DTYPE)
            dq_ref[:, s * SLAB:(s + 1) * SLAB] = (_dot(ds4, keys[a]) * ATTN_SCALE).astype(dq_ref.dtype)
            rk = _dot_tn(ds4, q)
            rv = _dot_tn(p4, do)
            for which in range(2):
                top = 2 * which * BLOCK
                dk[which][a] = dk[which][a] + jnp.where(lo, rk[top:top + BLOCK], rk[top + BLOCK:top + 2 * BLOCK])
                dv[which][a] = dv[which][a] + jnp.where(lo, rv[top:top + BLOCK], rv[top + BLOCK:top + 2 * BLOCK])
        dkb_ref[...] = _fold_pair(dk[0]) * ATTN_SCALE
        dka_ref[...] = _fold_pair(dk[1]) * ATTN_SCALE
        dvb_ref[...] = _fold_pair(dv[0])
        dva_ref[...] = _fold_pair(dv[1])

    kv_shape = jax.ShapeDtypeStruct((t, D_KV), F32)
    return pl.pallas_call(
        body, name="attn_bwd", grid=(N_KV_HEADS // KV_PAIR, t // BLOCK),
        in_specs=in_specs + [wide],
        out_specs=[wide, pair, pair, pair, pair, bias_spec, sink_out],
        out_shape=[jax.ShapeDtypeStruct((t, D_ATTN), MXU_DTYPE), kv_shape, kv_shape, kv_shape, kv_shape,
                   jax.ShapeDtypeStruct(bias.shape, F32), jax.ShapeDtypeStruct((N_Q_HEADS, 8, 128), F32)],
        compiler_params=pltpu.CompilerParams(dimension_semantics=("parallel", "arbitrary")),
    )(sinks, proj, proj, proj, proj, proj, bias, d_attn)


def _shift_blocks(cur, prev):
    return cur + jnp.concatenate([prev[BLOCK:], jnp.zeros_like(prev[:BLOCK])], axis=0)


def _mesh_pos():
    return lax.axis_index("x"), lax.axis_index("y"), lax.axis_index("c")


def _all_gather(x, *, name):
    def body(x_ref, out_ref, send_sems, recv_sems, local_sem):
        x, y, c = _mesh_pos()
        me, sibling = (x, y, c), (x, y, 1 - c)
        chips = [(1 - x, y), (x, 1 - y), (1 - x, 1 - y)]

        def slot(px, py, pc):
            return out_ref.at[4 * px + 2 * py + pc]

        def copy(k, block, to, src=None):
            return pltpu.make_async_remote_copy(
                src_ref=slot(*block) if src is None else src, dst_ref=slot(*block),
                send_sem=send_sems.at[k], recv_sem=recv_sems.at[k],
                device_id=to, device_id_type=pl.DeviceIdType.MESH)

        mine = pltpu.make_async_copy(x_ref, slot(*me), local_sem)
        mine.start()
        first = [copy(0, me, sibling, src=x_ref)]
        first += [copy(1 + j, me, (*chip, c), src=x_ref) for j, chip in enumerate(chips)]
        for cp in first:
            cp.start()
        passed = [copy(4 + j, (*chip, c), sibling) for j, chip in enumerate(chips)]
        for j, chip in enumerate(chips):
            copy(1 + j, (*chip, c), me).wait_recv()
            passed[j].start()
        copy(0, sibling, me).wait_recv()
        for j, chip in enumerate(chips):
            copy(4 + j, (*chip, 1 - c), me).wait_recv()
        for cp in first + passed:
            cp.wait_send()
        mine.wait()

    return pl.pallas_call(
        body, name=name,
        in_specs=[pl.BlockSpec(memory_space=pl.ANY)],
        out_specs=pl.BlockSpec(memory_space=pl.ANY),
        out_shape=jax.ShapeDtypeStruct((N_DEV,) + x.shape, x.dtype),
        scratch_shapes=[pltpu.SemaphoreType.DMA((7,)), pltpu.SemaphoreType.DMA((7,)), pltpu.SemaphoreType.DMA],
    )(x)


_HBM = pl.BlockSpec(memory_space=pltpu.HBM)
_SEM = pl.BlockSpec(memory_space=pltpu.SEMAPHORE)
_DATAFLOW = pltpu.SideEffectType.DATAFLOW_SIDE_EFFECTING


def _peers():
    x, y, c = _mesh_pos()
    others = []
    for k in range(1, N_DEV):
        px, py, pc = x ^ (k >> 2), y ^ ((k >> 1) & 1), c ^ (k & 1)
        others.append(((px, py, pc), 4 * px + 2 * py + pc))
    return 4 * x + 2 * y + c, others


def _split_start(bufs, plan, n_copies, *, name):
    nb = len(bufs)

    def body(*refs):
        send_sems, recv_sems, token = refs[nb], refs[nb + 1], refs[-1]
        for k, (src, dst, pos, _) in enumerate(plan(*refs[:nb])):
            pltpu.make_async_remote_copy(src_ref=src, dst_ref=dst, send_sem=send_sems.at[k], recv_sem=recv_sems.at[k],
                                         device_id=pos, device_id_type=pl.DeviceIdType.MESH).start()
        token[...] = jnp.zeros_like(token)

    return pl.pallas_call(
        body, name=name,
        out_shape=(pltpu.SemaphoreType.DMA((n_copies,)), pltpu.SemaphoreType.DMA((n_copies,)),
                   *[pltpu.HBM(b.shape, b.dtype) for b in bufs], jax.ShapeDtypeStruct((8, 128), F32)),
        in_specs=(_HBM,) * nb, out_specs=(_SEM, _SEM) + (_HBM,) * nb + (pl.BlockSpec(memory_space=pltpu.VMEM),),
        input_output_aliases={i: 2 + i for i in range(nb)},
        compiler_params=pltpu.CompilerParams(has_side_effects=_DATAFLOW),
    )(*[pltpu.with_memory_space_constraint(b, pltpu.HBM) for b in bufs])


def _split_wait(started, plan, after, *, name):
    send_sems, recv_sems, *thru = started[:-1]
    nb = len(thru)

    def body(*refs):
        send_sems, recv_sems = refs[nb], refs[nb + 1]
        for k, (src, _, pos, arrive) in enumerate(plan(*refs[:nb])):
            copy = pltpu.make_async_remote_copy(
                src_ref=src, dst_ref=arrive, send_sem=send_sems.at[k], recv_sem=recv_sems.at[k],
                device_id=pos, device_id_type=pl.DeviceIdType.MESH)
            copy.wait_send()
            copy.wait_recv()

    return pl.pallas_call(
        body, name=name,
        out_shape=tuple(pltpu.HBM(b.shape, b.dtype) for b in thru),
        in_specs=(_HBM,) * nb + (_SEM, _SEM, pl.BlockSpec(memory_space=pl.ANY)), out_specs=(_HBM,) * nb,
        input_output_aliases={i: i for i in range(nb)},
        compiler_params=pltpu.CompilerParams(has_side_effects=_DATAFLOW),
    )(*thru, send_sems, recv_sems, after)


def _plan_scatter(x_ref, land_ref):
    me, others = _peers()
    return [(x_ref.at[idx], land_ref.at[me], pos, land_ref.at[idx]) for pos, idx in others]


def _plan_gather(x_ref, land_ref):
    me, others = _peers()
    return [(x_ref, land_ref.at[me], pos, land_ref.at[idx]) for pos, idx in others]


def _near_and_far():
    x, y, c = _mesh_pos()
    chips = [(1 - x, y), (x, 1 - y), (1 - x, 1 - y)]
    near = [(x, y, 1 - c)] + [(px, py, c) for px, py in chips]
    relay = [(4 * px + 2 * py + c, 4 * px + 2 * py + 1 - c) for px, py in chips]
    return 4 * x + 2 * y + c, near, (x, y, 1 - c), relay


def _plan_gather_near(x_ref, land_ref):
    me, near, _, _ = _near_and_far()
    return [(x_ref, land_ref.at[me], pos, land_ref.at[4 * pos[0] + 2 * pos[1] + pos[2]]) for pos in near]


def _plan_gather_relay(land_ref):
    _, _, sibling, relay = _near_and_far()
    return [(land_ref.at[mine], land_ref.at[mine], sibling, land_ref.at[theirs]) for mine, theirs in relay]


def _landing_zone(own):
    me, _ = _peers()
    return lax.dynamic_update_index_in_dim(lax.empty((N_DEV,) + own.shape, own.dtype), own, me, 0)


def _scatter_start(x, *, name):
    if x.ndim == 2:
        return _split_start((x, _landing_zone(x)), _plan_gather, N_DEV - 1, name=name)
    me, _ = _peers()
    own = lax.dynamic_index_in_dim(x, me, 0, keepdims=False)
    return _split_start((x, _landing_zone(own)), _plan_scatter, N_DEV - 1, name=name)


def _scatter_wait(started, after, *, name):
    plan = _plan_gather if started[2].ndim == 2 else _plan_scatter
    return _split_wait(started, plan, after, name=name)[1]


def _adamw_math(w, g, m, v):
    m = ADAM_B1 * m + (1.0 - ADAM_B1) * g
    v = ADAM_B2 * v + (1.0 - ADAM_B2) * (g * g)
    m_hat = m / (1.0 - ADAM_B1 ** ADAM_STEP)
    v_hat = v / (1.0 - ADAM_B2 ** ADAM_STEP)
    delta = -ADAM_LR * (m_hat / (jnp.sqrt(v_hat) + ADAM_EPS) + ADAM_WD * w)
    return delta, m, v


def _adamw_reduce(parts, w, m, v, *, name, tr):
    r, c = w.shape
    tr = min(tr, r)
    spec = pl.BlockSpec((tr, c), lambda i: (i, 0))

    def body(p_ref, w_ref, m_ref, v_ref, g_ref, d_ref, nm_ref, nv_ref):
        g = p_ref[0].astype(F32)
        for s in range(1, N_DEV):
            g = g + p_ref[s].astype(F32)
        delta, nm, nv = _adamw_math(w_ref[...], g, m_ref[...], v_ref[...])
        g_ref[...] = g
        d_ref[...] = delta
        nm_ref[...] = nm
        nv_ref[...] = nv

    return pl.pallas_call(
        body, name=name, grid=(r // tr,),
        in_specs=[pl.BlockSpec((N_DEV, tr, c), lambda i: (0, i, 0)), spec, spec, spec],
        out_specs=[spec] * 4,
        out_shape=[jax.ShapeDtypeStruct((r, c), F32)] * 4,
        compiler_params=pltpu.CompilerParams(
            dimension_semantics=("parallel",),
            vmem_limit_bytes=_vmem_limit(2 * 15 * _nbytes((tr, c), F32))),
    )(parts, w, m, v)


SMALL = ["ssm_lambda_re", "ssm_lambda_im", "ssm_b_re", "ssm_b_im", "ssm_c_re", "ssm_c_im", "ssm_d",
         "ssm_log_step", "attn_sinks", "rel_bias_table", "ln_gain", "ln_bias"]


def _pack(arrs):
    flat = jnp.concatenate([a.reshape(-1) for a in arrs])
    pad = (-flat.shape[0]) % 1024
    return jnp.pad(flat, (0, pad)).reshape(-1, 128)


def _unpack(packed, like):
    flat = packed.reshape(-1)
    out, pos = [], 0
    for a in like:
        out.append(flat[pos:pos + a.size].reshape(a.shape))
        pos += a.size
    return out


def kernel(x, w_in, ssm_lambda_re, ssm_lambda_im, ssm_b_re, ssm_b_im, ssm_c_re, ssm_c_im, ssm_d, ssm_log_step, w_glu, attn_sinks, rel_bias_table, w_branch_ssm, w_branch_attn, w_out, ln_gain, ln_bias, loss_target, m_w_in, m_ssm_lambda_re, m_ssm_lambda_im, m_ssm_b_re, m_ssm_b_im, m_ssm_c_re, m_ssm_c_im, m_ssm_d, m_ssm_log_step, m_w_glu, m_attn_sinks, m_rel_bias_table, m_w_branch_ssm, m_w_branch_attn, m_w_out, m_ln_gain, m_ln_bias, v_w_in, v_ssm_lambda_re, v_ssm_lambda_im, v_ssm_b_re, v_ssm_b_im, v_ssm_c_re, v_ssm_c_im, v_ssm_d, v_ssm_log_step, v_w_glu, v_attn_sinks, v_rel_bias_table, v_w_branch_ssm, v_w_branch_attn, v_w_out, v_ln_gain, v_ln_bias):
    t = x.shape[1]
    xs = x[0]
    target = loss_target[0]
    col_in = w_in.shape[2]
    col_br = w_glu.shape[2]
    row_out = w_out.shape[1]

    w_in_mx = w_in[0].astype(MXU_DTYPE)
    near = _split_start((w_in_mx, _landing_zone(w_in_mx)), _plan_gather_near, 4, name="gather_w_in_near_start")
    ssm_params = (ssm_lambda_re[0], ssm_lambda_im[0], ssm_b_re[0], ssm_b_im[0], ssm_c_re[0], ssm_c_im[0],
                  ssm_log_step[0] + near[-1][0, 0])
    mats, mats_vjp = jax.vjp(_ssm_matrices, *ssm_params)
    mats_mx = _ssm_expand(*mats[:3]) + (mats[3],)
    bias, bias_vjp = jax.vjp(_band_bias, rel_bias_table)
    sinks = attn_sinks[0]
    d_skip = ssm_d
    _, landed = _split_wait(near, _plan_gather_near, mats_mx[1], name="gather_w_in_near_wait")
    relay = _split_start((landed,), _plan_gather_relay, 3, name="gather_w_in_relay_start")
    x_mx = (xs + relay[-1][0, 0]).astype(MXU_DTYPE)
    (g_in,) = _split_wait(relay, _plan_gather_relay, x_mx, name="gather_w_in_relay_wait")
    win = g_in.transpose(1, 0, 2).reshape(D_MODEL, D_IN)
    hold = g_in[0, 0, 0] * 0
    three = jnp.concatenate([w_glu[0], w_branch_ssm[0], w_branch_attn[0]], axis=0).astype(MXU_DTYPE)
    sent_three = _scatter_start(three + hold, name="gather_w_1024_start")
    sent_wout = _scatter_start(w_out[0].astype(MXU_DTYPE) + hold, name="gather_w_out_start")

    proj = _mm(x_mx, win, name="in_proj", tm=2048, tn=512, tk=2048, after=(sent_three[4], sent_wout[4]))
    y_conv, states = _ssm_forward(proj, mats_mx)

    def f_gelu(yv, u, d):
        ys = yv + d * u
        return ys, _gelu(ys)

    y_ssm, glu_in = _ew(f_gelu, [(y_conv, "row", 0), (proj, "row", OFF_U), (d_skip, "vec", 0)],
                        [(COL, F32, "row"), (COL, MXU_DTYPE, "row")], rows=t, cw=COL, ncb=4, tr=1024, name="ssm_gelu")
    g_three = _scatter_wait(sent_three, glu_in, name="gather_w_1024_wait")
    three_full = g_three.transpose(1, 0, 2).reshape(3 * D_SSM, N_DEV * col_br)
    wglu, wbs, wba = three_full[:D_SSM], three_full[D_SSM:2 * D_SSM], three_full[2 * D_SSM:]
    glu = _mm(glu_in, wglu, name="glu_proj", tm=2048, tn=512, tk=1024)

    def f_hssm(ga, gb, z):
        return (ga * _sigmoid(gb) * _silu(z),)

    (h_ssm,) = _ew(f_hssm, [(glu, "row", 0), (glu, "row", 4), (proj, "row", OFF_ZS)],
                   [(COL, MXU_DTYPE, "row")], rows=t, cw=COL, ncb=4, tr=1024, name="ssm_gate")

    attn = _attn_forward(proj, bias, sinks)

    def f_hattn(a, z):
        return (a * _silu(z),)

    (h_attn,) = _ew(f_hattn, [(attn, "row", 0), (proj, "row", OFF_ZA)], [(COL, MXU_DTYPE, "row")],
                    rows=t, cw=COL, ncb=4, tr=1024, name="attn_gate")
    p_ssm = _mm(h_ssm, wbs, name="branch_ssm", tm=2048, tn=512, tk=1024)
    p_attn = _mm(h_attn, wba, name="branch_attn", tm=2048, tn=512, tk=1024)

    def f_merge(ps, pa, ls, la):
        return (_sigmoid(ls) * ps + _sigmoid(la) * pa,)

    (merged,) = _ew(f_merge, [(p_ssm, "row", 0), (p_attn, "row", 0), (proj, "row", OFF_GS), (proj, "row", OFF_GA)],
                    [(COL, MXU_DTYPE, "row")], rows=t, cw=COL, ncb=8, tr=1024, name="merge")
    wout = _scatter_wait(sent_wout, merged, name="gather_w_out_wait").reshape(D_MODEL, D_MODEL)
    out = _mm(merged, wout, name="out_proj", tm=2048, tn=512, tk=2048)

    def f_norm(xv, ov, tg, gain, lbias):
        r = DEEPNORM_ALPHA * xv + ov
        mu = jnp.mean(r, axis=1, keepdims=True)
        cen = r - mu
        var = jnp.mean(cen * cen, axis=1, keepdims=True)
        rstd = lax.rsqrt(var + LN_EPS)
        xhat = cen * rstd
        yv = xhat * gain + lbias
        diff = yv - tg
        row_loss = 0.5 * jnp.mean(diff * diff, axis=1, keepdims=True)
        loss = jnp.broadcast_to(jnp.sum(row_loss, axis=0, keepdims=True), (1, 128))
        dy = diff * (1.0 / D_MODEL)
        dgain = jnp.sum(dy * xhat, axis=0, keepdims=True)
        dbias = jnp.sum(dy, axis=0, keepdims=True)
        dxh = dy * gain
        dr = rstd * (dxh - jnp.mean(dxh, axis=1, keepdims=True) - xhat * jnp.mean(dxh * xhat, axis=1, keepdims=True))
        return dr, loss, dgain, dbias

    dr, loss_part, g_ln_gain, g_ln_bias = _ew(
        f_norm, [(xs, "row", 0), (out, "row", 0), (target, "row", 0), (ln_gain, "vec", 0), (ln_bias, "vec", 0)],
        [(D_MODEL, F32, "row"), (128, F32, "acc"), (D_MODEL, F32, "acc"), (D_MODEL, F32, "acc")],
        rows=t, cw=D_MODEL, ncb=1, tr=256, name="norm_loss")

    def scatter_cols(g, cols):
        return g.reshape(g.shape[0], N_DEV, cols).transpose(1, 0, 2)

    gw_out = _mm(merged, dr, ta=True, out_dtype=WIRE_DTYPE, name="grad_w_out", tm=2048, tn=512, tk=1024)
    sent_out = _scatter_start(gw_out.reshape(N_DEV, row_out, D_MODEL), name="scatter_g_out_start")
    d_merged = _mm(dr, wout, tb=True, name="d_merged", tm=2048, tn=512, tk=2048, after=(sent_out[4],))

    def b_merge(dm, ps, pa, ls, la):
        gs, ga = _sigmoid(ls), _sigmoid(la)
        return dm * gs, dm * ga, dm * ps * gs * (1.0 - gs), dm * pa * ga * (1.0 - ga)

    dp_ssm, dp_attn, dgl_s, dgl_a = _ew(
        b_merge, [(d_merged, "row", 0), (p_ssm, "row", 0), (p_attn, "row", 0), (proj, "row", OFF_GS), (proj, "row", OFF_GA)],
        [(COL, MXU_DTYPE, "row")] * 4, rows=t, cw=COL, ncb=8, tr=1024, name="merge_bwd")
    gw_bs = _mm(h_ssm, dp_ssm, ta=True, out_dtype=WIRE_DTYPE, name="grad_w_branch_ssm", tm=1024, tn=512, tk=1024)
    gw_ba = _mm(h_attn, dp_attn, ta=True, out_dtype=WIRE_DTYPE, name="grad_w_branch_attn", tm=1024, tn=512, tk=1024)
    dh_ssm = _mm(dp_ssm, wbs, tb=True, name="d_h_ssm", tm=2048, tn=512, tk=2048)
    dh_attn = _mm(dp_attn, wba, tb=True, name="d_h_attn", tm=2048, tn=512, tk=2048)

    def b_hssm(dh, ga, gb, z):
        sg = _sigmoid(gb)
        dgate = dh * _silu(z)
        return dgate * sg, dgate * ga * sg * (1.0 - sg), dh * ga * sg * _silu_grad(z)

    dglu_a, dglu_b, dz_ssm = _ew(b_hssm, [(dh_ssm, "row", 0), (glu, "row", 0), (glu, "row", 4), (proj, "row", OFF_ZS)],
                                 [(COL, MXU_DTYPE, "row")] * 3, rows=t, cw=COL, ncb=4, tr=1024, name="ssm_gate_bwd")
    dglu = jnp.concatenate([dglu_a, dglu_b], axis=1)
    gw_glu = _mm(glu_in, dglu, ta=True, out_dtype=WIRE_DTYPE, name="grad_w_glu", tm=1024, tn=512, tk=1024)
    sent_three = _scatter_start(scatter_cols(jnp.concatenate([gw_glu, gw_bs, gw_ba], axis=0), col_br),
                                name="scatter_g_1024_start")
    dglu_in = _mm(dglu, wglu, tb=True, name="d_glu_in", tm=2048, tn=512, tk=2048, after=(sent_three[4],))

    def b_gelu(dgi, ys, u):
        dys = dgi * _gelu_grad(ys)
        return dys, jnp.sum(dys * u, axis=0, keepdims=True)

    dy_ssm, g_ssm_d = _ew(b_gelu, [(dglu_in, "row", 0), (y_ssm, "row", 0), (proj, "row", OFF_U)],
                          [(COL, F32, "row"), (COL, F32, "acc")],
                          rows=t, cw=COL, ncb=4, tr=1024, name="ssm_gelu_bwd")
    du_ssm, dmats = _ssm_backward(dy_ssm, proj, states, mats_mx)
    du = (du_ssm + d_skip * dy_ssm).astype(MXU_DTYPE)
    g_lre, g_lim, g_bre, g_bim, g_cre, g_cim, g_lstep = mats_vjp(dmats)

    def b_hattn(dh, a, z):
        return dh * _silu(z), dh * a * _silu_grad(z)

    d_attn, dz_attn = _ew(b_hattn, [(dh_attn, "row", 0), (attn, "row", 0), (proj, "row", OFF_ZA)],
                          [(COL, F32, "row"), (COL, MXU_DTYPE, "row")], rows=t, cw=COL, ncb=4, tr=1024, name="attn_gate_bwd")
    dq, dka, dkb, dva, dvb, dbias, dsink = _attn_backward(proj, bias, sinks, d_attn)
    dk = _shift_blocks(dka, dkb).astype(MXU_DTYPE)
    dv = _shift_blocks(dva, dvb).astype(MXU_DTYPE)
    (g_table,) = bias_vjp(dbias)
    g_sinks = dsink[:, 0, 0]

    dproj = jnp.concatenate([du, dz_ssm, dq, dk, dv, dz_attn, dgl_s, dgl_a], axis=1)
    gw_in = _mm(x_mx, dproj, ta=True, out_dtype=WIRE_DTYPE, name="grad_w_in", tm=2048, tn=512, tk=1024)
    sent_in = _scatter_start(scatter_cols(gw_in, col_in), name="scatter_g_in_start")
    grad_x = _mm(dproj, win, tb=True, add=dr, add_scale=DEEPNORM_ALPHA, name="grad_x", tm=1024, tn=1024, tk=2176,
                 after=(sent_in[4],))

    parts_out = _scatter_wait(sent_out, grad_x, name="scatter_g_out_wait")
    parts_three = _scatter_wait(sent_three, parts_out, name="scatter_g_1024_wait")
    parts_in = _scatter_wait(sent_in, parts_three, name="scatter_g_in_wait")

    o_in = _adamw_reduce(parts_in, w_in[0], m_w_in[0], v_w_in[0], name="adamw_w_in", tr=128)
    three_w = jnp.concatenate([w_glu[0], w_branch_ssm[0], w_branch_attn[0]], axis=0)
    three_m = jnp.concatenate([m_w_glu[0], m_w_branch_ssm[0], m_w_branch_attn[0]], axis=0)
    three_v = jnp.concatenate([v_w_glu[0], v_w_branch_ssm[0], v_w_branch_attn[0]], axis=0)
    o_three = _adamw_reduce(parts_three, three_w, three_m, three_v, name="adamw_w_1024", tr=512)
    o_out = _adamw_reduce(parts_out, w_out[0], m_w_out[0], v_w_out[0], name="adamw_w_out", tr=128)

    small_w = [ssm_lambda_re, ssm_lambda_im, ssm_b_re, ssm_b_im, ssm_c_re, ssm_c_im, ssm_d, ssm_log_step,
               attn_sinks, rel_bias_table, ln_gain, ln_bias]
    small_m = [m_ssm_lambda_re, m_ssm_lambda_im, m_ssm_b_re, m_ssm_b_im, m_ssm_c_re, m_ssm_c_im, m_ssm_d,
               m_ssm_log_step, m_attn_sinks, m_rel_bias_table, m_ln_gain, m_ln_bias]
    small_v = [v_ssm_lambda_re, v_ssm_lambda_im, v_ssm_b_re, v_ssm_b_im, v_ssm_c_re, v_ssm_c_im, v_ssm_d,
               v_ssm_log_step, v_attn_sinks, v_rel_bias_table, v_ln_gain, v_ln_bias]
    small_g = [g_lre, g_lim, g_bre, g_bim, g_cre, g_cim, g_ssm_d, g_lstep, g_sinks, g_table, g_ln_gain, g_ln_bias]
    parts_small = _all_gather(_pack(small_g), name="gather_g_small")
    o_small = _adamw_reduce(parts_small, _pack(small_w), _pack(small_m), _pack(small_v), name="adamw_small", tr=2160)
    sg, sd, sm, sv = [_unpack(o, small_w) for o in o_small]

    loss = lax.psum(loss_part[0, 0], MESH_AXES)

    def big(o, idx):
        g_in_, g_three_, g_out_ = o_in[idx], o_three[idx], o_out[idx]
        return {"w_in": g_in_[None], "w_glu": g_three_[None, :D_SSM], "w_branch_ssm": g_three_[None, D_SSM:2 * D_SSM],
                "w_branch_attn": g_three_[None, 2 * D_SSM:], "w_out": g_out_[None]}

    order = ["w_in", "ssm_lambda_re", "ssm_lambda_im", "ssm_b_re", "ssm_b_im", "ssm_c_re", "ssm_c_im", "ssm_d",
             "ssm_log_step", "w_glu", "attn_sinks", "rel_bias_table", "w_branch_ssm", "w_branch_attn", "w_out",
             "ln_gain", "ln_bias"]
    outs = [loss, grad_x[None]]
    for idx, small in enumerate([sg, sd, sm, sv]):
        table = big(None, idx)
        table.update(dict(zip(SMALL, small)))
        outs += [table[n] for n in order]
    return tuple(outs)
```

```python
import functools
import math

import jax
import jax.numpy as jnp
from jax import lax
from jax.experimental import pallas as pl
from jax.experimental.pallas import tpu as pltpu

F32 = jnp.float32
MXU_DTYPE = jnp.bfloat16
WIRE_DTYPE = jnp.bfloat16

D_MODEL = 2048
D_SSM = 1024
SSM_GROUP = 16
N_GROUPS = 64
SSM_STATE = 64
N_Q_HEADS = 16
N_KV_HEADS = 4
Q_PER_KV = 4
HEAD_DIM = 64
D_ATTN = 1024
D_KV = 256
WINDOW = 128
BLOCK = 128
N_BUCKETS = 32
MAX_DISTANCE = 128
D_IN = 8704
DEEPNORM_ALPHA = 2.0 ** 0.25
LN_EPS = 1e-5
NEG_INF = -1e30
ATTN_SCALE = HEAD_DIM ** -0.5

ADAM_LR = 0.001
ADAM_B1 = 0.9
ADAM_B2 = 0.999
ADAM_EPS = 1e-08
ADAM_WD = 0.01
ADAM_STEP = 10

N_DEV = 8
SSM_CHUNK = 16
GROUP_BLOCK = 8
N_GB = N_GROUPS // GROUP_BLOCK
GB_LANES = GROUP_BLOCK * SSM_GROUP
GB_STATE = GROUP_BLOCK * SSM_STATE
COL = 256
OFF_U, OFF_ZS, OFF_Q, OFF_K, OFF_V, OFF_ZA, OFF_GS, OFF_GA = 0, 4, 8, 12, 13, 14, 18, 26

VMEM_CAP = 56 * 1024 * 1024
MESH_AXES = ("x", "y", "c")


def _vmem_limit(block_bytes):
    return int(min(max(3 * block_bytes, 16 * 1024 * 1024), VMEM_CAP))


def _nbytes(shape, dtype):
    return math.prod(shape) * jnp.dtype(dtype).itemsize


def _tile(n, pref):
    if n <= pref:
        return n
    t = (pref // 128) * 128
    while t >= 128:
        if n % t == 0:
            return t
        t -= 128
    return n


def _mm(a, b, *, name, ta=False, tb=False, out_dtype=F32, tm=1024, tn=512, tk=512, add=None, add_scale=1.0,
        after=()):
    squeeze = a.ndim == 2
    if squeeze:
        a, b = a[None], b[None]
        if add is not None:
            add = add[None]
    nb = a.shape[0]
    m, k = (a.shape[2], a.shape[1]) if ta else (a.shape[1], a.shape[2])
    n = b.shape[1] if tb else b.shape[2]
    tm, tn, tk = _tile(m, tm), _tile(n, tn), _tile(k, tk)
    nk = k // tk
    dn = (((0 if ta else 1,), (1 if tb else 0,)), ((), ()))

    a_spec = (pl.BlockSpec((None, tk, tm), lambda g, i, j, kk: (g, kk, i)) if ta
              else pl.BlockSpec((None, tm, tk), lambda g, i, j, kk: (g, i, kk)))
    b_spec = (pl.BlockSpec((None, tn, tk), lambda g, i, j, kk: (g, j, kk)) if tb
              else pl.BlockSpec((None, tk, tn), lambda g, i, j, kk: (g, kk, j)))
    o_spec = pl.BlockSpec((None, tm, tn), lambda g, i, j, kk: (g, i, j))
    in_specs = [a_spec, b_spec]
    operands = [a, b]
    if add is not None:
        in_specs.append(o_spec)
        operands.append(add)
    for tok in after:
        in_specs.append(pl.BlockSpec(memory_space=pl.ANY))
        operands.append(tok)
    n_in = len(operands)

    def body(*refs):
        a_ref, b_ref = refs[0], refs[1]
        add_ref = refs[2] if add is not None else None
        o_ref = refs[n_in]
        acc_ref = refs[-1]
        kk = pl.program_id(3)
        part = lax.dot_general(a_ref[...].astype(MXU_DTYPE), b_ref[...].astype(MXU_DTYPE), dn,
                               preferred_element_type=F32)

        @pl.when(kk == 0)
        def _():
            acc_ref[...] = part

        @pl.when(kk > 0)
        def _():
            acc_ref[...] += part

        @pl.when(kk == nk - 1)
        def _():
            r = acc_ref[...]
            if add_ref is not None:
                r = r + add_scale * add_ref[...]
            o_ref[...] = r.astype(out_dtype)

    blocks = (_nbytes((tm, tk), a.dtype) + _nbytes((tk, tn), b.dtype) + _nbytes((tm, tn), out_dtype)
              + (_nbytes((tm, tn), F32) if add is not None else 0))
    out = pl.pallas_call(
        body,
        name=name,
        grid=(nb, m // tm, n // tn, nk),
        in_specs=in_specs,
        out_specs=o_spec,
        out_shape=jax.ShapeDtypeStruct((nb, m, n), out_dtype),
        scratch_shapes=[pltpu.VMEM((tm, tn), F32)],
        compiler_params=pltpu.CompilerParams(
            dimension_semantics=("parallel", "parallel", "parallel", "arbitrary"),
            vmem_limit_bytes=_vmem_limit(2 * blocks + 2 * _nbytes((tm, tn), F32))),
    )(*operands)
    return out[0] if squeeze else out


def _ew(fn, ins, outs, *, rows, cw, ncb, tr, name):
    tr = min(tr, rows)
    n_in = len(ins)

    def row_map(off):
        return lambda j, i: (i, off + j)

    def vec_map(off):
        return lambda j, i: (0, off + j)

    in_specs = []
    for arr, kind, off in ins:
        if kind == "row":
            in_specs.append(pl.BlockSpec((tr, cw), row_map(off)))
        else:
            in_specs.append(pl.BlockSpec((1, cw), vec_map(off)))
    out_specs, out_shapes = [], []
    for bw, dt, kind in outs:
        if kind == "row":
            out_specs.append(pl.BlockSpec((tr, bw), row_map(0)))
            out_shapes.append(jax.ShapeDtypeStruct((rows, ncb * bw), dt))
        else:
            out_specs.append(pl.BlockSpec((1, bw), vec_map(0)))
            out_shapes.append(jax.ShapeDtypeStruct((1, ncb * bw), F32))

    def body(*refs):
        i = pl.program_id(1)
        vals = fn(*[r[...] for r in refs[:n_in]])
        for r, (bw, dt, kind), v in zip(refs[n_in:], outs, vals):
            if kind == "row":
                r[...] = v.astype(dt)
            else:
                @pl.when(i == 0)
                def _(r=r):
                    r[...] = jnp.zeros_like(r)

                r[...] += v

    blocks = sum(_nbytes((tr, cw), a.dtype) for a, kind, _ in ins if kind == "row")
    blocks += sum(_nbytes((tr, bw), dt) for bw, dt, kind in outs if kind == "row")
    res = pl.pallas_call(
        body,
        name=name,
        grid=(ncb, rows // tr),
        in_specs=in_specs,
        out_specs=out_specs,
        out_shape=out_shapes,
        compiler_params=pltpu.CompilerParams(
            dimension_semantics=("parallel", "arbitrary"),
            vmem_limit_bytes=_vmem_limit(4 * blocks)),
    )(*[a for a, _, _ in ins])
    return res


def _sigmoid(x):
    return 1.0 / (1.0 + jnp.exp(-x))


INV_SQRT2 = 0.7071067811865476
INV_SQRT_2PI = 0.3989422804014327


def _gelu(x):
    return 0.5 * x * (1.0 + lax.erf(x * INV_SQRT2))


def _gelu_grad(x):
    return 0.5 * (1.0 + lax.erf(x * INV_SQRT2)) + x * INV_SQRT_2PI * jnp.exp(-0.5 * x * x)


def _silu(x):
    return x * _sigmoid(x)


def _silu_grad(x):
    s = _sigmoid(x)
    return s * (1.0 + x * (1.0 - s))


def _ssm_matrices(lam_re, lam_im, b_re, b_im, c_re, c_im, log_step):
    L = SSM_CHUNK
    step = jnp.exp(log_step)[:, None]
    ea, eb = lam_re * step, lam_im * step
    mag = jnp.exp(ea)
    lbr, lbi = mag * jnp.cos(eb), mag * jnp.sin(eb)
    den = lam_re * lam_re + lam_im * lam_im
    nr, ni = lbr - 1.0, lbi
    cr = (nr * lam_re + ni * lam_im) / den
    ci = (ni * lam_re - nr * lam_im) / den
    bbr = cr[..., None] * b_re - ci[..., None] * b_im
    bbi = cr[..., None] * b_im + ci[..., None] * b_re
    taus = jnp.arange(L + 1, dtype=F32)[:, None, None]
    pmag = jnp.exp(taus * ea[None])
    pwr, pwi = pmag * jnp.cos(taus * eb[None]), pmag * jnp.sin(taus * eb[None])
    mr = c_re[None] * pwr[:L, :, None, :] - c_im[None] * pwi[:L, :, None, :]
    mi = c_re[None] * pwi[:L, :, None, :] + c_im[None] * pwr[:L, :, None, :]
    bbr_t, bbi_t = jnp.transpose(bbr, (0, 2, 1)), jnp.transpose(bbi, (0, 2, 1))
    kk = jnp.sum(mr[..., None, :] * bbr_t[None, :, None] - mi[..., None, :] * bbi_t[None, :, None], axis=-1)
    taps = jnp.transpose(kk.reshape(L, N_GB, GROUP_BLOCK, SSM_GROUP, SSM_GROUP), (1, 0, 2, 4, 3))
    taps = taps.reshape(N_GB, L, GB_LANES, SSM_GROUP)
    rev_r, rev_i = pwr[L - 1 - jnp.arange(L)], pwi[L - 1 - jnp.arange(L)]
    wer = rev_r[..., None] * bbr[None] - rev_i[..., None] * bbi[None]
    wei = rev_r[..., None] * bbi[None] + rev_i[..., None] * bbr[None]

    def rows_in(w):
        w = jnp.transpose(w.reshape(L, N_GB, GROUP_BLOCK, SSM_STATE, SSM_GROUP), (1, 0, 2, 4, 3))
        return w.reshape(N_GB, L * GB_LANES, SSM_STATE)

    wend = jnp.concatenate([rows_in(wer), rows_in(wei)], axis=2)
    m1r = c_re[None] * pwr[1:, :, None, :] - c_im[None] * pwi[1:, :, None, :]
    m1i = c_re[None] * pwi[1:, :, None, :] + c_im[None] * pwr[1:, :, None, :]

    def rows_out(m):
        m = jnp.transpose(m.reshape(L, N_GB, GROUP_BLOCK, SSM_GROUP, SSM_STATE), (1, 2, 4, 0, 3))
        return m.reshape(N_GB, GB_STATE, L * SSM_GROUP)

    wout = jnp.concatenate([rows_out(m1r), rows_out(-m1i)], axis=1)
    acat = jnp.concatenate([pwr[L].reshape(N_GB, 1, GB_STATE), pwi[L].reshape(N_GB, 1, GB_STATE)], axis=2)
    return taps, wend, wout, acat


def _lane_group(shape, axis, shift):
    return (lax.broadcasted_iota(jnp.int32, shape, axis) >> shift) & (GROUP_BLOCK - 1)


def _ssm_expand(taps, wend, wout):
    L = SSM_CHUNK
    taps = jnp.pad(taps, ((0, 0), (0, 0), (0, 0), (0, GB_LANES - SSM_GROUP)))

    def body(t_ref, we_ref, wo_ref, d_ref, web_ref, wob_ref):
        def rc(shape):
            return lax.broadcasted_iota(jnp.int32, shape, 0), lax.broadcasted_iota(jnp.int32, shape, 1)

        r, c = rc((GB_LANES, GB_LANES))
        spread = ((r < SSM_GROUP) & (r == (c & (SSM_GROUP - 1)))).astype(MXU_DTYPE)
        same = _lane_group((GB_LANES, GB_LANES), 0, 4) == _lane_group((GB_LANES, GB_LANES), 1, 4)
        for tau in range(L):
            full = jnp.dot(t_ref[tau].astype(MXU_DTYPE), spread, preferred_element_type=F32)
            d_ref[tau] = jnp.where(same, full, 0.0).astype(d_ref.dtype)
        r, c = rc((2 * SSM_STATE, STATE_W))
        part = (r >> 6) == (c >> 9)
        spread = (part & ((r & (SSM_STATE - 1)) == (c & (SSM_STATE - 1)))).astype(MXU_DTYPE)
        keep = _lane_group((GB_LANES, STATE_W), 0, 4) == _lane_group((GB_LANES, STATE_W), 1, 6)
        for j in range(L):
            rows = slice(j * GB_LANES, (j + 1) * GB_LANES)
            full = jnp.dot(we_ref[rows, :].astype(MXU_DTYPE), spread, preferred_element_type=F32)
            web_ref[rows, :] = jnp.where(keep, full, 0.0).astype(web_ref.dtype)
        r, c = rc((GB_LANES, GB_LANES))
        own = _lane_group((STATE_W, GB_LANES), 0, 6) == _lane_group((STATE_W, GB_LANES), 1, 4)
        for tt in range(L):
            half, k = tt // GROUP_BLOCK, tt % GROUP_BLOCK
            spread = (((r >> 4) == k) & ((r & (SSM_GROUP - 1)) == (c & (SSM_GROUP - 1)))).astype(MXU_DTYPE)
            src = wo_ref[:, half * GB_LANES:(half + 1) * GB_LANES].astype(MXU_DTYPE)
            full = jnp.dot(src, spread, preferred_element_type=F32)
            wob_ref[:, tt * GB_LANES:(tt + 1) * GB_LANES] = jnp.where(own, full, 0.0).astype(wob_ref.dtype)

    def spec(shape):
        return pl.BlockSpec((None,) + shape[1:], lambda b: (b,) + (0,) * (len(shape) - 1))

    out_shapes = [(N_GB, L, GB_LANES, GB_LANES), (N_GB, L * GB_LANES, STATE_W), (N_GB, STATE_W, L * GB_LANES)]
    return tuple(pl.pallas_call(
        body, name="ssm_expand", grid=(N_GB,),
        in_specs=[spec(taps.shape), spec(wend.shape), spec(wout.shape)],
        out_specs=[spec(s) for s in out_shapes],
        out_shape=[jax.ShapeDtypeStruct(s, MXU_DTYPE) for s in out_shapes],
        compiler_params=pltpu.CompilerParams(dimension_semantics=("parallel",), vmem_limit_bytes=VMEM_CAP),
    )(taps, wend, wout))


def _dot(a, b):
    return jnp.dot(a.astype(MXU_DTYPE), b.astype(MXU_DTYPE), preferred_element_type=F32)


def _dot_nt(a, b):
    return lax.dot_general(a.astype(MXU_DTYPE), b.astype(MXU_DTYPE), (((1,), (1,)), ((), ())),
                           preferred_element_type=F32)


def _dot_tn(a, b):
    return lax.dot_general(a.astype(MXU_DTYPE), b.astype(MXU_DTYPE), (((0,), (0,)), ((), ())),
                           preferred_element_type=F32)


STATE_W = 2 * GB_STATE


def _chunk_scan(e, acat):
    nc = e.shape[0]
    spec = pl.BlockSpec((nc, STATE_W), lambda b: (0, b))
    aspec = pl.BlockSpec((None, 1, STATE_W), lambda b: (b, 0, 0))

    def body(e_ref, a_ref, s_ref):
        a_r, a_i = a_ref[:, :GB_STATE], a_ref[:, GB_STATE:]

        def step(c, carry):
            s_r, s_i = carry
            s_ref[pl.ds(c, 1), :GB_STATE] = s_r
            s_ref[pl.ds(c, 1), GB_STATE:] = s_i
            e_r = e_ref[pl.ds(c, 1), :GB_STATE]
            e_i = e_ref[pl.ds(c, 1), GB_STATE:]
            return (a_r * s_r - a_i * s_i + e_r, a_r * s_i + a_i * s_r + e_i)

        zero = jnp.zeros((1, GB_STATE), F32)
        lax.fori_loop(0, nc, step, (zero, zero))

    return pl.pallas_call(
        body, name="ssm_chunk_scan", grid=(N_GB,),
        in_specs=[spec, aspec], out_specs=spec,
        out_shape=jax.ShapeDtypeStruct(e.shape, F32),
        compiler_params=pltpu.CompilerParams(dimension_semantics=("parallel",)),
    )(e, acat)


def _chunk_scan_bwd(ds, s, acat):
    nc = ds.shape[0]
    spec = pl.BlockSpec((nc, STATE_W), lambda b: (0, b))
    aspec = pl.BlockSpec((None, 1, STATE_W), lambda b: (b, 0, 0))

    def body(ds_ref, s_ref, a_ref, ge_ref, da_ref):
        a_r, a_i = a_ref[:, :GB_STATE], a_ref[:, GB_STATE:]

        def step(t, carry):
            g_r, g_i, d_r, d_i = carry
            c = nc - 1 - t
            ge_ref[pl.ds(c, 1), :GB_STATE] = g_r
            ge_ref[pl.ds(c, 1), GB_STATE:] = g_i
            s_r = s_ref[pl.ds(c, 1), :GB_STATE]
            s_i = s_ref[pl.ds(c, 1), GB_STATE:]
            d_r = d_r + g_r * s_r + g_i * s_i
            d_i = d_i + g_i * s_r - g_r * s_i
            n_r = ds_ref[pl.ds(c, 1), :GB_STATE] + a_r * g_r + a_i * g_i
            n_i = ds_ref[pl.ds(c, 1), GB_STATE:] + a_r * g_i - a_i * g_r
            return (n_r, n_i, d_r, d_i)

        zero = jnp.zeros((1, GB_STATE), F32)
        _, _, d_r, d_i = lax.fori_loop(0, nc, step, (zero, zero, zero, zero))
        da_ref[:, :GB_STATE] = d_r
        da_ref[:, GB_STATE:] = d_i

    return pl.pallas_call(
        body, name="ssm_chunk_scan_bwd", grid=(N_GB,),
        in_specs=[spec, spec, aspec], out_specs=[spec, aspec],
        out_shape=[jax.ShapeDtypeStruct(ds.shape, F32), jax.ShapeDtypeStruct(acat.shape, F32)],
        compiler_params=pltpu.CompilerParams(dimension_semantics=("parallel",)),
    )(ds, s, acat)


def _step_rows(ref, j, nc):
    return ref[pl.ds(j, nc, stride=SSM_CHUNK), :]


def _fold_lanes(z, widths):
    for w in widths:
        z = z + pltpu.roll(z, w, 1)
    return z


def _ssm_forward(proj, mats):
    dblk, wend, wout, acat = mats
    t = proj.shape[0]
    nc = t // SSM_CHUNK
    L = SSM_CHUNK
    lanes = pl.BlockSpec((t, GB_LANES), lambda b: (0, b))
    state = pl.BlockSpec((nc, STATE_W), lambda b: (0, b))

    def body_end(u_ref, w_ref, e_ref):
        x = jnp.concatenate([_step_rows(u_ref, j, nc).astype(MXU_DTYPE) for j in range(L)], axis=1)
        e_ref[...] = jnp.dot(x, w_ref[...], preferred_element_type=F32)

    e = pl.pallas_call(
        body_end, name="ssm_chunk_end", grid=(N_GB,),
        in_specs=[lanes, pl.BlockSpec((None, L * GB_LANES, STATE_W), lambda b: (b, 0, 0))],
        out_specs=state, out_shape=jax.ShapeDtypeStruct((nc, N_GB * STATE_W), F32),
        compiler_params=pltpu.CompilerParams(dimension_semantics=("parallel",), vmem_limit_bytes=VMEM_CAP),
    )(proj, wend)
    s = _chunk_scan(e, acat)

    def body_out(u_ref, d_ref, s_ref, w_ref, y_ref):
        xs = [_step_rows(u_ref, j, nc).astype(MXU_DTYPE) for j in range(L)]
        sb = s_ref[...].astype(MXU_DTYPE)
        for tt in range(L):
            xcat = jnp.concatenate(xs[:tt + 1], axis=1)
            taps = jnp.concatenate([d_ref[tt - j] for j in range(tt + 1)], axis=0).astype(MXU_DTYPE)
            y = (jnp.dot(xcat, taps, preferred_element_type=F32)
                 + jnp.dot(sb, w_ref[:, tt * GB_LANES:(tt + 1) * GB_LANES], preferred_element_type=F32))
            y_ref[pl.ds(tt, nc, stride=L), :] = y

    y = pl.pallas_call(
        body_out, name="ssm_chunk_out", grid=(N_GB,),
        in_specs=[lanes, pl.BlockSpec((None, L, GB_LANES, GB_LANES), lambda b: (b, 0, 0, 0)), state,
                  pl.BlockSpec((None, STATE_W, L * GB_LANES), lambda b: (b, 0, 0))],
        out_specs=lanes, out_shape=jax.ShapeDtypeStruct((t, D_SSM), F32),
        compiler_params=pltpu.CompilerParams(dimension_semantics=("parallel",), vmem_limit_bytes=VMEM_CAP),
    )(proj, dblk, s, wout)
    return y, s


def _ssm_backward(dy, proj, s, mats):
    dblk, wend, wout, acat = mats
    t = proj.shape[0]
    nc = t // SSM_CHUNK
    L = SSM_CHUNK
    lanes = pl.BlockSpec((t, GB_LANES), lambda b: (0, b))
    state = pl.BlockSpec((nc, STATE_W), lambda b: (0, b))
    taps_spec = pl.BlockSpec((None, L, GB_LANES, GB_LANES), lambda b: (b, 0, 0, 0))

    def body_state(dy_ref, w_ref, ds_ref):
        dyc = jnp.concatenate([_step_rows(dy_ref, tt, nc).astype(MXU_DTYPE) for tt in range(L)], axis=1)
        ds_ref[...] = _dot_nt(dyc, w_ref[...])

    ds = pl.pallas_call(
        body_state, name="ssm_bwd_state", grid=(N_GB,),
        in_specs=[lanes, pl.BlockSpec((None, STATE_W, L * GB_LANES), lambda b: (b, 0, 0))],
        out_specs=state, out_shape=jax.ShapeDtypeStruct((nc, N_GB * STATE_W), F32),
        compiler_params=pltpu.CompilerParams(dimension_semantics=("parallel",), vmem_limit_bytes=VMEM_CAP),
    )(dy, wout)
    ge, dacat = _chunk_scan_bwd(ds, s, acat)

    def body_in(u_ref, dy_ref, d_ref, ge_ref, w_ref, du_ref, dd_ref):
        xs = [_step_rows(u_ref, j, nc).astype(MXU_DTYPE) for j in range(L)]
        dys = [_step_rows(dy_ref, tt, nc).astype(MXU_DTYPE) for tt in range(L)]
        ge = ge_ref[...].astype(MXU_DTYPE)
        for i in range(L):
            dyc = jnp.concatenate(dys[i:], axis=1)
            taps = jnp.concatenate([d_ref[tt - i] for tt in range(i, L)], axis=1).astype(MXU_DTYPE)
            du_ref[pl.ds(i, nc, stride=L), :] = (
                _dot_nt(dyc, taps) + _dot_nt(ge, w_ref[i * GB_LANES:(i + 1) * GB_LANES, :]))
        for j in range(L):
            m = _dot_tn(xs[j], jnp.concatenate(dys[j:], axis=1))
            for tau in range(L - j):
                part = m[:, tau * GB_LANES:(tau + 1) * GB_LANES]
                if j == 0:
                    dd_ref[tau] = part
                else:
                    dd_ref[tau] += part
        same = _lane_group((GB_LANES, GB_LANES), 0, 4) == _lane_group((GB_LANES, GB_LANES), 1, 4)
        for tau in range(L):
            dd_ref[tau] = _fold_lanes(jnp.where(same, dd_ref[tau], 0.0), (64, 32, 16))

    du, ddblk = pl.pallas_call(
        body_in, name="ssm_bwd_in", grid=(N_GB,),
        in_specs=[lanes, lanes, taps_spec, state,
                  pl.BlockSpec((None, L * GB_LANES, STATE_W), lambda b: (b, 0, 0))],
        out_specs=[lanes, taps_spec],
        out_shape=[jax.ShapeDtypeStruct((t, D_SSM), F32), jax.ShapeDtypeStruct(dblk.shape, F32)],
        compiler_params=pltpu.CompilerParams(dimension_semantics=("parallel",), vmem_limit_bytes=VMEM_CAP),
    )(proj, dy, dblk, ge, wend)

    def body_w(u_ref, dy_ref, ge_ref, s_ref, dwe_ref, dwo_ref):
        ge = ge_ref[...].astype(MXU_DTYPE)
        sb = s_ref[...].astype(MXU_DTYPE)
        keep = _lane_group((GB_LANES, STATE_W), 0, 4) == _lane_group((GB_LANES, STATE_W), 1, 6)
        low = lax.broadcasted_iota(jnp.int32, (GB_LANES, 2 * SSM_STATE), 1) < SSM_STATE

        def fold_state(z):
            z = z[:, :GB_STATE // 2] + z[:, GB_STATE // 2:]
            z = z[:, :GB_STATE // 4] + z[:, GB_STATE // 4:]
            return _fold_lanes(z, (SSM_STATE,))

        for j in range(L):
            z = jnp.where(keep, _dot_tn(_step_rows(u_ref, j, nc).astype(MXU_DTYPE), ge), 0.0)
            dwe_ref[j * GB_LANES:(j + 1) * GB_LANES, :] = jnp.where(
                low, fold_state(z[:, :GB_STATE]), fold_state(z[:, GB_STATE:]))
        own = _lane_group((STATE_W, GB_LANES), 0, 6) == _lane_group((STATE_W, GB_LANES), 1, 4)
        chunk = lax.broadcasted_iota(jnp.int32, (STATE_W, GB_LANES), 1) >> 4
        for half in range(L // GROUP_BLOCK):
            acc = jnp.zeros((STATE_W, GB_LANES), F32)
            for k in range(GROUP_BLOCK):
                tt = half * GROUP_BLOCK + k
                z = jnp.where(own, _dot_tn(sb, _step_rows(dy_ref, tt, nc).astype(MXU_DTYPE)), 0.0)
                acc = acc + jnp.where(chunk == k, _fold_lanes(z, (64, 32, 16)), 0.0)
            dwo_ref[:, half * GB_LANES:(half + 1) * GB_LANES] = acc

    dwend, dwout = pl.pallas_call(
        body_w, name="ssm_bwd_w", grid=(N_GB,),
        in_specs=[lanes, lanes, state, state],
        out_specs=[pl.BlockSpec((None, L * GB_LANES, 2 * SSM_STATE), lambda b: (b, 0, 0)),
                   pl.BlockSpec((None, STATE_W, L * SSM_GROUP), lambda b: (b, 0, 0))],
        out_shape=[jax.ShapeDtypeStruct((N_GB, L * GB_LANES, 2 * SSM_STATE), F32),
                   jax.ShapeDtypeStruct((N_GB, STATE_W, L * SSM_GROUP), F32)],
        compiler_params=pltpu.CompilerParams(dimension_semantics=("parallel",), vmem_limit_bytes=VMEM_CAP),
    )(proj, dy, ge, s)
    return du, (ddblk[:, :, :, :SSM_GROUP], dwend, dwout, dacat)


def _t5_bucket(dist):
    max_exact = N_BUCKETS // 2
    is_small = dist < max_exact
    d = jnp.maximum(dist, 1).astype(F32)
    large = max_exact + (jnp.log(d / max_exact) / math.log(MAX_DISTANCE / max_exact)
                         * (N_BUCKETS - max_exact)).astype(jnp.int32)
    large = jnp.minimum(large, N_BUCKETS - 1)
    return jnp.where(is_small, dist, large)


def _band_bias(rel_bias_table):
    i = jnp.arange(BLOCK)[:, None]
    j = jnp.arange(BLOCK)[None, :]
    bucket = _t5_bucket(jnp.where(j > i, BLOCK + i - j, i - j))
    onehot = (bucket[:, :, None] == jnp.arange(N_BUCKETS)[None, None, :]).astype(F32)
    return jnp.einsum("qsb,bh->hqs", onehot, rel_bias_table, precision=lax.Precision.HIGHEST)


KV_PAIR = 2
HEADS_PER_STEP = KV_PAIR * Q_PER_KV
Q_LANES = HEADS_PER_STEP * HEAD_DIM
SLAB = 2 * HEAD_DIM
Q_COL0 = OFF_Q * COL // Q_LANES
K_COL0 = OFF_K * COL // SLAB
V_COL0 = OFF_V * COL // SLAB


def _attn_specs():
    q_spec = pl.BlockSpec((BLOCK, Q_LANES), lambda m, n: (n, Q_COL0 + m))
    k_prev = pl.BlockSpec((BLOCK, SLAB), lambda m, n: (jnp.maximum(n - 1, 0), K_COL0 + m))
    k_cur = pl.BlockSpec((BLOCK, SLAB), lambda m, n: (n, K_COL0 + m))
    v_prev = pl.BlockSpec((BLOCK, SLAB), lambda m, n: (jnp.maximum(n - 1, 0), V_COL0 + m))
    v_cur = pl.BlockSpec((BLOCK, SLAB), lambda m, n: (n, V_COL0 + m))
    bias_spec = pl.BlockSpec((HEADS_PER_STEP, BLOCK, BLOCK), lambda m, n: (m, 0, 0))
    sink_spec = pl.BlockSpec(memory_space=pltpu.SMEM)
    wide = pl.BlockSpec((BLOCK, Q_LANES), lambda m, n: (n, m))
    pair = pl.BlockSpec((BLOCK, SLAB), lambda m, n: (n, m))
    return [sink_spec, q_spec, k_prev, k_cur, v_prev, v_cur, bias_spec], wide, pair


def _low_lanes(shape):
    return lax.broadcasted_iota(jnp.int32, shape, 1) < HEAD_DIM


def _pair_halves(ref):
    kb = ref[...]
    sw = pltpu.roll(kb, HEAD_DIM, 1)
    lo = _low_lanes(kb.shape)
    zero = jnp.zeros_like(kb)
    first = (jnp.where(lo, kb, zero).astype(MXU_DTYPE), jnp.where(lo, zero, sw).astype(MXU_DTYPE))
    second = (jnp.where(lo, sw, zero).astype(MXU_DTYPE), jnp.where(lo, zero, kb).astype(MXU_DTYPE))
    return first, second


def _fold_pair(acc):
    f = [x + pltpu.roll(x, HEAD_DIM, 1) for x in acc]
    return jnp.where(_low_lanes(f[0].shape), f[0], f[1])


def _from_prev(n):
    row = lax.broadcasted_iota(jnp.int32, (BLOCK, BLOCK), 0)
    col = lax.broadcasted_iota(jnp.int32, (BLOCK, BLOCK), 1)
    prev = col > row
    return prev, jnp.where(jnp.logical_and(n == 0, prev), NEG_INF, 0.0)


def _softmax_sink(s, sink):
    m = jnp.maximum(jnp.max(s, axis=1, keepdims=True), sink)
    e = jnp.exp(s - m)
    es = jnp.exp(sink - m)
    inv = 1.0 / (jnp.sum(e, axis=1, keepdims=True) + es)
    return e * inv, es * inv


def _stack_pair(prev_halves, own_halves, a):
    return jnp.concatenate([prev_halves[a][0], prev_halves[a][1], own_halves[a][0], own_halves[a][1]], axis=0)


def _split_heads(x4, prev):
    return [jnp.where(prev, x4[:, e * BLOCK:(e + 1) * BLOCK], x4[:, (2 + e) * BLOCK:(3 + e) * BLOCK]) for e in range(2)]


def _spread_heads(x, prev):
    return jnp.concatenate([jnp.where(prev, x[0], 0.0), jnp.where(prev, x[1], 0.0),
                            jnp.where(prev, 0.0, x[0]), jnp.where(prev, 0.0, x[1])], axis=1)


def _attn_forward(proj, bias, sinks):
    t = proj.shape[0]
    in_specs, wide, _ = _attn_specs()

    def body(sink_ref, q_ref, kp_ref, kc_ref, vp_ref, vc_ref, bias_ref, o_ref):
        m, n = pl.program_id(0), pl.program_id(1)
        kp, kc, vp, vc = (_pair_halves(r) for r in (kp_ref, kc_ref, vp_ref, vc_ref))
        keys = [_stack_pair(kp, kc, a) for a in range(KV_PAIR)]
        vals = [_stack_pair(vp, vc, a) for a in range(KV_PAIR)]
        prev, edge = _from_prev(n)
        for s in range(HEADS_PER_STEP // 2):
            a = s // (Q_PER_KV // 2)
            q = q_ref[:, s * SLAB:(s + 1) * SLAB].astype(MXU_DTYPE)
            logits = _split_heads(_dot_nt(q, keys[a]), prev)
            probs = [_softmax_sink(logits[e] * ATTN_SCALE + bias_ref[2 * s + e] + edge,
                                   sink_ref[m * HEADS_PER_STEP + 2 * s + e])[0] for e in range(2)]
            o_ref[:, s * SLAB:(s + 1) * SLAB] = _dot(_spread_heads(probs, prev), vals[a])

    return pl.pallas_call(
        body, name="attn_fwd", grid=(N_KV_HEADS // KV_PAIR, t // BLOCK),
        in_specs=in_specs, out_specs=wide,
        out_shape=jax.ShapeDtypeStruct((t, D_ATTN), F32),
        compiler_params=pltpu.CompilerParams(dimension_semantics=("parallel", "arbitrary")),
    )(sinks, proj, proj, proj, proj, proj, bias)


def _attn_backward(proj, bias, sinks, d_attn):
    t = proj.shape[0]
    in_specs, wide, pair = _attn_specs()
    bias_spec = in_specs[-1]
    sink_out = pl.BlockSpec((HEADS_PER_STEP, 8, 128), lambda m, n: (m, 0, 0))

    def body(sink_ref, q_ref, kp_ref, kc_ref, vp_ref, vc_ref, bias_ref, do_ref,
             dq_ref, dka_ref, dkb_ref, dva_ref, dvb_ref, dbias_ref, dsink_ref):
        m, n = pl.program_id(0), pl.program_id(1)

        @pl.when(n == 0)
        def _():
            dbias_ref[...] = jnp.zeros_like(dbias_ref)
            dsink_ref[...] = jnp.zeros_like(dsink_ref)

        kp, kc, vp, vc = (_pair_halves(r) for r in (kp_ref, kc_ref, vp_ref, vc_ref))
        keys = [_stack_pair(kp, kc, a) for a in range(KV_PAIR)]
        vals = [_stack_pair(vp, vc, a) for a in range(KV_PAIR)]
        prev, edge = _from_prev(n)
        lo = _low_lanes((BLOCK, SLAB))
        dk = [[jnp.zeros((BLOCK, SLAB), F32) for _ in range(KV_PAIR)] for _ in range(2)]
        dv = [[jnp.zeros((BLOCK, SLAB), F32) for _ in range(KV_PAIR)] for _ in range(2)]
        for s in range(HEADS_PER_STEP // 2):
            a = s // (Q_PER_KV // 2)
            q = q_ref[:, s * SLAB:(s + 1) * SLAB].astype(MXU_DTYPE)
            do = do_ref[:, s * SLAB:(s + 1) * SLAB].astype(MXU_DTYPE)
            logits = _split_heads(_dot_nt(q, keys[a]), prev)
            dprobs = _split_heads(_dot_nt(do, vals[a]), prev)
            probs, dlogits = [], []
            for e in range(2):
                h = 2 * s + e
                p, ps = _softmax_sink(logits[e] * ATTN_SCALE + bias_ref[h] + edge, sink_ref[m * HEADS_PER_STEP + h])
                delta = jnp.sum(p * dprobs[e], axis=1, keepdims=True)
                ds = p * (dprobs[e] - delta)
                dbias_ref[h] += ds
                dsink_ref[h] += jnp.broadcast_to(jnp.sum(-ps * delta, axis=0, keepdims=True), (8, 128))
                probs.append(p)
                dlogits.append(ds)
            ds4 = _spread_heads(dlogits, prev).astype(MXU_DTYPE)
            p4 = _spread_heads(probs, prev).astype(MXU_DTYPE)
            dq_ref[:, s * SLAB:(s + 1) * SLAB] = (_dot(ds4, keys[a]) * ATTN_SCALE).astype(dq_ref.dtype)
            rk = _dot_tn(ds4, q)
            rv = _dot_tn(p4, do)
            for which in range(2):
                top = 2 * which * BLOCK
                dk[which][a] = dk[which][a] + jnp.where(lo, rk[top:top + BLOCK], rk[top + BLOCK:top + 2 * BLOCK])
                dv[which][a] = dv[which][a] + jnp.where(lo, rv[top:top + BLOCK], rv[top + BLOCK:top + 2 * BLOCK])
        dkb_ref[...] = _fold_pair(dk[0]) * ATTN_SCALE
        dka_ref[...] = _fold_pair(dk[1]) * ATTN_SCALE
        dvb_ref[...] = _fold_pair(dv[0])
        dva_ref[...] = _fold_pair(dv[1])

    kv_shape = jax.ShapeDtypeStruct((t, D_KV), F32)
    return pl.pallas_call(
        body, name="attn_bwd", grid=(N_KV_HEADS // KV_PAIR, t // BLOCK),
        in_specs=in_specs + [wide],
        out_specs=[wide, pair, pair, pair, pair, bias_spec, sink_out],
        out_shape=[jax.ShapeDtypeStruct((t, D_ATTN), MXU_DTYPE), kv_shape, kv_shape, kv_shape, kv_shape,
                   jax.ShapeDtypeStruct(bias.shape, F32), jax.ShapeDtypeStruct((N_Q_HEADS, 8, 128), F32)],
        compiler_params=pltpu.CompilerParams(dimension_semantics=("parallel", "arbitrary")),
    )(sinks, proj, proj, proj, proj, proj, bias, d_attn)


def _shift_blocks(cur, prev):
    return cur + jnp.concatenate([prev[BLOCK:], jnp.zeros_like(prev[:BLOCK])], axis=0)


def _mesh_pos():
    return lax.axis_index("x"), lax.axis_index("y"), lax.axis_index("c")


def _all_gather(x, *, name):
    def body(x_ref, out_ref, send_sems, recv_sems, local_sem):
        x, y, c = _mesh_pos()
        me, sibling = (x, y, c), (x, y, 1 - c)
        chips = [(1 - x, y), (x, 1 - y), (1 - x, 1 - y)]

        def slot(px, py, pc):
            return out_ref.at[4 * px + 2 * py + pc]

        def copy(k, block, to, src=None):
            return pltpu.make_async_remote_copy(
                src_ref=slot(*block) if src is None else src, dst_ref=slot(*block),
                send_sem=send_sems.at[k], recv_sem=recv_sems.at[k],
                device_id=to, device_id_type=pl.DeviceIdType.MESH)

        mine = pltpu.make_async_copy(x_ref, slot(*me), local_sem)
        mine.start()
        first = [copy(0, me, sibling, src=x_ref)]
        first += [copy(1 + j, me, (*chip, c), src=x_ref) for j, chip in enumerate(chips)]
        for cp in first:
            cp.start()
        passed = [copy(4 + j, (*chip, c), sibling) for j, chip in enumerate(chips)]
        for j, chip in enumerate(chips):
            copy(1 + j, (*chip, c), me).wait_recv()
            passed[j].start()
        copy(0, sibling, me).wait_recv()
        for j, chip in enumerate(chips):
            copy(4 + j, (*chip, 1 - c), me).wait_recv()
        for cp in first + passed:
            cp.wait_send()
        mine.wait()

    return pl.pallas_call(
        body, name=name,
        in_specs=[pl.BlockSpec(memory_space=pl.ANY)],
        out_specs=pl.BlockSpec(memory_space=pl.ANY),
        out_shape=jax.ShapeDtypeStruct((N_DEV,) + x.shape, x.dtype),
        scratch_shapes=[pltpu.SemaphoreType.DMA((7,)), pltpu.SemaphoreType.DMA((7,)), pltpu.SemaphoreType.DMA],
    )(x)


_HBM = pl.BlockSpec(memory_space=pltpu.HBM)
_SEM = pl.BlockSpec(memory_space=pltpu.SEMAPHORE)
_DATAFLOW = pltpu.SideEffectType.DATAFLOW_SIDE_EFFECTING


def _peers():
    x, y, c = _mesh_pos()
    others = []
    for k in range(1, N_DEV):
        px, py, pc = x ^ (k >> 2), y ^ ((k >> 1) & 1), c ^ (k & 1)
        others.append(((px, py, pc), 4 * px + 2 * py + pc))
    return 4 * x + 2 * y + c, others


def _split_start(bufs, plan, n_copies, *, name):
    nb = len(bufs)

    def body(*refs):
        send_sems, recv_sems, token = refs[nb], refs[nb + 1], refs[-1]
        for k, (src, dst, pos, _) in enumerate(plan(*refs[:nb])):
            pltpu.make_async_remote_copy(src_ref=src, dst_ref=dst, send_sem=send_sems.at[k], recv_sem=recv_sems.at[k],
                                         device_id=pos, device_id_type=pl.DeviceIdType.MESH).start()
        token[...] = jnp.zeros_like(token)

    return pl.pallas_call(
        body, name=name,
        out_shape=(pltpu.SemaphoreType.DMA((n_copies,)), pltpu.SemaphoreType.DMA((n_copies,)),
                   *[pltpu.HBM(b.shape, b.dtype) for b in bufs], jax.ShapeDtypeStruct((8, 128), F32)),
        in_specs=(_HBM,) * nb, out_specs=(_SEM, _SEM) + (_HBM,) * nb + (pl.BlockSpec(memory_space=pltpu.VMEM),),
        input_output_aliases={i: 2 + i for i in range(nb)},
        compiler_params=pltpu.CompilerParams(has_side_effects=_DATAFLOW),
    )(*[pltpu.with_memory_space_constraint(b, pltpu.HBM) for b in bufs])


def _split_wait(started, plan, after, *, name):
    send_sems, recv_sems, *thru = started[:-1]
    nb = len(thru)

    def body(*refs):
        send_sems, recv_sems = refs[nb], refs[nb + 1]
        for k, (src, _, pos, arrive) in enumerate(plan(*refs[:nb])):
            copy = pltpu.make_async_remote_copy(
                src_ref=src, dst_ref=arrive, send_sem=send_sems.at[k], recv_sem=recv_sems.at[k],
                device_id=pos, device_id_type=pl.DeviceIdType.MESH)
            copy.wait_send()
            copy.wait_recv()

    return pl.pallas_call(
        body, name=name,
        out_shape=tuple(pltpu.HBM(b.shape, b.dtype) for b in thru),
        in_specs=(_HBM,) * nb + (_SEM, _SEM, pl.BlockSpec(memory_space=pl.ANY)), out_specs=(_HBM,) * nb,
        input_output_aliases={i: i for i in range(nb)},
        compiler_params=pltpu.CompilerParams(has_side_effects=_DATAFLOW),
    )(*thru, send_sems, recv_sems, after)


def _plan_scatter(x_ref, land_ref):
    me, others = _peers()
    return [(x_ref.at[idx], land_ref.at[me], pos, land_ref.at[idx]) for pos, idx in others]


def _plan_gather(x_ref, land_ref):
    me, others = _peers()
    return [(x_ref, land_ref.at[me], pos, land_ref.at[idx]) for pos, idx in others]


def _near_and_far():
    x, y, c = _mesh_pos()
    chips = [(1 - x, y), (x, 1 - y), (1 - x, 1 - y)]
    near = [(x, y, 1 - c)] + [(px, py, c) for px, py in chips]
    relay = [(4 * px + 2 * py + c, 4 * px + 2 * py + 1 - c) for px, py in chips]
    return 4 * x + 2 * y + c, near, (x, y, 1 - c), relay


def _plan_gather_near(x_ref, land_ref):
    me, near, _, _ = _near_and_far()
    return [(x_ref, land_ref.at[me], pos, land_ref.at[4 * pos[0] + 2 * pos[1] + pos[2]]) for pos in near]


def _plan_gather_relay(land_ref):
    _, _, sibling, relay = _near_and_far()
    return [(land_ref.at[mine], land_ref.at[mine], sibling, land_ref.at[theirs]) for mine, theirs in relay]


def _landing_zone(own):
    me, _ = _peers()
    return lax.dynamic_update_index_in_dim(lax.empty((N_DEV,) + own.shape, own.dtype), own, me, 0)


def _scatter_start(x, *, name):
    if x.ndim == 2:
        return _split_start((x, _landing_zone(x)), _plan_gather, N_DEV - 1, name=name)
    me, _ = _peers()
    own = lax.dynamic_index_in_dim(x, me, 0, keepdims=False)
    return _split_start((x, _landing_zone(own)), _plan_scatter, N_DEV - 1, name=name)


def _scatter_wait(started, after, *, name):
    plan = _plan_gather if started[2].ndim == 2 else _plan_scatter
    return _split_wait(started, plan, after, name=name)[1]


def _adamw_math(w, g, m, v):
    m = ADAM_B1 * m + (1.0 - ADAM_B1) * g
    v = ADAM_B2 * v + (1.0 - ADAM_B2) * (g * g)
    m_hat = m / (1.0 - ADAM_B1 ** ADAM_STEP)
    v_hat = v / (1.0 - ADAM_B2 ** ADAM_STEP)
    delta = -ADAM_LR * (m_hat / (jnp.sqrt(v_hat) + ADAM_EPS) + ADAM_WD * w)
    return delta, m, v


def _adamw_reduce(parts, w, m, v, *, name, tr):
    r, c = w.shape
    tr = min(tr, r)
    spec = pl.BlockSpec((tr, c), lambda i: (i, 0))

    def body(p_ref, w_ref, m_ref, v_ref, g_ref, d_ref, nm_ref, nv_ref):
        g = p_ref[0].astype(F32)
        for s in range(1, N_DEV):
            g = g + p_ref[s].astype(F32)
        delta, nm, nv = _adamw_math(w_ref[...], g, m_ref[...], v_ref[...])
        g_ref[...] = g
        d_ref[...] = delta
        nm_ref[...] = nm
        nv_ref[...] = nv

    return pl.pallas_call(
        body, name=name, grid=(r // tr,),
        in_specs=[pl.BlockSpec((N_DEV, tr, c), lambda i: (0, i, 0)), spec, spec, spec],
        out_specs=[spec] * 4,
        out_shape=[jax.ShapeDtypeStruct((r, c), F32)] * 4,
        compiler_params=pltpu.CompilerParams(
            dimension_semantics=("parallel",),
            vmem_limit_bytes=_vmem_limit(2 * 15 * _nbytes((tr, c), F32))),
    )(parts, w, m, v)


def _sum_parts(parts, *, name):
    def body(p_ref, o_ref):
        g = p_ref[0]
        for s in range(1, N_DEV):
            g = g + p_ref[s]
        o_ref[...] = g

    return pl.pallas_call(
        body, name=name, out_shape=jax.ShapeDtypeStruct(parts.shape[1:], F32),
        compiler_params=pltpu.CompilerParams(vmem_limit_bytes=_vmem_limit(_nbytes(parts.shape, F32))),
    )(parts)


def _adamw_native(w, g, m, v, *, name):
    def body(w_ref, g_ref, m_ref, v_ref, d_ref, nm_ref, nv_ref):
        d_ref[...], nm_ref[...], nv_ref[...] = _adamw_math(w_ref[...], g_ref[...], m_ref[...], v_ref[...])

    return pl.pallas_call(body, name=name, out_shape=[jax.ShapeDtypeStruct(w.shape, F32)] * 3)(w, g, m, v)


SMALL = ["ssm_lambda_re", "ssm_lambda_im", "ssm_b_re", "ssm_b_im", "ssm_c_re", "ssm_c_im", "ssm_d",
         "ssm_log_step", "attn_sinks", "rel_bias_table", "ln_gain", "ln_bias"]


def _pack(arrs):
    flat = jnp.concatenate([a.reshape(-1) for a in arrs])
    pad = (-flat.shape[0]) % 1024
    return jnp.pad(flat, (0, pad)).reshape(-1, 128)


def _unpack(packed, like):
    flat = packed.reshape(-1)
    out, pos = [], 0
    for a in like:
        out.append(flat[pos:pos + a.size].reshape(a.shape))
        pos += a.size
    return out


def kernel(x, w_in, ssm_lambda_re, ssm_lambda_im, ssm_b_re, ssm_b_im, ssm_c_re, ssm_c_im, ssm_d, ssm_log_step, w_glu, attn_sinks, rel_bias_table, w_branch_ssm, w_branch_attn, w_out, ln_gain, ln_bias, loss_target, m_w_in, m_ssm_lambda_re, m_ssm_lambda_im, m_ssm_b_re, m_ssm_b_im, m_ssm_c_re, m_ssm_c_im, m_ssm_d, m_ssm_log_step, m_w_glu, m_attn_sinks, m_rel_bias_table, m_w_branch_ssm, m_w_branch_attn, m_w_out, m_ln_gain, m_ln_bias, v_w_in, v_ssm_lambda_re, v_ssm_lambda_im, v_ssm_b_re, v_ssm_b_im, v_ssm_c_re, v_ssm_c_im, v_ssm_d, v_ssm_log_step, v_w_glu, v_attn_sinks, v_rel_bias_table, v_w_branch_ssm, v_w_branch_attn, v_w_out, v_ln_gain, v_ln_bias):
    t = x.shape[1]
    xs = x[0]
    target = loss_target[0]
    col_in = w_in.shape[2]
    col_br = w_glu.shape[2]
    row_out = w_out.shape[1]

    w_in_mx = w_in[0].astype(MXU_DTYPE)
    near = _split_start((w_in_mx, _landing_zone(w_in_mx)), _plan_gather_near, 4, name="gather_w_in_near_start")
    ssm_params = (ssm_lambda_re[0], ssm_lambda_im[0], ssm_b_re[0], ssm_b_im[0], ssm_c_re[0], ssm_c_im[0],
                  ssm_log_step[0] + near[-1][0, 0])
    mats, mats_vjp = jax.vjp(_ssm_matrices, *ssm_params)
    mats_mx = _ssm_expand(*mats[:3]) + (mats[3],)
    sinks = attn_sinks[0]
    d_skip = ssm_d
    _, landed = _split_wait(near, _plan_gather_near, mats_mx[1], name="gather_w_in_near_wait")
    relay = _split_start((landed,), _plan_gather_relay, 3, name="gather_w_in_relay_start")
    bias, bias_vjp = jax.vjp(_band_bias, rel_bias_table + relay[-1][0, 0])
    x_mx = xs.astype(MXU_DTYPE)
    (g_in,) = _split_wait(relay, _plan_gather_relay, bias, name="gather_w_in_relay_wait")
    win = g_in.transpose(1, 0, 2).reshape(D_MODEL, D_IN)
    hold = g_in[0, 0, 0] * 0
    three = jnp.concatenate([w_glu[0], w_branch_ssm[0], w_branch_attn[0]], axis=0).astype(MXU_DTYPE)
    sent_three = _scatter_start(three + hold, name="gather_w_1024_start")
    sent_wout = _scatter_start(w_out[0].astype(MXU_DTYPE) + hold, name="gather_w_out_start")

    proj = _mm(x_mx, win, name="in_proj", tm=2048, tn=512, tk=2048, after=(sent_three[4], sent_wout[4]))
    y_conv, states = _ssm_forward(proj, mats_mx)

    def f_gelu(yv, u, d):
        ys = yv + d * u
        return ys, _gelu(ys)

    y_ssm, glu_in = _ew(f_gelu, [(y_conv, "row", 0), (proj, "row", OFF_U), (d_skip, "vec", 0)],
                        [(COL, F32, "row"), (COL, MXU_DTYPE, "row")], rows=t, cw=COL, ncb=4, tr=1024, name="ssm_gelu")
    g_three = _scatter_wait(sent_three, glu_in, name="gather_w_1024_wait")
    three_full = g_three.transpose(1, 0, 2).reshape(3 * D_SSM, N_DEV * col_br)
    wglu, wbs, wba = three_full[:D_SSM], three_full[D_SSM:2 * D_SSM], three_full[2 * D_SSM:]
    glu = _mm(glu_in, wglu, name="glu_proj", tm=2048, tn=512, tk=1024)

    def f_hssm(ga, gb, z):
        return (ga * _sigmoid(gb) * _silu(z),)

    (h_ssm,) = _ew(f_hssm, [(glu, "row", 0), (glu, "row", 4), (proj, "row", OFF_ZS)],
                   [(COL, MXU_DTYPE, "row")], rows=t, cw=COL, ncb=4, tr=1024, name="ssm_gate")

    attn = _attn_forward(proj, bias, sinks)

    def f_hattn(a, z):
        return (a * _silu(z),)

    (h_attn,) = _ew(f_hattn, [(attn, "row", 0), (proj, "row", OFF_ZA)], [(COL, MXU_DTYPE, "row")],
                    rows=t, cw=COL, ncb=4, tr=1024, name="attn_gate")
    p_ssm = _mm(h_ssm, wbs, name="branch_ssm", tm=2048, tn=512, tk=1024)
    p_attn = _mm(h_attn, wba, name="branch_attn", tm=2048, tn=512, tk=1024)

    def f_merge(ps, pa, ls, la):
        return (_sigmoid(ls) * ps + _sigmoid(la) * pa,)

    (merged,) = _ew(f_merge, [(p_ssm, "row", 0), (p_attn, "row", 0), (proj, "row", OFF_GS), (proj, "row", OFF_GA)],
                    [(COL, MXU_DTYPE, "row")], rows=t, cw=COL, ncb=8, tr=1024, name="merge")
    wout = _scatter_wait(sent_wout, merged, name="gather_w_out_wait").reshape(D_MODEL, D_MODEL)
    out = _mm(merged, wout, name="out_proj", tm=2048, tn=512, tk=2048)

    def f_norm(xv, ov, tg, gain, lbias):
        r = DEEPNORM_ALPHA * xv + ov
        mu = jnp.mean(r, axis=1, keepdims=True)
        cen = r - mu
        var = jnp.mean(cen * cen, axis=1, keepdims=True)
        rstd = lax.rsqrt(var + LN_EPS)
        xhat = cen * rstd
        yv = xhat * gain + lbias
        diff = yv - tg
        row_loss = 0.5 * jnp.mean(diff * diff, axis=1, keepdims=True)
        loss = jnp.broadcast_to(jnp.sum(row_loss, axis=0, keepdims=True), (1, 128))
        dy = diff * (1.0 / D_MODEL)
        dgain = jnp.sum(dy * xhat, axis=0, keepdims=True)
        dbias = jnp.sum(dy, axis=0, keepdims=True)
        dxh = dy * gain
        dr = rstd * (dxh - jnp.mean(dxh, axis=1, keepdims=True) - xhat * jnp.mean(dxh * xhat, axis=1, keepdims=True))
        return dr, loss, dgain, dbias

    dr, loss_part, g_ln_gain, g_ln_bias = _ew(
        f_norm, [(xs, "row", 0), (out, "row", 0), (target, "row", 0), (ln_gain, "vec", 0), (ln_bias, "vec", 0)],
        [(D_MODEL, F32, "row"), (128, F32, "acc"), (D_MODEL, F32, "acc"), (D_MODEL, F32, "acc")],
        rows=t, cw=D_MODEL, ncb=1, tr=256, name="norm_loss")

    def scatter_cols(g, cols):
        return g.reshape(g.shape[0], N_DEV, cols).transpose(1, 0, 2)

    gw_out = _mm(merged, dr, ta=True, out_dtype=WIRE_DTYPE, name="grad_w_out", tm=2048, tn=512, tk=1024)
    sent_out = _scatter_start(gw_out.reshape(N_DEV, row_out, D_MODEL), name="scatter_g_out_start")
    d_merged = _mm(dr, wout, tb=True, name="d_merged", tm=2048, tn=512, tk=2048, after=(sent_out[4],))

    def b_merge(dm, ps, pa, ls, la):
        gs, ga = _sigmoid(ls), _sigmoid(la)
        return dm * gs, dm * ga, dm * ps * gs * (1.0 - gs), dm * pa * ga * (1.0 - ga)

    dp_ssm, dp_attn, dgl_s, dgl_a = _ew(
        b_merge, [(d_merged, "row", 0), (p_ssm, "row", 0), (p_attn, "row", 0), (proj, "row", OFF_GS), (proj, "row", OFF_GA)],
        [(COL, MXU_DTYPE, "row")] * 4, rows=t, cw=COL, ncb=8, tr=1024, name="merge_bwd")
    gw_bs = _mm(h_ssm, dp_ssm, ta=True, out_dtype=WIRE_DTYPE, name="grad_w_branch_ssm", tm=1024, tn=512, tk=1024)
    gw_ba = _mm(h_attn, dp_attn, ta=True, out_dtype=WIRE_DTYPE, name="grad_w_branch_attn", tm=1024, tn=512, tk=1024)
    dh_ssm = _mm(dp_ssm, wbs, tb=True, name="d_h_ssm", tm=2048, tn=512, tk=2048)
    dh_attn = _mm(dp_attn, wba, tb=True, name="d_h_attn", tm=2048, tn=512, tk=2048)

    def b_hssm(dh, ga, gb, z):
        sg = _sigmoid(gb)
        dgate = dh * _silu(z)
        return dgate * sg, dgate * ga * sg * (1.0 - sg), dh * ga * sg * _silu_grad(z)

    dglu_a, dglu_b, dz_ssm = _ew(b_hssm, [(dh_ssm, "row", 0), (glu, "row", 0), (glu, "row", 4), (proj, "row", OFF_ZS)],
                                 [(COL, MXU_DTYPE, "row")] * 3, rows=t, cw=COL, ncb=4, tr=1024, name="ssm_gate_bwd")
    dglu = jnp.concatenate([dglu_a, dglu_b], axis=1)
    gw_glu = _mm(glu_in, dglu, ta=True, out_dtype=WIRE_DTYPE, name="grad_w_glu", tm=1024, tn=512, tk=1024)
    sent_three = _scatter_start(scatter_cols(jnp.concatenate([gw_glu, gw_bs, gw_ba], axis=0), col_br),
                                name="scatter_g_1024_start")
    dglu_in = _mm(dglu, wglu, tb=True, name="d_glu_in", tm=2048, tn=512, tk=2048, after=(sent_three[4],))

    def b_gelu(dgi, ys, u):
        dys = dgi * _gelu_grad(ys)
        return dys, jnp.sum(dys * u, axis=0, keepdims=True)

    dy_ssm, g_ssm_d = _ew(b_gelu, [(dglu_in, "row", 0), (y_ssm, "row", 0), (proj, "row", OFF_U)],
                          [(COL, F32, "row"), (COL, F32, "acc")],
                          rows=t, cw=COL, ncb=4, tr=1024, name="ssm_gelu_bwd")
    du_ssm, dmats = _ssm_backward(dy_ssm, proj, states, mats_mx)
    du = (du_ssm + d_skip * dy_ssm).astype(MXU_DTYPE)
    g_lre, g_lim, g_bre, g_bim, g_cre, g_cim, g_lstep = mats_vjp(dmats)

    def b_hattn(dh, a, z):
        return dh * _silu(z), dh * a * _silu_grad(z)

    d_attn, dz_attn = _ew(b_hattn, [(dh_attn, "row", 0), (attn, "row", 0), (proj, "row", OFF_ZA)],
                          [(COL, F32, "row"), (COL, MXU_DTYPE, "row")], rows=t, cw=COL, ncb=4, tr=1024, name="attn_gate_bwd")
    dq, dka, dkb, dva, dvb, dbias, dsink = _attn_backward(proj, bias, sinks, d_attn)
    dk = _shift_blocks(dka, dkb).astype(MXU_DTYPE)
    dv = _shift_blocks(dva, dvb).astype(MXU_DTYPE)
    (g_table,) = bias_vjp(dbias)
    g_sinks = dsink[:, 0, 0]

    dproj = jnp.concatenate([du, dz_ssm, dq, dk, dv, dz_attn, dgl_s, dgl_a], axis=1)
    gw_in = _mm(x_mx.T, dproj, out_dtype=WIRE_DTYPE, name="grad_w_in", tm=2048, tn=512, tk=1024)
    sent_in = _scatter_start(scatter_cols(gw_in, col_in), name="scatter_g_in_start")
    grad_x = _mm(dproj, win, tb=True, add=dr, add_scale=DEEPNORM_ALPHA, name="grad_x", tm=1024, tn=1024, tk=2176,
                 after=(sent_in[4],))

    parts_out = _scatter_wait(sent_out, grad_x, name="scatter_g_out_wait")
    parts_three = _scatter_wait(sent_three, parts_out, name="scatter_g_1024_wait")
    parts_in = _scatter_wait(sent_in, parts_three, name="scatter_g_in_wait")

    o_in = _adamw_reduce(parts_in, w_in[0], m_w_in[0], v_w_in[0], name="adamw_w_in", tr=128)
    three_w = jnp.concatenate([w_glu[0], w_branch_ssm[0], w_branch_attn[0]], axis=0)
    three_m = jnp.concatenate([m_w_glu[0], m_w_branch_ssm[0], m_w_branch_attn[0]], axis=0)
    three_v = jnp.concatenate([v_w_glu[0], v_w_branch_ssm[0], v_w_branch_attn[0]], axis=0)
    o_three = _adamw_reduce(parts_three, three_w, three_m, three_v, name="adamw_w_1024", tr=512)
    o_out = _adamw_reduce(parts_out, w_out[0], m_w_out[0], v_w_out[0], name="adamw_w_out", tr=128)

    small_w = [ssm_lambda_re, ssm_lambda_im, ssm_b_re, ssm_b_im, ssm_c_re, ssm_c_im, ssm_d, ssm_log_step,
               attn_sinks, rel_bias_table, ln_gain, ln_bias]
    small_m = [m_ssm_lambda_re, m_ssm_lambda_im, m_ssm_b_re, m_ssm_b_im, m_ssm_c_re, m_ssm_c_im, m_ssm_d,
               m_ssm_log_step, m_attn_sinks, m_rel_bias_table, m_ln_gain, m_ln_bias]
    small_v = [v_ssm_lambda_re, v_ssm_lambda_im, v_ssm_b_re, v_ssm_b_im, v_ssm_c_re, v_ssm_c_im, v_ssm_d,
               v_ssm_log_step, v_attn_sinks, v_rel_bias_table, v_ln_gain, v_ln_bias]
    small_g = [g_lre, g_lim, g_bre, g_bim, g_cre, g_cim, g_ssm_d, g_lstep, g_sinks, g_table, g_ln_gain, g_ln_bias]
    parts_small = _all_gather(_pack(small_g), name="gather_g_small")
    sg = _unpack(_sum_parts(parts_small, name="sum_g_small"), small_w)
    updates = [_adamw_native(w, g, m, v, name="adamw_" + n)
               for n, w, g, m, v in zip(SMALL, small_w, sg, small_m, small_v)]
    sd, sm, sv = zip(*updates)

    loss = lax.psum(loss_part[0, 0], MESH_AXES)

    def big(o, idx):
        g_in_, g_three_, g_out_ = o_in[idx], o_three[idx], o_out[idx]
        return {"w_in": g_in_[None], "w_glu": g_three_[None, :D_SSM], "w_branch_ssm": g_three_[None, D_SSM:2 * D_SSM],
                "w_branch_attn": g_three_[None, 2 * D_SSM:], "w_out": g_out_[None]}

    order = ["w_in", "ssm_lambda_re", "ssm_lambda_im", "ssm_b_re", "ssm_b_im", "ssm_c_re", "ssm_c_im", "ssm_d",
             "ssm_log_step", "w_glu", "attn_sinks", "rel_bias_table", "w_branch_ssm", "w_branch_attn", "w_out",
             "ln_gain", "ln_bias"]
    outs = [loss, grad_x[None]]
    for idx, small in enumerate([sg, sd, sm, sv]):
        table = big(None, idx)
        table.update(dict(zip(SMALL, small)))
        outs += [table[n] for n in order]
    return tuple(outs)
```

```python
import functools
import math

import jax
import jax.numpy as jnp
from jax import lax
from jax.experimental import pallas as pl
from jax.experimental.pallas import tpu as pltpu

F32 = jnp.float32
MXU_DTYPE = jnp.bfloat16
WIRE_DTYPE = jnp.bfloat16

D_MODEL = 2048
D_SSM = 1024
SSM_GROUP = 16
N_GROUPS = 64
SSM_STATE = 64
N_Q_HEADS = 16
N_KV_HEADS = 4
Q_PER_KV = 4
HEAD_DIM = 64
D_ATTN = 1024
D_KV = 256
WINDOW = 128
BLOCK = 128
N_BUCKETS = 32
MAX_DISTANCE = 128
D_IN = 8704
DEEPNORM_ALPHA = 2.0 ** 0.25
LN_EPS = 1e-5
NEG_INF = -1e30
ATTN_SCALE = HEAD_DIM ** -0.5

ADAM_LR = 0.001
ADAM_B1 = 0.9
ADAM_B2 = 0.999
ADAM_EPS = 1e-08
ADAM_WD = 0.01
ADAM_STEP = 10

N_DEV = 8
SSM_CHUNK = 16
GROUP_BLOCK = 8
N_GB = N_GROUPS // GROUP_BLOCK
GB_LANES = GROUP_BLOCK * SSM_GROUP
GB_STATE = GROUP_BLOCK * SSM_STATE
COL = 256
OFF_U, OFF_ZS, OFF_Q, OFF_K, OFF_V, OFF_ZA, OFF_GS, OFF_GA = 0, 4, 8, 12, 13, 14, 18, 26

VMEM_CAP = 56 * 1024 * 1024
MESH_AXES = ("x", "y", "c")


def _vmem_limit(block_bytes):
    return int(min(max(3 * block_bytes, 16 * 1024 * 1024), VMEM_CAP))


def _nbytes(shape, dtype):
    return math.prod(shape) * jnp.dtype(dtype).itemsize


def _tile(n, pref):
    if n <= pref:
        return n
    t = (pref // 128) * 128
    while t >= 128:
        if n % t == 0:
            return t
        t -= 128
    return n


def _mm(a, b, *, name, ta=False, tb=False, out_dtype=F32, tm=1024, tn=512, tk=512, add=None, add_scale=1.0,
        after=()):
    squeeze = a.ndim == 2
    if squeeze:
        a, b = a[None], b[None]
        if add is not None:
            add = add[None]
    nb = a.shape[0]
    m, k = (a.shape[2], a.shape[1]) if ta else (a.shape[1], a.shape[2])
    n = b.shape[1] if tb else b.shape[2]
    tm, tn, tk = _tile(m, tm), _tile(n, tn), _tile(k, tk)
    nk = k // tk
    dn = (((0 if ta else 1,), (1 if tb else 0,)), ((), ()))

    a_spec = (pl.BlockSpec((None, tk, tm), lambda g, i, j, kk: (g, kk, i)) if ta
              else pl.BlockSpec((None, tm, tk), lambda g, i, j, kk: (g, i, kk)))
    b_spec = (pl.BlockSpec((None, tn, tk), lambda g, i, j, kk: (g, j, kk)) if tb
              else pl.BlockSpec((None, tk, tn), lambda g, i, j, kk: (g, kk, j)))
    o_spec = pl.BlockSpec((None, tm, tn), lambda g, i, j, kk: (g, i, j))
    in_specs = [a_spec, b_spec]
    operands = [a, b]
    if add is not None:
        in_specs.append(o_spec)
        operands.append(add)
    for tok in after:
        in_specs.append(pl.BlockSpec(memory_space=pl.ANY))
        operands.append(tok)
    n_in = len(operands)

    def body(*refs):
        a_ref, b_ref = refs[0], refs[1]
        add_ref = refs[2] if add is not None else None
        o_ref = refs[n_in]
        acc_ref = refs[-1]
        kk = pl.program_id(3)
        part = lax.dot_general(a_ref[...].astype(MXU_DTYPE), b_ref[...].astype(MXU_DTYPE), dn,
                               preferred_element_type=F32)

        def finish(r):
            if add_ref is not None:
                r = r + add_scale * add_ref[...]
            o_ref[...] = r.astype(out_dtype)

        if nk == 1:
            finish(part)
            return

        @pl.when(kk == 0)
        def _():
            acc_ref[...] = part

        @pl.when(jnp.logical_and(kk > 0, kk < nk - 1))
        def _():
            acc_ref[...] += part

        @pl.when(kk == nk - 1)
        def _():
            finish(acc_ref[...] + part)

    blocks = (_nbytes((tm, tk), a.dtype) + _nbytes((tk, tn), b.dtype) + _nbytes((tm, tn), out_dtype)
              + (_nbytes((tm, tn), F32) if add is not None else 0))
    out = pl.pallas_call(
        body,
        name=name,
        grid=(nb, m // tm, n // tn, nk),
        in_specs=in_specs,
        out_specs=o_spec,
        out_shape=jax.ShapeDtypeStruct((nb, m, n), out_dtype),
        scratch_shapes=[pltpu.VMEM((tm, tn), F32)],
        compiler_params=pltpu.CompilerParams(
            dimension_semantics=("parallel", "parallel", "parallel", "arbitrary"),
            vmem_limit_bytes=_vmem_limit(2 * blocks + 2 * _nbytes((tm, tn), F32))),
    )(*operands)
    return out[0] if squeeze else out


def _ew(fn, ins, outs, *, rows, cw, ncb, tr, name):
    tr = min(tr, rows)
    n_in = len(ins)

    def row_map(off):
        return lambda j, i: (i, off + j)

    def vec_map(off):
        return lambda j, i: (0, off + j)

    in_specs = []
    for arr, kind, off in ins:
        if kind == "row":
            in_specs.append(pl.BlockSpec((tr, cw), row_map(off)))
        else:
            in_specs.append(pl.BlockSpec((1, cw), vec_map(off)))
    out_specs, out_shapes = [], []
    for bw, dt, kind in outs:
        if kind == "row":
            out_specs.append(pl.BlockSpec((tr, bw), row_map(0)))
            out_shapes.append(jax.ShapeDtypeStruct((rows, ncb * bw), dt))
        else:
            out_specs.append(pl.BlockSpec((1, bw), vec_map(0)))
            out_shapes.append(jax.ShapeDtypeStruct((1, ncb * bw), F32))

    def body(*refs):
        i = pl.program_id(1)
        vals = fn(*[r[...] for r in refs[:n_in]])
        for r, (bw, dt, kind), v in zip(refs[n_in:], outs, vals):
            if kind == "row":
                r[...] = v.astype(dt)
            else:
                @pl.when(i == 0)
                def _(r=r):
                    r[...] = jnp.zeros_like(r)

                r[...] += v

    blocks = sum(_nbytes((tr, cw), a.dtype) for a, kind, _ in ins if kind == "row")
    blocks += sum(_nbytes((tr, bw), dt) for bw, dt, kind in outs if kind == "row")
    res = pl.pallas_call(
        body,
        name=name,
        grid=(ncb, rows // tr),
        in_specs=in_specs,
        out_specs=out_specs,
        out_shape=out_shapes,
        compiler_params=pltpu.CompilerParams(
            dimension_semantics=("parallel", "arbitrary"),
            vmem_limit_bytes=_vmem_limit(4 * blocks)),
    )(*[a for a, _, _ in ins])
    return res


def _sigmoid(x):
    return 1.0 / (1.0 + jnp.exp(-x))


INV_SQRT2 = 0.7071067811865476
INV_SQRT_2PI = 0.3989422804014327


def _gelu(x):
    return 0.5 * x * (1.0 + lax.erf(x * INV_SQRT2))


def _gelu_grad(x):
    return 0.5 * (1.0 + lax.erf(x * INV_SQRT2)) + x * INV_SQRT_2PI * jnp.exp(-0.5 * x * x)


def _silu(x):
    return x * _sigmoid(x)


def _silu_grad(x):
    s = _sigmoid(x)
    return s * (1.0 + x * (1.0 - s))


@jax.custom_vjp
def _taps_product(mr, mi, bbr, bbi):
    bbr_t, bbi_t = jnp.transpose(bbr, (0, 2, 1)), jnp.transpose(bbi, (0, 2, 1))
    return jnp.sum(mr[..., None, :] * bbr_t[None, :, None] - mi[..., None, :] * bbi_t[None, :, None], axis=-1)


def _taps_product_fwd(mr, mi, bbr, bbi):
    return _taps_product(mr, mi, bbr, bbi), (mr, mi, bbr, bbi)


def _taps_product_bwd(res, g):
    mr, mi, bbr, bbi = res
    hi = lax.Precision.HIGHEST
    return (jnp.einsum("tghk,gpk->tghp", g, bbr, precision=hi), -jnp.einsum("tghk,gpk->tghp", g, bbi, precision=hi),
            jnp.einsum("tghk,tghp->gpk", g, mr, precision=hi), -jnp.einsum("tghk,tghp->gpk", g, mi, precision=hi))


_taps_product.defvjp(_taps_product_fwd, _taps_product_bwd)


def _ssm_matrices(lam_re, lam_im, b_re, b_im, c_re, c_im, log_step):
    L = SSM_CHUNK
    step = jnp.exp(log_step)[:, None]
    ea, eb = lam_re * step, lam_im * step
    mag = jnp.exp(ea)
    lbr, lbi = mag * jnp.cos(eb), mag * jnp.sin(eb)
    den = lam_re * lam_re + lam_im * lam_im
    nr, ni = lbr - 1.0, lbi
    cr = (nr * lam_re + ni * lam_im) / den
    ci = (ni * lam_re - nr * lam_im) / den
    bbr = cr[..., None] * b_re - ci[..., None] * b_im
    bbi = cr[..., None] * b_im + ci[..., None] * b_re
    taus = jnp.arange(L + 1, dtype=F32)[:, None, None]
    pmag = jnp.exp(taus * ea[None])
    pwr, pwi = pmag * jnp.cos(taus * eb[None]), pmag * jnp.sin(taus * eb[None])
    mr = c_re[None] * pwr[:L, :, None, :] - c_im[None] * pwi[:L, :, None, :]
    mi = c_re[None] * pwi[:L, :, None, :] + c_im[None] * pwr[:L, :, None, :]
    kk = _taps_product(mr, mi, bbr, bbi)
    taps = jnp.transpose(kk.reshape(L, N_GB, GROUP_BLOCK, SSM_GROUP, SSM_GROUP), (1, 0, 2, 4, 3))
    taps = taps.reshape(N_GB, L, GB_LANES, SSM_GROUP)
    rev_r, rev_i = pwr[L - 1 - jnp.arange(L)], pwi[L - 1 - jnp.arange(L)]
    wer = rev_r[..., None] * bbr[None] - rev_i[..., None] * bbi[None]
    wei = rev_r[..., None] * bbi[None] + rev_i[..., None] * bbr[None]

    def rows_in(w):
        w = jnp.transpose(w.reshape(L, N_GB, GROUP_BLOCK, SSM_STATE, SSM_GROUP), (1, 0, 2, 4, 3))
        return w.reshape(N_GB, L * GB_LANES, SSM_STATE)

    wend = jnp.concatenate([rows_in(wer), rows_in(wei)], axis=2)
    m1r = c_re[None] * pwr[1:, :, None, :] - c_im[None] * pwi[1:, :, None, :]
    m1i = c_re[None] * pwi[1:, :, None, :] + c_im[None] * pwr[1:, :, None, :]

    def rows_out(m):
        m = jnp.transpose(m.reshape(L, N_GB, GROUP_BLOCK, SSM_GROUP, SSM_STATE), (1, 2, 4, 0, 3))
        return m.reshape(N_GB, GB_STATE, L * SSM_GROUP)

    wout = jnp.concatenate([rows_out(m1r), rows_out(-m1i)], axis=1)
    acat = jnp.concatenate([pwr[L].reshape(N_GB, 1, GB_STATE), pwi[L].reshape(N_GB, 1, GB_STATE)], axis=2)
    return taps, wend, wout, acat


def _lane_group(shape, axis, shift):
    return (lax.broadcasted_iota(jnp.int32, shape, axis) >> shift) & (GROUP_BLOCK - 1)


def _ssm_expand(taps, wend, wout):
    L = SSM_CHUNK
    taps = jnp.pad(taps, ((0, 0), (0, 0), (0, 0), (0, GB_LANES - SSM_GROUP)))

    def body(t_ref, we_ref, wo_ref, d_ref, web_ref, wob_ref):
        def rc(shape):
            return lax.broadcasted_iota(jnp.int32, shape, 0), lax.broadcasted_iota(jnp.int32, shape, 1)

        r, c = rc((GB_LANES, GB_LANES))
        spread = ((r < SSM_GROUP) & (r == (c & (SSM_GROUP - 1)))).astype(MXU_DTYPE)
        same = _lane_group((GB_LANES, GB_LANES), 0, 4) == _lane_group((GB_LANES, GB_LANES), 1, 4)
        for tau in range(L):
            full = jnp.dot(t_ref[tau].astype(MXU_DTYPE), spread, preferred_element_type=F32)
            d_ref[tau] = jnp.where(same, full, 0.0).astype(d_ref.dtype)
        r, c = rc((2 * SSM_STATE, STATE_W))
        part = (r >> 6) == (c >> 9)
        spread = (part & ((r & (SSM_STATE - 1)) == (c & (SSM_STATE - 1)))).astype(MXU_DTYPE)
        keep = _lane_group((GB_LANES, STATE_W), 0, 4) == _lane_group((GB_LANES, STATE_W), 1, 6)
        for j in range(L):
            rows = slice(j * GB_LANES, (j + 1) * GB_LANES)
            full = jnp.dot(we_ref[rows, :].astype(MXU_DTYPE), spread, preferred_element_type=F32)
            web_ref[rows, :] = jnp.where(keep, full, 0.0).astype(web_ref.dtype)
        r, c = rc((GB_LANES, GB_LANES))
        own = _lane_group((STATE_W, GB_LANES), 0, 6) == _lane_group((STATE_W, GB_LANES), 1, 4)
        for tt in range(L):
            half, k = tt // GROUP_BLOCK, tt % GROUP_BLOCK
            spread = (((r >> 4) == k) & ((r & (SSM_GROUP - 1)) == (c & (SSM_GROUP - 1)))).astype(MXU_DTYPE)
            src = wo_ref[:, half * GB_LANES:(half + 1) * GB_LANES].astype(MXU_DTYPE)
            full = jnp.dot(src, spread, preferred_element_type=F32)
            wob_ref[:, tt * GB_LANES:(tt + 1) * GB_LANES] = jnp.where(own, full, 0.0).astype(wob_ref.dtype)

    def spec(shape):
        return pl.BlockSpec((None,) + shape[1:], lambda b: (b,) + (0,) * (len(shape) - 1))

    out_shapes = [(N_GB, L, GB_LANES, GB_LANES), (N_GB, L * GB_LANES, STATE_W), (N_GB, STATE_W, L * GB_LANES)]
    return tuple(pl.pallas_call(
        body, name="ssm_expand", grid=(N_GB,),
        in_specs=[spec(taps.shape), spec(wend.shape), spec(wout.shape)],
        out_specs=[spec(s) for s in out_shapes],
        out_shape=[jax.ShapeDtypeStruct(s, MXU_DTYPE) for s in out_shapes],
        compiler_params=pltpu.CompilerParams(dimension_semantics=("parallel",), vmem_limit_bytes=VMEM_CAP),
    )(taps, wend, wout))


def _dot(a, b):
    return jnp.dot(a.astype(MXU_DTYPE), b.astype(MXU_DTYPE), preferred_element_type=F32)


def _dot_nt(a, b):
    return lax.dot_general(a.astype(MXU_DTYPE), b.astype(MXU_DTYPE), (((1,), (1,)), ((), ())),
                           preferred_element_type=F32)


def _dot_tn(a, b):
    return lax.dot_general(a.astype(MXU_DTYPE), b.astype(MXU_DTYPE), (((0,), (0,)), ((), ())),
                           preferred_element_type=F32)


STATE_W = 2 * GB_STATE


def _chunk_scan(e, acat):
    nc = e.shape[0]
    spec = pl.BlockSpec((nc, STATE_W), lambda b: (0, b))
    aspec = pl.BlockSpec((None, 1, STATE_W), lambda b: (b, 0, 0))

    def body(e_ref, a_ref, s_ref):
        a_r, a_i = a_ref[:, :GB_STATE], a_ref[:, GB_STATE:]

        def step(c, carry):
            s_r, s_i = carry
            s_ref[pl.ds(c, 1), :GB_STATE] = s_r
            s_ref[pl.ds(c, 1), GB_STATE:] = s_i
            e_r = e_ref[pl.ds(c, 1), :GB_STATE]
            e_i = e_ref[pl.ds(c, 1), GB_STATE:]
            return (a_r * s_r - a_i * s_i + e_r, a_r * s_i + a_i * s_r + e_i)

        zero = jnp.zeros((1, GB_STATE), F32)
        lax.fori_loop(0, nc, step, (zero, zero))

    return pl.pallas_call(
        body, name="ssm_chunk_scan", grid=(N_GB,),
        in_specs=[spec, aspec], out_specs=spec,
        out_shape=jax.ShapeDtypeStruct(e.shape, F32),
        compiler_params=pltpu.CompilerParams(dimension_semantics=("parallel",)),
    )(e, acat)


def _chunk_scan_bwd(ds, s, acat):
    nc = ds.shape[0]
    spec = pl.BlockSpec((nc, STATE_W), lambda b: (0, b))
    aspec = pl.BlockSpec((None, 1, STATE_W), lambda b: (b, 0, 0))

    def body(ds_ref, s_ref, a_ref, ge_ref, da_ref):
        a_r, a_i = a_ref[:, :GB_STATE], a_ref[:, GB_STATE:]

        def step(t, carry):
            g_r, g_i, d_r, d_i = carry
            c = nc - 1 - t
            ge_ref[pl.ds(c, 1), :GB_STATE] = g_r
            ge_ref[pl.ds(c, 1), GB_STATE:] = g_i
            s_r = s_ref[pl.ds(c, 1), :GB_STATE]
            s_i = s_ref[pl.ds(c, 1), GB_STATE:]
            d_r = d_r + g_r * s_r + g_i * s_i
            d_i = d_i + g_i * s_r - g_r * s_i
            n_r = ds_ref[pl.ds(c, 1), :GB_STATE] + a_r * g_r + a_i * g_i
            n_i = ds_ref[pl.ds(c, 1), GB_STATE:] + a_r * g_i - a_i * g_r
            return (n_r, n_i, d_r, d_i)

        zero = jnp.zeros((1, GB_STATE), F32)
        _, _, d_r, d_i = lax.fori_loop(0, nc, step, (zero, zero, zero, zero))
        da_ref[:, :GB_STATE] = d_r
        da_ref[:, GB_STATE:] = d_i

    return pl.pallas_call(
        body, name="ssm_chunk_scan_bwd", grid=(N_GB,),
        in_specs=[spec, spec, aspec], out_specs=[spec, aspec],
        out_shape=[jax.ShapeDtypeStruct(ds.shape, F32), jax.ShapeDtypeStruct(acat.shape, F32)],
        compiler_params=pltpu.CompilerParams(dimension_semantics=("parallel",)),
    )(ds, s, acat)


def _step_rows(ref, j, nc):
    return ref[pl.ds(j, nc, stride=SSM_CHUNK), :]


def _fold_lanes(z, widths):
    for w in widths:
        z = z + pltpu.roll(z, w, 1)
    return z


def _ssm_forward(proj, mats):
    dblk, wend, wout, acat = mats
    t = proj.shape[0]
    nc = t // SSM_CHUNK
    L = SSM_CHUNK
    lanes = pl.BlockSpec((t, GB_LANES), lambda b: (0, b))
    state = pl.BlockSpec((nc, STATE_W), lambda b: (0, b))

    def body_end(u_ref, w_ref, e_ref):
        x = jnp.concatenate([_step_rows(u_ref, j, nc).astype(MXU_DTYPE) for j in range(L)], axis=1)
        e_ref[...] = jnp.dot(x, w_ref[...], preferred_element_type=F32)

    e = pl.pallas_call(
        body_end, name="ssm_chunk_end", grid=(N_GB,),
        in_specs=[lanes, pl.BlockSpec((None, L * GB_LANES, STATE_W), lambda b: (b, 0, 0))],
        out_specs=state, out_shape=jax.ShapeDtypeStruct((nc, N_GB * STATE_W), F32),
        compiler_params=pltpu.CompilerParams(dimension_semantics=("parallel",), vmem_limit_bytes=VMEM_CAP),
    )(proj, wend)
    s = _chunk_scan(e, acat)

    def body_out(u_ref, d_ref, s_ref, w_ref, y_ref):
        xs = [_step_rows(u_ref, j, nc).astype(MXU_DTYPE) for j in range(L)]
        sb = s_ref[...].astype(MXU_DTYPE)
        for tt in range(L):
            xcat = jnp.concatenate(xs[:tt + 1], axis=1)
            taps = jnp.concatenate([d_ref[tt - j] for j in range(tt + 1)], axis=0).astype(MXU_DTYPE)
            y = (jnp.dot(xcat, taps, preferred_element_type=F32)
                 + jnp.dot(sb, w_ref[:, tt * GB_LANES:(tt + 1) * GB_LANES], preferred_element_type=F32))
            y_ref[pl.ds(tt, nc, stride=L), :] = y

    y = pl.pallas_call(
        body_out, name="ssm_chunk_out", grid=(N_GB,),
        in_specs=[lanes, pl.BlockSpec((None, L, GB_LANES, GB_LANES), lambda b: (b, 0, 0, 0)), state,
                  pl.BlockSpec((None, STATE_W, L * GB_LANES), lambda b: (b, 0, 0))],
        out_specs=lanes, out_shape=jax.ShapeDtypeStruct((t, D_SSM), F32),
        compiler_params=pltpu.CompilerParams(dimension_semantics=("parallel",), vmem_limit_bytes=VMEM_CAP),
    )(proj, dblk, s, wout)
    return y, s


def _ssm_backward(dy, proj, s, mats):
    dblk, wend, wout, acat = mats
    t = proj.shape[0]
    nc = t // SSM_CHUNK
    L = SSM_CHUNK
    lanes = pl.BlockSpec((t, GB_LANES), lambda b: (0, b))
    state = pl.BlockSpec((nc, STATE_W), lambda b: (0, b))
    taps_spec = pl.BlockSpec((None, L, GB_LANES, GB_LANES), lambda b: (b, 0, 0, 0))

    def body_state(dy_ref, w_ref, ds_ref):
        dyc = jnp.concatenate([_step_rows(dy_ref, tt, nc).astype(MXU_DTYPE) for tt in range(L)], axis=1)
        ds_ref[...] = _dot_nt(dyc, w_ref[...])

    ds = pl.pallas_call(
        body_state, name="ssm_bwd_state", grid=(N_GB,),
        in_specs=[lanes, pl.BlockSpec((None, STATE_W, L * GB_LANES), lambda b: (b, 0, 0))],
        out_specs=state, out_shape=jax.ShapeDtypeStruct((nc, N_GB * STATE_W), F32),
        compiler_params=pltpu.CompilerParams(dimension_semantics=("parallel",), vmem_limit_bytes=VMEM_CAP),
    )(dy, wout)
    ge, dacat = _chunk_scan_bwd(ds, s, acat)

    def body_in(u_ref, dy_ref, d_ref, ge_ref, w_ref, du_ref, dd_ref):
        xs = [_step_rows(u_ref, j, nc).astype(MXU_DTYPE) for j in range(L)]
        dys = [_step_rows(dy_ref, tt, nc).astype(MXU_DTYPE) for tt in range(L)]
        ge = ge_ref[...].astype(MXU_DTYPE)
        for i in range(L):
            dyc = jnp.concatenate(dys[i:], axis=1)
            taps = jnp.concatenate([d_ref[tt - i] for tt in range(i, L)], axis=1).astype(MXU_DTYPE)
            du_ref[pl.ds(i, nc, stride=L), :] = (
                _dot_nt(dyc, taps) + _dot_nt(ge, w_ref[i * GB_LANES:(i + 1) * GB_LANES, :]))
        for j in range(L):
            m = _dot_tn(xs[j], jnp.concatenate(dys[j:], axis=1))
            for tau in range(L - j):
                part = m[:, tau * GB_LANES:(tau + 1) * GB_LANES]
                if j == 0:
                    dd_ref[tau] = part
                else:
                    dd_ref[tau] += part
        same = _lane_group((GB_LANES, GB_LANES), 0, 4) == _lane_group((GB_LANES, GB_LANES), 1, 4)
        for tau in range(L):
            dd_ref[tau] = _fold_lanes(jnp.where(same, dd_ref[tau], 0.0), (64, 32, 16))

    du, ddblk = pl.pallas_call(
        body_in, name="ssm_bwd_in", grid=(N_GB,),
        in_specs=[lanes, lanes, taps_spec, state,
                  pl.BlockSpec((None, L * GB_LANES, STATE_W), lambda b: (b, 0, 0))],
        out_specs=[lanes, taps_spec],
        out_shape=[jax.ShapeDtypeStruct((t, D_SSM), F32), jax.ShapeDtypeStruct(dblk.shape, F32)],
        compiler_params=pltpu.CompilerParams(dimension_semantics=("parallel",), vmem_limit_bytes=VMEM_CAP),
    )(proj, dy, dblk, ge, wend)

    def body_w(u_ref, dy_ref, ge_ref, s_ref, dwe_ref, dwo_ref):
        ge = ge_ref[...].astype(MXU_DTYPE)
        sb = s_ref[...].astype(MXU_DTYPE)
        keep = _lane_group((GB_LANES, STATE_W), 0, 4) == _lane_group((GB_LANES, STATE_W), 1, 6)
        low = lax.broadcasted_iota(jnp.int32, (GB_LANES, 2 * SSM_STATE), 1) < SSM_STATE

        def fold_state(z):
            z = z[:, :GB_STATE // 2] + z[:, GB_STATE // 2:]
            z = z[:, :GB_STATE // 4] + z[:, GB_STATE // 4:]
            return _fold_lanes(z, (SSM_STATE,))

        for j in range(L):
            z = jnp.where(keep, _dot_tn(_step_rows(u_ref, j, nc).astype(MXU_DTYPE), ge), 0.0)
            dwe_ref[j * GB_LANES:(j + 1) * GB_LANES, :] = jnp.where(
                low, fold_state(z[:, :GB_STATE]), fold_state(z[:, GB_STATE:]))
        own = _lane_group((STATE_W, GB_LANES), 0, 6) == _lane_group((STATE_W, GB_LANES), 1, 4)
        chunk = lax.broadcasted_iota(jnp.int32, (STATE_W, GB_LANES), 1) >> 4
        for half in range(L // GROUP_BLOCK):
            acc = jnp.zeros((STATE_W, GB_LANES), F32)
            for k in range(GROUP_BLOCK):
                tt = half * GROUP_BLOCK + k
                z = jnp.where(own, _dot_tn(sb, _step_rows(dy_ref, tt, nc).astype(MXU_DTYPE)), 0.0)
                acc = acc + jnp.where(chunk == k, _fold_lanes(z, (64, 32, 16)), 0.0)
            dwo_ref[:, half * GB_LANES:(half + 1) * GB_LANES] = acc

    dwend, dwout = pl.pallas_call(
        body_w, name="ssm_bwd_w", grid=(N_GB,),
        in_specs=[lanes, lanes, state, state],
        out_specs=[pl.BlockSpec((None, L * GB_LANES, 2 * SSM_STATE), lambda b: (b, 0, 0)),
                   pl.BlockSpec((None, STATE_W, L * SSM_GROUP), lambda b: (b, 0, 0))],
        out_shape=[jax.ShapeDtypeStruct((N_GB, L * GB_LANES, 2 * SSM_STATE), F32),
                   jax.ShapeDtypeStruct((N_GB, STATE_W, L * SSM_GROUP), F32)],
        compiler_params=pltpu.CompilerParams(dimension_semantics=("parallel",), vmem_limit_bytes=VMEM_CAP),
    )(proj, dy, ge, s)
    return du, (ddblk[:, :, :, :SSM_GROUP], dwend, dwout, dacat)


def _t5_bucket(dist):
    max_exact = N_BUCKETS // 2
    is_small = dist < max_exact
    d = jnp.maximum(dist, 1).astype(F32)
    large = max_exact + (jnp.log(d / max_exact) / math.log(MAX_DISTANCE / max_exact)
                         * (N_BUCKETS - max_exact)).astype(jnp.int32)
    large = jnp.minimum(large, N_BUCKETS - 1)
    return jnp.where(is_small, dist, large)


def _band_bias(rel_bias_table):
    i = jnp.arange(BLOCK)[:, None]
    j = jnp.arange(BLOCK)[None, :]
    bucket = _t5_bucket(jnp.where(j > i, BLOCK + i - j, i - j))
    onehot = (bucket[:, :, None] == jnp.arange(N_BUCKETS)[None, None, :]).astype(F32)
    return jnp.einsum("qsb,bh->hqs", onehot, rel_bias_table, precision=lax.Precision.HIGHEST)


KV_PAIR = 2
HEADS_PER_STEP = KV_PAIR * Q_PER_KV
Q_LANES = HEADS_PER_STEP * HEAD_DIM
SLAB = 2 * HEAD_DIM
Q_COL0 = OFF_Q * COL // Q_LANES
K_COL0 = OFF_K * COL // SLAB
V_COL0 = OFF_V * COL // SLAB


def _attn_specs():
    q_spec = pl.BlockSpec((BLOCK, Q_LANES), lambda m, n: (n, Q_COL0 + m))
    k_prev = pl.BlockSpec((BLOCK, SLAB), lambda m, n: (jnp.maximum(n - 1, 0), K_COL0 + m))
    k_cur = pl.BlockSpec((BLOCK, SLAB), lambda m, n: (n, K_COL0 + m))
    v_prev = pl.BlockSpec((BLOCK, SLAB), lambda m, n: (jnp.maximum(n - 1, 0), V_COL0 + m))
    v_cur = pl.BlockSpec((BLOCK, SLAB), lambda m, n: (n, V_COL0 + m))
    bias_spec = pl.BlockSpec((HEADS_PER_STEP, BLOCK, BLOCK), lambda m, n: (m, 0, 0))
    sink_spec = pl.BlockSpec(memory_space=pltpu.SMEM)
    wide = pl.BlockSpec((BLOCK, Q_LANES), lambda m, n: (n, m))
    pair = pl.BlockSpec((BLOCK, SLAB), lambda m, n: (n, m))
    return [sink_spec, q_spec, k_prev, k_cur, v_prev, v_cur, bias_spec], wide, pair


def _low_lanes(shape):
    return lax.broadcasted_iota(jnp.int32, shape, 1) < HEAD_DIM


def _pair_halves(ref):
    kb = ref[...]
    sw = pltpu.roll(kb, HEAD_DIM, 1)
    lo = _low_lanes(kb.shape)
    zero = jnp.zeros_like(kb)
    first = (jnp.where(lo, kb, zero).astype(MXU_DTYPE), jnp.where(lo, zero, sw).astype(MXU_DTYPE))
    second = (jnp.where(lo, sw, zero).astype(MXU_DTYPE), jnp.where(lo, zero, kb).astype(MXU_DTYPE))
    return first, second


def _fold_pair(acc):
    f = [x + pltpu.roll(x, HEAD_DIM, 1) for x in acc]
    return jnp.where(_low_lanes(f[0].shape), f[0], f[1])


def _from_prev(n):
    row = lax.broadcasted_iota(jnp.int32, (BLOCK, BLOCK), 0)
    col = lax.broadcasted_iota(jnp.int32, (BLOCK, BLOCK), 1)
    prev = col > row
    return prev, jnp.where(jnp.logical_and(n == 0, prev), NEG_INF, 0.0)


def _softmax_sink(s, sink):
    m = jnp.maximum(jnp.max(s, axis=1, keepdims=True), sink)
    e = jnp.exp(s - m)
    es = jnp.exp(sink - m)
    inv = 1.0 / (jnp.sum(e, axis=1, keepdims=True) + es)
    return e * inv, es * inv


def _stack_pair(prev_halves, own_halves, a):
    return jnp.concatenate([prev_halves[a][0], prev_halves[a][1], own_halves[a][0], own_halves[a][1]], axis=0)


def _split_heads(x4, prev):
    return [jnp.where(prev, x4[:, e * BLOCK:(e + 1) * BLOCK], x4[:, (2 + e) * BLOCK:(3 + e) * BLOCK]) for e in range(2)]


def _spread_heads(x, prev):
    return jnp.concatenate([jnp.where(prev, x[0], 0.0), jnp.where(prev, x[1], 0.0),
                            jnp.where(prev, 0.0, x[0]), jnp.where(prev, 0.0, x[1])], axis=1)


def _attn_forward(proj, bias, sinks):
    t = proj.shape[0]
    in_specs, wide, _ = _attn_specs()

    def body(sink_ref, q_ref, kp_ref, kc_ref, vp_ref, vc_ref, bias_ref, o_ref):
        m, n = pl.program_id(0), pl.program_id(1)
        kp, kc, vp, vc = (_pair_halves(r) for r in (kp_ref, kc_ref, vp_ref, vc_ref))
        keys = [_stack_pair(kp, kc, a) for a in range(KV_PAIR)]
        vals = [_stack_pair(vp, vc, a) for a in range(KV_PAIR)]
        prev, edge = _from_prev(n)
        prev2 = jnp.concatenate([prev, prev], axis=0)
        for a in range(KV_PAIR):
            slabs = (2 * a, 2 * a + 1)
            q = jnp.concatenate([q_ref[:, s * SLAB:(s + 1) * SLAB] for s in slabs], axis=0).astype(MXU_DTYPE)
            logits = _split_heads(_dot_nt(q, keys[a]), prev2)
            probs = []
            for e in range(2):
                rows = [_softmax_sink(logits[e][r * BLOCK:(r + 1) * BLOCK] * ATTN_SCALE + bias_ref[2 * s + e] + edge,
                                      sink_ref[m * HEADS_PER_STEP + 2 * s + e])[0] for r, s in enumerate(slabs)]
                probs.append(jnp.concatenate(rows, axis=0))
            out = _dot(_spread_heads(probs, prev2), vals[a])
            for r, s in enumerate(slabs):
                o_ref[:, s * SLAB:(s + 1) * SLAB] = out[r * BLOCK:(r + 1) * BLOCK]

    return pl.pallas_call(
        body, name="attn_fwd", grid=(N_KV_HEADS // KV_PAIR, t // BLOCK),
        in_specs=in_specs, out_specs=wide,
        out_shape=jax.ShapeDtypeStruct((t, D_ATTN), F32),
        compiler_params=pltpu.CompilerParams(dimension_semantics=("parallel", "arbitrary")),
    )(sinks, proj, proj, proj, proj, proj, bias)


def _attn_backward(proj, bias, sinks, d_attn):
    t = proj.shape[0]
    in_specs, wide, pair = _attn_specs()
    bias_spec = in_specs[-1]
    sink_out = pl.BlockSpec((HEADS_PER_STEP, 8, 128), lambda m, n: (m, 0, 0))

    def body(sink_ref, q_ref, kp_ref, kc_ref, vp_ref, vc_ref, bias_ref, do_ref,
             dq_ref, dka_ref, dkb_ref, dva_ref, dvb_ref, dbias_ref, dsink_ref):
        m, n = pl.program_id(0), pl.program_id(1)

        @pl.when(n == 0)
        def _():
            dbias_ref[...] = jnp.zeros_like(dbias_ref)
            dsink_ref[...] = jnp.zeros_like(dsink_ref)

        kp, kc, vp, vc = (_pair_halves(r) for r in (kp_ref, kc_ref, vp_ref, vc_ref))
        keys = [_stack_pair(kp, kc, a) for a in range(KV_PAIR)]
        vals = [_stack_pair(vp, vc, a) for a in range(KV_PAIR)]
        prev, edge = _from_prev(n)
        lo = _low_lanes((BLOCK, SLAB))
        prev2 = jnp.concatenate([prev, prev], axis=0)
        dk = [[None] * KV_PAIR for _ in range(2)]
        dv = [[None] * KV_PAIR for _ in range(2)]
        for a in range(KV_PAIR):
            slabs = (2 * a, 2 * a + 1)
            q = jnp.concatenate([q_ref[:, s * SLAB:(s + 1) * SLAB] for s in slabs], axis=0).astype(MXU_DTYPE)
            do = jnp.concatenate([do_ref[:, s * SLAB:(s + 1) * SLAB] for s in slabs], axis=0).astype(MXU_DTYPE)
            logits = _split_heads(_dot_nt(q, keys[a]), prev2)
            dprobs = _split_heads(_dot_nt(do, vals[a]), prev2)
            probs, dlogits = [], []
            for e in range(2):
                p_rows, ds_rows = [], []
                for r, s in enumerate(slabs):
                    h = 2 * s + e
                    rows = slice(r * BLOCK, (r + 1) * BLOCK)
                    p, ps = _softmax_sink(logits[e][rows] * ATTN_SCALE + bias_ref[h] + edge,
                                          sink_ref[m * HEADS_PER_STEP + h])
                    delta = jnp.sum(p * dprobs[e][rows], axis=1, keepdims=True)
                    ds = p * (dprobs[e][rows] - delta)
                    dbias_ref[h] += ds
                    dsink_ref[h] += jnp.broadcast_to(jnp.sum(-ps * delta, axis=0, keepdims=True), (8, 128))
                    p_rows.append(p)
                    ds_rows.append(ds)
                probs.append(jnp.concatenate(p_rows, axis=0))
                dlogits.append(jnp.concatenate(ds_rows, axis=0))
            ds4 = _spread_heads(dlogits, prev2).astype(MXU_DTYPE)
            p4 = _spread_heads(probs, prev2).astype(MXU_DTYPE)
            dq = _dot(ds4, keys[a]) * ATTN_SCALE
            for r, s in enumerate(slabs):
                dq_ref[:, s * SLAB:(s + 1) * SLAB] = dq[r * BLOCK:(r + 1) * BLOCK].astype(dq_ref.dtype)
            rk = _dot_tn(ds4, q)
            rv = _dot_tn(p4, do)
            for which in range(2):
                top = 2 * which * BLOCK
                dk[which][a] = jnp.where(lo, rk[top:top + BLOCK], rk[top + BLOCK:top + 2 * BLOCK])
                dv[which][a] = jnp.where(lo, rv[top:top + BLOCK], rv[top + BLOCK:top + 2 * BLOCK])
        dkb_ref[...] = _fold_pair(dk[0]) * ATTN_SCALE
        dka_ref[...] = _fold_pair(dk[1]) * ATTN_SCALE
        dvb_ref[...] = _fold_pair(dv[0])
        dva_ref[...] = _fold_pair(dv[1])

    kv_shape = jax.ShapeDtypeStruct((t, D_KV), F32)
    return pl.pallas_call(
        body, name="attn_bwd", grid=(N_KV_HEADS // KV_PAIR, t // BLOCK),
        in_specs=in_specs + [wide],
        out_specs=[wide, pair, pair, pair, pair, bias_spec, sink_out],
        out_shape=[jax.ShapeDtypeStruct((t, D_ATTN), MXU_DTYPE), kv_shape, kv_shape, kv_shape, kv_shape,
                   jax.ShapeDtypeStruct(bias.shape, F32), jax.ShapeDtypeStruct((N_Q_HEADS, 8, 128), F32)],
        compiler_params=pltpu.CompilerParams(dimension_semantics=("parallel", "arbitrary")),
    )(sinks, proj, proj, proj, proj, proj, bias, d_attn)


def _shift_blocks(cur, prev):
    return cur + jnp.concatenate([prev[BLOCK:], jnp.zeros_like(prev[:BLOCK])], axis=0)


def _mesh_pos():
    return lax.axis_index("x"), lax.axis_index("y"), lax.axis_index("c")


def _all_gather(x, *, name):
    def body(x_ref, out_ref, send_sems, recv_sems, local_sem):
        x, y, c = _mesh_pos()
        me, sibling = (x, y, c), (x, y, 1 - c)
        chips = [(1 - x, y), (x, 1 - y), (1 - x, 1 - y)]

        def slot(px, py, pc):
            return out_ref.at[4 * px + 2 * py + pc]

        def copy(k, block, to, src=None):
            return pltpu.make_async_remote_copy(
                src_ref=slot(*block) if src is None else src, dst_ref=slot(*block),
                send_sem=send_sems.at[k], recv_sem=recv_sems.at[k],
                device_id=to, device_id_type=pl.DeviceIdType.MESH)

        mine = pltpu.make_async_copy(x_ref, slot(*me), local_sem)
        mine.start()
        first = [copy(0, me, sibling, src=x_ref)]
        first += [copy(1 + j, me, (*chip, c), src=x_ref) for j, chip in enumerate(chips)]
        for cp in first:
            cp.start()
        passed = [copy(4 + j, (*chip, c), sibling) for j, chip in enumerate(chips)]
        for j, chip in enumerate(chips):
            copy(1 + j, (*chip, c), me).wait_recv()
            passed[j].start()
        copy(0, sibling, me).wait_recv()
        for j, chip in enumerate(chips):
            copy(4 + j, (*chip, 1 - c), me).wait_recv()
        for cp in first + passed:
            cp.wait_send()
        mine.wait()

    return pl.pallas_call(
        body, name=name,
        in_specs=[pl.BlockSpec(memory_space=pl.ANY)],
        out_specs=pl.BlockSpec(memory_space=pl.ANY),
        out_shape=jax.ShapeDtypeStruct((N_DEV,) + x.shape, x.dtype),
        scratch_shapes=[pltpu.SemaphoreType.DMA((7,)), pltpu.SemaphoreType.DMA((7,)), pltpu.SemaphoreType.DMA],
    )(x)


_HBM = pl.BlockSpec(memory_space=pltpu.HBM)
_SEM = pl.BlockSpec(memory_space=pltpu.SEMAPHORE)
_DATAFLOW = pltpu.SideEffectType.DATAFLOW_SIDE_EFFECTING


def _peers():
    x, y, c = _mesh_pos()
    others = []
    for k in range(1, N_DEV):
        px, py, pc = x ^ (k >> 2), y ^ ((k >> 1) & 1), c ^ (k & 1)
        others.append(((px, py, pc), 4 * px + 2 * py + pc))
    return 4 * x + 2 * y + c, others


def _split_start(bufs, plan, n_copies, *, name):
    nb = len(bufs)

    def body(*refs):
        send_sems, recv_sems, token = refs[nb], refs[nb + 1], refs[-1]
        for k, (src, dst, pos, _) in enumerate(plan(*refs[:nb])):
            pltpu.make_async_remote_copy(src_ref=src, dst_ref=dst, send_sem=send_sems.at[k], recv_sem=recv_sems.at[k],
                                         device_id=pos, device_id_type=pl.DeviceIdType.MESH).start()
        token[...] = jnp.zeros_like(token)

    return pl.pallas_call(
        body, name=name,
        out_shape=(pltpu.SemaphoreType.DMA((n_copies,)), pltpu.SemaphoreType.DMA((n_copies,)),
                   *[pltpu.HBM(b.shape, b.dtype) for b in bufs], jax.ShapeDtypeStruct((8, 128), F32)),
        in_specs=(_HBM,) * nb, out_specs=(_SEM, _SEM) + (_HBM,) * nb + (pl.BlockSpec(memory_space=pltpu.VMEM),),
        input_output_aliases={i: 2 + i for i in range(nb)},
        compiler_params=pltpu.CompilerParams(has_side_effects=_DATAFLOW),
    )(*[pltpu.with_memory_space_constraint(b, pltpu.HBM) for b in bufs])


def _split_wait(started, plan, after, *, name):
    send_sems, recv_sems, *thru = started[:-1]
    nb = len(thru)

    def body(*refs):
        send_sems, recv_sems = refs[nb], refs[nb + 1]
        for k, (src, _, pos, arrive) in enumerate(plan(*refs[:nb])):
            copy = pltpu.make_async_remote_copy(
                src_ref=src, dst_ref=arrive, send_sem=send_sems.at[k], recv_sem=recv_sems.at[k],
                device_id=pos, device_id_type=pl.DeviceIdType.MESH)
            copy.wait_send()
            copy.wait_recv()

    return pl.pallas_call(
        body, name=name,
        out_shape=tuple(pltpu.HBM(b.shape, b.dtype) for b in thru),
        in_specs=(_HBM,) * nb + (_SEM, _SEM, pl.BlockSpec(memory_space=pl.ANY)), out_specs=(_HBM,) * nb,
        input_output_aliases={i: i for i in range(nb)},
        compiler_params=pltpu.CompilerParams(has_side_effects=_DATAFLOW),
    )(*thru, send_sems, recv_sems, after)


def _plan_scatter(x_ref, land_ref):
    me, others = _peers()
    return [(x_ref.at[idx], land_ref.at[me], pos, land_ref.at[idx]) for pos, idx in others]


def _plan_gather(x_ref, land_ref):
    me, others = _peers()
    return [(x_ref, land_ref.at[me], pos, land_ref.at[idx]) for pos, idx in others]


def _near_and_far():
    x, y, c = _mesh_pos()
    chips = [(1 - x, y), (x, 1 - y), (1 - x, 1 - y)]
    near = [(x, y, 1 - c)] + [(px, py, c) for px, py in chips]
    relay = [(4 * px + 2 * py + c, 4 * px + 2 * py + 1 - c) for px, py in chips]
    return 4 * x + 2 * y + c, near, (x, y, 1 - c), relay


def _plan_gather_near(x_ref, land_ref):
    me, near, _, _ = _near_and_far()
    return [(x_ref, land_ref.at[me], pos, land_ref.at[4 * pos[0] + 2 * pos[1] + pos[2]]) for pos in near]


def _plan_gather_relay(land_ref):
    _, _, sibling, relay = _near_and_far()
    return [(land_ref.at[mine], land_ref.at[mine], sibling, land_ref.at[theirs]) for mine, theirs in relay]


def _landing_zone(own):
    me, _ = _peers()
    return lax.dynamic_update_index_in_dim(lax.empty((N_DEV,) + own.shape, own.dtype), own, me, 0)


def _scatter_start(x, *, name):
    if x.ndim == 2:
        return _split_start((x, _landing_zone(x)), _plan_gather, N_DEV - 1, name=name)
    me, _ = _peers()
    own = lax.dynamic_index_in_dim(x, me, 0, keepdims=False)
    return _split_start((x, _landing_zone(own)), _plan_scatter, N_DEV - 1, name=name)


def _scatter_wait(started, after, *, name):
    plan = _plan_gather if started[2].ndim == 2 else _plan_scatter
    return _split_wait(started, plan, after, name=name)[1]


def _adamw_math(w, g, m, v):
    m = ADAM_B1 * m + (1.0 - ADAM_B1) * g
    v = ADAM_B2 * v + (1.0 - ADAM_B2) * (g * g)
    m_hat = m / (1.0 - ADAM_B1 ** ADAM_STEP)
    v_hat = v / (1.0 - ADAM_B2 ** ADAM_STEP)
    delta = -ADAM_LR * (m_hat / (jnp.sqrt(v_hat) + ADAM_EPS) + ADAM_WD * w)
    return delta, m, v


def _adamw_reduce(parts, w, m, v, *, name, tr):
    r, c = w.shape
    tr = min(tr, r)
    spec = pl.BlockSpec((tr, c), lambda i: (i, 0))

    def body(p_ref, w_ref, m_ref, v_ref, g_ref, d_ref, nm_ref, nv_ref):
        g = p_ref[0].astype(F32)
        for s in range(1, N_DEV):
            g = g + p_ref[s].astype(F32)
        delta, nm, nv = _adamw_math(w_ref[...], g, m_ref[...], v_ref[...])
        g_ref[...] = g
        d_ref[...] = delta
        nm_ref[...] = nm
        nv_ref[...] = nv

    return pl.pallas_call(
        body, name=name, grid=(r // tr,),
        in_specs=[pl.BlockSpec((N_DEV, tr, c), lambda i: (0, i, 0)), spec, spec, spec],
        out_specs=[spec] * 4,
        out_shape=[jax.ShapeDtypeStruct((r, c), F32)] * 4,
        compiler_params=pltpu.CompilerParams(
            dimension_semantics=("parallel",),
            vmem_limit_bytes=_vmem_limit(2 * 15 * _nbytes((tr, c), F32))),
    )(parts, w, m, v)


def _sum_parts(parts, *, name):
    def body(p_ref, o_ref):
        g = p_ref[0]
        for s in range(1, N_DEV):
            g = g + p_ref[s]
        o_ref[...] = g

    return pl.pallas_call(
        body, name=name, out_shape=jax.ShapeDtypeStruct(parts.shape[1:], F32),
        compiler_params=pltpu.CompilerParams(vmem_limit_bytes=_vmem_limit(_nbytes(parts.shape, F32))),
    )(parts)


def _adamw_native(w, g, m, v, *, name):
    def body(w_ref, g_ref, m_ref, v_ref, d_ref, nm_ref, nv_ref):
        d_ref[...], nm_ref[...], nv_ref[...] = _adamw_math(w_ref[...], g_ref[...], m_ref[...], v_ref[...])

    return pl.pallas_call(body, name=name, out_shape=[jax.ShapeDtypeStruct(w.shape, F32)] * 3)(w, g, m, v)


SMALL = ["ssm_lambda_re", "ssm_lambda_im", "ssm_b_re", "ssm_b_im", "ssm_c_re", "ssm_c_im", "ssm_d",
         "ssm_log_step", "attn_sinks", "rel_bias_table", "ln_gain", "ln_bias"]


def _pack(arrs):
    flat = jnp.concatenate([a.reshape(-1) for a in arrs])
    pad = (-flat.shape[0]) % 1024
    return jnp.pad(flat, (0, pad)).reshape(-1, 128)


def _unpack(packed, like):
    flat = packed.reshape(-1)
    out, pos = [], 0
    for a in like:
        out.append(flat[pos:pos + a.size].reshape(a.shape))
        pos += a.size
    return out


def kernel(x, w_in, ssm_lambda_re, ssm_lambda_im, ssm_b_re, ssm_b_im, ssm_c_re, ssm_c_im, ssm_d, ssm_log_step, w_glu, attn_sinks, rel_bias_table, w_branch_ssm, w_branch_attn, w_out, ln_gain, ln_bias, loss_target, m_w_in, m_ssm_lambda_re, m_ssm_lambda_im, m_ssm_b_re, m_ssm_b_im, m_ssm_c_re, m_ssm_c_im, m_ssm_d, m_ssm_log_step, m_w_glu, m_attn_sinks, m_rel_bias_table, m_w_branch_ssm, m_w_branch_attn, m_w_out, m_ln_gain, m_ln_bias, v_w_in, v_ssm_lambda_re, v_ssm_lambda_im, v_ssm_b_re, v_ssm_b_im, v_ssm_c_re, v_ssm_c_im, v_ssm_d, v_ssm_log_step, v_w_glu, v_attn_sinks, v_rel_bias_table, v_w_branch_ssm, v_w_branch_attn, v_w_out, v_ln_gain, v_ln_bias):
    t = x.shape[1]
    xs = x[0]
    target = loss_target[0]
    col_in = w_in.shape[2]
    col_br = w_glu.shape[2]
    row_out = w_out.shape[1]

    w_in_mx = w_in[0].astype(MXU_DTYPE)
    near = _split_start((w_in_mx, _landing_zone(w_in_mx)), _plan_gather_near, 4, name="gather_w_in_near_start")
    ssm_params = (ssm_lambda_re[0], ssm_lambda_im[0], ssm_b_re[0], ssm_b_im[0], ssm_c_re[0], ssm_c_im[0],
                  ssm_log_step[0] + near[-1][0, 0])
    mats, mats_vjp = jax.vjp(_ssm_matrices, *ssm_params)
    mats_mx = _ssm_expand(*mats[:3]) + (mats[3],)
    sinks = attn_sinks[0]
    d_skip = ssm_d
    _, landed = _split_wait(near, _plan_gather_near, mats_mx[1], name="gather_w_in_near_wait")
    relay = _split_start((landed,), _plan_gather_relay, 3, name="gather_w_in_relay_start")
    bias, bias_vjp = jax.vjp(_band_bias, rel_bias_table + relay[-1][0, 0])
    x_mx = xs.astype(MXU_DTYPE)
    (g_in,) = _split_wait(relay, _plan_gather_relay, bias, name="gather_w_in_relay_wait")
    win = g_in.transpose(1, 0, 2).reshape(D_MODEL, D_IN)
    hold = g_in[0, 0, 0] * 0
    three = jnp.concatenate([w_glu[0], w_branch_ssm[0], w_branch_attn[0]], axis=0).astype(MXU_DTYPE)
    sent_three = _scatter_start(three + hold, name="gather_w_1024_start")
    sent_wout = _scatter_start(w_out[0].astype(MXU_DTYPE) + hold, name="gather_w_out_start")

    proj = _mm(x_mx, win, name="in_proj", tm=2048, tn=512, tk=2048, after=(sent_three[4], sent_wout[4]))
    y_conv, states = _ssm_forward(proj, mats_mx)

    def f_gelu(yv, u, d):
        ys = yv + d * u
        return ys, _gelu(ys)

    y_ssm, glu_in = _ew(f_gelu, [(y_conv, "row", 0), (proj, "row", OFF_U), (d_skip, "vec", 0)],
                        [(COL, F32, "row"), (COL, MXU_DTYPE, "row")], rows=t, cw=COL, ncb=4, tr=1024, name="ssm_gelu")
    g_three = _scatter_wait(sent_three, glu_in, name="gather_w_1024_wait")
    three_full = g_three.transpose(1, 0, 2).reshape(3 * D_SSM, N_DEV * col_br)
    wglu, wbs, wba = three_full[:D_SSM], three_full[D_SSM:2 * D_SSM], three_full[2 * D_SSM:]
    glu = _mm(glu_in, wglu, name="glu_proj", tm=2048, tn=512, tk=1024)

    def f_hssm(ga, gb, z):
        return (ga * _sigmoid(gb) * _silu(z),)

    (h_ssm,) = _ew(f_hssm, [(glu, "row", 0), (glu, "row", 4), (proj, "row", OFF_ZS)],
                   [(COL, MXU_DTYPE, "row")], rows=t, cw=COL, ncb=4, tr=1024, name="ssm_gate")

    attn = _attn_forward(proj, bias, sinks)

    def f_hattn(a, z):
        return (a * _silu(z),)

    (h_attn,) = _ew(f_hattn, [(attn, "row", 0), (proj, "row", OFF_ZA)], [(COL, MXU_DTYPE, "row")],
                    rows=t, cw=COL, ncb=4, tr=1024, name="attn_gate")
    p_ssm = _mm(h_ssm, wbs, name="branch_ssm", tm=2048, tn=512, tk=1024)
    p_attn = _mm(h_attn, wba, name="branch_attn", tm=2048, tn=512, tk=1024)

    def f_merge(ps, pa, ls, la):
        return (_sigmoid(ls) * ps + _sigmoid(la) * pa,)

    (merged,) = _ew(f_merge, [(p_ssm, "row", 0), (p_attn, "row", 0), (proj, "row", OFF_GS), (proj, "row", OFF_GA)],
                    [(COL, MXU_DTYPE, "row")], rows=t, cw=COL, ncb=8, tr=1024, name="merge")
    wout = _scatter_wait(sent_wout, merged, name="gather_w_out_wait").reshape(D_MODEL, D_MODEL)
    out = _mm(merged, wout, name="out_proj", tm=2048, tn=512, tk=2048)

    def f_norm(xv, ov, tg, gain, lbias):
        r = DEEPNORM_ALPHA * xv + ov
        mu = jnp.mean(r, axis=1, keepdims=True)
        cen = r - mu
        var = jnp.mean(cen * cen, axis=1, keepdims=True)
        rstd = lax.rsqrt(var + LN_EPS)
        xhat = cen * rstd
        yv = xhat * gain + lbias
        diff = yv - tg
        row_loss = 0.5 * jnp.mean(diff * diff, axis=1, keepdims=True)
        loss = jnp.broadcast_to(jnp.sum(row_loss, axis=0, keepdims=True), (1, 128))
        dy = diff * (1.0 / D_MODEL)
        dgain = jnp.sum(dy * xhat, axis=0, keepdims=True)
        dbias = jnp.sum(dy, axis=0, keepdims=True)
        dxh = dy * gain
        dr = rstd * (dxh - jnp.mean(dxh, axis=1, keepdims=True) - xhat * jnp.mean(dxh * xhat, axis=1, keepdims=True))
        return dr, loss, dgain, dbias

    dr, loss_part, g_ln_gain, g_ln_bias = _ew(
        f_norm, [(xs, "row", 0), (out, "row", 0), (target, "row", 0), (ln_gain, "vec", 0), (ln_bias, "vec", 0)],
        [(D_MODEL, F32, "row"), (128, F32, "acc"), (D_MODEL, F32, "acc"), (D_MODEL, F32, "acc")],
        rows=t, cw=D_MODEL, ncb=1, tr=256, name="norm_loss")

    def scatter_cols(g, cols):
        return g.reshape(g.shape[0], N_DEV, cols).transpose(1, 0, 2)

    gw_out = _mm(merged, dr, ta=True, out_dtype=WIRE_DTYPE, name="grad_w_out", tm=2048, tn=512, tk=2048)
    sent_out = _scatter_start(gw_out.reshape(N_DEV, row_out, D_MODEL), name="scatter_g_out_start")
    d_merged = _mm(dr, wout, tb=True, name="d_merged", tm=2048, tn=512, tk=2048, after=(sent_out[4],))

    def b_merge(dm, ps, pa, ls, la):
        gs, ga = _sigmoid(ls), _sigmoid(la)
        return dm * gs, dm * ga, dm * ps * gs * (1.0 - gs), dm * pa * ga * (1.0 - ga)

    dp_ssm, dp_attn, dgl_s, dgl_a = _ew(
        b_merge, [(d_merged, "row", 0), (p_ssm, "row", 0), (p_attn, "row", 0), (proj, "row", OFF_GS), (proj, "row", OFF_GA)],
        [(COL, MXU_DTYPE, "row")] * 4, rows=t, cw=COL, ncb=8, tr=1024, name="merge_bwd")
    gw_bs = _mm(h_ssm, dp_ssm, ta=True, out_dtype=WIRE_DTYPE, name="grad_w_branch_ssm", tm=1024, tn=512, tk=2048)
    gw_ba = _mm(h_attn, dp_attn, ta=True, out_dtype=WIRE_DTYPE, name="grad_w_branch_attn", tm=1024, tn=512, tk=2048)
    dh_ssm = _mm(dp_ssm, wbs, tb=True, name="d_h_ssm", tm=2048, tn=512, tk=2048)
    dh_attn = _mm(dp_attn, wba, tb=True, name="d_h_attn", tm=2048, tn=512, tk=2048)

    def b_hssm(dh, ga, gb, z):
        sg = _sigmoid(gb)
        dgate = dh * _silu(z)
        return dgate * sg, dgate * ga * sg * (1.0 - sg), dh * ga * sg * _silu_grad(z)

    dglu_a, dglu_b, dz_ssm = _ew(b_hssm, [(dh_ssm, "row", 0), (glu, "row", 0), (glu, "row", 4), (proj, "row", OFF_ZS)],
                                 [(COL, MXU_DTYPE, "row")] * 3, rows=t, cw=COL, ncb=4, tr=1024, name="ssm_gate_bwd")
    dglu = jnp.concatenate([dglu_a, dglu_b], axis=1)
    gw_glu = _mm(glu_in, dglu, ta=True, out_dtype=WIRE_DTYPE, name="grad_w_glu", tm=1024, tn=512, tk=2048)
    sent_three = _scatter_start(scatter_cols(jnp.concatenate([gw_glu, gw_bs, gw_ba], axis=0), col_br),
                                name="scatter_g_1024_start")
    dglu_in = _mm(dglu, wglu, tb=True, name="d_glu_in", tm=2048, tn=512, tk=2048, after=(sent_three[4],))

    def b_gelu(dgi, ys, u):
        dys = dgi * _gelu_grad(ys)
        return dys, jnp.sum(dys * u, axis=0, keepdims=True)

    dy_ssm, g_ssm_d = _ew(b_gelu, [(dglu_in, "row", 0), (y_ssm, "row", 0), (proj, "row", OFF_U)],
                          [(COL, F32, "row"), (COL, F32, "acc")],
                          rows=t, cw=COL, ncb=4, tr=1024, name="ssm_gelu_bwd")
    du_ssm, dmats = _ssm_backward(dy_ssm, proj, states, mats_mx)
    du = (du_ssm + d_skip * dy_ssm).astype(MXU_DTYPE)
    g_lre, g_lim, g_bre, g_bim, g_cre, g_cim, g_lstep = mats_vjp(dmats)

    def b_hattn(dh, a, z):
        return dh * _silu(z), dh * a * _silu_grad(z)

    d_attn, dz_attn = _ew(b_hattn, [(dh_attn, "row", 0), (attn, "row", 0), (proj, "row", OFF_ZA)],
                          [(COL, F32, "row"), (COL, MXU_DTYPE, "row")], rows=t, cw=COL, ncb=4, tr=1024, name="attn_gate_bwd")
    dq, dka, dkb, dva, dvb, dbias, dsink = _attn_backward(proj, bias, sinks, d_attn)
    dk = _shift_blocks(dka, dkb).astype(MXU_DTYPE)
    dv = _shift_blocks(dva, dvb).astype(MXU_DTYPE)
    (g_table,) = bias_vjp(dbias)
    g_sinks = dsink[:, 0, 0]

    dproj = jnp.concatenate([du, dz_ssm, dq, dk, dv, dz_attn, dgl_s, dgl_a], axis=1)
    gw_in = _mm(x_mx, dproj, ta=True, out_dtype=WIRE_DTYPE, name="grad_w_in", tm=2048, tn=512, tk=2048)
    sent_in = _scatter_start(scatter_cols(gw_in, col_in), name="scatter_g_in_start")
    grad_x = _mm(dproj, win, tb=True, add=dr, add_scale=DEEPNORM_ALPHA, name="grad_x", tm=1024, tn=1024, tk=2176,
                 after=(sent_in[4],))

    parts_out = _scatter_wait(sent_out, grad_x, name="scatter_g_out_wait")
    parts_three = _scatter_wait(sent_three, parts_out, name="scatter_g_1024_wait")
    parts_in = _scatter_wait(sent_in, parts_three, name="scatter_g_in_wait")

    o_in = _adamw_reduce(parts_in, w_in[0], m_w_in[0], v_w_in[0], name="adamw_w_in", tr=128)
    three_w = jnp.concatenate([w_glu[0], w_branch_ssm[0], w_branch_attn[0]], axis=0)
    three_m = jnp.concatenate([m_w_glu[0], m_w_branch_ssm[0], m_w_branch_attn[0]], axis=0)
    three_v = jnp.concatenate([v_w_glu[0], v_w_branch_ssm[0], v_w_branch_attn[0]], axis=0)
    o_three = _adamw_reduce(parts_three, three_w, three_m, three_v, name="adamw_w_1024", tr=512)
    o_out = _adamw_reduce(parts_out, w_out[0], m_w_out[0], v_w_out[0], name="adamw_w_out", tr=128)

    small_w = [ssm_lambda_re, ssm_lambda_im, ssm_b_re, ssm_b_im, ssm_c_re, ssm_c_im, ssm_d, ssm_log_step,
               attn_sinks, rel_bias_table, ln_gain, ln_bias]
    small_m = [m_ssm_lambda_re, m_ssm_lambda_im, m_ssm_b_re, m_ssm_b_im, m_ssm_c_re, m_ssm_c_im, m_ssm_d,
               m_ssm_log_step, m_attn_sinks, m_rel_bias_table, m_ln_gain, m_ln_bias]
    small_v = [v_ssm_lambda_re, v_ssm_lambda_im, v_ssm_b_re, v_ssm_b_im, v_ssm_c_re, v_ssm_c_im, v_ssm_d,
               v_ssm_log_step, v_attn_sinks, v_rel_bias_table, v_ln_gain, v_ln_bias]
    small_g = [g_lre, g_lim, g_bre, g_bim, g_cre, g_cim, g_ssm_d, g_lstep, g_sinks, g_table, g_ln_gain, g_ln_bias]
    parts_small = _all_gather(_pack(small_g), name="gather_g_small")
    sg = _unpack(_sum_parts(parts_small, name="sum_g_small"), small_w)
    updates = [_adamw_native(w, g, m, v, name="adamw_" + n)
               for n, w, g, m, v in zip(SMALL, small_w, sg, small_m, small_v)]
    sd, sm, sv = zip(*updates)

    loss = lax.psum(loss_part[0, 0], MESH_AXES)

    def big(o, idx):
        g_in_, g_three_, g_out_ = o_in[idx], o_three[idx], o_out[idx]
        return {"w_in": g_in_[None], "w_glu": g_three_[None, :D_SSM], "w_branch_ssm": g_three_[None, D_SSM:2 * D_SSM],
                "w_branch_attn": g_three_[None, 2 * D_SSM:], "w_out": g_out_[None]}

    order = ["w_in", "ssm_lambda_re", "ssm_lambda_im", "ssm_b_re", "ssm_b_im", "ssm_c_re", "ssm_c_im", "ssm_d",
             "ssm_log_step", "w_glu", "attn_sinks", "rel_bias_table", "w_branch_ssm", "w_branch_attn", "w_out",
             "ln_gain", "ln_bias"]
    outs = [loss, grad_x[None]]
    for idx, small in enumerate([sg, sd, sm, sv]):
        table = big(None, idx)
        table.update(dict(zip(SMALL, small)))
        outs += [table[n] for n in order]
    return tuple(outs)
```

```python
import functools
import math

import jax
import jax.numpy as jnp
from jax import lax
from jax.experimental import pallas as pl
from jax.experimental.pallas import tpu as pltpu

F32 = jnp.float32
MXU_DTYPE = jnp.bfloat16
WIRE_DTYPE = jnp.bfloat16

D_MODEL = 2048
D_SSM = 1024
SSM_GROUP = 16
N_GROUPS = 64
SSM_STATE = 64
N_Q_HEADS = 16
N_KV_HEADS = 4
Q_PER_KV = 4
HEAD_DIM = 64
D_ATTN = 1024
D_KV = 256
WINDOW = 128
BLOCK = 128
N_BUCKETS = 32
MAX_DISTANCE = 128
D_IN = 8704
DEEPNORM_ALPHA = 2.0 ** 0.25
LN_EPS = 1e-5
NEG_INF = -1e30
ATTN_SCALE = HEAD_DIM ** -0.5

ADAM_LR = 0.001
ADAM_B1 = 0.9
ADAM_B2 = 0.999
ADAM_EPS = 1e-08
ADAM_WD = 0.01
ADAM_STEP = 10

N_DEV = 8
SSM_CHUNK = 16
GROUP_BLOCK = 8
N_GB = N_GROUPS // GROUP_BLOCK
GB_LANES = GROUP_BLOCK * SSM_GROUP
GB_STATE = GROUP_BLOCK * SSM_STATE
COL = 256
OFF_U, OFF_ZS, OFF_Q, OFF_K, OFF_V, OFF_ZA, OFF_GS, OFF_GA = 0, 4, 8, 12, 13, 14, 18, 26

VMEM_CAP = 56 * 1024 * 1024
MESH_AXES = ("x", "y", "c")


def _vmem_limit(block_bytes):
    return int(min(max(3 * block_bytes, 16 * 1024 * 1024), VMEM_CAP))


def _nbytes(shape, dtype):
    return math.prod(shape) * jnp.dtype(dtype).itemsize


def _tile(n, pref):
    if n <= pref:
        return n
    t = (pref // 128) * 128
    while t >= 128:
        if n % t == 0:
            return t
        t -= 128
    return n


def _mm(a, b, *, name, ta=False, tb=False, out_dtype=F32, tm=1024, tn=512, tk=512, add=None, add_scale=1.0,
        after=()):
    squeeze = a.ndim == 2
    if squeeze:
        a, b = a[None], b[None]
        if add is not None:
            add = add[None]
    nb = a.shape[0]
    m, k = (a.shape[2], a.shape[1]) if ta else (a.shape[1], a.shape[2])
    n = b.shape[1] if tb else b.shape[2]
    tm, tn, tk = _tile(m, tm), _tile(n, tn), _tile(k, tk)
    nk = k // tk
    dn = (((0 if ta else 1,), (1 if tb else 0,)), ((), ()))

    a_spec = (pl.BlockSpec((None, tk, tm), lambda g, i, j, kk: (g, kk, i)) if ta
              else pl.BlockSpec((None, tm, tk), lambda g, i, j, kk: (g, i, kk)))
    b_spec = (pl.BlockSpec((None, tn, tk), lambda g, i, j, kk: (g, j, kk)) if tb
              else pl.BlockSpec((None, tk, tn), lambda g, i, j, kk: (g, kk, j)))
    o_spec = pl.BlockSpec((None, tm, tn), lambda g, i, j, kk: (g, i, j))
    in_specs = [a_spec, b_spec]
    operands = [a, b]
    if add is not None:
        in_specs.append(o_spec)
        operands.append(add)
    for tok in after:
        in_specs.append(pl.BlockSpec(memory_space=pl.ANY))
        operands.append(tok)
    n_in = len(operands)

    def body(*refs):
        a_ref, b_ref = refs[0], refs[1]
        add_ref = refs[2] if add is not None else None
        o_ref = refs[n_in]
        acc_ref = refs[-1]
        kk = pl.program_id(3)
        part = lax.dot_general(a_ref[...].astype(MXU_DTYPE), b_ref[...].astype(MXU_DTYPE), dn,
                               preferred_element_type=F32)

        def finish(r):
            if add_ref is not None:
                r = r + add_scale * add_ref[...]
            o_ref[...] = r.astype(out_dtype)

        if nk == 1:
            finish(part)
            return

        @pl.when(kk == 0)
        def _():
            acc_ref[...] = part

        @pl.when(jnp.logical_and(kk > 0, kk < nk - 1))
        def _():
            acc_ref[...] += part

        @pl.when(kk == nk - 1)
        def _():
            finish(acc_ref[...] + part)

    blocks = (_nbytes((tm, tk), a.dtype) + _nbytes((tk, tn), b.dtype) + _nbytes((tm, tn), out_dtype)
              + (_nbytes((tm, tn), F32) if add is not None else 0))
    out = pl.pallas_call(
        body,
        name=name,
        grid=(nb, m // tm, n // tn, nk),
        in_specs=in_specs,
        out_specs=o_spec,
        out_shape=jax.ShapeDtypeStruct((nb, m, n), out_dtype),
        scratch_shapes=[pltpu.VMEM((tm, tn), F32)],
        compiler_params=pltpu.CompilerParams(
            dimension_semantics=("parallel", "parallel", "parallel", "arbitrary"),
            vmem_limit_bytes=_vmem_limit(2 * blocks + 2 * _nbytes((tm, tn), F32))),
    )(*operands)
    return out[0] if squeeze else out


def _ew(fn, ins, outs, *, rows, cw, ncb, tr, name):
    tr = min(tr, rows)
    n_in = len(ins)

    def row_map(off):
        return lambda j, i: (i, off + j)

    def vec_map(off):
        return lambda j, i: (0, off + j)

    in_specs = []
    for arr, kind, off in ins:
        if kind == "row":
            in_specs.append(pl.BlockSpec((tr, cw), row_map(off)))
        else:
            in_specs.append(pl.BlockSpec((1, cw), vec_map(off)))
    out_specs, out_shapes = [], []
    for bw, dt, kind in outs:
        if kind == "row":
            out_specs.append(pl.BlockSpec((tr, bw), row_map(0)))
            out_shapes.append(jax.ShapeDtypeStruct((rows, ncb * bw), dt))
        else:
            out_specs.append(pl.BlockSpec((1, bw), vec_map(0)))
            out_shapes.append(jax.ShapeDtypeStruct((1, ncb * bw), F32))

    def body(*refs):
        i = pl.program_id(1)
        vals = fn(*[r[...] for r in refs[:n_in]])
        for r, (bw, dt, kind), v in zip(refs[n_in:], outs, vals):
            if kind == "row":
                r[...] = v.astype(dt)
            else:
                @pl.when(i == 0)
                def _(r=r):
                    r[...] = jnp.zeros_like(r)

                r[...] += v

    blocks = sum(_nbytes((tr, cw), a.dtype) for a, kind, _ in ins if kind == "row")
    blocks += sum(_nbytes((tr, bw), dt) for bw, dt, kind in outs if kind == "row")
    res = pl.pallas_call(
        body,
        name=name,
        grid=(ncb, rows // tr),
        in_specs=in_specs,
        out_specs=out_specs,
        out_shape=out_shapes,
        compiler_params=pltpu.CompilerParams(
            dimension_semantics=("parallel", "arbitrary"),
            vmem_limit_bytes=_vmem_limit(4 * blocks)),
    )(*[a for a, _, _ in ins])
    return res


def _sigmoid(x):
    return 1.0 / (1.0 + jnp.exp(-x))


INV_SQRT2 = 0.7071067811865476
INV_SQRT_2PI = 0.3989422804014327


def _gelu(x):
    return 0.5 * x * (1.0 + lax.erf(x * INV_SQRT2))


def _gelu_grad(x):
    return 0.5 * (1.0 + lax.erf(x * INV_SQRT2)) + x * INV_SQRT_2PI * jnp.exp(-0.5 * x * x)


def _silu(x):
    return x * _sigmoid(x)


def _silu_grad(x):
    s = _sigmoid(x)
    return s * (1.0 + x * (1.0 - s))


@jax.custom_vjp
def _taps_product(mr, mi, bbr, bbi):
    bbr_t, bbi_t = jnp.transpose(bbr, (0, 2, 1)), jnp.transpose(bbi, (0, 2, 1))
    return jnp.sum(mr[..., None, :] * bbr_t[None, :, None] - mi[..., None, :] * bbi_t[None, :, None], axis=-1)


def _taps_product_fwd(mr, mi, bbr, bbi):
    return _taps_product(mr, mi, bbr, bbi), (mr, mi, bbr, bbi)


def _taps_product_bwd(res, g):
    mr, mi, bbr, bbi = res
    hi = lax.Precision.HIGHEST
    return (jnp.einsum("tghk,gpk->tghp", g, bbr, precision=hi), -jnp.einsum("tghk,gpk->tghp", g, bbi, precision=hi),
            jnp.einsum("tghk,tghp->gpk", g, mr, precision=hi), -jnp.einsum("tghk,tghp->gpk", g, mi, precision=hi))


_taps_product.defvjp(_taps_product_fwd, _taps_product_bwd)


def _ssm_matrices(lam_re, lam_im, b_re, b_im, c_re, c_im, log_step):
    L = SSM_CHUNK
    step = jnp.exp(log_step)[:, None]
    ea, eb = lam_re * step, lam_im * step
    mag = jnp.exp(ea)
    lbr, lbi = mag * jnp.cos(eb), mag * jnp.sin(eb)
    den = lam_re * lam_re + lam_im * lam_im
    nr, ni = lbr - 1.0, lbi
    cr = (nr * lam_re + ni * lam_im) / den
    ci = (ni * lam_re - nr * lam_im) / den
    bbr = cr[..., None] * b_re - ci[..., None] * b_im
    bbi = cr[..., None] * b_im + ci[..., None] * b_re
    taus = jnp.arange(L + 1, dtype=F32)[:, None, None]
    pmag = jnp.exp(taus * ea[None])
    pwr, pwi = pmag * jnp.cos(taus * eb[None]), pmag * jnp.sin(taus * eb[None])
    mr = c_re[None] * pwr[:L, :, None, :] - c_im[None] * pwi[:L, :, None, :]
    mi = c_re[None] * pwi[:L, :, None, :] + c_im[None] * pwr[:L, :, None, :]
    kk = _taps_product(mr, mi, bbr, bbi)
    taps = jnp.transpose(kk.reshape(L, N_GB, GROUP_BLOCK, SSM_GROUP, SSM_GROUP), (1, 0, 2, 4, 3))
    taps = taps.reshape(N_GB, L, GB_LANES, SSM_GROUP)
    rev_r, rev_i = pwr[L - 1 - jnp.arange(L)], pwi[L - 1 - jnp.arange(L)]
    wer = rev_r[..., None] * bbr[None] - rev_i[..., None] * bbi[None]
    wei = rev_r[..., None] * bbi[None] + rev_i[..., None] * bbr[None]

    def rows_in(w):
        w = jnp.transpose(w.reshape(L, N_GB, GROUP_BLOCK, SSM_STATE, SSM_GROUP), (1, 0, 2, 4, 3))
        return w.reshape(N_GB, L * GB_LANES, SSM_STATE)

    wend = jnp.concatenate([rows_in(wer), rows_in(wei)], axis=2)
    m1r = c_re[None] * pwr[1:, :, None, :] - c_im[None] * pwi[1:, :, None, :]
    m1i = c_re[None] * pwi[1:, :, None, :] + c_im[None] * pwr[1:, :, None, :]

    def rows_out(m):
        m = jnp.transpose(m.reshape(L, N_GB, GROUP_BLOCK, SSM_GROUP, SSM_STATE), (1, 2, 4, 0, 3))
        return m.reshape(N_GB, GB_STATE, L * SSM_GROUP)

    wout = jnp.concatenate([rows_out(m1r), rows_out(-m1i)], axis=1)
    acat = jnp.concatenate([pwr[L].reshape(N_GB, 1, GB_STATE), pwi[L].reshape(N_GB, 1, GB_STATE)], axis=2)
    return taps, wend, wout, acat


def _lane_group(shape, axis, shift):
    return (lax.broadcasted_iota(jnp.int32, shape, axis) >> shift) & (GROUP_BLOCK - 1)


def _ssm_expand(taps, wend, wout):
    L = SSM_CHUNK
    taps = jnp.pad(taps, ((0, 0), (0, 0), (0, 0), (0, GB_LANES - SSM_GROUP)))

    def body(t_ref, we_ref, wo_ref, d_ref, web_ref, wob_ref):
        def rc(shape):
            return lax.broadcasted_iota(jnp.int32, shape, 0), lax.broadcasted_iota(jnp.int32, shape, 1)

        r, c = rc((GB_LANES, GB_LANES))
        spread = ((r < SSM_GROUP) & (r == (c & (SSM_GROUP - 1)))).astype(MXU_DTYPE)
        same = _lane_group((GB_LANES, GB_LANES), 0, 4) == _lane_group((GB_LANES, GB_LANES), 1, 4)
        for tau in range(L):
            full = jnp.dot(t_ref[tau].astype(MXU_DTYPE), spread, preferred_element_type=F32)
            d_ref[tau] = jnp.where(same, full, 0.0).astype(d_ref.dtype)
        r, c = rc((2 * SSM_STATE, STATE_W))
        part = (r >> 6) == (c >> 9)
        spread = (part & ((r & (SSM_STATE - 1)) == (c & (SSM_STATE - 1)))).astype(MXU_DTYPE)
        keep = _lane_group((GB_LANES, STATE_W), 0, 4) == _lane_group((GB_LANES, STATE_W), 1, 6)
        for j in range(L):
            rows = slice(j * GB_LANES, (j + 1) * GB_LANES)
            full = jnp.dot(we_ref[rows, :].astype(MXU_DTYPE), spread, preferred_element_type=F32)
            web_ref[rows, :] = jnp.where(keep, full, 0.0).astype(web_ref.dtype)
        r, c = rc((GB_LANES, GB_LANES))
        own = _lane_group((STATE_W, GB_LANES), 0, 6) == _lane_group((STATE_W, GB_LANES), 1, 4)
        for tt in range(L):
            half, k = tt // GROUP_BLOCK, tt % GROUP_BLOCK
            spread = (((r >> 4) == k) & ((r & (SSM_GROUP - 1)) == (c & (SSM_GROUP - 1)))).astype(MXU_DTYPE)
            src = wo_ref[:, half * GB_LANES:(half + 1) * GB_LANES].astype(MXU_DTYPE)
            full = jnp.dot(src, spread, preferred_element_type=F32)
            wob_ref[:, tt * GB_LANES:(tt + 1) * GB_LANES] = jnp.where(own, full, 0.0).astype(wob_ref.dtype)

    def spec(shape):
        return pl.BlockSpec((None,) + shape[1:], lambda b: (b,) + (0,) * (len(shape) - 1))

    out_shapes = [(N_GB, L, GB_LANES, GB_LANES), (N_GB, L * GB_LANES, STATE_W), (N_GB, STATE_W, L * GB_LANES)]
    return tuple(pl.pallas_call(
        body, name="ssm_expand", grid=(N_GB,),
        in_specs=[spec(taps.shape), spec(wend.shape), spec(wout.shape)],
        out_specs=[spec(s) for s in out_shapes],
        out_shape=[jax.ShapeDtypeStruct(s, MXU_DTYPE) for s in out_shapes],
        compiler_params=pltpu.CompilerParams(dimension_semantics=("parallel",), vmem_limit_bytes=VMEM_CAP),
    )(taps, wend, wout))


def _dot(a, b):
    return jnp.dot(a.astype(MXU_DTYPE), b.astype(MXU_DTYPE), preferred_element_type=F32)


def _dot_nt(a, b):
    return lax.dot_general(a.astype(MXU_DTYPE), b.astype(MXU_DTYPE), (((1,), (1,)), ((), ())),
                           preferred_element_type=F32)


def _dot_tn(a, b):
    return lax.dot_general(a.astype(MXU_DTYPE), b.astype(MXU_DTYPE), (((0,), (0,)), ((), ())),
                           preferred_element_type=F32)


STATE_W = 2 * GB_STATE


def _chunk_scan(e, acat):
    nc = e.shape[0]
    spec = pl.BlockSpec((nc, STATE_W), lambda b: (0, b))
    aspec = pl.BlockSpec((None, 1, STATE_W), lambda b: (b, 0, 0))

    def body(e_ref, a_ref, s_ref):
        a_r, a_i = a_ref[:, :GB_STATE], a_ref[:, GB_STATE:]

        def step(c, carry):
            s_r, s_i = carry
            s_ref[pl.ds(c, 1), :GB_STATE] = s_r
            s_ref[pl.ds(c, 1), GB_STATE:] = s_i
            e_r = e_ref[pl.ds(c, 1), :GB_STATE]
            e_i = e_ref[pl.ds(c, 1), GB_STATE:]
            return (a_r * s_r - a_i * s_i + e_r, a_r * s_i + a_i * s_r + e_i)

        zero = jnp.zeros((1, GB_STATE), F32)
        lax.fori_loop(0, nc, step, (zero, zero))

    return pl.pallas_call(
        body, name="ssm_chunk_scan", grid=(N_GB,),
        in_specs=[spec, aspec], out_specs=spec,
        out_shape=jax.ShapeDtypeStruct(e.shape, F32),
        compiler_params=pltpu.CompilerParams(dimension_semantics=("parallel",)),
    )(e, acat)


def _chunk_scan_bwd(ds, s, acat):
    nc = ds.shape[0]
    spec = pl.BlockSpec((nc, STATE_W), lambda b: (0, b))
    aspec = pl.BlockSpec((None, 1, STATE_W), lambda b: (b, 0, 0))

    def body(ds_ref, s_ref, a_ref, ge_ref, da_ref):
        a_r, a_i = a_ref[:, :GB_STATE], a_ref[:, GB_STATE:]

        def step(t, carry):
            g_r, g_i, d_r, d_i = carry
            c = nc - 1 - t
            ge_ref[pl.ds(c, 1), :GB_STATE] = g_r
            ge_ref[pl.ds(c, 1), GB_STATE:] = g_i
            s_r = s_ref[pl.ds(c, 1), :GB_STATE]
            s_i = s_ref[pl.ds(c, 1), GB_STATE:]
            d_r = d_r + g_r * s_r + g_i * s_i
            d_i = d_i + g_i * s_r - g_r * s_i
            n_r = ds_ref[pl.ds(c, 1), :GB_STATE] + a_r * g_r + a_i * g_i
            n_i = ds_ref[pl.ds(c, 1), GB_STATE:] + a_r * g_i - a_i * g_r
            return (n_r, n_i, d_r, d_i)

        zero = jnp.zeros((1, GB_STATE), F32)
        _, _, d_r, d_i = lax.fori_loop(0, nc, step, (zero, zero, zero, zero))
        da_ref[:, :GB_STATE] = d_r
        da_ref[:, GB_STATE:] = d_i

    return pl.pallas_call(
        body, name="ssm_chunk_scan_bwd", grid=(N_GB,),
        in_specs=[spec, spec, aspec], out_specs=[spec, aspec],
        out_shape=[jax.ShapeDtypeStruct(ds.shape, F32), jax.ShapeDtypeStruct(acat.shape, F32)],
        compiler_params=pltpu.CompilerParams(dimension_semantics=("parallel",)),
    )(ds, s, acat)


def _step_rows(ref, j, nc):
    return ref[pl.ds(j, nc, stride=SSM_CHUNK), :]


def _fold_lanes(z, widths):
    for w in widths:
        z = z + pltpu.roll(z, w, 1)
    return z


def _ssm_forward(proj, mats):
    dblk, wend, wout, acat = mats
    t = proj.shape[0]
    nc = t // SSM_CHUNK
    L = SSM_CHUNK
    lanes = pl.BlockSpec((t, GB_LANES), lambda b: (0, b))
    state = pl.BlockSpec((nc, STATE_W), lambda b: (0, b))

    def body_end(u_ref, w_ref, e_ref):
        x = jnp.concatenate([_step_rows(u_ref, j, nc).astype(MXU_DTYPE) for j in range(L)], axis=1)
        e_ref[...] = jnp.dot(x, w_ref[...], preferred_element_type=F32)

    e = pl.pallas_call(
        body_end, name="ssm_chunk_end", grid=(N_GB,),
        in_specs=[lanes, pl.BlockSpec((None, L * GB_LANES, STATE_W), lambda b: (b, 0, 0))],
        out_specs=state, out_shape=jax.ShapeDtypeStruct((nc, N_GB * STATE_W), F32),
        compiler_params=pltpu.CompilerParams(dimension_semantics=("parallel",), vmem_limit_bytes=VMEM_CAP),
    )(proj, wend)
    s = _chunk_scan(e, acat)

    def body_out(u_ref, d_ref, s_ref, w_ref, y_ref):
        xs = [_step_rows(u_ref, j, nc).astype(MXU_DTYPE) for j in range(L)]
        sb = s_ref[...].astype(MXU_DTYPE)
        for tt in range(L):
            xcat = jnp.concatenate(xs[:tt + 1], axis=1)
            taps = jnp.concatenate([d_ref[tt - j] for j in range(tt + 1)], axis=0).astype(MXU_DTYPE)
            y = (jnp.dot(xcat, taps, preferred_element_type=F32)
                 + jnp.dot(sb, w_ref[:, tt * GB_LANES:(tt + 1) * GB_LANES], preferred_element_type=F32))
            y_ref[pl.ds(tt, nc, stride=L), :] = y

    y = pl.pallas_call(
        body_out, name="ssm_chunk_out", grid=(N_GB,),
        in_specs=[lanes, pl.BlockSpec((None, L, GB_LANES, GB_LANES), lambda b: (b, 0, 0, 0)), state,
                  pl.BlockSpec((None, STATE_W, L * GB_LANES), lambda b: (b, 0, 0))],
        out_specs=lanes, out_shape=jax.ShapeDtypeStruct((t, D_SSM), F32),
        compiler_params=pltpu.CompilerParams(dimension_semantics=("parallel",), vmem_limit_bytes=VMEM_CAP),
    )(proj, dblk, s, wout)
    return y, s


def _ssm_backward(dy, proj, s, mats):
    dblk, wend, wout, acat = mats
    t = proj.shape[0]
    nc = t // SSM_CHUNK
    L = SSM_CHUNK
    lanes = pl.BlockSpec((t, GB_LANES), lambda b: (0, b))
    state = pl.BlockSpec((nc, STATE_W), lambda b: (0, b))
    taps_spec = pl.BlockSpec((None, L, GB_LANES, GB_LANES), lambda b: (b, 0, 0, 0))

    def body_state(dy_ref, w_ref, ds_ref):
        dyc = jnp.concatenate([_step_rows(dy_ref, tt, nc).astype(MXU_DTYPE) for tt in range(L)], axis=1)
        ds_ref[...] = _dot_nt(dyc, w_ref[...])

    ds = pl.pallas_call(
        body_state, name="ssm_bwd_state", grid=(N_GB,),
        in_specs=[lanes, pl.BlockSpec((None, STATE_W, L * GB_LANES), lambda b: (b, 0, 0))],
        out_specs=state, out_shape=jax.ShapeDtypeStruct((nc, N_GB * STATE_W), F32),
        compiler_params=pltpu.CompilerParams(dimension_semantics=("parallel",), vmem_limit_bytes=VMEM_CAP),
    )(dy, wout)
    ge, dacat = _chunk_scan_bwd(ds, s, acat)

    def body_in(u_ref, dy_ref, d_ref, ge_ref, w_ref, du_ref, dd_ref):
        xs = [_step_rows(u_ref, j, nc).astype(MXU_DTYPE) for j in range(L)]
        dys = [_step_rows(dy_ref, tt, nc).astype(MXU_DTYPE) for tt in range(L)]
        ge = ge_ref[...].astype(MXU_DTYPE)
        for i in range(L):
            dyc = jnp.concatenate(dys[i:], axis=1)
            taps = jnp.concatenate([d_ref[tt - i] for tt in range(i, L)], axis=1).astype(MXU_DTYPE)
            du_ref[pl.ds(i, nc, stride=L), :] = (
                _dot_nt(dyc, taps) + _dot_nt(ge, w_ref[i * GB_LANES:(i + 1) * GB_LANES, :]))
        for j in range(L):
            m = _dot_tn(xs[j], jnp.concatenate(dys[j:], axis=1))
            for tau in range(L - j):
                part = m[:, tau * GB_LANES:(tau + 1) * GB_LANES]
                if j == 0:
                    dd_ref[tau] = part
                else:
                    dd_ref[tau] += part
        same = _lane_group((GB_LANES, GB_LANES), 0, 4) == _lane_group((GB_LANES, GB_LANES), 1, 4)
        for tau in range(L):
            dd_ref[tau] = _fold_lanes(jnp.where(same, dd_ref[tau], 0.0), (64, 32, 16))

    du, ddblk = pl.pallas_call(
        body_in, name="ssm_bwd_in", grid=(N_GB,),
        in_specs=[lanes, lanes, taps_spec, state,
                  pl.BlockSpec((None, L * GB_LANES, STATE_W), lambda b: (b, 0, 0))],
        out_specs=[lanes, taps_spec],
        out_shape=[jax.ShapeDtypeStruct((t, D_SSM), F32), jax.ShapeDtypeStruct(dblk.shape, F32)],
        compiler_params=pltpu.CompilerParams(dimension_semantics=("parallel",), vmem_limit_bytes=VMEM_CAP),
    )(proj, dy, dblk, ge, wend)

    def body_w(u_ref, dy_ref, ge_ref, s_ref, dwe_ref, dwo_ref):
        ge = ge_ref[...].astype(MXU_DTYPE)
        sb = s_ref[...].astype(MXU_DTYPE)
        keep = _lane_group((GB_LANES, STATE_W), 0, 4) == _lane_group((GB_LANES, STATE_W), 1, 6)
        low = lax.broadcasted_iota(jnp.int32, (GB_LANES, 2 * SSM_STATE), 1) < SSM_STATE

        def fold_state(z):
            z = z[:, :GB_STATE // 2] + z[:, GB_STATE // 2:]
            z = z[:, :GB_STATE // 4] + z[:, GB_STATE // 4:]
            return _fold_lanes(z, (SSM_STATE,))

        for j in range(L):
            z = jnp.where(keep, _dot_tn(_step_rows(u_ref, j, nc).astype(MXU_DTYPE), ge), 0.0)
            dwe_ref[j * GB_LANES:(j + 1) * GB_LANES, :] = jnp.where(
                low, fold_state(z[:, :GB_STATE]), fold_state(z[:, GB_STATE:]))
        own = _lane_group((STATE_W, GB_LANES), 0, 6) == _lane_group((STATE_W, GB_LANES), 1, 4)
        chunk = lax.broadcasted_iota(jnp.int32, (STATE_W, GB_LANES), 1) >> 4
        for half in range(L // GROUP_BLOCK):
            acc = jnp.zeros((STATE_W, GB_LANES), F32)
            for k in range(GROUP_BLOCK):
                tt = half * GROUP_BLOCK + k
                z = jnp.where(own, _dot_tn(sb, _step_rows(dy_ref, tt, nc).astype(MXU_DTYPE)), 0.0)
                acc = acc + jnp.where(chunk == k, _fold_lanes(z, (64, 32, 16)), 0.0)
            dwo_ref[:, half * GB_LANES:(half + 1) * GB_LANES] = acc

    dwend, dwout = pl.pallas_call(
        body_w, name="ssm_bwd_w", grid=(N_GB,),
        in_specs=[lanes, lanes, state, state],
        out_specs=[pl.BlockSpec((None, L * GB_LANES, 2 * SSM_STATE), lambda b: (b, 0, 0)),
                   pl.BlockSpec((None, STATE_W, L * SSM_GROUP), lambda b: (b, 0, 0))],
        out_shape=[jax.ShapeDtypeStruct((N_GB, L * GB_LANES, 2 * SSM_STATE), F32),
                   jax.ShapeDtypeStruct((N_GB, STATE_W, L * SSM_GROUP), F32)],
        compiler_params=pltpu.CompilerParams(dimension_semantics=("parallel",), vmem_limit_bytes=VMEM_CAP),
    )(proj, dy, ge, s)
    return du, (ddblk[:, :, :, :SSM_GROUP], dwend, dwout, dacat)


def _t5_bucket(dist):
    max_exact = N_BUCKETS // 2
    is_small = dist < max_exact
    d = jnp.maximum(dist, 1).astype(F32)
    large = max_exact + (jnp.log(d / max_exact) / math.log(MAX_DISTANCE / max_exact)
                         * (N_BUCKETS - max_exact)).astype(jnp.int32)
    large = jnp.minimum(large, N_BUCKETS - 1)
    return jnp.where(is_small, dist, large)


def _band_bias(rel_bias_table):
    i = jnp.arange(BLOCK)[:, None]
    j = jnp.arange(BLOCK)[None, :]
    bucket = _t5_bucket(jnp.where(j > i, BLOCK + i - j, i - j))
    onehot = (bucket[:, :, None] == jnp.arange(N_BUCKETS)[None, None, :]).astype(F32)
    return jnp.einsum("qsb,bh->hqs", onehot, rel_bias_table, precision=lax.Precision.HIGHEST)


KV_PAIR = 2
HEADS_PER_STEP = KV_PAIR * Q_PER_KV
Q_LANES = HEADS_PER_STEP * HEAD_DIM
SLAB = 2 * HEAD_DIM
Q_COL0 = OFF_Q * COL // Q_LANES
K_COL0 = OFF_K * COL // SLAB
V_COL0 = OFF_V * COL // SLAB
ZA_COL0 = OFF_ZA * COL // Q_LANES


def _attn_specs():
    q_spec = pl.BlockSpec((BLOCK, Q_LANES), lambda m, n: (n, Q_COL0 + m))
    k_prev = pl.BlockSpec((BLOCK, SLAB), lambda m, n: (jnp.maximum(n - 1, 0), K_COL0 + m))
    k_cur = pl.BlockSpec((BLOCK, SLAB), lambda m, n: (n, K_COL0 + m))
    v_prev = pl.BlockSpec((BLOCK, SLAB), lambda m, n: (jnp.maximum(n - 1, 0), V_COL0 + m))
    v_cur = pl.BlockSpec((BLOCK, SLAB), lambda m, n: (n, V_COL0 + m))
    bias_spec = pl.BlockSpec((HEADS_PER_STEP, BLOCK, BLOCK), lambda m, n: (m, 0, 0))
    sink_spec = pl.BlockSpec(memory_space=pltpu.SMEM)
    wide = pl.BlockSpec((BLOCK, Q_LANES), lambda m, n: (n, m))
    gate = pl.BlockSpec((BLOCK, Q_LANES), lambda m, n: (n, ZA_COL0 + m))
    pair = pl.BlockSpec((BLOCK, SLAB), lambda m, n: (n, m))
    return [sink_spec, q_spec, k_prev, k_cur, v_prev, v_cur, bias_spec, gate], wide, pair


def _low_lanes(shape):
    return lax.broadcasted_iota(jnp.int32, shape, 1) < HEAD_DIM


def _pair_halves(ref):
    kb = ref[...]
    sw = pltpu.roll(kb, HEAD_DIM, 1)
    lo = _low_lanes(kb.shape)
    zero = jnp.zeros_like(kb)
    first = (jnp.where(lo, kb, zero).astype(MXU_DTYPE), jnp.where(lo, zero, sw).astype(MXU_DTYPE))
    second = (jnp.where(lo, sw, zero).astype(MXU_DTYPE), jnp.where(lo, zero, kb).astype(MXU_DTYPE))
    return first, second


def _fold_pair(acc):
    f = [x + pltpu.roll(x, HEAD_DIM, 1) for x in acc]
    return jnp.where(_low_lanes(f[0].shape), f[0], f[1])


def _from_prev(n):
    row = lax.broadcasted_iota(jnp.int32, (BLOCK, BLOCK), 0)
    col = lax.broadcasted_iota(jnp.int32, (BLOCK, BLOCK), 1)
    prev = col > row
    return prev, jnp.where(jnp.logical_and(n == 0, prev), NEG_INF, 0.0)


def _softmax_sink(s, sink):
    m = jnp.maximum(jnp.max(s, axis=1, keepdims=True), sink)
    e = jnp.exp(s - m)
    es = jnp.exp(sink - m)
    inv = 1.0 / (jnp.sum(e, axis=1, keepdims=True) + es)
    return e * inv, es * inv


def _stack_pair(prev_halves, own_halves, a):
    return jnp.concatenate([prev_halves[a][0], prev_halves[a][1], own_halves[a][0], own_halves[a][1]], axis=0)


def _split_heads(x4, prev):
    return [jnp.where(prev, x4[:, e * BLOCK:(e + 1) * BLOCK], x4[:, (2 + e) * BLOCK:(3 + e) * BLOCK]) for e in range(2)]


def _spread_heads(x, prev):
    return jnp.concatenate([jnp.where(prev, x[0], 0.0), jnp.where(prev, x[1], 0.0),
                            jnp.where(prev, 0.0, x[0]), jnp.where(prev, 0.0, x[1])], axis=1)


def _attn_forward(proj, bias, sinks):
    t = proj.shape[0]
    in_specs, wide, _ = _attn_specs()

    def body(sink_ref, q_ref, kp_ref, kc_ref, vp_ref, vc_ref, bias_ref, z_ref, o_ref, h_ref):
        m, n = pl.program_id(0), pl.program_id(1)
        kp, kc, vp, vc = (_pair_halves(r) for r in (kp_ref, kc_ref, vp_ref, vc_ref))
        keys = [_stack_pair(kp, kc, a) for a in range(KV_PAIR)]
        vals = [_stack_pair(vp, vc, a) for a in range(KV_PAIR)]
        prev, edge = _from_prev(n)
        prev2 = jnp.concatenate([prev, prev], axis=0)
        for a in range(KV_PAIR):
            slabs = (2 * a, 2 * a + 1)
            q = jnp.concatenate([q_ref[:, s * SLAB:(s + 1) * SLAB] for s in slabs], axis=0).astype(MXU_DTYPE)
            logits = _split_heads(_dot_nt(q, keys[a]), prev2)
            probs = []
            for e in range(2):
                rows = [_softmax_sink(logits[e][r * BLOCK:(r + 1) * BLOCK] * ATTN_SCALE + bias_ref[2 * s + e] + edge,
                                      sink_ref[m * HEADS_PER_STEP + 2 * s + e])[0] for r, s in enumerate(slabs)]
                probs.append(jnp.concatenate(rows, axis=0))
            out = _dot(_spread_heads(probs, prev2), vals[a])
            for r, s in enumerate(slabs):
                cols = slice(s * SLAB, (s + 1) * SLAB)
                o_ref[:, cols] = out[r * BLOCK:(r + 1) * BLOCK]
                h_ref[:, cols] = (out[r * BLOCK:(r + 1) * BLOCK] * _silu(z_ref[:, cols])).astype(h_ref.dtype)

    return pl.pallas_call(
        body, name="attn_fwd", grid=(N_KV_HEADS // KV_PAIR, t // BLOCK),
        in_specs=in_specs, out_specs=[wide, wide],
        out_shape=[jax.ShapeDtypeStruct((t, D_ATTN), F32), jax.ShapeDtypeStruct((t, D_ATTN), MXU_DTYPE)],
        compiler_params=pltpu.CompilerParams(dimension_semantics=("parallel", "arbitrary")),
    )(sinks, proj, proj, proj, proj, proj, bias, proj)


def _attn_backward(proj, bias, sinks, attn, dh):
    t = proj.shape[0]
    in_specs, wide, pair = _attn_specs()
    bias_spec = in_specs[-2]
    sink_out = pl.BlockSpec((HEADS_PER_STEP, 8, 128), lambda m, n: (m, 0, 0))

    def body(sink_ref, q_ref, kp_ref, kc_ref, vp_ref, vc_ref, bias_ref, z_ref, o_ref, dh_ref,
             dq_ref, dz_ref, dka_ref, dkb_ref, dva_ref, dvb_ref, dbias_ref, dsink_ref):
        m, n = pl.program_id(0), pl.program_id(1)

        @pl.when(n == 0)
        def _():
            dbias_ref[...] = jnp.zeros_like(dbias_ref)
            dsink_ref[...] = jnp.zeros_like(dsink_ref)

        kp, kc, vp, vc = (_pair_halves(r) for r in (kp_ref, kc_ref, vp_ref, vc_ref))
        keys = [_stack_pair(kp, kc, a) for a in range(KV_PAIR)]
        vals = [_stack_pair(vp, vc, a) for a in range(KV_PAIR)]
        prev, edge = _from_prev(n)
        lo = _low_lanes((BLOCK, SLAB))
        prev2 = jnp.concatenate([prev, prev], axis=0)
        dk = [[None] * KV_PAIR for _ in range(2)]
        dv = [[None] * KV_PAIR for _ in range(2)]
        for a in range(KV_PAIR):
            slabs = (2 * a, 2 * a + 1)
            q = jnp.concatenate([q_ref[:, s * SLAB:(s + 1) * SLAB] for s in slabs], axis=0).astype(MXU_DTYPE)
            gated = []
            for s in slabs:
                cols = slice(s * SLAB, (s + 1) * SLAB)
                dh_s, z_s = dh_ref[:, cols], z_ref[:, cols]
                dz_ref[:, cols] = (dh_s * o_ref[:, cols] * _silu_grad(z_s)).astype(dz_ref.dtype)
                gated.append(dh_s * _silu(z_s))
            do = jnp.concatenate(gated, axis=0).astype(MXU_DTYPE)
            logits = _split_heads(_dot_nt(q, keys[a]), prev2)
            dprobs = _split_heads(_dot_nt(do, vals[a]), prev2)
            probs, dlogits = [], []
            for e in range(2):
                p_rows, ds_rows = [], []
                for r, s in enumerate(slabs):
                    h = 2 * s + e
                    rows = slice(r * BLOCK, (r + 1) * BLOCK)
                    p, ps = _softmax_sink(logits[e][rows] * ATTN_SCALE + bias_ref[h] + edge,
                                          sink_ref[m * HEADS_PER_STEP + h])
                    delta = jnp.sum(p * dprobs[e][rows], axis=1, keepdims=True)
                    ds = p * (dprobs[e][rows] - delta)
                    dbias_ref[h] += ds
                    dsink_ref[h] += jnp.broadcast_to(jnp.sum(-ps * delta, axis=0, keepdims=True), (8, 128))
                    p_rows.append(p)
                    ds_rows.append(ds)
                probs.append(jnp.concatenate(p_rows, axis=0))
                dlogits.append(jnp.concatenate(ds_rows, axis=0))
            ds4 = _spread_heads(dlogits, prev2).astype(MXU_DTYPE)
            p4 = _spread_heads(probs, prev2).astype(MXU_DTYPE)
            dq = _dot(ds4, keys[a]) * ATTN_SCALE
            for r, s in enumerate(slabs):
                dq_ref[:, s * SLAB:(s + 1) * SLAB] = dq[r * BLOCK:(r + 1) * BLOCK].astype(dq_ref.dtype)
            rk = _dot_tn(ds4, q)
            rv = _dot_tn(p4, do)
            for which in range(2):
                top = 2 * which * BLOCK
                dk[which][a] = jnp.where(lo, rk[top:top + BLOCK], rk[top + BLOCK:top + 2 * BLOCK])
                dv[which][a] = jnp.where(lo, rv[top:top + BLOCK], rv[top + BLOCK:top + 2 * BLOCK])
        dkb_ref[...] = _fold_pair(dk[0]) * ATTN_SCALE
        dka_ref[...] = _fold_pair(dk[1]) * ATTN_SCALE
        dvb_ref[...] = _fold_pair(dv[0])
        dva_ref[...] = _fold_pair(dv[1])

    kv_shape = jax.ShapeDtypeStruct((t, D_KV), F32)
    return pl.pallas_call(
        body, name="attn_bwd", grid=(N_KV_HEADS // KV_PAIR, t // BLOCK),
        in_specs=in_specs + [wide, wide],
        out_specs=[wide, wide, pair, pair, pair, pair, bias_spec, sink_out],
        out_shape=[jax.ShapeDtypeStruct((t, D_ATTN), MXU_DTYPE), jax.ShapeDtypeStruct((t, D_ATTN), MXU_DTYPE),
                   kv_shape, kv_shape, kv_shape, kv_shape,
                   jax.ShapeDtypeStruct(bias.shape, F32), jax.ShapeDtypeStruct((N_Q_HEADS, 8, 128), F32)],
        compiler_params=pltpu.CompilerParams(dimension_semantics=("parallel", "arbitrary")),
    )(sinks, proj, proj, proj, proj, proj, bias, proj, attn, dh)


def _shift_blocks(cur, prev):
    return cur + jnp.concatenate([prev[BLOCK:], jnp.zeros_like(prev[:BLOCK])], axis=0)


def _mesh_pos():
    return lax.axis_index("x"), lax.axis_index("y"), lax.axis_index("c")


def _all_gather(x, *, name):
    def body(x_ref, out_ref, send_sems, recv_sems, local_sem):
        x, y, c = _mesh_pos()
        me, sibling = (x, y, c), (x, y, 1 - c)
        chips = [(1 - x, y), (x, 1 - y), (1 - x, 1 - y)]

        def slot(px, py, pc):
            return out_ref.at[4 * px + 2 * py + pc]

        def copy(k, block, to, src=None):
            return pltpu.make_async_remote_copy(
                src_ref=slot(*block) if src is None else src, dst_ref=slot(*block),
                send_sem=send_sems.at[k], recv_sem=recv_sems.at[k],
                device_id=to, device_id_type=pl.DeviceIdType.MESH)

        mine = pltpu.make_async_copy(x_ref, slot(*me), local_sem)
        mine.start()
        first = [copy(0, me, sibling, src=x_ref)]
        first += [copy(1 + j, me, (*chip, c), src=x_ref) for j, chip in enumerate(chips)]
        for cp in first:
            cp.start()
        passed = [copy(4 + j, (*chip, c), sibling) for j, chip in enumerate(chips)]
        for j, chip in enumerate(chips):
            copy(1 + j, (*chip, c), me).wait_recv()
            passed[j].start()
        copy(0, sibling, me).wait_recv()
        for j, chip in enumerate(chips):
            copy(4 + j, (*chip, 1 - c), me).wait_recv()
        for cp in first + passed:
            cp.wait_send()
        mine.wait()

    return pl.pallas_call(
        body, name=name,
        in_specs=[pl.BlockSpec(memory_space=pl.ANY)],
        out_specs=pl.BlockSpec(memory_space=pl.ANY),
        out_shape=jax.ShapeDtypeStruct((N_DEV,) + x.shape, x.dtype),
        scratch_shapes=[pltpu.SemaphoreType.DMA((7,)), pltpu.SemaphoreType.DMA((7,)), pltpu.SemaphoreType.DMA],
    )(x)


_HBM = pl.BlockSpec(memory_space=pltpu.HBM)
_SEM = pl.BlockSpec(memory_space=pltpu.SEMAPHORE)
_DATAFLOW = pltpu.SideEffectType.DATAFLOW_SIDE_EFFECTING


def _peers():
    x, y, c = _mesh_pos()
    others = []
    for k in range(1, N_DEV):
        px, py, pc = x ^ (k >> 2), y ^ ((k >> 1) & 1), c ^ (k & 1)
        others.append(((px, py, pc), 4 * px + 2 * py + pc))
    return 4 * x + 2 * y + c, others


def _split_start(bufs, plan, n_copies, *, name):
    nb = len(bufs)

    def body(*refs):
        send_sems, recv_sems, token = refs[nb], refs[nb + 1], refs[-1]
        for k, (src, dst, pos, _) in enumerate(plan(*refs[:nb])):
            pltpu.make_async_remote_copy(src_ref=src, dst_ref=dst, send_sem=send_sems.at[k], recv_sem=recv_sems.at[k],
                                         device_id=pos, device_id_type=pl.DeviceIdType.MESH).start()
        token[...] = jnp.zeros_like(token)

    return pl.pallas_call(
        body, name=name,
        out_shape=(pltpu.SemaphoreType.DMA((n_copies,)), pltpu.SemaphoreType.DMA((n_copies,)),
                   *[pltpu.HBM(b.shape, b.dtype) for b in bufs], jax.ShapeDtypeStruct((8, 128), F32)),
        in_specs=(_HBM,) * nb, out_specs=(_SEM, _SEM) + (_HBM,) * nb + (pl.BlockSpec(memory_space=pltpu.VMEM),),
        input_output_aliases={i: 2 + i for i in range(nb)},
        compiler_params=pltpu.CompilerParams(has_side_effects=_DATAFLOW),
    )(*[pltpu.with_memory_space_constraint(b, pltpu.HBM) for b in bufs])


def _split_wait(started, plan, after, *, name):
    send_sems, recv_sems, *thru = started[:-1]
    nb = len(thru)

    def body(*refs):
        send_sems, recv_sems = refs[nb], refs[nb + 1]
        for k, (src, _, pos, arrive) in enumerate(plan(*refs[:nb])):
            copy = pltpu.make_async_remote_copy(
                src_ref=src, dst_ref=arrive, send_sem=send_sems.at[k], recv_sem=recv_sems.at[k],
                device_id=pos, device_id_type=pl.DeviceIdType.MESH)
            copy.wait_send()
            copy.wait_recv()

    return pl.pallas_call(
        body, name=name,
        out_shape=tuple(pltpu.HBM(b.shape, b.dtype) for b in thru),
        in_specs=(_HBM,) * nb + (_SEM, _SEM, pl.BlockSpec(memory_space=pl.ANY)), out_specs=(_HBM,) * nb,
        input_output_aliases={i: i for i in range(nb)},
        compiler_params=pltpu.CompilerParams(has_side_effects=_DATAFLOW),
    )(*thru, send_sems, recv_sems, after)


def _plan_scatter(x_ref, land_ref):
    me, others = _peers()
    return [(x_ref.at[idx], land_ref.at[me], pos, land_ref.at[idx]) for pos, idx in others]


def _plan_gather(x_ref, land_ref):
    me, others = _peers()
    return [(x_ref, land_ref.at[me], pos, land_ref.at[idx]) for pos, idx in others]


def _near_and_far():
    x, y, c = _mesh_pos()
    chips = [(1 - x, y), (x, 1 - y), (1 - x, 1 - y)]
    near = [(x, y, 1 - c)] + [(px, py, c) for px, py in chips]
    relay = [(4 * px + 2 * py + c, 4 * px + 2 * py + 1 - c) for px, py in chips]
    return 4 * x + 2 * y + c, near, (x, y, 1 - c), relay


def _plan_gather_near(x_ref, land_ref):
    me, near, _, _ = _near_and_far()
    return [(x_ref, land_ref.at[me], pos, land_ref.at[4 * pos[0] + 2 * pos[1] + pos[2]]) for pos in near]


def _plan_gather_relay(land_ref):
    _, _, sibling, relay = _near_and_far()
    return [(land_ref.at[mine], land_ref.at[mine], sibling, land_ref.at[theirs]) for mine, theirs in relay]


def _landing_zone(own):
    me, _ = _peers()
    return lax.dynamic_update_index_in_dim(lax.empty((N_DEV,) + own.shape, own.dtype), own, me, 0)


def _scatter_start(x, *, name):
    if x.ndim == 2:
        return _split_start((x, _landing_zone(x)), _plan_gather, N_DEV - 1, name=name)
    me, _ = _peers()
    own = lax.dynamic_index_in_dim(x, me, 0, keepdims=False)
    return _split_start((x, _landing_zone(own)), _plan_scatter, N_DEV - 1, name=name)


def _scatter_wait(started, after, *, name):
    plan = _plan_gather if started[2].ndim == 2 else _plan_scatter
    return _split_wait(started, plan, after, name=name)[1]


def _adamw_math(w, g, m, v):
    m = ADAM_B1 * m + (1.0 - ADAM_B1) * g
    v = ADAM_B2 * v + (1.0 - ADAM_B2) * (g * g)
    m_hat = m / (1.0 - ADAM_B1 ** ADAM_STEP)
    v_hat = v / (1.0 - ADAM_B2 ** ADAM_STEP)
    delta = -ADAM_LR * (m_hat / (jnp.sqrt(v_hat) + ADAM_EPS) + ADAM_WD * w)
    return delta, m, v


def _adamw_reduce(parts, w, m, v, *, name, tr):
    r, c = w.shape
    tr = min(tr, r)
    spec = pl.BlockSpec((tr, c), lambda i: (i, 0))

    def body(p_ref, w_ref, m_ref, v_ref, g_ref, d_ref, nm_ref, nv_ref):
        g = p_ref[0].astype(F32)
        for s in range(1, N_DEV):
            g = g + p_ref[s].astype(F32)
        delta, nm, nv = _adamw_math(w_ref[...], g, m_ref[...], v_ref[...])
        g_ref[...] = g
        d_ref[...] = delta
        nm_ref[...] = nm
        nv_ref[...] = nv

    return pl.pallas_call(
        body, name=name, grid=(r // tr,),
        in_specs=[pl.BlockSpec((N_DEV, tr, c), lambda i: (0, i, 0)), spec, spec, spec],
        out_specs=[spec] * 4,
        out_shape=[jax.ShapeDtypeStruct((r, c), F32)] * 4,
        compiler_params=pltpu.CompilerParams(
            dimension_semantics=("parallel",),
            vmem_limit_bytes=_vmem_limit(2 * 15 * _nbytes((tr, c), F32))),
    )(parts, w, m, v)


def _sum_parts(parts, *, name):
    def body(p_ref, o_ref):
        g = p_ref[0]
        for s in range(1, N_DEV):
            g = g + p_ref[s]
        o_ref[...] = g

    return pl.pallas_call(
        body, name=name, out_shape=jax.ShapeDtypeStruct(parts.shape[1:], F32),
        compiler_params=pltpu.CompilerParams(vmem_limit_bytes=_vmem_limit(_nbytes(parts.shape, F32))),
    )(parts)


def _adamw_native(w, g, m, v, *, name):
    def body(w_ref, g_ref, m_ref, v_ref, d_ref, nm_ref, nv_ref):
        d_ref[...], nm_ref[...], nv_ref[...] = _adamw_math(w_ref[...], g_ref[...], m_ref[...], v_ref[...])

    return pl.pallas_call(body, name=name, out_shape=[jax.ShapeDtypeStruct(w.shape, F32)] * 3)(w, g, m, v)


SMALL = ["ssm_lambda_re", "ssm_lambda_im", "ssm_b_re", "ssm_b_im", "ssm_c_re", "ssm_c_im", "ssm_d",
         "ssm_log_step", "attn_sinks", "rel_bias_table", "ln_gain", "ln_bias"]


def _pack(arrs):
    flat = jnp.concatenate([a.reshape(-1) for a in arrs])
    pad = (-flat.shape[0]) % 1024
    return jnp.pad(flat, (0, pad)).reshape(-1, 128)


def _unpack(packed, like):
    flat = packed.reshape(-1)
    out, pos = [], 0
    for a in like:
        out.append(flat[pos:pos + a.size].reshape(a.shape))
        pos += a.size
    return out


def kernel(x, w_in, ssm_lambda_re, ssm_lambda_im, ssm_b_re, ssm_b_im, ssm_c_re, ssm_c_im, ssm_d, ssm_log_step, w_glu, attn_sinks, rel_bias_table, w_branch_ssm, w_branch_attn, w_out, ln_gain, ln_bias, loss_target, m_w_in, m_ssm_lambda_re, m_ssm_lambda_im, m_ssm_b_re, m_ssm_b_im, m_ssm_c_re, m_ssm_c_im, m_ssm_d, m_ssm_log_step, m_w_glu, m_attn_sinks, m_rel_bias_table, m_w_branch_ssm, m_w_branch_attn, m_w_out, m_ln_gain, m_ln_bias, v_w_in, v_ssm_lambda_re, v_ssm_lambda_im, v_ssm_b_re, v_ssm_b_im, v_ssm_c_re, v_ssm_c_im, v_ssm_d, v_ssm_log_step, v_w_glu, v_attn_sinks, v_rel_bias_table, v_w_branch_ssm, v_w_branch_attn, v_w_out, v_ln_gain, v_ln_bias):
    t = x.shape[1]
    xs = x[0]
    target = loss_target[0]
    col_in = w_in.shape[2]
    col_br = w_glu.shape[2]
    row_out = w_out.shape[1]

    w_in_mx = w_in[0].astype(MXU_DTYPE)
    near = _split_start((w_in_mx, _landing_zone(w_in_mx)), _plan_gather_near, 4, name="gather_w_in_near_start")
    ssm_params = (ssm_lambda_re[0], ssm_lambda_im[0], ssm_b_re[0], ssm_b_im[0], ssm_c_re[0], ssm_c_im[0],
                  ssm_log_step[0] + near[-1][0, 0])
    mats, mats_vjp = jax.vjp(_ssm_matrices, *ssm_params)
    mats_mx = _ssm_expand(*mats[:3]) + (mats[3],)
    sinks = attn_sinks[0]
    d_skip = ssm_d
    _, landed = _split_wait(near, _plan_gather_near, mats_mx[1], name="gather_w_in_near_wait")
    relay = _split_start((landed,), _plan_gather_relay, 3, name="gather_w_in_relay_start")
    bias, bias_vjp = jax.vjp(_band_bias, rel_bias_table + relay[-1][0, 0])
    x_mx = xs.astype(MXU_DTYPE)
    (g_in,) = _split_wait(relay, _plan_gather_relay, bias, name="gather_w_in_relay_wait")
    win = g_in.transpose(1, 0, 2).reshape(D_MODEL, D_IN)
    hold = g_in[0, 0, 0] * 0
    three = jnp.concatenate([w_glu[0], w_branch_ssm[0], w_branch_attn[0]], axis=0).astype(MXU_DTYPE)
    sent_three = _scatter_start(three + hold, name="gather_w_1024_start")
    sent_wout = _scatter_start(w_out[0].astype(MXU_DTYPE) + hold, name="gather_w_out_start")

    proj = _mm(x_mx, win, name="in_proj", tm=2048, tn=512, tk=2048, after=(sent_three[4], sent_wout[4]))
    y_conv, states = _ssm_forward(proj, mats_mx)

    def f_gelu(yv, u, d):
        ys = yv + d * u
        return ys, _gelu(ys)

    y_ssm, glu_in = _ew(f_gelu, [(y_conv, "row", 0), (proj, "row", OFF_U), (d_skip, "vec", 0)],
                        [(COL, F32, "row"), (COL, MXU_DTYPE, "row")], rows=t, cw=COL, ncb=4, tr=1024, name="ssm_gelu")
    g_three = _scatter_wait(sent_three, glu_in, name="gather_w_1024_wait")
    three_full = g_three.transpose(1, 0, 2).reshape(3 * D_SSM, N_DEV * col_br)
    wglu, wbs, wba = three_full[:D_SSM], three_full[D_SSM:2 * D_SSM], three_full[2 * D_SSM:]
    glu = _mm(glu_in, wglu, name="glu_proj", tm=2048, tn=512, tk=1024)

    def f_hssm(ga, gb, z):
        return (ga * _sigmoid(gb) * _silu(z),)

    (h_ssm,) = _ew(f_hssm, [(glu, "row", 0), (glu, "row", 4), (proj, "row", OFF_ZS)],
                   [(COL, MXU_DTYPE, "row")], rows=t, cw=COL, ncb=4, tr=1024, name="ssm_gate")

    attn, h_attn = _attn_forward(proj, bias, sinks)
    p_ssm = _mm(h_ssm, wbs, name="branch_ssm", tm=2048, tn=512, tk=1024)
    p_attn = _mm(h_attn, wba, name="branch_attn", tm=2048, tn=512, tk=1024)

    def f_merge(ps, pa, ls, la):
        return (_sigmoid(ls) * ps + _sigmoid(la) * pa,)

    (merged,) = _ew(f_merge, [(p_ssm, "row", 0), (p_attn, "row", 0), (proj, "row", OFF_GS), (proj, "row", OFF_GA)],
                    [(COL, MXU_DTYPE, "row")], rows=t, cw=COL, ncb=8, tr=1024, name="merge")
    wout = _scatter_wait(sent_wout, merged, name="gather_w_out_wait").reshape(D_MODEL, D_MODEL)
    out = _mm(merged, wout, name="out_proj", tm=2048, tn=512, tk=2048)

    def f_norm(xv, ov, tg, gain, lbias):
        r = DEEPNORM_ALPHA * xv + ov
        mu = jnp.mean(r, axis=1, keepdims=True)
        cen = r - mu
        var = jnp.mean(cen * cen, axis=1, keepdims=True)
        rstd = lax.rsqrt(var + LN_EPS)
        xhat = cen * rstd
        yv = xhat * gain + lbias
        diff = yv - tg
        row_loss = 0.5 * jnp.mean(diff * diff, axis=1, keepdims=True)
        loss = jnp.broadcast_to(jnp.sum(row_loss, axis=0, keepdims=True), (1, 128))
        dy = diff * (1.0 / D_MODEL)
        dgain = jnp.sum(dy * xhat, axis=0, keepdims=True)
        dbias = jnp.sum(dy, axis=0, keepdims=True)
        dxh = dy * gain
        dr = rstd * (dxh - jnp.mean(dxh, axis=1, keepdims=True) - xhat * jnp.mean(dxh * xhat, axis=1, keepdims=True))
        return dr, loss, dgain, dbias

    dr, loss_part, g_ln_gain, g_ln_bias = _ew(
        f_norm, [(xs, "row", 0), (out, "row", 0), (target, "row", 0), (ln_gain, "vec", 0), (ln_bias, "vec", 0)],
        [(D_MODEL, F32, "row"), (128, F32, "acc"), (D_MODEL, F32, "acc"), (D_MODEL, F32, "acc")],
        rows=t, cw=D_MODEL, ncb=1, tr=256, name="norm_loss")

    def scatter_cols(g, cols):
        return g.reshape(g.shape[0], N_DEV, cols).transpose(1, 0, 2)

    gw_out = _mm(merged, dr, ta=True, out_dtype=WIRE_DTYPE, name="grad_w_out", tm=2048, tn=512, tk=2048)
    sent_out = _scatter_start(gw_out.reshape(N_DEV, row_out, D_MODEL), name="scatter_g_out_start")
    d_merged = _mm(dr, wout, tb=True, name="d_merged", tm=2048, tn=512, tk=2048, after=(sent_out[4],))

    def b_merge(dm, ps, pa, ls, la):
        gs, ga = _sigmoid(ls), _sigmoid(la)
        return dm * gs, dm * ga, dm * ps * gs * (1.0 - gs), dm * pa * ga * (1.0 - ga)

    dp_ssm, dp_attn, dgl_s, dgl_a = _ew(
        b_merge, [(d_merged, "row", 0), (p_ssm, "row", 0), (p_attn, "row", 0), (proj, "row", OFF_GS), (proj, "row", OFF_GA)],
        [(COL, MXU_DTYPE, "row")] * 4, rows=t, cw=COL, ncb=8, tr=1024, name="merge_bwd")
    gw_bs = _mm(h_ssm, dp_ssm, ta=True, out_dtype=WIRE_DTYPE, name="grad_w_branch_ssm", tm=1024, tn=512, tk=2048)
    gw_ba = _mm(h_attn, dp_attn, ta=True, out_dtype=WIRE_DTYPE, name="grad_w_branch_attn", tm=1024, tn=512, tk=2048)
    dh_ssm = _mm(dp_ssm, wbs, tb=True, name="d_h_ssm", tm=2048, tn=512, tk=2048)
    dh_attn = _mm(dp_attn, wba, tb=True, name="d_h_attn", tm=2048, tn=512, tk=2048)

    def b_hssm(dh, ga, gb, z):
        sg = _sigmoid(gb)
        dgate = dh * _silu(z)
        return dgate * sg, dgate * ga * sg * (1.0 - sg), dh * ga * sg * _silu_grad(z)

    dglu_a, dglu_b, dz_ssm = _ew(b_hssm, [(dh_ssm, "row", 0), (glu, "row", 0), (glu, "row", 4), (proj, "row", OFF_ZS)],
                                 [(COL, MXU_DTYPE, "row")] * 3, rows=t, cw=COL, ncb=4, tr=1024, name="ssm_gate_bwd")
    dglu = jnp.concatenate([dglu_a, dglu_b], axis=1)
    gw_glu = _mm(glu_in, dglu, ta=True, out_dtype=WIRE_DTYPE, name="grad_w_glu", tm=1024, tn=512, tk=2048)
    sent_three = _scatter_start(scatter_cols(jnp.concatenate([gw_glu, gw_bs, gw_ba], axis=0), col_br),
                                name="scatter_g_1024_start")
    dglu_in = _mm(dglu, wglu, tb=True, name="d_glu_in", tm=2048, tn=512, tk=2048, after=(sent_three[4],))

    def b_gelu(dgi, ys, u):
        dys = dgi * _gelu_grad(ys)
        return dys, jnp.sum(dys * u, axis=0, keepdims=True)

    dy_ssm, g_ssm_d = _ew(b_gelu, [(dglu_in, "row", 0), (y_ssm, "row", 0), (proj, "row", OFF_U)],
                          [(COL, F32, "row"), (COL, F32, "acc")],
                          rows=t, cw=COL, ncb=4, tr=1024, name="ssm_gelu_bwd")
    du_ssm, dmats = _ssm_backward(dy_ssm, proj, states, mats_mx)
    du = (du_ssm + d_skip * dy_ssm).astype(MXU_DTYPE)
    g_lre, g_lim, g_bre, g_bim, g_cre, g_cim, g_lstep = mats_vjp(dmats)

    dq, dz_attn, dka, dkb, dva, dvb, dbias, dsink = _attn_backward(proj, bias, sinks, attn, dh_attn)
    dk = _shift_blocks(dka, dkb).astype(MXU_DTYPE)
    dv = _shift_blocks(dva, dvb).astype(MXU_DTYPE)
    (g_table,) = bias_vjp(dbias)
    g_sinks = dsink[:, 0, 0]

    dproj = jnp.concatenate([du, dz_ssm, dq, dk, dv, dz_attn, dgl_s, dgl_a], axis=1)
    gw_in = _mm(x_mx, dproj, ta=True, out_dtype=WIRE_DTYPE, name="grad_w_in", tm=2048, tn=512, tk=2048)
    sent_in = _scatter_start(scatter_cols(gw_in, col_in), name="scatter_g_in_start")
    grad_x = _mm(dproj, win, tb=True, add=dr, add_scale=DEEPNORM_ALPHA, name="grad_x", tm=1024, tn=512, tk=4352,
                 after=(sent_in[4],))

    parts_out = _scatter_wait(sent_out, grad_x, name="scatter_g_out_wait")
    parts_three = _scatter_wait(sent_three, parts_out, name="scatter_g_1024_wait")
    parts_in = _scatter_wait(sent_in, parts_three, name="scatter_g_in_wait")

    o_in = _adamw_reduce(parts_in, w_in[0], m_w_in[0], v_w_in[0], name="adamw_w_in", tr=128)
    three_w = jnp.concatenate([w_glu[0], w_branch_ssm[0], w_branch_attn[0]], axis=0)
    three_m = jnp.concatenate([m_w_glu[0], m_w_branch_ssm[0], m_w_branch_attn[0]], axis=0)
    three_v = jnp.concatenate([v_w_glu[0], v_w_branch_ssm[0], v_w_branch_attn[0]], axis=0)
    o_three = _adamw_reduce(parts_three, three_w, three_m, three_v, name="adamw_w_1024", tr=512)
    o_out = _adamw_reduce(parts_out, w_out[0], m_w_out[0], v_w_out[0], name="adamw_w_out", tr=128)

    small_w = [ssm_lambda_re, ssm_lambda_im, ssm_b_re, ssm_b_im, ssm_c_re, ssm_c_im, ssm_d, ssm_log_step,
               attn_sinks, rel_bias_table, ln_gain, ln_bias]
    small_m = [m_ssm_lambda_re, m_ssm_lambda_im, m_ssm_b_re, m_ssm_b_im, m_ssm_c_re, m_ssm_c_im, m_ssm_d,
               m_ssm_log_step, m_attn_sinks, m_rel_bias_table, m_ln_gain, m_ln_bias]
    small_v = [v_ssm_lambda_re, v_ssm_lambda_im, v_ssm_b_re, v_ssm_b_im, v_ssm_c_re, v_ssm_c_im, v_ssm_d,
               v_ssm_log_step, v_attn_sinks, v_rel_bias_table, v_ln_gain, v_ln_bias]
    small_g = [g_lre, g_lim, g_bre, g_bim, g_cre, g_cim, g_ssm_d, g_lstep, g_sinks, g_table, g_ln_gain, g_ln_bias]
    parts_small = _all_gather(_pack(small_g), name="gather_g_small")
    sg = _unpack(_sum_parts(parts_small, name="sum_g_small"), small_w)
    updates = [_adamw_native(w, g, m, v, name="adamw_" + n)
               for n, w, g, m, v in zip(SMALL, small_w, sg, small_m, small_v)]
    sd, sm, sv = zip(*updates)

    loss = lax.psum(loss_part[0, 0], MESH_AXES)

    def big(o, idx):
        g_in_, g_three_, g_out_ = o_in[idx], o_three[idx], o_out[idx]
        return {"w_in": g_in_[None], "w_glu": g_three_[None, :D_SSM], "w_branch_ssm": g_three_[None, D_SSM:2 * D_SSM],
                "w_branch_attn": g_three_[None, 2 * D_SSM:], "w_out": g_out_[None]}

    order = ["w_in", "ssm_lambda_re", "ssm_lambda_im", "ssm_b_re", "ssm_b_im", "ssm_c_re", "ssm_c_im", "ssm_d",
             "ssm_log_step", "w_glu", "attn_sinks", "rel_bias_table", "w_branch_ssm", "w_branch_attn", "w_out",
             "ln_gain", "ln_bias"]
    outs = [loss, grad_x[None]]
    for idx, small in enumerate([sg, sd, sm, sv]):
        table = big(None, idx)
        table.update(dict(zip(SMALL, small)))
        outs += [table[n] for n in order]
    return tuple(outs)
```

```python
import functools
import math

import jax
import jax.numpy as jnp
from jax import lax
from jax.experimental import pallas as pl
from jax.experimental.pallas import tpu as pltpu

F32 = jnp.float32
MXU_DTYPE = jnp.bfloat16
WIRE_DTYPE = jnp.bfloat16

D_MODEL = 2048
D_SSM = 1024
SSM_GROUP = 16
N_GROUPS = 64
SSM_STATE = 64
N_Q_HEADS = 16
N_KV_HEADS = 4
Q_PER_KV = 4
HEAD_DIM = 64
D_ATTN = 1024
D_KV = 256
WINDOW = 128
BLOCK = 128
N_BUCKETS = 32
MAX_DISTANCE = 128
D_IN = 8704
DEEPNORM_ALPHA = 2.0 ** 0.25
LN_EPS = 1e-5
NEG_INF = -1e30
ATTN_SCALE = HEAD_DIM ** -0.5

ADAM_LR = 0.001
ADAM_B1 = 0.9
ADAM_B2 = 0.999
ADAM_EPS = 1e-08
ADAM_WD = 0.01
ADAM_STEP = 10

N_DEV = 8
SSM_CHUNK = 16
GROUP_BLOCK = 8
N_GB = N_GROUPS // GROUP_BLOCK
GB_LANES = GROUP_BLOCK * SSM_GROUP
GB_STATE = GROUP_BLOCK * SSM_STATE
COL = 256
OFF_U, OFF_ZS, OFF_Q, OFF_K, OFF_V, OFF_ZA, OFF_GS, OFF_GA = 0, 4, 8, 12, 13, 14, 18, 26

VMEM_CAP = 56 * 1024 * 1024
MESH_AXES = ("x", "y", "c")


def _vmem_limit(block_bytes):
    return int(min(max(3 * block_bytes, 16 * 1024 * 1024), VMEM_CAP))


def _nbytes(shape, dtype):
    return math.prod(shape) * jnp.dtype(dtype).itemsize


def _tile(n, pref):
    if n <= pref:
        return n
    t = (pref // 128) * 128
    while t >= 128:
        if n % t == 0:
            return t
        t -= 128
    return n


def _mm(a, b, *, name, ta=False, tb=False, out_dtype=F32, tm=1024, tn=512, tk=512, add=None, add_scale=1.0,
        after=()):
    squeeze = a.ndim == 2
    if squeeze:
        a, b = a[None], b[None]
        if add is not None:
            add = add[None]
    nb = a.shape[0]
    m, k = (a.shape[2], a.shape[1]) if ta else (a.shape[1], a.shape[2])
    n = b.shape[1] if tb else b.shape[2]
    tm, tn, tk = _tile(m, tm), _tile(n, tn), _tile(k, tk)
    nk = k // tk
    dn = (((0 if ta else 1,), (1 if tb else 0,)), ((), ()))

    a_spec = (pl.BlockSpec((None, tk, tm), lambda g, i, j, kk: (g, kk, i)) if ta
              else pl.BlockSpec((None, tm, tk), lambda g, i, j, kk: (g, i, kk)))
    b_spec = (pl.BlockSpec((None, tn, tk), lambda g, i, j, kk: (g, j, kk)) if tb
              else pl.BlockSpec((None, tk, tn), lambda g, i, j, kk: (g, kk, j)))
    o_spec = pl.BlockSpec((None, tm, tn), lambda g, i, j, kk: (g, i, j))
    in_specs = [a_spec, b_spec]
    operands = [a, b]
    if add is not None:
        in_specs.append(o_spec)
        operands.append(add)
    for tok in after:
        in_specs.append(pl.BlockSpec(memory_space=pl.ANY))
        operands.append(tok)
    n_in = len(operands)

    def body(*refs):
        a_ref, b_ref = refs[0], refs[1]
        add_ref = refs[2] if add is not None else None
        o_ref = refs[n_in]
        acc_ref = refs[-1]
        kk = pl.program_id(3)
        part = lax.dot_general(a_ref[...].astype(MXU_DTYPE), b_ref[...].astype(MXU_DTYPE), dn,
                               preferred_element_type=F32)

        def finish(r):
            if add_ref is not None:
                r = r + add_scale * add_ref[...]
            o_ref[...] = r.astype(out_dtype)

        if nk == 1:
            finish(part)
            return

        @pl.when(kk == 0)
        def _():
            acc_ref[...] = part

        @pl.when(jnp.logical_and(kk > 0, kk < nk - 1))
        def _():
            acc_ref[...] += part

        @pl.when(kk == nk - 1)
        def _():
            finish(acc_ref[...] + part)

    blocks = (_nbytes((tm, tk), a.dtype) + _nbytes((tk, tn), b.dtype) + _nbytes((tm, tn), out_dtype)
              + (_nbytes((tm, tn), F32) if add is not None else 0))
    out = pl.pallas_call(
        body,
        name=name,
        grid=(nb, m // tm, n // tn, nk),
        in_specs=in_specs,
        out_specs=o_spec,
        out_shape=jax.ShapeDtypeStruct((nb, m, n), out_dtype),
        scratch_shapes=[pltpu.VMEM((tm, tn), F32)],
        compiler_params=pltpu.CompilerParams(
            dimension_semantics=("parallel", "parallel", "parallel", "arbitrary"),
            vmem_limit_bytes=_vmem_limit(2 * blocks + 2 * _nbytes((tm, tn), F32))),
    )(*operands)
    return out[0] if squeeze else out


def _mm_fused(pairs, extras, out_dtypes, epilogue, *, name, tm, tn, after=()):
    m = pairs[0][0].shape[0]
    n = pairs[0][1].shape[0] if pairs[0][2] else pairs[0][1].shape[1]
    tm, tn = _tile(m, tm), _tile(n, tn)
    in_specs, operands, dns = [], [], []
    for a, b, tb in pairs:
        k = a.shape[1]
        in_specs += [pl.BlockSpec((tm, k), lambda i, j: (i, 0)),
                     pl.BlockSpec((tn, k), lambda i, j: (j, 0)) if tb else pl.BlockSpec((k, tn), lambda i, j: (0, j))]
        operands += [a, b]
        dns.append((((1,), (1 if tb else 0,)), ((), ())))

    def tile_at(col):
        return pl.BlockSpec((tm, tn), lambda i, j: (i, col + j))

    in_specs += [tile_at(col) for _, col in extras]
    operands += [arr for arr, _ in extras]
    in_specs += [pl.BlockSpec(memory_space=pl.ANY)] * len(after)
    operands += list(after)
    n_pairs, n_extra, n_in = len(pairs), len(extras), len(operands)

    def body(*refs):
        products = [lax.dot_general(refs[2 * p][...].astype(MXU_DTYPE), refs[2 * p + 1][...].astype(MXU_DTYPE),
                                    dns[p], preferred_element_type=F32) for p in range(n_pairs)]
        tiles = [refs[2 * n_pairs + e][...] for e in range(n_extra)]
        for o_ref, val in zip(refs[n_in:], epilogue(products, tiles)):
            o_ref[...] = val.astype(o_ref.dtype)

    blocks = sum(_nbytes((tm, a.shape[1]), a.dtype) + _nbytes((tn, a.shape[1]), b.dtype) for a, b, _ in pairs)
    blocks += sum(_nbytes((tm, tn), arr.dtype) for arr, _ in extras) + sum(_nbytes((tm, tn), dt) for dt in out_dtypes)
    return pl.pallas_call(
        body, name=name, grid=(m // tm, n // tn), in_specs=in_specs,
        out_specs=[tile_at(0)] * len(out_dtypes),
        out_shape=[jax.ShapeDtypeStruct((m, n), dt) for dt in out_dtypes],
        compiler_params=pltpu.CompilerParams(
            dimension_semantics=("parallel", "parallel"),
            vmem_limit_bytes=_vmem_limit(2 * blocks + n_pairs * _nbytes((tm, tn), F32))),
    )(*operands)


def _ew(fn, ins, outs, *, rows, cw, ncb, tr, name):
    tr = min(tr, rows)
    n_in = len(ins)

    def row_map(off):
        return lambda j, i: (i, off + j)

    def vec_map(off):
        return lambda j, i: (0, off + j)

    in_specs = []
    for arr, kind, off in ins:
        if kind == "row":
            in_specs.append(pl.BlockSpec((tr, cw), row_map(off)))
        else:
            in_specs.append(pl.BlockSpec((1, cw), vec_map(off)))
    out_specs, out_shapes = [], []
    for bw, dt, kind in outs:
        if kind == "row":
            out_specs.append(pl.BlockSpec((tr, bw), row_map(0)))
            out_shapes.append(jax.ShapeDtypeStruct((rows, ncb * bw), dt))
        else:
            out_specs.append(pl.BlockSpec((1, bw), vec_map(0)))
            out_shapes.append(jax.ShapeDtypeStruct((1, ncb * bw), F32))

    def body(*refs):
        i = pl.program_id(1)
        vals = fn(*[r[...] for r in refs[:n_in]])
        for r, (bw, dt, kind), v in zip(refs[n_in:], outs, vals):
            if kind == "row":
                r[...] = v.astype(dt)
            else:
                @pl.when(i == 0)
                def _(r=r):
                    r[...] = jnp.zeros_like(r)

                r[...] += v

    blocks = sum(_nbytes((tr, cw), a.dtype) for a, kind, _ in ins if kind == "row")
    blocks += sum(_nbytes((tr, bw), dt) for bw, dt, kind in outs if kind == "row")
    res = pl.pallas_call(
        body,
        name=name,
        grid=(ncb, rows // tr),
        in_specs=in_specs,
        out_specs=out_specs,
        out_shape=out_shapes,
        compiler_params=pltpu.CompilerParams(
            dimension_semantics=("parallel", "arbitrary"),
            vmem_limit_bytes=_vmem_limit(4 * blocks)),
    )(*[a for a, _, _ in ins])
    return res


def _sigmoid(x):
    return 1.0 / (1.0 + jnp.exp(-x))


INV_SQRT2 = 0.7071067811865476
INV_SQRT_2PI = 0.3989422804014327


def _gelu(x):
    return 0.5 * x * (1.0 + lax.erf(x * INV_SQRT2))


def _gelu_grad(x):
    return 0.5 * (1.0 + lax.erf(x * INV_SQRT2)) + x * INV_SQRT_2PI * jnp.exp(-0.5 * x * x)


def _silu(x):
    return x * _sigmoid(x)


def _silu_grad(x):
    s = _sigmoid(x)
    return s * (1.0 + x * (1.0 - s))


@jax.custom_vjp
def _taps_product(mr, mi, bbr, bbi):
    bbr_t, bbi_t = jnp.transpose(bbr, (0, 2, 1)), jnp.transpose(bbi, (0, 2, 1))
    return jnp.sum(mr[..., None, :] * bbr_t[None, :, None] - mi[..., None, :] * bbi_t[None, :, None], axis=-1)


def _taps_product_fwd(mr, mi, bbr, bbi):
    return _taps_product(mr, mi, bbr, bbi), (mr, mi, bbr, bbi)


def _taps_product_bwd(res, g):
    mr, mi, bbr, bbi = res
    hi = lax.Precision.HIGHEST
    return (jnp.einsum("tghk,gpk->tghp", g, bbr, precision=hi), -jnp.einsum("tghk,gpk->tghp", g, bbi, precision=hi),
            jnp.einsum("tghk,tghp->gpk", g, mr, precision=hi), -jnp.einsum("tghk,tghp->gpk", g, mi, precision=hi))


_taps_product.defvjp(_taps_product_fwd, _taps_product_bwd)


def _ssm_matrices(lam_re, lam_im, b_re, b_im, c_re, c_im, log_step):
    L = SSM_CHUNK
    step = jnp.exp(log_step)[:, None]
    ea, eb = lam_re * step, lam_im * step
    mag = jnp.exp(ea)
    lbr, lbi = mag * jnp.cos(eb), mag * jnp.sin(eb)
    den = lam_re * lam_re + lam_im * lam_im
    nr, ni = lbr - 1.0, lbi
    cr = (nr * lam_re + ni * lam_im) / den
    ci = (ni * lam_re - nr * lam_im) / den
    bbr = cr[..., None] * b_re - ci[..., None] * b_im
    bbi = cr[..., None] * b_im + ci[..., None] * b_re
    taus = jnp.arange(L + 1, dtype=F32)[:, None, None]
    pmag = jnp.exp(taus * ea[None])
    pwr, pwi = pmag * jnp.cos(taus * eb[None]), pmag * jnp.sin(taus * eb[None])
    mr = c_re[None] * pwr[:L, :, None, :] - c_im[None] * pwi[:L, :, None, :]
    mi = c_re[None] * pwi[:L, :, None, :] + c_im[None] * pwr[:L, :, None, :]
    kk = _taps_product(mr, mi, bbr, bbi)
    taps = jnp.transpose(kk.reshape(L, N_GB, GROUP_BLOCK, SSM_GROUP, SSM_GROUP), (1, 0, 2, 4, 3))
    taps = taps.reshape(N_GB, L, GB_LANES, SSM_GROUP)
    rev_r, rev_i = pwr[L - 1 - jnp.arange(L)], pwi[L - 1 - jnp.arange(L)]
    wer = rev_r[..., None] * bbr[None] - rev_i[..., None] * bbi[None]
    wei = rev_r[..., None] * bbi[None] + rev_i[..., None] * bbr[None]

    def rows_in(w):
        w = jnp.transpose(w.reshape(L, N_GB, GROUP_BLOCK, SSM_STATE, SSM_GROUP), (1, 0, 2, 4, 3))
        return w.reshape(N_GB, L * GB_LANES, SSM_STATE)

    wend = jnp.concatenate([rows_in(wer), rows_in(wei)], axis=2)
    m1r = c_re[None] * pwr[1:, :, None, :] - c_im[None] * pwi[1:, :, None, :]
    m1i = c_re[None] * pwi[1:, :, None, :] + c_im[None] * pwr[1:, :, None, :]

    def rows_out(m):
        m = jnp.transpose(m.reshape(L, N_GB, GROUP_BLOCK, SSM_GROUP, SSM_STATE), (1, 2, 4, 0, 3))
        return m.reshape(N_GB, GB_STATE, L * SSM_GROUP)

    wout = jnp.concatenate([rows_out(m1r), rows_out(-m1i)], axis=1)
    acat = jnp.concatenate([pwr[L].reshape(N_GB, 1, GB_STATE), pwi[L].reshape(N_GB, 1, GB_STATE)], axis=2)
    return taps, wend, wout, acat


def _lane_group(shape, axis, shift):
    return (lax.broadcasted_iota(jnp.int32, shape, axis) >> shift) & (GROUP_BLOCK - 1)


def _ssm_expand(taps, wend, wout):
    L = SSM_CHUNK
    taps = jnp.pad(taps, ((0, 0), (0, 0), (0, 0), (0, GB_LANES - SSM_GROUP)))

    def body(t_ref, we_ref, wo_ref, d_ref, web_ref, wob_ref):
        def rc(shape):
            return lax.broadcasted_iota(jnp.int32, shape, 0), lax.broadcasted_iota(jnp.int32, shape, 1)

        r, c = rc((GB_LANES, GB_LANES))
        spread = ((r < SSM_GROUP) & (r == (c & (SSM_GROUP - 1)))).astype(MXU_DTYPE)
        same = _lane_group((GB_LANES, GB_LANES), 0, 4) == _lane_group((GB_LANES, GB_LANES), 1, 4)
        for tau in range(L):
            full = jnp.dot(t_ref[tau].astype(MXU_DTYPE), spread, preferred_element_type=F32)
            d_ref[tau] = jnp.where(same, full, 0.0).astype(d_ref.dtype)
        r, c = rc((2 * SSM_STATE, STATE_W))
        part = (r >> 6) == (c >> 9)
        spread = (part & ((r & (SSM_STATE - 1)) == (c & (SSM_STATE - 1)))).astype(MXU_DTYPE)
        keep = _lane_group((GB_LANES, STATE_W), 0, 4) == _lane_group((GB_LANES, STATE_W), 1, 6)
        for j in range(L):
            rows = slice(j * GB_LANES, (j + 1) * GB_LANES)
            full = jnp.dot(we_ref[rows, :].astype(MXU_DTYPE), spread, preferred_element_type=F32)
            web_ref[rows, :] = jnp.where(keep, full, 0.0).astype(web_ref.dtype)
        r, c = rc((GB_LANES, GB_LANES))
        own = _lane_group((STATE_W, GB_LANES), 0, 6) == _lane_group((STATE_W, GB_LANES), 1, 4)
        for tt in range(L):
            half, k = tt // GROUP_BLOCK, tt % GROUP_BLOCK
            spread = (((r >> 4) == k) & ((r & (SSM_GROUP - 1)) == (c & (SSM_GROUP - 1)))).astype(MXU_DTYPE)
            src = wo_ref[:, half * GB_LANES:(half + 1) * GB_LANES].astype(MXU_DTYPE)
            full = jnp.dot(src, spread, preferred_element_type=F32)
            wob_ref[:, tt * GB_LANES:(tt + 1) * GB_LANES] = jnp.where(own, full, 0.0).astype(wob_ref.dtype)

    def spec(shape):
        return pl.BlockSpec((None,) + shape[1:], lambda b: (b,) + (0,) * (len(shape) - 1))

    out_shapes = [(N_GB, L, GB_LANES, GB_LANES), (N_GB, L * GB_LANES, STATE_W), (N_GB, STATE_W, L * GB_LANES)]
    return tuple(pl.pallas_call(
        body, name="ssm_expand", grid=(N_GB,),
        in_specs=[spec(taps.shape), spec(wend.shape), spec(wout.shape)],
        out_specs=[spec(s) for s in out_shapes],
        out_shape=[jax.ShapeDtypeStruct(s, MXU_DTYPE) for s in out_shapes],
        compiler_params=pltpu.CompilerParams(dimension_semantics=("parallel",), vmem_limit_bytes=VMEM_CAP),
    )(taps, wend, wout))


def _dot(a, b):
    return jnp.dot(a.astype(MXU_DTYPE), b.astype(MXU_DTYPE), preferred_element_type=F32)


def _dot_nt(a, b):
    return lax.dot_general(a.astype(MXU_DTYPE), b.astype(MXU_DTYPE), (((1,), (1,)), ((), ())),
                           preferred_element_type=F32)


def _dot_tn(a, b):
    return lax.dot_general(a.astype(MXU_DTYPE), b.astype(MXU_DTYPE), (((0,), (0,)), ((), ())),
                           preferred_element_type=F32)


STATE_W = 2 * GB_STATE


def _chunk_scan(e, acat):
    nc = e.shape[0]
    spec = pl.BlockSpec((nc, STATE_W), lambda b: (0, b))
    aspec = pl.BlockSpec((None, 1, STATE_W), lambda b: (b, 0, 0))

    def body(e_ref, a_ref, s_ref):
        a_r, a_i = a_ref[:, :GB_STATE], a_ref[:, GB_STATE:]

        def step(c, carry):
            s_r, s_i = carry
            s_ref[pl.ds(c, 1), :GB_STATE] = s_r
            s_ref[pl.ds(c, 1), GB_STATE:] = s_i
            e_r = e_ref[pl.ds(c, 1), :GB_STATE]
            e_i = e_ref[pl.ds(c, 1), GB_STATE:]
            return (a_r * s_r - a_i * s_i + e_r, a_r * s_i + a_i * s_r + e_i)

        zero = jnp.zeros((1, GB_STATE), F32)
        lax.fori_loop(0, nc, step, (zero, zero))

    return pl.pallas_call(
        body, name="ssm_chunk_scan", grid=(N_GB,),
        in_specs=[spec, aspec], out_specs=spec,
        out_shape=jax.ShapeDtypeStruct(e.shape, F32),
        compiler_params=pltpu.CompilerParams(dimension_semantics=("parallel",)),
    )(e, acat)


def _chunk_scan_bwd(ds, s, acat):
    nc = ds.shape[0]
    spec = pl.BlockSpec((nc, STATE_W), lambda b: (0, b))
    aspec = pl.BlockSpec((None, 1, STATE_W), lambda b: (b, 0, 0))

    def body(ds_ref, s_ref, a_ref, ge_ref, da_ref):
        a_r, a_i = a_ref[:, :GB_STATE], a_ref[:, GB_STATE:]

        def step(t, carry):
            g_r, g_i, d_r, d_i = carry
            c = nc - 1 - t
            ge_ref[pl.ds(c, 1), :GB_STATE] = g_r
            ge_ref[pl.ds(c, 1), GB_STATE:] = g_i
            s_r = s_ref[pl.ds(c, 1), :GB_STATE]
            s_i = s_ref[pl.ds(c, 1), GB_STATE:]
            d_r = d_r + g_r * s_r + g_i * s_i
            d_i = d_i + g_i * s_r - g_r * s_i
            n_r = ds_ref[pl.ds(c, 1), :GB_STATE] + a_r * g_r + a_i * g_i
            n_i = ds_ref[pl.ds(c, 1), GB_STATE:] + a_r * g_i - a_i * g_r
            return (n_r, n_i, d_r, d_i)

        zero = jnp.zeros((1, GB_STATE), F32)
        _, _, d_r, d_i = lax.fori_loop(0, nc, step, (zero, zero, zero, zero))
        da_ref[:, :GB_STATE] = d_r
        da_ref[:, GB_STATE:] = d_i

    return pl.pallas_call(
        body, name="ssm_chunk_scan_bwd", grid=(N_GB,),
        in_specs=[spec, spec, aspec], out_specs=[spec, aspec],
        out_shape=[jax.ShapeDtypeStruct(ds.shape, F32), jax.ShapeDtypeStruct(acat.shape, F32)],
        compiler_params=pltpu.CompilerParams(dimension_semantics=("parallel",)),
    )(ds, s, acat)


def _step_rows(ref, j, nc):
    return ref[pl.ds(j, nc, stride=SSM_CHUNK), :]


def _fold_lanes(z, widths):
    for w in widths:
        z = z + pltpu.roll(z, w, 1)
    return z


def _ssm_forward(proj, mats):
    dblk, wend, wout, acat = mats
    t = proj.shape[0]
    nc = t // SSM_CHUNK
    L = SSM_CHUNK
    lanes = pl.BlockSpec((t, GB_LANES), lambda b: (0, b))
    state = pl.BlockSpec((nc, STATE_W), lambda b: (0, b))

    def body_end(u_ref, w_ref, e_ref):
        x = jnp.concatenate([_step_rows(u_ref, j, nc).astype(MXU_DTYPE) for j in range(L)], axis=1)
        e_ref[...] = jnp.dot(x, w_ref[...], preferred_element_type=F32)

    e = pl.pallas_call(
        body_end, name="ssm_chunk_end", grid=(N_GB,),
        in_specs=[lanes, pl.BlockSpec((None, L * GB_LANES, STATE_W), lambda b: (b, 0, 0))],
        out_specs=state, out_shape=jax.ShapeDtypeStruct((nc, N_GB * STATE_W), F32),
        compiler_params=pltpu.CompilerParams(dimension_semantics=("parallel",), vmem_limit_bytes=VMEM_CAP),
    )(proj, wend)
    s = _chunk_scan(e, acat)

    def body_out(u_ref, d_ref, s_ref, w_ref, y_ref):
        xs = [_step_rows(u_ref, j, nc).astype(MXU_DTYPE) for j in range(L)]
        sb = s_ref[...].astype(MXU_DTYPE)
        for tt in range(L):
            xcat = jnp.concatenate(xs[:tt + 1], axis=1)
            taps = jnp.concatenate([d_ref[tt - j] for j in range(tt + 1)], axis=0).astype(MXU_DTYPE)
            y = (jnp.dot(xcat, taps, preferred_element_type=F32)
                 + jnp.dot(sb, w_ref[:, tt * GB_LANES:(tt + 1) * GB_LANES], preferred_element_type=F32))
            y_ref[pl.ds(tt, nc, stride=L), :] = y

    y = pl.pallas_call(
        body_out, name="ssm_chunk_out", grid=(N_GB,),
        in_specs=[lanes, pl.BlockSpec((None, L, GB_LANES, GB_LANES), lambda b: (b, 0, 0, 0)), state,
                  pl.BlockSpec((None, STATE_W, L * GB_LANES), lambda b: (b, 0, 0))],
        out_specs=lanes, out_shape=jax.ShapeDtypeStruct((t, D_SSM), F32),
        compiler_params=pltpu.CompilerParams(dimension_semantics=("parallel",), vmem_limit_bytes=VMEM_CAP),
    )(proj, dblk, s, wout)
    return y, s


def _ssm_backward(dy, proj, s, mats):
    dblk, wend, wout, acat = mats
    t = proj.shape[0]
    nc = t // SSM_CHUNK
    L = SSM_CHUNK
    lanes = pl.BlockSpec((t, GB_LANES), lambda b: (0, b))
    state = pl.BlockSpec((nc, STATE_W), lambda b: (0, b))
    taps_spec = pl.BlockSpec((None, L, GB_LANES, GB_LANES), lambda b: (b, 0, 0, 0))

    def body_state(dy_ref, w_ref, ds_ref):
        dyc = jnp.concatenate([_step_rows(dy_ref, tt, nc).astype(MXU_DTYPE) for tt in range(L)], axis=1)
        ds_ref[...] = _dot_nt(dyc, w_ref[...])

    ds = pl.pallas_call(
        body_state, name="ssm_bwd_state", grid=(N_GB,),
        in_specs=[lanes, pl.BlockSpec((None, STATE_W, L * GB_LANES), lambda b: (b, 0, 0))],
        out_specs=state, out_shape=jax.ShapeDtypeStruct((nc, N_GB * STATE_W), F32),
        compiler_params=pltpu.CompilerParams(dimension_semantics=("parallel",), vmem_limit_bytes=VMEM_CAP),
    )(dy, wout)
    ge, dacat = _chunk_scan_bwd(ds, s, acat)

    def body_in(u_ref, dy_ref, d_ref, ge_ref, w_ref, du_ref, dd_ref):
        xs = [_step_rows(u_ref, j, nc).astype(MXU_DTYPE) for j in range(L)]
        dys = [_step_rows(dy_ref, tt, nc).astype(MXU_DTYPE) for tt in range(L)]
        ge = ge_ref[...].astype(MXU_DTYPE)
        for i in range(L):
            dyc = jnp.concatenate(dys[i:], axis=1)
            taps = jnp.concatenate([d_ref[tt - i] for tt in range(i, L)], axis=1).astype(MXU_DTYPE)
            du_ref[pl.ds(i, nc, stride=L), :] = (
                _dot_nt(dyc, taps) + _dot_nt(ge, w_ref[i * GB_LANES:(i + 1) * GB_LANES, :]))
        for j in range(L):
            m = _dot_tn(xs[j], jnp.concatenate(dys[j:], axis=1))
            for tau in range(L - j):
                part = m[:, tau * GB_LANES:(tau + 1) * GB_LANES]
                if j == 0:
                    dd_ref[tau] = part
                else:
                    dd_ref[tau] += part
        same = _lane_group((GB_LANES, GB_LANES), 0, 4) == _lane_group((GB_LANES, GB_LANES), 1, 4)
        for tau in range(L):
            dd_ref[tau] = _fold_lanes(jnp.where(same, dd_ref[tau], 0.0), (64, 32, 16))

    du, ddblk = pl.pallas_call(
        body_in, name="ssm_bwd_in", grid=(N_GB,),
        in_specs=[lanes, lanes, taps_spec, state,
                  pl.BlockSpec((None, L * GB_LANES, STATE_W), lambda b: (b, 0, 0))],
        out_specs=[lanes, taps_spec],
        out_shape=[jax.ShapeDtypeStruct((t, D_SSM), F32), jax.ShapeDtypeStruct(dblk.shape, F32)],
        compiler_params=pltpu.CompilerParams(dimension_semantics=("parallel",), vmem_limit_bytes=VMEM_CAP),
    )(proj, dy, dblk, ge, wend)

    def body_w(u_ref, dy_ref, ge_ref, s_ref, dwe_ref, dwo_ref):
        ge = ge_ref[...].astype(MXU_DTYPE)
        sb = s_ref[...].astype(MXU_DTYPE)
        keep = _lane_group((GB_LANES, STATE_W), 0, 4) == _lane_group((GB_LANES, STATE_W), 1, 6)
        low = lax.broadcasted_iota(jnp.int32, (GB_LANES, 2 * SSM_STATE), 1) < SSM_STATE

        def fold_state(z):
            z = z[:, :GB_STATE // 2] + z[:, GB_STATE // 2:]
            z = z[:, :GB_STATE // 4] + z[:, GB_STATE // 4:]
            return _fold_lanes(z, (SSM_STATE,))

        for j in range(L):
            z = jnp.where(keep, _dot_tn(_step_rows(u_ref, j, nc).astype(MXU_DTYPE), ge), 0.0)
            dwe_ref[j * GB_LANES:(j + 1) * GB_LANES, :] = jnp.where(
                low, fold_state(z[:, :GB_STATE]), fold_state(z[:, GB_STATE:]))
        own = _lane_group((STATE_W, GB_LANES), 0, 6) == _lane_group((STATE_W, GB_LANES), 1, 4)
        chunk = lax.broadcasted_iota(jnp.int32, (STATE_W, GB_LANES), 1) >> 4
        for half in range(L // GROUP_BLOCK):
            acc = jnp.zeros((STATE_W, GB_LANES), F32)
            for k in range(GROUP_BLOCK):
                tt = half * GROUP_BLOCK + k
                z = jnp.where(own, _dot_tn(sb, _step_rows(dy_ref, tt, nc).astype(MXU_DTYPE)), 0.0)
                acc = acc + jnp.where(chunk == k, _fold_lanes(z, (64, 32, 16)), 0.0)
            dwo_ref[:, half * GB_LANES:(half + 1) * GB_LANES] = acc

    dwend, dwout = pl.pallas_call(
        body_w, name="ssm_bwd_w", grid=(N_GB,),
        in_specs=[lanes, lanes, state, state],
        out_specs=[pl.BlockSpec((None, L * GB_LANES, 2 * SSM_STATE), lambda b: (b, 0, 0)),
                   pl.BlockSpec((None, STATE_W, L * SSM_GROUP), lambda b: (b, 0, 0))],
        out_shape=[jax.ShapeDtypeStruct((N_GB, L * GB_LANES, 2 * SSM_STATE), F32),
                   jax.ShapeDtypeStruct((N_GB, STATE_W, L * SSM_GROUP), F32)],
        compiler_params=pltpu.CompilerParams(dimension_semantics=("parallel",), vmem_limit_bytes=VMEM_CAP),
    )(proj, dy, ge, s)
    return du, (ddblk[:, :, :, :SSM_GROUP], dwend, dwout, dacat)


def _t5_bucket(dist):
    max_exact = N_BUCKETS // 2
    is_small = dist < max_exact
    d = jnp.maximum(dist, 1).astype(F32)
    large = max_exact + (jnp.log(d / max_exact) / math.log(MAX_DISTANCE / max_exact)
                         * (N_BUCKETS - max_exact)).astype(jnp.int32)
    large = jnp.minimum(large, N_BUCKETS - 1)
    return jnp.where(is_small, dist, large)


def _band_bias(rel_bias_table):
    i = jnp.arange(BLOCK)[:, None]
    j = jnp.arange(BLOCK)[None, :]
    bucket = _t5_bucket(jnp.where(j > i, BLOCK + i - j, i - j))
    onehot = (bucket[:, :, None] == jnp.arange(N_BUCKETS)[None, None, :]).astype(F32)
    return jnp.einsum("qsb,bh->hqs", onehot, rel_bias_table, precision=lax.Precision.HIGHEST)


KV_PAIR = 2
HEADS_PER_STEP = KV_PAIR * Q_PER_KV
Q_LANES = HEADS_PER_STEP * HEAD_DIM
SLAB = 2 * HEAD_DIM
Q_COL0 = OFF_Q * COL // Q_LANES
K_COL0 = OFF_K * COL // SLAB
V_COL0 = OFF_V * COL // SLAB
ZA_COL0 = OFF_ZA * COL // Q_LANES


def _attn_specs():
    q_spec = pl.BlockSpec((BLOCK, Q_LANES), lambda m, n: (n, Q_COL0 + m))
    k_prev = pl.BlockSpec((BLOCK, SLAB), lambda m, n: (jnp.maximum(n - 1, 0), K_COL0 + m))
    k_cur = pl.BlockSpec((BLOCK, SLAB), lambda m, n: (n, K_COL0 + m))
    v_prev = pl.BlockSpec((BLOCK, SLAB), lambda m, n: (jnp.maximum(n - 1, 0), V_COL0 + m))
    v_cur = pl.BlockSpec((BLOCK, SLAB), lambda m, n: (n, V_COL0 + m))
    bias_spec = pl.BlockSpec((HEADS_PER_STEP, BLOCK, BLOCK), lambda m, n: (m, 0, 0))
    sink_spec = pl.BlockSpec(memory_space=pltpu.SMEM)
    wide = pl.BlockSpec((BLOCK, Q_LANES), lambda m, n: (n, m))
    gate = pl.BlockSpec((BLOCK, Q_LANES), lambda m, n: (n, ZA_COL0 + m))
    pair = pl.BlockSpec((BLOCK, SLAB), lambda m, n: (n, m))
    return [sink_spec, q_spec, k_prev, k_cur, v_prev, v_cur, bias_spec, gate], wide, pair


def _low_lanes(shape):
    return lax.broadcasted_iota(jnp.int32, shape, 1) < HEAD_DIM


def _pair_halves(ref):
    kb = ref[...]
    sw = pltpu.roll(kb, HEAD_DIM, 1)
    lo = _low_lanes(kb.shape)
    zero = jnp.zeros_like(kb)
    first = (jnp.where(lo, kb, zero).astype(MXU_DTYPE), jnp.where(lo, zero, sw).astype(MXU_DTYPE))
    second = (jnp.where(lo, sw, zero).astype(MXU_DTYPE), jnp.where(lo, zero, kb).astype(MXU_DTYPE))
    return first, second


def _fold_pair(acc):
    f = [x + pltpu.roll(x, HEAD_DIM, 1) for x in acc]
    return jnp.where(_low_lanes(f[0].shape), f[0], f[1])


def _from_prev(n):
    row = lax.broadcasted_iota(jnp.int32, (BLOCK, BLOCK), 0)
    col = lax.broadcasted_iota(jnp.int32, (BLOCK, BLOCK), 1)
    prev = col > row
    return prev, jnp.where(jnp.logical_and(n == 0, prev), NEG_INF, 0.0)


def _softmax_sink(s, sink):
    m = jnp.maximum(jnp.max(s, axis=1, keepdims=True), sink)
    e = jnp.exp(s - m)
    es = jnp.exp(sink - m)
    inv = 1.0 / (jnp.sum(e, axis=1, keepdims=True) + es)
    return e * inv, es * inv


def _stack_pair(prev_halves, own_halves, a):
    return jnp.concatenate([prev_halves[a][0], prev_halves[a][1], own_halves[a][0], own_halves[a][1]], axis=0)


def _split_heads(x4, prev):
    return [jnp.where(prev, x4[:, e * BLOCK:(e + 1) * BLOCK], x4[:, (2 + e) * BLOCK:(3 + e) * BLOCK]) for e in range(2)]


def _spread_heads(x, prev):
    return jnp.concatenate([jnp.where(prev, x[0], 0.0), jnp.where(prev, x[1], 0.0),
                            jnp.where(prev, 0.0, x[0]), jnp.where(prev, 0.0, x[1])], axis=1)


def _attn_forward(proj, bias, sinks):
    t = proj.shape[0]
    in_specs, wide, _ = _attn_specs()

    def body(sink_ref, q_ref, kp_ref, kc_ref, vp_ref, vc_ref, bias_ref, z_ref, o_ref, h_ref):
        m, n = pl.program_id(0), pl.program_id(1)
        kp, kc, vp, vc = (_pair_halves(r) for r in (kp_ref, kc_ref, vp_ref, vc_ref))
        keys = [_stack_pair(kp, kc, a) for a in range(KV_PAIR)]
        vals = [_stack_pair(vp, vc, a) for a in range(KV_PAIR)]
        prev, edge = _from_prev(n)
        prev2 = jnp.concatenate([prev, prev], axis=0)
        for a in range(KV_PAIR):
            slabs = (2 * a, 2 * a + 1)
            q = jnp.concatenate([q_ref[:, s * SLAB:(s + 1) * SLAB] for s in slabs], axis=0).astype(MXU_DTYPE)
            logits = _split_heads(_dot_nt(q, keys[a]), prev2)
            probs = []
            for e in range(2):
                rows = [_softmax_sink(logits[e][r * BLOCK:(r + 1) * BLOCK] * ATTN_SCALE + bias_ref[2 * s + e] + edge,
                                      sink_ref[m * HEADS_PER_STEP + 2 * s + e])[0] for r, s in enumerate(slabs)]
                probs.append(jnp.concatenate(rows, axis=0))
            out = _dot(_spread_heads(probs, prev2), vals[a])
            for r, s in enumerate(slabs):
                cols = slice(s * SLAB, (s + 1) * SLAB)
                o_ref[:, cols] = out[r * BLOCK:(r + 1) * BLOCK]
                h_ref[:, cols] = (out[r * BLOCK:(r + 1) * BLOCK] * _silu(z_ref[:, cols])).astype(h_ref.dtype)

    return pl.pallas_call(
        body, name="attn_fwd", grid=(N_KV_HEADS // KV_PAIR, t // BLOCK),
        in_specs=in_specs, out_specs=[wide, wide],
        out_shape=[jax.ShapeDtypeStruct((t, D_ATTN), F32), jax.ShapeDtypeStruct((t, D_ATTN), MXU_DTYPE)],
        compiler_params=pltpu.CompilerParams(dimension_semantics=("parallel", "arbitrary")),
    )(sinks, proj, proj, proj, proj, proj, bias, proj)


def _attn_backward(proj, bias, sinks, attn, dh):
    t = proj.shape[0]
    in_specs, wide, pair = _attn_specs()
    bias_spec = in_specs[-2]
    sink_out = pl.BlockSpec((HEADS_PER_STEP, 8, 128), lambda m, n: (m, 0, 0))

    def body(sink_ref, q_ref, kp_ref, kc_ref, vp_ref, vc_ref, bias_ref, z_ref, o_ref, dh_ref,
             dq_ref, dz_ref, dka_ref, dkb_ref, dva_ref, dvb_ref, dbias_ref, dsink_ref):
        m, n = pl.program_id(0), pl.program_id(1)

        @pl.when(n == 0)
        def _():
            dbias_ref[...] = jnp.zeros_like(dbias_ref)
            dsink_ref[...] = jnp.zeros_like(dsink_ref)

        kp, kc, vp, vc = (_pair_halves(r) for r in (kp_ref, kc_ref, vp_ref, vc_ref))
        keys = [_stack_pair(kp, kc, a) for a in range(KV_PAIR)]
        vals = [_stack_pair(vp, vc, a) for a in range(KV_PAIR)]
        prev, edge = _from_prev(n)
        lo = _low_lanes((BLOCK, SLAB))
        prev2 = jnp.concatenate([prev, prev], axis=0)
        dk = [[None] * KV_PAIR for _ in range(2)]
        dv = [[None] * KV_PAIR for _ in range(2)]
        for a in range(KV_PAIR):
            slabs = (2 * a, 2 * a + 1)
            q = jnp.concatenate([q_ref[:, s * SLAB:(s + 1) * SLAB] for s in slabs], axis=0).astype(MXU_DTYPE)
            gated = []
            for s in slabs:
                cols = slice(s * SLAB, (s + 1) * SLAB)
                dh_s, z_s = dh_ref[:, cols], z_ref[:, cols]
                dz_ref[:, cols] = (dh_s * o_ref[:, cols] * _silu_grad(z_s)).astype(dz_ref.dtype)
                gated.append(dh_s * _silu(z_s))
            do = jnp.concatenate(gated, axis=0).astype(MXU_DTYPE)
            logits = _split_heads(_dot_nt(q, keys[a]), prev2)
            dprobs = _split_heads(_dot_nt(do, vals[a]), prev2)
            probs, dlogits = [], []
            for e in range(2):
                p_rows, ds_rows = [], []
                for r, s in enumerate(slabs):
                    h = 2 * s + e
                    rows = slice(r * BLOCK, (r + 1) * BLOCK)
                    p, ps = _softmax_sink(logits[e][rows] * ATTN_SCALE + bias_ref[h] + edge,
                                          sink_ref[m * HEADS_PER_STEP + h])
                    delta = jnp.sum(p * dprobs[e][rows], axis=1, keepdims=True)
                    ds = p * (dprobs[e][rows] - delta)
                    dbias_ref[h] += ds
                    dsink_ref[h] += jnp.broadcast_to(jnp.sum(-ps * delta, axis=0, keepdims=True), (8, 128))
                    p_rows.append(p)
                    ds_rows.append(ds)
                probs.append(jnp.concatenate(p_rows, axis=0))
                dlogits.append(jnp.concatenate(ds_rows, axis=0))
            ds4 = _spread_heads(dlogits, prev2).astype(MXU_DTYPE)
            p4 = _spread_heads(probs, prev2).astype(MXU_DTYPE)
            dq = _dot(ds4, keys[a]) * ATTN_SCALE
            for r, s in enumerate(slabs):
                dq_ref[:, s * SLAB:(s + 1) * SLAB] = dq[r * BLOCK:(r + 1) * BLOCK].astype(dq_ref.dtype)
            rk = _dot_tn(ds4, q)
            rv = _dot_tn(p4, do)
            for which in range(2):
                top = 2 * which * BLOCK
                dk[which][a] = jnp.where(lo, rk[top:top + BLOCK], rk[top + BLOCK:top + 2 * BLOCK])
                dv[which][a] = jnp.where(lo, rv[top:top + BLOCK], rv[top + BLOCK:top + 2 * BLOCK])
        dkb_ref[...] = _fold_pair(dk[0]) * ATTN_SCALE
        dka_ref[...] = _fold_pair(dk[1]) * ATTN_SCALE
        dvb_ref[...] = _fold_pair(dv[0])
        dva_ref[...] = _fold_pair(dv[1])

    kv_shape = jax.ShapeDtypeStruct((t, D_KV), F32)
    return pl.pallas_call(
        body, name="attn_bwd", grid=(N_KV_HEADS // KV_PAIR, t // BLOCK),
        in_specs=in_specs + [wide, wide],
        out_specs=[wide, wide, pair, pair, pair, pair, bias_spec, sink_out],
        out_shape=[jax.ShapeDtypeStruct((t, D_ATTN), MXU_DTYPE), jax.ShapeDtypeStruct((t, D_ATTN), MXU_DTYPE),
                   kv_shape, kv_shape, kv_shape, kv_shape,
                   jax.ShapeDtypeStruct(bias.shape, F32), jax.ShapeDtypeStruct((N_Q_HEADS, 8, 128), F32)],
        compiler_params=pltpu.CompilerParams(dimension_semantics=("parallel", "arbitrary")),
    )(sinks, proj, proj, proj, proj, proj, bias, proj, attn, dh)


def _shift_blocks(cur, prev):
    return cur + jnp.concatenate([prev[BLOCK:], jnp.zeros_like(prev[:BLOCK])], axis=0)


def _mesh_pos():
    return lax.axis_index("x"), lax.axis_index("y"), lax.axis_index("c")


def _all_gather(x, *, name):
    def body(x_ref, out_ref, send_sems, recv_sems, local_sem):
        x, y, c = _mesh_pos()
        me, sibling = (x, y, c), (x, y, 1 - c)
        chips = [(1 - x, y), (x, 1 - y), (1 - x, 1 - y)]

        def slot(px, py, pc):
            return out_ref.at[4 * px + 2 * py + pc]

        def copy(k, block, to, src=None):
            return pltpu.make_async_remote_copy(
                src_ref=slot(*block) if src is None else src, dst_ref=slot(*block),
                send_sem=send_sems.at[k], recv_sem=recv_sems.at[k],
                device_id=to, device_id_type=pl.DeviceIdType.MESH)

        mine = pltpu.make_async_copy(x_ref, slot(*me), local_sem)
        mine.start()
        first = [copy(0, me, sibling, src=x_ref)]
        first += [copy(1 + j, me, (*chip, c), src=x_ref) for j, chip in enumerate(chips)]
        for cp in first:
            cp.start()
        passed = [copy(4 + j, (*chip, c), sibling) for j, chip in enumerate(chips)]
        for j, chip in enumerate(chips):
            copy(1 + j, (*chip, c), me).wait_recv()
            passed[j].start()
        copy(0, sibling, me).wait_recv()
        for j, chip in enumerate(chips):
            copy(4 + j, (*chip, 1 - c), me).wait_recv()
        for cp in first + passed:
            cp.wait_send()
        mine.wait()

    return pl.pallas_call(
        body, name=name,
        in_specs=[pl.BlockSpec(memory_space=pl.ANY)],
        out_specs=pl.BlockSpec(memory_space=pl.ANY),
        out_shape=jax.ShapeDtypeStruct((N_DEV,) + x.shape, x.dtype),
        scratch_shapes=[pltpu.SemaphoreType.DMA((7,)), pltpu.SemaphoreType.DMA((7,)), pltpu.SemaphoreType.DMA],
    )(x)


_HBM = pl.BlockSpec(memory_space=pltpu.HBM)
_SEM = pl.BlockSpec(memory_space=pltpu.SEMAPHORE)
_DATAFLOW = pltpu.SideEffectType.DATAFLOW_SIDE_EFFECTING


def _peers():
    x, y, c = _mesh_pos()
    others = []
    for k in range(1, N_DEV):
        px, py, pc = x ^ (k >> 2), y ^ ((k >> 1) & 1), c ^ (k & 1)
        others.append(((px, py, pc), 4 * px + 2 * py + pc))
    return 4 * x + 2 * y + c, others


def _split_start(bufs, plan, n_copies, *, name):
    nb = len(bufs)

    def body(*refs):
        send_sems, recv_sems, token = refs[nb], refs[nb + 1], refs[-1]
        for k, (src, dst, pos, _) in enumerate(plan(*refs[:nb])):
            pltpu.make_async_remote_copy(src_ref=src, dst_ref=dst, send_sem=send_sems.at[k], recv_sem=recv_sems.at[k],
                                         device_id=pos, device_id_type=pl.DeviceIdType.MESH).start()
        token[...] = jnp.zeros_like(token)

    return pl.pallas_call(
        body, name=name,
        out_shape=(pltpu.SemaphoreType.DMA((n_copies,)), pltpu.SemaphoreType.DMA((n_copies,)),
                   *[pltpu.HBM(b.shape, b.dtype) for b in bufs], jax.ShapeDtypeStruct((8, 128), F32)),
        in_specs=(_HBM,) * nb, out_specs=(_SEM, _SEM) + (_HBM,) * nb + (pl.BlockSpec(memory_space=pltpu.VMEM),),
        input_output_aliases={i: 2 + i for i in range(nb)},
        compiler_params=pltpu.CompilerParams(has_side_effects=_DATAFLOW),
    )(*[pltpu.with_memory_space_constraint(b, pltpu.HBM) for b in bufs])


def _split_wait(started, plan, after, *, name):
    send_sems, recv_sems, *thru = started[:-1]
    nb = len(thru)

    def body(*refs):
        send_sems, recv_sems = refs[nb], refs[nb + 1]
        for k, (src, _, pos, arrive) in enumerate(plan(*refs[:nb])):
            copy = pltpu.make_async_remote_copy(
                src_ref=src, dst_ref=arrive, send_sem=send_sems.at[k], recv_sem=recv_sems.at[k],
                device_id=pos, device_id_type=pl.DeviceIdType.MESH)
            copy.wait_send()
            copy.wait_recv()

    return pl.pallas_call(
        body, name=name,
        out_shape=tuple(pltpu.HBM(b.shape, b.dtype) for b in thru),
        in_specs=(_HBM,) * nb + (_SEM, _SEM, pl.BlockSpec(memory_space=pl.ANY)), out_specs=(_HBM,) * nb,
        input_output_aliases={i: i for i in range(nb)},
        compiler_params=pltpu.CompilerParams(has_side_effects=_DATAFLOW),
    )(*thru, send_sems, recv_sems, after)


def _plan_scatter(x_ref, land_ref):
    me, others = _peers()
    return [(x_ref.at[idx], land_ref.at[me], pos, land_ref.at[idx]) for pos, idx in others]


def _plan_gather(x_ref, land_ref):
    me, others = _peers()
    return [(x_ref, land_ref.at[me], pos, land_ref.at[idx]) for pos, idx in others]


def _near_and_far():
    x, y, c = _mesh_pos()
    chips = [(1 - x, y), (x, 1 - y), (1 - x, 1 - y)]
    near = [(x, y, 1 - c)] + [(px, py, c) for px, py in chips]
    relay = [(4 * px + 2 * py + c, 4 * px + 2 * py + 1 - c) for px, py in chips]
    return 4 * x + 2 * y + c, near, (x, y, 1 - c), relay


def _plan_gather_near(x_ref, land_ref):
    me, near, _, _ = _near_and_far()
    return [(x_ref, land_ref.at[me], pos, land_ref.at[4 * pos[0] + 2 * pos[1] + pos[2]]) for pos in near]


def _plan_gather_relay(land_ref):
    _, _, sibling, relay = _near_and_far()
    return [(land_ref.at[mine], land_ref.at[mine], sibling, land_ref.at[theirs]) for mine, theirs in relay]


def _landing_zone(own):
    me, _ = _peers()
    return lax.dynamic_update_index_in_dim(lax.empty((N_DEV,) + own.shape, own.dtype), own, me, 0)


def _scatter_start(x, *, name):
    if x.ndim == 2:
        return _split_start((x, _landing_zone(x)), _plan_gather, N_DEV - 1, name=name)
    me, _ = _peers()
    own = lax.dynamic_index_in_dim(x, me, 0, keepdims=False)
    return _split_start((x, _landing_zone(own)), _plan_scatter, N_DEV - 1, name=name)


def _scatter_wait(started, after, *, name):
    plan = _plan_gather if started[2].ndim == 2 else _plan_scatter
    return _split_wait(started, plan, after, name=name)[1]


def _adamw_math(w, g, m, v):
    m = ADAM_B1 * m + (1.0 - ADAM_B1) * g
    v = ADAM_B2 * v + (1.0 - ADAM_B2) * (g * g)
    m_hat = m / (1.0 - ADAM_B1 ** ADAM_STEP)
    v_hat = v / (1.0 - ADAM_B2 ** ADAM_STEP)
    delta = -ADAM_LR * (m_hat / (jnp.sqrt(v_hat) + ADAM_EPS) + ADAM_WD * w)
    return delta, m, v


def _adamw_reduce(parts, w, m, v, *, name, tr):
    r, c = w.shape
    tr = min(tr, r)
    spec = pl.BlockSpec((tr, c), lambda i: (i, 0))

    def body(p_ref, w_ref, m_ref, v_ref, g_ref, d_ref, nm_ref, nv_ref):
        g = p_ref[0].astype(F32)
        for s in range(1, N_DEV):
            g = g + p_ref[s].astype(F32)
        delta, nm, nv = _adamw_math(w_ref[...], g, m_ref[...], v_ref[...])
        g_ref[...] = g
        d_ref[...] = delta
        nm_ref[...] = nm
        nv_ref[...] = nv

    return pl.pallas_call(
        body, name=name, grid=(r // tr,),
        in_specs=[pl.BlockSpec((N_DEV, tr, c), lambda i: (0, i, 0)), spec, spec, spec],
        out_specs=[spec] * 4,
        out_shape=[jax.ShapeDtypeStruct((r, c), F32)] * 4,
        compiler_params=pltpu.CompilerParams(
            dimension_semantics=("parallel",),
            vmem_limit_bytes=_vmem_limit(2 * 15 * _nbytes((tr, c), F32))),
    )(parts, w, m, v)


def _sum_parts(parts, *, name):
    def body(p_ref, o_ref):
        g = p_ref[0]
        for s in range(1, N_DEV):
            g = g + p_ref[s]
        o_ref[...] = g

    return pl.pallas_call(
        body, name=name, out_shape=jax.ShapeDtypeStruct(parts.shape[1:], F32),
        compiler_params=pltpu.CompilerParams(vmem_limit_bytes=_vmem_limit(_nbytes(parts.shape, F32))),
    )(parts)


def _adamw_native(w, g, m, v, *, name):
    def body(w_ref, g_ref, m_ref, v_ref, d_ref, nm_ref, nv_ref):
        d_ref[...], nm_ref[...], nv_ref[...] = _adamw_math(w_ref[...], g_ref[...], m_ref[...], v_ref[...])

    return pl.pallas_call(body, name=name, out_shape=[jax.ShapeDtypeStruct(w.shape, F32)] * 3)(w, g, m, v)


SMALL = ["ssm_lambda_re", "ssm_lambda_im", "ssm_b_re", "ssm_b_im", "ssm_c_re", "ssm_c_im", "ssm_d",
         "ssm_log_step", "attn_sinks", "rel_bias_table", "ln_gain", "ln_bias"]


def _pack(arrs):
    flat = jnp.concatenate([a.reshape(-1) for a in arrs])
    pad = (-flat.shape[0]) % 1024
    return jnp.pad(flat, (0, pad)).reshape(-1, 128)


def _unpack(packed, like):
    flat = packed.reshape(-1)
    out, pos = [], 0
    for a in like:
        out.append(flat[pos:pos + a.size].reshape(a.shape))
        pos += a.size
    return out


def kernel(x, w_in, ssm_lambda_re, ssm_lambda_im, ssm_b_re, ssm_b_im, ssm_c_re, ssm_c_im, ssm_d, ssm_log_step, w_glu, attn_sinks, rel_bias_table, w_branch_ssm, w_branch_attn, w_out, ln_gain, ln_bias, loss_target, m_w_in, m_ssm_lambda_re, m_ssm_lambda_im, m_ssm_b_re, m_ssm_b_im, m_ssm_c_re, m_ssm_c_im, m_ssm_d, m_ssm_log_step, m_w_glu, m_attn_sinks, m_rel_bias_table, m_w_branch_ssm, m_w_branch_attn, m_w_out, m_ln_gain, m_ln_bias, v_w_in, v_ssm_lambda_re, v_ssm_lambda_im, v_ssm_b_re, v_ssm_b_im, v_ssm_c_re, v_ssm_c_im, v_ssm_d, v_ssm_log_step, v_w_glu, v_attn_sinks, v_rel_bias_table, v_w_branch_ssm, v_w_branch_attn, v_w_out, v_ln_gain, v_ln_bias):
    t = x.shape[1]
    xs = x[0]
    target = loss_target[0]
    col_in = w_in.shape[2]
    col_br = w_glu.shape[2]
    row_out = w_out.shape[1]

    w_in_mx = w_in[0].astype(MXU_DTYPE)
    near = _split_start((w_in_mx, _landing_zone(w_in_mx)), _plan_gather_near, 4, name="gather_w_in_near_start")
    ssm_params = (ssm_lambda_re[0], ssm_lambda_im[0], ssm_b_re[0], ssm_b_im[0], ssm_c_re[0], ssm_c_im[0],
                  ssm_log_step[0] + near[-1][0, 0])
    mats, mats_vjp = jax.vjp(_ssm_matrices, *ssm_params)
    mats_mx = _ssm_expand(*mats[:3]) + (mats[3],)
    sinks = attn_sinks[0]
    d_skip = ssm_d
    _, landed = _split_wait(near, _plan_gather_near, mats_mx[1], name="gather_w_in_near_wait")
    relay = _split_start((landed,), _plan_gather_relay, 3, name="gather_w_in_relay_start")
    bias, bias_vjp = jax.vjp(_band_bias, rel_bias_table + relay[-1][0, 0])
    x_mx = xs.astype(MXU_DTYPE)
    (g_in,) = _split_wait(relay, _plan_gather_relay, bias, name="gather_w_in_relay_wait")
    win = g_in.transpose(1, 0, 2).reshape(D_MODEL, D_IN)
    hold = g_in[0, 0, 0] * 0
    three = jnp.concatenate([w_glu[0], w_branch_ssm[0], w_branch_attn[0]], axis=0).astype(MXU_DTYPE)
    sent_three = _scatter_start(three + hold, name="gather_w_1024_start")
    sent_wout = _scatter_start(w_out[0].astype(MXU_DTYPE) + hold, name="gather_w_out_start")

    proj = _mm(x_mx, win, name="in_proj", tm=2048, tn=512, tk=2048, after=(sent_three[4], sent_wout[4]))
    y_conv, states = _ssm_forward(proj, mats_mx)

    def f_gelu(yv, u, d):
        ys = yv + d * u
        return ys, _gelu(ys)

    y_ssm, glu_in = _ew(f_gelu, [(y_conv, "row", 0), (proj, "row", OFF_U), (d_skip, "vec", 0)],
                        [(COL, F32, "row"), (COL, MXU_DTYPE, "row")], rows=t, cw=COL, ncb=4, tr=1024, name="ssm_gelu")
    g_three = _scatter_wait(sent_three, glu_in, name="gather_w_1024_wait")
    three_full = g_three.transpose(1, 0, 2).reshape(3 * D_SSM, N_DEV * col_br)
    wglu, wbs, wba = three_full[:D_SSM], three_full[D_SSM:2 * D_SSM], three_full[2 * D_SSM:]
    glu = _mm(glu_in, wglu, name="glu_proj", tm=2048, tn=512, tk=1024)

    def f_hssm(ga, gb, z):
        return (ga * _sigmoid(gb) * _silu(z),)

    (h_ssm,) = _ew(f_hssm, [(glu, "row", 0), (glu, "row", 4), (proj, "row", OFF_ZS)],
                   [(COL, MXU_DTYPE, "row")], rows=t, cw=COL, ncb=4, tr=1024, name="ssm_gate")

    attn, h_attn = _attn_forward(proj, bias, sinks)
    gate_tn = 512
    gate_cols = [(proj, OFF_GS * COL // gate_tn), (proj, OFF_GA * COL // gate_tn)]

    def f_merge(products, gates):
        (ps, pa), (ls, la) = products, gates
        return ps, pa, _sigmoid(ls) * ps + _sigmoid(la) * pa

    p_ssm, p_attn, merged = _mm_fused([(h_ssm, wbs, False), (h_attn, wba, False)], gate_cols, [F32, F32, MXU_DTYPE],
                                      f_merge, name="branch_merge", tm=1024, tn=gate_tn)
    wout = _scatter_wait(sent_wout, merged, name="gather_w_out_wait").reshape(D_MODEL, D_MODEL)
    out = _mm(merged, wout, name="out_proj", tm=2048, tn=512, tk=2048)

    def f_norm(xv, ov, tg, gain, lbias):
        r = DEEPNORM_ALPHA * xv + ov
        mu = jnp.mean(r, axis=1, keepdims=True)
        cen = r - mu
        var = jnp.mean(cen * cen, axis=1, keepdims=True)
        rstd = lax.rsqrt(var + LN_EPS)
        xhat = cen * rstd
        yv = xhat * gain + lbias
        diff = yv - tg
        row_loss = 0.5 * jnp.mean(diff * diff, axis=1, keepdims=True)
        loss = jnp.broadcast_to(jnp.sum(row_loss, axis=0, keepdims=True), (1, 128))
        dy = diff * (1.0 / D_MODEL)
        dgain = jnp.sum(dy * xhat, axis=0, keepdims=True)
        dbias = jnp.sum(dy, axis=0, keepdims=True)
        dxh = dy * gain
        dr = rstd * (dxh - jnp.mean(dxh, axis=1, keepdims=True) - xhat * jnp.mean(dxh * xhat, axis=1, keepdims=True))
        return dr, loss, dgain, dbias

    dr, loss_part, g_ln_gain, g_ln_bias = _ew(
        f_norm, [(xs, "row", 0), (out, "row", 0), (target, "row", 0), (ln_gain, "vec", 0), (ln_bias, "vec", 0)],
        [(D_MODEL, F32, "row"), (128, F32, "acc"), (D_MODEL, F32, "acc"), (D_MODEL, F32, "acc")],
        rows=t, cw=D_MODEL, ncb=1, tr=256, name="norm_loss")

    def scatter_cols(g, cols):
        return g.reshape(g.shape[0], N_DEV, cols).transpose(1, 0, 2)

    gw_out = _mm(merged, dr, ta=True, out_dtype=WIRE_DTYPE, name="grad_w_out", tm=2048, tn=512, tk=2048)
    sent_out = _scatter_start(gw_out.reshape(N_DEV, row_out, D_MODEL), name="scatter_g_out_start")
    def b_merge(products, tiles):
        (dm,), (ps, pa, ls, la) = products, tiles
        gs, ga = _sigmoid(ls), _sigmoid(la)
        return dm * gs, dm * ga, dm * ps * gs * (1.0 - gs), dm * pa * ga * (1.0 - ga)

    dp_ssm, dp_attn, dgl_s, dgl_a = _mm_fused(
        [(dr, wout, True)], [(p_ssm, 0), (p_attn, 0)] + gate_cols, [MXU_DTYPE] * 4, b_merge,
        name="merge_bwd", tm=512, tn=gate_tn, after=(sent_out[4],))
    gw_bs = _mm(h_ssm, dp_ssm, ta=True, out_dtype=WIRE_DTYPE, name="grad_w_branch_ssm", tm=1024, tn=512, tk=2048)
    gw_ba = _mm(h_attn, dp_attn, ta=True, out_dtype=WIRE_DTYPE, name="grad_w_branch_attn", tm=1024, tn=512, tk=2048)
    dh_ssm = _mm(dp_ssm, wbs, tb=True, name="d_h_ssm", tm=2048, tn=512, tk=2048)
    dh_attn = _mm(dp_attn, wba, tb=True, name="d_h_attn", tm=2048, tn=512, tk=2048)

    def b_hssm(dh, ga, gb, z):
        sg = _sigmoid(gb)
        dgate = dh * _silu(z)
        return dgate * sg, dgate * ga * sg * (1.0 - sg), dh * ga * sg * _silu_grad(z)

    dglu_a, dglu_b, dz_ssm = _ew(b_hssm, [(dh_ssm, "row", 0), (glu, "row", 0), (glu, "row", 4), (proj, "row", OFF_ZS)],
                                 [(COL, MXU_DTYPE, "row")] * 3, rows=t, cw=COL, ncb=4, tr=1024, name="ssm_gate_bwd")
    dglu = jnp.concatenate([dglu_a, dglu_b], axis=1)
    gw_glu = _mm(glu_in, dglu, ta=True, out_dtype=WIRE_DTYPE, name="grad_w_glu", tm=1024, tn=512, tk=2048)
    sent_three = _scatter_start(scatter_cols(jnp.concatenate([gw_glu, gw_bs, gw_ba], axis=0), col_br),
                                name="scatter_g_1024_start")
    dglu_in = _mm(dglu, wglu, tb=True, name="d_glu_in", tm=2048, tn=512, tk=2048, after=(sent_three[4],))

    def b_gelu(dgi, ys, u):
        dys = dgi * _gelu_grad(ys)
        return dys, jnp.sum(dys * u, axis=0, keepdims=True)

    dy_ssm, g_ssm_d = _ew(b_gelu, [(dglu_in, "row", 0), (y_ssm, "row", 0), (proj, "row", OFF_U)],
                          [(COL, F32, "row"), (COL, F32, "acc")],
                          rows=t, cw=COL, ncb=4, tr=1024, name="ssm_gelu_bwd")
    du_ssm, dmats = _ssm_backward(dy_ssm, proj, states, mats_mx)
    du = (du_ssm + d_skip * dy_ssm).astype(MXU_DTYPE)
    g_lre, g_lim, g_bre, g_bim, g_cre, g_cim, g_lstep = mats_vjp(dmats)

    dq, dz_attn, dka, dkb, dva, dvb, dbias, dsink = _attn_backward(proj, bias, sinks, attn, dh_attn)
    dk = _shift_blocks(dka, dkb).astype(MXU_DTYPE)
    dv = _shift_blocks(dva, dvb).astype(MXU_DTYPE)
    (g_table,) = bias_vjp(dbias)
    g_sinks = dsink[:, 0, 0]

    dproj = jnp.concatenate([du, dz_ssm, dq, dk, dv, dz_attn, dgl_s, dgl_a], axis=1)
    gw_in = _mm(x_mx, dproj, ta=True, out_dtype=WIRE_DTYPE, name="grad_w_in", tm=2048, tn=512, tk=2048)
    sent_in = _scatter_start(scatter_cols(gw_in, col_in), name="scatter_g_in_start")
    grad_x = _mm(dproj, win, tb=True, add=dr, add_scale=DEEPNORM_ALPHA, name="grad_x", tm=1024, tn=512, tk=4352,
                 after=(sent_in[4],))

    parts_out = _scatter_wait(sent_out, grad_x, name="scatter_g_out_wait")
    parts_three = _scatter_wait(sent_three, parts_out, name="scatter_g_1024_wait")
    parts_in = _scatter_wait(sent_in, parts_three, name="scatter_g_in_wait")

    o_in = _adamw_reduce(parts_in, w_in[0], m_w_in[0], v_w_in[0], name="adamw_w_in", tr=128)
    three_w = jnp.concatenate([w_glu[0], w_branch_ssm[0], w_branch_attn[0]], axis=0)
    three_m = jnp.concatenate([m_w_glu[0], m_w_branch_ssm[0], m_w_branch_attn[0]], axis=0)
    three_v = jnp.concatenate([v_w_glu[0], v_w_branch_ssm[0], v_w_branch_attn[0]], axis=0)
    o_three = _adamw_reduce(parts_three, three_w, three_m, three_v, name="adamw_w_1024", tr=512)
    o_out = _adamw_reduce(parts_out, w_out[0], m_w_out[0], v_w_out[0], name="adamw_w_out", tr=128)

    small_w = [ssm_lambda_re, ssm_lambda_im, ssm_b_re, ssm_b_im, ssm_c_re, ssm_c_im, ssm_d, ssm_log_step,
               attn_sinks, rel_bias_table, ln_gain, ln_bias]
    small_m = [m_ssm_lambda_re, m_ssm_lambda_im, m_ssm_b_re, m_ssm_b_im, m_ssm_c_re, m_ssm_c_im, m_ssm_d,
               m_ssm_log_step, m_attn_sinks, m_rel_bias_table, m_ln_gain, m_ln_bias]
    small_v = [v_ssm_lambda_re, v_ssm_lambda_im, v_ssm_b_re, v_ssm_b_im, v_ssm_c_re, v_ssm_c_im, v_ssm_d,
               v_ssm_log_step, v_attn_sinks, v_rel_bias_table, v_ln_gain, v_ln_bias]
    small_g = [g_lre, g_lim, g_bre, g_bim, g_cre, g_cim, g_ssm_d, g_lstep, g_sinks, g_table, g_ln_gain, g_ln_bias]
    parts_small = _all_gather(_pack(small_g), name="gather_g_small")
    sg = _unpack(_sum_parts(parts_small, name="sum_g_small"), small_w)
    updates = [_adamw_native(w, g, m, v, name="adamw_" + n)
               for n, w, g, m, v in zip(SMALL, small_w, sg, small_m, small_v)]
    sd, sm, sv = zip(*updates)

    loss = lax.psum(loss_part[0, 0], MESH_AXES)

    def big(o, idx):
        g_in_, g_three_, g_out_ = o_in[idx], o_three[idx], o_out[idx]
        return {"w_in": g_in_[None], "w_glu": g_three_[None, :D_SSM], "w_branch_ssm": g_three_[None, D_SSM:2 * D_SSM],
                "w_branch_attn": g_three_[None, 2 * D_SSM:], "w_out": g_out_[None]}

    order = ["w_in", "ssm_lambda_re", "ssm_lambda_im", "ssm_b_re", "ssm_b_im", "ssm_c_re", "ssm_c_im", "ssm_d",
             "ssm_log_step", "w_glu", "attn_sinks", "rel_bias_table", "w_branch_ssm", "w_branch_attn", "w_out",
             "ln_gain", "ln_bias"]
    outs = [loss, grad_x[None]]
    for idx, small in enumerate([sg, sd, sm, sv]):
        table = big(None, idx)
        table.update(dict(zip(SMALL, small)))
        outs += [table[n] for n in order]
    return tuple(outs)
```

```python
import math

import jax
import jax.numpy as jnp
from jax import lax
from jax.experimental import pallas as pl
from jax.experimental.pallas import tpu as pltpu

F32 = jnp.float32
MXU_DTYPE = jnp.bfloat16
WIRE_DTYPE = jnp.bfloat16

D_MODEL = 2048
D_SSM = 1024
SSM_GROUP = 16
N_GROUPS = 64
SSM_STATE = 64
N_Q_HEADS = 16
N_KV_HEADS = 4
Q_PER_KV = 4
HEAD_DIM = 64
D_ATTN = 1024
D_KV = 256
WINDOW = 128
BLOCK = 128
N_BUCKETS = 32
MAX_DISTANCE = 128
D_IN = 8704
DEEPNORM_ALPHA = 2.0 ** 0.25
LN_EPS = 1e-5
NEG_INF = -1e30
ATTN_SCALE = HEAD_DIM ** -0.5

ADAM_LR = 0.001
ADAM_B1 = 0.9
ADAM_B2 = 0.999
ADAM_EPS = 1e-08
ADAM_WD = 0.01
ADAM_STEP = 10

N_DEV = 8
SSM_CHUNK = 16
GROUP_BLOCK = 8
N_GB = N_GROUPS // GROUP_BLOCK
GB_LANES = GROUP_BLOCK * SSM_GROUP
GB_STATE = GROUP_BLOCK * SSM_STATE
COL = 256
OFF_U, OFF_ZS, OFF_Q, OFF_K, OFF_V, OFF_ZA, OFF_GS, OFF_GA = 0, 4, 8, 12, 13, 14, 18, 26

VMEM_CAP = 56 * 1024 * 1024
MESH_AXES = ("x", "y", "c")


def _vmem_limit(block_bytes):
    return int(min(max(3 * block_bytes, 16 * 1024 * 1024), VMEM_CAP))


def _nbytes(shape, dtype):
    return math.prod(shape) * jnp.dtype(dtype).itemsize


def _tile(n, pref):
    if n <= pref:
        return n
    t = (pref // 128) * 128
    while t >= 128:
        if n % t == 0:
            return t
        t -= 128
    return n


def _mm(a, b, *, name, ta=False, tb=False, out_dtype=F32, tm=1024, tn=512, tk=512, add=None, add_scale=1.0,
        after=()):
    squeeze = a.ndim == 2
    if squeeze:
        a, b = a[None], b[None]
        if add is not None:
            add = add[None]
    nb = a.shape[0]
    m, k = (a.shape[2], a.shape[1]) if ta else (a.shape[1], a.shape[2])
    n = b.shape[1] if tb else b.shape[2]
    tm, tn, tk = _tile(m, tm), _tile(n, tn), _tile(k, tk)
    nk = k // tk
    dn = (((0 if ta else 1,), (1 if tb else 0,)), ((), ()))

    a_spec = (pl.BlockSpec((None, tk, tm), lambda g, i, j, kk: (g, kk, i)) if ta
              else pl.BlockSpec((None, tm, tk), lambda g, i, j, kk: (g, i, kk)))
    b_spec = (pl.BlockSpec((None, tn, tk), lambda g, i, j, kk: (g, j, kk)) if tb
              else pl.BlockSpec((None, tk, tn), lambda g, i, j, kk: (g, kk, j)))
    o_spec = pl.BlockSpec((None, tm, tn), lambda g, i, j, kk: (g, i, j))
    in_specs = [a_spec, b_spec]
    operands = [a, b]
    if add is not None:
        in_specs.append(o_spec)
        operands.append(add)
    for tok in after:
        in_specs.append(pl.BlockSpec(memory_space=pl.ANY))
        operands.append(tok)
    n_in = len(operands)

    def body(*refs):
        a_ref, b_ref = refs[0], refs[1]
        add_ref = refs[2] if add is not None else None
        o_ref = refs[n_in]
        acc_ref = refs[-1]
        kk = pl.program_id(3)
        part = lax.dot_general(a_ref[...].astype(MXU_DTYPE), b_ref[...].astype(MXU_DTYPE), dn,
                               preferred_element_type=F32)

        def finish(r):
            if add_ref is not None:
                r = r + add_scale * add_ref[...]
            o_ref[...] = r.astype(out_dtype)

        if nk == 1:
            finish(part)
            return

        @pl.when(kk == 0)
        def _():
            acc_ref[...] = part

        @pl.when(jnp.logical_and(kk > 0, kk < nk - 1))
        def _():
            acc_ref[...] += part

        @pl.when(kk == nk - 1)
        def _():
            finish(acc_ref[...] + part)

    blocks = (_nbytes((tm, tk), a.dtype) + _nbytes((tk, tn), b.dtype) + _nbytes((tm, tn), out_dtype)
              + (_nbytes((tm, tn), F32) if add is not None else 0))
    out = pl.pallas_call(
        body,
        name=name,
        grid=(nb, m // tm, n // tn, nk),
        in_specs=in_specs,
        out_specs=o_spec,
        out_shape=jax.ShapeDtypeStruct((nb, m, n), out_dtype),
        scratch_shapes=[pltpu.VMEM((tm, tn), F32)],
        compiler_params=pltpu.CompilerParams(
            dimension_semantics=("parallel", "parallel", "parallel", "arbitrary"),
            vmem_limit_bytes=_vmem_limit(2 * blocks + 2 * _nbytes((tm, tn), F32))),
    )(*operands)
    return out[0] if squeeze else out


def _mm_fused(pairs, extras, out_dtypes, epilogue, *, name, tm, tn, after=()):
    m = pairs[0][0].shape[0]
    n = pairs[0][1].shape[0] if pairs[0][2] else pairs[0][1].shape[1]
    tm, tn = _tile(m, tm), _tile(n, tn)
    in_specs, operands, dns = [], [], []
    for a, b, tb in pairs:
        k = a.shape[1]
        in_specs += [pl.BlockSpec((tm, k), lambda i, j: (i, 0)),
                     pl.BlockSpec((tn, k), lambda i, j: (j, 0)) if tb else pl.BlockSpec((k, tn), lambda i, j: (0, j))]
        operands += [a, b]
        dns.append((((1,), (1 if tb else 0,)), ((), ())))

    def tile_at(col):
        return pl.BlockSpec((tm, tn), lambda i, j: (i, col + j))

    in_specs += [tile_at(col) for _, col in extras]
    operands += [arr for arr, _ in extras]
    in_specs += [pl.BlockSpec(memory_space=pl.ANY)] * len(after)
    operands += list(after)
    n_pairs, n_extra, n_in = len(pairs), len(extras), len(operands)

    def body(*refs):
        products = [lax.dot_general(refs[2 * p][...].astype(MXU_DTYPE), refs[2 * p + 1][...].astype(MXU_DTYPE),
                                    dns[p], preferred_element_type=F32) for p in range(n_pairs)]
        tiles = [refs[2 * n_pairs + e][...] for e in range(n_extra)]
        for o_ref, val in zip(refs[n_in:], epilogue(products, tiles)):
            o_ref[...] = val.astype(o_ref.dtype)

    blocks = sum(_nbytes((tm, a.shape[1]), a.dtype) + _nbytes((tn, a.shape[1]), b.dtype) for a, b, _ in pairs)
    blocks += sum(_nbytes((tm, tn), arr.dtype) for arr, _ in extras) + sum(_nbytes((tm, tn), dt) for dt in out_dtypes)
    return pl.pallas_call(
        body, name=name, grid=(m // tm, n // tn), in_specs=in_specs,
        out_specs=[tile_at(0)] * len(out_dtypes),
        out_shape=[jax.ShapeDtypeStruct((m, n), dt) for dt in out_dtypes],
        compiler_params=pltpu.CompilerParams(
            dimension_semantics=("parallel", "parallel"),
            vmem_limit_bytes=_vmem_limit(2 * blocks + n_pairs * _nbytes((tm, tn), F32))),
    )(*operands)


def _ew(fn, ins, outs, *, rows, cw, ncb, tr, name):
    tr = min(tr, rows)
    n_in = len(ins)

    def row_map(off):
        return lambda j, i: (i, off + j)

    def vec_map(off):
        return lambda j, i: (0, off + j)

    in_specs = []
    for arr, kind, off in ins:
        if kind == "row":
            in_specs.append(pl.BlockSpec((tr, cw), row_map(off)))
        else:
            in_specs.append(pl.BlockSpec((1, cw), vec_map(off)))
    out_specs, out_shapes = [], []
    for bw, dt, kind in outs:
        if kind == "row":
            out_specs.append(pl.BlockSpec((tr, bw), row_map(0)))
            out_shapes.append(jax.ShapeDtypeStruct((rows, ncb * bw), dt))
        else:
            out_specs.append(pl.BlockSpec((1, bw), vec_map(0)))
            out_shapes.append(jax.ShapeDtypeStruct((1, ncb * bw), F32))

    def body(*refs):
        i = pl.program_id(1)
        vals = fn(*[r[...] for r in refs[:n_in]])
        for r, (bw, dt, kind), v in zip(refs[n_in:], outs, vals):
            if kind == "row":
                r[...] = v.astype(dt)
            else:
                @pl.when(i == 0)
                def _(r=r):
                    r[...] = jnp.zeros_like(r)

                r[...] += v

    blocks = sum(_nbytes((tr, cw), a.dtype) for a, kind, _ in ins if kind == "row")
    blocks += sum(_nbytes((tr, bw), dt) for bw, dt, kind in outs if kind == "row")
    res = pl.pallas_call(
        body,
        name=name,
        grid=(ncb, rows // tr),
        in_specs=in_specs,
        out_specs=out_specs,
        out_shape=out_shapes,
        compiler_params=pltpu.CompilerParams(
            dimension_semantics=("parallel", "arbitrary"),
            vmem_limit_bytes=_vmem_limit(4 * blocks)),
    )(*[a for a, _, _ in ins])
    return res


def _sigmoid(x):
    return 1.0 / (1.0 + jnp.exp(-x))


INV_SQRT2 = 0.7071067811865476
INV_SQRT_2PI = 0.3989422804014327


def _gelu(x):
    return 0.5 * x * (1.0 + lax.erf(x * INV_SQRT2))


def _gelu_grad(x):
    return 0.5 * (1.0 + lax.erf(x * INV_SQRT2)) + x * INV_SQRT_2PI * jnp.exp(-0.5 * x * x)


def _silu(x):
    return x * _sigmoid(x)


def _silu_grad(x):
    s = _sigmoid(x)
    return s * (1.0 + x * (1.0 - s))


@jax.custom_vjp
def _taps_product(mr, mi, bbr, bbi):
    bbr_t, bbi_t = jnp.transpose(bbr, (0, 2, 1)), jnp.transpose(bbi, (0, 2, 1))
    return jnp.sum(mr[..., None, :] * bbr_t[None, :, None] - mi[..., None, :] * bbi_t[None, :, None], axis=-1)


def _taps_product_fwd(mr, mi, bbr, bbi):
    return _taps_product(mr, mi, bbr, bbi), (mr, mi, bbr, bbi)


def _taps_product_bwd(res, g):
    mr, mi, bbr, bbi = res
    hi = lax.Precision.HIGHEST
    return (jnp.einsum("tghk,gpk->tghp", g, bbr, precision=hi), -jnp.einsum("tghk,gpk->tghp", g, bbi, precision=hi),
            jnp.einsum("tghk,tghp->gpk", g, mr, precision=hi), -jnp.einsum("tghk,tghp->gpk", g, mi, precision=hi))


_taps_product.defvjp(_taps_product_fwd, _taps_product_bwd)


def _ssm_matrices(lam_re, lam_im, b_re, b_im, c_re, c_im, log_step):
    L = SSM_CHUNK
    step = jnp.exp(log_step)[:, None]
    ea, eb = lam_re * step, lam_im * step
    mag = jnp.exp(ea)
    lbr, lbi = mag * jnp.cos(eb), mag * jnp.sin(eb)
    den = lam_re * lam_re + lam_im * lam_im
    nr, ni = lbr - 1.0, lbi
    cr = (nr * lam_re + ni * lam_im) / den
    ci = (ni * lam_re - nr * lam_im) / den
    bbr = cr[..., None] * b_re - ci[..., None] * b_im
    bbi = cr[..., None] * b_im + ci[..., None] * b_re
    taus = jnp.arange(L + 1, dtype=F32)[:, None, None]
    pmag = jnp.exp(taus * ea[None])
    pwr, pwi = pmag * jnp.cos(taus * eb[None]), pmag * jnp.sin(taus * eb[None])
    mr = c_re[None] * pwr[:L, :, None, :] - c_im[None] * pwi[:L, :, None, :]
    mi = c_re[None] * pwi[:L, :, None, :] + c_im[None] * pwr[:L, :, None, :]
    kk = _taps_product(mr, mi, bbr, bbi)
    taps = jnp.transpose(kk.reshape(L, N_GB, GROUP_BLOCK, SSM_GROUP, SSM_GROUP), (1, 0, 2, 4, 3))
    taps = taps.reshape(N_GB, L, GB_LANES, SSM_GROUP)
    rev_r, rev_i = pwr[L - 1 - jnp.arange(L)], pwi[L - 1 - jnp.arange(L)]
    wer = rev_r[..., None] * bbr[None] - rev_i[..., None] * bbi[None]
    wei = rev_r[..., None] * bbi[None] + rev_i[..., None] * bbr[None]

    def rows_in(w):
        w = jnp.transpose(w.reshape(L, N_GB, GROUP_BLOCK, SSM_STATE, SSM_GROUP), (1, 0, 2, 4, 3))
        return w.reshape(N_GB, L * GB_LANES, SSM_STATE)

    wend = jnp.concatenate([rows_in(wer), rows_in(wei)], axis=2)
    m1r = c_re[None] * pwr[1:, :, None, :] - c_im[None] * pwi[1:, :, None, :]
    m1i = c_re[None] * pwi[1:, :, None, :] + c_im[None] * pwr[1:, :, None, :]

    def rows_out(m):
        m = jnp.transpose(m.reshape(L, N_GB, GROUP_BLOCK, SSM_GROUP, SSM_STATE), (1, 2, 4, 0, 3))
        return m.reshape(N_GB, GB_STATE, L * SSM_GROUP)

    wout = jnp.concatenate([rows_out(m1r), rows_out(-m1i)], axis=1)
    acat = jnp.concatenate([pwr[L].reshape(N_GB, 1, GB_STATE), pwi[L].reshape(N_GB, 1, GB_STATE)], axis=2)
    return taps, wend, wout, acat


def _lane_group(shape, axis, shift):
    return (lax.broadcasted_iota(jnp.int32, shape, axis) >> shift) & (GROUP_BLOCK - 1)


def _ssm_expand(taps, wend, wout):
    L = SSM_CHUNK
    taps = jnp.pad(taps, ((0, 0), (0, 0), (0, 0), (0, GB_LANES - SSM_GROUP)))

    def body(t_ref, we_ref, wo_ref, d_ref, web_ref, wob_ref):
        def rc(shape):
            return lax.broadcasted_iota(jnp.int32, shape, 0), lax.broadcasted_iota(jnp.int32, shape, 1)

        r, c = rc((GB_LANES, GB_LANES))
        spread = ((r < SSM_GROUP) & (r == (c & (SSM_GROUP - 1)))).astype(MXU_DTYPE)
        same = _lane_group((GB_LANES, GB_LANES), 0, 4) == _lane_group((GB_LANES, GB_LANES), 1, 4)
        for tau in range(L):
            full = jnp.dot(t_ref[tau].astype(MXU_DTYPE), spread, preferred_element_type=F32)
            d_ref[tau] = jnp.where(same, full, 0.0).astype(d_ref.dtype)
        r, c = rc((2 * SSM_STATE, STATE_W))
        part = (r >> 6) == (c >> 9)
        spread = (part & ((r & (SSM_STATE - 1)) == (c & (SSM_STATE - 1)))).astype(MXU_DTYPE)
        keep = _lane_group((GB_LANES, STATE_W), 0, 4) == _lane_group((GB_LANES, STATE_W), 1, 6)
        for j in range(L):
            rows = slice(j * GB_LANES, (j + 1) * GB_LANES)
            full = jnp.dot(we_ref[rows, :].astype(MXU_DTYPE), spread, preferred_element_type=F32)
            web_ref[rows, :] = jnp.where(keep, full, 0.0).astype(web_ref.dtype)
        r, c = rc((GB_LANES, GB_LANES))
        own = _lane_group((STATE_W, GB_LANES), 0, 6) == _lane_group((STATE_W, GB_LANES), 1, 4)
        for tt in range(L):
            half, k = tt // GROUP_BLOCK, tt % GROUP_BLOCK
            spread = (((r >> 4) == k) & ((r & (SSM_GROUP - 1)) == (c & (SSM_GROUP - 1)))).astype(MXU_DTYPE)
            src = wo_ref[:, half * GB_LANES:(half + 1) * GB_LANES].astype(MXU_DTYPE)
            full = jnp.dot(src, spread, preferred_element_type=F32)
            wob_ref[:, tt * GB_LANES:(tt + 1) * GB_LANES] = jnp.where(own, full, 0.0).astype(wob_ref.dtype)

    def spec(shape):
        return pl.BlockSpec((None,) + shape[1:], lambda b: (b,) + (0,) * (len(shape) - 1))

    out_shapes = [(N_GB, L, GB_LANES, GB_LANES), (N_GB, L * GB_LANES, STATE_W), (N_GB, STATE_W, L * GB_LANES)]
    return tuple(pl.pallas_call(
        body, name="ssm_expand", grid=(N_GB,),
        in_specs=[spec(taps.shape), spec(wend.shape), spec(wout.shape)],
        out_specs=[spec(s) for s in out_shapes],
        out_shape=[jax.ShapeDtypeStruct(s, MXU_DTYPE) for s in out_shapes],
        compiler_params=pltpu.CompilerParams(dimension_semantics=("parallel",), vmem_limit_bytes=VMEM_CAP),
    )(taps, wend, wout))


def _dot(a, b):
    return jnp.dot(a.astype(MXU_DTYPE), b.astype(MXU_DTYPE), preferred_element_type=F32)


def _dot_nt(a, b):
    return lax.dot_general(a.astype(MXU_DTYPE), b.astype(MXU_DTYPE), (((1,), (1,)), ((), ())),
                           preferred_element_type=F32)


def _dot_tn(a, b):
    return lax.dot_general(a.astype(MXU_DTYPE), b.astype(MXU_DTYPE), (((0,), (0,)), ((), ())),
                           preferred_element_type=F32)


STATE_W = 2 * GB_STATE


def _chunk_scan(e, acat):
    nc = e.shape[0]
    spec = pl.BlockSpec((nc, STATE_W), lambda b: (0, b))
    aspec = pl.BlockSpec((None, 1, STATE_W), lambda b: (b, 0, 0))

    def body(e_ref, a_ref, s_ref):
        a_r, a_i = a_ref[:, :GB_STATE], a_ref[:, GB_STATE:]

        def step(c, carry):
            s_r, s_i = carry
            s_ref[pl.ds(c, 1), :GB_STATE] = s_r
            s_ref[pl.ds(c, 1), GB_STATE:] = s_i
            e_r = e_ref[pl.ds(c, 1), :GB_STATE]
            e_i = e_ref[pl.ds(c, 1), GB_STATE:]
            return (a_r * s_r - a_i * s_i + e_r, a_r * s_i + a_i * s_r + e_i)

        zero = jnp.zeros((1, GB_STATE), F32)
        lax.fori_loop(0, nc, step, (zero, zero))

    return pl.pallas_call(
        body, name="ssm_chunk_scan", grid=(N_GB,),
        in_specs=[spec, aspec], out_specs=spec,
        out_shape=jax.ShapeDtypeStruct(e.shape, F32),
        compiler_params=pltpu.CompilerParams(dimension_semantics=("parallel",)),
    )(e, acat)


def _chunk_scan_bwd(ds, s, acat):
    nc = ds.shape[0]
    spec = pl.BlockSpec((nc, STATE_W), lambda b: (0, b))
    aspec = pl.BlockSpec((None, 1, STATE_W), lambda b: (b, 0, 0))

    def body(ds_ref, s_ref, a_ref, ge_ref, da_ref):
        a_r, a_i = a_ref[:, :GB_STATE], a_ref[:, GB_STATE:]

        def step(t, carry):
            g_r, g_i, d_r, d_i = carry
            c = nc - 1 - t
            ge_ref[pl.ds(c, 1), :GB_STATE] = g_r
            ge_ref[pl.ds(c, 1), GB_STATE:] = g_i
            s_r = s_ref[pl.ds(c, 1), :GB_STATE]
            s_i = s_ref[pl.ds(c, 1), GB_STATE:]
            d_r = d_r + g_r * s_r + g_i * s_i
            d_i = d_i + g_i * s_r - g_r * s_i
            n_r = ds_ref[pl.ds(c, 1), :GB_STATE] + a_r * g_r + a_i * g_i
            n_i = ds_ref[pl.ds(c, 1), GB_STATE:] + a_r * g_i - a_i * g_r
            return (n_r, n_i, d_r, d_i)

        zero = jnp.zeros((1, GB_STATE), F32)
        _, _, d_r, d_i = lax.fori_loop(0, nc, step, (zero, zero, zero, zero))
        da_ref[:, :GB_STATE] = d_r
        da_ref[:, GB_STATE:] = d_i

    return pl.pallas_call(
        body, name="ssm_chunk_scan_bwd", grid=(N_GB,),
        in_specs=[spec, spec, aspec], out_specs=[spec, aspec],
        out_shape=[jax.ShapeDtypeStruct(ds.shape, F32), jax.ShapeDtypeStruct(acat.shape, F32)],
        compiler_params=pltpu.CompilerParams(dimension_semantics=("parallel",)),
    )(ds, s, acat)


def _step_rows(ref, j, nc):
    return ref[pl.ds(j, nc, stride=SSM_CHUNK), :]


def _fold_lanes(z, widths):
    for w in widths:
        z = z + pltpu.roll(z, w, 1)
    return z


def _ssm_forward(proj, mats):
    dblk, wend, wout, acat = mats
    t = proj.shape[0]
    nc = t // SSM_CHUNK
    L = SSM_CHUNK
    lanes = pl.BlockSpec((t, GB_LANES), lambda b: (0, b))
    state = pl.BlockSpec((nc, STATE_W), lambda b: (0, b))

    def body_end(u_ref, w_ref, e_ref):
        x = jnp.concatenate([_step_rows(u_ref, j, nc).astype(MXU_DTYPE) for j in range(L)], axis=1)
        e_ref[...] = jnp.dot(x, w_ref[...], preferred_element_type=F32)

    e = pl.pallas_call(
        body_end, name="ssm_chunk_end", grid=(N_GB,),
        in_specs=[lanes, pl.BlockSpec((None, L * GB_LANES, STATE_W), lambda b: (b, 0, 0))],
        out_specs=state, out_shape=jax.ShapeDtypeStruct((nc, N_GB * STATE_W), F32),
        compiler_params=pltpu.CompilerParams(dimension_semantics=("parallel",), vmem_limit_bytes=VMEM_CAP),
    )(proj, wend)
    s = _chunk_scan(e, acat)

    def body_out(u_ref, d_ref, s_ref, w_ref, y_ref):
        xs = [_step_rows(u_ref, j, nc).astype(MXU_DTYPE) for j in range(L)]
        sb = s_ref[...].astype(MXU_DTYPE)
        for tt in range(L):
            xcat = jnp.concatenate(xs[:tt + 1], axis=1)
            taps = jnp.concatenate([d_ref[tt - j] for j in range(tt + 1)], axis=0).astype(MXU_DTYPE)
            y = (jnp.dot(xcat, taps, preferred_element_type=F32)
                 + jnp.dot(sb, w_ref[:, tt * GB_LANES:(tt + 1) * GB_LANES], preferred_element_type=F32))
            y_ref[pl.ds(tt, nc, stride=L), :] = y

    y = pl.pallas_call(
        body_out, name="ssm_chunk_out", grid=(N_GB,),
        in_specs=[lanes, pl.BlockSpec((None, L, GB_LANES, GB_LANES), lambda b: (b, 0, 0, 0)), state,
                  pl.BlockSpec((None, STATE_W, L * GB_LANES), lambda b: (b, 0, 0))],
        out_specs=lanes, out_shape=jax.ShapeDtypeStruct((t, D_SSM), F32),
        compiler_params=pltpu.CompilerParams(dimension_semantics=("parallel",), vmem_limit_bytes=VMEM_CAP),
    )(proj, dblk, s, wout)
    return y, s


def _ssm_backward(dy, proj, s, mats):
    dblk, wend, wout, acat = mats
    t = proj.shape[0]
    nc = t // SSM_CHUNK
    L = SSM_CHUNK
    lanes = pl.BlockSpec((t, GB_LANES), lambda b: (0, b))
    state = pl.BlockSpec((nc, STATE_W), lambda b: (0, b))
    taps_spec = pl.BlockSpec((None, L, GB_LANES, GB_LANES), lambda b: (b, 0, 0, 0))

    def body_state(dy_ref, w_ref, ds_ref):
        dyc = jnp.concatenate([_step_rows(dy_ref, tt, nc).astype(MXU_DTYPE) for tt in range(L)], axis=1)
        ds_ref[...] = _dot_nt(dyc, w_ref[...])

    ds = pl.pallas_call(
        body_state, name="ssm_bwd_state", grid=(N_GB,),
        in_specs=[lanes, pl.BlockSpec((None, STATE_W, L * GB_LANES), lambda b: (b, 0, 0))],
        out_specs=state, out_shape=jax.ShapeDtypeStruct((nc, N_GB * STATE_W), F32),
        compiler_params=pltpu.CompilerParams(dimension_semantics=("parallel",), vmem_limit_bytes=VMEM_CAP),
    )(dy, wout)
    ge, dacat = _chunk_scan_bwd(ds, s, acat)

    def body_in(u_ref, dy_ref, d_ref, ge_ref, w_ref, du_ref, dd_ref):
        xs = [_step_rows(u_ref, j, nc).astype(MXU_DTYPE) for j in range(L)]
        dys = [_step_rows(dy_ref, tt, nc).astype(MXU_DTYPE) for tt in range(L)]
        ge = ge_ref[...].astype(MXU_DTYPE)
        for i in range(L):
            dyc = jnp.concatenate(dys[i:], axis=1)
            taps = jnp.concatenate([d_ref[tt - i] for tt in range(i, L)], axis=1).astype(MXU_DTYPE)
            du_ref[pl.ds(i, nc, stride=L), :] = (
                _dot_nt(dyc, taps) + _dot_nt(ge, w_ref[i * GB_LANES:(i + 1) * GB_LANES, :]))
        for j in range(L):
            m = _dot_tn(xs[j], jnp.concatenate(dys[j:], axis=1))
            for tau in range(L - j):
                part = m[:, tau * GB_LANES:(tau + 1) * GB_LANES]
                if j == 0:
                    dd_ref[tau] = part
                else:
                    dd_ref[tau] += part
        same = _lane_group((GB_LANES, GB_LANES), 0, 4) == _lane_group((GB_LANES, GB_LANES), 1, 4)
        for tau in range(L):
            dd_ref[tau] = _fold_lanes(jnp.where(same, dd_ref[tau], 0.0), (64, 32, 16))

    du, ddblk = pl.pallas_call(
        body_in, name="ssm_bwd_in", grid=(N_GB,),
        in_specs=[lanes, lanes, taps_spec, state,
                  pl.BlockSpec((None, L * GB_LANES, STATE_W), lambda b: (b, 0, 0))],
        out_specs=[lanes, taps_spec],
        out_shape=[jax.ShapeDtypeStruct((t, D_SSM), F32), jax.ShapeDtypeStruct(dblk.shape, F32)],
        compiler_params=pltpu.CompilerParams(dimension_semantics=("parallel",), vmem_limit_bytes=VMEM_CAP),
    )(proj, dy, dblk, ge, wend)

    def body_w(u_ref, dy_ref, ge_ref, s_ref, dwe_ref, dwo_ref):
        ge = ge_ref[...].astype(MXU_DTYPE)
        sb = s_ref[...].astype(MXU_DTYPE)
        keep = _lane_group((GB_LANES, STATE_W), 0, 4) == _lane_group((GB_LANES, STATE_W), 1, 6)
        low = lax.broadcasted_iota(jnp.int32, (GB_LANES, 2 * SSM_STATE), 1) < SSM_STATE

        def fold_state(z):
            z = z[:, :GB_STATE // 2] + z[:, GB_STATE // 2:]
            z = z[:, :GB_STATE // 4] + z[:, GB_STATE // 4:]
            return _fold_lanes(z, (SSM_STATE,))

        for j in range(L):
            z = jnp.where(keep, _dot_tn(_step_rows(u_ref, j, nc).astype(MXU_DTYPE), ge), 0.0)
            dwe_ref[j * GB_LANES:(j + 1) * GB_LANES, :] = jnp.where(
                low, fold_state(z[:, :GB_STATE]), fold_state(z[:, GB_STATE:]))
        own = _lane_group((STATE_W, GB_LANES), 0, 6) == _lane_group((STATE_W, GB_LANES), 1, 4)
        chunk = lax.broadcasted_iota(jnp.int32, (STATE_W, GB_LANES), 1) >> 4
        for half in range(L // GROUP_BLOCK):
            acc = jnp.zeros((STATE_W, GB_LANES), F32)
            for k in range(GROUP_BLOCK):
                tt = half * GROUP_BLOCK + k
                z = jnp.where(own, _dot_tn(sb, _step_rows(dy_ref, tt, nc).astype(MXU_DTYPE)), 0.0)
                acc = acc + jnp.where(chunk == k, _fold_lanes(z, (64, 32, 16)), 0.0)
            dwo_ref[:, half * GB_LANES:(half + 1) * GB_LANES] = acc

    dwend, dwout = pl.pallas_call(
        body_w, name="ssm_bwd_w", grid=(N_GB,),
        in_specs=[lanes, lanes, state, state],
        out_specs=[pl.BlockSpec((None, L * GB_LANES, 2 * SSM_STATE), lambda b: (b, 0, 0)),
                   pl.BlockSpec((None, STATE_W, L * SSM_GROUP), lambda b: (b, 0, 0))],
        out_shape=[jax.ShapeDtypeStruct((N_GB, L * GB_LANES, 2 * SSM_STATE), F32),
                   jax.ShapeDtypeStruct((N_GB, STATE_W, L * SSM_GROUP), F32)],
        compiler_params=pltpu.CompilerParams(dimension_semantics=("parallel",), vmem_limit_bytes=VMEM_CAP),
    )(proj, dy, ge, s)
    return du, (ddblk[:, :, :, :SSM_GROUP], dwend, dwout, dacat)


def _t5_bucket(dist):
    max_exact = N_BUCKETS // 2
    is_small = dist < max_exact
    d = jnp.maximum(dist, 1).astype(F32)
    large = max_exact + (jnp.log(d / max_exact) / math.log(MAX_DISTANCE / max_exact)
                         * (N_BUCKETS - max_exact)).astype(jnp.int32)
    large = jnp.minimum(large, N_BUCKETS - 1)
    return jnp.where(is_small, dist, large)


def _band_bias(rel_bias_table):
    i = jnp.arange(BLOCK)[:, None]
    j = jnp.arange(BLOCK)[None, :]
    bucket = _t5_bucket(jnp.where(j > i, BLOCK + i - j, i - j))
    onehot = (bucket[:, :, None] == jnp.arange(N_BUCKETS)[None, None, :]).astype(F32)
    return jnp.einsum("qsb,bh->hqs", onehot, rel_bias_table, precision=lax.Precision.HIGHEST)


assert WINDOW == BLOCK
KV_PAIR = 2
HEADS_PER_STEP = KV_PAIR * Q_PER_KV
Q_LANES = HEADS_PER_STEP * HEAD_DIM
SLAB = 2 * HEAD_DIM
Q_COL0 = OFF_Q * COL // Q_LANES
K_COL0 = OFF_K * COL // SLAB
V_COL0 = OFF_V * COL // SLAB
ZA_COL0 = OFF_ZA * COL // Q_LANES


def _attn_specs():
    q_spec = pl.BlockSpec((BLOCK, Q_LANES), lambda m, n: (n, Q_COL0 + m))
    k_prev = pl.BlockSpec((BLOCK, SLAB), lambda m, n: (jnp.maximum(n - 1, 0), K_COL0 + m))
    k_cur = pl.BlockSpec((BLOCK, SLAB), lambda m, n: (n, K_COL0 + m))
    v_prev = pl.BlockSpec((BLOCK, SLAB), lambda m, n: (jnp.maximum(n - 1, 0), V_COL0 + m))
    v_cur = pl.BlockSpec((BLOCK, SLAB), lambda m, n: (n, V_COL0 + m))
    bias_spec = pl.BlockSpec((HEADS_PER_STEP, BLOCK, BLOCK), lambda m, n: (m, 0, 0))
    sink_spec = pl.BlockSpec(memory_space=pltpu.SMEM)
    wide = pl.BlockSpec((BLOCK, Q_LANES), lambda m, n: (n, m))
    gate = pl.BlockSpec((BLOCK, Q_LANES), lambda m, n: (n, ZA_COL0 + m))
    pair = pl.BlockSpec((BLOCK, SLAB), lambda m, n: (n, m))
    return [sink_spec, q_spec, k_prev, k_cur, v_prev, v_cur, bias_spec, gate], wide, pair


def _low_lanes(shape):
    return lax.broadcasted_iota(jnp.int32, shape, 1) < HEAD_DIM


def _pair_halves(ref):
    kb = ref[...]
    sw = pltpu.roll(kb, HEAD_DIM, 1)
    lo = _low_lanes(kb.shape)
    zero = jnp.zeros_like(kb)
    first = (jnp.where(lo, kb, zero).astype(MXU_DTYPE), jnp.where(lo, zero, sw).astype(MXU_DTYPE))
    second = (jnp.where(lo, sw, zero).astype(MXU_DTYPE), jnp.where(lo, zero, kb).astype(MXU_DTYPE))
    return first, second


def _fold_pair(acc):
    f = [x + pltpu.roll(x, HEAD_DIM, 1) for x in acc]
    return jnp.where(_low_lanes(f[0].shape), f[0], f[1])


def _from_prev(n):
    row = lax.broadcasted_iota(jnp.int32, (BLOCK, BLOCK), 0)
    col = lax.broadcasted_iota(jnp.int32, (BLOCK, BLOCK), 1)
    prev = col > row
    return prev, jnp.where(jnp.logical_and(n == 0, prev), NEG_INF, 0.0)


def _softmax_sink(s, sink):
    m = jnp.maximum(jnp.max(s, axis=1, keepdims=True), sink)
    e = jnp.exp(s - m)
    es = jnp.exp(sink - m)
    inv = 1.0 / (jnp.sum(e, axis=1, keepdims=True) + es)
    return e * inv, es * inv


def _stack_pair(prev_halves, own_halves, a):
    return jnp.concatenate([prev_halves[a][0], prev_halves[a][1], own_halves[a][0], own_halves[a][1]], axis=0)


def _split_heads(x4, prev):
    return [jnp.where(prev, x4[:, e * BLOCK:(e + 1) * BLOCK], x4[:, (2 + e) * BLOCK:(3 + e) * BLOCK]) for e in range(2)]


def _spread_heads(x, prev):
    return jnp.concatenate([jnp.where(prev, x[0], 0.0), jnp.where(prev, x[1], 0.0),
                            jnp.where(prev, 0.0, x[0]), jnp.where(prev, 0.0, x[1])], axis=1)


def _attn_forward(proj, bias, sinks):
    t = proj.shape[0]
    in_specs, wide, _ = _attn_specs()

    def body(sink_ref, q_ref, kp_ref, kc_ref, vp_ref, vc_ref, bias_ref, z_ref, o_ref, h_ref):
        m, n = pl.program_id(0), pl.program_id(1)
        kp, kc, vp, vc = (_pair_halves(r) for r in (kp_ref, kc_ref, vp_ref, vc_ref))
        keys = [_stack_pair(kp, kc, a) for a in range(KV_PAIR)]
        vals = [_stack_pair(vp, vc, a) for a in range(KV_PAIR)]
        prev, edge = _from_prev(n)
        prev2 = jnp.concatenate([prev, prev], axis=0)
        for a in range(KV_PAIR):
            slabs = (2 * a, 2 * a + 1)
            q = jnp.concatenate([q_ref[:, s * SLAB:(s + 1) * SLAB] for s in slabs], axis=0).astype(MXU_DTYPE)
            logits = _split_heads(_dot_nt(q, keys[a]), prev2)
            probs = []
            for e in range(2):
                rows = [_softmax_sink(logits[e][r * BLOCK:(r + 1) * BLOCK] * ATTN_SCALE + bias_ref[2 * s + e] + edge,
                                      sink_ref[m * HEADS_PER_STEP + 2 * s + e])[0] for r, s in enumerate(slabs)]
                probs.append(jnp.concatenate(rows, axis=0))
            out = _dot(_spread_heads(probs, prev2), vals[a])
            for r, s in enumerate(slabs):
                cols = slice(s * SLAB, (s + 1) * SLAB)
                o_ref[:, cols] = out[r * BLOCK:(r + 1) * BLOCK]
                h_ref[:, cols] = (out[r * BLOCK:(r + 1) * BLOCK] * _silu(z_ref[:, cols])).astype(h_ref.dtype)

    return pl.pallas_call(
        body, name="attn_fwd", grid=(N_KV_HEADS // KV_PAIR, t // BLOCK),
        in_specs=in_specs, out_specs=[wide, wide],
        out_shape=[jax.ShapeDtypeStruct((t, D_ATTN), F32), jax.ShapeDtypeStruct((t, D_ATTN), MXU_DTYPE)],
        compiler_params=pltpu.CompilerParams(dimension_semantics=("parallel", "arbitrary")),
    )(sinks, proj, proj, proj, proj, proj, bias, proj)


def _attn_backward(proj, bias, sinks, attn, dh):
    t = proj.shape[0]
    in_specs, wide, pair = _attn_specs()
    bias_spec = in_specs[-2]
    sink_out = pl.BlockSpec((HEADS_PER_STEP, 8, 128), lambda m, n: (m, 0, 0))

    def body(sink_ref, q_ref, kp_ref, kc_ref, vp_ref, vc_ref, bias_ref, z_ref, o_ref, dh_ref,
             dq_ref, dz_ref, dka_ref, dkb_ref, dva_ref, dvb_ref, dbias_ref, dsink_ref):
        m, n = pl.program_id(0), pl.program_id(1)

        @pl.when(n == 0)
        def _():
            dbias_ref[...] = jnp.zeros_like(dbias_ref)
            dsink_ref[...] = jnp.zeros_like(dsink_ref)

        kp, kc, vp, vc = (_pair_halves(r) for r in (kp_ref, kc_ref, vp_ref, vc_ref))
        keys = [_stack_pair(kp, kc, a) for a in range(KV_PAIR)]
        vals = [_stack_pair(vp, vc, a) for a in range(KV_PAIR)]
        prev, edge = _from_prev(n)
        lo = _low_lanes((BLOCK, SLAB))
        prev2 = jnp.concatenate([prev, prev], axis=0)
        dk = [[None] * KV_PAIR for _ in range(2)]
        dv = [[None] * KV_PAIR for _ in range(2)]
        for a in range(KV_PAIR):
            slabs = (2 * a, 2 * a + 1)
            q = jnp.concatenate([q_ref[:, s * SLAB:(s + 1) * SLAB] for s in slabs], axis=0).astype(MXU_DTYPE)
            gated = []
            for s in slabs:
                cols = slice(s * SLAB, (s + 1) * SLAB)
                dh_s, z_s = dh_ref[:, cols], z_ref[:, cols]
                dz_ref[:, cols] = (dh_s * o_ref[:, cols] * _silu_grad(z_s)).astype(dz_ref.dtype)
                gated.append(dh_s * _silu(z_s))
            do = jnp.concatenate(gated, axis=0).astype(MXU_DTYPE)
            logits = _split_heads(_dot_nt(q, keys[a]), prev2)
            dprobs = _split_heads(_dot_nt(do, vals[a]), prev2)
            probs, dlogits = [], []
            for e in range(2):
                p_rows, ds_rows = [], []
                for r, s in enumerate(slabs):
                    h = 2 * s + e
                    rows = slice(r * BLOCK, (r + 1) * BLOCK)
                    p, ps = _softmax_sink(logits[e][rows] * ATTN_SCALE + bias_ref[h] + edge,
                                          sink_ref[m * HEADS_PER_STEP + h])
                    delta = jnp.sum(p * dprobs[e][rows], axis=1, keepdims=True)
                    ds = p * (dprobs[e][rows] - delta)
                    dbias_ref[h] += ds
                    dsink_ref[h] += jnp.broadcast_to(jnp.sum(-ps * delta, axis=0, keepdims=True), (8, 128))
                    p_rows.append(p)
                    ds_rows.append(ds)
                probs.append(jnp.concatenate(p_rows, axis=0))
                dlogits.append(jnp.concatenate(ds_rows, axis=0))
            ds4 = _spread_heads(dlogits, prev2).astype(MXU_DTYPE)
            p4 = _spread_heads(probs, prev2).astype(MXU_DTYPE)
            dq = _dot(ds4, keys[a]) * ATTN_SCALE
            for r, s in enumerate(slabs):
                dq_ref[:, s * SLAB:(s + 1) * SLAB] = dq[r * BLOCK:(r + 1) * BLOCK].astype(dq_ref.dtype)
            rk = _dot_tn(ds4, q)
            rv = _dot_tn(p4, do)
            for which in range(2):
                top = 2 * which * BLOCK
                dk[which][a] = jnp.where(lo, rk[top:top + BLOCK], rk[top + BLOCK:top + 2 * BLOCK])
                dv[which][a] = jnp.where(lo, rv[top:top + BLOCK], rv[top + BLOCK:top + 2 * BLOCK])
        dkb_ref[...] = _fold_pair(dk[0]) * ATTN_SCALE
        dka_ref[...] = _fold_pair(dk[1]) * ATTN_SCALE
        dvb_ref[...] = _fold_pair(dv[0])
        dva_ref[...] = _fold_pair(dv[1])

    kv_shape = jax.ShapeDtypeStruct((t, D_KV), F32)
    return pl.pallas_call(
        body, name="attn_bwd", grid=(N_KV_HEADS // KV_PAIR, t // BLOCK),
        in_specs=in_specs + [wide, wide],
        out_specs=[wide, wide, pair, pair, pair, pair, bias_spec, sink_out],
        out_shape=[jax.ShapeDtypeStruct((t, D_ATTN), MXU_DTYPE), jax.ShapeDtypeStruct((t, D_ATTN), MXU_DTYPE),
                   kv_shape, kv_shape, kv_shape, kv_shape,
                   jax.ShapeDtypeStruct(bias.shape, F32), jax.ShapeDtypeStruct((N_Q_HEADS, 8, 128), F32)],
        compiler_params=pltpu.CompilerParams(dimension_semantics=("parallel", "arbitrary")),
    )(sinks, proj, proj, proj, proj, proj, bias, proj, attn, dh)


def _shift_blocks(cur, prev):
    return cur + jnp.concatenate([prev[BLOCK:], jnp.zeros_like(prev[:BLOCK])], axis=0)


def _mesh_pos():
    return lax.axis_index("x"), lax.axis_index("y"), lax.axis_index("c")


def _all_gather(x, *, name):
    def body(x_ref, out_ref, send_sems, recv_sems, local_sem):
        x, y, c = _mesh_pos()
        me, sibling = (x, y, c), (x, y, 1 - c)
        chips = [(1 - x, y), (x, 1 - y), (1 - x, 1 - y)]

        def slot(px, py, pc):
            return out_ref.at[4 * px + 2 * py + pc]

        def copy(k, block, to, src=None):
            return pltpu.make_async_remote_copy(
                src_ref=slot(*block) if src is None else src, dst_ref=slot(*block),
                send_sem=send_sems.at[k], recv_sem=recv_sems.at[k],
                device_id=to, device_id_type=pl.DeviceIdType.MESH)

        mine = pltpu.make_async_copy(x_ref, slot(*me), local_sem)
        mine.start()
        first = [copy(0, me, sibling, src=x_ref)]
        first += [copy(1 + j, me, (*chip, c), src=x_ref) for j, chip in enumerate(chips)]
        for cp in first:
            cp.start()
        passed = [copy(4 + j, (*chip, c), sibling) for j, chip in enumerate(chips)]
        for j, chip in enumerate(chips):
            copy(1 + j, (*chip, c), me).wait_recv()
            passed[j].start()
        copy(0, sibling, me).wait_recv()
        for j, chip in enumerate(chips):
            copy(4 + j, (*chip, 1 - c), me).wait_recv()
        for cp in first + passed:
            cp.wait_send()
        mine.wait()

    return pl.pallas_call(
        body, name=name,
        in_specs=[pl.BlockSpec(memory_space=pl.ANY)],
        out_specs=pl.BlockSpec(memory_space=pl.ANY),
        out_shape=jax.ShapeDtypeStruct((N_DEV,) + x.shape, x.dtype),
        scratch_shapes=[pltpu.SemaphoreType.DMA((7,)), pltpu.SemaphoreType.DMA((7,)), pltpu.SemaphoreType.DMA],
    )(x)


_HBM = pl.BlockSpec(memory_space=pltpu.HBM)
_SEM = pl.BlockSpec(memory_space=pltpu.SEMAPHORE)
_DATAFLOW = pltpu.SideEffectType.DATAFLOW_SIDE_EFFECTING


def _peers():
    x, y, c = _mesh_pos()
    others = []
    for k in range(1, N_DEV):
        px, py, pc = x ^ (k >> 2), y ^ ((k >> 1) & 1), c ^ (k & 1)
        others.append(((px, py, pc), 4 * px + 2 * py + pc))
    return 4 * x + 2 * y + c, others


def _split_start(bufs, plan, n_copies, *, name):
    nb = len(bufs)

    def body(*refs):
        send_sems, recv_sems, token = refs[nb], refs[nb + 1], refs[-1]
        for k, (src, dst, pos, _) in enumerate(plan(*refs[:nb])):
            pltpu.make_async_remote_copy(src_ref=src, dst_ref=dst, send_sem=send_sems.at[k], recv_sem=recv_sems.at[k],
                                         device_id=pos, device_id_type=pl.DeviceIdType.MESH).start()
        token[...] = jnp.zeros_like(token)

    return pl.pallas_call(
        body, name=name,
        out_shape=(pltpu.SemaphoreType.DMA((n_copies,)), pltpu.SemaphoreType.DMA((n_copies,)),
                   *[pltpu.HBM(b.shape, b.dtype) for b in bufs], jax.ShapeDtypeStruct((8, 128), F32)),
        in_specs=(_HBM,) * nb, out_specs=(_SEM, _SEM) + (_HBM,) * nb + (pl.BlockSpec(memory_space=pltpu.VMEM),),
        input_output_aliases={i: 2 + i for i in range(nb)},
        compiler_params=pltpu.CompilerParams(has_side_effects=_DATAFLOW),
    )(*[pltpu.with_memory_space_constraint(b, pltpu.HBM) for b in bufs])


def _split_wait(started, plan, after, *, name):
    send_sems, recv_sems, *thru = started[:-1]
    nb = len(thru)

    def body(*refs):
        send_sems, recv_sems = refs[nb], refs[nb + 1]
        for k, (src, _, pos, arrive) in enumerate(plan(*refs[:nb])):
            copy = pltpu.make_async_remote_copy(
                src_ref=src, dst_ref=arrive, send_sem=send_sems.at[k], recv_sem=recv_sems.at[k],
                device_id=pos, device_id_type=pl.DeviceIdType.MESH)
            copy.wait_send()
            copy.wait_recv()

    return pl.pallas_call(
        body, name=name,
        out_shape=tuple(pltpu.HBM(b.shape, b.dtype) for b in thru),
        in_specs=(_HBM,) * nb + (_SEM, _SEM, pl.BlockSpec(memory_space=pl.ANY)), out_specs=(_HBM,) * nb,
        input_output_aliases={i: i for i in range(nb)},
        compiler_params=pltpu.CompilerParams(has_side_effects=_DATAFLOW),
    )(*thru, send_sems, recv_sems, after)


def _plan_scatter(x_ref, land_ref):
    me, others = _peers()
    return [(x_ref.at[idx], land_ref.at[me], pos, land_ref.at[idx]) for pos, idx in others]


def _plan_gather(x_ref, land_ref):
    me, others = _peers()
    return [(x_ref, land_ref.at[me], pos, land_ref.at[idx]) for pos, idx in others]


def _near_and_far():
    x, y, c = _mesh_pos()
    chips = [(1 - x, y), (x, 1 - y), (1 - x, 1 - y)]
    near = [(x, y, 1 - c)] + [(px, py, c) for px, py in chips]
    relay = [(4 * px + 2 * py + c, 4 * px + 2 * py + 1 - c) for px, py in chips]
    return 4 * x + 2 * y + c, near, (x, y, 1 - c), relay


def _plan_gather_near(x_ref, land_ref):
    me, near, _, _ = _near_and_far()
    return [(x_ref, land_ref.at[me], pos, land_ref.at[4 * pos[0] + 2 * pos[1] + pos[2]]) for pos in near]


def _plan_gather_relay(land_ref):
    _, _, sibling, relay = _near_and_far()
    return [(land_ref.at[mine], land_ref.at[mine], sibling, land_ref.at[theirs]) for mine, theirs in relay]


def _landing_zone(own):
    me, _ = _peers()
    return lax.dynamic_update_index_in_dim(lax.empty((N_DEV,) + own.shape, own.dtype), own, me, 0)


def _scatter_start(x, *, name):
    if x.ndim == 2:
        return _split_start((x, _landing_zone(x)), _plan_gather, N_DEV - 1, name=name)
    me, _ = _peers()
    own = lax.dynamic_index_in_dim(x, me, 0, keepdims=False)
    return _split_start((x, _landing_zone(own)), _plan_scatter, N_DEV - 1, name=name)


def _scatter_wait(started, after, *, name):
    plan = _plan_gather if started[2].ndim == 2 else _plan_scatter
    return _split_wait(started, plan, after, name=name)[1]


def _adamw_math(w, g, m, v):
    m = ADAM_B1 * m + (1.0 - ADAM_B1) * g
    v = ADAM_B2 * v + (1.0 - ADAM_B2) * (g * g)
    m_hat = m / (1.0 - ADAM_B1 ** ADAM_STEP)
    v_hat = v / (1.0 - ADAM_B2 ** ADAM_STEP)
    delta = -ADAM_LR * (m_hat / (jnp.sqrt(v_hat) + ADAM_EPS) + ADAM_WD * w)
    return delta, m, v


def _adamw_reduce(parts, w, m, v, *, name, tr):
    r, c = w.shape
    tr = min(tr, r)
    spec = pl.BlockSpec((tr, c), lambda i: (i, 0))

    def body(p_ref, w_ref, m_ref, v_ref, g_ref, d_ref, nm_ref, nv_ref):
        g = p_ref[0].astype(F32)
        for s in range(1, N_DEV):
            g = g + p_ref[s].astype(F32)
        delta, nm, nv = _adamw_math(w_ref[...], g, m_ref[...], v_ref[...])
        g_ref[...] = g
        d_ref[...] = delta
        nm_ref[...] = nm
        nv_ref[...] = nv

    return pl.pallas_call(
        body, name=name, grid=(r // tr,),
        in_specs=[pl.BlockSpec((N_DEV, tr, c), lambda i: (0, i, 0)), spec, spec, spec],
        out_specs=[spec] * 4,
        out_shape=[jax.ShapeDtypeStruct((r, c), F32)] * 4,
        compiler_params=pltpu.CompilerParams(
            dimension_semantics=("parallel",),
            vmem_limit_bytes=_vmem_limit(2 * 15 * _nbytes((tr, c), F32))),
    )(parts, w, m, v)


def _sum_parts(parts, *, name):
    def body(p_ref, o_ref):
        g = p_ref[0]
        for s in range(1, N_DEV):
            g = g + p_ref[s]
        o_ref[...] = g

    return pl.pallas_call(
        body, name=name, out_shape=jax.ShapeDtypeStruct(parts.shape[1:], F32),
        compiler_params=pltpu.CompilerParams(vmem_limit_bytes=_vmem_limit(_nbytes(parts.shape, F32))),
    )(parts)


def _adamw_native(w, g, m, v, *, name):
    def body(w_ref, g_ref, m_ref, v_ref, d_ref, nm_ref, nv_ref):
        d_ref[...], nm_ref[...], nv_ref[...] = _adamw_math(w_ref[...], g_ref[...], m_ref[...], v_ref[...])

    return pl.pallas_call(body, name=name, out_shape=[jax.ShapeDtypeStruct(w.shape, F32)] * 3)(w, g, m, v)


SMALL = ["ssm_lambda_re", "ssm_lambda_im", "ssm_b_re", "ssm_b_im", "ssm_c_re", "ssm_c_im", "ssm_d",
         "ssm_log_step", "attn_sinks", "rel_bias_table", "ln_gain", "ln_bias"]


def _pack(arrs):
    flat = jnp.concatenate([a.reshape(-1) for a in arrs])
    pad = (-flat.shape[0]) % 1024
    return jnp.pad(flat, (0, pad)).reshape(-1, 128)


def _unpack(packed, like):
    flat = packed.reshape(-1)
    out, pos = [], 0
    for a in like:
        out.append(flat[pos:pos + a.size].reshape(a.shape))
        pos += a.size
    return out


def kernel(x, w_in, ssm_lambda_re, ssm_lambda_im, ssm_b_re, ssm_b_im, ssm_c_re, ssm_c_im, ssm_d, ssm_log_step, w_glu, attn_sinks, rel_bias_table, w_branch_ssm, w_branch_attn, w_out, ln_gain, ln_bias, loss_target, m_w_in, m_ssm_lambda_re, m_ssm_lambda_im, m_ssm_b_re, m_ssm_b_im, m_ssm_c_re, m_ssm_c_im, m_ssm_d, m_ssm_log_step, m_w_glu, m_attn_sinks, m_rel_bias_table, m_w_branch_ssm, m_w_branch_attn, m_w_out, m_ln_gain, m_ln_bias, v_w_in, v_ssm_lambda_re, v_ssm_lambda_im, v_ssm_b_re, v_ssm_b_im, v_ssm_c_re, v_ssm_c_im, v_ssm_d, v_ssm_log_step, v_w_glu, v_attn_sinks, v_rel_bias_table, v_w_branch_ssm, v_w_branch_attn, v_w_out, v_ln_gain, v_ln_bias):
    t = x.shape[1]
    xs = x[0]
    target = loss_target[0]
    col_in = w_in.shape[2]
    col_br = w_glu.shape[2]
    row_out = w_out.shape[1]

    w_in_mx = w_in[0].astype(MXU_DTYPE)
    near = _split_start((w_in_mx, _landing_zone(w_in_mx)), _plan_gather_near, 4, name="gather_w_in_near_start")
    ssm_params = (ssm_lambda_re[0], ssm_lambda_im[0], ssm_b_re[0], ssm_b_im[0], ssm_c_re[0], ssm_c_im[0],
                  ssm_log_step[0] + near[-1][0, 0])
    mats, mats_vjp = jax.vjp(_ssm_matrices, *ssm_params)
    mats_mx = _ssm_expand(*mats[:3]) + (mats[3],)
    sinks = attn_sinks[0]
    d_skip = ssm_d
    _, landed = _split_wait(near, _plan_gather_near, mats_mx[1], name="gather_w_in_near_wait")
    relay = _split_start((landed,), _plan_gather_relay, 3, name="gather_w_in_relay_start")
    bias, bias_vjp = jax.vjp(_band_bias, rel_bias_table + relay[-1][0, 0])
    x_mx = xs.astype(MXU_DTYPE)
    (g_in,) = _split_wait(relay, _plan_gather_relay, bias, name="gather_w_in_relay_wait")
    win = g_in.transpose(1, 0, 2).reshape(D_MODEL, D_IN)
    hold = g_in[0, 0, 0] * 0
    three = jnp.concatenate([w_glu[0], w_branch_ssm[0], w_branch_attn[0]], axis=0).astype(MXU_DTYPE)
    sent_three = _scatter_start(three + hold, name="gather_w_1024_start")
    sent_wout = _scatter_start(w_out[0].astype(MXU_DTYPE) + hold, name="gather_w_out_start")

    proj = _mm(x_mx, win, name="in_proj", tm=2048, tn=512, tk=2048, after=(sent_three[4], sent_wout[4]))
    y_conv, states = _ssm_forward(proj, mats_mx)

    def f_gelu(yv, u, d):
        ys = yv + d * u
        return ys, _gelu(ys)

    y_ssm, glu_in = _ew(f_gelu, [(y_conv, "row", 0), (proj, "row", OFF_U), (d_skip, "vec", 0)],
                        [(COL, F32, "row"), (COL, MXU_DTYPE, "row")], rows=t, cw=COL, ncb=4, tr=1024, name="ssm_gelu")
    g_three = _scatter_wait(sent_three, glu_in, name="gather_w_1024_wait")
    three_full = g_three.transpose(1, 0, 2).reshape(3 * D_SSM, N_DEV * col_br)
    wglu, wbs, wba = three_full[:D_SSM], three_full[D_SSM:2 * D_SSM], three_full[2 * D_SSM:]
    gate_tn = 512
    z_ssm_col = [(proj, OFF_ZS * COL // gate_tn)]

    def f_hssm(products, gate):
        (ga, gb), (z,) = products, gate
        return ga, gb, ga * _sigmoid(gb) * _silu(z)

    glu_a, glu_b, h_ssm = _mm_fused([(glu_in, wglu[:, :D_SSM], False), (glu_in, wglu[:, D_SSM:], False)], z_ssm_col,
                                    [F32, F32, MXU_DTYPE], f_hssm, name="glu_gate", tm=1024, tn=gate_tn)

    attn, h_attn = _attn_forward(proj, bias, sinks)
    gate_cols = [(proj, OFF_GS * COL // gate_tn), (proj, OFF_GA * COL // gate_tn)]

    def f_merge(products, gates):
        (ps, pa), (ls, la) = products, gates
        return ps, pa, _sigmoid(ls) * ps + _sigmoid(la) * pa

    p_ssm, p_attn, merged = _mm_fused([(h_ssm, wbs, False), (h_attn, wba, False)], gate_cols, [F32, F32, MXU_DTYPE],
                                      f_merge, name="branch_merge", tm=1024, tn=gate_tn)
    wout = _scatter_wait(sent_wout, merged, name="gather_w_out_wait").reshape(D_MODEL, D_MODEL)
    out = _mm(merged, wout, name="out_proj", tm=2048, tn=512, tk=2048)

    def f_norm(xv, ov, tg, gain, lbias):
        r = DEEPNORM_ALPHA * xv + ov
        mu = jnp.mean(r, axis=1, keepdims=True)
        cen = r - mu
        var = jnp.mean(cen * cen, axis=1, keepdims=True)
        rstd = lax.rsqrt(var + LN_EPS)
        xhat = cen * rstd
        yv = xhat * gain + lbias
        diff = yv - tg
        row_loss = 0.5 * jnp.mean(diff * diff, axis=1, keepdims=True)
        loss = jnp.broadcast_to(jnp.sum(row_loss, axis=0, keepdims=True), (1, 128))
        dy = diff * (1.0 / D_MODEL)
        dgain = jnp.sum(dy * xhat, axis=0, keepdims=True)
        dbias = jnp.sum(dy, axis=0, keepdims=True)
        dxh = dy * gain
        dr = rstd * (dxh - jnp.mean(dxh, axis=1, keepdims=True) - xhat * jnp.mean(dxh * xhat, axis=1, keepdims=True))
        return dr, loss, dgain, dbias

    dr, loss_part, g_ln_gain, g_ln_bias = _ew(
        f_norm, [(xs, "row", 0), (out, "row", 0), (target, "row", 0), (ln_gain, "vec", 0), (ln_bias, "vec", 0)],
        [(D_MODEL, F32, "row"), (128, F32, "acc"), (D_MODEL, F32, "acc"), (D_MODEL, F32, "acc")],
        rows=t, cw=D_MODEL, ncb=1, tr=256, name="norm_loss")

    def scatter_cols(g, cols):
        return g.reshape(g.shape[0], N_DEV, cols).transpose(1, 0, 2)

    gw_out = _mm(merged, dr, ta=True, out_dtype=WIRE_DTYPE, name="grad_w_out", tm=2048, tn=512, tk=2048)
    sent_out = _scatter_start(gw_out.reshape(N_DEV, row_out, D_MODEL), name="scatter_g_out_start")
    def b_merge(products, tiles):
        (dm,), (ps, pa, ls, la) = products, tiles
        gs, ga = _sigmoid(ls), _sigmoid(la)
        return dm * gs, dm * ga, dm * ps * gs * (1.0 - gs), dm * pa * ga * (1.0 - ga)

    dp_ssm, dp_attn, dgl_s, dgl_a = _mm_fused(
        [(dr, wout, True)], [(p_ssm, 0), (p_attn, 0)] + gate_cols, [MXU_DTYPE] * 4, b_merge,
        name="merge_bwd", tm=512, tn=gate_tn, after=(sent_out[4],))
    gw_bs = _mm(h_ssm, dp_ssm, ta=True, out_dtype=WIRE_DTYPE, name="grad_w_branch_ssm", tm=1024, tn=512, tk=2048)
    gw_ba = _mm(h_attn, dp_attn, ta=True, out_dtype=WIRE_DTYPE, name="grad_w_branch_attn", tm=1024, tn=512, tk=2048)
    dh_attn = _mm(dp_attn, wba, tb=True, name="d_h_attn", tm=2048, tn=512, tk=2048)

    def b_hssm(products, tiles):
        (dh,), (ga, gb, z) = products, tiles
        sg = _sigmoid(gb)
        dgate = dh * _silu(z)
        return dgate * sg, dgate * ga * sg * (1.0 - sg), dh * ga * sg * _silu_grad(z)

    dglu_a, dglu_b, dz_ssm = _mm_fused([(dp_ssm, wbs, True)], [(glu_a, 0), (glu_b, 0)] + z_ssm_col, [MXU_DTYPE] * 3,
                                       b_hssm, name="glu_gate_bwd", tm=1024, tn=gate_tn)
    dglu = jnp.concatenate([dglu_a, dglu_b], axis=1)
    gw_glu = _mm(glu_in, dglu, ta=True, out_dtype=WIRE_DTYPE, name="grad_w_glu", tm=1024, tn=512, tk=2048)
    sent_three = _scatter_start(scatter_cols(jnp.concatenate([gw_glu, gw_bs, gw_ba], axis=0), col_br),
                                name="scatter_g_1024_start")
    dglu_in = _mm(dglu, wglu, tb=True, name="d_glu_in", tm=2048, tn=512, tk=2048, after=(sent_three[4],))

    def b_gelu(dgi, ys, u):
        dys = dgi * _gelu_grad(ys)
        return dys, jnp.sum(dys * u, axis=0, keepdims=True)

    dy_ssm, g_ssm_d = _ew(b_gelu, [(dglu_in, "row", 0), (y_ssm, "row", 0), (proj, "row", OFF_U)],
                          [(COL, F32, "row"), (COL, F32, "acc")],
                          rows=t, cw=COL, ncb=4, tr=1024, name="ssm_gelu_bwd")
    du_ssm, dmats = _ssm_backward(dy_ssm, proj, states, mats_mx)
    du = (du_ssm + d_skip * dy_ssm).astype(MXU_DTYPE)
    g_lre, g_lim, g_bre, g_bim, g_cre, g_cim, g_lstep = mats_vjp(dmats)

    dq, dz_attn, dka, dkb, dva, dvb, dbias, dsink = _attn_backward(proj, bias, sinks, attn, dh_attn)
    dk = _shift_blocks(dka, dkb).astype(MXU_DTYPE)
    dv = _shift_blocks(dva, dvb).astype(MXU_DTYPE)
    (g_table,) = bias_vjp(dbias)
    g_sinks = dsink[:, 0, 0]

    dproj = jnp.concatenate([du, dz_ssm, dq, dk, dv, dz_attn, dgl_s, dgl_a], axis=1)
    gw_in = _mm(x_mx, dproj, ta=True, out_dtype=WIRE_DTYPE, name="grad_w_in", tm=2048, tn=512, tk=2048)
    sent_in = _scatter_start(scatter_cols(gw_in, col_in), name="scatter_g_in_start")
    grad_x = _mm(dproj, win, tb=True, add=dr, add_scale=DEEPNORM_ALPHA, name="grad_x", tm=1024, tn=512, tk=4352,
                 after=(sent_in[4],))

    parts_out = _scatter_wait(sent_out, grad_x, name="scatter_g_out_wait")
    parts_three = _scatter_wait(sent_three, parts_out, name="scatter_g_1024_wait")
    parts_in = _scatter_wait(sent_in, parts_three, name="scatter_g_in_wait")

    o_in = _adamw_reduce(parts_in, w_in[0], m_w_in[0], v_w_in[0], name="adamw_w_in", tr=128)
    three_w = jnp.concatenate([w_glu[0], w_branch_ssm[0], w_branch_attn[0]], axis=0)
    three_m = jnp.concatenate([m_w_glu[0], m_w_branch_ssm[0], m_w_branch_attn[0]], axis=0)
    three_v = jnp.concatenate([v_w_glu[0], v_w_branch_ssm[0], v_w_branch_attn[0]], axis=0)
    o_three = _adamw_reduce(parts_three, three_w, three_m, three_v, name="adamw_w_1024", tr=512)
    o_out = _adamw_reduce(parts_out, w_out[0], m_w_out[0], v_w_out[0], name="adamw_w_out", tr=128)

    small_w = [ssm_lambda_re, ssm_lambda_im, ssm_b_re, ssm_b_im, ssm_c_re, ssm_c_im, ssm_d, ssm_log_step,
               attn_sinks, rel_bias_table, ln_gain, ln_bias]
    small_m = [m_ssm_lambda_re, m_ssm_lambda_im, m_ssm_b_re, m_ssm_b_im, m_ssm_c_re, m_ssm_c_im, m_ssm_d,
               m_ssm_log_step, m_attn_sinks, m_rel_bias_table, m_ln_gain, m_ln_bias]
    small_v = [v_ssm_lambda_re, v_ssm_lambda_im, v_ssm_b_re, v_ssm_b_im, v_ssm_c_re, v_ssm_c_im, v_ssm_d,
               v_ssm_log_step, v_attn_sinks, v_rel_bias_table, v_ln_gain, v_ln_bias]
    small_g = [g_lre, g_lim, g_bre, g_bim, g_cre, g_cim, g_ssm_d, g_lstep, g_sinks, g_table, g_ln_gain, g_ln_bias]
    parts_small = _all_gather(_pack(small_g), name="gather_g_small")
    sg = _unpack(_sum_parts(parts_small, name="sum_g_small"), small_w)
    updates = [_adamw_native(w, g, m, v, name="adamw_" + n)
               for n, w, g, m, v in zip(SMALL, small_w, sg, small_m, small_v)]
    sd, sm, sv = zip(*updates)

    loss = lax.psum(loss_part[0, 0], MESH_AXES)

    def big(o, idx):
        g_in_, g_three_, g_out_ = o_in[idx], o_three[idx], o_out[idx]
        return {"w_in": g_in_[None], "w_glu": g_three_[None, :D_SSM], "w_branch_ssm": g_three_[None, D_SSM:2 * D_SSM],
                "w_branch_attn": g_three_[None, 2 * D_SSM:], "w_out": g_out_[None]}

    order = ["w_in", "ssm_lambda_re", "ssm_lambda_im", "ssm_b_re", "ssm_b_im", "ssm_c_re", "ssm_c_im", "ssm_d",
             "ssm_log_step", "w_glu", "attn_sinks", "rel_bias_table", "w_branch_ssm", "w_branch_attn", "w_out",
             "ln_gain", "ln_bias"]
    outs = [loss, grad_x[None]]
    for idx, small in enumerate([sg, sd, sm, sv]):
        table = big(None, idx)
        table.update(dict(zip(SMALL, small)))
        outs += [table[n] for n in order]
    return tuple(outs)
```

```python
import math

import jax
import jax.numpy as jnp
from jax import lax
from jax.experimental import pallas as pl
from jax.experimental.pallas import tpu as pltpu

F32 = jnp.float32
MXU_DTYPE = jnp.bfloat16
WIRE_DTYPE = jnp.bfloat16

D_MODEL = 2048
D_SSM = 1024
SSM_GROUP = 16
N_GROUPS = 64
SSM_STATE = 64
N_Q_HEADS = 16
N_KV_HEADS = 4
Q_PER_KV = 4
HEAD_DIM = 64
D_ATTN = 1024
D_KV = 256
WINDOW = 128
BLOCK = 128
N_BUCKETS = 32
MAX_DISTANCE = 128
D_IN = 8704
DEEPNORM_ALPHA = 2.0 ** 0.25
LN_EPS = 1e-5
NEG_INF = -1e30
ATTN_SCALE = HEAD_DIM ** -0.5

ADAM_LR = 0.001
ADAM_B1 = 0.9
ADAM_B2 = 0.999
ADAM_EPS = 1e-08
ADAM_WD = 0.01
ADAM_STEP = 10

N_DEV = 8
SSM_CHUNK = 16
GROUP_BLOCK = 8
N_GB = N_GROUPS // GROUP_BLOCK
GB_LANES = GROUP_BLOCK * SSM_GROUP
GB_STATE = GROUP_BLOCK * SSM_STATE
COL = 256
OFF_U, OFF_ZS, OFF_Q, OFF_K, OFF_V, OFF_ZA, OFF_GS, OFF_GA = 0, 4, 8, 12, 13, 14, 18, 26

VMEM_CAP = 56 * 1024 * 1024
MESH_AXES = ("x", "y", "c")


def _vmem_limit(block_bytes):
    return int(min(max(3 * block_bytes, 16 * 1024 * 1024), VMEM_CAP))


def _nbytes(shape, dtype):
    return math.prod(shape) * jnp.dtype(dtype).itemsize


def _tile(n, pref):
    if n <= pref:
        return n
    t = (pref // 128) * 128
    while t >= 128:
        if n % t == 0:
            return t
        t -= 128
    return n


def _mm(a, b, *, name, ta=False, tb=False, out_dtype=F32, tm=1024, tn=512, tk=512, add=None, add_scale=1.0,
        after=()):
    squeeze = a.ndim == 2
    if squeeze:
        a, b = a[None], b[None]
        if add is not None:
            add = add[None]
    nb = a.shape[0]
    m, k = (a.shape[2], a.shape[1]) if ta else (a.shape[1], a.shape[2])
    n = b.shape[1] if tb else b.shape[2]
    tm, tn, tk = _tile(m, tm), _tile(n, tn), _tile(k, tk)
    nk = k // tk
    dn = (((0 if ta else 1,), (1 if tb else 0,)), ((), ()))

    a_spec = (pl.BlockSpec((None, tk, tm), lambda g, i, j, kk: (g, kk, i)) if ta
              else pl.BlockSpec((None, tm, tk), lambda g, i, j, kk: (g, i, kk)))
    b_spec = (pl.BlockSpec((None, tn, tk), lambda g, i, j, kk: (g, j, kk)) if tb
              else pl.BlockSpec((None, tk, tn), lambda g, i, j, kk: (g, kk, j)))
    o_spec = pl.BlockSpec((None, tm, tn), lambda g, i, j, kk: (g, i, j))
    in_specs = [a_spec, b_spec]
    operands = [a, b]
    if add is not None:
        in_specs.append(o_spec)
        operands.append(add)
    for tok in after:
        in_specs.append(pl.BlockSpec(memory_space=pl.ANY))
        operands.append(tok)
    n_in = len(operands)

    def body(*refs):
        a_ref, b_ref = refs[0], refs[1]
        add_ref = refs[2] if add is not None else None
        o_ref = refs[n_in]
        acc_ref = refs[-1]
        kk = pl.program_id(3)
        part = lax.dot_general(a_ref[...].astype(MXU_DTYPE), b_ref[...].astype(MXU_DTYPE), dn,
                               preferred_element_type=F32)

        def finish(r):
            if add_ref is not None:
                r = r + add_scale * add_ref[...]
            o_ref[...] = r.astype(out_dtype)

        if nk == 1:
            finish(part)
            return

        @pl.when(kk == 0)
        def _():
            acc_ref[...] = part

        @pl.when(jnp.logical_and(kk > 0, kk < nk - 1))
        def _():
            acc_ref[...] += part

        @pl.when(kk == nk - 1)
        def _():
            finish(acc_ref[...] + part)

    blocks = (_nbytes((tm, tk), a.dtype) + _nbytes((tk, tn), b.dtype) + _nbytes((tm, tn), out_dtype)
              + (_nbytes((tm, tn), F32) if add is not None else 0))
    out = pl.pallas_call(
        body,
        name=name,
        grid=(nb, m // tm, n // tn, nk),
        in_specs=in_specs,
        out_specs=o_spec,
        out_shape=jax.ShapeDtypeStruct((nb, m, n), out_dtype),
        scratch_shapes=[pltpu.VMEM((tm, tn), F32)],
        compiler_params=pltpu.CompilerParams(
            dimension_semantics=("parallel", "parallel", "parallel", "arbitrary"),
            vmem_limit_bytes=_vmem_limit(2 * blocks + 2 * _nbytes((tm, tn), F32))),
    )(*operands)
    return out[0] if squeeze else out


def _mm_fused(pairs, extras, out_dtypes, epilogue, *, name, tm, tn, after=()):
    m = pairs[0][0].shape[0]
    n = pairs[0][1].shape[0] if pairs[0][2] else pairs[0][1].shape[1]
    tm, tn = _tile(m, tm), _tile(n, tn)
    in_specs, operands, dns = [], [], []
    for a, b, tb in pairs:
        k = a.shape[1]
        in_specs += [pl.BlockSpec((tm, k), lambda i, j: (i, 0)),
                     pl.BlockSpec((tn, k), lambda i, j: (j, 0)) if tb else pl.BlockSpec((k, tn), lambda i, j: (0, j))]
        operands += [a, b]
        dns.append((((1,), (1 if tb else 0,)), ((), ())))

    def tile_at(col):
        return pl.BlockSpec((tm, tn), lambda i, j: (i, col + j))

    in_specs += [tile_at(col) for _, col in extras]
    operands += [arr for arr, _ in extras]
    in_specs += [pl.BlockSpec(memory_space=pl.ANY)] * len(after)
    operands += list(after)
    n_pairs, n_extra, n_in = len(pairs), len(extras), len(operands)

    def body(*refs):
        products = [lax.dot_general(refs[2 * p][...].astype(MXU_DTYPE), refs[2 * p + 1][...].astype(MXU_DTYPE),
                                    dns[p], preferred_element_type=F32) for p in range(n_pairs)]
        tiles = [refs[2 * n_pairs + e][...] for e in range(n_extra)]
        for o_ref, val in zip(refs[n_in:], epilogue(products, tiles)):
            o_ref[...] = val.astype(o_ref.dtype)

    blocks = sum(_nbytes((tm, a.shape[1]), a.dtype) + _nbytes((tn, a.shape[1]), b.dtype) for a, b, _ in pairs)
    blocks += sum(_nbytes((tm, tn), arr.dtype) for arr, _ in extras) + sum(_nbytes((tm, tn), dt) for dt in out_dtypes)
    return pl.pallas_call(
        body, name=name, grid=(m // tm, n // tn), in_specs=in_specs,
        out_specs=[tile_at(0)] * len(out_dtypes),
        out_shape=[jax.ShapeDtypeStruct((m, n), dt) for dt in out_dtypes],
        compiler_params=pltpu.CompilerParams(
            dimension_semantics=("parallel", "parallel"),
            vmem_limit_bytes=_vmem_limit(2 * blocks + n_pairs * _nbytes((tm, tn), F32))),
    )(*operands)


def _ew(fn, ins, outs, *, rows, cw, ncb, tr, name):
    tr = min(tr, rows)
    n_in = len(ins)

    def row_map(off):
        return lambda j, i: (i, off + j)

    def vec_map(off):
        return lambda j, i: (0, off + j)

    in_specs = []
    for arr, kind, off in ins:
        if kind == "row":
            in_specs.append(pl.BlockSpec((tr, cw), row_map(off)))
        else:
            in_specs.append(pl.BlockSpec((1, cw), vec_map(off)))
    out_specs, out_shapes = [], []
    for bw, dt, kind in outs:
        if kind == "row":
            out_specs.append(pl.BlockSpec((tr, bw), row_map(0)))
            out_shapes.append(jax.ShapeDtypeStruct((rows, ncb * bw), dt))
        else:
            out_specs.append(pl.BlockSpec((1, bw), vec_map(0)))
            out_shapes.append(jax.ShapeDtypeStruct((1, ncb * bw), F32))

    def body(*refs):
        i = pl.program_id(1)
        vals = fn(*[r[...] for r in refs[:n_in]])
        for r, (bw, dt, kind), v in zip(refs[n_in:], outs, vals):
            if kind == "row":
                r[...] = v.astype(dt)
            else:
                @pl.when(i == 0)
                def _(r=r):
                    r[...] = jnp.zeros_like(r)

                r[...] += v

    blocks = sum(_nbytes((tr, cw), a.dtype) for a, kind, _ in ins if kind == "row")
    blocks += sum(_nbytes((tr, bw), dt) for bw, dt, kind in outs if kind == "row")
    res = pl.pallas_call(
        body,
        name=name,
        grid=(ncb, rows // tr),
        in_specs=in_specs,
        out_specs=out_specs,
        out_shape=out_shapes,
        compiler_params=pltpu.CompilerParams(
            dimension_semantics=("parallel", "arbitrary"),
            vmem_limit_bytes=_vmem_limit(4 * blocks)),
    )(*[a for a, _, _ in ins])
    return res


def _sigmoid(x):
    return 1.0 / (1.0 + jnp.exp(-x))


INV_SQRT2 = 0.7071067811865476
INV_SQRT_2PI = 0.3989422804014327


def _gelu(x):
    return 0.5 * x * (1.0 + lax.erf(x * INV_SQRT2))


def _gelu_grad(x):
    return 0.5 * (1.0 + lax.erf(x * INV_SQRT2)) + x * INV_SQRT_2PI * jnp.exp(-0.5 * x * x)


def _silu(x):
    return x * _sigmoid(x)


def _silu_grad(x):
    s = _sigmoid(x)
    return s * (1.0 + x * (1.0 - s))


@jax.custom_vjp
def _taps_product(mr, mi, bbr, bbi):
    bbr_t, bbi_t = jnp.transpose(bbr, (0, 2, 1)), jnp.transpose(bbi, (0, 2, 1))
    return jnp.sum(mr[..., None, :] * bbr_t[None, :, None] - mi[..., None, :] * bbi_t[None, :, None], axis=-1)


def _taps_product_fwd(mr, mi, bbr, bbi):
    return _taps_product(mr, mi, bbr, bbi), (mr, mi, bbr, bbi)


def _taps_product_bwd(res, g):
    mr, mi, bbr, bbi = res
    hi = lax.Precision.HIGHEST
    return (jnp.einsum("tghk,gpk->tghp", g, bbr, precision=hi), -jnp.einsum("tghk,gpk->tghp", g, bbi, precision=hi),
            jnp.einsum("tghk,tghp->gpk", g, mr, precision=hi), -jnp.einsum("tghk,tghp->gpk", g, mi, precision=hi))


_taps_product.defvjp(_taps_product_fwd, _taps_product_bwd)


def _ssm_matrices(lam_re, lam_im, b_re, b_im, c_re, c_im, log_step):
    L = SSM_CHUNK
    step = jnp.exp(log_step)[:, None]
    ea, eb = lam_re * step, lam_im * step
    mag = jnp.exp(ea)
    lbr, lbi = mag * jnp.cos(eb), mag * jnp.sin(eb)
    den = lam_re * lam_re + lam_im * lam_im
    nr, ni = lbr - 1.0, lbi
    cr = (nr * lam_re + ni * lam_im) / den
    ci = (ni * lam_re - nr * lam_im) / den
    bbr = cr[..., None] * b_re - ci[..., None] * b_im
    bbi = cr[..., None] * b_im + ci[..., None] * b_re
    taus = jnp.arange(L + 1, dtype=F32)[:, None, None]
    pmag = jnp.exp(taus * ea[None])
    pwr, pwi = pmag * jnp.cos(taus * eb[None]), pmag * jnp.sin(taus * eb[None])
    mr = c_re[None] * pwr[:L, :, None, :] - c_im[None] * pwi[:L, :, None, :]
    mi = c_re[None] * pwi[:L, :, None, :] + c_im[None] * pwr[:L, :, None, :]
    kk = _taps_product(mr, mi, bbr, bbi)
    taps = jnp.transpose(kk.reshape(L, N_GB, GROUP_BLOCK, SSM_GROUP, SSM_GROUP), (1, 0, 2, 4, 3))
    taps = taps.reshape(N_GB, L, GB_LANES, SSM_GROUP)
    rev_r, rev_i = pwr[L - 1 - jnp.arange(L)], pwi[L - 1 - jnp.arange(L)]
    wer = rev_r[..., None] * bbr[None] - rev_i[..., None] * bbi[None]
    wei = rev_r[..., None] * bbi[None] + rev_i[..., None] * bbr[None]

    def rows_in(w):
        w = jnp.transpose(w.reshape(L, N_GB, GROUP_BLOCK, SSM_STATE, SSM_GROUP), (1, 0, 2, 4, 3))
        return w.reshape(N_GB, L * GB_LANES, SSM_STATE)

    wend = jnp.concatenate([rows_in(wer), rows_in(wei)], axis=2)
    m1r = c_re[None] * pwr[1:, :, None, :] - c_im[None] * pwi[1:, :, None, :]
    m1i = c_re[None] * pwi[1:, :, None, :] + c_im[None] * pwr[1:, :, None, :]

    def rows_out(m):
        m = jnp.transpose(m.reshape(L, N_GB, GROUP_BLOCK, SSM_GROUP, SSM_STATE), (1, 2, 4, 0, 3))
        return m.reshape(N_GB, GB_STATE, L * SSM_GROUP)

    wout = jnp.concatenate([rows_out(m1r), rows_out(-m1i)], axis=1)
    acat = jnp.concatenate([pwr[L].reshape(N_GB, 1, GB_STATE), pwi[L].reshape(N_GB, 1, GB_STATE)], axis=2)
    return taps, wend, wout, acat


def _lane_group(shape, axis, shift):
    return (lax.broadcasted_iota(jnp.int32, shape, axis) >> shift) & (GROUP_BLOCK - 1)


def _ssm_expand(taps, wend, wout):
    L = SSM_CHUNK
    taps = jnp.pad(taps, ((0, 0), (0, 0), (0, 0), (0, GB_LANES - SSM_GROUP)))

    def body(t_ref, we_ref, wo_ref, d_ref, web_ref, wob_ref):
        def rc(shape):
            return lax.broadcasted_iota(jnp.int32, shape, 0), lax.broadcasted_iota(jnp.int32, shape, 1)

        r, c = rc((GB_LANES, GB_LANES))
        spread = ((r < SSM_GROUP) & (r == (c & (SSM_GROUP - 1)))).astype(MXU_DTYPE)
        same = _lane_group((GB_LANES, GB_LANES), 0, 4) == _lane_group((GB_LANES, GB_LANES), 1, 4)
        for tau in range(L):
            full = jnp.dot(t_ref[tau].astype(MXU_DTYPE), spread, preferred_element_type=F32)
            d_ref[tau] = jnp.where(same, full, 0.0).astype(d_ref.dtype)
        r, c = rc((2 * SSM_STATE, STATE_W))
        part = (r >> 6) == (c >> 9)
        spread = (part & ((r & (SSM_STATE - 1)) == (c & (SSM_STATE - 1)))).astype(MXU_DTYPE)
        keep = _lane_group((GB_LANES, STATE_W), 0, 4) == _lane_group((GB_LANES, STATE_W), 1, 6)
        for j in range(L):
            rows = slice(j * GB_LANES, (j + 1) * GB_LANES)
            full = jnp.dot(we_ref[rows, :].astype(MXU_DTYPE), spread, preferred_element_type=F32)
            web_ref[rows, :] = jnp.where(keep, full, 0.0).astype(web_ref.dtype)
        r, c = rc((GB_LANES, GB_LANES))
        own = _lane_group((STATE_W, GB_LANES), 0, 6) == _lane_group((STATE_W, GB_LANES), 1, 4)
        for tt in range(L):
            half, k = tt // GROUP_BLOCK, tt % GROUP_BLOCK
            spread = (((r >> 4) == k) & ((r & (SSM_GROUP - 1)) == (c & (SSM_GROUP - 1)))).astype(MXU_DTYPE)
            src = wo_ref[:, half * GB_LANES:(half + 1) * GB_LANES].astype(MXU_DTYPE)
            full = jnp.dot(src, spread, preferred_element_type=F32)
            wob_ref[:, tt * GB_LANES:(tt + 1) * GB_LANES] = jnp.where(own, full, 0.0).astype(wob_ref.dtype)

    def spec(shape):
        return pl.BlockSpec((None,) + shape[1:], lambda b: (b,) + (0,) * (len(shape) - 1))

    out_shapes = [(N_GB, L, GB_LANES, GB_LANES), (N_GB, L * GB_LANES, STATE_W), (N_GB, STATE_W, L * GB_LANES)]
    return tuple(pl.pallas_call(
        body, name="ssm_expand", grid=(N_GB,),
        in_specs=[spec(taps.shape), spec(wend.shape), spec(wout.shape)],
        out_specs=[spec(s) for s in out_shapes],
        out_shape=[jax.ShapeDtypeStruct(s, MXU_DTYPE) for s in out_shapes],
        compiler_params=pltpu.CompilerParams(dimension_semantics=("parallel",), vmem_limit_bytes=VMEM_CAP),
    )(taps, wend, wout))


def _dot(a, b):
    return jnp.dot(a.astype(MXU_DTYPE), b.astype(MXU_DTYPE), preferred_element_type=F32)


def _dot_nt(a, b):
    return lax.dot_general(a.astype(MXU_DTYPE), b.astype(MXU_DTYPE), (((1,), (1,)), ((), ())),
                           preferred_element_type=F32)


def _dot_tn(a, b):
    return lax.dot_general(a.astype(MXU_DTYPE), b.astype(MXU_DTYPE), (((0,), (0,)), ((), ())),
                           preferred_element_type=F32)


STATE_W = 2 * GB_STATE


def _chunk_scan(e, acat):
    nc = e.shape[0]
    spec = pl.BlockSpec((nc, STATE_W), lambda b: (0, b))
    aspec = pl.BlockSpec((None, 1, STATE_W), lambda b: (b, 0, 0))

    def body(e_ref, a_ref, s_ref):
        a_r, a_i = a_ref[:, :GB_STATE], a_ref[:, GB_STATE:]

        def step(c, carry):
            s_r, s_i = carry
            s_ref[pl.ds(c, 1), :GB_STATE] = s_r
            s_ref[pl.ds(c, 1), GB_STATE:] = s_i
            e_r = e_ref[pl.ds(c, 1), :GB_STATE]
            e_i = e_ref[pl.ds(c, 1), GB_STATE:]
            return (a_r * s_r - a_i * s_i + e_r, a_r * s_i + a_i * s_r + e_i)

        zero = jnp.zeros((1, GB_STATE), F32)
        lax.fori_loop(0, nc, step, (zero, zero))

    return pl.pallas_call(
        body, name="ssm_chunk_scan", grid=(N_GB,),
        in_specs=[spec, aspec], out_specs=spec,
        out_shape=jax.ShapeDtypeStruct(e.shape, F32),
        compiler_params=pltpu.CompilerParams(dimension_semantics=("parallel",)),
    )(e, acat)


def _chunk_scan_bwd(ds, s, acat):
    nc = ds.shape[0]
    spec = pl.BlockSpec((nc, STATE_W), lambda b: (0, b))
    aspec = pl.BlockSpec((None, 1, STATE_W), lambda b: (b, 0, 0))

    def body(ds_ref, s_ref, a_ref, ge_ref, da_ref):
        a_r, a_i = a_ref[:, :GB_STATE], a_ref[:, GB_STATE:]

        def step(t, carry):
            g_r, g_i, d_r, d_i = carry
            c = nc - 1 - t
            ge_ref[pl.ds(c, 1), :GB_STATE] = g_r
            ge_ref[pl.ds(c, 1), GB_STATE:] = g_i
            s_r = s_ref[pl.ds(c, 1), :GB_STATE]
            s_i = s_ref[pl.ds(c, 1), GB_STATE:]
            d_r = d_r + g_r * s_r + g_i * s_i
            d_i = d_i + g_i * s_r - g_r * s_i
            n_r = ds_ref[pl.ds(c, 1), :GB_STATE] + a_r * g_r + a_i * g_i
            n_i = ds_ref[pl.ds(c, 1), GB_STATE:] + a_r * g_i - a_i * g_r
            return (n_r, n_i, d_r, d_i)

        zero = jnp.zeros((1, GB_STATE), F32)
        _, _, d_r, d_i = lax.fori_loop(0, nc, step, (zero, zero, zero, zero))
        da_ref[:, :GB_STATE] = d_r
        da_ref[:, GB_STATE:] = d_i

    return pl.pallas_call(
        body, name="ssm_chunk_scan_bwd", grid=(N_GB,),
        in_specs=[spec, spec, aspec], out_specs=[spec, aspec],
        out_shape=[jax.ShapeDtypeStruct(ds.shape, F32), jax.ShapeDtypeStruct(acat.shape, F32)],
        compiler_params=pltpu.CompilerParams(dimension_semantics=("parallel",)),
    )(ds, s, acat)


def _step_rows(ref, j, nc):
    return ref[pl.ds(j, nc, stride=SSM_CHUNK), :]


def _fold_lanes(z, widths):
    for w in widths:
        z = z + pltpu.roll(z, w, 1)
    return z


def _ssm_forward(proj, mats):
    dblk, wend, wout, acat = mats
    t = proj.shape[0]
    nc = t // SSM_CHUNK
    L = SSM_CHUNK
    lanes = pl.BlockSpec((t, GB_LANES), lambda b: (0, b))
    state = pl.BlockSpec((nc, STATE_W), lambda b: (0, b))

    def body_end(u_ref, w_ref, e_ref):
        x = jnp.concatenate([_step_rows(u_ref, j, nc).astype(MXU_DTYPE) for j in range(L)], axis=1)
        e_ref[...] = jnp.dot(x, w_ref[...], preferred_element_type=F32)

    e = pl.pallas_call(
        body_end, name="ssm_chunk_end", grid=(N_GB,),
        in_specs=[lanes, pl.BlockSpec((None, L * GB_LANES, STATE_W), lambda b: (b, 0, 0))],
        out_specs=state, out_shape=jax.ShapeDtypeStruct((nc, N_GB * STATE_W), F32),
        compiler_params=pltpu.CompilerParams(dimension_semantics=("parallel",), vmem_limit_bytes=VMEM_CAP),
    )(proj, wend)
    s = _chunk_scan(e, acat)

    def body_out(u_ref, d_ref, s_ref, w_ref, y_ref):
        xs = [_step_rows(u_ref, j, nc).astype(MXU_DTYPE) for j in range(L)]
        sb = s_ref[...].astype(MXU_DTYPE)
        for tt in range(L):
            xcat = jnp.concatenate(xs[:tt + 1], axis=1)
            taps = jnp.concatenate([d_ref[tt - j] for j in range(tt + 1)], axis=0).astype(MXU_DTYPE)
            y = (jnp.dot(xcat, taps, preferred_element_type=F32)
                 + jnp.dot(sb, w_ref[:, tt * GB_LANES:(tt + 1) * GB_LANES], preferred_element_type=F32))
            y_ref[pl.ds(tt, nc, stride=L), :] = y

    y = pl.pallas_call(
        body_out, name="ssm_chunk_out", grid=(N_GB,),
        in_specs=[lanes, pl.BlockSpec((None, L, GB_LANES, GB_LANES), lambda b: (b, 0, 0, 0)), state,
                  pl.BlockSpec((None, STATE_W, L * GB_LANES), lambda b: (b, 0, 0))],
        out_specs=lanes, out_shape=jax.ShapeDtypeStruct((t, D_SSM), F32),
        compiler_params=pltpu.CompilerParams(dimension_semantics=("parallel",), vmem_limit_bytes=VMEM_CAP),
    )(proj, dblk, s, wout)
    return y, s


def _ssm_backward(dy, proj, s, mats):
    dblk, wend, wout, acat = mats
    t = proj.shape[0]
    nc = t // SSM_CHUNK
    L = SSM_CHUNK
    lanes = pl.BlockSpec((t, GB_LANES), lambda b: (0, b))
    state = pl.BlockSpec((nc, STATE_W), lambda b: (0, b))
    taps_spec = pl.BlockSpec((None, L, GB_LANES, GB_LANES), lambda b: (b, 0, 0, 0))

    def body_state(dy_ref, w_ref, ds_ref):
        dyc = jnp.concatenate([_step_rows(dy_ref, tt, nc).astype(MXU_DTYPE) for tt in range(L)], axis=1)
        ds_ref[...] = _dot_nt(dyc, w_ref[...])

    ds = pl.pallas_call(
        body_state, name="ssm_bwd_state", grid=(N_GB,),
        in_specs=[lanes, pl.BlockSpec((None, STATE_W, L * GB_LANES), lambda b: (b, 0, 0))],
        out_specs=state, out_shape=jax.ShapeDtypeStruct((nc, N_GB * STATE_W), F32),
        compiler_params=pltpu.CompilerParams(dimension_semantics=("parallel",), vmem_limit_bytes=VMEM_CAP),
    )(dy, wout)
    ge, dacat = _chunk_scan_bwd(ds, s, acat)

    def body_in(u_ref, dy_ref, d_ref, ge_ref, w_ref, du_ref, dd_ref, gd_ref):
        gd_ref[...] = jnp.sum(u_ref[...] * dy_ref[...], axis=0, keepdims=True)
        xs = [_step_rows(u_ref, j, nc).astype(MXU_DTYPE) for j in range(L)]
        dys = [_step_rows(dy_ref, tt, nc).astype(MXU_DTYPE) for tt in range(L)]
        ge = ge_ref[...].astype(MXU_DTYPE)
        for i in range(L):
            dyc = jnp.concatenate(dys[i:], axis=1)
            taps = jnp.concatenate([d_ref[tt - i] for tt in range(i, L)], axis=1).astype(MXU_DTYPE)
            du_ref[pl.ds(i, nc, stride=L), :] = (
                _dot_nt(dyc, taps) + _dot_nt(ge, w_ref[i * GB_LANES:(i + 1) * GB_LANES, :]))
        for j in range(L):
            m = _dot_tn(xs[j], jnp.concatenate(dys[j:], axis=1))
            for tau in range(L - j):
                part = m[:, tau * GB_LANES:(tau + 1) * GB_LANES]
                if j == 0:
                    dd_ref[tau] = part
                else:
                    dd_ref[tau] += part
        same = _lane_group((GB_LANES, GB_LANES), 0, 4) == _lane_group((GB_LANES, GB_LANES), 1, 4)
        for tau in range(L):
            dd_ref[tau] = _fold_lanes(jnp.where(same, dd_ref[tau], 0.0), (64, 32, 16))

    du, ddblk, gd = pl.pallas_call(
        body_in, name="ssm_bwd_in", grid=(N_GB,),
        in_specs=[lanes, lanes, taps_spec, state,
                  pl.BlockSpec((None, L * GB_LANES, STATE_W), lambda b: (b, 0, 0))],
        out_specs=[lanes, taps_spec, pl.BlockSpec((1, GB_LANES), lambda b: (0, b))],
        out_shape=[jax.ShapeDtypeStruct((t, D_SSM), F32), jax.ShapeDtypeStruct(dblk.shape, F32),
                   jax.ShapeDtypeStruct((1, D_SSM), F32)],
        compiler_params=pltpu.CompilerParams(dimension_semantics=("parallel",), vmem_limit_bytes=VMEM_CAP),
    )(proj, dy, dblk, ge, wend)

    def body_w(u_ref, dy_ref, ge_ref, s_ref, dwe_ref, dwo_ref):
        ge = ge_ref[...].astype(MXU_DTYPE)
        sb = s_ref[...].astype(MXU_DTYPE)
        keep = _lane_group((GB_LANES, STATE_W), 0, 4) == _lane_group((GB_LANES, STATE_W), 1, 6)
        low = lax.broadcasted_iota(jnp.int32, (GB_LANES, 2 * SSM_STATE), 1) < SSM_STATE

        def fold_state(z):
            z = z[:, :GB_STATE // 2] + z[:, GB_STATE // 2:]
            z = z[:, :GB_STATE // 4] + z[:, GB_STATE // 4:]
            return _fold_lanes(z, (SSM_STATE,))

        for j in range(L):
            z = jnp.where(keep, _dot_tn(_step_rows(u_ref, j, nc).astype(MXU_DTYPE), ge), 0.0)
            dwe_ref[j * GB_LANES:(j + 1) * GB_LANES, :] = jnp.where(
                low, fold_state(z[:, :GB_STATE]), fold_state(z[:, GB_STATE:]))
        own = _lane_group((STATE_W, GB_LANES), 0, 6) == _lane_group((STATE_W, GB_LANES), 1, 4)
        chunk = lax.broadcasted_iota(jnp.int32, (STATE_W, GB_LANES), 1) >> 4
        for half in range(L // GROUP_BLOCK):
            acc = jnp.zeros((STATE_W, GB_LANES), F32)
            for k in range(GROUP_BLOCK):
                tt = half * GROUP_BLOCK + k
                z = jnp.where(own, _dot_tn(sb, _step_rows(dy_ref, tt, nc).astype(MXU_DTYPE)), 0.0)
                acc = acc + jnp.where(chunk == k, _fold_lanes(z, (64, 32, 16)), 0.0)
            dwo_ref[:, half * GB_LANES:(half + 1) * GB_LANES] = acc

    dwend, dwout = pl.pallas_call(
        body_w, name="ssm_bwd_w", grid=(N_GB,),
        in_specs=[lanes, lanes, state, state],
        out_specs=[pl.BlockSpec((None, L * GB_LANES, 2 * SSM_STATE), lambda b: (b, 0, 0)),
                   pl.BlockSpec((None, STATE_W, L * SSM_GROUP), lambda b: (b, 0, 0))],
        out_shape=[jax.ShapeDtypeStruct((N_GB, L * GB_LANES, 2 * SSM_STATE), F32),
                   jax.ShapeDtypeStruct((N_GB, STATE_W, L * SSM_GROUP), F32)],
        compiler_params=pltpu.CompilerParams(dimension_semantics=("parallel",), vmem_limit_bytes=VMEM_CAP),
    )(proj, dy, ge, s)
    return du, gd, (ddblk[:, :, :, :SSM_GROUP], dwend, dwout, dacat)


def _t5_bucket(dist):
    max_exact = N_BUCKETS // 2
    is_small = dist < max_exact
    d = jnp.maximum(dist, 1).astype(F32)
    large = max_exact + (jnp.log(d / max_exact) / math.log(MAX_DISTANCE / max_exact)
                         * (N_BUCKETS - max_exact)).astype(jnp.int32)
    large = jnp.minimum(large, N_BUCKETS - 1)
    return jnp.where(is_small, dist, large)


def _band_bias(rel_bias_table):
    i = jnp.arange(BLOCK)[:, None]
    j = jnp.arange(BLOCK)[None, :]
    bucket = _t5_bucket(jnp.where(j > i, BLOCK + i - j, i - j))
    onehot = (bucket[:, :, None] == jnp.arange(N_BUCKETS)[None, None, :]).astype(F32)
    return jnp.einsum("qsb,bh->hqs", onehot, rel_bias_table, precision=lax.Precision.HIGHEST)


assert WINDOW == BLOCK
KV_PAIR = 2
HEADS_PER_STEP = KV_PAIR * Q_PER_KV
Q_LANES = HEADS_PER_STEP * HEAD_DIM
SLAB = 2 * HEAD_DIM
Q_COL0 = OFF_Q * COL // Q_LANES
K_COL0 = OFF_K * COL // SLAB
V_COL0 = OFF_V * COL // SLAB
ZA_COL0 = OFF_ZA * COL // Q_LANES


def _attn_specs():
    q_spec = pl.BlockSpec((BLOCK, Q_LANES), lambda m, n: (n, Q_COL0 + m))
    k_prev = pl.BlockSpec((BLOCK, SLAB), lambda m, n: (jnp.maximum(n - 1, 0), K_COL0 + m))
    k_cur = pl.BlockSpec((BLOCK, SLAB), lambda m, n: (n, K_COL0 + m))
    v_prev = pl.BlockSpec((BLOCK, SLAB), lambda m, n: (jnp.maximum(n - 1, 0), V_COL0 + m))
    v_cur = pl.BlockSpec((BLOCK, SLAB), lambda m, n: (n, V_COL0 + m))
    bias_spec = pl.BlockSpec((HEADS_PER_STEP, BLOCK, BLOCK), lambda m, n: (m, 0, 0))
    sink_spec = pl.BlockSpec(memory_space=pltpu.SMEM)
    wide = pl.BlockSpec((BLOCK, Q_LANES), lambda m, n: (n, m))
    gate = pl.BlockSpec((BLOCK, Q_LANES), lambda m, n: (n, ZA_COL0 + m))
    pair = pl.BlockSpec((BLOCK, SLAB), lambda m, n: (n, m))
    return [sink_spec, q_spec, k_prev, k_cur, v_prev, v_cur, bias_spec, gate], wide, pair


def _low_lanes(shape):
    return lax.broadcasted_iota(jnp.int32, shape, 1) < HEAD_DIM


def _pair_halves(ref):
    kb = ref[...]
    sw = pltpu.roll(kb, HEAD_DIM, 1)
    lo = _low_lanes(kb.shape)
    zero = jnp.zeros_like(kb)
    first = (jnp.where(lo, kb, zero).astype(MXU_DTYPE), jnp.where(lo, zero, sw).astype(MXU_DTYPE))
    second = (jnp.where(lo, sw, zero).astype(MXU_DTYPE), jnp.where(lo, zero, kb).astype(MXU_DTYPE))
    return first, second


def _fold_pair(acc):
    f = [x + pltpu.roll(x, HEAD_DIM, 1) for x in acc]
    return jnp.where(_low_lanes(f[0].shape), f[0], f[1])


def _from_prev(n):
    row = lax.broadcasted_iota(jnp.int32, (BLOCK, BLOCK), 0)
    col = lax.broadcasted_iota(jnp.int32, (BLOCK, BLOCK), 1)
    prev = col > row
    return prev, jnp.where(jnp.logical_and(n == 0, prev), NEG_INF, 0.0)


def _softmax_sink(s, sink):
    m = jnp.maximum(jnp.max(s, axis=1, keepdims=True), sink)
    e = jnp.exp(s - m)
    es = jnp.exp(sink - m)
    inv = 1.0 / (jnp.sum(e, axis=1, keepdims=True) + es)
    return e * inv, es * inv


def _stack_pair(prev_halves, own_halves, a):
    return jnp.concatenate([prev_halves[a][0], prev_halves[a][1], own_halves[a][0], own_halves[a][1]], axis=0)


def _split_heads(x4, prev):
    return [jnp.where(prev, x4[:, e * BLOCK:(e + 1) * BLOCK], x4[:, (2 + e) * BLOCK:(3 + e) * BLOCK]) for e in range(2)]


def _spread_heads(x, prev):
    return jnp.concatenate([jnp.where(prev, x[0], 0.0), jnp.where(prev, x[1], 0.0),
                            jnp.where(prev, 0.0, x[0]), jnp.where(prev, 0.0, x[1])], axis=1)


def _attn_forward(proj, bias, sinks):
    t = proj.shape[0]
    in_specs, wide, _ = _attn_specs()

    def body(sink_ref, q_ref, kp_ref, kc_ref, vp_ref, vc_ref, bias_ref, z_ref, o_ref, h_ref):
        m, n = pl.program_id(0), pl.program_id(1)
        kp, kc, vp, vc = (_pair_halves(r) for r in (kp_ref, kc_ref, vp_ref, vc_ref))
        keys = [_stack_pair(kp, kc, a) for a in range(KV_PAIR)]
        vals = [_stack_pair(vp, vc, a) for a in range(KV_PAIR)]
        prev, edge = _from_prev(n)
        prev2 = jnp.concatenate([prev, prev], axis=0)
        for a in range(KV_PAIR):
            slabs = (2 * a, 2 * a + 1)
            q = jnp.concatenate([q_ref[:, s * SLAB:(s + 1) * SLAB] for s in slabs], axis=0).astype(MXU_DTYPE)
            logits = _split_heads(_dot_nt(q, keys[a]), prev2)
            probs = []
            for e in range(2):
                rows = [_softmax_sink(logits[e][r * BLOCK:(r + 1) * BLOCK] * ATTN_SCALE + bias_ref[2 * s + e] + edge,
                                      sink_ref[m * HEADS_PER_STEP + 2 * s + e])[0] for r, s in enumerate(slabs)]
                probs.append(jnp.concatenate(rows, axis=0))
            out = _dot(_spread_heads(probs, prev2), vals[a])
            for r, s in enumerate(slabs):
                cols = slice(s * SLAB, (s + 1) * SLAB)
                o_ref[:, cols] = out[r * BLOCK:(r + 1) * BLOCK]
                h_ref[:, cols] = (out[r * BLOCK:(r + 1) * BLOCK] * _silu(z_ref[:, cols])).astype(h_ref.dtype)

    return pl.pallas_call(
        body, name="attn_fwd", grid=(N_KV_HEADS // KV_PAIR, t // BLOCK),
        in_specs=in_specs, out_specs=[wide, wide],
        out_shape=[jax.ShapeDtypeStruct((t, D_ATTN), F32), jax.ShapeDtypeStruct((t, D_ATTN), MXU_DTYPE)],
        compiler_params=pltpu.CompilerParams(dimension_semantics=("parallel", "arbitrary")),
    )(sinks, proj, proj, proj, proj, proj, bias, proj)


def _attn_backward(proj, bias, sinks, attn, dh):
    t = proj.shape[0]
    in_specs, wide, pair = _attn_specs()
    bias_spec = in_specs[-2]
    sink_out = pl.BlockSpec((HEADS_PER_STEP, 8, 128), lambda m, n: (m, 0, 0))

    def body(sink_ref, q_ref, kp_ref, kc_ref, vp_ref, vc_ref, bias_ref, z_ref, o_ref, dh_ref,
             dq_ref, dz_ref, dka_ref, dkb_ref, dva_ref, dvb_ref, dbias_ref, dsink_ref):
        m, n = pl.program_id(0), pl.program_id(1)

        @pl.when(n == 0)
        def _():
            dbias_ref[...] = jnp.zeros_like(dbias_ref)
            dsink_ref[...] = jnp.zeros_like(dsink_ref)

        kp, kc, vp, vc = (_pair_halves(r) for r in (kp_ref, kc_ref, vp_ref, vc_ref))
        keys = [_stack_pair(kp, kc, a) for a in range(KV_PAIR)]
        vals = [_stack_pair(vp, vc, a) for a in range(KV_PAIR)]
        prev, edge = _from_prev(n)
        lo = _low_lanes((BLOCK, SLAB))
        prev2 = jnp.concatenate([prev, prev], axis=0)
        dk = [[None] * KV_PAIR for _ in range(2)]
        dv = [[None] * KV_PAIR for _ in range(2)]
        for a in range(KV_PAIR):
            slabs = (2 * a, 2 * a + 1)
            q = jnp.concatenate([q_ref[:, s * SLAB:(s + 1) * SLAB] for s in slabs], axis=0).astype(MXU_DTYPE)
            gated = []
            for s in slabs:
                cols = slice(s * SLAB, (s + 1) * SLAB)
                dh_s, z_s = dh_ref[:, cols], z_ref[:, cols]
                dz_ref[:, cols] = (dh_s * o_ref[:, cols] * _silu_grad(z_s)).astype(dz_ref.dtype)
                gated.append(dh_s * _silu(z_s))
            do = jnp.concatenate(gated, axis=0).astype(MXU_DTYPE)
            logits = _split_heads(_dot_nt(q, keys[a]), prev2)
            dprobs = _split_heads(_dot_nt(do, vals[a]), prev2)
            probs, dlogits = [], []
            for e in range(2):
                p_rows, ds_rows = [], []
                for r, s in enumerate(slabs):
                    h = 2 * s + e
                    rows = slice(r * BLOCK, (r + 1) * BLOCK)
                    p, ps = _softmax_sink(logits[e][rows] * ATTN_SCALE + bias_ref[h] + edge,
                                          sink_ref[m * HEADS_PER_STEP + h])
                    delta = jnp.sum(p * dprobs[e][rows], axis=1, keepdims=True)
                    ds = p * (dprobs[e][rows] - delta)
                    dbias_ref[h] += ds
                    dsink_ref[h] += jnp.broadcast_to(jnp.sum(-ps * delta, axis=0, keepdims=True), (8, 128))
                    p_rows.append(p)
                    ds_rows.append(ds)
                probs.append(jnp.concatenate(p_rows, axis=0))
                dlogits.append(jnp.concatenate(ds_rows, axis=0))
            ds4 = _spread_heads(dlogits, prev2).astype(MXU_DTYPE)
            p4 = _spread_heads(probs, prev2).astype(MXU_DTYPE)
            dq = _dot(ds4, keys[a]) * ATTN_SCALE
            for r, s in enumerate(slabs):
                dq_ref[:, s * SLAB:(s + 1) * SLAB] = dq[r * BLOCK:(r + 1) * BLOCK].astype(dq_ref.dtype)
            rk = _dot_tn(ds4, q)
            rv = _dot_tn(p4, do)
            for which in range(2):
                top = 2 * which * BLOCK
                dk[which][a] = jnp.where(lo, rk[top:top + BLOCK], rk[top + BLOCK:top + 2 * BLOCK])
                dv[which][a] = jnp.where(lo, rv[top:top + BLOCK], rv[top + BLOCK:top + 2 * BLOCK])
        dkb_ref[...] = _fold_pair(dk[0]) * ATTN_SCALE
        dka_ref[...] = _fold_pair(dk[1]) * ATTN_SCALE
        dvb_ref[...] = _fold_pair(dv[0])
        dva_ref[...] = _fold_pair(dv[1])

    kv_shape = jax.ShapeDtypeStruct((t, D_KV), F32)
    return pl.pallas_call(
        body, name="attn_bwd", grid=(N_KV_HEADS // KV_PAIR, t // BLOCK),
        in_specs=in_specs + [wide, wide],
        out_specs=[wide, wide, pair, pair, pair, pair, bias_spec, sink_out],
        out_shape=[jax.ShapeDtypeStruct((t, D_ATTN), MXU_DTYPE), jax.ShapeDtypeStruct((t, D_ATTN), MXU_DTYPE),
                   kv_shape, kv_shape, kv_shape, kv_shape,
                   jax.ShapeDtypeStruct(bias.shape, F32), jax.ShapeDtypeStruct((N_Q_HEADS, 8, 128), F32)],
        compiler_params=pltpu.CompilerParams(dimension_semantics=("parallel", "arbitrary")),
    )(sinks, proj, proj, proj, proj, proj, bias, proj, attn, dh)


def _shift_blocks(cur, prev):
    return cur + jnp.concatenate([prev[BLOCK:], jnp.zeros_like(prev[:BLOCK])], axis=0)


def _mesh_pos():
    return lax.axis_index("x"), lax.axis_index("y"), lax.axis_index("c")


def _all_gather(x, *, name):
    def body(x_ref, out_ref, send_sems, recv_sems, local_sem):
        x, y, c = _mesh_pos()
        me, sibling = (x, y, c), (x, y, 1 - c)
        chips = [(1 - x, y), (x, 1 - y), (1 - x, 1 - y)]

        def slot(px, py, pc):
            return out_ref.at[4 * px + 2 * py + pc]

        def copy(k, block, to, src=None):
            return pltpu.make_async_remote_copy(
                src_ref=slot(*block) if src is None else src, dst_ref=slot(*block),
                send_sem=send_sems.at[k], recv_sem=recv_sems.at[k],
                device_id=to, device_id_type=pl.DeviceIdType.MESH)

        mine = pltpu.make_async_copy(x_ref, slot(*me), local_sem)
        mine.start()
        first = [copy(0, me, sibling, src=x_ref)]
        first += [copy(1 + j, me, (*chip, c), src=x_ref) for j, chip in enumerate(chips)]
        for cp in first:
            cp.start()
        passed = [copy(4 + j, (*chip, c), sibling) for j, chip in enumerate(chips)]
        for j, chip in enumerate(chips):
            copy(1 + j, (*chip, c), me).wait_recv()
            passed[j].start()
        copy(0, sibling, me).wait_recv()
        for j, chip in enumerate(chips):
            copy(4 + j, (*chip, 1 - c), me).wait_recv()
        for cp in first + passed:
            cp.wait_send()
        mine.wait()

    return pl.pallas_call(
        body, name=name,
        in_specs=[pl.BlockSpec(memory_space=pl.ANY)],
        out_specs=pl.BlockSpec(memory_space=pl.ANY),
        out_shape=jax.ShapeDtypeStruct((N_DEV,) + x.shape, x.dtype),
        scratch_shapes=[pltpu.SemaphoreType.DMA((7,)), pltpu.SemaphoreType.DMA((7,)), pltpu.SemaphoreType.DMA],
    )(x)


_HBM = pl.BlockSpec(memory_space=pltpu.HBM)
_SEM = pl.BlockSpec(memory_space=pltpu.SEMAPHORE)
_DATAFLOW = pltpu.SideEffectType.DATAFLOW_SIDE_EFFECTING


def _peers():
    x, y, c = _mesh_pos()
    others = []
    for k in range(1, N_DEV):
        px, py, pc = x ^ (k >> 2), y ^ ((k >> 1) & 1), c ^ (k & 1)
        others.append(((px, py, pc), 4 * px + 2 * py + pc))
    return 4 * x + 2 * y + c, others


def _split_start(bufs, plan, n_copies, *, name):
    nb = len(bufs)

    def body(*refs):
        send_sems, recv_sems, token = refs[nb], refs[nb + 1], refs[-1]
        for k, (src, dst, pos, _) in enumerate(plan(*refs[:nb])):
            pltpu.make_async_remote_copy(src_ref=src, dst_ref=dst, send_sem=send_sems.at[k], recv_sem=recv_sems.at[k],
                                         device_id=pos, device_id_type=pl.DeviceIdType.MESH).start()
        token[...] = jnp.zeros_like(token)

    return pl.pallas_call(
        body, name=name,
        out_shape=(pltpu.SemaphoreType.DMA((n_copies,)), pltpu.SemaphoreType.DMA((n_copies,)),
                   *[pltpu.HBM(b.shape, b.dtype) for b in bufs], jax.ShapeDtypeStruct((8, 128), F32)),
        in_specs=(_HBM,) * nb, out_specs=(_SEM, _SEM) + (_HBM,) * nb + (pl.BlockSpec(memory_space=pltpu.VMEM),),
        input_output_aliases={i: 2 + i for i in range(nb)},
        compiler_params=pltpu.CompilerParams(has_side_effects=_DATAFLOW),
    )(*[pltpu.with_memory_space_constraint(b, pltpu.HBM) for b in bufs])


def _split_wait(started, plan, after, *, name):
    send_sems, recv_sems, *thru = started[:-1]
    nb = len(thru)

    def body(*refs):
        send_sems, recv_sems = refs[nb], refs[nb + 1]
        for k, (src, _, pos, arrive) in enumerate(plan(*refs[:nb])):
            copy = pltpu.make_async_remote_copy(
                src_ref=src, dst_ref=arrive, send_sem=send_sems.at[k], recv_sem=recv_sems.at[k],
                device_id=pos, device_id_type=pl.DeviceIdType.MESH)
            copy.wait_send()
            copy.wait_recv()

    return pl.pallas_call(
        body, name=name,
        out_shape=tuple(pltpu.HBM(b.shape, b.dtype) for b in thru),
        in_specs=(_HBM,) * nb + (_SEM, _SEM, pl.BlockSpec(memory_space=pl.ANY)), out_specs=(_HBM,) * nb,
        input_output_aliases={i: i for i in range(nb)},
        compiler_params=pltpu.CompilerParams(has_side_effects=_DATAFLOW),
    )(*thru, send_sems, recv_sems, after)


def _plan_scatter(x_ref, land_ref):
    me, others = _peers()
    return [(x_ref.at[idx], land_ref.at[me], pos, land_ref.at[idx]) for pos, idx in others]


def _plan_gather(x_ref, land_ref):
    me, others = _peers()
    return [(x_ref, land_ref.at[me], pos, land_ref.at[idx]) for pos, idx in others]


def _near_and_far():
    x, y, c = _mesh_pos()
    chips = [(1 - x, y), (x, 1 - y), (1 - x, 1 - y)]
    near = [(x, y, 1 - c)] + [(px, py, c) for px, py in chips]
    relay = [(4 * px + 2 * py + c, 4 * px + 2 * py + 1 - c) for px, py in chips]
    return 4 * x + 2 * y + c, near, (x, y, 1 - c), relay


def _plan_gather_near(x_ref, land_ref):
    me, near, _, _ = _near_and_far()
    return [(x_ref, land_ref.at[me], pos, land_ref.at[4 * pos[0] + 2 * pos[1] + pos[2]]) for pos in near]


def _plan_gather_relay(land_ref):
    _, _, sibling, relay = _near_and_far()
    return [(land_ref.at[mine], land_ref.at[mine], sibling, land_ref.at[theirs]) for mine, theirs in relay]


def _landing_zone(own):
    me, _ = _peers()
    return lax.dynamic_update_index_in_dim(lax.empty((N_DEV,) + own.shape, own.dtype), own, me, 0)


def _scatter_start(x, *, name):
    if x.ndim == 2:
        return _split_start((x, _landing_zone(x)), _plan_gather, N_DEV - 1, name=name)
    me, _ = _peers()
    own = lax.dynamic_index_in_dim(x, me, 0, keepdims=False)
    return _split_start((x, _landing_zone(own)), _plan_scatter, N_DEV - 1, name=name)


def _scatter_wait(started, after, *, name):
    plan = _plan_gather if started[2].ndim == 2 else _plan_scatter
    return _split_wait(started, plan, after, name=name)[1]


def _adamw_math(w, g, m, v):
    m = ADAM_B1 * m + (1.0 - ADAM_B1) * g
    v = ADAM_B2 * v + (1.0 - ADAM_B2) * (g * g)
    m_hat = m / (1.0 - ADAM_B1 ** ADAM_STEP)
    v_hat = v / (1.0 - ADAM_B2 ** ADAM_STEP)
    delta = -ADAM_LR * (m_hat / (jnp.sqrt(v_hat) + ADAM_EPS) + ADAM_WD * w)
    return delta, m, v


def _adamw_reduce(parts, w, m, v, *, name, tr):
    r, c = w.shape
    tr = min(tr, r)
    spec = pl.BlockSpec((tr, c), lambda i: (i, 0))

    def body(p_ref, w_ref, m_ref, v_ref, g_ref, d_ref, nm_ref, nv_ref):
        g = p_ref[0].astype(F32)
        for s in range(1, N_DEV):
            g = g + p_ref[s].astype(F32)
        delta, nm, nv = _adamw_math(w_ref[...], g, m_ref[...], v_ref[...])
        g_ref[...] = g
        d_ref[...] = delta
        nm_ref[...] = nm
        nv_ref[...] = nv

    return pl.pallas_call(
        body, name=name, grid=(r // tr,),
        in_specs=[pl.BlockSpec((N_DEV, tr, c), lambda i: (0, i, 0)), spec, spec, spec],
        out_specs=[spec] * 4,
        out_shape=[jax.ShapeDtypeStruct((r, c), F32)] * 4,
        compiler_params=pltpu.CompilerParams(
            dimension_semantics=("parallel",),
            vmem_limit_bytes=_vmem_limit(2 * 15 * _nbytes((tr, c), F32))),
    )(parts, w, m, v)


def _sum_parts(parts, *, name):
    def body(p_ref, o_ref):
        g = p_ref[0]
        for s in range(1, N_DEV):
            g = g + p_ref[s]
        o_ref[...] = g

    return pl.pallas_call(
        body, name=name, out_shape=jax.ShapeDtypeStruct(parts.shape[1:], F32),
        compiler_params=pltpu.CompilerParams(vmem_limit_bytes=_vmem_limit(_nbytes(parts.shape, F32))),
    )(parts)


def _adamw_native(w, g, m, v, *, name):
    def body(w_ref, g_ref, m_ref, v_ref, d_ref, nm_ref, nv_ref):
        d_ref[...], nm_ref[...], nv_ref[...] = _adamw_math(w_ref[...], g_ref[...], m_ref[...], v_ref[...])

    return pl.pallas_call(body, name=name, out_shape=[jax.ShapeDtypeStruct(w.shape, F32)] * 3)(w, g, m, v)


SMALL = ["ssm_lambda_re", "ssm_lambda_im", "ssm_b_re", "ssm_b_im", "ssm_c_re", "ssm_c_im", "ssm_d",
         "ssm_log_step", "attn_sinks", "rel_bias_table", "ln_gain", "ln_bias"]


def _pack(arrs):
    flat = jnp.concatenate([a.reshape(-1) for a in arrs])
    pad = (-flat.shape[0]) % 1024
    return jnp.pad(flat, (0, pad)).reshape(-1, 128)


def _unpack(packed, like):
    flat = packed.reshape(-1)
    out, pos = [], 0
    for a in like:
        out.append(flat[pos:pos + a.size].reshape(a.shape))
        pos += a.size
    return out


def kernel(x, w_in, ssm_lambda_re, ssm_lambda_im, ssm_b_re, ssm_b_im, ssm_c_re, ssm_c_im, ssm_d, ssm_log_step, w_glu, attn_sinks, rel_bias_table, w_branch_ssm, w_branch_attn, w_out, ln_gain, ln_bias, loss_target, m_w_in, m_ssm_lambda_re, m_ssm_lambda_im, m_ssm_b_re, m_ssm_b_im, m_ssm_c_re, m_ssm_c_im, m_ssm_d, m_ssm_log_step, m_w_glu, m_attn_sinks, m_rel_bias_table, m_w_branch_ssm, m_w_branch_attn, m_w_out, m_ln_gain, m_ln_bias, v_w_in, v_ssm_lambda_re, v_ssm_lambda_im, v_ssm_b_re, v_ssm_b_im, v_ssm_c_re, v_ssm_c_im, v_ssm_d, v_ssm_log_step, v_w_glu, v_attn_sinks, v_rel_bias_table, v_w_branch_ssm, v_w_branch_attn, v_w_out, v_ln_gain, v_ln_bias):
    t = x.shape[1]
    xs = x[0]
    target = loss_target[0]
    col_in = w_in.shape[2]
    col_br = w_glu.shape[2]
    row_out = w_out.shape[1]

    w_in_mx = w_in[0].astype(MXU_DTYPE)
    near = _split_start((w_in_mx, _landing_zone(w_in_mx)), _plan_gather_near, 4, name="gather_w_in_near_start")
    ssm_params = (ssm_lambda_re[0], ssm_lambda_im[0], ssm_b_re[0], ssm_b_im[0], ssm_c_re[0], ssm_c_im[0],
                  ssm_log_step[0] + near[-1][0, 0])
    mats, mats_vjp = jax.vjp(_ssm_matrices, *ssm_params)
    mats_mx = _ssm_expand(*mats[:3]) + (mats[3],)
    sinks = attn_sinks[0]
    d_skip = ssm_d
    _, landed = _split_wait(near, _plan_gather_near, mats_mx[1], name="gather_w_in_near_wait")
    relay = _split_start((landed,), _plan_gather_relay, 3, name="gather_w_in_relay_start")
    bias, bias_vjp = jax.vjp(_band_bias, rel_bias_table + relay[-1][0, 0])
    x_mx = xs.astype(MXU_DTYPE)
    (g_in,) = _split_wait(relay, _plan_gather_relay, bias, name="gather_w_in_relay_wait")
    win = g_in.transpose(1, 0, 2).reshape(D_MODEL, D_IN)
    hold = g_in[0, 0, 0] * 0
    three = jnp.concatenate([w_glu[0], w_branch_ssm[0], w_branch_attn[0]], axis=0).astype(MXU_DTYPE)
    sent_three = _scatter_start(three + hold, name="gather_w_1024_start")
    sent_wout = _scatter_start(w_out[0].astype(MXU_DTYPE) + hold, name="gather_w_out_start")

    proj = _mm(x_mx, win, name="in_proj", tm=2048, tn=512, tk=2048, after=(sent_three[4], sent_wout[4]))
    y_conv, states = _ssm_forward(proj, mats_mx)

    def f_gelu(yv, u, d):
        ys = yv + d * u
        return ys, _gelu(ys)

    y_ssm, glu_in = _ew(f_gelu, [(y_conv, "row", 0), (proj, "row", OFF_U), (d_skip, "vec", 0)],
                        [(COL, F32, "row"), (COL, MXU_DTYPE, "row")], rows=t, cw=COL, ncb=4, tr=1024, name="ssm_gelu")
    g_three = _scatter_wait(sent_three, glu_in, name="gather_w_1024_wait")
    three_full = g_three.transpose(1, 0, 2).reshape(3 * D_SSM, N_DEV * col_br)
    wglu, wbs, wba = three_full[:D_SSM], three_full[D_SSM:2 * D_SSM], three_full[2 * D_SSM:]
    gate_tn = 512
    z_ssm_col = [(proj, OFF_ZS * COL // gate_tn)]

    def f_hssm(products, gate):
        (ga, gb), (z,) = products, gate
        return ga, gb, ga * _sigmoid(gb) * _silu(z)

    glu_a, glu_b, h_ssm = _mm_fused([(glu_in, wglu[:, :D_SSM], False), (glu_in, wglu[:, D_SSM:], False)], z_ssm_col,
                                    [F32, F32, MXU_DTYPE], f_hssm, name="glu_gate", tm=1024, tn=gate_tn)

    attn, h_attn = _attn_forward(proj, bias, sinks)
    gate_cols = [(proj, OFF_GS * COL // gate_tn), (proj, OFF_GA * COL // gate_tn)]

    def f_merge(products, gates):
        (ps, pa), (ls, la) = products, gates
        return ps, pa, _sigmoid(ls) * ps + _sigmoid(la) * pa

    p_ssm, p_attn, merged = _mm_fused([(h_ssm, wbs, False), (h_attn, wba, False)], gate_cols, [F32, F32, MXU_DTYPE],
                                      f_merge, name="branch_merge", tm=1024, tn=gate_tn)
    wout = _scatter_wait(sent_wout, merged, name="gather_w_out_wait").reshape(D_MODEL, D_MODEL)
    out = _mm(merged, wout, name="out_proj", tm=2048, tn=512, tk=2048)

    def f_norm(xv, ov, tg, gain, lbias):
        r = DEEPNORM_ALPHA * xv + ov
        mu = jnp.mean(r, axis=1, keepdims=True)
        cen = r - mu
        var = jnp.mean(cen * cen, axis=1, keepdims=True)
        rstd = lax.rsqrt(var + LN_EPS)
        xhat = cen * rstd
        yv = xhat * gain + lbias
        diff = yv - tg
        row_loss = 0.5 * jnp.mean(diff * diff, axis=1, keepdims=True)
        loss = jnp.broadcast_to(jnp.sum(row_loss, axis=0, keepdims=True), (1, 128))
        dy = diff * (1.0 / D_MODEL)
        dgain = jnp.sum(dy * xhat, axis=0, keepdims=True)
        dbias = jnp.sum(dy, axis=0, keepdims=True)
        dxh = dy * gain
        dr = rstd * (dxh - jnp.mean(dxh, axis=1, keepdims=True) - xhat * jnp.mean(dxh * xhat, axis=1, keepdims=True))
        return dr, loss, dgain, dbias

    dr, loss_part, g_ln_gain, g_ln_bias = _ew(
        f_norm, [(xs, "row", 0), (out, "row", 0), (target, "row", 0), (ln_gain, "vec", 0), (ln_bias, "vec", 0)],
        [(D_MODEL, F32, "row"), (128, F32, "acc"), (D_MODEL, F32, "acc"), (D_MODEL, F32, "acc")],
        rows=t, cw=D_MODEL, ncb=1, tr=256, name="norm_loss")

    def scatter_cols(g, cols):
        return g.reshape(g.shape[0], N_DEV, cols).transpose(1, 0, 2)

    gw_out = _mm(merged, dr, ta=True, out_dtype=WIRE_DTYPE, name="grad_w_out", tm=2048, tn=512, tk=2048)
    sent_out = _scatter_start(gw_out.reshape(N_DEV, row_out, D_MODEL), name="scatter_g_out_start")
    def b_merge(products, tiles):
        (dm,), (ps, pa, ls, la) = products, tiles
        gs, ga = _sigmoid(ls), _sigmoid(la)
        return dm * gs, dm * ga, dm * ps * gs * (1.0 - gs), dm * pa * ga * (1.0 - ga)

    dp_ssm, dp_attn, dgl_s, dgl_a = _mm_fused(
        [(dr, wout, True)], [(p_ssm, 0), (p_attn, 0)] + gate_cols, [MXU_DTYPE] * 4, b_merge,
        name="merge_bwd", tm=512, tn=gate_tn, after=(sent_out[4],))
    gw_bs = _mm(h_ssm, dp_ssm, ta=True, out_dtype=WIRE_DTYPE, name="grad_w_branch_ssm", tm=1024, tn=512, tk=2048)
    gw_ba = _mm(h_attn, dp_attn, ta=True, out_dtype=WIRE_DTYPE, name="grad_w_branch_attn", tm=1024, tn=512, tk=2048)
    dh_attn = _mm(dp_attn, wba, tb=True, name="d_h_attn", tm=2048, tn=512, tk=2048)

    def b_hssm(products, tiles):
        (dh,), (ga, gb, z) = products, tiles
        sg = _sigmoid(gb)
        dgate = dh * _silu(z)
        return dgate * sg, dgate * ga * sg * (1.0 - sg), dh * ga * sg * _silu_grad(z)

    dglu_a, dglu_b, dz_ssm = _mm_fused([(dp_ssm, wbs, True)], [(glu_a, 0), (glu_b, 0)] + z_ssm_col, [MXU_DTYPE] * 3,
                                       b_hssm, name="glu_gate_bwd", tm=1024, tn=gate_tn)
    dglu = jnp.concatenate([dglu_a, dglu_b], axis=1)
    gw_glu = _mm(glu_in, dglu, ta=True, out_dtype=WIRE_DTYPE, name="grad_w_glu", tm=1024, tn=512, tk=2048)
    sent_three = _scatter_start(scatter_cols(jnp.concatenate([gw_glu, gw_bs, gw_ba], axis=0), col_br),
                                name="scatter_g_1024_start")
    def b_gelu(products, tiles):
        return (products[0] * _gelu_grad(tiles[0]),)

    (dy_ssm,) = _mm_fused([(dglu, wglu, True)], [(y_ssm, 0)], [F32], b_gelu, name="gelu_bwd", tm=1024, tn=gate_tn,
                          after=(sent_three[4],))
    du_ssm, g_ssm_d, dmats = _ssm_backward(dy_ssm, proj, states, mats_mx)
    du = (du_ssm + d_skip * dy_ssm).astype(MXU_DTYPE)
    g_lre, g_lim, g_bre, g_bim, g_cre, g_cim, g_lstep = mats_vjp(dmats)

    dq, dz_attn, dka, dkb, dva, dvb, dbias, dsink = _attn_backward(proj, bias, sinks, attn, dh_attn)
    dk = _shift_blocks(dka, dkb).astype(MXU_DTYPE)
    dv = _shift_blocks(dva, dvb).astype(MXU_DTYPE)
    (g_table,) = bias_vjp(dbias)
    g_sinks = dsink[:, 0, 0]

    dproj = jnp.concatenate([du, dz_ssm, dq, dk, dv, dz_attn, dgl_s, dgl_a], axis=1)
    gw_in = _mm(x_mx, dproj, ta=True, out_dtype=WIRE_DTYPE, name="grad_w_in", tm=2048, tn=512, tk=2048)
    sent_in = _scatter_start(scatter_cols(gw_in, col_in), name="scatter_g_in_start")
    grad_x = _mm(dproj, win, tb=True, add=dr, add_scale=DEEPNORM_ALPHA, name="grad_x", tm=1024, tn=512, tk=4352,
                 after=(sent_in[4],))

    parts_out = _scatter_wait(sent_out, grad_x, name="scatter_g_out_wait")
    parts_three = _scatter_wait(sent_three, parts_out, name="scatter_g_1024_wait")
    parts_in = _scatter_wait(sent_in, parts_three, name="scatter_g_in_wait")

    o_in = _adamw_reduce(parts_in, w_in[0], m_w_in[0], v_w_in[0], name="adamw_w_in", tr=128)
    three_w = jnp.concatenate([w_glu[0], w_branch_ssm[0], w_branch_attn[0]], axis=0)
    three_m = jnp.concatenate([m_w_glu[0], m_w_branch_ssm[0], m_w_branch_attn[0]], axis=0)
    three_v = jnp.concatenate([v_w_glu[0], v_w_branch_ssm[0], v_w_branch_attn[0]], axis=0)
    o_three = _adamw_reduce(parts_three, three_w, three_m, three_v, name="adamw_w_1024", tr=512)
    o_out = _adamw_reduce(parts_out, w_out[0], m_w_out[0], v_w_out[0], name="adamw_w_out", tr=128)

    small_w = [ssm_lambda_re, ssm_lambda_im, ssm_b_re, ssm_b_im, ssm_c_re, ssm_c_im, ssm_d, ssm_log_step,
               attn_sinks, rel_bias_table, ln_gain, ln_bias]
    small_m = [m_ssm_lambda_re, m_ssm_lambda_im, m_ssm_b_re, m_ssm_b_im, m_ssm_c_re, m_ssm_c_im, m_ssm_d,
               m_ssm_log_step, m_attn_sinks, m_rel_bias_table, m_ln_gain, m_ln_bias]
    small_v = [v_ssm_lambda_re, v_ssm_lambda_im, v_ssm_b_re, v_ssm_b_im, v_ssm_c_re, v_ssm_c_im, v_ssm_d,
               v_ssm_log_step, v_attn_sinks, v_rel_bias_table, v_ln_gain, v_ln_bias]
    small_g = [g_lre, g_lim, g_bre, g_bim, g_cre, g_cim, g_ssm_d, g_lstep, g_sinks, g_table, g_ln_gain, g_ln_bias]
    parts_small = _all_gather(_pack(small_g), name="gather_g_small")
    sg = _unpack(_sum_parts(parts_small, name="sum_g_small"), small_w)
    updates = [_adamw_native(w, g, m, v, name="adamw_" + n)
               for n, w, g, m, v in zip(SMALL, small_w, sg, small_m, small_v)]
    sd, sm, sv = zip(*updates)

    loss = lax.psum(loss_part[0, 0], MESH_AXES)

    def big(o, idx):
        g_in_, g_three_, g_out_ = o_in[idx], o_three[idx], o_out[idx]
        return {"w_in": g_in_[None], "w_glu": g_three_[None, :D_SSM], "w_branch_ssm": g_three_[None, D_SSM:2 * D_SSM],
                "w_branch_attn": g_three_[None, 2 * D_SSM:], "w_out": g_out_[None]}

    order = ["w_in", "ssm_lambda_re", "ssm_lambda_im", "ssm_b_re", "ssm_b_im", "ssm_c_re", "ssm_c_im", "ssm_d",
             "ssm_log_step", "w_glu", "attn_sinks", "rel_bias_table", "w_branch_ssm", "w_branch_attn", "w_out",
             "ln_gain", "ln_bias"]
    outs = [loss, grad_x[None]]
    for idx, small in enumerate([sg, sd, sm, sv]):
        table = big(None, idx)
        table.update(dict(zip(SMALL, small)))
        outs += [table[n] for n in order]
    return tuple(outs)
```

```python
import math

import jax
import jax.numpy as jnp
from jax import lax
from jax.experimental import pallas as pl
from jax.experimental.pallas import tpu as pltpu

F32 = jnp.float32
MXU_DTYPE = jnp.bfloat16
WIRE_DTYPE = jnp.bfloat16

D_MODEL = 2048
D_SSM = 1024
SSM_GROUP = 16
N_GROUPS = 64
SSM_STATE = 64
N_Q_HEADS = 16
N_KV_HEADS = 4
Q_PER_KV = 4
HEAD_DIM = 64
D_ATTN = 1024
D_KV = 256
WINDOW = 128
BLOCK = 128
N_BUCKETS = 32
MAX_DISTANCE = 128
D_IN = 8704
DEEPNORM_ALPHA = 2.0 ** 0.25
LN_EPS = 1e-5
NEG_INF = -1e30
ATTN_SCALE = HEAD_DIM ** -0.5

ADAM_LR = 0.001
ADAM_B1 = 0.9
ADAM_B2 = 0.999
ADAM_EPS = 1e-08
ADAM_WD = 0.01
ADAM_STEP = 10

N_DEV = 8
SSM_CHUNK = 16
GROUP_BLOCK = 8
N_GB = N_GROUPS // GROUP_BLOCK
GB_LANES = GROUP_BLOCK * SSM_GROUP
GB_STATE = GROUP_BLOCK * SSM_STATE
COL = 256
OFF_U, OFF_ZS, OFF_Q, OFF_K, OFF_V, OFF_ZA, OFF_GS, OFF_GA = 0, 4, 8, 12, 13, 14, 18, 26

VMEM_CAP = 56 * 1024 * 1024
MESH_AXES = ("x", "y", "c")


def _vmem_limit(block_bytes):
    return int(min(max(3 * block_bytes, 16 * 1024 * 1024), VMEM_CAP))


def _nbytes(shape, dtype):
    return math.prod(shape) * jnp.dtype(dtype).itemsize


def _tile(n, pref):
    if n <= pref:
        return n
    t = (pref // 128) * 128
    while t >= 128:
        if n % t == 0:
            return t
        t -= 128
    return n


def _mm(a, b, *, name, ta=False, tb=False, out_dtype=F32, tm=1024, tn=512, tk=512, add=None, add_scale=1.0,
        after=()):
    squeeze = a.ndim == 2
    if squeeze:
        a, b = a[None], b[None]
        if add is not None:
            add = add[None]
    nb = a.shape[0]
    m, k = (a.shape[2], a.shape[1]) if ta else (a.shape[1], a.shape[2])
    n = b.shape[1] if tb else b.shape[2]
    tm, tn, tk = _tile(m, tm), _tile(n, tn), _tile(k, tk)
    nk = k // tk
    dn = (((0 if ta else 1,), (1 if tb else 0,)), ((), ()))

    a_spec = (pl.BlockSpec((None, tk, tm), lambda g, i, j, kk: (g, kk, i)) if ta
              else pl.BlockSpec((None, tm, tk), lambda g, i, j, kk: (g, i, kk)))
    b_spec = (pl.BlockSpec((None, tn, tk), lambda g, i, j, kk: (g, j, kk)) if tb
              else pl.BlockSpec((None, tk, tn), lambda g, i, j, kk: (g, kk, j)))
    o_spec = pl.BlockSpec((None, tm, tn), lambda g, i, j, kk: (g, i, j))
    in_specs = [a_spec, b_spec]
    operands = [a, b]
    if add is not None:
        in_specs.append(o_spec)
        operands.append(add)
    for tok in after:
        in_specs.append(pl.BlockSpec(memory_space=pl.ANY))
        operands.append(tok)
    n_in = len(operands)

    def body(*refs):
        a_ref, b_ref = refs[0], refs[1]
        add_ref = refs[2] if add is not None else None
        o_ref = refs[n_in]
        acc_ref = refs[-1]
        kk = pl.program_id(3)
        part = lax.dot_general(a_ref[...].astype(MXU_DTYPE), b_ref[...].astype(MXU_DTYPE), dn,
                               preferred_element_type=F32)

        def finish(r):
            if add_ref is not None:
                r = r + add_scale * add_ref[...]
            o_ref[...] = r.astype(out_dtype)

        if nk == 1:
            finish(part)
            return

        @pl.when(kk == 0)
        def _():
            acc_ref[...] = part

        @pl.when(jnp.logical_and(kk > 0, kk < nk - 1))
        def _():
            acc_ref[...] += part

        @pl.when(kk == nk - 1)
        def _():
            finish(acc_ref[...] + part)

    blocks = (_nbytes((tm, tk), a.dtype) + _nbytes((tk, tn), b.dtype) + _nbytes((tm, tn), out_dtype)
              + (_nbytes((tm, tn), F32) if add is not None else 0))
    out = pl.pallas_call(
        body,
        name=name,
        grid=(nb, m // tm, n // tn, nk),
        in_specs=in_specs,
        out_specs=o_spec,
        out_shape=jax.ShapeDtypeStruct((nb, m, n), out_dtype),
        scratch_shapes=[pltpu.VMEM((tm, tn), F32)],
        compiler_params=pltpu.CompilerParams(
            dimension_semantics=("parallel", "parallel", "parallel", "arbitrary"),
            vmem_limit_bytes=_vmem_limit(2 * blocks + 2 * _nbytes((tm, tn), F32))),
    )(*operands)
    return out[0] if squeeze else out


def _mm_fused(pairs, extras, out_dtypes, epilogue, *, name, tm, tn, after=()):
    m = pairs[0][0].shape[0]
    n = pairs[0][1].shape[0] if pairs[0][2] else pairs[0][1].shape[1]
    tm, tn = _tile(m, tm), _tile(n, tn)
    in_specs, operands, dns = [], [], []
    for a, b, tb in pairs:
        k = a.shape[1]
        in_specs += [pl.BlockSpec((tm, k), lambda i, j: (i, 0)),
                     pl.BlockSpec((tn, k), lambda i, j: (j, 0)) if tb else pl.BlockSpec((k, tn), lambda i, j: (0, j))]
        operands += [a, b]
        dns.append((((1,), (1 if tb else 0,)), ((), ())))

    def tile_at(col):
        return pl.BlockSpec((tm, tn), lambda i, j: (i, col + j))

    in_specs += [tile_at(col) for _, col in extras]
    operands += [arr for arr, _ in extras]
    in_specs += [pl.BlockSpec(memory_space=pl.ANY)] * len(after)
    operands += list(after)
    n_pairs, n_extra, n_in = len(pairs), len(extras), len(operands)

    def body(*refs):
        products = [lax.dot_general(refs[2 * p][...].astype(MXU_DTYPE), refs[2 * p + 1][...].astype(MXU_DTYPE),
                                    dns[p], preferred_element_type=F32) for p in range(n_pairs)]
        tiles = [refs[2 * n_pairs + e][...] for e in range(n_extra)]
        for o_ref, val in zip(refs[n_in:], epilogue(products, tiles)):
            o_ref[...] = val.astype(o_ref.dtype)

    blocks = sum(_nbytes((tm, a.shape[1]), a.dtype) + _nbytes((tn, a.shape[1]), b.dtype) for a, b, _ in pairs)
    blocks += sum(_nbytes((tm, tn), arr.dtype) for arr, _ in extras) + sum(_nbytes((tm, tn), dt) for dt in out_dtypes)
    return pl.pallas_call(
        body, name=name, grid=(m // tm, n // tn), in_specs=in_specs,
        out_specs=[tile_at(0)] * len(out_dtypes),
        out_shape=[jax.ShapeDtypeStruct((m, n), dt) for dt in out_dtypes],
        compiler_params=pltpu.CompilerParams(
            dimension_semantics=("parallel", "parallel"),
            vmem_limit_bytes=_vmem_limit(2 * blocks + n_pairs * _nbytes((tm, tn), F32))),
    )(*operands)


def _ew(fn, ins, outs, *, rows, cw, ncb, tr, name):
    tr = min(tr, rows)
    n_in = len(ins)

    def row_map(off):
        return lambda j, i: (i, off + j)

    def vec_map(off):
        return lambda j, i: (0, off + j)

    in_specs = []
    for arr, kind, off in ins:
        if kind == "row":
            in_specs.append(pl.BlockSpec((tr, cw), row_map(off)))
        else:
            in_specs.append(pl.BlockSpec((1, cw), vec_map(off)))
    out_specs, out_shapes = [], []
    for bw, dt, kind in outs:
        if kind == "row":
            out_specs.append(pl.BlockSpec((tr, bw), row_map(0)))
            out_shapes.append(jax.ShapeDtypeStruct((rows, ncb * bw), dt))
        else:
            out_specs.append(pl.BlockSpec((1, bw), vec_map(0)))
            out_shapes.append(jax.ShapeDtypeStruct((1, ncb * bw), F32))

    def body(*refs):
        i = pl.program_id(1)
        vals = fn(*[r[...] for r in refs[:n_in]])
        for r, (bw, dt, kind), v in zip(refs[n_in:], outs, vals):
            if kind == "row":
                r[...] = v.astype(dt)
            else:
                @pl.when(i == 0)
                def _(r=r):
                    r[...] = jnp.zeros_like(r)

                r[...] += v

    blocks = sum(_nbytes((tr, cw), a.dtype) for a, kind, _ in ins if kind == "row")
    blocks += sum(_nbytes((tr, bw), dt) for bw, dt, kind in outs if kind == "row")
    res = pl.pallas_call(
        body,
        name=name,
        grid=(ncb, rows // tr),
        in_specs=in_specs,
        out_specs=out_specs,
        out_shape=out_shapes,
        compiler_params=pltpu.CompilerParams(
            dimension_semantics=("parallel", "arbitrary"),
            vmem_limit_bytes=_vmem_limit(4 * blocks)),
    )(*[a for a, _, _ in ins])
    return res


def _sigmoid(x):
    return 1.0 / (1.0 + jnp.exp(-x))


INV_SQRT2 = 0.7071067811865476
INV_SQRT_2PI = 0.3989422804014327


def _gelu(x):
    return 0.5 * x * (1.0 + lax.erf(x * INV_SQRT2))


def _gelu_grad(x):
    return 0.5 * (1.0 + lax.erf(x * INV_SQRT2)) + x * INV_SQRT_2PI * jnp.exp(-0.5 * x * x)


def _silu(x):
    return x * _sigmoid(x)


def _silu_grad(x):
    s = _sigmoid(x)
    return s * (1.0 + x * (1.0 - s))


@jax.custom_vjp
def _taps_product(mr, mi, bbr, bbi):
    bbr_t, bbi_t = jnp.transpose(bbr, (0, 2, 1)), jnp.transpose(bbi, (0, 2, 1))
    return jnp.sum(mr[..., None, :] * bbr_t[None, :, None] - mi[..., None, :] * bbi_t[None, :, None], axis=-1)


def _taps_product_fwd(mr, mi, bbr, bbi):
    return _taps_product(mr, mi, bbr, bbi), (mr, mi, bbr, bbi)


def _taps_product_bwd(res, g):
    mr, mi, bbr, bbi = res
    hi = lax.Precision.HIGHEST
    return (jnp.einsum("tghk,gpk->tghp", g, bbr, precision=hi), -jnp.einsum("tghk,gpk->tghp", g, bbi, precision=hi),
            jnp.einsum("tghk,tghp->gpk", g, mr, precision=hi), -jnp.einsum("tghk,tghp->gpk", g, mi, precision=hi))


_taps_product.defvjp(_taps_product_fwd, _taps_product_bwd)


def _ssm_matrices(lam_re, lam_im, b_re, b_im, c_re, c_im, log_step):
    L = SSM_CHUNK
    step = jnp.exp(log_step)[:, None]
    ea, eb = lam_re * step, lam_im * step
    mag = jnp.exp(ea)
    lbr, lbi = mag * jnp.cos(eb), mag * jnp.sin(eb)
    den = lam_re * lam_re + lam_im * lam_im
    nr, ni = lbr - 1.0, lbi
    cr = (nr * lam_re + ni * lam_im) / den
    ci = (ni * lam_re - nr * lam_im) / den
    bbr = cr[..., None] * b_re - ci[..., None] * b_im
    bbi = cr[..., None] * b_im + ci[..., None] * b_re
    taus = jnp.arange(L + 1, dtype=F32)[:, None, None]
    pmag = jnp.exp(taus * ea[None])
    pwr, pwi = pmag * jnp.cos(taus * eb[None]), pmag * jnp.sin(taus * eb[None])
    mr = c_re[None] * pwr[:L, :, None, :] - c_im[None] * pwi[:L, :, None, :]
    mi = c_re[None] * pwi[:L, :, None, :] + c_im[None] * pwr[:L, :, None, :]
    kk = _taps_product(mr, mi, bbr, bbi)
    taps = jnp.transpose(kk.reshape(L, N_GB, GROUP_BLOCK, SSM_GROUP, SSM_GROUP), (1, 0, 2, 4, 3))
    taps = taps.reshape(N_GB, L, GB_LANES, SSM_GROUP)
    rev_r, rev_i = pwr[L - 1 - jnp.arange(L)], pwi[L - 1 - jnp.arange(L)]
    wer = rev_r[..., None] * bbr[None] - rev_i[..., None] * bbi[None]
    wei = rev_r[..., None] * bbi[None] + rev_i[..., None] * bbr[None]

    def rows_in(w):
        w = jnp.transpose(w.reshape(L, N_GB, GROUP_BLOCK, SSM_STATE, SSM_GROUP), (1, 0, 2, 4, 3))
        return w.reshape(N_GB, L * GB_LANES, SSM_STATE)

    wend = jnp.concatenate([rows_in(wer), rows_in(wei)], axis=2)
    m1r = c_re[None] * pwr[1:, :, None, :] - c_im[None] * pwi[1:, :, None, :]
    m1i = c_re[None] * pwi[1:, :, None, :] + c_im[None] * pwr[1:, :, None, :]

    def rows_out(m):
        m = jnp.transpose(m.reshape(L, N_GB, GROUP_BLOCK, SSM_GROUP, SSM_STATE), (1, 2, 4, 0, 3))
        return m.reshape(N_GB, GB_STATE, L * SSM_GROUP)

    wout = jnp.concatenate([rows_out(m1r), rows_out(-m1i)], axis=1)
    acat = jnp.concatenate([pwr[L].reshape(N_GB, 1, GB_STATE), pwi[L].reshape(N_GB, 1, GB_STATE)], axis=2)
    return taps, wend, wout, acat


def _lane_group(shape, axis, shift):
    return (lax.broadcasted_iota(jnp.int32, shape, axis) >> shift) & (GROUP_BLOCK - 1)


def _ssm_expand(taps, wend, wout):
    L = SSM_CHUNK
    taps = jnp.pad(taps, ((0, 0), (0, 0), (0, 0), (0, GB_LANES - SSM_GROUP)))

    def body(t_ref, we_ref, wo_ref, d_ref, web_ref, wob_ref):
        def rc(shape):
            return lax.broadcasted_iota(jnp.int32, shape, 0), lax.broadcasted_iota(jnp.int32, shape, 1)

        r, c = rc((GB_LANES, GB_LANES))
        spread = ((r < SSM_GROUP) & (r == (c & (SSM_GROUP - 1)))).astype(MXU_DTYPE)
        same = _lane_group((GB_LANES, GB_LANES), 0, 4) == _lane_group((GB_LANES, GB_LANES), 1, 4)
        for tau in range(L):
            full = jnp.dot(t_ref[tau].astype(MXU_DTYPE), spread, preferred_element_type=F32)
            d_ref[tau] = jnp.where(same, full, 0.0).astype(d_ref.dtype)
        r, c = rc((2 * SSM_STATE, STATE_W))
        part = (r >> 6) == (c >> 9)
        spread = (part & ((r & (SSM_STATE - 1)) == (c & (SSM_STATE - 1)))).astype(MXU_DTYPE)
        keep = _lane_group((GB_LANES, STATE_W), 0, 4) == _lane_group((GB_LANES, STATE_W), 1, 6)
        for j in range(L):
            rows = slice(j * GB_LANES, (j + 1) * GB_LANES)
            full = jnp.dot(we_ref[rows, :].astype(MXU_DTYPE), spread, preferred_element_type=F32)
            web_ref[rows, :] = jnp.where(keep, full, 0.0).astype(web_ref.dtype)
        r, c = rc((GB_LANES, GB_LANES))
        own = _lane_group((STATE_W, GB_LANES), 0, 6) == _lane_group((STATE_W, GB_LANES), 1, 4)
        for tt in range(L):
            half, k = tt // GROUP_BLOCK, tt % GROUP_BLOCK
            spread = (((r >> 4) == k) & ((r & (SSM_GROUP - 1)) == (c & (SSM_GROUP - 1)))).astype(MXU_DTYPE)
            src = wo_ref[:, half * GB_LANES:(half + 1) * GB_LANES].astype(MXU_DTYPE)
            full = jnp.dot(src, spread, preferred_element_type=F32)
            wob_ref[:, tt * GB_LANES:(tt + 1) * GB_LANES] = jnp.where(own, full, 0.0).astype(wob_ref.dtype)

    def spec(shape):
        return pl.BlockSpec((None,) + shape[1:], lambda b: (b,) + (0,) * (len(shape) - 1))

    out_shapes = [(N_GB, L, GB_LANES, GB_LANES), (N_GB, L * GB_LANES, STATE_W), (N_GB, STATE_W, L * GB_LANES)]
    return tuple(pl.pallas_call(
        body, name="ssm_expand", grid=(N_GB,),
        in_specs=[spec(taps.shape), spec(wend.shape), spec(wout.shape)],
        out_specs=[spec(s) for s in out_shapes],
        out_shape=[jax.ShapeDtypeStruct(s, MXU_DTYPE) for s in out_shapes],
        compiler_params=pltpu.CompilerParams(dimension_semantics=("parallel",), vmem_limit_bytes=VMEM_CAP),
    )(taps, wend, wout))


def _dot(a, b):
    return jnp.dot(a.astype(MXU_DTYPE), b.astype(MXU_DTYPE), preferred_element_type=F32)


def _dot_nt(a, b):
    return lax.dot_general(a.astype(MXU_DTYPE), b.astype(MXU_DTYPE), (((1,), (1,)), ((), ())),
                           preferred_element_type=F32)


def _dot_tn(a, b):
    return lax.dot_general(a.astype(MXU_DTYPE), b.astype(MXU_DTYPE), (((0,), (0,)), ((), ())),
                           preferred_element_type=F32)


STATE_W = 2 * GB_STATE


def _chunk_scan(e, acat):
    nc = e.shape[0]
    spec = pl.BlockSpec((nc, STATE_W), lambda b: (0, b))
    aspec = pl.BlockSpec((None, 1, STATE_W), lambda b: (b, 0, 0))

    def body(e_ref, a_ref, s_ref):
        a_r, a_i = a_ref[:, :GB_STATE], a_ref[:, GB_STATE:]

        def step(c, carry):
            s_r, s_i = carry
            s_ref[pl.ds(c, 1), :GB_STATE] = s_r
            s_ref[pl.ds(c, 1), GB_STATE:] = s_i
            e_r = e_ref[pl.ds(c, 1), :GB_STATE]
            e_i = e_ref[pl.ds(c, 1), GB_STATE:]
            return (a_r * s_r - a_i * s_i + e_r, a_r * s_i + a_i * s_r + e_i)

        zero = jnp.zeros((1, GB_STATE), F32)
        lax.fori_loop(0, nc, step, (zero, zero))

    return pl.pallas_call(
        body, name="ssm_chunk_scan", grid=(N_GB,),
        in_specs=[spec, aspec], out_specs=spec,
        out_shape=jax.ShapeDtypeStruct(e.shape, F32),
        compiler_params=pltpu.CompilerParams(dimension_semantics=("parallel",)),
    )(e, acat)


def _chunk_scan_bwd(ds, s, acat):
    nc = ds.shape[0]
    spec = pl.BlockSpec((nc, STATE_W), lambda b: (0, b))
    aspec = pl.BlockSpec((None, 1, STATE_W), lambda b: (b, 0, 0))

    def body(ds_ref, s_ref, a_ref, ge_ref, da_ref):
        a_r, a_i = a_ref[:, :GB_STATE], a_ref[:, GB_STATE:]

        def step(t, carry):
            g_r, g_i, d_r, d_i = carry
            c = nc - 1 - t
            ge_ref[pl.ds(c, 1), :GB_STATE] = g_r
            ge_ref[pl.ds(c, 1), GB_STATE:] = g_i
            s_r = s_ref[pl.ds(c, 1), :GB_STATE]
            s_i = s_ref[pl.ds(c, 1), GB_STATE:]
            d_r = d_r + g_r * s_r + g_i * s_i
            d_i = d_i + g_i * s_r - g_r * s_i
            n_r = ds_ref[pl.ds(c, 1), :GB_STATE] + a_r * g_r + a_i * g_i
            n_i = ds_ref[pl.ds(c, 1), GB_STATE:] + a_r * g_i - a_i * g_r
            return (n_r, n_i, d_r, d_i)

        zero = jnp.zeros((1, GB_STATE), F32)
        _, _, d_r, d_i = lax.fori_loop(0, nc, step, (zero, zero, zero, zero))
        da_ref[:, :GB_STATE] = d_r
        da_ref[:, GB_STATE:] = d_i

    return pl.pallas_call(
        body, name="ssm_chunk_scan_bwd", grid=(N_GB,),
        in_specs=[spec, spec, aspec], out_specs=[spec, aspec],
        out_shape=[jax.ShapeDtypeStruct(ds.shape, F32), jax.ShapeDtypeStruct(acat.shape, F32)],
        compiler_params=pltpu.CompilerParams(dimension_semantics=("parallel",)),
    )(ds, s, acat)


def _step_rows(ref, j, nc):
    return ref[pl.ds(j, nc, stride=SSM_CHUNK), :]


def _fold_lanes(z, widths):
    for w in widths:
        z = z + pltpu.roll(z, w, 1)
    return z


def _ssm_forward(proj, mats):
    dblk, wend, wout, acat = mats
    t = proj.shape[0]
    nc = t // SSM_CHUNK
    L = SSM_CHUNK
    lanes = pl.BlockSpec((t, GB_LANES), lambda b: (0, b))
    state = pl.BlockSpec((nc, STATE_W), lambda b: (0, b))

    def body_end(u_ref, w_ref, e_ref):
        x = jnp.concatenate([_step_rows(u_ref, j, nc).astype(MXU_DTYPE) for j in range(L)], axis=1)
        e_ref[...] = jnp.dot(x, w_ref[...], preferred_element_type=F32)

    e = pl.pallas_call(
        body_end, name="ssm_chunk_end", grid=(N_GB,),
        in_specs=[lanes, pl.BlockSpec((None, L * GB_LANES, STATE_W), lambda b: (b, 0, 0))],
        out_specs=state, out_shape=jax.ShapeDtypeStruct((nc, N_GB * STATE_W), F32),
        compiler_params=pltpu.CompilerParams(dimension_semantics=("parallel",), vmem_limit_bytes=VMEM_CAP),
    )(proj, wend)
    s = _chunk_scan(e, acat)

    def body_out(u_ref, d_ref, s_ref, w_ref, y_ref):
        xs = [_step_rows(u_ref, j, nc).astype(MXU_DTYPE) for j in range(L)]
        sb = s_ref[...].astype(MXU_DTYPE)
        for tt in range(L):
            xcat = jnp.concatenate(xs[:tt + 1], axis=1)
            taps = jnp.concatenate([d_ref[tt - j] for j in range(tt + 1)], axis=0).astype(MXU_DTYPE)
            y = (jnp.dot(xcat, taps, preferred_element_type=F32)
                 + jnp.dot(sb, w_ref[:, tt * GB_LANES:(tt + 1) * GB_LANES], preferred_element_type=F32))
            y_ref[pl.ds(tt, nc, stride=L), :] = y

    y = pl.pallas_call(
        body_out, name="ssm_chunk_out", grid=(N_GB,),
        in_specs=[lanes, pl.BlockSpec((None, L, GB_LANES, GB_LANES), lambda b: (b, 0, 0, 0)), state,
                  pl.BlockSpec((None, STATE_W, L * GB_LANES), lambda b: (b, 0, 0))],
        out_specs=lanes, out_shape=jax.ShapeDtypeStruct((t, D_SSM), F32),
        compiler_params=pltpu.CompilerParams(dimension_semantics=("parallel",), vmem_limit_bytes=VMEM_CAP),
    )(proj, dblk, s, wout)
    return y, s


def _ssm_backward(dy, proj, s, mats):
    dblk, wend, wout, acat = mats
    t = proj.shape[0]
    nc = t // SSM_CHUNK
    L = SSM_CHUNK
    lanes = pl.BlockSpec((t, GB_LANES), lambda b: (0, b))
    state = pl.BlockSpec((nc, STATE_W), lambda b: (0, b))
    taps_spec = pl.BlockSpec((None, L, GB_LANES, GB_LANES), lambda b: (b, 0, 0, 0))

    def body_state(dy_ref, w_ref, ds_ref):
        dyc = jnp.concatenate([_step_rows(dy_ref, tt, nc).astype(MXU_DTYPE) for tt in range(L)], axis=1)
        ds_ref[...] = _dot_nt(dyc, w_ref[...])

    ds = pl.pallas_call(
        body_state, name="ssm_bwd_state", grid=(N_GB,),
        in_specs=[lanes, pl.BlockSpec((None, STATE_W, L * GB_LANES), lambda b: (b, 0, 0))],
        out_specs=state, out_shape=jax.ShapeDtypeStruct((nc, N_GB * STATE_W), F32),
        compiler_params=pltpu.CompilerParams(dimension_semantics=("parallel",), vmem_limit_bytes=VMEM_CAP),
    )(dy, wout)
    ge, dacat = _chunk_scan_bwd(ds, s, acat)

    def body_in(u_ref, dy_ref, d_ref, ge_ref, w_ref, du_ref, dd_ref, gd_ref):
        gd_ref[...] = jnp.sum(u_ref[...] * dy_ref[...], axis=0, keepdims=True)
        xs = [_step_rows(u_ref, j, nc).astype(MXU_DTYPE) for j in range(L)]
        dys = [_step_rows(dy_ref, tt, nc).astype(MXU_DTYPE) for tt in range(L)]
        ge = ge_ref[...].astype(MXU_DTYPE)
        for i in range(L):
            dyc = jnp.concatenate(dys[i:], axis=1)
            taps = jnp.concatenate([d_ref[tt - i] for tt in range(i, L)], axis=1).astype(MXU_DTYPE)
            du_ref[pl.ds(i, nc, stride=L), :] = (
                _dot_nt(dyc, taps) + _dot_nt(ge, w_ref[i * GB_LANES:(i + 1) * GB_LANES, :]))
        for j in range(L):
            m = _dot_tn(xs[j], jnp.concatenate(dys[j:], axis=1))
            for tau in range(L - j):
                part = m[:, tau * GB_LANES:(tau + 1) * GB_LANES]
                if j == 0:
                    dd_ref[tau] = part
                else:
                    dd_ref[tau] += part
        same = _lane_group((GB_LANES, GB_LANES), 0, 4) == _lane_group((GB_LANES, GB_LANES), 1, 4)
        for tau in range(L):
            dd_ref[tau] = _fold_lanes(jnp.where(same, dd_ref[tau], 0.0), (64, 32, 16))

    du, ddblk, gd = pl.pallas_call(
        body_in, name="ssm_bwd_in", grid=(N_GB,),
        in_specs=[lanes, lanes, taps_spec, state,
                  pl.BlockSpec((None, L * GB_LANES, STATE_W), lambda b: (b, 0, 0))],
        out_specs=[lanes, taps_spec, pl.BlockSpec((1, GB_LANES), lambda b: (0, b))],
        out_shape=[jax.ShapeDtypeStruct((t, D_SSM), F32), jax.ShapeDtypeStruct(dblk.shape, F32),
                   jax.ShapeDtypeStruct((1, D_SSM), F32)],
        compiler_params=pltpu.CompilerParams(dimension_semantics=("parallel",), vmem_limit_bytes=VMEM_CAP),
    )(proj, dy, dblk, ge, wend)

    def body_w(u_ref, dy_ref, ge_ref, s_ref, dwe_ref, dwo_ref):
        ge = ge_ref[...].astype(MXU_DTYPE)
        sb = s_ref[...].astype(MXU_DTYPE)
        keep = _lane_group((GB_LANES, STATE_W), 0, 4) == _lane_group((GB_LANES, STATE_W), 1, 6)
        low = lax.broadcasted_iota(jnp.int32, (GB_LANES, 2 * SSM_STATE), 1) < SSM_STATE

        def fold_state(z):
            z = z[:, :GB_STATE // 2] + z[:, GB_STATE // 2:]
            z = z[:, :GB_STATE // 4] + z[:, GB_STATE // 4:]
            return _fold_lanes(z, (SSM_STATE,))

        for j in range(L):
            z = jnp.where(keep, _dot_tn(_step_rows(u_ref, j, nc).astype(MXU_DTYPE), ge), 0.0)
            dwe_ref[j * GB_LANES:(j + 1) * GB_LANES, :] = jnp.where(
                low, fold_state(z[:, :GB_STATE]), fold_state(z[:, GB_STATE:]))
        own = _lane_group((STATE_W, GB_LANES), 0, 6) == _lane_group((STATE_W, GB_LANES), 1, 4)
        chunk = lax.broadcasted_iota(jnp.int32, (STATE_W, GB_LANES), 1) >> 4
        for half in range(L // GROUP_BLOCK):
            acc = jnp.zeros((STATE_W, GB_LANES), F32)
            for k in range(GROUP_BLOCK):
                tt = half * GROUP_BLOCK + k
                z = jnp.where(own, _dot_tn(sb, _step_rows(dy_ref, tt, nc).astype(MXU_DTYPE)), 0.0)
                acc = acc + jnp.where(chunk == k, _fold_lanes(z, (64, 32, 16)), 0.0)
            dwo_ref[:, half * GB_LANES:(half + 1) * GB_LANES] = acc

    dwend, dwout = pl.pallas_call(
        body_w, name="ssm_bwd_w", grid=(N_GB,),
        in_specs=[lanes, lanes, state, state],
        out_specs=[pl.BlockSpec((None, L * GB_LANES, 2 * SSM_STATE), lambda b: (b, 0, 0)),
                   pl.BlockSpec((None, STATE_W, L * SSM_GROUP), lambda b: (b, 0, 0))],
        out_shape=[jax.ShapeDtypeStruct((N_GB, L * GB_LANES, 2 * SSM_STATE), F32),
                   jax.ShapeDtypeStruct((N_GB, STATE_W, L * SSM_GROUP), F32)],
        compiler_params=pltpu.CompilerParams(dimension_semantics=("parallel",), vmem_limit_bytes=VMEM_CAP),
    )(proj, dy, ge, s)
    return du, gd, (ddblk[:, :, :, :SSM_GROUP], dwend, dwout, dacat)


def _t5_bucket(dist):
    max_exact = N_BUCKETS // 2
    is_small = dist < max_exact
    d = jnp.maximum(dist, 1).astype(F32)
    large = max_exact + (jnp.log(d / max_exact) / math.log(MAX_DISTANCE / max_exact)
                         * (N_BUCKETS - max_exact)).astype(jnp.int32)
    large = jnp.minimum(large, N_BUCKETS - 1)
    return jnp.where(is_small, dist, large)


def _band_bias(rel_bias_table):
    i = jnp.arange(BLOCK)[:, None]
    j = jnp.arange(BLOCK)[None, :]
    bucket = _t5_bucket(jnp.where(j > i, BLOCK + i - j, i - j))
    onehot = (bucket[:, :, None] == jnp.arange(N_BUCKETS)[None, None, :]).astype(F32)
    return jnp.einsum("qsb,bh->hqs", onehot, rel_bias_table, precision=lax.Precision.HIGHEST)


assert WINDOW == BLOCK
KV_PAIR = 2
HEADS_PER_STEP = KV_PAIR * Q_PER_KV
Q_LANES = HEADS_PER_STEP * HEAD_DIM
SLAB = 2 * HEAD_DIM
Q_COL0 = OFF_Q * COL // Q_LANES
K_COL0 = OFF_K * COL // SLAB
V_COL0 = OFF_V * COL // SLAB
ZA_COL0 = OFF_ZA * COL // Q_LANES


def _attn_specs():
    q_spec = pl.BlockSpec((BLOCK, Q_LANES), lambda m, n: (n, Q_COL0 + m))
    k_prev = pl.BlockSpec((BLOCK, SLAB), lambda m, n: (jnp.maximum(n - 1, 0), K_COL0 + m))
    k_cur = pl.BlockSpec((BLOCK, SLAB), lambda m, n: (n, K_COL0 + m))
    v_prev = pl.BlockSpec((BLOCK, SLAB), lambda m, n: (jnp.maximum(n - 1, 0), V_COL0 + m))
    v_cur = pl.BlockSpec((BLOCK, SLAB), lambda m, n: (n, V_COL0 + m))
    bias_spec = pl.BlockSpec((HEADS_PER_STEP, BLOCK, BLOCK), lambda m, n: (m, 0, 0))
    sink_spec = pl.BlockSpec(memory_space=pltpu.SMEM)
    wide = pl.BlockSpec((BLOCK, Q_LANES), lambda m, n: (n, m))
    gate = pl.BlockSpec((BLOCK, Q_LANES), lambda m, n: (n, ZA_COL0 + m))
    pair = pl.BlockSpec((BLOCK, SLAB), lambda m, n: (n, m))
    return [sink_spec, q_spec, k_prev, k_cur, v_prev, v_cur, bias_spec, gate], wide, pair


def _low_lanes(shape):
    return lax.broadcasted_iota(jnp.int32, shape, 1) < HEAD_DIM


def _pair_halves(ref):
    kb = ref[...]
    sw = pltpu.roll(kb, HEAD_DIM, 1)
    lo = _low_lanes(kb.shape)
    zero = jnp.zeros_like(kb)
    first = (jnp.where(lo, kb, zero).astype(MXU_DTYPE), jnp.where(lo, zero, sw).astype(MXU_DTYPE))
    second = (jnp.where(lo, sw, zero).astype(MXU_DTYPE), jnp.where(lo, zero, kb).astype(MXU_DTYPE))
    return first, second


def _fold_pair(acc):
    f = [x + pltpu.roll(x, HEAD_DIM, 1) for x in acc]
    return jnp.where(_low_lanes(f[0].shape), f[0], f[1])


def _from_prev(n):
    row = lax.broadcasted_iota(jnp.int32, (BLOCK, BLOCK), 0)
    col = lax.broadcasted_iota(jnp.int32, (BLOCK, BLOCK), 1)
    prev = col > row
    return prev, jnp.where(jnp.logical_and(n == 0, prev), NEG_INF, 0.0)


def _softmax_sink(s, sink):
    m = jnp.maximum(jnp.max(s, axis=1, keepdims=True), sink)
    e = jnp.exp(s - m)
    es = jnp.exp(sink - m)
    inv = 1.0 / (jnp.sum(e, axis=1, keepdims=True) + es)
    return e * inv, es * inv


def _stack_pair(prev_halves, own_halves, a):
    return jnp.concatenate([prev_halves[a][0], prev_halves[a][1], own_halves[a][0], own_halves[a][1]], axis=0)


def _split_heads(x4, prev):
    return [jnp.where(prev, x4[:, e * BLOCK:(e + 1) * BLOCK], x4[:, (2 + e) * BLOCK:(3 + e) * BLOCK]) for e in range(2)]


def _spread_heads(x, prev):
    return jnp.concatenate([jnp.where(prev, x[0], 0.0), jnp.where(prev, x[1], 0.0),
                            jnp.where(prev, 0.0, x[0]), jnp.where(prev, 0.0, x[1])], axis=1)


def _attn_forward(proj, bias, sinks):
    t = proj.shape[0]
    in_specs, wide, _ = _attn_specs()

    def body(sink_ref, q_ref, kp_ref, kc_ref, vp_ref, vc_ref, bias_ref, z_ref, o_ref, h_ref):
        m, n = pl.program_id(0), pl.program_id(1)
        kp, kc, vp, vc = (_pair_halves(r) for r in (kp_ref, kc_ref, vp_ref, vc_ref))
        keys = [_stack_pair(kp, kc, a) for a in range(KV_PAIR)]
        vals = [_stack_pair(vp, vc, a) for a in range(KV_PAIR)]
        prev, edge = _from_prev(n)
        prev2 = jnp.concatenate([prev, prev], axis=0)
        for a in range(KV_PAIR):
            slabs = (2 * a, 2 * a + 1)
            q = jnp.concatenate([q_ref[:, s * SLAB:(s + 1) * SLAB] for s in slabs], axis=0).astype(MXU_DTYPE)
            logits = _split_heads(_dot_nt(q, keys[a]), prev2)
            probs = []
            for e in range(2):
                rows = [_softmax_sink(logits[e][r * BLOCK:(r + 1) * BLOCK] * ATTN_SCALE + bias_ref[2 * s + e] + edge,
                                      sink_ref[m * HEADS_PER_STEP + 2 * s + e])[0] for r, s in enumerate(slabs)]
                probs.append(jnp.concatenate(rows, axis=0))
            out = _dot(_spread_heads(probs, prev2), vals[a])
            for r, s in enumerate(slabs):
                cols = slice(s * SLAB, (s + 1) * SLAB)
                o_ref[:, cols] = out[r * BLOCK:(r + 1) * BLOCK]
                h_ref[:, cols] = (out[r * BLOCK:(r + 1) * BLOCK] * _silu(z_ref[:, cols])).astype(h_ref.dtype)

    return pl.pallas_call(
        body, name="attn_fwd", grid=(N_KV_HEADS // KV_PAIR, t // BLOCK),
        in_specs=in_specs, out_specs=[wide, wide],
        out_shape=[jax.ShapeDtypeStruct((t, D_ATTN), F32), jax.ShapeDtypeStruct((t, D_ATTN), MXU_DTYPE)],
        compiler_params=pltpu.CompilerParams(dimension_semantics=("parallel", "arbitrary")),
    )(sinks, proj, proj, proj, proj, proj, bias, proj)


def _attn_backward(proj, bias, sinks, attn, dh):
    t = proj.shape[0]
    in_specs, wide, pair = _attn_specs()
    bias_spec = in_specs[-2]
    sink_out = pl.BlockSpec((HEADS_PER_STEP, 8, 128), lambda m, n: (m, 0, 0))

    def body(sink_ref, q_ref, kp_ref, kc_ref, vp_ref, vc_ref, bias_ref, z_ref, o_ref, dh_ref,
             dq_ref, dz_ref, dka_ref, dkb_ref, dva_ref, dvb_ref, dbias_ref, dsink_ref):
        m, n = pl.program_id(0), pl.program_id(1)

        @pl.when(n == 0)
        def _():
            dbias_ref[...] = jnp.zeros_like(dbias_ref)
            dsink_ref[...] = jnp.zeros_like(dsink_ref)

        kp, kc, vp, vc = (_pair_halves(r) for r in (kp_ref, kc_ref, vp_ref, vc_ref))
        keys = [_stack_pair(kp, kc, a) for a in range(KV_PAIR)]
        vals = [_stack_pair(vp, vc, a) for a in range(KV_PAIR)]
        prev, edge = _from_prev(n)
        lo = _low_lanes((BLOCK, SLAB))
        prev2 = jnp.concatenate([prev, prev], axis=0)
        dk = [[None] * KV_PAIR for _ in range(2)]
        dv = [[None] * KV_PAIR for _ in range(2)]
        for a in range(KV_PAIR):
            slabs = (2 * a, 2 * a + 1)
            q = jnp.concatenate([q_ref[:, s * SLAB:(s + 1) * SLAB] for s in slabs], axis=0).astype(MXU_DTYPE)
            gated = []
            for s in slabs:
                cols = slice(s * SLAB, (s + 1) * SLAB)
                dh_s, z_s = dh_ref[:, cols], z_ref[:, cols]
                dz_ref[:, cols] = (dh_s * o_ref[:, cols] * _silu_grad(z_s)).astype(dz_ref.dtype)
                gated.append(dh_s * _silu(z_s))
            do = jnp.concatenate(gated, axis=0).astype(MXU_DTYPE)
            logits = _split_heads(_dot_nt(q, keys[a]), prev2)
            dprobs = _split_heads(_dot_nt(do, vals[a]), prev2)
            probs, dlogits = [], []
            for e in range(2):
                p_rows, ds_rows = [], []
                for r, s in enumerate(slabs):
                    h = 2 * s + e
                    rows = slice(r * BLOCK, (r + 1) * BLOCK)
                    p, ps = _softmax_sink(logits[e][rows] * ATTN_SCALE + bias_ref[h] + edge,
                                          sink_ref[m * HEADS_PER_STEP + h])
                    delta = jnp.sum(p * dprobs[e][rows], axis=1, keepdims=True)
                    ds = p * (dprobs[e][rows] - delta)
                    dbias_ref[h] += ds
                    dsink_ref[h] += jnp.broadcast_to(jnp.sum(-ps * delta, axis=0, keepdims=True), (8, 128))
                    p_rows.append(p)
                    ds_rows.append(ds)
                probs.append(jnp.concatenate(p_rows, axis=0))
                dlogits.append(jnp.concatenate(ds_rows, axis=0))
            ds4 = _spread_heads(dlogits, prev2).astype(MXU_DTYPE)
            p4 = _spread_heads(probs, prev2).astype(MXU_DTYPE)
            dq = _dot(ds4, keys[a]) * ATTN_SCALE
            for r, s in enumerate(slabs):
                dq_ref[:, s * SLAB:(s + 1) * SLAB] = dq[r * BLOCK:(r + 1) * BLOCK].astype(dq_ref.dtype)
            rk = _dot_tn(ds4, q)
            rv = _dot_tn(p4, do)
            for which in range(2):
                top = 2 * which * BLOCK
                dk[which][a] = jnp.where(lo, rk[top:top + BLOCK], rk[top + BLOCK:top + 2 * BLOCK])
                dv[which][a] = jnp.where(lo, rv[top:top + BLOCK], rv[top + BLOCK:top + 2 * BLOCK])
        dkb_ref[...] = _fold_pair(dk[0]) * ATTN_SCALE
        dka_ref[...] = _fold_pair(dk[1]) * ATTN_SCALE
        dvb_ref[...] = _fold_pair(dv[0])
        dva_ref[...] = _fold_pair(dv[1])

    kv_shape = jax.ShapeDtypeStruct((t, D_KV), F32)
    return pl.pallas_call(
        body, name="attn_bwd", grid=(N_KV_HEADS // KV_PAIR, t // BLOCK),
        in_specs=in_specs + [wide, wide],
        out_specs=[wide, wide, pair, pair, pair, pair, bias_spec, sink_out],
        out_shape=[jax.ShapeDtypeStruct((t, D_ATTN), MXU_DTYPE), jax.ShapeDtypeStruct((t, D_ATTN), MXU_DTYPE),
                   kv_shape, kv_shape, kv_shape, kv_shape,
                   jax.ShapeDtypeStruct(bias.shape, F32), jax.ShapeDtypeStruct((N_Q_HEADS, 8, 128), F32)],
        compiler_params=pltpu.CompilerParams(dimension_semantics=("parallel", "arbitrary")),
    )(sinks, proj, proj, proj, proj, proj, bias, proj, attn, dh)


def _shift_blocks(cur, prev):
    return cur + jnp.concatenate([prev[BLOCK:], jnp.zeros_like(prev[:BLOCK])], axis=0)


def _mesh_pos():
    return lax.axis_index("x"), lax.axis_index("y"), lax.axis_index("c")


_HBM = pl.BlockSpec(memory_space=pltpu.HBM)
_SEM = pl.BlockSpec(memory_space=pltpu.SEMAPHORE)
_DATAFLOW = pltpu.SideEffectType.DATAFLOW_SIDE_EFFECTING


def _peers():
    x, y, c = _mesh_pos()
    others = []
    for k in range(1, N_DEV):
        px, py, pc = x ^ (k >> 2), y ^ ((k >> 1) & 1), c ^ (k & 1)
        others.append(((px, py, pc), 4 * px + 2 * py + pc))
    return 4 * x + 2 * y + c, others


def _split_start(bufs, plan, n_copies, *, name):
    nb = len(bufs)

    def body(*refs):
        send_sems, recv_sems, token = refs[nb], refs[nb + 1], refs[-1]
        for k, (src, dst, pos, _) in enumerate(plan(*refs[:nb])):
            pltpu.make_async_remote_copy(src_ref=src, dst_ref=dst, send_sem=send_sems.at[k], recv_sem=recv_sems.at[k],
                                         device_id=pos, device_id_type=pl.DeviceIdType.MESH).start()
        token[...] = jnp.zeros_like(token)

    return pl.pallas_call(
        body, name=name,
        out_shape=(pltpu.SemaphoreType.DMA((n_copies,)), pltpu.SemaphoreType.DMA((n_copies,)),
                   *[pltpu.HBM(b.shape, b.dtype) for b in bufs], jax.ShapeDtypeStruct((8, 128), F32)),
        in_specs=(_HBM,) * nb, out_specs=(_SEM, _SEM) + (_HBM,) * nb + (pl.BlockSpec(memory_space=pltpu.VMEM),),
        input_output_aliases={i: 2 + i for i in range(nb)},
        compiler_params=pltpu.CompilerParams(has_side_effects=_DATAFLOW),
    )(*[pltpu.with_memory_space_constraint(b, pltpu.HBM) for b in bufs])


def _split_wait(started, plan, after, *, name):
    send_sems, recv_sems, *thru = started[:-1]
    nb = len(thru)

    def body(*refs):
        send_sems, recv_sems = refs[nb], refs[nb + 1]
        for k, (src, _, pos, arrive) in enumerate(plan(*refs[:nb])):
            copy = pltpu.make_async_remote_copy(
                src_ref=src, dst_ref=arrive, send_sem=send_sems.at[k], recv_sem=recv_sems.at[k],
                device_id=pos, device_id_type=pl.DeviceIdType.MESH)
            copy.wait_send()
            copy.wait_recv()

    return pl.pallas_call(
        body, name=name,
        out_shape=tuple(pltpu.HBM(b.shape, b.dtype) for b in thru),
        in_specs=(_HBM,) * nb + (_SEM, _SEM, pl.BlockSpec(memory_space=pl.ANY)), out_specs=(_HBM,) * nb,
        input_output_aliases={i: i for i in range(nb)},
        compiler_params=pltpu.CompilerParams(has_side_effects=_DATAFLOW),
    )(*thru, send_sems, recv_sems, after)


def _plan_scatter(x_ref, land_ref):
    me, others = _peers()
    return [(x_ref.at[idx], land_ref.at[me], pos, land_ref.at[idx]) for pos, idx in others]


def _plan_gather(x_ref, land_ref):
    me, others = _peers()
    return [(x_ref, land_ref.at[me], pos, land_ref.at[idx]) for pos, idx in others]


def _near_and_far():
    x, y, c = _mesh_pos()
    chips = [(1 - x, y), (x, 1 - y), (1 - x, 1 - y)]
    near = [(x, y, 1 - c)] + [(px, py, c) for px, py in chips]
    relay = [(4 * px + 2 * py + c, 4 * px + 2 * py + 1 - c) for px, py in chips]
    return 4 * x + 2 * y + c, near, (x, y, 1 - c), relay


def _plan_gather_near(x_ref, land_ref):
    me, near, _, _ = _near_and_far()
    return [(x_ref, land_ref.at[me], pos, land_ref.at[4 * pos[0] + 2 * pos[1] + pos[2]]) for pos in near]


def _plan_gather_relay(land_ref):
    _, _, sibling, relay = _near_and_far()
    return [(land_ref.at[mine], land_ref.at[mine], sibling, land_ref.at[theirs]) for mine, theirs in relay]


def _landing_zone(own):
    me, _ = _peers()
    return lax.dynamic_update_index_in_dim(lax.empty((N_DEV,) + own.shape, own.dtype), own, me, 0)


def _scatter_start(x, *, name):
    if x.ndim == 2:
        return _split_start((x, _landing_zone(x)), _plan_gather, N_DEV - 1, name=name)
    me, _ = _peers()
    own = lax.dynamic_index_in_dim(x, me, 0, keepdims=False)
    return _split_start((x, _landing_zone(own)), _plan_scatter, N_DEV - 1, name=name)


def _scatter_wait(started, after, *, name):
    plan = _plan_gather if started[2].ndim == 2 else _plan_scatter
    return _split_wait(started, plan, after, name=name)[1]


def _adamw_math(w, g, m, v):
    m = ADAM_B1 * m + (1.0 - ADAM_B1) * g
    v = ADAM_B2 * v + (1.0 - ADAM_B2) * (g * g)
    m_hat = m / (1.0 - ADAM_B1 ** ADAM_STEP)
    v_hat = v / (1.0 - ADAM_B2 ** ADAM_STEP)
    delta = -ADAM_LR * (m_hat / (jnp.sqrt(v_hat) + ADAM_EPS) + ADAM_WD * w)
    return delta, m, v


def _adamw_reduce(parts, w, m, v, *, name, tr):
    r, c = w.shape
    tr = min(tr, r)
    spec = pl.BlockSpec((tr, c), lambda i: (i, 0))

    def body(p_ref, w_ref, m_ref, v_ref, g_ref, d_ref, nm_ref, nv_ref):
        g = p_ref[0].astype(F32)
        for s in range(1, N_DEV):
            g = g + p_ref[s].astype(F32)
        delta, nm, nv = _adamw_math(w_ref[...], g, m_ref[...], v_ref[...])
        g_ref[...] = g
        d_ref[...] = delta
        nm_ref[...] = nm
        nv_ref[...] = nv

    return pl.pallas_call(
        body, name=name, grid=(r // tr,),
        in_specs=[pl.BlockSpec((N_DEV, tr, c), lambda i: (0, i, 0)), spec, spec, spec],
        out_specs=[spec] * 4,
        out_shape=[jax.ShapeDtypeStruct((r, c), F32)] * 4,
        compiler_params=pltpu.CompilerParams(
            dimension_semantics=("parallel",),
            vmem_limit_bytes=_vmem_limit(2 * 15 * _nbytes((tr, c), F32))),
    )(parts, w, m, v)


def _sum_parts(parts, *, name):
    def body(p_ref, o_ref):
        g = p_ref[0]
        for s in range(1, N_DEV):
            g = g + p_ref[s]
        o_ref[...] = g

    return pl.pallas_call(
        body, name=name, out_shape=jax.ShapeDtypeStruct(parts.shape[1:], F32),
        compiler_params=pltpu.CompilerParams(vmem_limit_bytes=_vmem_limit(_nbytes(parts.shape, F32))),
    )(parts)


def _adamw_native(w, g, m, v, *, name):
    def body(w_ref, g_ref, m_ref, v_ref, d_ref, nm_ref, nv_ref):
        d_ref[...], nm_ref[...], nv_ref[...] = _adamw_math(w_ref[...], g_ref[...], m_ref[...], v_ref[...])

    return pl.pallas_call(body, name=name, out_shape=[jax.ShapeDtypeStruct(w.shape, F32)] * 3)(w, g, m, v)


SMALL = ["ssm_lambda_re", "ssm_lambda_im", "ssm_b_re", "ssm_b_im", "ssm_c_re", "ssm_c_im", "ssm_d",
         "ssm_log_step", "attn_sinks", "rel_bias_table", "ln_gain", "ln_bias"]


def _pack(arrs):
    flat = jnp.concatenate([a.reshape(-1) for a in arrs])
    pad = (-flat.shape[0]) % 1024
    return jnp.pad(flat, (0, pad)).reshape(-1, 128)


def _unpack(packed, like):
    flat = packed.reshape(-1)
    out, pos = [], 0
    for a in like:
        out.append(flat[pos:pos + a.size].reshape(a.shape))
        pos += a.size
    return out


def kernel(x, w_in, ssm_lambda_re, ssm_lambda_im, ssm_b_re, ssm_b_im, ssm_c_re, ssm_c_im, ssm_d, ssm_log_step, w_glu, attn_sinks, rel_bias_table, w_branch_ssm, w_branch_attn, w_out, ln_gain, ln_bias, loss_target, m_w_in, m_ssm_lambda_re, m_ssm_lambda_im, m_ssm_b_re, m_ssm_b_im, m_ssm_c_re, m_ssm_c_im, m_ssm_d, m_ssm_log_step, m_w_glu, m_attn_sinks, m_rel_bias_table, m_w_branch_ssm, m_w_branch_attn, m_w_out, m_ln_gain, m_ln_bias, v_w_in, v_ssm_lambda_re, v_ssm_lambda_im, v_ssm_b_re, v_ssm_b_im, v_ssm_c_re, v_ssm_c_im, v_ssm_d, v_ssm_log_step, v_w_glu, v_attn_sinks, v_rel_bias_table, v_w_branch_ssm, v_w_branch_attn, v_w_out, v_ln_gain, v_ln_bias):
    t = x.shape[1]
    xs = x[0]
    target = loss_target[0]
    col_in = w_in.shape[2]
    col_br = w_glu.shape[2]
    row_out = w_out.shape[1]

    w_in_mx = w_in[0].astype(MXU_DTYPE)
    near = _split_start((w_in_mx, _landing_zone(w_in_mx)), _plan_gather_near, 4, name="gather_w_in_near_start")
    ssm_params = (ssm_lambda_re[0], ssm_lambda_im[0], ssm_b_re[0], ssm_b_im[0], ssm_c_re[0], ssm_c_im[0],
                  ssm_log_step[0] + near[-1][0, 0])
    mats, mats_vjp = jax.vjp(_ssm_matrices, *ssm_params)
    mats_mx = _ssm_expand(*mats[:3]) + (mats[3],)
    sinks = attn_sinks[0]
    d_skip = ssm_d
    _, landed = _split_wait(near, _plan_gather_near, mats_mx[1], name="gather_w_in_near_wait")
    relay = _split_start((landed,), _plan_gather_relay, 3, name="gather_w_in_relay_start")
    bias, bias_vjp = jax.vjp(_band_bias, rel_bias_table + relay[-1][0, 0])
    x_mx = xs.astype(MXU_DTYPE)
    (g_in,) = _split_wait(relay, _plan_gather_relay, bias, name="gather_w_in_relay_wait")
    win = g_in.transpose(1, 0, 2).reshape(D_MODEL, D_IN)
    hold = g_in[0, 0, 0] * 0
    three = jnp.concatenate([w_glu[0], w_branch_ssm[0], w_branch_attn[0]], axis=0).astype(MXU_DTYPE)
    sent_three = _scatter_start(three + hold, name="gather_w_1024_start")
    sent_wout = _scatter_start(w_out[0].astype(MXU_DTYPE) + hold, name="gather_w_out_start")

    proj = _mm(x_mx, win, name="in_proj", tm=2048, tn=512, tk=2048, after=(sent_three[4], sent_wout[4]))
    y_conv, states = _ssm_forward(proj, mats_mx)

    def f_gelu(yv, u, d):
        ys = yv + d * u
        return ys, _gelu(ys)

    y_ssm, glu_in = _ew(f_gelu, [(y_conv, "row", 0), (proj, "row", OFF_U), (d_skip, "vec", 0)],
                        [(COL, F32, "row"), (COL, MXU_DTYPE, "row")], rows=t, cw=COL, ncb=4, tr=1024, name="ssm_gelu")
    g_three = _scatter_wait(sent_three, glu_in, name="gather_w_1024_wait")
    three_full = g_three.transpose(1, 0, 2).reshape(3 * D_SSM, N_DEV * col_br)
    wglu, wbs, wba = three_full[:D_SSM], three_full[D_SSM:2 * D_SSM], three_full[2 * D_SSM:]
    gate_tn = 512
    z_ssm_col = [(proj, OFF_ZS * COL // gate_tn)]

    def f_hssm(products, gate):
        (ga, gb), (z,) = products, gate
        return ga, gb, ga * _sigmoid(gb) * _silu(z)

    glu_a, glu_b, h_ssm = _mm_fused([(glu_in, wglu[:, :D_SSM], False), (glu_in, wglu[:, D_SSM:], False)], z_ssm_col,
                                    [F32, F32, MXU_DTYPE], f_hssm, name="glu_gate", tm=1024, tn=gate_tn)

    attn, h_attn = _attn_forward(proj, bias, sinks)
    gate_cols = [(proj, OFF_GS * COL // gate_tn), (proj, OFF_GA * COL // gate_tn)]

    def f_merge(products, gates):
        (ps, pa), (ls, la) = products, gates
        return ps, pa, _sigmoid(ls) * ps + _sigmoid(la) * pa

    p_ssm, p_attn, merged = _mm_fused([(h_ssm, wbs, False), (h_attn, wba, False)], gate_cols, [F32, F32, MXU_DTYPE],
                                      f_merge, name="branch_merge", tm=1024, tn=gate_tn)
    wout = _scatter_wait(sent_wout, merged, name="gather_w_out_wait").reshape(D_MODEL, D_MODEL)
    out = _mm(merged, wout, name="out_proj", tm=2048, tn=512, tk=2048)

    def f_norm(xv, ov, tg, gain, lbias):
        r = DEEPNORM_ALPHA * xv + ov
        mu = jnp.mean(r, axis=1, keepdims=True)
        cen = r - mu
        var = jnp.mean(cen * cen, axis=1, keepdims=True)
        rstd = lax.rsqrt(var + LN_EPS)
        xhat = cen * rstd
        yv = xhat * gain + lbias
        diff = yv - tg
        row_loss = 0.5 * jnp.mean(diff * diff, axis=1, keepdims=True)
        loss = jnp.broadcast_to(jnp.sum(row_loss, axis=0, keepdims=True), (1, 128))
        dy = diff * (1.0 / D_MODEL)
        dgain = jnp.sum(dy * xhat, axis=0, keepdims=True)
        dbias = jnp.sum(dy, axis=0, keepdims=True)
        dxh = dy * gain
        dr = rstd * (dxh - jnp.mean(dxh, axis=1, keepdims=True) - xhat * jnp.mean(dxh * xhat, axis=1, keepdims=True))
        return dr, loss, dgain, dbias

    dr, loss_part, g_ln_gain, g_ln_bias = _ew(
        f_norm, [(xs, "row", 0), (out, "row", 0), (target, "row", 0), (ln_gain, "vec", 0), (ln_bias, "vec", 0)],
        [(D_MODEL, F32, "row"), (128, F32, "acc"), (D_MODEL, F32, "acc"), (D_MODEL, F32, "acc")],
        rows=t, cw=D_MODEL, ncb=1, tr=256, name="norm_loss")

    def scatter_cols(g, cols):
        return g.reshape(g.shape[0], N_DEV, cols).transpose(1, 0, 2)

    gw_out = _mm(merged, dr, ta=True, out_dtype=WIRE_DTYPE, name="grad_w_out", tm=2048, tn=512, tk=2048)
    sent_out = _scatter_start(gw_out.reshape(N_DEV, row_out, D_MODEL), name="scatter_g_out_start")
    def b_merge(products, tiles):
        (dm,), (ps, pa, ls, la) = products, tiles
        gs, ga = _sigmoid(ls), _sigmoid(la)
        return dm * gs, dm * ga, dm * ps * gs * (1.0 - gs), dm * pa * ga * (1.0 - ga)

    dp_ssm, dp_attn, dgl_s, dgl_a = _mm_fused(
        [(dr, wout, True)], [(p_ssm, 0), (p_attn, 0), (proj, OFF_GS), (proj, OFF_GA)], [MXU_DTYPE] * 4, b_merge,
        name="merge_bwd", tm=1024, tn=COL, after=(sent_out[4],))
    gw_bs = _mm(h_ssm, dp_ssm, ta=True, out_dtype=WIRE_DTYPE, name="grad_w_branch_ssm", tm=1024, tn=512, tk=2048)
    gw_ba = _mm(h_attn, dp_attn, ta=True, out_dtype=WIRE_DTYPE, name="grad_w_branch_attn", tm=1024, tn=512, tk=2048)
    dh_attn = _mm(dp_attn, wba, tb=True, name="d_h_attn", tm=2048, tn=512, tk=2048)

    def b_hssm(products, tiles):
        (dh,), (ga, gb, z) = products, tiles
        sg = _sigmoid(gb)
        dgate = dh * _silu(z)
        return dgate * sg, dgate * ga * sg * (1.0 - sg), dh * ga * sg * _silu_grad(z)

    dglu_a, dglu_b, dz_ssm = _mm_fused([(dp_ssm, wbs, True)], [(glu_a, 0), (glu_b, 0)] + z_ssm_col, [MXU_DTYPE] * 3,
                                       b_hssm, name="glu_gate_bwd", tm=1024, tn=gate_tn)
    dglu = jnp.concatenate([dglu_a, dglu_b], axis=1)
    gw_glu = _mm(glu_in, dglu, ta=True, out_dtype=WIRE_DTYPE, name="grad_w_glu", tm=1024, tn=512, tk=2048)
    sent_three = _scatter_start(scatter_cols(jnp.concatenate([gw_glu, gw_bs, gw_ba], axis=0), col_br),
                                name="scatter_g_1024_start")
    def b_gelu(products, tiles):
        return (products[0] * _gelu_grad(tiles[0]),)

    (dy_ssm,) = _mm_fused([(dglu, wglu, True)], [(y_ssm, 0)], [F32], b_gelu, name="gelu_bwd", tm=1024, tn=gate_tn,
                          after=(sent_three[4],))
    du_ssm, g_ssm_d, dmats = _ssm_backward(dy_ssm, proj, states, mats_mx)
    du = (du_ssm + d_skip * dy_ssm).astype(MXU_DTYPE)
    g_lre, g_lim, g_bre, g_bim, g_cre, g_cim, g_lstep = mats_vjp(dmats)

    dq, dz_attn, dka, dkb, dva, dvb, dbias, dsink = _attn_backward(proj, bias, sinks, attn, dh_attn)
    dk = _shift_blocks(dka, dkb).astype(MXU_DTYPE)
    dv = _shift_blocks(dva, dvb).astype(MXU_DTYPE)
    (g_table,) = bias_vjp(dbias)
    g_sinks = dsink[:, 0, 0]

    dproj = jnp.concatenate([du, dz_ssm, dq, dk, dv, dz_attn, dgl_s, dgl_a], axis=1)
    small_g = [g_lre, g_lim, g_bre, g_bim, g_cre, g_cim, g_ssm_d, g_lstep, g_sinks, g_table, g_ln_gain, g_ln_bias]
    sent_small = _scatter_start(_pack(small_g), name="gather_g_small_start")
    gw_in = _mm(x_mx, dproj, ta=True, out_dtype=WIRE_DTYPE, name="grad_w_in", tm=2048, tn=512, tk=2048,
                after=(sent_small[4],))
    sent_in = _scatter_start(scatter_cols(gw_in, col_in), name="scatter_g_in_start")
    grad_x = _mm(dproj, win, tb=True, add=dr, add_scale=DEEPNORM_ALPHA, name="grad_x", tm=1024, tn=512, tk=4352,
                 after=(sent_in[4],))

    parts_out = _scatter_wait(sent_out, grad_x, name="scatter_g_out_wait")
    parts_three = _scatter_wait(sent_three, parts_out, name="scatter_g_1024_wait")
    parts_in = _scatter_wait(sent_in, parts_three, name="scatter_g_in_wait")

    o_in = _adamw_reduce(parts_in, w_in[0], m_w_in[0], v_w_in[0], name="adamw_w_in", tr=128)
    three_w = jnp.concatenate([w_glu[0], w_branch_ssm[0], w_branch_attn[0]], axis=0)
    three_m = jnp.concatenate([m_w_glu[0], m_w_branch_ssm[0], m_w_branch_attn[0]], axis=0)
    three_v = jnp.concatenate([v_w_glu[0], v_w_branch_ssm[0], v_w_branch_attn[0]], axis=0)
    o_three = _adamw_reduce(parts_three, three_w, three_m, three_v, name="adamw_w_1024", tr=512)
    o_out = _adamw_reduce(parts_out, w_out[0], m_w_out[0], v_w_out[0], name="adamw_w_out", tr=128)

    small_w = [ssm_lambda_re, ssm_lambda_im, ssm_b_re, ssm_b_im, ssm_c_re, ssm_c_im, ssm_d, ssm_log_step,
               attn_sinks, rel_bias_table, ln_gain, ln_bias]
    small_m = [m_ssm_lambda_re, m_ssm_lambda_im, m_ssm_b_re, m_ssm_b_im, m_ssm_c_re, m_ssm_c_im, m_ssm_d,
               m_ssm_log_step, m_attn_sinks, m_rel_bias_table, m_ln_gain, m_ln_bias]
    small_v = [v_ssm_lambda_re, v_ssm_lambda_im, v_ssm_b_re, v_ssm_b_im, v_ssm_c_re, v_ssm_c_im, v_ssm_d,
               v_ssm_log_step, v_attn_sinks, v_rel_bias_table, v_ln_gain, v_ln_bias]
    parts_small = _scatter_wait(sent_small, parts_in, name="gather_g_small_wait")
    sg = _unpack(_sum_parts(parts_small, name="sum_g_small"), small_w)
    updates = [_adamw_native(w, g, m, v, name="adamw_" + n)
               for n, w, g, m, v in zip(SMALL, small_w, sg, small_m, small_v)]
    sd, sm, sv = zip(*updates)

    loss = lax.psum(loss_part[0, 0], MESH_AXES)

    def big(o, idx):
        g_in_, g_three_, g_out_ = o_in[idx], o_three[idx], o_out[idx]
        return {"w_in": g_in_[None], "w_glu": g_three_[None, :D_SSM], "w_branch_ssm": g_three_[None, D_SSM:2 * D_SSM],
                "w_branch_attn": g_three_[None, 2 * D_SSM:], "w_out": g_out_[None]}

    order = ["w_in", "ssm_lambda_re", "ssm_lambda_im", "ssm_b_re", "ssm_b_im", "ssm_c_re", "ssm_c_im", "ssm_d",
             "ssm_log_step", "w_glu", "attn_sinks", "rel_bias_table", "w_branch_ssm", "w_branch_attn", "w_out",
             "ln_gain", "ln_bias"]
    outs = [loss, grad_x[None]]
    for idx, small in enumerate([sg, sd, sm, sv]):
        table = big(None, idx)
        table.update(dict(zip(SMALL, small)))
        outs += [table[n] for n in order]
    return tuple(outs)
```

```python
import math

import jax
import jax.numpy as jnp
from jax import lax
from jax.experimental import pallas as pl
from jax.experimental.pallas import tpu as pltpu

F32 = jnp.float32
MXU_DTYPE = jnp.bfloat16
WIRE_DTYPE = jnp.bfloat16

D_MODEL = 2048
D_SSM = 1024
SSM_GROUP = 16
N_GROUPS = 64
SSM_STATE = 64
N_Q_HEADS = 16
N_KV_HEADS = 4
Q_PER_KV = 4
HEAD_DIM = 64
D_ATTN = 1024
D_KV = 256
WINDOW = 128
BLOCK = 128
N_BUCKETS = 32
MAX_DISTANCE = 128
D_IN = 8704
DEEPNORM_ALPHA = 2.0 ** 0.25
LN_EPS = 1e-5
NEG_INF = -1e30
ATTN_SCALE = HEAD_DIM ** -0.5

ADAM_LR = 0.001
ADAM_B1 = 0.9
ADAM_B2 = 0.999
ADAM_EPS = 1e-08
ADAM_WD = 0.01
ADAM_STEP = 10

N_DEV = 8
SSM_CHUNK = 16
GROUP_BLOCK = 8
N_GB = N_GROUPS // GROUP_BLOCK
GB_LANES = GROUP_BLOCK * SSM_GROUP
GB_STATE = GROUP_BLOCK * SSM_STATE
COL = 256
OFF_U, OFF_ZS, OFF_Q, OFF_K, OFF_V, OFF_ZA, OFF_GS, OFF_GA = 0, 4, 8, 12, 13, 14, 18, 26

VMEM_CAP = 56 * 1024 * 1024
MESH_AXES = ("x", "y", "c")


def _vmem_limit(block_bytes):
    return int(min(max(3 * block_bytes, 16 * 1024 * 1024), VMEM_CAP))


def _nbytes(shape, dtype):
    return math.prod(shape) * jnp.dtype(dtype).itemsize


def _tile(n, pref):
    if n <= pref:
        return n
    t = (pref // 128) * 128
    while t >= 128:
        if n % t == 0:
            return t
        t -= 128
    return n


def _mm(a, b, *, name, ta=False, tb=False, out_dtype=F32, tm=1024, tn=512, tk=512, add=None, add_scale=1.0,
        after=()):
    squeeze = a.ndim == 2
    if squeeze:
        a, b = a[None], b[None]
        if add is not None:
            add = add[None]
    nb = a.shape[0]
    m, k = (a.shape[2], a.shape[1]) if ta else (a.shape[1], a.shape[2])
    n = b.shape[1] if tb else b.shape[2]
    tm, tn, tk = _tile(m, tm), _tile(n, tn), _tile(k, tk)
    nk = k // tk
    dn = (((0 if ta else 1,), (1 if tb else 0,)), ((), ()))

    a_spec = (pl.BlockSpec((None, tk, tm), lambda g, i, j, kk: (g, kk, i)) if ta
              else pl.BlockSpec((None, tm, tk), lambda g, i, j, kk: (g, i, kk)))
    b_spec = (pl.BlockSpec((None, tn, tk), lambda g, i, j, kk: (g, j, kk)) if tb
              else pl.BlockSpec((None, tk, tn), lambda g, i, j, kk: (g, kk, j)))
    o_spec = pl.BlockSpec((None, tm, tn), lambda g, i, j, kk: (g, i, j))
    in_specs = [a_spec, b_spec]
    operands = [a, b]
    if add is not None:
        in_specs.append(o_spec)
        operands.append(add)
    for tok in after:
        in_specs.append(pl.BlockSpec(memory_space=pl.ANY))
        operands.append(tok)
    n_in = len(operands)

    def body(*refs):
        a_ref, b_ref = refs[0], refs[1]
        add_ref = refs[2] if add is not None else None
        o_ref = refs[n_in]
        acc_ref = refs[-1]
        kk = pl.program_id(3)
        part = lax.dot_general(a_ref[...].astype(MXU_DTYPE), b_ref[...].astype(MXU_DTYPE), dn,
                               preferred_element_type=F32)

        def finish(r):
            if add_ref is not None:
                r = r + add_scale * add_ref[...]
            o_ref[...] = r.astype(out_dtype)

        if nk == 1:
            finish(part)
            return

        @pl.when(kk == 0)
        def _():
            acc_ref[...] = part

        @pl.when(jnp.logical_and(kk > 0, kk < nk - 1))
        def _():
            acc_ref[...] += part

        @pl.when(kk == nk - 1)
        def _():
            finish(acc_ref[...] + part)

    blocks = (_nbytes((tm, tk), a.dtype) + _nbytes((tk, tn), b.dtype) + _nbytes((tm, tn), out_dtype)
              + (_nbytes((tm, tn), F32) if add is not None else 0))
    out = pl.pallas_call(
        body,
        name=name,
        grid=(nb, m // tm, n // tn, nk),
        in_specs=in_specs,
        out_specs=o_spec,
        out_shape=jax.ShapeDtypeStruct((nb, m, n), out_dtype),
        scratch_shapes=[pltpu.VMEM((tm, tn), F32)],
        compiler_params=pltpu.CompilerParams(
            dimension_semantics=("parallel", "parallel", "parallel", "arbitrary"),
            vmem_limit_bytes=_vmem_limit(2 * blocks + 2 * _nbytes((tm, tn), F32))),
    )(*operands)
    return out[0] if squeeze else out


def _mm_fused(pairs, extras, out_dtypes, epilogue, *, name, tm, tn, after=()):
    m = pairs[0][0].shape[0]
    n = pairs[0][1].shape[0] if pairs[0][2] else pairs[0][1].shape[1]
    tm, tn = _tile(m, tm), _tile(n, tn)
    in_specs, operands, dns = [], [], []
    for a, b, tb in pairs:
        k = a.shape[1]
        in_specs += [pl.BlockSpec((tm, k), lambda i, j: (i, 0)),
                     pl.BlockSpec((tn, k), lambda i, j: (j, 0)) if tb else pl.BlockSpec((k, tn), lambda i, j: (0, j))]
        operands += [a, b]
        dns.append((((1,), (1 if tb else 0,)), ((), ())))

    def tile_at(col):
        return pl.BlockSpec((tm, tn), lambda i, j: (i, col + j))

    in_specs += [tile_at(col) for _, col in extras]
    operands += [arr for arr, _ in extras]
    in_specs += [pl.BlockSpec(memory_space=pl.ANY)] * len(after)
    operands += list(after)
    n_pairs, n_extra, n_in = len(pairs), len(extras), len(operands)

    def body(*refs):
        products = [lax.dot_general(refs[2 * p][...].astype(MXU_DTYPE), refs[2 * p + 1][...].astype(MXU_DTYPE),
                                    dns[p], preferred_element_type=F32) for p in range(n_pairs)]
        tiles = [refs[2 * n_pairs + e][...] for e in range(n_extra)]
        for o_ref, val in zip(refs[n_in:], epilogue(products, tiles)):
            o_ref[...] = val.astype(o_ref.dtype)

    blocks = sum(_nbytes((tm, a.shape[1]), a.dtype) + _nbytes((tn, a.shape[1]), b.dtype) for a, b, _ in pairs)
    blocks += sum(_nbytes((tm, tn), arr.dtype) for arr, _ in extras) + sum(_nbytes((tm, tn), dt) for dt in out_dtypes)
    return pl.pallas_call(
        body, name=name, grid=(m // tm, n // tn), in_specs=in_specs,
        out_specs=[tile_at(0)] * len(out_dtypes),
        out_shape=[jax.ShapeDtypeStruct((m, n), dt) for dt in out_dtypes],
        compiler_params=pltpu.CompilerParams(
            dimension_semantics=("parallel", "parallel"),
            vmem_limit_bytes=_vmem_limit(2 * blocks + n_pairs * _nbytes((tm, tn), F32))),
    )(*operands)


def _ew(fn, ins, outs, *, rows, cw, ncb, tr, name):
    tr = min(tr, rows)
    n_in = len(ins)

    def row_map(off):
        return lambda j, i: (i, off + j)

    def vec_map(off):
        return lambda j, i: (0, off + j)

    in_specs = []
    for arr, kind, off in ins:
        if kind == "row":
            in_specs.append(pl.BlockSpec((tr, cw), row_map(off)))
        else:
            in_specs.append(pl.BlockSpec((1, cw), vec_map(off)))
    out_specs, out_shapes = [], []
    for bw, dt, kind in outs:
        if kind == "row":
            out_specs.append(pl.BlockSpec((tr, bw), row_map(0)))
            out_shapes.append(jax.ShapeDtypeStruct((rows, ncb * bw), dt))
        else:
            out_specs.append(pl.BlockSpec((1, bw), vec_map(0)))
            out_shapes.append(jax.ShapeDtypeStruct((1, ncb * bw), F32))

    def body(*refs):
        i = pl.program_id(1)
        vals = fn(*[r[...] for r in refs[:n_in]])
        for r, (bw, dt, kind), v in zip(refs[n_in:], outs, vals):
            if kind == "row":
                r[...] = v.astype(dt)
            else:
                @pl.when(i == 0)
                def _(r=r):
                    r[...] = jnp.zeros_like(r)

                r[...] += v

    blocks = sum(_nbytes((tr, cw), a.dtype) for a, kind, _ in ins if kind == "row")
    blocks += sum(_nbytes((tr, bw), dt) for bw, dt, kind in outs if kind == "row")
    res = pl.pallas_call(
        body,
        name=name,
        grid=(ncb, rows // tr),
        in_specs=in_specs,
        out_specs=out_specs,
        out_shape=out_shapes,
        compiler_params=pltpu.CompilerParams(
            dimension_semantics=("parallel", "arbitrary"),
            vmem_limit_bytes=_vmem_limit(4 * blocks)),
    )(*[a for a, _, _ in ins])
    return res


def _sigmoid(x):
    return 1.0 / (1.0 + jnp.exp(-x))


INV_SQRT2 = 0.7071067811865476
INV_SQRT_2PI = 0.3989422804014327


def _gelu(x):
    return 0.5 * x * (1.0 + lax.erf(x * INV_SQRT2))


def _gelu_grad(x):
    return 0.5 * (1.0 + lax.erf(x * INV_SQRT2)) + x * INV_SQRT_2PI * jnp.exp(-0.5 * x * x)


def _silu(x):
    return x * _sigmoid(x)


def _silu_grad(x):
    s = _sigmoid(x)
    return s * (1.0 + x * (1.0 - s))


@jax.custom_vjp
def _taps_product(mr, mi, bbr, bbi):
    bbr_t, bbi_t = jnp.transpose(bbr, (0, 2, 1)), jnp.transpose(bbi, (0, 2, 1))
    return jnp.sum(mr[..., None, :] * bbr_t[None, :, None] - mi[..., None, :] * bbi_t[None, :, None], axis=-1)


def _taps_product_fwd(mr, mi, bbr, bbi):
    return _taps_product(mr, mi, bbr, bbi), (mr, mi, bbr, bbi)


def _taps_product_bwd(res, g):
    mr, mi, bbr, bbi = res
    hi = lax.Precision.HIGHEST
    return (jnp.einsum("tghk,gpk->tghp", g, bbr, precision=hi), -jnp.einsum("tghk,gpk->tghp", g, bbi, precision=hi),
            jnp.einsum("tghk,tghp->gpk", g, mr, precision=hi), -jnp.einsum("tghk,tghp->gpk", g, mi, precision=hi))


_taps_product.defvjp(_taps_product_fwd, _taps_product_bwd)


def _ssm_matrices(lam_re, lam_im, b_re, b_im, c_re, c_im, log_step):
    L = SSM_CHUNK
    step = jnp.exp(log_step)[:, None]
    ea, eb = lam_re * step, lam_im * step
    mag = jnp.exp(ea)
    lbr, lbi = mag * jnp.cos(eb), mag * jnp.sin(eb)
    den = lam_re * lam_re + lam_im * lam_im
    nr, ni = lbr - 1.0, lbi
    cr = (nr * lam_re + ni * lam_im) / den
    ci = (ni * lam_re - nr * lam_im) / den
    bbr = cr[..., None] * b_re - ci[..., None] * b_im
    bbi = cr[..., None] * b_im + ci[..., None] * b_re
    taus = jnp.arange(L + 1, dtype=F32)[:, None, None]
    pmag = jnp.exp(taus * ea[None])
    pwr, pwi = pmag * jnp.cos(taus * eb[None]), pmag * jnp.sin(taus * eb[None])
    mr = c_re[None] * pwr[:L, :, None, :] - c_im[None] * pwi[:L, :, None, :]
    mi = c_re[None] * pwi[:L, :, None, :] + c_im[None] * pwr[:L, :, None, :]
    kk = _taps_product(mr, mi, bbr, bbi)
    taps = jnp.transpose(kk.reshape(L, N_GB, GROUP_BLOCK, SSM_GROUP, SSM_GROUP), (1, 0, 2, 4, 3))
    taps = taps.reshape(N_GB, L, GB_LANES, SSM_GROUP)
    rev_r, rev_i = pwr[L - 1 - jnp.arange(L)], pwi[L - 1 - jnp.arange(L)]
    wer = rev_r[..., None] * bbr[None] - rev_i[..., None] * bbi[None]
    wei = rev_r[..., None] * bbi[None] + rev_i[..., None] * bbr[None]

    def rows_in(w):
        w = jnp.transpose(w.reshape(L, N_GB, GROUP_BLOCK, SSM_STATE, SSM_GROUP), (1, 0, 2, 4, 3))
        return w.reshape(N_GB, L * GB_LANES, SSM_STATE)

    wend = jnp.concatenate([rows_in(wer), rows_in(wei)], axis=2)
    m1r = c_re[None] * pwr[1:, :, None, :] - c_im[None] * pwi[1:, :, None, :]
    m1i = c_re[None] * pwi[1:, :, None, :] + c_im[None] * pwr[1:, :, None, :]

    def rows_out(m):
        m = jnp.transpose(m.reshape(L, N_GB, GROUP_BLOCK, SSM_GROUP, SSM_STATE), (1, 2, 4, 0, 3))
        return m.reshape(N_GB, GB_STATE, L * SSM_GROUP)

    wout = jnp.concatenate([rows_out(m1r), rows_out(-m1i)], axis=1)
    acat = jnp.concatenate([pwr[L].reshape(N_GB, 1, GB_STATE), pwi[L].reshape(N_GB, 1, GB_STATE)], axis=2)
    return taps, wend, wout, acat


def _lane_group(shape, axis, shift):
    return (lax.broadcasted_iota(jnp.int32, shape, axis) >> shift) & (GROUP_BLOCK - 1)


def _ssm_expand(taps, wend, wout):
    L = SSM_CHUNK
    taps = jnp.pad(taps, ((0, 0), (0, 0), (0, 0), (0, GB_LANES - SSM_GROUP)))

    def body(t_ref, we_ref, wo_ref, d_ref, web_ref, wob_ref):
        def rc(shape):
            return lax.broadcasted_iota(jnp.int32, shape, 0), lax.broadcasted_iota(jnp.int32, shape, 1)

        r, c = rc((GB_LANES, GB_LANES))
        spread = ((r < SSM_GROUP) & (r == (c & (SSM_GROUP - 1)))).astype(MXU_DTYPE)
        same = _lane_group((GB_LANES, GB_LANES), 0, 4) == _lane_group((GB_LANES, GB_LANES), 1, 4)
        for tau in range(L):
            full = jnp.dot(t_ref[tau].astype(MXU_DTYPE), spread, preferred_element_type=F32)
            d_ref[tau] = jnp.where(same, full, 0.0).astype(d_ref.dtype)
        r, c = rc((2 * SSM_STATE, STATE_W))
        part = (r >> 6) == (c >> 9)
        spread = (part & ((r & (SSM_STATE - 1)) == (c & (SSM_STATE - 1)))).astype(MXU_DTYPE)
        keep = _lane_group((GB_LANES, STATE_W), 0, 4) == _lane_group((GB_LANES, STATE_W), 1, 6)
        for j in range(L):
            rows = slice(j * GB_LANES, (j + 1) * GB_LANES)
            full = jnp.dot(we_ref[rows, :].astype(MXU_DTYPE), spread, preferred_element_type=F32)
            web_ref[rows, :] = jnp.where(keep, full, 0.0).astype(web_ref.dtype)
        r, c = rc((GB_LANES, GB_LANES))
        own = _lane_group((STATE_W, GB_LANES), 0, 6) == _lane_group((STATE_W, GB_LANES), 1, 4)
        for tt in range(L):
            half, k = tt // GROUP_BLOCK, tt % GROUP_BLOCK
            spread = (((r >> 4) == k) & ((r & (SSM_GROUP - 1)) == (c & (SSM_GROUP - 1)))).astype(MXU_DTYPE)
            src = wo_ref[:, half * GB_LANES:(half + 1) * GB_LANES].astype(MXU_DTYPE)
            full = jnp.dot(src, spread, preferred_element_type=F32)
            wob_ref[:, tt * GB_LANES:(tt + 1) * GB_LANES] = jnp.where(own, full, 0.0).astype(wob_ref.dtype)

    def spec(shape):
        return pl.BlockSpec((None,) + shape[1:], lambda b: (b,) + (0,) * (len(shape) - 1))

    out_shapes = [(N_GB, L, GB_LANES, GB_LANES), (N_GB, L * GB_LANES, STATE_W), (N_GB, STATE_W, L * GB_LANES)]
    return tuple(pl.pallas_call(
        body, name="ssm_expand", grid=(N_GB,),
        in_specs=[spec(taps.shape), spec(wend.shape), spec(wout.shape)],
        out_specs=[spec(s) for s in out_shapes],
        out_shape=[jax.ShapeDtypeStruct(s, MXU_DTYPE) for s in out_shapes],
        compiler_params=pltpu.CompilerParams(dimension_semantics=("parallel",), vmem_limit_bytes=VMEM_CAP),
    )(taps, wend, wout))


def _dot(a, b):
    return jnp.dot(a.astype(MXU_DTYPE), b.astype(MXU_DTYPE), preferred_element_type=F32)


def _dot_nt(a, b):
    return lax.dot_general(a.astype(MXU_DTYPE), b.astype(MXU_DTYPE), (((1,), (1,)), ((), ())),
                           preferred_element_type=F32)


def _dot_tn(a, b):
    return lax.dot_general(a.astype(MXU_DTYPE), b.astype(MXU_DTYPE), (((0,), (0,)), ((), ())),
                           preferred_element_type=F32)


STATE_W = 2 * GB_STATE


def _chunk_scan(e, acat):
    nc = e.shape[0]
    spec = pl.BlockSpec((nc, STATE_W), lambda b: (0, b))
    aspec = pl.BlockSpec((None, 1, STATE_W), lambda b: (b, 0, 0))

    def body(e_ref, a_ref, s_ref):
        a_r, a_i = a_ref[:, :GB_STATE], a_ref[:, GB_STATE:]

        def step(c, carry):
            s_r, s_i = carry
            s_ref[pl.ds(c, 1), :GB_STATE] = s_r
            s_ref[pl.ds(c, 1), GB_STATE:] = s_i
            e_r = e_ref[pl.ds(c, 1), :GB_STATE]
            e_i = e_ref[pl.ds(c, 1), GB_STATE:]
            return (a_r * s_r - a_i * s_i + e_r, a_r * s_i + a_i * s_r + e_i)

        zero = jnp.zeros((1, GB_STATE), F32)
        lax.fori_loop(0, nc, step, (zero, zero))

    return pl.pallas_call(
        body, name="ssm_chunk_scan", grid=(N_GB,),
        in_specs=[spec, aspec], out_specs=spec,
        out_shape=jax.ShapeDtypeStruct(e.shape, F32),
        compiler_params=pltpu.CompilerParams(dimension_semantics=("parallel",)),
    )(e, acat)


def _chunk_scan_bwd(ds, s, acat):
    nc = ds.shape[0]
    spec = pl.BlockSpec((nc, STATE_W), lambda b: (0, b))
    aspec = pl.BlockSpec((None, 1, STATE_W), lambda b: (b, 0, 0))

    def body(ds_ref, s_ref, a_ref, ge_ref, da_ref):
        a_r, a_i = a_ref[:, :GB_STATE], a_ref[:, GB_STATE:]

        def step(t, carry):
            g_r, g_i, d_r, d_i = carry
            c = nc - 1 - t
            ge_ref[pl.ds(c, 1), :GB_STATE] = g_r
            ge_ref[pl.ds(c, 1), GB_STATE:] = g_i
            s_r = s_ref[pl.ds(c, 1), :GB_STATE]
            s_i = s_ref[pl.ds(c, 1), GB_STATE:]
            d_r = d_r + g_r * s_r + g_i * s_i
            d_i = d_i + g_i * s_r - g_r * s_i
            n_r = ds_ref[pl.ds(c, 1), :GB_STATE] + a_r * g_r + a_i * g_i
            n_i = ds_ref[pl.ds(c, 1), GB_STATE:] + a_r * g_i - a_i * g_r
            return (n_r, n_i, d_r, d_i)

        zero = jnp.zeros((1, GB_STATE), F32)
        _, _, d_r, d_i = lax.fori_loop(0, nc, step, (zero, zero, zero, zero))
        da_ref[:, :GB_STATE] = d_r
        da_ref[:, GB_STATE:] = d_i

    return pl.pallas_call(
        body, name="ssm_chunk_scan_bwd", grid=(N_GB,),
        in_specs=[spec, spec, aspec], out_specs=[spec, aspec],
        out_shape=[jax.ShapeDtypeStruct(ds.shape, F32), jax.ShapeDtypeStruct(acat.shape, F32)],
        compiler_params=pltpu.CompilerParams(dimension_semantics=("parallel",)),
    )(ds, s, acat)


def _step_rows(ref, j, nc):
    return ref[pl.ds(j, nc, stride=SSM_CHUNK), :]


def _fold_lanes(z, widths):
    for w in widths:
        z = z + pltpu.roll(z, w, 1)
    return z


def _ssm_forward(proj, mats):
    dblk, wend, wout, acat = mats
    t = proj.shape[0]
    nc = t // SSM_CHUNK
    L = SSM_CHUNK
    lanes = pl.BlockSpec((t, GB_LANES), lambda b: (0, b))
    state = pl.BlockSpec((nc, STATE_W), lambda b: (0, b))

    def body_end(u_ref, w_ref, e_ref):
        x = jnp.concatenate([_step_rows(u_ref, j, nc).astype(MXU_DTYPE) for j in range(L)], axis=1)
        e_ref[...] = jnp.dot(x, w_ref[...], preferred_element_type=F32)

    e = pl.pallas_call(
        body_end, name="ssm_chunk_end", grid=(N_GB,),
        in_specs=[lanes, pl.BlockSpec((None, L * GB_LANES, STATE_W), lambda b: (b, 0, 0))],
        out_specs=state, out_shape=jax.ShapeDtypeStruct((nc, N_GB * STATE_W), F32),
        compiler_params=pltpu.CompilerParams(dimension_semantics=("parallel",), vmem_limit_bytes=VMEM_CAP),
    )(proj, wend)
    s = _chunk_scan(e, acat)

    def body_out(u_ref, d_ref, s_ref, w_ref, y_ref):
        xs = [_step_rows(u_ref, j, nc).astype(MXU_DTYPE) for j in range(L)]
        sb = s_ref[...].astype(MXU_DTYPE)
        for tt in range(L):
            xcat = jnp.concatenate(xs[:tt + 1], axis=1)
            taps = jnp.concatenate([d_ref[tt - j] for j in range(tt + 1)], axis=0).astype(MXU_DTYPE)
            y = (jnp.dot(xcat, taps, preferred_element_type=F32)
                 + jnp.dot(sb, w_ref[:, tt * GB_LANES:(tt + 1) * GB_LANES], preferred_element_type=F32))
            y_ref[pl.ds(tt, nc, stride=L), :] = y

    y = pl.pallas_call(
        body_out, name="ssm_chunk_out", grid=(N_GB,),
        in_specs=[lanes, pl.BlockSpec((None, L, GB_LANES, GB_LANES), lambda b: (b, 0, 0, 0)), state,
                  pl.BlockSpec((None, STATE_W, L * GB_LANES), lambda b: (b, 0, 0))],
        out_specs=lanes, out_shape=jax.ShapeDtypeStruct((t, D_SSM), F32),
        compiler_params=pltpu.CompilerParams(dimension_semantics=("parallel",), vmem_limit_bytes=VMEM_CAP),
    )(proj, dblk, s, wout)
    return y, s


def _ssm_backward(dy, proj, s, mats):
    dblk, wend, wout, acat = mats
    t = proj.shape[0]
    nc = t // SSM_CHUNK
    L = SSM_CHUNK
    lanes = pl.BlockSpec((t, GB_LANES), lambda b: (0, b))
    state = pl.BlockSpec((nc, STATE_W), lambda b: (0, b))
    taps_spec = pl.BlockSpec((None, L, GB_LANES, GB_LANES), lambda b: (b, 0, 0, 0))

    def body_state(dy_ref, w_ref, ds_ref):
        dyc = jnp.concatenate([_step_rows(dy_ref, tt, nc).astype(MXU_DTYPE) for tt in range(L)], axis=1)
        ds_ref[...] = _dot_nt(dyc, w_ref[...])

    ds = pl.pallas_call(
        body_state, name="ssm_bwd_state", grid=(N_GB,),
        in_specs=[lanes, pl.BlockSpec((None, STATE_W, L * GB_LANES), lambda b: (b, 0, 0))],
        out_specs=state, out_shape=jax.ShapeDtypeStruct((nc, N_GB * STATE_W), F32),
        compiler_params=pltpu.CompilerParams(dimension_semantics=("parallel",), vmem_limit_bytes=VMEM_CAP),
    )(dy, wout)
    ge, dacat = _chunk_scan_bwd(ds, s, acat)

    def body_in(u_ref, dy_ref, d_ref, ge_ref, w_ref, du_ref, dd_ref, gd_ref):
        gd_ref[...] = jnp.sum(u_ref[...] * dy_ref[...], axis=0, keepdims=True)
        xs = [_step_rows(u_ref, j, nc).astype(MXU_DTYPE) for j in range(L)]
        dys = [_step_rows(dy_ref, tt, nc).astype(MXU_DTYPE) for tt in range(L)]
        ge = ge_ref[...].astype(MXU_DTYPE)
        for i in range(L):
            dyc = jnp.concatenate(dys[i:], axis=1)
            taps = jnp.concatenate([d_ref[tt - i] for tt in range(i, L)], axis=1).astype(MXU_DTYPE)
            du_ref[pl.ds(i, nc, stride=L), :] = (
                _dot_nt(dyc, taps) + _dot_nt(ge, w_ref[i * GB_LANES:(i + 1) * GB_LANES, :]))
        for j in range(L):
            m = _dot_tn(xs[j], jnp.concatenate(dys[j:], axis=1))
            for tau in range(L - j):
                part = m[:, tau * GB_LANES:(tau + 1) * GB_LANES]
                if j == 0:
                    dd_ref[tau] = part
                else:
                    dd_ref[tau] += part
        same = _lane_group((GB_LANES, GB_LANES), 0, 4) == _lane_group((GB_LANES, GB_LANES), 1, 4)
        for tau in range(L):
            dd_ref[tau] = _fold_lanes(jnp.where(same, dd_ref[tau], 0.0), (64, 32, 16))

    du, ddblk, gd = pl.pallas_call(
        body_in, name="ssm_bwd_in", grid=(N_GB,),
        in_specs=[lanes, lanes, taps_spec, state,
                  pl.BlockSpec((None, L * GB_LANES, STATE_W), lambda b: (b, 0, 0))],
        out_specs=[lanes, taps_spec, pl.BlockSpec((1, GB_LANES), lambda b: (0, b))],
        out_shape=[jax.ShapeDtypeStruct((t, D_SSM), F32), jax.ShapeDtypeStruct(dblk.shape, F32),
                   jax.ShapeDtypeStruct((1, D_SSM), F32)],
        compiler_params=pltpu.CompilerParams(dimension_semantics=("parallel",), vmem_limit_bytes=VMEM_CAP),
    )(proj, dy, dblk, ge, wend)

    def body_w(u_ref, dy_ref, ge_ref, s_ref, dwe_ref, dwo_ref):
        ge = ge_ref[...].astype(MXU_DTYPE)
        sb = s_ref[...].astype(MXU_DTYPE)
        keep = _lane_group((GB_LANES, STATE_W), 0, 4) == _lane_group((GB_LANES, STATE_W), 1, 6)
        low = lax.broadcasted_iota(jnp.int32, (GB_LANES, 2 * SSM_STATE), 1) < SSM_STATE

        def fold_state(z):
            z = z[:, :GB_STATE // 2] + z[:, GB_STATE // 2:]
            z = z[:, :GB_STATE // 4] + z[:, GB_STATE // 4:]
            return _fold_lanes(z, (SSM_STATE,))

        for j in range(L):
            z = jnp.where(keep, _dot_tn(_step_rows(u_ref, j, nc).astype(MXU_DTYPE), ge), 0.0)
            dwe_ref[j * GB_LANES:(j + 1) * GB_LANES, :] = jnp.where(
                low, fold_state(z[:, :GB_STATE]), fold_state(z[:, GB_STATE:]))
        own = _lane_group((STATE_W, GB_LANES), 0, 6) == _lane_group((STATE_W, GB_LANES), 1, 4)
        chunk = lax.broadcasted_iota(jnp.int32, (STATE_W, GB_LANES), 1) >> 4
        for half in range(L // GROUP_BLOCK):
            acc = jnp.zeros((STATE_W, GB_LANES), F32)
            for k in range(GROUP_BLOCK):
                tt = half * GROUP_BLOCK + k
                z = jnp.where(own, _dot_tn(sb, _step_rows(dy_ref, tt, nc).astype(MXU_DTYPE)), 0.0)
                acc = acc + jnp.where(chunk == k, _fold_lanes(z, (64, 32, 16)), 0.0)
            dwo_ref[:, half * GB_LANES:(half + 1) * GB_LANES] = acc

    dwend, dwout = pl.pallas_call(
        body_w, name="ssm_bwd_w", grid=(N_GB,),
        in_specs=[lanes, lanes, state, state],
        out_specs=[pl.BlockSpec((None, L * GB_LANES, 2 * SSM_STATE), lambda b: (b, 0, 0)),
                   pl.BlockSpec((None, STATE_W, L * SSM_GROUP), lambda b: (b, 0, 0))],
        out_shape=[jax.ShapeDtypeStruct((N_GB, L * GB_LANES, 2 * SSM_STATE), F32),
                   jax.ShapeDtypeStruct((N_GB, STATE_W, L * SSM_GROUP), F32)],
        compiler_params=pltpu.CompilerParams(dimension_semantics=("parallel",), vmem_limit_bytes=VMEM_CAP),
    )(proj, dy, ge, s)
    return du, gd, (ddblk[:, :, :, :SSM_GROUP], dwend, dwout, dacat)


def _t5_bucket(dist):
    max_exact = N_BUCKETS // 2
    is_small = dist < max_exact
    d = jnp.maximum(dist, 1).astype(F32)
    large = max_exact + (jnp.log(d / max_exact) / math.log(MAX_DISTANCE / max_exact)
                         * (N_BUCKETS - max_exact)).astype(jnp.int32)
    large = jnp.minimum(large, N_BUCKETS - 1)
    return jnp.where(is_small, dist, large)


def _band_bias(rel_bias_table):
    i = jnp.arange(BLOCK)[:, None]
    j = jnp.arange(BLOCK)[None, :]
    bucket = _t5_bucket(jnp.where(j > i, BLOCK + i - j, i - j))
    onehot = (bucket[:, :, None] == jnp.arange(N_BUCKETS)[None, None, :]).astype(F32)
    return jnp.einsum("qsb,bh->hqs", onehot, rel_bias_table, precision=lax.Precision.HIGHEST)


assert WINDOW == BLOCK
KV_PAIR = 2
HEADS_PER_STEP = KV_PAIR * Q_PER_KV
Q_LANES = HEADS_PER_STEP * HEAD_DIM
SLAB = 2 * HEAD_DIM
Q_COL0 = OFF_Q * COL // Q_LANES
K_COL0 = OFF_K * COL // SLAB
V_COL0 = OFF_V * COL // SLAB
ZA_COL0 = OFF_ZA * COL // Q_LANES


def _attn_specs():
    q_spec = pl.BlockSpec((BLOCK, Q_LANES), lambda m, n: (n, Q_COL0 + m))
    k_prev = pl.BlockSpec((BLOCK, SLAB), lambda m, n: (jnp.maximum(n - 1, 0), K_COL0 + m))
    k_cur = pl.BlockSpec((BLOCK, SLAB), lambda m, n: (n, K_COL0 + m))
    v_prev = pl.BlockSpec((BLOCK, SLAB), lambda m, n: (jnp.maximum(n - 1, 0), V_COL0 + m))
    v_cur = pl.BlockSpec((BLOCK, SLAB), lambda m, n: (n, V_COL0 + m))
    bias_spec = pl.BlockSpec((HEADS_PER_STEP, BLOCK, BLOCK), lambda m, n: (m, 0, 0))
    sink_spec = pl.BlockSpec(memory_space=pltpu.SMEM)
    wide = pl.BlockSpec((BLOCK, Q_LANES), lambda m, n: (n, m))
    gate = pl.BlockSpec((BLOCK, Q_LANES), lambda m, n: (n, ZA_COL0 + m))
    pair = pl.BlockSpec((BLOCK, SLAB), lambda m, n: (n, m))
    return [sink_spec, q_spec, k_prev, k_cur, v_prev, v_cur, bias_spec, gate], wide, pair


def _low_lanes(shape):
    return lax.broadcasted_iota(jnp.int32, shape, 1) < HEAD_DIM


def _pair_halves(ref):
    kb = ref[...]
    sw = pltpu.roll(kb, HEAD_DIM, 1)
    lo = _low_lanes(kb.shape)
    zero = jnp.zeros_like(kb)
    first = (jnp.where(lo, kb, zero).astype(MXU_DTYPE), jnp.where(lo, zero, sw).astype(MXU_DTYPE))
    second = (jnp.where(lo, sw, zero).astype(MXU_DTYPE), jnp.where(lo, zero, kb).astype(MXU_DTYPE))
    return first, second


def _fold_pair(acc):
    f = [x + pltpu.roll(x, HEAD_DIM, 1) for x in acc]
    return jnp.where(_low_lanes(f[0].shape), f[0], f[1])


def _from_prev(n):
    row = lax.broadcasted_iota(jnp.int32, (BLOCK, BLOCK), 0)
    col = lax.broadcasted_iota(jnp.int32, (BLOCK, BLOCK), 1)
    prev = col > row
    return prev, jnp.where(jnp.logical_and(n == 0, prev), NEG_INF, 0.0)


def _softmax_sink(s, sink):
    m = jnp.maximum(jnp.max(s, axis=1, keepdims=True), sink)
    e = jnp.exp(s - m)
    es = jnp.exp(sink - m)
    inv = 1.0 / (jnp.sum(e, axis=1, keepdims=True) + es)
    return e * inv, es * inv


def _stack_pair(prev_halves, own_halves, a):
    return jnp.concatenate([prev_halves[a][0], prev_halves[a][1], own_halves[a][0], own_halves[a][1]], axis=0)


def _split_heads(x4, prev):
    return [jnp.where(prev, x4[:, e * BLOCK:(e + 1) * BLOCK], x4[:, (2 + e) * BLOCK:(3 + e) * BLOCK]) for e in range(2)]


def _spread_heads(x, prev):
    return jnp.concatenate([jnp.where(prev, x[0], 0.0), jnp.where(prev, x[1], 0.0),
                            jnp.where(prev, 0.0, x[0]), jnp.where(prev, 0.0, x[1])], axis=1)


def _attn_forward(proj, bias, sinks):
    t = proj.shape[0]
    in_specs, wide, _ = _attn_specs()

    def body(sink_ref, q_ref, kp_ref, kc_ref, vp_ref, vc_ref, bias_ref, z_ref, o_ref, h_ref):
        m, n = pl.program_id(0), pl.program_id(1)
        kp, kc, vp, vc = (_pair_halves(r) for r in (kp_ref, kc_ref, vp_ref, vc_ref))
        keys = [_stack_pair(kp, kc, a) for a in range(KV_PAIR)]
        vals = [_stack_pair(vp, vc, a) for a in range(KV_PAIR)]
        prev, edge = _from_prev(n)
        prev2 = jnp.concatenate([prev, prev], axis=0)
        for a in range(KV_PAIR):
            slabs = (2 * a, 2 * a + 1)
            q = jnp.concatenate([q_ref[:, s * SLAB:(s + 1) * SLAB] for s in slabs], axis=0).astype(MXU_DTYPE)
            logits = _split_heads(_dot_nt(q, keys[a]), prev2)
            probs = []
            for e in range(2):
                rows = [_softmax_sink(logits[e][r * BLOCK:(r + 1) * BLOCK] * ATTN_SCALE + bias_ref[2 * s + e] + edge,
                                      sink_ref[m * HEADS_PER_STEP + 2 * s + e])[0] for r, s in enumerate(slabs)]
                probs.append(jnp.concatenate(rows, axis=0))
            out = _dot(_spread_heads(probs, prev2), vals[a])
            for r, s in enumerate(slabs):
                cols = slice(s * SLAB, (s + 1) * SLAB)
                o_ref[:, cols] = out[r * BLOCK:(r + 1) * BLOCK]
                h_ref[:, cols] = (out[r * BLOCK:(r + 1) * BLOCK] * _silu(z_ref[:, cols])).astype(h_ref.dtype)

    return pl.pallas_call(
        body, name="attn_fwd", grid=(N_KV_HEADS // KV_PAIR, t // BLOCK),
        in_specs=in_specs, out_specs=[wide, wide],
        out_shape=[jax.ShapeDtypeStruct((t, D_ATTN), F32), jax.ShapeDtypeStruct((t, D_ATTN), MXU_DTYPE)],
        compiler_params=pltpu.CompilerParams(dimension_semantics=("parallel", "arbitrary")),
    )(sinks, proj, proj, proj, proj, proj, bias, proj)


def _attn_backward(proj, bias, sinks, attn, dh):
    t = proj.shape[0]
    in_specs, wide, pair = _attn_specs()
    bias_spec = in_specs[-2]
    sink_out = pl.BlockSpec((HEADS_PER_STEP, 8, 128), lambda m, n: (m, 0, 0))

    def body(sink_ref, q_ref, kp_ref, kc_ref, vp_ref, vc_ref, bias_ref, z_ref, o_ref, dh_ref,
             dq_ref, dz_ref, dka_ref, dkb_ref, dva_ref, dvb_ref, dbias_ref, dsink_ref):
        m, n = pl.program_id(0), pl.program_id(1)

        @pl.when(n == 0)
        def _():
            dbias_ref[...] = jnp.zeros_like(dbias_ref)
            dsink_ref[...] = jnp.zeros_like(dsink_ref)

        kp, kc, vp, vc = (_pair_halves(r) for r in (kp_ref, kc_ref, vp_ref, vc_ref))
        keys = [_stack_pair(kp, kc, a) for a in range(KV_PAIR)]
        vals = [_stack_pair(vp, vc, a) for a in range(KV_PAIR)]
        prev, edge = _from_prev(n)
        lo = _low_lanes((BLOCK, SLAB))
        prev2 = jnp.concatenate([prev, prev], axis=0)
        dk = [[None] * KV_PAIR for _ in range(2)]
        dv = [[None] * KV_PAIR for _ in range(2)]
        for a in range(KV_PAIR):
            slabs = (2 * a, 2 * a + 1)
            q = jnp.concatenate([q_ref[:, s * SLAB:(s + 1) * SLAB] for s in slabs], axis=0).astype(MXU_DTYPE)
            gated = []
            for s in slabs:
                cols = slice(s * SLAB, (s + 1) * SLAB)
                dh_s, z_s = dh_ref[:, cols], z_ref[:, cols]
                dz_ref[:, cols] = (dh_s * o_ref[:, cols] * _silu_grad(z_s)).astype(dz_ref.dtype)
                gated.append(dh_s * _silu(z_s))
            do = jnp.concatenate(gated, axis=0).astype(MXU_DTYPE)
            logits = _split_heads(_dot_nt(q, keys[a]), prev2)
            dprobs = _split_heads(_dot_nt(do, vals[a]), prev2)
            probs, dlogits = [], []
            for e in range(2):
                p_rows, ds_rows = [], []
                for r, s in enumerate(slabs):
                    h = 2 * s + e
                    rows = slice(r * BLOCK, (r + 1) * BLOCK)
                    p, ps = _softmax_sink(logits[e][rows] * ATTN_SCALE + bias_ref[h] + edge,
                                          sink_ref[m * HEADS_PER_STEP + h])
                    delta = jnp.sum(p * dprobs[e][rows], axis=1, keepdims=True)
                    ds = p * (dprobs[e][rows] - delta)
                    dbias_ref[h] += ds
                    dsink_ref[h] += jnp.broadcast_to(jnp.sum(-ps * delta, axis=0, keepdims=True), (8, 128))
                    p_rows.append(p)
                    ds_rows.append(ds)
                probs.append(jnp.concatenate(p_rows, axis=0))
                dlogits.append(jnp.concatenate(ds_rows, axis=0))
            ds4 = _spread_heads(dlogits, prev2).astype(MXU_DTYPE)
            p4 = _spread_heads(probs, prev2).astype(MXU_DTYPE)
            dq = _dot(ds4, keys[a]) * ATTN_SCALE
            for r, s in enumerate(slabs):
                dq_ref[:, s * SLAB:(s + 1) * SLAB] = dq[r * BLOCK:(r + 1) * BLOCK].astype(dq_ref.dtype)
            rk = _dot_tn(ds4, q)
            rv = _dot_tn(p4, do)
            for which in range(2):
                top = 2 * which * BLOCK
                dk[which][a] = jnp.where(lo, rk[top:top + BLOCK], rk[top + BLOCK:top + 2 * BLOCK])
                dv[which][a] = jnp.where(lo, rv[top:top + BLOCK], rv[top + BLOCK:top + 2 * BLOCK])
        dkb_ref[...] = _fold_pair(dk[0]) * ATTN_SCALE
        dka_ref[...] = _fold_pair(dk[1]) * ATTN_SCALE
        dvb_ref[...] = _fold_pair(dv[0])
        dva_ref[...] = _fold_pair(dv[1])

    kv_shape = jax.ShapeDtypeStruct((t, D_KV), F32)
    return pl.pallas_call(
        body, name="attn_bwd", grid=(N_KV_HEADS // KV_PAIR, t // BLOCK),
        in_specs=in_specs + [wide, wide],
        out_specs=[wide, wide, pair, pair, pair, pair, bias_spec, sink_out],
        out_shape=[jax.ShapeDtypeStruct((t, D_ATTN), MXU_DTYPE), jax.ShapeDtypeStruct((t, D_ATTN), MXU_DTYPE),
                   kv_shape, kv_shape, kv_shape, kv_shape,
                   jax.ShapeDtypeStruct(bias.shape, F32), jax.ShapeDtypeStruct((N_Q_HEADS, 8, 128), F32)],
        compiler_params=pltpu.CompilerParams(dimension_semantics=("parallel", "arbitrary")),
    )(sinks, proj, proj, proj, proj, proj, bias, proj, attn, dh)


def _shift_blocks(cur, prev):
    return cur + jnp.concatenate([prev[BLOCK:], jnp.zeros_like(prev[:BLOCK])], axis=0)


def _mesh_pos():
    return lax.axis_index("x"), lax.axis_index("y"), lax.axis_index("c")


_HBM = pl.BlockSpec(memory_space=pltpu.HBM)
_SEM = pl.BlockSpec(memory_space=pltpu.SEMAPHORE)
_DATAFLOW = pltpu.SideEffectType.DATAFLOW_SIDE_EFFECTING


def _peers():
    x, y, c = _mesh_pos()
    others = []
    for k in range(1, N_DEV):
        px, py, pc = x ^ (k >> 2), y ^ ((k >> 1) & 1), c ^ (k & 1)
        others.append(((px, py, pc), 4 * px + 2 * py + pc))
    return 4 * x + 2 * y + c, others


def _split_start(bufs, plan, n_copies, *, name):
    nb = len(bufs)

    def body(*refs):
        send_sems, recv_sems, token = refs[nb], refs[nb + 1], refs[-1]
        for k, (src, dst, pos, _) in enumerate(plan(*refs[:nb])):
            pltpu.make_async_remote_copy(src_ref=src, dst_ref=dst, send_sem=send_sems.at[k], recv_sem=recv_sems.at[k],
                                         device_id=pos, device_id_type=pl.DeviceIdType.MESH).start()
        token[...] = jnp.zeros_like(token)

    return pl.pallas_call(
        body, name=name,
        out_shape=(pltpu.SemaphoreType.DMA((n_copies,)), pltpu.SemaphoreType.DMA((n_copies,)),
                   *[pltpu.HBM(b.shape, b.dtype) for b in bufs], jax.ShapeDtypeStruct((8, 128), F32)),
        in_specs=(_HBM,) * nb, out_specs=(_SEM, _SEM) + (_HBM,) * nb + (pl.BlockSpec(memory_space=pltpu.VMEM),),
        input_output_aliases={i: 2 + i for i in range(nb)},
        compiler_params=pltpu.CompilerParams(has_side_effects=_DATAFLOW),
    )(*[pltpu.with_memory_space_constraint(b, pltpu.HBM) for b in bufs])


def _split_wait(started, plan, after, *, name):
    send_sems, recv_sems, *thru = started[:-1]
    nb = len(thru)

    def body(*refs):
        send_sems, recv_sems = refs[nb], refs[nb + 1]
        for k, (src, _, pos, arrive) in enumerate(plan(*refs[:nb])):
            copy = pltpu.make_async_remote_copy(
                src_ref=src, dst_ref=arrive, send_sem=send_sems.at[k], recv_sem=recv_sems.at[k],
                device_id=pos, device_id_type=pl.DeviceIdType.MESH)
            copy.wait_send()
            copy.wait_recv()

    return pl.pallas_call(
        body, name=name,
        out_shape=tuple(pltpu.HBM(b.shape, b.dtype) for b in thru),
        in_specs=(_HBM,) * nb + (_SEM, _SEM, pl.BlockSpec(memory_space=pl.ANY)), out_specs=(_HBM,) * nb,
        input_output_aliases={i: i for i in range(nb)},
        compiler_params=pltpu.CompilerParams(has_side_effects=_DATAFLOW),
    )(*thru, send_sems, recv_sems, after)


def _plan_scatter(x_ref, land_ref):
    me, others = _peers()
    return [(x_ref.at[idx], land_ref.at[me], pos, land_ref.at[idx]) for pos, idx in others]


def _plan_gather(x_ref, land_ref):
    me, others = _peers()
    return [(x_ref, land_ref.at[me], pos, land_ref.at[idx]) for pos, idx in others]


def _near_and_far():
    x, y, c = _mesh_pos()
    chips = [(1 - x, y), (x, 1 - y), (1 - x, 1 - y)]
    near = [(x, y, 1 - c)] + [(px, py, c) for px, py in chips]
    relay = [(4 * px + 2 * py + c, 4 * px + 2 * py + 1 - c) for px, py in chips]
    return 4 * x + 2 * y + c, near, (x, y, 1 - c), relay


def _plan_gather_near(x_ref, land_ref):
    me, near, _, _ = _near_and_far()
    return [(x_ref, land_ref.at[me], pos, land_ref.at[4 * pos[0] + 2 * pos[1] + pos[2]]) for pos in near]


def _plan_gather_relay(land_ref):
    _, _, sibling, relay = _near_and_far()
    return [(land_ref.at[mine], land_ref.at[mine], sibling, land_ref.at[theirs]) for mine, theirs in relay]


def _landing_zone(own):
    me, _ = _peers()
    return lax.dynamic_update_index_in_dim(lax.empty((N_DEV,) + own.shape, own.dtype), own, me, 0)


def _scatter_start(x, *, name):
    if x.ndim == 2:
        return _split_start((x, _landing_zone(x)), _plan_gather, N_DEV - 1, name=name)
    me, _ = _peers()
    own = lax.dynamic_index_in_dim(x, me, 0, keepdims=False)
    return _split_start((x, _landing_zone(own)), _plan_scatter, N_DEV - 1, name=name)


def _scatter_wait(started, after, *, name):
    plan = _plan_gather if started[2].ndim == 2 else _plan_scatter
    return _split_wait(started, plan, after, name=name)[1]


def _adamw_math(w, g, m, v):
    m = ADAM_B1 * m + (1.0 - ADAM_B1) * g
    v = ADAM_B2 * v + (1.0 - ADAM_B2) * (g * g)
    m_hat = m / (1.0 - ADAM_B1 ** ADAM_STEP)
    v_hat = v / (1.0 - ADAM_B2 ** ADAM_STEP)
    delta = -ADAM_LR * (m_hat / (jnp.sqrt(v_hat) + ADAM_EPS) + ADAM_WD * w)
    return delta, m, v


def _adamw_reduce(parts, w, m, v, *, name, tr):
    r, c = w.shape
    tr = min(tr, r)
    spec = pl.BlockSpec((tr, c), lambda i: (i, 0))

    def body(p_ref, w_ref, m_ref, v_ref, g_ref, d_ref, nm_ref, nv_ref):
        g = p_ref[0].astype(F32)
        for s in range(1, N_DEV):
            g = g + p_ref[s].astype(F32)
        delta, nm, nv = _adamw_math(w_ref[...], g, m_ref[...], v_ref[...])
        g_ref[...] = g
        d_ref[...] = delta
        nm_ref[...] = nm
        nv_ref[...] = nv

    return pl.pallas_call(
        body, name=name, grid=(r // tr,),
        in_specs=[pl.BlockSpec((N_DEV, tr, c), lambda i: (0, i, 0)), spec, spec, spec],
        out_specs=[spec] * 4,
        out_shape=[jax.ShapeDtypeStruct((r, c), F32)] * 4,
        compiler_params=pltpu.CompilerParams(
            dimension_semantics=("parallel",),
            vmem_limit_bytes=_vmem_limit(2 * 15 * _nbytes((tr, c), F32))),
    )(parts, w, m, v)


def _sum_parts(parts, *, name):
    def body(p_ref, o_ref):
        g = p_ref[0]
        for s in range(1, N_DEV):
            g = g + p_ref[s]
        o_ref[...] = g

    return pl.pallas_call(
        body, name=name, out_shape=jax.ShapeDtypeStruct(parts.shape[1:], F32),
        compiler_params=pltpu.CompilerParams(vmem_limit_bytes=_vmem_limit(_nbytes(parts.shape, F32))),
    )(parts)


def _adamw_native(w, g, m, v, *, name):
    def body(w_ref, g_ref, m_ref, v_ref, d_ref, nm_ref, nv_ref):
        d_ref[...], nm_ref[...], nv_ref[...] = _adamw_math(w_ref[...], g_ref[...], m_ref[...], v_ref[...])

    return pl.pallas_call(body, name=name, out_shape=[jax.ShapeDtypeStruct(w.shape, F32)] * 3)(w, g, m, v)


SMALL = ["ssm_lambda_re", "ssm_lambda_im", "ssm_b_re", "ssm_b_im", "ssm_c_re", "ssm_c_im", "ssm_d",
         "ssm_log_step", "attn_sinks", "rel_bias_table", "ln_gain", "ln_bias"]


def _pack(arrs):
    flat = jnp.concatenate([a.reshape(-1) for a in arrs])
    pad = (-flat.shape[0]) % 1024
    return jnp.pad(flat, (0, pad)).reshape(-1, 128)


def _unpack(packed, like):
    flat = packed.reshape(-1)
    out, pos = [], 0
    for a in like:
        out.append(flat[pos:pos + a.size].reshape(a.shape))
        pos += a.size
    return out


def kernel(x, w_in, ssm_lambda_re, ssm_lambda_im, ssm_b_re, ssm_b_im, ssm_c_re, ssm_c_im, ssm_d, ssm_log_step, w_glu, attn_sinks, rel_bias_table, w_branch_ssm, w_branch_attn, w_out, ln_gain, ln_bias, loss_target, m_w_in, m_ssm_lambda_re, m_ssm_lambda_im, m_ssm_b_re, m_ssm_b_im, m_ssm_c_re, m_ssm_c_im, m_ssm_d, m_ssm_log_step, m_w_glu, m_attn_sinks, m_rel_bias_table, m_w_branch_ssm, m_w_branch_attn, m_w_out, m_ln_gain, m_ln_bias, v_w_in, v_ssm_lambda_re, v_ssm_lambda_im, v_ssm_b_re, v_ssm_b_im, v_ssm_c_re, v_ssm_c_im, v_ssm_d, v_ssm_log_step, v_w_glu, v_attn_sinks, v_rel_bias_table, v_w_branch_ssm, v_w_branch_attn, v_w_out, v_ln_gain, v_ln_bias):
    t = x.shape[1]
    xs = x[0]
    target = loss_target[0]
    col_in = w_in.shape[2]
    col_br = w_glu.shape[2]
    row_out = w_out.shape[1]

    w_in_mx = w_in[0].astype(MXU_DTYPE)
    near = _split_start((w_in_mx, _landing_zone(w_in_mx)), _plan_gather_near, 4, name="gather_w_in_near_start")
    ssm_params = (ssm_lambda_re[0], ssm_lambda_im[0], ssm_b_re[0], ssm_b_im[0], ssm_c_re[0], ssm_c_im[0],
                  ssm_log_step[0] + near[-1][0, 0])
    mats, mats_vjp = jax.vjp(_ssm_matrices, *ssm_params)
    mats_mx = _ssm_expand(*mats[:3]) + (mats[3],)
    sinks = attn_sinks[0]
    d_skip = ssm_d
    _, landed = _split_wait(near, _plan_gather_near, mats_mx[1], name="gather_w_in_near_wait")
    relay = _split_start((landed,), _plan_gather_relay, 3, name="gather_w_in_relay_start")
    bias, bias_vjp = jax.vjp(_band_bias, rel_bias_table + relay[-1][0, 0])
    x_mx = xs.astype(MXU_DTYPE)
    (g_in,) = _split_wait(relay, _plan_gather_relay, bias, name="gather_w_in_relay_wait")
    win = g_in.transpose(1, 0, 2).reshape(D_MODEL, D_IN)
    hold = g_in[0, 0, 0] * 0
    three = jnp.concatenate([w_glu[0], w_branch_ssm[0], w_branch_attn[0]], axis=0).astype(MXU_DTYPE)
    sent_three = _scatter_start(three + hold, name="gather_w_1024_start")
    sent_wout = _scatter_start(w_out[0].astype(MXU_DTYPE) + hold, name="gather_w_out_start")

    proj = _mm(x_mx, win, name="in_proj", tm=2048, tn=512, tk=2048, after=(sent_three[4], sent_wout[4]))
    y_conv, states = _ssm_forward(proj, mats_mx)

    def f_gelu(yv, u, d):
        ys = yv + d * u
        return ys, _gelu(ys)

    y_ssm, glu_in = _ew(f_gelu, [(y_conv, "row", 0), (proj, "row", OFF_U), (d_skip, "vec", 0)],
                        [(COL, F32, "row"), (COL, MXU_DTYPE, "row")], rows=t, cw=COL, ncb=4, tr=1024, name="ssm_gelu")
    g_three = _scatter_wait(sent_three, glu_in, name="gather_w_1024_wait")
    three_full = g_three.transpose(1, 0, 2).reshape(3 * D_SSM, N_DEV * col_br)
    wglu, wbs, wba = three_full[:D_SSM], three_full[D_SSM:2 * D_SSM], three_full[2 * D_SSM:]
    gate_tn = COL
    z_ssm_col = [(proj, OFF_ZS * COL // gate_tn)]

    def f_hssm(products, gate):
        (ga, gb), (z,) = products, gate
        return ga, gb, ga * _sigmoid(gb) * _silu(z)

    glu_a, glu_b, h_ssm = _mm_fused([(glu_in, wglu[:, :D_SSM], False), (glu_in, wglu[:, D_SSM:], False)], z_ssm_col,
                                    [F32, F32, MXU_DTYPE], f_hssm, name="glu_gate", tm=2048, tn=gate_tn)

    attn, h_attn = _attn_forward(proj, bias, sinks)
    gate_cols = [(proj, OFF_GS * COL // gate_tn), (proj, OFF_GA * COL // gate_tn)]

    def f_merge(products, gates):
        (ps, pa), (ls, la) = products, gates
        return ps, pa, _sigmoid(ls) * ps + _sigmoid(la) * pa

    p_ssm, p_attn, merged = _mm_fused([(h_ssm, wbs, False), (h_attn, wba, False)], gate_cols, [F32, F32, MXU_DTYPE],
                                      f_merge, name="branch_merge", tm=2048, tn=gate_tn)
    wout = _scatter_wait(sent_wout, merged, name="gather_w_out_wait").reshape(D_MODEL, D_MODEL)
    out = _mm(merged, wout, name="out_proj", tm=2048, tn=512, tk=2048)

    def f_norm(xv, ov, tg, gain, lbias):
        r = DEEPNORM_ALPHA * xv + ov
        mu = jnp.mean(r, axis=1, keepdims=True)
        cen = r - mu
        var = jnp.mean(cen * cen, axis=1, keepdims=True)
        rstd = lax.rsqrt(var + LN_EPS)
        xhat = cen * rstd
        yv = xhat * gain + lbias
        diff = yv - tg
        row_loss = 0.5 * jnp.mean(diff * diff, axis=1, keepdims=True)
        loss = jnp.broadcast_to(jnp.sum(row_loss, axis=0, keepdims=True), (1, 128))
        dy = diff * (1.0 / D_MODEL)
        dgain = jnp.sum(dy * xhat, axis=0, keepdims=True)
        dbias = jnp.sum(dy, axis=0, keepdims=True)
        dxh = dy * gain
        dr = rstd * (dxh - jnp.mean(dxh, axis=1, keepdims=True) - xhat * jnp.mean(dxh * xhat, axis=1, keepdims=True))
        return dr, loss, dgain, dbias

    dr, loss_part, g_ln_gain, g_ln_bias = _ew(
        f_norm, [(xs, "row", 0), (out, "row", 0), (target, "row", 0), (ln_gain, "vec", 0), (ln_bias, "vec", 0)],
        [(D_MODEL, F32, "row"), (128, F32, "acc"), (D_MODEL, F32, "acc"), (D_MODEL, F32, "acc")],
        rows=t, cw=D_MODEL, ncb=1, tr=256, name="norm_loss")

    def scatter_cols(g, cols):
        return g.reshape(g.shape[0], N_DEV, cols).transpose(1, 0, 2)

    gw_out = _mm(merged, dr, ta=True, out_dtype=WIRE_DTYPE, name="grad_w_out", tm=2048, tn=512, tk=2048)
    sent_out = _scatter_start(gw_out.reshape(N_DEV, row_out, D_MODEL), name="scatter_g_out_start")
    def b_merge(products, tiles):
        (dm,), (ps, pa, ls, la) = products, tiles
        gs, ga = _sigmoid(ls), _sigmoid(la)
        return dm * gs, dm * ga, dm * ps * gs * (1.0 - gs), dm * pa * ga * (1.0 - ga)

    dp_ssm, dp_attn, dgl_s, dgl_a = _mm_fused(
        [(dr, wout, True)], [(p_ssm, 0), (p_attn, 0), (proj, OFF_GS), (proj, OFF_GA)], [MXU_DTYPE] * 4, b_merge,
        name="merge_bwd", tm=1024, tn=COL, after=(sent_out[4],))
    gw_bs = _mm(h_ssm, dp_ssm, ta=True, out_dtype=WIRE_DTYPE, name="grad_w_branch_ssm", tm=1024, tn=512, tk=2048)
    gw_ba = _mm(h_attn, dp_attn, ta=True, out_dtype=WIRE_DTYPE, name="grad_w_branch_attn", tm=1024, tn=512, tk=2048)
    dh_attn = _mm(dp_attn, wba, tb=True, name="d_h_attn", tm=2048, tn=512, tk=2048)

    def b_hssm(products, tiles):
        (dh,), (ga, gb, z) = products, tiles
        sg = _sigmoid(gb)
        dgate = dh * _silu(z)
        return dgate * sg, dgate * ga * sg * (1.0 - sg), dh * ga * sg * _silu_grad(z)

    dglu_a, dglu_b, dz_ssm = _mm_fused([(dp_ssm, wbs, True)], [(glu_a, 0), (glu_b, 0)] + z_ssm_col, [MXU_DTYPE] * 3,
                                       b_hssm, name="glu_gate_bwd", tm=2048, tn=gate_tn)
    dglu = jnp.concatenate([dglu_a, dglu_b], axis=1)
    gw_glu = _mm(glu_in, dglu, ta=True, out_dtype=WIRE_DTYPE, name="grad_w_glu", tm=1024, tn=512, tk=2048)
    sent_three = _scatter_start(scatter_cols(jnp.concatenate([gw_glu, gw_bs, gw_ba], axis=0), col_br),
                                name="scatter_g_1024_start")
    def b_gelu(products, tiles):
        return (products[0] * _gelu_grad(tiles[0]),)

    (dy_ssm,) = _mm_fused([(dglu, wglu, True)], [(y_ssm, 0)], [F32], b_gelu, name="gelu_bwd", tm=2048, tn=gate_tn,
                          after=(sent_three[4],))
    du_ssm, g_ssm_d, dmats = _ssm_backward(dy_ssm, proj, states, mats_mx)
    du = (du_ssm + d_skip * dy_ssm).astype(MXU_DTYPE)
    g_lre, g_lim, g_bre, g_bim, g_cre, g_cim, g_lstep = mats_vjp(dmats)

    dq, dz_attn, dka, dkb, dva, dvb, dbias, dsink = _attn_backward(proj, bias, sinks, attn, dh_attn)
    dk = _shift_blocks(dka, dkb).astype(MXU_DTYPE)
    dv = _shift_blocks(dva, dvb).astype(MXU_DTYPE)
    (g_table,) = bias_vjp(dbias)
    g_sinks = dsink[:, 0, 0]

    dproj = jnp.concatenate([du, dz_ssm, dq, dk, dv, dz_attn, dgl_s, dgl_a], axis=1)
    small_g = [g_lre, g_lim, g_bre, g_bim, g_cre, g_cim, g_ssm_d, g_lstep, g_sinks, g_table, g_ln_gain, g_ln_bias]
    sent_small = _scatter_start(_pack(small_g), name="gather_g_small_start")
    gw_in = _mm(x_mx, dproj, ta=True, out_dtype=WIRE_DTYPE, name="grad_w_in", tm=2048, tn=512, tk=2048,
                after=(sent_small[4],))
    sent_in = _scatter_start(scatter_cols(gw_in, col_in), name="scatter_g_in_start")
    grad_x = _mm(dproj, win, tb=True, add=dr, add_scale=DEEPNORM_ALPHA, name="grad_x", tm=1024, tn=512, tk=4352,
                 after=(sent_in[4],))

    parts_out = _scatter_wait(sent_out, grad_x, name="scatter_g_out_wait")
    parts_three = _scatter_wait(sent_three, parts_out, name="scatter_g_1024_wait")
    parts_in = _scatter_wait(sent_in, parts_three, name="scatter_g_in_wait")

    o_in = _adamw_reduce(parts_in, w_in[0], m_w_in[0], v_w_in[0], name="adamw_w_in", tr=128)
    three_w = jnp.concatenate([w_glu[0], w_branch_ssm[0], w_branch_attn[0]], axis=0)
    three_m = jnp.concatenate([m_w_glu[0], m_w_branch_ssm[0], m_w_branch_attn[0]], axis=0)
    three_v = jnp.concatenate([v_w_glu[0], v_w_branch_ssm[0], v_w_branch_attn[0]], axis=0)
    o_three = _adamw_reduce(parts_three, three_w, three_m, three_v, name="adamw_w_1024", tr=512)
    o_out = _adamw_reduce(parts_out, w_out[0], m_w_out[0], v_w_out[0], name="adamw_w_out", tr=128)

    small_w = [ssm_lambda_re, ssm_lambda_im, ssm_b_re, ssm_b_im, ssm_c_re, ssm_c_im, ssm_d, ssm_log_step,
               attn_sinks, rel_bias_table, ln_gain, ln_bias]
    small_m = [m_ssm_lambda_re, m_ssm_lambda_im, m_ssm_b_re, m_ssm_b_im, m_ssm_c_re, m_ssm_c_im, m_ssm_d,
               m_ssm_log_step, m_attn_sinks, m_rel_bias_table, m_ln_gain, m_ln_bias]
    small_v = [v_ssm_lambda_re, v_ssm_lambda_im, v_ssm_b_re, v_ssm_b_im, v_ssm_c_re, v_ssm_c_im, v_ssm_d,
               v_ssm_log_step, v_attn_sinks, v_rel_bias_table, v_ln_gain, v_ln_bias]
    parts_small = _scatter_wait(sent_small, parts_in, name="gather_g_small_wait")
    sg = _unpack(_sum_parts(parts_small, name="sum_g_small"), small_w)
    updates = [_adamw_native(w, g, m, v, name="adamw_" + n)
               for n, w, g, m, v in zip(SMALL, small_w, sg, small_m, small_v)]
    sd, sm, sv = zip(*updates)

    loss = lax.psum(loss_part[0, 0], MESH_AXES)

    def big(o, idx):
        g_in_, g_three_, g_out_ = o_in[idx], o_three[idx], o_out[idx]
        return {"w_in": g_in_[None], "w_glu": g_three_[None, :D_SSM], "w_branch_ssm": g_three_[None, D_SSM:2 * D_SSM],
                "w_branch_attn": g_three_[None, 2 * D_SSM:], "w_out": g_out_[None]}

    order = ["w_in", "ssm_lambda_re", "ssm_lambda_im", "ssm_b_re", "ssm_b_im", "ssm_c_re", "ssm_c_im", "ssm_d",
             "ssm_log_step", "w_glu", "attn_sinks", "rel_bias_table", "w_branch_ssm", "w_branch_attn", "w_out",
             "ln_gain", "ln_bias"]
    outs = [loss, grad_x[None]]
    for idx, small in enumerate([sg, sd, sm, sv]):
        table = big(None, idx)
        table.update(dict(zip(SMALL, small)))
        outs += [table[n] for n in order]
    return tuple(outs)
```

```python
import math

import jax
import jax.numpy as jnp
from jax import lax
from jax.experimental import pallas as pl
from jax.experimental.pallas import tpu as pltpu

F32 = jnp.float32
MXU_DTYPE = jnp.bfloat16
WIRE_DTYPE = jnp.bfloat16

D_MODEL = 2048
D_SSM = 1024
SSM_GROUP = 16
N_GROUPS = 64
SSM_STATE = 64
N_Q_HEADS = 16
N_KV_HEADS = 4
Q_PER_KV = 4
HEAD_DIM = 64
D_ATTN = 1024
D_KV = 256
WINDOW = 128
BLOCK = 128
N_BUCKETS = 32
MAX_DISTANCE = 128
D_IN = 8704
DEEPNORM_ALPHA = 2.0 ** 0.25
LN_EPS = 1e-5
NEG_INF = -1e30
ATTN_SCALE = HEAD_DIM ** -0.5

ADAM_LR = 0.001
ADAM_B1 = 0.9
ADAM_B2 = 0.999
ADAM_EPS = 1e-08
ADAM_WD = 0.01
ADAM_STEP = 10

N_DEV = 8
SSM_CHUNK = 16
GROUP_BLOCK = 8
N_GB = N_GROUPS // GROUP_BLOCK
GB_LANES = GROUP_BLOCK * SSM_GROUP
GB_STATE = GROUP_BLOCK * SSM_STATE
COL = 256
OFF_U, OFF_ZS, OFF_Q, OFF_K, OFF_V, OFF_ZA, OFF_GS, OFF_GA = 0, 4, 8, 12, 13, 14, 18, 26

VMEM_CAP = 56 * 1024 * 1024
MESH_AXES = ("x", "y", "c")


def _vmem_limit(block_bytes):
    return int(min(max(3 * block_bytes, 16 * 1024 * 1024), VMEM_CAP))


def _nbytes(shape, dtype):
    return math.prod(shape) * jnp.dtype(dtype).itemsize


def _tile(n, pref):
    if n <= pref:
        return n
    t = (pref // 128) * 128
    while t >= 128:
        if n % t == 0:
            return t
        t -= 128
    return n


def _mm(a, b, *, name, ta=False, tb=False, out_dtype=F32, tm=1024, tn=512, tk=512, add=None, add_scale=1.0,
        after=()):
    squeeze = a.ndim == 2
    if squeeze:
        a, b = a[None], b[None]
        if add is not None:
            add = add[None]
    nb = a.shape[0]
    m, k = (a.shape[2], a.shape[1]) if ta else (a.shape[1], a.shape[2])
    n = b.shape[1] if tb else b.shape[2]
    tm, tn, tk = _tile(m, tm), _tile(n, tn), _tile(k, tk)
    nk = k // tk
    dn = (((0 if ta else 1,), (1 if tb else 0,)), ((), ()))

    a_spec = (pl.BlockSpec((None, tk, tm), lambda g, i, j, kk: (g, kk, i)) if ta
              else pl.BlockSpec((None, tm, tk), lambda g, i, j, kk: (g, i, kk)))
    b_spec = (pl.BlockSpec((None, tn, tk), lambda g, i, j, kk: (g, j, kk)) if tb
              else pl.BlockSpec((None, tk, tn), lambda g, i, j, kk: (g, kk, j)))
    o_spec = pl.BlockSpec((None, tm, tn), lambda g, i, j, kk: (g, i, j))
    in_specs = [a_spec, b_spec]
    operands = [a, b]
    if add is not None:
        in_specs.append(o_spec)
        operands.append(add)
    for tok in after:
        in_specs.append(pl.BlockSpec(memory_space=pl.ANY))
        operands.append(tok)
    n_in = len(operands)

    def body(*refs):
        a_ref, b_ref = refs[0], refs[1]
        add_ref = refs[2] if add is not None else None
        o_ref = refs[n_in]
        acc_ref = refs[-1]
        kk = pl.program_id(3)
        part = lax.dot_general(a_ref[...].astype(MXU_DTYPE), b_ref[...].astype(MXU_DTYPE), dn,
                               preferred_element_type=F32)

        def finish(r):
            if add_ref is not None:
                r = r + add_scale * add_ref[...]
            o_ref[...] = r.astype(out_dtype)

        if nk == 1:
            finish(part)
            return

        @pl.when(kk == 0)
        def _():
            acc_ref[...] = part

        @pl.when(jnp.logical_and(kk > 0, kk < nk - 1))
        def _():
            acc_ref[...] += part

        @pl.when(kk == nk - 1)
        def _():
            finish(acc_ref[...] + part)

    blocks = (_nbytes((tm, tk), a.dtype) + _nbytes((tk, tn), b.dtype) + _nbytes((tm, tn), out_dtype)
              + (_nbytes((tm, tn), F32) if add is not None else 0))
    out = pl.pallas_call(
        body,
        name=name,
        grid=(nb, m // tm, n // tn, nk),
        in_specs=in_specs,
        out_specs=o_spec,
        out_shape=jax.ShapeDtypeStruct((nb, m, n), out_dtype),
        scratch_shapes=[pltpu.VMEM((tm, tn), F32)],
        compiler_params=pltpu.CompilerParams(
            dimension_semantics=("parallel", "parallel", "parallel", "arbitrary"),
            vmem_limit_bytes=_vmem_limit(2 * blocks + 2 * _nbytes((tm, tn), F32))),
    )(*operands)
    return out[0] if squeeze else out


def _mm_fused(pairs, extras, out_dtypes, epilogue, *, name, tm, tn, after=()):
    m = pairs[0][0].shape[0]
    n = pairs[0][1].shape[0] if pairs[0][2] else pairs[0][1].shape[1]
    tm, tn = _tile(m, tm), _tile(n, tn)
    in_specs, operands, dns = [], [], []
    for a, b, tb in pairs:
        k = a.shape[1]
        in_specs += [pl.BlockSpec((tm, k), lambda i, j: (i, 0)),
                     pl.BlockSpec((tn, k), lambda i, j: (j, 0)) if tb else pl.BlockSpec((k, tn), lambda i, j: (0, j))]
        operands += [a, b]
        dns.append((((1,), (1 if tb else 0,)), ((), ())))

    def tile_at(col):
        return pl.BlockSpec((tm, tn), lambda i, j: (i, col + j))

    in_specs += [tile_at(col) for _, col in extras]
    operands += [arr for arr, _ in extras]
    in_specs += [pl.BlockSpec(memory_space=pl.ANY)] * len(after)
    operands += list(after)
    n_pairs, n_extra, n_in = len(pairs), len(extras), len(operands)

    def body(*refs):
        products = [lax.dot_general(refs[2 * p][...].astype(MXU_DTYPE), refs[2 * p + 1][...].astype(MXU_DTYPE),
                                    dns[p], preferred_element_type=F32) for p in range(n_pairs)]
        tiles = [refs[2 * n_pairs + e][...] for e in range(n_extra)]
        for o_ref, val in zip(refs[n_in:], epilogue(products, tiles)):
            o_ref[...] = val.astype(o_ref.dtype)

    blocks = sum(_nbytes((tm, a.shape[1]), a.dtype) + _nbytes((tn, a.shape[1]), b.dtype) for a, b, _ in pairs)
    blocks += sum(_nbytes((tm, tn), arr.dtype) for arr, _ in extras) + sum(_nbytes((tm, tn), dt) for dt in out_dtypes)
    return pl.pallas_call(
        body, name=name, grid=(m // tm, n // tn), in_specs=in_specs,
        out_specs=[tile_at(0)] * len(out_dtypes),
        out_shape=[jax.ShapeDtypeStruct((m, n), dt) for dt in out_dtypes],
        compiler_params=pltpu.CompilerParams(
            dimension_semantics=("parallel", "parallel"),
            vmem_limit_bytes=_vmem_limit(2 * blocks + n_pairs * _nbytes((tm, tn), F32))),
    )(*operands)


def _ew(fn, ins, outs, *, rows, cw, ncb, tr, name):
    tr = min(tr, rows)
    n_in = len(ins)

    def row_map(off):
        return lambda j, i: (i, off + j)

    def vec_map(off):
        return lambda j, i: (0, off + j)

    in_specs = []
    for arr, kind, off in ins:
        if kind == "row":
            in_specs.append(pl.BlockSpec((tr, cw), row_map(off)))
        else:
            in_specs.append(pl.BlockSpec((1, cw), vec_map(off)))
    out_specs, out_shapes = [], []
    for bw, dt, kind in outs:
        if kind == "row":
            out_specs.append(pl.BlockSpec((tr, bw), row_map(0)))
            out_shapes.append(jax.ShapeDtypeStruct((rows, ncb * bw), dt))
        else:
            out_specs.append(pl.BlockSpec((1, bw), vec_map(0)))
            out_shapes.append(jax.ShapeDtypeStruct((1, ncb * bw), F32))

    def body(*refs):
        i = pl.program_id(1)
        vals = fn(*[r[...] for r in refs[:n_in]])
        for r, (bw, dt, kind), v in zip(refs[n_in:], outs, vals):
            if kind == "row":
                r[...] = v.astype(dt)
            else:
                @pl.when(i == 0)
                def _(r=r):
                    r[...] = jnp.zeros_like(r)

                r[...] += v

    blocks = sum(_nbytes((tr, cw), a.dtype) for a, kind, _ in ins if kind == "row")
    blocks += sum(_nbytes((tr, bw), dt) for bw, dt, kind in outs if kind == "row")
    res = pl.pallas_call(
        body,
        name=name,
        grid=(ncb, rows // tr),
        in_specs=in_specs,
        out_specs=out_specs,
        out_shape=out_shapes,
        compiler_params=pltpu.CompilerParams(
            dimension_semantics=("parallel", "arbitrary"),
            vmem_limit_bytes=_vmem_limit(4 * blocks)),
    )(*[a for a, _, _ in ins])
    return res


def _sigmoid(x):
    return 1.0 / (1.0 + jnp.exp(-x))


INV_SQRT2 = 0.7071067811865476
INV_SQRT_2PI = 0.3989422804014327


def _gelu(x):
    return 0.5 * x * (1.0 + lax.erf(x * INV_SQRT2))


def _gelu_grad(x):
    return 0.5 * (1.0 + lax.erf(x * INV_SQRT2)) + x * INV_SQRT_2PI * jnp.exp(-0.5 * x * x)


def _silu(x):
    return x * _sigmoid(x)


def _silu_grad(x):
    s = _sigmoid(x)
    return s * (1.0 + x * (1.0 - s))


@jax.custom_vjp
def _taps_product(mr, mi, bbr, bbi):
    bbr_t, bbi_t = jnp.transpose(bbr, (0, 2, 1)), jnp.transpose(bbi, (0, 2, 1))
    return jnp.sum(mr[..., None, :] * bbr_t[None, :, None] - mi[..., None, :] * bbi_t[None, :, None], axis=-1)


def _taps_product_fwd(mr, mi, bbr, bbi):
    return _taps_product(mr, mi, bbr, bbi), (mr, mi, bbr, bbi)


def _taps_product_bwd(res, g):
    mr, mi, bbr, bbi = res
    hi = lax.Precision.HIGHEST
    return (jnp.einsum("tghk,gpk->tghp", g, bbr, precision=hi), -jnp.einsum("tghk,gpk->tghp", g, bbi, precision=hi),
            jnp.einsum("tghk,tghp->gpk", g, mr, precision=hi), -jnp.einsum("tghk,tghp->gpk", g, mi, precision=hi))


_taps_product.defvjp(_taps_product_fwd, _taps_product_bwd)


def _ssm_matrices(lam_re, lam_im, b_re, b_im, c_re, c_im, log_step):
    L = SSM_CHUNK
    step = jnp.exp(log_step)[:, None]
    ea, eb = lam_re * step, lam_im * step
    mag = jnp.exp(ea)
    lbr, lbi = mag * jnp.cos(eb), mag * jnp.sin(eb)
    den = lam_re * lam_re + lam_im * lam_im
    nr, ni = lbr - 1.0, lbi
    cr = (nr * lam_re + ni * lam_im) / den
    ci = (ni * lam_re - nr * lam_im) / den
    bbr = cr[..., None] * b_re - ci[..., None] * b_im
    bbi = cr[..., None] * b_im + ci[..., None] * b_re
    taus = jnp.arange(L + 1, dtype=F32)[:, None, None]
    pmag = jnp.exp(taus * ea[None])
    pwr, pwi = pmag * jnp.cos(taus * eb[None]), pmag * jnp.sin(taus * eb[None])
    mr = c_re[None] * pwr[:L, :, None, :] - c_im[None] * pwi[:L, :, None, :]
    mi = c_re[None] * pwi[:L, :, None, :] + c_im[None] * pwr[:L, :, None, :]
    kk = _taps_product(mr, mi, bbr, bbi)
    taps = jnp.transpose(kk.reshape(L, N_GB, GROUP_BLOCK, SSM_GROUP, SSM_GROUP), (1, 0, 2, 4, 3))
    taps = taps.reshape(N_GB, L, GB_LANES, SSM_GROUP)
    rev_r, rev_i = pwr[L - 1 - jnp.arange(L)], pwi[L - 1 - jnp.arange(L)]
    wer = rev_r[..., None] * bbr[None] - rev_i[..., None] * bbi[None]
    wei = rev_r[..., None] * bbi[None] + rev_i[..., None] * bbr[None]

    def rows_in(w):
        w = jnp.transpose(w.reshape(L, N_GB, GROUP_BLOCK, SSM_STATE, SSM_GROUP), (1, 0, 2, 4, 3))
        return w.reshape(N_GB, L * GB_LANES, SSM_STATE)

    wend = jnp.concatenate([rows_in(wer), rows_in(wei)], axis=2)
    m1r = c_re[None] * pwr[1:, :, None, :] - c_im[None] * pwi[1:, :, None, :]
    m1i = c_re[None] * pwi[1:, :, None, :] + c_im[None] * pwr[1:, :, None, :]

    def rows_out(m):
        m = jnp.transpose(m.reshape(L, N_GB, GROUP_BLOCK, SSM_GROUP, SSM_STATE), (1, 2, 4, 0, 3))
        return m.reshape(N_GB, GB_STATE, L * SSM_GROUP)

    wout = jnp.concatenate([rows_out(m1r), rows_out(-m1i)], axis=1)
    acat = jnp.concatenate([pwr[L].reshape(N_GB, 1, GB_STATE), pwi[L].reshape(N_GB, 1, GB_STATE)], axis=2)
    return taps, wend, wout, acat


def _lane_group(shape, axis, shift):
    return (lax.broadcasted_iota(jnp.int32, shape, axis) >> shift) & (GROUP_BLOCK - 1)


def _ssm_expand(taps, wend, wout):
    L = SSM_CHUNK
    taps = jnp.pad(taps, ((0, 0), (0, 0), (0, 0), (0, GB_LANES - SSM_GROUP)))

    def body(t_ref, we_ref, wo_ref, d_ref, web_ref, wob_ref):
        def rc(shape):
            return lax.broadcasted_iota(jnp.int32, shape, 0), lax.broadcasted_iota(jnp.int32, shape, 1)

        r, c = rc((GB_LANES, GB_LANES))
        spread = ((r < SSM_GROUP) & (r == (c & (SSM_GROUP - 1)))).astype(MXU_DTYPE)
        same = _lane_group((GB_LANES, GB_LANES), 0, 4) == _lane_group((GB_LANES, GB_LANES), 1, 4)
        for tau in range(L):
            full = jnp.dot(t_ref[tau].astype(MXU_DTYPE), spread, preferred_element_type=F32)
            d_ref[tau] = jnp.where(same, full, 0.0).astype(d_ref.dtype)
        r, c = rc((2 * SSM_STATE, STATE_W))
        part = (r >> 6) == (c >> 9)
        spread = (part & ((r & (SSM_STATE - 1)) == (c & (SSM_STATE - 1)))).astype(MXU_DTYPE)
        keep = _lane_group((GB_LANES, STATE_W), 0, 4) == _lane_group((GB_LANES, STATE_W), 1, 6)
        for j in range(L):
            rows = slice(j * GB_LANES, (j + 1) * GB_LANES)
            full = jnp.dot(we_ref[rows, :].astype(MXU_DTYPE), spread, preferred_element_type=F32)
            web_ref[rows, :] = jnp.where(keep, full, 0.0).astype(web_ref.dtype)
        r, c = rc((GB_LANES, GB_LANES))
        own = _lane_group((STATE_W, GB_LANES), 0, 6) == _lane_group((STATE_W, GB_LANES), 1, 4)
        for tt in range(L):
            half, k = tt // GROUP_BLOCK, tt % GROUP_BLOCK
            spread = (((r >> 4) == k) & ((r & (SSM_GROUP - 1)) == (c & (SSM_GROUP - 1)))).astype(MXU_DTYPE)
            src = wo_ref[:, half * GB_LANES:(half + 1) * GB_LANES].astype(MXU_DTYPE)
            full = jnp.dot(src, spread, preferred_element_type=F32)
            wob_ref[:, tt * GB_LANES:(tt + 1) * GB_LANES] = jnp.where(own, full, 0.0).astype(wob_ref.dtype)

    def spec(shape):
        return pl.BlockSpec((None,) + shape[1:], lambda b: (b,) + (0,) * (len(shape) - 1))

    out_shapes = [(N_GB, L, GB_LANES, GB_LANES), (N_GB, L * GB_LANES, STATE_W), (N_GB, STATE_W, L * GB_LANES)]
    return tuple(pl.pallas_call(
        body, name="ssm_expand", grid=(N_GB,),
        in_specs=[spec(taps.shape), spec(wend.shape), spec(wout.shape)],
        out_specs=[spec(s) for s in out_shapes],
        out_shape=[jax.ShapeDtypeStruct(s, MXU_DTYPE) for s in out_shapes],
        compiler_params=pltpu.CompilerParams(dimension_semantics=("parallel",), vmem_limit_bytes=VMEM_CAP),
    )(taps, wend, wout))


def _dot(a, b):
    return jnp.dot(a.astype(MXU_DTYPE), b.astype(MXU_DTYPE), preferred_element_type=F32)


def _dot_nt(a, b):
    return lax.dot_general(a.astype(MXU_DTYPE), b.astype(MXU_DTYPE), (((1,), (1,)), ((), ())),
                           preferred_element_type=F32)


def _dot_tn(a, b):
    return lax.dot_general(a.astype(MXU_DTYPE), b.astype(MXU_DTYPE), (((0,), (0,)), ((), ())),
                           preferred_element_type=F32)


STATE_W = 2 * GB_STATE


def _chunk_scan(e, acat):
    nc = e.shape[0]
    spec = pl.BlockSpec((nc, STATE_W), lambda b: (0, b))
    aspec = pl.BlockSpec((None, 1, STATE_W), lambda b: (b, 0, 0))

    def body(e_ref, a_ref, s_ref):
        a_r, a_i = a_ref[:, :GB_STATE], a_ref[:, GB_STATE:]

        def step(c, carry):
            s_r, s_i = carry
            s_ref[pl.ds(c, 1), :GB_STATE] = s_r
            s_ref[pl.ds(c, 1), GB_STATE:] = s_i
            e_r = e_ref[pl.ds(c, 1), :GB_STATE]
            e_i = e_ref[pl.ds(c, 1), GB_STATE:]
            return (a_r * s_r - a_i * s_i + e_r, a_r * s_i + a_i * s_r + e_i)

        zero = jnp.zeros((1, GB_STATE), F32)
        lax.fori_loop(0, nc, step, (zero, zero))

    return pl.pallas_call(
        body, name="ssm_chunk_scan", grid=(N_GB,),
        in_specs=[spec, aspec], out_specs=spec,
        out_shape=jax.ShapeDtypeStruct(e.shape, F32),
        compiler_params=pltpu.CompilerParams(dimension_semantics=("parallel",)),
    )(e, acat)


def _chunk_scan_bwd(ds, s, acat):
    nc = ds.shape[0]
    spec = pl.BlockSpec((nc, STATE_W), lambda b: (0, b))
    aspec = pl.BlockSpec((None, 1, STATE_W), lambda b: (b, 0, 0))

    def body(ds_ref, s_ref, a_ref, ge_ref, da_ref):
        a_r, a_i = a_ref[:, :GB_STATE], a_ref[:, GB_STATE:]

        def step(t, carry):
            g_r, g_i, d_r, d_i = carry
            c = nc - 1 - t
            ge_ref[pl.ds(c, 1), :GB_STATE] = g_r
            ge_ref[pl.ds(c, 1), GB_STATE:] = g_i
            s_r = s_ref[pl.ds(c, 1), :GB_STATE]
            s_i = s_ref[pl.ds(c, 1), GB_STATE:]
            d_r = d_r + g_r * s_r + g_i * s_i
            d_i = d_i + g_i * s_r - g_r * s_i
            n_r = ds_ref[pl.ds(c, 1), :GB_STATE] + a_r * g_r + a_i * g_i
            n_i = ds_ref[pl.ds(c, 1), GB_STATE:] + a_r * g_i - a_i * g_r
            return (n_r, n_i, d_r, d_i)

        zero = jnp.zeros((1, GB_STATE), F32)
        _, _, d_r, d_i = lax.fori_loop(0, nc, step, (zero, zero, zero, zero))
        da_ref[:, :GB_STATE] = d_r
        da_ref[:, GB_STATE:] = d_i

    return pl.pallas_call(
        body, name="ssm_chunk_scan_bwd", grid=(N_GB,),
        in_specs=[spec, spec, aspec], out_specs=[spec, aspec],
        out_shape=[jax.ShapeDtypeStruct(ds.shape, F32), jax.ShapeDtypeStruct(acat.shape, F32)],
        compiler_params=pltpu.CompilerParams(dimension_semantics=("parallel",)),
    )(ds, s, acat)


def _step_rows(ref, j, nc):
    return ref[pl.ds(j, nc, stride=SSM_CHUNK), :]


def _fold_lanes(z, widths):
    for w in widths:
        z = z + pltpu.roll(z, w, 1)
    return z


def _ssm_forward(proj, mats):
    dblk, wend, wout, acat = mats
    t = proj.shape[0]
    nc = t // SSM_CHUNK
    L = SSM_CHUNK
    lanes = pl.BlockSpec((t, GB_LANES), lambda b: (0, b))
    state = pl.BlockSpec((nc, STATE_W), lambda b: (0, b))

    def body_end(u_ref, w_ref, e_ref):
        x = jnp.concatenate([_step_rows(u_ref, j, nc).astype(MXU_DTYPE) for j in range(L)], axis=1)
        e_ref[...] = jnp.dot(x, w_ref[...], preferred_element_type=F32)

    e = pl.pallas_call(
        body_end, name="ssm_chunk_end", grid=(N_GB,),
        in_specs=[lanes, pl.BlockSpec((None, L * GB_LANES, STATE_W), lambda b: (b, 0, 0))],
        out_specs=state, out_shape=jax.ShapeDtypeStruct((nc, N_GB * STATE_W), F32),
        compiler_params=pltpu.CompilerParams(dimension_semantics=("parallel",), vmem_limit_bytes=VMEM_CAP),
    )(proj, wend)
    s = _chunk_scan(e, acat)

    def body_out(u_ref, d_ref, s_ref, w_ref, y_ref):
        xs = [_step_rows(u_ref, j, nc).astype(MXU_DTYPE) for j in range(L)]
        sb = s_ref[...].astype(MXU_DTYPE)
        for tt in range(L):
            xcat = jnp.concatenate(xs[:tt + 1], axis=1)
            taps = jnp.concatenate([d_ref[tt - j] for j in range(tt + 1)], axis=0).astype(MXU_DTYPE)
            y = (jnp.dot(xcat, taps, preferred_element_type=F32)
                 + jnp.dot(sb, w_ref[:, tt * GB_LANES:(tt + 1) * GB_LANES], preferred_element_type=F32))
            y_ref[pl.ds(tt, nc, stride=L), :] = y

    y = pl.pallas_call(
        body_out, name="ssm_chunk_out", grid=(N_GB,),
        in_specs=[lanes, pl.BlockSpec((None, L, GB_LANES, GB_LANES), lambda b: (b, 0, 0, 0)), state,
                  pl.BlockSpec((None, STATE_W, L * GB_LANES), lambda b: (b, 0, 0))],
        out_specs=lanes, out_shape=jax.ShapeDtypeStruct((t, D_SSM), F32),
        compiler_params=pltpu.CompilerParams(dimension_semantics=("parallel",), vmem_limit_bytes=VMEM_CAP),
    )(proj, dblk, s, wout)
    return y, s


def _ssm_backward(dy, proj, s, mats):
    dblk, wend, wout, acat = mats
    t = proj.shape[0]
    nc = t // SSM_CHUNK
    L = SSM_CHUNK
    lanes = pl.BlockSpec((t, GB_LANES), lambda b: (0, b))
    state = pl.BlockSpec((nc, STATE_W), lambda b: (0, b))
    taps_spec = pl.BlockSpec((None, L, GB_LANES, GB_LANES), lambda b: (b, 0, 0, 0))

    def body_state(dy_ref, w_ref, ds_ref):
        dyc = jnp.concatenate([_step_rows(dy_ref, tt, nc).astype(MXU_DTYPE) for tt in range(L)], axis=1)
        ds_ref[...] = _dot_nt(dyc, w_ref[...])

    ds = pl.pallas_call(
        body_state, name="ssm_bwd_state", grid=(N_GB,),
        in_specs=[lanes, pl.BlockSpec((None, STATE_W, L * GB_LANES), lambda b: (b, 0, 0))],
        out_specs=state, out_shape=jax.ShapeDtypeStruct((nc, N_GB * STATE_W), F32),
        compiler_params=pltpu.CompilerParams(dimension_semantics=("parallel",), vmem_limit_bytes=VMEM_CAP),
    )(dy, wout)
    ge, dacat = _chunk_scan_bwd(ds, s, acat)

    def body_in(u_ref, dy_ref, d_ref, ge_ref, w_ref, du_ref, dd_ref, gd_ref):
        gd_ref[...] = jnp.sum(u_ref[...] * dy_ref[...], axis=0, keepdims=True)
        xs = [_step_rows(u_ref, j, nc).astype(MXU_DTYPE) for j in range(L)]
        dys = [_step_rows(dy_ref, tt, nc).astype(MXU_DTYPE) for tt in range(L)]
        ge = ge_ref[...].astype(MXU_DTYPE)
        for i in range(L):
            dyc = jnp.concatenate(dys[i:], axis=1)
            taps = jnp.concatenate([d_ref[tt - i] for tt in range(i, L)], axis=1).astype(MXU_DTYPE)
            du_ref[pl.ds(i, nc, stride=L), :] = (
                _dot_nt(dyc, taps) + _dot_nt(ge, w_ref[i * GB_LANES:(i + 1) * GB_LANES, :]))
        for j in range(L):
            m = _dot_tn(xs[j], jnp.concatenate(dys[j:], axis=1))
            for tau in range(L - j):
                part = m[:, tau * GB_LANES:(tau + 1) * GB_LANES]
                if j == 0:
                    dd_ref[tau] = part
                else:
                    dd_ref[tau] += part
        same = _lane_group((GB_LANES, GB_LANES), 0, 4) == _lane_group((GB_LANES, GB_LANES), 1, 4)
        for tau in range(L):
            dd_ref[tau] = _fold_lanes(jnp.where(same, dd_ref[tau], 0.0), (64, 32, 16))

    du, ddblk, gd = pl.pallas_call(
        body_in, name="ssm_bwd_in", grid=(N_GB,),
        in_specs=[lanes, lanes, taps_spec, state,
                  pl.BlockSpec((None, L * GB_LANES, STATE_W), lambda b: (b, 0, 0))],
        out_specs=[lanes, taps_spec, pl.BlockSpec((1, GB_LANES), lambda b: (0, b))],
        out_shape=[jax.ShapeDtypeStruct((t, D_SSM), F32), jax.ShapeDtypeStruct(dblk.shape, F32),
                   jax.ShapeDtypeStruct((1, D_SSM), F32)],
        compiler_params=pltpu.CompilerParams(dimension_semantics=("parallel",), vmem_limit_bytes=VMEM_CAP),
    )(proj, dy, dblk, ge, wend)

    def body_w(u_ref, dy_ref, ge_ref, s_ref, dwe_ref, dwo_ref):
        ge = ge_ref[...].astype(MXU_DTYPE)
        sb = s_ref[...].astype(MXU_DTYPE)
        keep = _lane_group((GB_LANES, STATE_W), 0, 4) == _lane_group((GB_LANES, STATE_W), 1, 6)
        low = lax.broadcasted_iota(jnp.int32, (GB_LANES, 2 * SSM_STATE), 1) < SSM_STATE

        def fold_state(z):
            z = z[:, :GB_STATE // 2] + z[:, GB_STATE // 2:]
            z = z[:, :GB_STATE // 4] + z[:, GB_STATE // 4:]
            return _fold_lanes(z, (SSM_STATE,))

        for j in range(L):
            z = jnp.where(keep, _dot_tn(_step_rows(u_ref, j, nc).astype(MXU_DTYPE), ge), 0.0)
            dwe_ref[j * GB_LANES:(j + 1) * GB_LANES, :] = jnp.where(
                low, fold_state(z[:, :GB_STATE]), fold_state(z[:, GB_STATE:]))
        own = _lane_group((STATE_W, GB_LANES), 0, 6) == _lane_group((STATE_W, GB_LANES), 1, 4)
        chunk = lax.broadcasted_iota(jnp.int32, (STATE_W, GB_LANES), 1) >> 4
        for half in range(L // GROUP_BLOCK):
            acc = jnp.zeros((STATE_W, GB_LANES), F32)
            for k in range(GROUP_BLOCK):
                tt = half * GROUP_BLOCK + k
                z = jnp.where(own, _dot_tn(sb, _step_rows(dy_ref, tt, nc).astype(MXU_DTYPE)), 0.0)
                acc = acc + jnp.where(chunk == k, _fold_lanes(z, (64, 32, 16)), 0.0)
            dwo_ref[:, half * GB_LANES:(half + 1) * GB_LANES] = acc

    dwend, dwout = pl.pallas_call(
        body_w, name="ssm_bwd_w", grid=(N_GB,),
        in_specs=[lanes, lanes, state, state],
        out_specs=[pl.BlockSpec((None, L * GB_LANES, 2 * SSM_STATE), lambda b: (b, 0, 0)),
                   pl.BlockSpec((None, STATE_W, L * SSM_GROUP), lambda b: (b, 0, 0))],
        out_shape=[jax.ShapeDtypeStruct((N_GB, L * GB_LANES, 2 * SSM_STATE), F32),
                   jax.ShapeDtypeStruct((N_GB, STATE_W, L * SSM_GROUP), F32)],
        compiler_params=pltpu.CompilerParams(dimension_semantics=("parallel",), vmem_limit_bytes=VMEM_CAP),
    )(proj, dy, ge, s)
    return du, gd, (ddblk[:, :, :, :SSM_GROUP], dwend, dwout, dacat)


def _t5_bucket(dist):
    max_exact = N_BUCKETS // 2
    is_small = dist < max_exact
    d = jnp.maximum(dist, 1).astype(F32)
    large = max_exact + (jnp.log(d / max_exact) / math.log(MAX_DISTANCE / max_exact)
                         * (N_BUCKETS - max_exact)).astype(jnp.int32)
    large = jnp.minimum(large, N_BUCKETS - 1)
    return jnp.where(is_small, dist, large)


def _band_bias(rel_bias_table):
    i = jnp.arange(BLOCK)[:, None]
    j = jnp.arange(BLOCK)[None, :]
    bucket = _t5_bucket(jnp.where(j > i, BLOCK + i - j, i - j))
    onehot = (bucket[:, :, None] == jnp.arange(N_BUCKETS)[None, None, :]).astype(F32)
    return jnp.einsum("qsb,bh->hqs", onehot, rel_bias_table, precision=lax.Precision.HIGHEST)


assert WINDOW == BLOCK
KV_PAIR = 2
HEADS_PER_STEP = KV_PAIR * Q_PER_KV
Q_LANES = HEADS_PER_STEP * HEAD_DIM
SLAB = 2 * HEAD_DIM
Q_COL0 = OFF_Q * COL // Q_LANES
K_COL0 = OFF_K * COL // SLAB
V_COL0 = OFF_V * COL // SLAB
ZA_COL0 = OFF_ZA * COL // Q_LANES


def _attn_specs():
    q_spec = pl.BlockSpec((BLOCK, Q_LANES), lambda m, n: (n, Q_COL0 + m))
    k_prev = pl.BlockSpec((BLOCK, SLAB), lambda m, n: (jnp.maximum(n - 1, 0), K_COL0 + m))
    k_cur = pl.BlockSpec((BLOCK, SLAB), lambda m, n: (n, K_COL0 + m))
    v_prev = pl.BlockSpec((BLOCK, SLAB), lambda m, n: (jnp.maximum(n - 1, 0), V_COL0 + m))
    v_cur = pl.BlockSpec((BLOCK, SLAB), lambda m, n: (n, V_COL0 + m))
    bias_spec = pl.BlockSpec((HEADS_PER_STEP, BLOCK, BLOCK), lambda m, n: (m, 0, 0))
    sink_spec = pl.BlockSpec(memory_space=pltpu.SMEM)
    wide = pl.BlockSpec((BLOCK, Q_LANES), lambda m, n: (n, m))
    gate = pl.BlockSpec((BLOCK, Q_LANES), lambda m, n: (n, ZA_COL0 + m))
    pair = pl.BlockSpec((BLOCK, SLAB), lambda m, n: (n, m))
    return [sink_spec, q_spec, k_prev, k_cur, v_prev, v_cur, bias_spec, gate], wide, pair


def _low_lanes(shape):
    return lax.broadcasted_iota(jnp.int32, shape, 1) < HEAD_DIM


def _pair_halves(ref):
    kb = ref[...]
    sw = pltpu.roll(kb, HEAD_DIM, 1)
    lo = _low_lanes(kb.shape)
    zero = jnp.zeros_like(kb)
    first = (jnp.where(lo, kb, zero).astype(MXU_DTYPE), jnp.where(lo, zero, sw).astype(MXU_DTYPE))
    second = (jnp.where(lo, sw, zero).astype(MXU_DTYPE), jnp.where(lo, zero, kb).astype(MXU_DTYPE))
    return first, second


def _fold_pair(acc):
    f = [x + pltpu.roll(x, HEAD_DIM, 1) for x in acc]
    return jnp.where(_low_lanes(f[0].shape), f[0], f[1])


def _from_prev(n):
    row = lax.broadcasted_iota(jnp.int32, (BLOCK, BLOCK), 0)
    col = lax.broadcasted_iota(jnp.int32, (BLOCK, BLOCK), 1)
    prev = col > row
    return prev, jnp.where(jnp.logical_and(n == 0, prev), NEG_INF, 0.0)


def _softmax_sink(s, sink):
    m = jnp.maximum(jnp.max(s, axis=1, keepdims=True), sink)
    e = jnp.exp(s - m)
    es = jnp.exp(sink - m)
    inv = 1.0 / (jnp.sum(e, axis=1, keepdims=True) + es)
    return e * inv, es * inv


def _stack_pair(prev_halves, own_halves, a):
    return jnp.concatenate([prev_halves[a][0], prev_halves[a][1], own_halves[a][0], own_halves[a][1]], axis=0)


def _split_heads(x4, prev):
    return [jnp.where(prev, x4[:, e * BLOCK:(e + 1) * BLOCK], x4[:, (2 + e) * BLOCK:(3 + e) * BLOCK]) for e in range(2)]


def _spread_heads(x, prev):
    return jnp.concatenate([jnp.where(prev, x[0], 0.0), jnp.where(prev, x[1], 0.0),
                            jnp.where(prev, 0.0, x[0]), jnp.where(prev, 0.0, x[1])], axis=1)


def _attn_forward(proj, bias, sinks):
    t = proj.shape[0]
    in_specs, wide, _ = _attn_specs()

    def body(sink_ref, q_ref, kp_ref, kc_ref, vp_ref, vc_ref, bias_ref, z_ref, o_ref, h_ref):
        m, n = pl.program_id(0), pl.program_id(1)
        kp, kc, vp, vc = (_pair_halves(r) for r in (kp_ref, kc_ref, vp_ref, vc_ref))
        keys = [_stack_pair(kp, kc, a) for a in range(KV_PAIR)]
        vals = [_stack_pair(vp, vc, a) for a in range(KV_PAIR)]
        prev, edge = _from_prev(n)
        prev2 = jnp.concatenate([prev, prev], axis=0)
        for a in range(KV_PAIR):
            slabs = (2 * a, 2 * a + 1)
            q = jnp.concatenate([q_ref[:, s * SLAB:(s + 1) * SLAB] for s in slabs], axis=0).astype(MXU_DTYPE)
            logits = _split_heads(_dot_nt(q, keys[a]), prev2)
            probs = []
            for e in range(2):
                rows = [_softmax_sink(logits[e][r * BLOCK:(r + 1) * BLOCK] * ATTN_SCALE + bias_ref[2 * s + e] + edge,
                                      sink_ref[m * HEADS_PER_STEP + 2 * s + e])[0] for r, s in enumerate(slabs)]
                probs.append(jnp.concatenate(rows, axis=0))
            out = _dot(_spread_heads(probs, prev2), vals[a])
            for r, s in enumerate(slabs):
                cols = slice(s * SLAB, (s + 1) * SLAB)
                o_ref[:, cols] = out[r * BLOCK:(r + 1) * BLOCK]
                h_ref[:, cols] = (out[r * BLOCK:(r + 1) * BLOCK] * _silu(z_ref[:, cols])).astype(h_ref.dtype)

    return pl.pallas_call(
        body, name="attn_fwd", grid=(N_KV_HEADS // KV_PAIR, t // BLOCK),
        in_specs=in_specs, out_specs=[wide, wide],
        out_shape=[jax.ShapeDtypeStruct((t, D_ATTN), F32), jax.ShapeDtypeStruct((t, D_ATTN), MXU_DTYPE)],
        compiler_params=pltpu.CompilerParams(dimension_semantics=("parallel", "arbitrary")),
    )(sinks, proj, proj, proj, proj, proj, bias, proj)


def _attn_backward(proj, bias, sinks, attn, dh):
    t = proj.shape[0]
    in_specs, wide, pair = _attn_specs()
    bias_spec = in_specs[-2]
    sink_out = pl.BlockSpec((HEADS_PER_STEP, 8, 128), lambda m, n: (m, 0, 0))

    def body(sink_ref, q_ref, kp_ref, kc_ref, vp_ref, vc_ref, bias_ref, z_ref, o_ref, dh_ref,
             dq_ref, dz_ref, dka_ref, dkb_ref, dva_ref, dvb_ref, dbias_ref, dsink_ref):
        m, n = pl.program_id(0), pl.program_id(1)

        @pl.when(n == 0)
        def _():
            dbias_ref[...] = jnp.zeros_like(dbias_ref)
            dsink_ref[...] = jnp.zeros_like(dsink_ref)

        kp, kc, vp, vc = (_pair_halves(r) for r in (kp_ref, kc_ref, vp_ref, vc_ref))
        keys = [_stack_pair(kp, kc, a) for a in range(KV_PAIR)]
        vals = [_stack_pair(vp, vc, a) for a in range(KV_PAIR)]
        prev, edge = _from_prev(n)
        lo = _low_lanes((BLOCK, SLAB))
        prev2 = jnp.concatenate([prev, prev], axis=0)
        dk = [[None] * KV_PAIR for _ in range(2)]
        dv = [[None] * KV_PAIR for _ in range(2)]
        for a in range(KV_PAIR):
            slabs = (2 * a, 2 * a + 1)
            q = jnp.concatenate([q_ref[:, s * SLAB:(s + 1) * SLAB] for s in slabs], axis=0).astype(MXU_DTYPE)
            gated = []
            for s in slabs:
                cols = slice(s * SLAB, (s + 1) * SLAB)
                dh_s, z_s = dh_ref[:, cols], z_ref[:, cols]
                dz_ref[:, cols] = (dh_s * o_ref[:, cols] * _silu_grad(z_s)).astype(dz_ref.dtype)
                gated.append(dh_s * _silu(z_s))
            do = jnp.concatenate(gated, axis=0).astype(MXU_DTYPE)
            logits = _split_heads(_dot_nt(q, keys[a]), prev2)
            dprobs = _split_heads(_dot_nt(do, vals[a]), prev2)
            probs, dlogits = [], []
            for e in range(2):
                p_rows, ds_rows = [], []
                for r, s in enumerate(slabs):
                    h = 2 * s + e
                    rows = slice(r * BLOCK, (r + 1) * BLOCK)
                    p, ps = _softmax_sink(logits[e][rows] * ATTN_SCALE + bias_ref[h] + edge,
                                          sink_ref[m * HEADS_PER_STEP + h])
                    delta = jnp.sum(p * dprobs[e][rows], axis=1, keepdims=True)
                    ds = p * (dprobs[e][rows] - delta)
                    dbias_ref[h] += ds
                    dsink_ref[h] += jnp.broadcast_to(jnp.sum(-ps * delta, axis=0, keepdims=True), (8, 128))
                    p_rows.append(p)
                    ds_rows.append(ds)
                probs.append(jnp.concatenate(p_rows, axis=0))
                dlogits.append(jnp.concatenate(ds_rows, axis=0))
            ds4 = _spread_heads(dlogits, prev2).astype(MXU_DTYPE)
            p4 = _spread_heads(probs, prev2).astype(MXU_DTYPE)
            dq = _dot(ds4, keys[a]) * ATTN_SCALE
            for r, s in enumerate(slabs):
                dq_ref[:, s * SLAB:(s + 1) * SLAB] = dq[r * BLOCK:(r + 1) * BLOCK].astype(dq_ref.dtype)
            rk = _dot_tn(ds4, q)
            rv = _dot_tn(p4, do)
            for which in range(2):
                top = 2 * which * BLOCK
                dk[which][a] = jnp.where(lo, rk[top:top + BLOCK], rk[top + BLOCK:top + 2 * BLOCK])
                dv[which][a] = jnp.where(lo, rv[top:top + BLOCK], rv[top + BLOCK:top + 2 * BLOCK])
        dkb_ref[...] = _fold_pair(dk[0]) * ATTN_SCALE
        dka_ref[...] = _fold_pair(dk[1]) * ATTN_SCALE
        dvb_ref[...] = _fold_pair(dv[0])
        dva_ref[...] = _fold_pair(dv[1])

    kv_shape = jax.ShapeDtypeStruct((t, D_KV), F32)
    return pl.pallas_call(
        body, name="attn_bwd", grid=(N_KV_HEADS // KV_PAIR, t // BLOCK),
        in_specs=in_specs + [wide, wide],
        out_specs=[wide, wide, pair, pair, pair, pair, bias_spec, sink_out],
        out_shape=[jax.ShapeDtypeStruct((t, D_ATTN), MXU_DTYPE), jax.ShapeDtypeStruct((t, D_ATTN), MXU_DTYPE),
                   kv_shape, kv_shape, kv_shape, kv_shape,
                   jax.ShapeDtypeStruct(bias.shape, F32), jax.ShapeDtypeStruct((N_Q_HEADS, 8, 128), F32)],
        compiler_params=pltpu.CompilerParams(dimension_semantics=("parallel", "arbitrary")),
    )(sinks, proj, proj, proj, proj, proj, bias, proj, attn, dh)


def _shift_blocks(cur, prev):
    return cur + jnp.concatenate([prev[BLOCK:], jnp.zeros_like(prev[:BLOCK])], axis=0)


def _mesh_pos():
    return lax.axis_index("x"), lax.axis_index("y"), lax.axis_index("c")


_HBM = pl.BlockSpec(memory_space=pltpu.HBM)
_SEM = pl.BlockSpec(memory_space=pltpu.SEMAPHORE)
_DATAFLOW = pltpu.SideEffectType.DATAFLOW_SIDE_EFFECTING


def _peers():
    x, y, c = _mesh_pos()
    others = []
    for k in range(1, N_DEV):
        px, py, pc = x ^ (k >> 2), y ^ ((k >> 1) & 1), c ^ (k & 1)
        others.append(((px, py, pc), 4 * px + 2 * py + pc))
    return 4 * x + 2 * y + c, others


def _split_start(bufs, plan, n_copies, *, name):
    nb = len(bufs)

    def body(*refs):
        send_sems, recv_sems, token = refs[nb], refs[nb + 1], refs[-1]
        for k, (src, dst, pos, _) in enumerate(plan(*refs[:nb])):
            pltpu.make_async_remote_copy(src_ref=src, dst_ref=dst, send_sem=send_sems.at[k], recv_sem=recv_sems.at[k],
                                         device_id=pos, device_id_type=pl.DeviceIdType.MESH).start()
        token[...] = jnp.zeros_like(token)

    return pl.pallas_call(
        body, name=name,
        out_shape=(pltpu.SemaphoreType.DMA((n_copies,)), pltpu.SemaphoreType.DMA((n_copies,)),
                   *[pltpu.HBM(b.shape, b.dtype) for b in bufs], jax.ShapeDtypeStruct((8, 128), F32)),
        in_specs=(_HBM,) * nb, out_specs=(_SEM, _SEM) + (_HBM,) * nb + (pl.BlockSpec(memory_space=pltpu.VMEM),),
        input_output_aliases={i: 2 + i for i in range(nb)},
        compiler_params=pltpu.CompilerParams(has_side_effects=_DATAFLOW),
    )(*[pltpu.with_memory_space_constraint(b, pltpu.HBM) for b in bufs])


def _split_wait(started, plan, after, *, name):
    send_sems, recv_sems, *thru = started[:-1]
    nb = len(thru)

    def body(*refs):
        send_sems, recv_sems = refs[nb], refs[nb + 1]
        for k, (src, _, pos, arrive) in enumerate(plan(*refs[:nb])):
            copy = pltpu.make_async_remote_copy(
                src_ref=src, dst_ref=arrive, send_sem=send_sems.at[k], recv_sem=recv_sems.at[k],
                device_id=pos, device_id_type=pl.DeviceIdType.MESH)
            copy.wait_send()
            copy.wait_recv()

    return pl.pallas_call(
        body, name=name,
        out_shape=tuple(pltpu.HBM(b.shape, b.dtype) for b in thru),
        in_specs=(_HBM,) * nb + (_SEM, _SEM, pl.BlockSpec(memory_space=pl.ANY)), out_specs=(_HBM,) * nb,
        input_output_aliases={i: i for i in range(nb)},
        compiler_params=pltpu.CompilerParams(has_side_effects=_DATAFLOW),
    )(*thru, send_sems, recv_sems, after)


def _plan_scatter(x_ref, land_ref):
    me, others = _peers()
    return [(x_ref.at[idx], land_ref.at[me], pos, land_ref.at[idx]) for pos, idx in others]


def _plan_gather(x_ref, land_ref):
    me, others = _peers()
    return [(x_ref, land_ref.at[me], pos, land_ref.at[idx]) for pos, idx in others]


def _near_and_far():
    x, y, c = _mesh_pos()
    chips = [(1 - x, y), (x, 1 - y), (1 - x, 1 - y)]
    near = [(x, y, 1 - c)] + [(px, py, c) for px, py in chips]
    relay = [(4 * px + 2 * py + c, 4 * px + 2 * py + 1 - c) for px, py in chips]
    return 4 * x + 2 * y + c, near, (x, y, 1 - c), relay


def _plan_gather_near(x_ref, land_ref):
    me, near, _, _ = _near_and_far()
    return [(x_ref, land_ref.at[me], pos, land_ref.at[4 * pos[0] + 2 * pos[1] + pos[2]]) for pos in near]


def _plan_gather_relay(land_ref):
    _, _, sibling, relay = _near_and_far()
    return [(land_ref.at[mine], land_ref.at[mine], sibling, land_ref.at[theirs]) for mine, theirs in relay]


def _landing_zone(own):
    me, _ = _peers()
    return lax.dynamic_update_index_in_dim(lax.empty((N_DEV,) + own.shape, own.dtype), own, me, 0)


def _scatter_start(x, *, name):
    if x.ndim == 2:
        return _split_start((x, _landing_zone(x)), _plan_gather, N_DEV - 1, name=name)
    me, _ = _peers()
    own = lax.dynamic_index_in_dim(x, me, 0, keepdims=False)
    return _split_start((x, _landing_zone(own)), _plan_scatter, N_DEV - 1, name=name)


def _scatter_wait(started, after, *, name):
    plan = _plan_gather if started[2].ndim == 2 else _plan_scatter
    return _split_wait(started, plan, after, name=name)[1]


def _adamw_math(w, g, m, v):
    m = ADAM_B1 * m + (1.0 - ADAM_B1) * g
    v = ADAM_B2 * v + (1.0 - ADAM_B2) * (g * g)
    m_hat = m / (1.0 - ADAM_B1 ** ADAM_STEP)
    v_hat = v / (1.0 - ADAM_B2 ** ADAM_STEP)
    delta = -ADAM_LR * (m_hat / (jnp.sqrt(v_hat) + ADAM_EPS) + ADAM_WD * w)
    return delta, m, v


def _adamw_reduce(parts, w, m, v, *, name, tr):
    r, c = w.shape
    tr = min(tr, r)
    spec = pl.BlockSpec((tr, c), lambda i: (i, 0))

    def body(p_ref, w_ref, m_ref, v_ref, g_ref, d_ref, nm_ref, nv_ref):
        g = p_ref[0].astype(F32)
        for s in range(1, N_DEV):
            g = g + p_ref[s].astype(F32)
        delta, nm, nv = _adamw_math(w_ref[...], g, m_ref[...], v_ref[...])
        g_ref[...] = g
        d_ref[...] = delta
        nm_ref[...] = nm
        nv_ref[...] = nv

    return pl.pallas_call(
        body, name=name, grid=(r // tr,),
        in_specs=[pl.BlockSpec((N_DEV, tr, c), lambda i: (0, i, 0)), spec, spec, spec],
        out_specs=[spec] * 4,
        out_shape=[jax.ShapeDtypeStruct((r, c), F32)] * 4,
        compiler_params=pltpu.CompilerParams(
            dimension_semantics=("parallel",),
            vmem_limit_bytes=_vmem_limit(2 * 15 * _nbytes((tr, c), F32))),
    )(parts, w, m, v)


def _sum_parts(parts, *, name):
    def body(p_ref, o_ref):
        g = p_ref[0]
        for s in range(1, N_DEV):
            g = g + p_ref[s]
        o_ref[...] = g

    return pl.pallas_call(
        body, name=name, out_shape=jax.ShapeDtypeStruct(parts.shape[1:], F32),
        compiler_params=pltpu.CompilerParams(vmem_limit_bytes=_vmem_limit(_nbytes(parts.shape, F32))),
    )(parts)


def _adamw_native(w, g, m, v, *, name):
    def body(w_ref, g_ref, m_ref, v_ref, d_ref, nm_ref, nv_ref):
        d_ref[...], nm_ref[...], nv_ref[...] = _adamw_math(w_ref[...], g_ref[...], m_ref[...], v_ref[...])

    return pl.pallas_call(body, name=name, out_shape=[jax.ShapeDtypeStruct(w.shape, F32)] * 3)(w, g, m, v)


SMALL = ["ssm_lambda_re", "ssm_lambda_im", "ssm_b_re", "ssm_b_im", "ssm_c_re", "ssm_c_im", "ssm_d",
         "ssm_log_step", "attn_sinks", "rel_bias_table", "ln_gain", "ln_bias"]


def _pack(arrs):
    flat = jnp.concatenate([a.reshape(-1) for a in arrs])
    pad = (-flat.shape[0]) % 1024
    return jnp.pad(flat, (0, pad)).reshape(-1, 128)


def _unpack(packed, like):
    flat = packed.reshape(-1)
    out, pos = [], 0
    for a in like:
        out.append(flat[pos:pos + a.size].reshape(a.shape))
        pos += a.size
    return out


def kernel(x, w_in, ssm_lambda_re, ssm_lambda_im, ssm_b_re, ssm_b_im, ssm_c_re, ssm_c_im, ssm_d, ssm_log_step, w_glu, attn_sinks, rel_bias_table, w_branch_ssm, w_branch_attn, w_out, ln_gain, ln_bias, loss_target, m_w_in, m_ssm_lambda_re, m_ssm_lambda_im, m_ssm_b_re, m_ssm_b_im, m_ssm_c_re, m_ssm_c_im, m_ssm_d, m_ssm_log_step, m_w_glu, m_attn_sinks, m_rel_bias_table, m_w_branch_ssm, m_w_branch_attn, m_w_out, m_ln_gain, m_ln_bias, v_w_in, v_ssm_lambda_re, v_ssm_lambda_im, v_ssm_b_re, v_ssm_b_im, v_ssm_c_re, v_ssm_c_im, v_ssm_d, v_ssm_log_step, v_w_glu, v_attn_sinks, v_rel_bias_table, v_w_branch_ssm, v_w_branch_attn, v_w_out, v_ln_gain, v_ln_bias):
    t = x.shape[1]
    xs = x[0]
    target = loss_target[0]
    col_in = w_in.shape[2]
    col_br = w_glu.shape[2]
    row_out = w_out.shape[1]

    w_in_mx = w_in[0].astype(MXU_DTYPE)
    near = _split_start((w_in_mx, _landing_zone(w_in_mx)), _plan_gather_near, 4, name="gather_w_in_near_start")
    ssm_params = (ssm_lambda_re[0], ssm_lambda_im[0], ssm_b_re[0], ssm_b_im[0], ssm_c_re[0], ssm_c_im[0],
                  ssm_log_step[0] + near[-1][0, 0])
    mats, mats_vjp = jax.vjp(_ssm_matrices, *ssm_params)
    mats_mx = _ssm_expand(*mats[:3]) + (mats[3],)
    sinks = attn_sinks[0]
    d_skip = ssm_d
    _, landed = _split_wait(near, _plan_gather_near, mats_mx[1], name="gather_w_in_near_wait")
    relay = _split_start((landed,), _plan_gather_relay, 3, name="gather_w_in_relay_start")
    bias, bias_vjp = jax.vjp(_band_bias, rel_bias_table + relay[-1][0, 0])
    x_mx = xs.astype(MXU_DTYPE)
    (g_in,) = _split_wait(relay, _plan_gather_relay, bias, name="gather_w_in_relay_wait")
    win = g_in.transpose(1, 0, 2).reshape(D_MODEL, D_IN)
    hold = g_in[0, 0, 0] * 0
    three = jnp.concatenate([w_glu[0], w_branch_ssm[0], w_branch_attn[0]], axis=0).astype(MXU_DTYPE)
    sent_three = _scatter_start(three + hold, name="gather_w_1024_start")
    sent_wout = _scatter_start(w_out[0].astype(MXU_DTYPE) + hold, name="gather_w_out_start")

    proj = _mm(x_mx, win, name="in_proj", tm=2048, tn=512, tk=2048, after=(sent_three[4], sent_wout[4]))
    y_conv, states = _ssm_forward(proj, mats_mx)

    def f_gelu(yv, u, d):
        ys = yv + d * u
        return ys, _gelu(ys)

    y_ssm, glu_in = _ew(f_gelu, [(y_conv, "row", 0), (proj, "row", OFF_U), (d_skip, "vec", 0)],
                        [(COL, F32, "row"), (COL, MXU_DTYPE, "row")], rows=t, cw=COL, ncb=4, tr=1024, name="ssm_gelu")
    g_three = _scatter_wait(sent_three, glu_in, name="gather_w_1024_wait")
    three_full = g_three.transpose(1, 0, 2).reshape(3 * D_SSM, N_DEV * col_br)
    wglu, wbs, wba = three_full[:D_SSM], three_full[D_SSM:2 * D_SSM], three_full[2 * D_SSM:]
    gate_tn = COL
    z_ssm_col = [(proj, OFF_ZS * COL // gate_tn)]

    def f_hssm(products, gate):
        (ga, gb), (z,) = products, gate
        return ga, gb, ga * _sigmoid(gb) * _silu(z)

    glu_a, glu_b, h_ssm = _mm_fused([(glu_in, wglu[:, :D_SSM], False), (glu_in, wglu[:, D_SSM:], False)], z_ssm_col,
                                    [F32, F32, MXU_DTYPE], f_hssm, name="glu_gate", tm=2048, tn=gate_tn)

    attn, h_attn = _attn_forward(proj, bias, sinks)
    gate_cols = [(proj, OFF_GS * COL // gate_tn), (proj, OFF_GA * COL // gate_tn)]

    def f_merge(products, gates):
        (ps, pa), (ls, la) = products, gates
        return ps, pa, _sigmoid(ls) * ps + _sigmoid(la) * pa

    p_ssm, p_attn, merged = _mm_fused([(h_ssm, wbs, False), (h_attn, wba, False)], gate_cols, [F32, F32, MXU_DTYPE],
                                      f_merge, name="branch_merge", tm=2048, tn=gate_tn)
    wout = _scatter_wait(sent_wout, merged, name="gather_w_out_wait").reshape(D_MODEL, D_MODEL)
    out = _mm(merged, wout, name="out_proj", tm=2048, tn=512, tk=2048)

    def f_norm(xv, ov, tg, gain, lbias):
        r = DEEPNORM_ALPHA * xv + ov
        mu = jnp.mean(r, axis=1, keepdims=True)
        cen = r - mu
        var = jnp.mean(cen * cen, axis=1, keepdims=True)
        rstd = lax.rsqrt(var + LN_EPS)
        xhat = cen * rstd
        yv = xhat * gain + lbias
        diff = yv - tg
        row_loss = 0.5 * jnp.mean(diff * diff, axis=1, keepdims=True)
        loss = jnp.broadcast_to(jnp.sum(row_loss, axis=0, keepdims=True), (1, 128))
        dy = diff * (1.0 / D_MODEL)
        dgain = jnp.sum(dy * xhat, axis=0, keepdims=True)
        dbias = jnp.sum(dy, axis=0, keepdims=True)
        dxh = dy * gain
        dr = rstd * (dxh - jnp.mean(dxh, axis=1, keepdims=True) - xhat * jnp.mean(dxh * xhat, axis=1, keepdims=True))
        return dr, dr, loss, dgain, dbias

    dr, dr_mx, loss_part, g_ln_gain, g_ln_bias = _ew(
        f_norm, [(xs, "row", 0), (out, "row", 0), (target, "row", 0), (ln_gain, "vec", 0), (ln_bias, "vec", 0)],
        [(D_MODEL, F32, "row"), (D_MODEL, MXU_DTYPE, "row"), (128, F32, "acc"), (D_MODEL, F32, "acc"),
         (D_MODEL, F32, "acc")],
        rows=t, cw=D_MODEL, ncb=1, tr=256, name="norm_loss")

    def scatter_cols(g, cols):
        return g.reshape(g.shape[0], N_DEV, cols).transpose(1, 0, 2)

    gw_out = _mm(merged, dr_mx, ta=True, out_dtype=WIRE_DTYPE, name="grad_w_out", tm=2048, tn=512, tk=2048)
    sent_out = _scatter_start(gw_out.reshape(N_DEV, row_out, D_MODEL), name="scatter_g_out_start")
    def b_merge(products, tiles):
        (dm,), (ps, pa, ls, la) = products, tiles
        gs, ga = _sigmoid(ls), _sigmoid(la)
        return dm * gs, dm * ga, dm * ps * gs * (1.0 - gs), dm * pa * ga * (1.0 - ga)

    dp_ssm, dp_attn, dgl_s, dgl_a = _mm_fused(
        [(dr_mx, wout, True)], [(p_ssm, 0), (p_attn, 0), (proj, OFF_GS), (proj, OFF_GA)], [MXU_DTYPE] * 4, b_merge,
        name="merge_bwd", tm=1024, tn=COL, after=(sent_out[4],))
    gw_bs = _mm(h_ssm, dp_ssm, ta=True, out_dtype=WIRE_DTYPE, name="grad_w_branch_ssm", tm=1024, tn=512, tk=2048)
    gw_ba = _mm(h_attn, dp_attn, ta=True, out_dtype=WIRE_DTYPE, name="grad_w_branch_attn", tm=1024, tn=512, tk=2048)
    dh_attn = _mm(dp_attn, wba, tb=True, name="d_h_attn", tm=2048, tn=512, tk=2048)

    def b_hssm(products, tiles):
        (dh,), (ga, gb, z) = products, tiles
        sg = _sigmoid(gb)
        dgate = dh * _silu(z)
        return dgate * sg, dgate * ga * sg * (1.0 - sg), dh * ga * sg * _silu_grad(z)

    dglu_a, dglu_b, dz_ssm = _mm_fused([(dp_ssm, wbs, True)], [(glu_a, 0), (glu_b, 0)] + z_ssm_col, [MXU_DTYPE] * 3,
                                       b_hssm, name="glu_gate_bwd", tm=2048, tn=gate_tn)
    dglu = jnp.concatenate([dglu_a, dglu_b], axis=1)
    gw_glu = _mm(glu_in, dglu, ta=True, out_dtype=WIRE_DTYPE, name="grad_w_glu", tm=1024, tn=512, tk=2048)
    sent_three = _scatter_start(scatter_cols(jnp.concatenate([gw_glu, gw_bs, gw_ba], axis=0), col_br),
                                name="scatter_g_1024_start")
    def b_gelu(products, tiles):
        return (products[0] * _gelu_grad(tiles[0]),)

    (dy_ssm,) = _mm_fused([(dglu, wglu, True)], [(y_ssm, 0)], [F32], b_gelu, name="gelu_bwd", tm=2048, tn=gate_tn,
                          after=(sent_three[4],))
    du_ssm, g_ssm_d, dmats = _ssm_backward(dy_ssm, proj, states, mats_mx)
    du = (du_ssm + d_skip * dy_ssm).astype(MXU_DTYPE)
    g_lre, g_lim, g_bre, g_bim, g_cre, g_cim, g_lstep = mats_vjp(dmats)

    dq, dz_attn, dka, dkb, dva, dvb, dbias, dsink = _attn_backward(proj, bias, sinks, attn, dh_attn)
    dk = _shift_blocks(dka, dkb).astype(MXU_DTYPE)
    dv = _shift_blocks(dva, dvb).astype(MXU_DTYPE)
    (g_table,) = bias_vjp(dbias)
    g_sinks = dsink[:, 0, 0]

    dproj = jnp.concatenate([du, dz_ssm, dq, dk, dv, dz_attn, dgl_s, dgl_a], axis=1)
    small_g = [g_lre, g_lim, g_bre, g_bim, g_cre, g_cim, g_ssm_d, g_lstep, g_sinks, g_table, g_ln_gain, g_ln_bias]
    sent_small = _scatter_start(_pack(small_g), name="gather_g_small_start")
    gw_in = _mm(x_mx, dproj, ta=True, out_dtype=WIRE_DTYPE, name="grad_w_in", tm=2048, tn=512, tk=2048,
                after=(sent_small[4],))
    sent_in = _scatter_start(scatter_cols(gw_in, col_in), name="scatter_g_in_start")
    grad_x = _mm(dproj, win, tb=True, add=dr, add_scale=DEEPNORM_ALPHA, name="grad_x", tm=1024, tn=512, tk=4352,
                 after=(sent_in[4],))

    parts_out = _scatter_wait(sent_out, grad_x, name="scatter_g_out_wait")
    parts_three = _scatter_wait(sent_three, parts_out, name="scatter_g_1024_wait")
    parts_in = _scatter_wait(sent_in, parts_three, name="scatter_g_in_wait")

    o_in = _adamw_reduce(parts_in, w_in[0], m_w_in[0], v_w_in[0], name="adamw_w_in", tr=128)
    three_w = jnp.concatenate([w_glu[0], w_branch_ssm[0], w_branch_attn[0]], axis=0)
    three_m = jnp.concatenate([m_w_glu[0], m_w_branch_ssm[0], m_w_branch_attn[0]], axis=0)
    three_v = jnp.concatenate([v_w_glu[0], v_w_branch_ssm[0], v_w_branch_attn[0]], axis=0)
    o_three = _adamw_reduce(parts_three, three_w, three_m, three_v, name="adamw_w_1024", tr=512)
    o_out = _adamw_reduce(parts_out, w_out[0], m_w_out[0], v_w_out[0], name="adamw_w_out", tr=128)

    small_w = [ssm_lambda_re, ssm_lambda_im, ssm_b_re, ssm_b_im, ssm_c_re, ssm_c_im, ssm_d, ssm_log_step,
               attn_sinks, rel_bias_table, ln_gain, ln_bias]
    small_m = [m_ssm_lambda_re, m_ssm_lambda_im, m_ssm_b_re, m_ssm_b_im, m_ssm_c_re, m_ssm_c_im, m_ssm_d,
               m_ssm_log_step, m_attn_sinks, m_rel_bias_table, m_ln_gain, m_ln_bias]
    small_v = [v_ssm_lambda_re, v_ssm_lambda_im, v_ssm_b_re, v_ssm_b_im, v_ssm_c_re, v_ssm_c_im, v_ssm_d,
               v_ssm_log_step, v_attn_sinks, v_rel_bias_table, v_ln_gain, v_ln_bias]
    parts_small = _scatter_wait(sent_small, parts_in, name="gather_g_small_wait")
    sg = _unpack(_sum_parts(parts_small, name="sum_g_small"), small_w)
    updates = [_adamw_native(w, g, m, v, name="adamw_" + n)
               for n, w, g, m, v in zip(SMALL, small_w, sg, small_m, small_v)]
    sd, sm, sv = zip(*updates)

    loss = lax.psum(loss_part[0, 0], MESH_AXES)

    def big(o, idx):
        g_in_, g_three_, g_out_ = o_in[idx], o_three[idx], o_out[idx]
        return {"w_in": g_in_[None], "w_glu": g_three_[None, :D_SSM], "w_branch_ssm": g_three_[None, D_SSM:2 * D_SSM],
                "w_branch_attn": g_three_[None, 2 * D_SSM:], "w_out": g_out_[None]}

    order = ["w_in", "ssm_lambda_re", "ssm_lambda_im", "ssm_b_re", "ssm_b_im", "ssm_c_re", "ssm_c_im", "ssm_d",
             "ssm_log_step", "w_glu", "attn_sinks", "rel_bias_table", "w_branch_ssm", "w_branch_attn", "w_out",
             "ln_gain", "ln_bias"]
    outs = [loss, grad_x[None]]
    for idx, small in enumerate([sg, sd, sm, sv]):
        table = big(None, idx)
        table.update(dict(zip(SMALL, small)))
        outs += [table[n] for n in order]
    return tuple(outs)
```
